```python
import math
import jax, jax.numpy as jnp
from jax import lax
import numpy as np

D_MODEL = 1024
BATCH = 8
SEQ = 2048
DEPTH = 1
DEC_BATCH = 128
DEC_SEQ = 1
PAST_LEN = 16384
PAGE_SIZE = 128

MIX_WIDTH = D_MODEL
POOL_WIDTH = MIX_WIDTH // 2
POOL_WINDOWS = (2, 4, 8, 16)
POOL_GROUPS = len(POOL_WINDOWS)
POOL_GROUP_DIM = POOL_WIDTH // POOL_GROUPS
POOL_BUF = max(POOL_WINDOWS) - 1
SSD_WIDTH = MIX_WIDTH - POOL_WIDTH
SSD_HEAD_DIM = 64
SSD_HEADS = SSD_WIDTH // SSD_HEAD_DIM
SSD_GROUPS = 2
SSD_STATE = 64
SSD_CONV = 4
SSD_CHUNK = 128
SSD_BC = SSD_GROUPS * SSD_STATE
CONV_DIM = SSD_WIDTH + 2 * SSD_BC
IN_DIM = POOL_WIDTH + SSD_WIDTH + CONV_DIM + SSD_HEADS
N_EXPERT_GROUPS = 4
EXPERTS_PER_GROUP = 8
EXPERT_TOP_K = 2
EXPERT_FF = 256
PLE_DIM = 256
EPS = 1e-6

kernel_name = 'hybrid_pool_ssd_hmoe_step'


def _rmsnorm(x, g):
    xf = x.astype(jnp.float32)
    y = xf * lax.rsqrt(jnp.mean(xf * xf, axis=-1, keepdims=True) + EPS) * g.astype(jnp.float32)
    return y.astype(x.dtype)


def _pool_mixer(v, buf, pos0, lin_w, lin_b, scale):
    n, L, c = v.shape
    ext = jnp.concatenate([buf.astype(v.dtype), v], axis=1)
    extf = ext.astype(jnp.float32)
    cs = jnp.concatenate([jnp.zeros((n, 1, c), jnp.float32), jnp.cumsum(extf, axis=1)], axis=1)
    pos = jnp.arange(L, dtype=jnp.int32) + pos0
    end = cs[:, POOL_BUF + 1:POOL_BUF + 1 + L]
    tok = extf[:, POOL_BUF:]
    groups = []
    for g, w in enumerate(POOL_WINDOWS):
        sl = slice(g * POOL_GROUP_DIM, (g + 1) * POOL_GROUP_DIM)
        start = cs[:, POOL_BUF + 1 - w:POOL_BUF + 1 - w + L, sl]
        cnt = jnp.minimum(pos + 1, w).astype(jnp.float32)[None, :, None]
        groups.append((end[..., sl] - start) / cnt - tok[..., sl])
    m = jnp.stack(groups, axis=2)
    y = jnp.einsum('blgc,gcd->blgd', m, lin_w.astype(jnp.float32)) + lin_b.astype(jnp.float32)
    y = y.reshape(n, L, POOL_WIDTH) * scale.astype(jnp.float32)
    return y.astype(v.dtype), ext[:, -POOL_BUF:]


def _ssd_chunked(x, dt, a, bm, cm, h0, chunk):
    n, L, H, P = x.shape
    nc = L // chunk
    hpg = H // bm.shape[2]
    bh = jnp.repeat(bm, hpg, axis=2).reshape(n, nc, chunk, H, -1)
    ch = jnp.repeat(cm, hpg, axis=2).reshape(n, nc, chunk, H, -1)
    xdt = (x * dt[..., None]).reshape(n, nc, chunk, H, P)
    acs = jnp.cumsum((dt * a).reshape(n, nc, chunk, H), axis=2)
    idx = jnp.arange(chunk)
    causal = (idx[:, None] >= idx[None, :])[:, :, None]
    seg = acs[:, :, :, None, :] - acs[:, :, None, :, :]
    decay = jnp.exp(jnp.where(causal, seg, -jnp.inf))
    scores = jnp.einsum('bclhn,bcshn->bclsh', ch, bh) * decay
    y_diag = jnp.einsum('bclsh,bcshp->bclhp', scores, xdt)
    to_end = jnp.exp(acs[:, :, -1:, :] - acs)
    chunk_states = jnp.einsum('bclhn,bclh,bclhp->bchpn', bh, to_end, xdt)
    chunk_decay = jnp.exp(acs[:, :, -1, :])

    def step(h, inp):
        s, d = inp
        return h * d[:, :, None, None] + s, h

    h_final, h_in = lax.scan(step, h0, (jnp.moveaxis(chunk_states, 1, 0), jnp.moveaxis(chunk_decay, 1, 0)))
    h_in = jnp.moveaxis(h_in, 0, 1)
    y_off = jnp.einsum('bclhn,bchpn->bclhp', ch, h_in) * jnp.exp(acs)[..., None]
    return (y_diag + y_off).reshape(n, L, H, P), h_final


def _ssd_mixer(z, xbc, dt_raw, conv_buf, h0, conv_w, conv_b, dt_bias, a_log, d_skip, norm_w):
    n, L, _ = xbc.shape
    f32 = jnp.float32
    ext = jnp.concatenate([conv_buf.astype(xbc.dtype), xbc], axis=1)
    extf = ext.astype(f32)
    cw = conv_w.astype(f32)
    conv = conv_b.astype(f32) + extf[:, 0:L] * cw[0]
    for k in range(1, SSD_CONV):
        conv = conv + extf[:, k:k + L] * cw[k]
    conv = jax.nn.silu(conv)
    xs = conv[..., :SSD_WIDTH].reshape(n, L, SSD_HEADS, SSD_HEAD_DIM)
    bm = conv[..., SSD_WIDTH:SSD_WIDTH + SSD_BC].reshape(n, L, SSD_GROUPS, SSD_STATE)
    cm = conv[..., SSD_WIDTH + SSD_BC:].reshape(n, L, SSD_GROUPS, SSD_STATE)
    dt = jax.nn.softplus(dt_raw.astype(f32) + dt_bias.astype(f32))
    a = -jnp.exp(a_log.astype(f32))
    y, h_final = _ssd_chunked(xs, dt, a, bm, cm, h0.astype(f32), math.gcd(L, SSD_CHUNK))
    y = y + d_skip.astype(f32)[:, None] * xs
    y = y.reshape(n, L, SSD_WIDTH) * jax.nn.silu(z.astype(f32))
    yg = y.reshape(n, L, SSD_GROUPS, SSD_WIDTH // SSD_GROUPS)
    yg = yg * lax.rsqrt(jnp.mean(yg * yg, axis=-1, keepdims=True) + EPS)
    y = yg.reshape(n, L, SSD_WIDTH) * norm_w.astype(f32)
    return y.astype(z.dtype), ext[:, -(SSD_CONV - 1):], h_final.astype(h0.dtype)


def _hier_moe(u, rg_w, rg_b, re_w, re_b, w_gate, w_up, w_down):
    f32 = jnp.float32
    n, L, d = u.shape
    t = u.reshape(n * L, d)
    g_prob = jax.nn.softmax((t @ rg_w + rg_b).astype(f32), axis=-1)
    g_w, g_idx = lax.top_k(g_prob, 1)
    e_logit = (t @ re_w + re_b).astype(f32).reshape(-1, N_EXPERT_GROUPS, EXPERTS_PER_GROUP)
    e_sel = jnp.take_along_axis(e_logit, g_idx[:, :, None], axis=1)[:, 0]
    e_w, e_idx = lax.top_k(jax.nn.softmax(e_sel, axis=-1), EXPERT_TOP_K)
    e_w = e_w / jnp.sum(e_w, axis=-1, keepdims=True)
    local_gate = jnp.einsum('tk,tke->te', g_w * e_w, jax.nn.one_hot(e_idx, EXPERTS_PER_GROUP, dtype=f32))
    group_mask = jax.nn.one_hot(g_idx[:, 0], N_EXPERT_GROUPS, dtype=f32)
    out = jnp.zeros((n * L, d), f32)
    for g in range(N_EXPERT_GROUPS):
        gate = local_gate * group_mask[:, g:g + 1]
        hg = jnp.einsum('td,edf->tef', t, w_gate[g]).astype(f32)
        hu = jnp.einsum('td,edf->tef', t, w_up[g]).astype(f32)
        act = jax.nn.silu(hg) * hu * gate[:, :, None]
        out = out + jnp.einsum('tef,efd->td', act, w_down[g].astype(f32))
    return out.reshape(n, L, d).astype(u.dtype)


def _layer(h, p, pool_buf, conv_buf, ssm_h, pos0,
           norm_mix, w_in, pool_lin_w, pool_lin_b, pool_scale, conv_w, conv_b, dt_bias, a_log,
           d_skip, ssd_norm, w_out, norm_ffn, router_grp_w, router_grp_b, router_exp_w,
           router_exp_b, exp_w_gate, exp_w_up, exp_w_down, norm_ple, ple_gate_w, ple_gate_b,
           ple_proj_w):
    a = _rmsnorm(h, norm_mix)
    proj = a @ w_in
    o1 = POOL_WIDTH
    o2 = o1 + SSD_WIDTH
    o3 = o2 + CONV_DIM
    v_pool = proj[..., :o1]
    z = proj[..., o1:o2]
    xbc = proj[..., o2:o3]
    dt_raw = proj[..., o3:]
    pool_out, new_pool = _pool_mixer(v_pool, pool_buf, pos0, pool_lin_w, pool_lin_b, pool_scale)
    ssd_out, new_conv, new_ssm = _ssd_mixer(z, xbc, dt_raw, conv_buf, ssm_h, conv_w, conv_b,
                                            dt_bias, a_log, d_skip, ssd_norm)
    h = h + jnp.concatenate([pool_out, ssd_out], axis=-1) @ w_out
    h = h + _hier_moe(_rmsnorm(h, norm_ffn), router_grp_w, router_grp_b, router_exp_w,
                      router_exp_b, exp_w_gate, exp_w_up, exp_w_down)
    gate = jax.nn.sigmoid((_rmsnorm(h, norm_ple) @ ple_gate_w + ple_gate_b).astype(jnp.float32))
    h = h + (gate * (p @ ple_proj_w).astype(jnp.float32)).astype(h.dtype)
    return h, new_pool, new_conv, new_ssm


def _trunk(x, p, pool_st, conv_st, ssm_st, pos0, layer_weights, norm_final):
    h = x
    pools, convs, ssms = [], [], []
    for i in range(DEPTH):
        lw = [w[i] for w in layer_weights]
        h, ps, cs, ss = _layer(h, p[i], pool_st[i], conv_st[i], ssm_st[i], pos0, *lw)
        pools.append(ps)
        convs.append(cs)
        ssms.append(ss)
    return _rmsnorm(h, norm_final), jnp.stack(pools), jnp.stack(convs), jnp.stack(ssms)


def setup_inputs(seed: int = 0) -> dict:
    key = jax.random.key(seed)
    ks = iter(jax.random.split(key, 40))
    f32 = jnp.float32

    def nrm(shape, scale):
        return jax.random.normal(next(ks), shape, f32) * scale

    def gain(shape):
        return 1.0 + nrm(shape, 0.05)

    dt0 = jnp.exp(jax.random.uniform(next(ks), (DEPTH, SSD_HEADS), f32, math.log(1e-3), math.log(1e-1)))
    dt_bias = dt0 + jnp.log(-jnp.expm1(-dt0))
    a_log = jnp.log(jax.random.uniform(next(ks), (DEPTH, SSD_HEADS), f32, 1.0, 16.0))
    NE = N_EXPERT_GROUPS * EXPERTS_PER_GROUP
    return {
        'x_prompt': nrm((BATCH, SEQ, D_MODEL), 1.0),
        'x_sample': nrm((DEC_BATCH, DEC_SEQ, D_MODEL), 1.0),
        'p_prompt': nrm((DEPTH, BATCH, SEQ, PLE_DIM), 1.0),
        'p_sample': nrm((DEPTH, DEC_BATCH, DEC_SEQ, PLE_DIM), 1.0),
        'state_pool': nrm((DEPTH, DEC_BATCH, POOL_BUF, POOL_WIDTH), 1.0),
        'state_conv': nrm((DEPTH, DEC_BATCH, SSD_CONV - 1, CONV_DIM), 1.0),
        'state_ssm': nrm((DEPTH, DEC_BATCH, SSD_HEADS, SSD_HEAD_DIM, SSD_STATE), 0.5),
        'norm_mix': gain((DEPTH, D_MODEL)),
        'w_in': nrm((DEPTH, D_MODEL, IN_DIM), D_MODEL ** -0.5),
        'pool_lin_w': nrm((DEPTH, POOL_GROUPS, POOL_GROUP_DIM, POOL_GROUP_DIM), POOL_GROUP_DIM ** -0.5),
        'pool_lin_b': nrm((DEPTH, POOL_GROUPS, POOL_GROUP_DIM), 0.02),
        'pool_scale': gain((DEPTH, POOL_WIDTH)),
        'conv_w': nrm((DEPTH, SSD_CONV, CONV_DIM), SSD_CONV ** -0.5),
        'conv_b': nrm((DEPTH, CONV_DIM), 0.02),
        'dt_bias': dt_bias,
        'a_log': a_log,
        'd_skip': 1.0 + nrm((DEPTH, SSD_HEADS), 0.1),
        'ssd_norm': gain((DEPTH, SSD_WIDTH)),
        'w_out': nrm((DEPTH, MIX_WIDTH, D_MODEL), MIX_WIDTH ** -0.5),
        'norm_ffn': gain((DEPTH, D_MODEL)),
        'router_grp_w': nrm((DEPTH, D_MODEL, N_EXPERT_GROUPS), D_MODEL ** -0.5),
        'router_grp_b': nrm((DEPTH, N_EXPERT_GROUPS), 0.01),
        'router_exp_w': nrm((DEPTH, D_MODEL, NE), D_MODEL ** -0.5),
        'router_exp_b': nrm((DEPTH, NE), 0.01),
        'exp_w_gate': nrm((DEPTH, N_EXPERT_GROUPS, EXPERTS_PER_GROUP, D_MODEL, EXPERT_FF), D_MODEL ** -0.5),
        'exp_w_up': nrm((DEPTH, N_EXPERT_GROUPS, EXPERTS_PER_GROUP, D_MODEL, EXPERT_FF), D_MODEL ** -0.5),
        'exp_w_down': nrm((DEPTH, N_EXPERT_GROUPS, EXPERTS_PER_GROUP, EXPERT_FF, D_MODEL), EXPERT_FF ** -0.5),
        'norm_ple': gain((DEPTH, D_MODEL)),
        'ple_gate_w': nrm((DEPTH, D_MODEL, D_MODEL), D_MODEL ** -0.5),
        'ple_gate_b': nrm((DEPTH, D_MODEL), 0.02),
        'ple_proj_w': nrm((DEPTH, PLE_DIM, D_MODEL), PLE_DIM ** -0.5),
        'norm_final': gain((D_MODEL,)),
    }


def reference(x_prompt, x_sample, p_prompt, p_sample, state_pool, state_conv, state_ssm,
              norm_mix, w_in, pool_lin_w, pool_lin_b, pool_scale, conv_w, conv_b, dt_bias, a_log,
              d_skip, ssd_norm, w_out, norm_ffn, router_grp_w, router_grp_b, router_exp_w,
              router_exp_b, exp_w_gate, exp_w_up, exp_w_down, norm_ple, ple_gate_w, ple_gate_b,
              ple_proj_w, norm_final):
    layer_weights = (norm_mix, w_in, pool_lin_w, pool_lin_b, pool_scale, conv_w, conv_b, dt_bias,
                     a_log, d_skip, ssd_norm, w_out, norm_ffn, router_grp_w, router_grp_b,
                     router_exp_w, router_exp_b, exp_w_gate, exp_w_up, exp_w_down, norm_ple,
                     ple_gate_w, ple_gate_b, ple_proj_w)
    nb = x_prompt.shape[0]
    dtp = x_prompt.dtype
    pool0 = jnp.zeros((DEPTH, nb, POOL_BUF, POOL_WIDTH), dtp)
    conv0 = jnp.zeros((DEPTH, nb, SSD_CONV - 1, CONV_DIM), dtp)
    ssm0 = jnp.zeros((DEPTH, nb, SSD_HEADS, SSD_HEAD_DIM, SSD_STATE), dtp)
    y_prompt, pool_p, conv_p, ssm_p = _trunk(x_prompt, p_prompt, pool0, conv0, ssm0, 0,
                                             layer_weights, norm_final)
    y_sample, pool_s, conv_s, ssm_s = _trunk(x_sample, p_sample, state_pool, state_conv, state_ssm,
                                             PAST_LEN, layer_weights, norm_final)
    return (y_prompt, y_sample, pool_p, conv_p, ssm_p, pool_s, conv_s, ssm_s)
```

```python
import functools

import jax
import jax.numpy as jnp
from jax import lax
from jax.experimental import pallas as pl
from jax.experimental.pallas import tpu as pltpu

F32 = jnp.float32
BF16 = jnp.bfloat16

D_MODEL = 1024
POOL_WIDTH = 512
POOL_WINDOWS = (2, 4, 8, 16)
POOL_GROUP_DIM = 128
POOL_BUF = 15
SSD_WIDTH = 512
SSD_HEAD_DIM = 64
SSD_HEADS = 8
SSD_GROUPS = 2
SSD_STATE = 64
SSD_CONV = 4
SSD_BC = SSD_GROUPS * SSD_STATE
CONV_DIM = SSD_WIDTH + 2 * SSD_BC
N_EXPERT_GROUPS = 4
EXPERTS_PER_GROUP = 8
N_EXPERTS = N_EXPERT_GROUPS * EXPERTS_PER_GROUP
EXPERT_FF = 256
PLE_DIM = 256
PAST_LEN = 16384
EPS = 1e-6

LANES = 128
IN_COLS = 1920
DT_OFF = POOL_WIDTH + SSD_WIDTH + CONV_DIM
VMEM_LIMIT = 56 * 1024 * 1024


def _cparams(*sem):
    return pltpu.CompilerParams(dimension_semantics=sem, vmem_limit_bytes=VMEM_LIMIT)


def _rms(x, g):
    return x * lax.rsqrt(jnp.mean(x * x, axis=-1, keepdims=True) + EPS) * g


def _sigmoid(x):
    return 1.0 / (1.0 + jnp.exp(-x))


def _silu(x):
    return x * _sigmoid(x)


def _split3(v):
    hi = v.astype(BF16)
    r = v - hi.astype(F32)
    mid = r.astype(BF16)
    lo = (r - mid.astype(F32)).astype(BF16)
    return hi, mid, lo


def _dot(a, b):
    return jnp.dot(a, b, preferred_element_type=F32)


def _dot_nt(a, b):
    return lax.dot_general(a, b, (((1,), (1,)), ((), ())), preferred_element_type=F32)


def _dot_tn(a, b):
    return lax.dot_general(a, b, (((0,), (0,)), ((), ())), preferred_element_type=F32)


def _dot_exact_lhs(sel, v, terms=3):
    acc = None
    for t in _split3(v)[:terms]:
        p = _dot(sel, t)
        acc = p if acc is None else acc + p
    return acc


def _dot_exact_rhs(v, sel, terms=3):
    acc = None
    for t in _split3(v)[:terms]:
        p = _dot(t, sel)
        acc = p if acc is None else acc + p
    return acc


def _in_proj_body(x_ref, g_ref, w_ref, vp_ref, z_ref, xbc_ref, dt_ref):
    a16 = _rms(x_ref[...], g_ref[...]).astype(BF16)
    vp_ref[...] = _dot(a16, w_ref[:, 0:POOL_WIDTH])
    z_ref[...] = _dot(a16, w_ref[:, POOL_WIDTH:POOL_WIDTH + SSD_WIDTH])
    xbc_ref[...] = _dot(a16, w_ref[:, POOL_WIDTH + SSD_WIDTH:DT_OFF])
    dt_ref[...] = _dot(a16, w_ref[:, DT_OFF:IN_COLS])


def _in_proj(x, g, w16, tm):
    t = x.shape[0]
    row = lambda i: (i, 0)
    fixed = lambda i: (0, 0)
    return pl.pallas_call(
        _in_proj_body,
        grid=(t // tm,),
        in_specs=[pl.BlockSpec((tm, D_MODEL), row), pl.BlockSpec((1, D_MODEL), fixed),
                  pl.BlockSpec((D_MODEL, IN_COLS), fixed)],
        out_specs=[pl.BlockSpec((tm, POOL_WIDTH), row), pl.BlockSpec((tm, SSD_WIDTH), row),
                   pl.BlockSpec((tm, CONV_DIM), row), pl.BlockSpec((tm, LANES), row)],
        out_shape=[jax.ShapeDtypeStruct((t, POOL_WIDTH), F32), jax.ShapeDtypeStruct((t, SSD_WIDTH), F32),
                   jax.ShapeDtypeStruct((t, CONV_DIM), F32), jax.ShapeDtypeStruct((t, LANES), F32)],
        compiler_params=_cparams("parallel"),
        name="in_proj",
    )(x, g, w16)


def _route(lg):
    tm = lg.shape[1]
    gl = lg[0:N_EXPERT_GROUPS, :]
    gmax = jnp.max(gl, axis=0, keepdims=True)
    gsum = jnp.sum(jnp.exp(gl - gmax), axis=0, keepdims=True)
    g_w = 1.0 / gsum
    gi = lax.broadcasted_iota(jnp.int32, gl.shape, 0)
    g_idx = jnp.min(jnp.where(gl == gmax, gi, N_EXPERT_GROUPS), axis=0, keepdims=True)
    sel = jnp.zeros((EXPERTS_PER_GROUP, tm), F32)
    for g in range(N_EXPERT_GROUPS):
        blk = lg[8 + g * EXPERTS_PER_GROUP:8 + (g + 1) * EXPERTS_PER_GROUP, :]
        sel = jnp.where(g_idx == g, blk, sel)
    ei = lax.broadcasted_iota(jnp.int32, sel.shape, 0)
    m1 = jnp.max(sel, axis=0, keepdims=True)
    i1 = jnp.min(jnp.where(sel == m1, ei, EXPERTS_PER_GROUP), axis=0, keepdims=True)
    rest = jnp.where(ei == i1, -jnp.inf, sel)
    m2 = jnp.max(rest, axis=0, keepdims=True)
    i2 = jnp.min(jnp.where(rest == m2, ei, EXPERTS_PER_GROUP), axis=0, keepdims=True)
    p2 = jnp.exp(m2 - m1)
    w1 = g_w / (1.0 + p2)
    w2 = g_w * p2 / (1.0 + p2)
    lg8 = jnp.where(ei == i1, w1, 0.0) + jnp.where(ei == i2, w2, 0.0)
    rows = [jnp.where(g_idx == g, lg8, 0.0) for g in range(N_EXPERT_GROUPS)]
    return jnp.concatenate(rows, axis=0)


def _out_proj_body(x_ref, mix_ref, w_ref, g_ref, rh_ref, rl_ref, rb_ref, h_ref, u_ref, gate_ref):
    h = x_ref[...] + _dot(mix_ref[...], w_ref[...])
    h_ref[...] = h
    u = _rms(h, g_ref[...])
    u_hi = u.astype(BF16)
    u_ref[...] = u_hi
    u_lo = (u - u_hi.astype(F32)).astype(BF16)
    lg = _dot_nt(rh_ref[...], u_hi) + _dot_nt(rh_ref[...], u_lo) + _dot_nt(rl_ref[...], u_hi) + rb_ref[...]
    gates = _route(lg)
    tm = gates.shape[1]
    full = jnp.concatenate([gates, jnp.zeros((LANES - N_EXPERTS, tm), F32)], axis=0)
    gate_ref[...] = full.T


def _out_proj(x, mix16, w16, g, r_hi, r_lo, r_bias, tm):
    t = x.shape[0]
    row = lambda i: (i, 0)
    fixed = lambda i: (0, 0)
    return pl.pallas_call(
        _out_proj_body,
        grid=(t // tm,),
        in_specs=[pl.BlockSpec((tm, D_MODEL), row), pl.BlockSpec((tm, D_MODEL), row),
                  pl.BlockSpec((D_MODEL, D_MODEL), fixed), pl.BlockSpec((1, D_MODEL), fixed),
                  pl.BlockSpec((LANES, D_MODEL), fixed), pl.BlockSpec((LANES, D_MODEL), fixed),
                  pl.BlockSpec((LANES, tm), fixed)],
        out_specs=[pl.BlockSpec((tm, D_MODEL), row), pl.BlockSpec((tm, D_MODEL), row),
                   pl.BlockSpec((tm, LANES), row)],
        out_shape=[jax.ShapeDtypeStruct((t, D_MODEL), F32), jax.ShapeDtypeStruct((t, D_MODEL), BF16),
                   jax.ShapeDtypeStruct((t, LANES), F32)],
        compiler_params=_cparams("parallel"),
        name="out_proj_router",
    )(x, mix16, w16, g, r_hi, r_lo, r_bias)


def _moe_dense_body(u_ref, gate_ref, wg_ref, wu_ref, wd_ref, o_ref, acc_ref):
    e = pl.program_id(1)

    @pl.when(e == 0)
    def _():
        acc_ref[...] = jnp.zeros_like(acc_ref)

    u = u_ref[...]
    hg = _dot(u, wg_ref[0].astype(BF16))
    hu = _dot(u, wu_ref[0].astype(BF16))
    gates = gate_ref[...]
    lane = lax.broadcasted_iota(jnp.int32, gates.shape, 1)
    gcol = jnp.sum(jnp.where(lane == e, gates, 0.0), axis=1, keepdims=True)
    act = _silu(hg) * hu * gcol
    acc_ref[...] += _dot(act.astype(BF16), wd_ref[0].astype(BF16))

    @pl.when(e == N_EXPERTS - 1)
    def _():
        o_ref[...] = acc_ref[...]


def _moe_dense(u16, gates, wg, wu, wd, tm):
    t = u16.shape[0]
    row = lambda i, e: (i, 0)
    return pl.pallas_call(
        _moe_dense_body,
        grid=(t // tm, N_EXPERTS),
        in_specs=[pl.BlockSpec((tm, D_MODEL), row), pl.BlockSpec((tm, LANES), row),
                  pl.BlockSpec((1, D_MODEL, EXPERT_FF), lambda i, e: (e, 0, 0)),
                  pl.BlockSpec((1, D_MODEL, EXPERT_FF), lambda i, e: (e, 0, 0)),
                  pl.BlockSpec((1, EXPERT_FF, D_MODEL), lambda i, e: (e, 0, 0))],
        out_specs=pl.BlockSpec((tm, D_MODEL), row),
        out_shape=jax.ShapeDtypeStruct((t, D_MODEL), F32),
        scratch_shapes=[pltpu.VMEM((tm, D_MODEL), F32)],
        compiler_params=_cparams("parallel", "arbitrary"),
        name="moe_dense",
    )(u16, gates, wg, wu, wd)


def _ple_body(h_ref, y_ref, p_ref, gn_ref, wg_ref, bg_ref, wp_ref, fn_ref, o_ref):
    h = h_ref[...] + y_ref[...]
    a16 = _rms(h, gn_ref[...]).astype(BF16)
    gate = _sigmoid(_dot(a16, wg_ref[...]) + bg_ref[...])
    pp = _dot(p_ref[...].astype(BF16), wp_ref[...])
    h = h + gate * pp
    o_ref[...] = _rms(h, fn_ref[...])


def _ple(h1, ymoe, p, g_ple, wg16, bg, wp16, g_final, tm):
    t = h1.shape[0]
    row = lambda i: (i, 0)
    fixed = lambda i: (0, 0)
    return pl.pallas_call(
        _ple_body,
        grid=(t // tm,),
        in_specs=[pl.BlockSpec((tm, D_MODEL), row), pl.BlockSpec((tm, D_MODEL), row),
                  pl.BlockSpec((tm, PLE_DIM), row), pl.BlockSpec((1, D_MODEL), fixed),
                  pl.BlockSpec((D_MODEL, D_MODEL), fixed), pl.BlockSpec((1, D_MODEL), fixed),
                  pl.BlockSpec((PLE_DIM, D_MODEL), fixed), pl.BlockSpec((1, D_MODEL), fixed)],
        out_specs=pl.BlockSpec((tm, D_MODEL), row),
        out_shape=jax.ShapeDtypeStruct((t, D_MODEL), F32),
        compiler_params=_cparams("parallel"),
        name="ple_final",
    )(h1, ymoe, p, g_ple, wg16, bg, wp16, g_final)


MIX_ROWS = 512
SSD_Q = 128
POOL_HDR = 128
CONV_HDR = 8


def _softplus(x):
    return jnp.maximum(x, 0.0) + jnp.log1p(jnp.exp(-jnp.abs(x)))


def _head_expand_matrix():
    r = lax.broadcasted_iota(jnp.int32, (LANES, SSD_WIDTH), 0)
    c = lax.broadcasted_iota(jnp.int32, (LANES, SSD_WIDTH), 1)
    return (lax.shift_right_logical(c, 6) == r).astype(BF16)


def _state_block_mask():
    r = lax.broadcasted_iota(jnp.int32, (SSD_BC, SSD_WIDTH), 0)
    c = lax.broadcasted_iota(jnp.int32, (SSD_BC, SSD_WIDTH), 1)
    return (lax.shift_right_logical(r, 6) == lax.shift_right_logical(c, 8)).astype(F32)


def _mix_body(vp_ref, z_ref, xbc_ref, dt_ref, cw_ref, cb_ref, dtb_ref, alog_ref, dsk_ref, nw_ref,
              plw_ref, plb_ref, psc_ref, mix_ref, st_ref, pool_ext, conv_ext, s_ref):
    c = pl.program_id(1)
    rows = MIX_ROWS
    q_len = SSD_Q

    @pl.when(c == 0)
    def _():
        pool_ext[0:POOL_HDR, :] = jnp.zeros((POOL_HDR, POOL_WIDTH), F32)
        conv_ext[0:CONV_HDR, :] = jnp.zeros((CONV_HDR, CONV_DIM), F32)
        s_ref[...] = jnp.zeros_like(s_ref)

    @pl.when(c > 0)
    def _():
        pool_ext[0:POOL_HDR, :] = pool_ext[rows:rows + POOL_HDR, :]
        conv_ext[0:CONV_HDR, :] = conv_ext[rows:rows + CONV_HDR, :]

    pool_ext[POOL_HDR:POOL_HDR + rows, :] = vp_ref[...]
    conv_ext[CONV_HDR:CONV_HDR + rows, :] = xbc_ref[...]

    li = lax.broadcasted_iota(jnp.int32, (q_len, q_len), 0)
    si = lax.broadcasted_iota(jnp.int32, (q_len, q_len), 1)
    causal = li >= si
    tri = causal.astype(BF16)
    expand = _head_expand_matrix()
    blockmask = _state_block_mask()
    lane = lax.broadcasted_iota(jnp.int32, (q_len, LANES), 1)
    left = lane < SSD_HEAD_DIM
    a_neg = -jnp.exp(alog_ref[...])
    wl = lax.broadcasted_iota(jnp.int32, (q_len, 2 * q_len), 0)
    wj = lax.broadcasted_iota(jnp.int32, (q_len, 2 * q_len), 1)
    rowi = lax.broadcasted_iota(jnp.int32, (q_len, LANES), 0)

    for q in range(rows // q_len):
        r0 = q * q_len
        base = CONV_HDR + r0
        conv = cb_ref[...] + conv_ext[base - 3:base - 3 + q_len, :] * cw_ref[0:1, :]
        conv = conv + conv_ext[base - 2:base - 2 + q_len, :] * cw_ref[1:2, :]
        conv = conv + conv_ext[base - 1:base - 1 + q_len, :] * cw_ref[2:3, :]
        conv = conv + conv_ext[base:base + q_len, :] * cw_ref[3:4, :]
        conv = _silu(conv)
        xs = conv[:, 0:SSD_WIDTH]
        b16 = conv[:, SSD_WIDTH:SSD_WIDTH + SSD_BC].astype(BF16)
        c_all = conv[:, SSD_WIDTH + SSD_BC:CONV_DIM]
        c16 = c_all.astype(BF16)
        dt = _softplus(dt_ref[r0:r0 + q_len, :] + dtb_ref[...])
        acs = _dot_exact_lhs(tri, dt * a_neg)
        acs_t = acs.T
        dtx = _dot_exact_rhs(dt, expand, terms=2)
        acsx = _dot_exact_rhs(acs, expand)
        last_x = acsx[q_len - 1:q_len, :]
        xdt = xs * dtx
        xdt16 = xdt.astype(BF16)
        s_old = s_ref[...]
        y_off = _dot(c16, s_old.astype(BF16)) * jnp.exp(acsx)
        contrib = _dot_tn(b16, (xdt * jnp.exp(last_x - acsx)).astype(BF16)) * blockmask
        s_ref[...] = s_old * jnp.exp(last_x) + contrib
        cb = []
        for g in range(SSD_GROUPS):
            cg = jnp.where(lax.shift_right_logical(lane, 6) == g, c_all, 0.0).astype(BF16)
            cb.append(_dot_nt(cg, b16))
        z = z_ref[r0:r0 + q_len, :]
        y_blocks = []
        for j in range(SSD_HEADS // 2):
            blk = slice(j * LANES, (j + 1) * LANES)
            zs = []
            for h in (2 * j, 2 * j + 1):
                seg = acs[:, h:h + 1] - acs_t[h:h + 1, :]
                decay = jnp.where(causal, jnp.exp(seg), 0.0)
                scores = (cb[h // (SSD_HEADS // SSD_GROUPS)] * decay).astype(BF16)
                zs.append(_dot(scores, xdt16[:, blk]))
            y = jnp.where(left, zs[0], zs[1]) + y_off[:, blk]
            y = y + dsk_ref[:, blk] * xs[:, blk]
            y_blocks.append(y * _silu(z[:, blk]))
        for g in range(SSD_GROUPS):
            y0, y1 = y_blocks[2 * g], y_blocks[2 * g + 1]
            ss = jnp.sum(y0 * y0, axis=-1, keepdims=True) + jnp.sum(y1 * y1, axis=-1, keepdims=True)
            rs = lax.rsqrt(ss * (1.0 / (2 * LANES)) + EPS)
            for k, yk in ((2 * g, y0), (2 * g + 1, y1)):
                blk = slice(k * LANES, (k + 1) * LANES)
                out = yk * rs * nw_ref[:, blk]
                mix_ref[r0:r0 + q_len, POOL_WIDTH + k * LANES:POOL_WIDTH + (k + 1) * LANES] = out.astype(BF16)
        pos = c * rows + r0 + rowi
        for g, w in enumerate(POOL_WINDOWS):
            blk = slice(g * POOL_GROUP_DIM, (g + 1) * POOL_GROUP_DIM)
            pe = pool_ext[r0:r0 + 2 * q_len, blk]
            band = ((wj <= wl + POOL_HDR) & (wj > wl + POOL_HDR - w)).astype(BF16)
            winsum = _dot_exact_lhs(band, pe, terms=2)
            cnt = jnp.minimum(pos + 1, w).astype(F32)
            m = winsum / cnt - pe[q_len:2 * q_len, :]
            yg = _dot(m.astype(BF16), plw_ref[g]) + plb_ref[:, blk]
            mix_ref[r0:r0 + q_len, blk] = (yg * psc_ref[:, blk]).astype(BF16)

    @pl.when(c == pl.num_programs(1) - 1)
    def _():
        st_ref[0] = s_ref[...]


def _mix_prompt(vp, z, xbc, dt, cw, cb, dtb, alog, dsk, nw, plw16, plb, psc, nb, seq):
    steps = seq // MIX_ROWS
    row = lambda b, c: (b * steps + c, 0)
    fixed2 = lambda b, c: (0, 0)
    return pl.pallas_call(
        _mix_body,
        grid=(nb, steps),
        in_specs=[pl.BlockSpec((MIX_ROWS, POOL_WIDTH), row), pl.BlockSpec((MIX_ROWS, SSD_WIDTH), row),
                  pl.BlockSpec((MIX_ROWS, CONV_DIM), row), pl.BlockSpec((MIX_ROWS, LANES), row),
                  pl.BlockSpec((SSD_CONV, CONV_DIM), fixed2), pl.BlockSpec((1, CONV_DIM), fixed2),
                  pl.BlockSpec((1, LANES), fixed2), pl.BlockSpec((1, LANES), fixed2),
                  pl.BlockSpec((1, SSD_WIDTH), fixed2), pl.BlockSpec((1, SSD_WIDTH), fixed2),
                  pl.BlockSpec((len(POOL_WINDOWS), POOL_GROUP_DIM, POOL_GROUP_DIM), lambda b, c: (0, 0, 0)),
                  pl.BlockSpec((1, POOL_WIDTH), fixed2), pl.BlockSpec((1, POOL_WIDTH), fixed2)],
        out_specs=[pl.BlockSpec((MIX_ROWS, D_MODEL), row),
                   pl.BlockSpec((1, SSD_BC, SSD_WIDTH), lambda b, c: (b, 0, 0))],
        out_shape=[jax.ShapeDtypeStruct((nb * seq, D_MODEL), BF16),
                   jax.ShapeDtypeStruct((nb, SSD_BC, SSD_WIDTH), F32)],
        scratch_shapes=[pltpu.VMEM((POOL_HDR + MIX_ROWS, POOL_WIDTH), F32),
                        pltpu.VMEM((CONV_HDR + MIX_ROWS, CONV_DIM), F32),
                        pltpu.VMEM((SSD_BC, SSD_WIDTH), F32)],
        compiler_params=_cparams("parallel", "arbitrary"),
        name="mix_prompt",
    )(vp, z, xbc, dt, cw, cb, dtb, alog, dsk, nw, plw16, plb, psc)


STEP_SEQS = 16


def _mix_step_body(vp_ref, z_ref, xbc_ref, dt_ref, sp_ref, sc_ref, st_ref, cw_ref, cb_ref, dtb_ref, alog_ref,
                   dsk_ref, nw_ref, plw_ref, plb_ref, psc_ref, mix_ref, so_ref,
                   xdt_t, dec_t, bc_rows, xs_keep, y_t):
    i = pl.program_id(0)
    n_seq = vp_ref.shape[0]

    @pl.when(i == 0)
    def _():
        conv = cb_ref[...] + sc_ref[:, 0, :] * cw_ref[0:1, :]
        conv = conv + sc_ref[:, 1, :] * cw_ref[1:2, :]
        conv = conv + sc_ref[:, 2, :] * cw_ref[2:3, :]
        conv = conv + xbc_ref[...] * cw_ref[3:4, :]
        conv = _silu(conv)
        xs = conv[:, 0:SSD_WIDTH]
        b_all = conv[:, SSD_WIDTH:SSD_WIDTH + SSD_BC]
        c_all = conv[:, SSD_WIDTH + SSD_BC:CONV_DIM]
        expand = _head_expand_matrix()
        dt = _softplus(dt_ref[...] + dtb_ref[...])
        d_a = dt * (-jnp.exp(alog_ref[...]))
        dtx = _dot_exact_rhs(dt, expand, terms=2)
        decx = jnp.exp(_dot_exact_rhs(d_a, expand))
        xdt_t[...] = (xs * dtx).T
        dec_t[...] = decx.T
        bc_rows[0] = b_all
        bc_rows[1] = pltpu.roll(b_all, SSD_STATE, 1)
        bc_rows[2] = c_all
        bc_rows[3] = pltpu.roll(c_all, SSD_STATE, 1)
        xs_keep[...] = xs
        y_t[...] = jnp.zeros_like(y_t)
        v = vp_ref[...]
        for g, w in enumerate(POOL_WINDOWS):
            blk = slice(g * POOL_GROUP_DIM, (g + 1) * POOL_GROUP_DIM)
            acc = sp_ref[:, POOL_BUF - (w - 1), blk]
            for k in range(w - 2, 0, -1):
                acc = acc + sp_ref[:, POOL_BUF - k, blk]
            acc = acc + v[:, blk]
            m = acc / float(min(PAST_LEN + 1, w)) - v[:, blk]
            yg = _dot(m.astype(BF16), plw_ref[g]) + plb_ref[:, blk]
            mix_ref[:, blk] = (yg * psc_ref[:, blk]).astype(BF16)

    lane = lax.broadcasted_iota(jnp.int32, (1, LANES), 1)
    top = lax.broadcasted_iota(jnp.int32, (SSD_WIDTH, SSD_STATE), 0) < SSD_WIDTH // SSD_GROUPS

    def one_seq(bb, carry):
        b = i * STEP_SEQS + bb
        pick = lane == b
        xcol = jnp.sum(jnp.where(pick, xdt_t[...], 0.0), axis=1, keepdims=True)
        dcol = jnp.sum(jnp.where(pick, dec_t[...], 0.0), axis=1, keepdims=True)
        rows = [bc_rows[k, pl.ds(b, 1), 0:SSD_STATE] for k in range(4)]
        bsel = jnp.where(top, rows[0], rows[1])
        csel = jnp.where(top, rows[2], rows[3])
        s_new = st_ref[bb].reshape(SSD_WIDTH, SSD_STATE) * dcol + xcol * bsel
        so_ref[bb] = s_new.reshape(SSD_HEADS, SSD_HEAD_DIM, SSD_STATE)
        ycol = jnp.sum(s_new * csel, axis=1, keepdims=True)
        y_t[...] = jnp.where(pick, ycol, y_t[...])
        return carry

    lax.fori_loop(0, STEP_SEQS, one_seq, 0)

    @pl.when(i == pl.num_programs(0) - 1)
    def _():
        xs = xs_keep[...]
        y = y_t[...].T + dsk_ref[...] * xs
        y = y * _silu(z_ref[...])
        width = SSD_WIDTH // SSD_GROUPS
        for g in range(SSD_GROUPS):
            blk = slice(g * width, (g + 1) * width)
            yg = y[:, blk]
            rs = lax.rsqrt(jnp.mean(yg * yg, axis=-1, keepdims=True) + EPS)
            mix_ref[:, POOL_WIDTH + g * width:POOL_WIDTH + (g + 1) * width] = (yg * rs * nw_ref[:, blk]).astype(BF16)


def _mix_step(vp, z, xbc, dt, state_pool, state_conv, state_ssm, cw, cb, dtb, alog, dsk, nw, plw16, plb, psc):
    n = vp.shape[0]
    fixed2 = lambda i: (0, 0)
    fixed3 = lambda i: (0, 0, 0)
    st_spec = pl.BlockSpec((STEP_SEQS, SSD_HEADS, SSD_HEAD_DIM, SSD_STATE), lambda i: (i, 0, 0, 0))
    return pl.pallas_call(
        _mix_step_body,
        grid=(n // STEP_SEQS,),
        in_specs=[pl.BlockSpec((n, POOL_WIDTH), fixed2), pl.BlockSpec((n, SSD_WIDTH), fixed2),
                  pl.BlockSpec((n, CONV_DIM), fixed2), pl.BlockSpec((n, LANES), fixed2),
                  pl.BlockSpec((n, POOL_BUF, POOL_WIDTH), fixed3),
                  pl.BlockSpec((n, SSD_CONV - 1, CONV_DIM), fixed3),
                  st_spec,
                  pl.BlockSpec((SSD_CONV, CONV_DIM), fixed2), pl.BlockSpec((1, CONV_DIM), fixed2),
                  pl.BlockSpec((1, LANES), fixed2), pl.BlockSpec((1, LANES), fixed2),
                  pl.BlockSpec((1, SSD_WIDTH), fixed2), pl.BlockSpec((1, SSD_WIDTH), fixed2),
                  pl.BlockSpec((len(POOL_WINDOWS), POOL_GROUP_DIM, POOL_GROUP_DIM), fixed3),
                  pl.BlockSpec((1, POOL_WIDTH), fixed2), pl.BlockSpec((1, POOL_WIDTH), fixed2)],
        out_specs=[pl.BlockSpec((n, D_MODEL), fixed2), st_spec],
        out_shape=[jax.ShapeDtypeStruct((n, D_MODEL), BF16),
                   jax.ShapeDtypeStruct(state_ssm.shape, F32)],
        scratch_shapes=[pltpu.VMEM((SSD_WIDTH, LANES), F32), pltpu.VMEM((SSD_WIDTH, LANES), F32),
                        pltpu.VMEM((4, n, LANES), F32), pltpu.VMEM((n, SSD_WIDTH), F32),
                        pltpu.VMEM((SSD_WIDTH, LANES), F32)],
        compiler_params=_cparams("arbitrary"),
        name="mix_step",
    )(vp, z, xbc, dt, state_pool, state_conv, state_ssm, cw, cb, dtb, alog, dsk, nw, plw16, plb, psc)


PROMPT_TILE = 512
MOE_TILE = 1024


def _row(v):
    return v.reshape(1, -1).astype(F32)


def _pad_lanes(v):
    return jnp.pad(v.reshape(1, -1).astype(F32), ((0, 0), (0, LANES - v.size)))


def _state_from_blocks(st):
    n = st.shape[0]
    hpg = SSD_HEADS // SSD_GROUPS
    s6 = st.reshape(n, SSD_GROUPS, SSD_STATE, SSD_GROUPS, hpg, SSD_HEAD_DIM)
    per_group = [s6[:, g, :, g] for g in range(SSD_GROUPS)]
    s = jnp.stack(per_group, axis=1)
    return jnp.transpose(s, (0, 1, 3, 4, 2)).reshape(n, SSD_HEADS, SSD_HEAD_DIM, SSD_STATE)


def kernel(x_prompt, x_sample, p_prompt, p_sample, state_pool, state_conv, state_ssm, norm_mix, w_in, pool_lin_w, pool_lin_b, pool_scale, conv_w, conv_b, dt_bias, a_log, d_skip, ssd_norm, w_out, norm_ffn, router_grp_w, router_grp_b, router_exp_w, router_exp_b, exp_w_gate, exp_w_up, exp_w_down, norm_ple, ple_gate_w, ple_gate_b, ple_proj_w, norm_final):
    nb, seq, _ = x_prompt.shape
    ns = x_sample.shape[0]
    assert ns == LANES and x_sample.shape[1] == 1 and seq % MIX_ROWS == 0 and seq >= POOL_BUF

    w_in16 = jnp.pad(w_in[0], ((0, 0), (0, IN_COLS - w_in.shape[2]))).astype(BF16)
    w_out16 = w_out[0].astype(BF16)
    g_mix, g_ffn, g_ple, g_fin = _row(norm_mix[0]), _row(norm_ffn[0]), _row(norm_ple[0]), _row(norm_final)
    cw, cb = conv_w[0].astype(F32), _row(conv_b[0])
    dtb, alog = _pad_lanes(dt_bias[0]), _pad_lanes(a_log[0])
    dsk = _row(jnp.repeat(d_skip[0], SSD_HEAD_DIM))
    nw = _row(ssd_norm[0])
    plw16 = pool_lin_w[0].astype(BF16)
    plb, psc = _row(pool_lin_b[0]), _row(pool_scale[0])
    zeros4 = jnp.zeros((D_MODEL, 8 - N_EXPERT_GROUPS), F32)
    r_w = jnp.concatenate([router_grp_w[0], zeros4, router_exp_w[0],
                           jnp.zeros((D_MODEL, LANES - 8 - N_EXPERTS), F32)], axis=1).T
    r_hi = r_w.astype(BF16)
    r_lo = (r_w - r_hi.astype(F32)).astype(BF16)
    r_b = jnp.concatenate([router_grp_b[0], jnp.zeros((8 - N_EXPERT_GROUPS,), F32), router_exp_b[0],
                           jnp.zeros((LANES - 8 - N_EXPERTS,), F32)])
    wg = exp_w_gate[0].reshape(N_EXPERTS, D_MODEL, EXPERT_FF)
    wu = exp_w_up[0].reshape(N_EXPERTS, D_MODEL, EXPERT_FF)
    wd = exp_w_down[0].reshape(N_EXPERTS, EXPERT_FF, D_MODEL)
    pg16 = ple_gate_w[0].astype(BF16)
    pgb = _row(ple_gate_b[0])
    pp16 = ple_proj_w[0].astype(BF16)

    def tokenwise_tail(x, mix16, p, tm, moe_tm):
        r_bias = jnp.broadcast_to(r_b[:, None], (LANES, tm))
        h1, u16, gates = _out_proj(x, mix16, w_out16, g_ffn, r_hi, r_lo, r_bias, tm)
        ymoe = _moe_dense(u16, gates, wg, wu, wd, moe_tm)
        return _ple(h1, ymoe, p, g_ple, pg16, pgb, pp16, g_fin, tm)

    xp = x_prompt.reshape(nb * seq, D_MODEL)
    vp, z, xbc, dt = _in_proj(xp, g_mix, w_in16, PROMPT_TILE)
    mix16, st = _mix_prompt(vp, z, xbc, dt, cw, cb, dtb, alog, dsk, nw, plw16, plb, psc, nb, seq)
    y_prompt = tokenwise_tail(xp, mix16, p_prompt[0].reshape(nb * seq, PLE_DIM), PROMPT_TILE, MOE_TILE)
    pool_p = vp.reshape(nb, seq, POOL_WIDTH)[:, seq - POOL_BUF:]
    conv_p = xbc.reshape(nb, seq, CONV_DIM)[:, seq - (SSD_CONV - 1):]
    ssm_p = _state_from_blocks(st)

    xs_ = x_sample.reshape(ns, D_MODEL)
    vp_s, z_s, xbc_s, dt_s = _in_proj(xs_, g_mix, w_in16, ns)
    mix_s, ssm_s = _mix_step(vp_s, z_s, xbc_s, dt_s, state_pool[0], state_conv[0], state_ssm[0],
                             cw, cb, dtb, alog, dsk, nw, plw16, plb, psc)
    y_sample = tokenwise_tail(xs_, mix_s, p_sample[0].reshape(ns, PLE_DIM), ns, ns)
    pool_s = jnp.concatenate([state_pool[0][:, 1:], vp_s[:, None, :]], axis=1)
    conv_s = jnp.concatenate([state_conv[0][:, 1:], xbc_s[:, None, :]], axis=1)

    return (y_prompt.reshape(nb, seq, D_MODEL), y_sample.reshape(ns, 1, D_MODEL),
            pool_p[None], conv_p[None], ssm_p[None], pool_s[None], conv_s[None], ssm_s[None])
```

```python
import functools

import jax
import jax.numpy as jnp
from jax import lax
from jax.experimental import pallas as pl
from jax.experimental.pallas import tpu as pltpu

F32 = jnp.float32
BF16 = jnp.bfloat16

D_MODEL = 1024
POOL_WIDTH = 512
POOL_WINDOWS = (2, 4, 8, 16)
POOL_GROUP_DIM = 128
POOL_BUF = 15
SSD_WIDTH = 512
SSD_HEAD_DIM = 64
SSD_HEADS = 8
SSD_GROUPS = 2
SSD_STATE = 64
SSD_CONV = 4
SSD_BC = SSD_GROUPS * SSD_STATE
CONV_DIM = SSD_WIDTH + 2 * SSD_BC
N_EXPERT_GROUPS = 4
EXPERTS_PER_GROUP = 8
N_EXPERTS = N_EXPERT_GROUPS * EXPERTS_PER_GROUP
EXPERT_FF = 256
PLE_DIM = 256
PAST_LEN = 16384
EPS = 1e-6

LANES = 128
IN_COLS = 1920
DT_OFF = POOL_WIDTH + SSD_WIDTH + CONV_DIM
VMEM_LIMIT = 56 * 1024 * 1024


def _cparams(*sem):
    return pltpu.CompilerParams(dimension_semantics=sem, vmem_limit_bytes=VMEM_LIMIT)


def _rms(x, g):
    return x * lax.rsqrt(jnp.mean(x * x, axis=-1, keepdims=True) + EPS) * g


def _sigmoid(x):
    return 1.0 / (1.0 + jnp.exp(-x))


def _silu(x):
    return x * _sigmoid(x)


def _split3(v):
    hi = v.astype(BF16)
    r = v - hi.astype(F32)
    mid = r.astype(BF16)
    lo = (r - mid.astype(F32)).astype(BF16)
    return hi, mid, lo


def _dot(a, b):
    return jnp.dot(a, b, preferred_element_type=F32)


def _dot_nt(a, b):
    return lax.dot_general(a, b, (((1,), (1,)), ((), ())), preferred_element_type=F32)


def _dot_tn(a, b):
    return lax.dot_general(a, b, (((0,), (0,)), ((), ())), preferred_element_type=F32)


def _dot_exact_lhs(sel, v, terms=3):
    acc = None
    for t in _split3(v)[:terms]:
        p = _dot(sel, t)
        acc = p if acc is None else acc + p
    return acc


def _dot_exact_rhs(v, sel, terms=3):
    acc = None
    for t in _split3(v)[:terms]:
        p = _dot(t, sel)
        acc = p if acc is None else acc + p
    return acc


def _in_proj_body(x_ref, g_ref, w_ref, vp_ref, z_ref, xbc_ref, dt_ref):
    a16 = _rms(x_ref[...], g_ref[...]).astype(BF16)
    vp_ref[...] = _dot(a16, w_ref[:, 0:POOL_WIDTH])
    z_ref[...] = _dot(a16, w_ref[:, POOL_WIDTH:POOL_WIDTH + SSD_WIDTH])
    xbc_ref[...] = _dot(a16, w_ref[:, POOL_WIDTH + SSD_WIDTH:DT_OFF])
    dt_ref[...] = _dot(a16, w_ref[:, DT_OFF:IN_COLS])


def _in_proj(x, g, w16, tm):
    t = x.shape[0]
    row = lambda i: (i, 0)
    fixed = lambda i: (0, 0)
    return pl.pallas_call(
        _in_proj_body,
        grid=(t // tm,),
        in_specs=[pl.BlockSpec((tm, D_MODEL), row), pl.BlockSpec((1, D_MODEL), fixed),
                  pl.BlockSpec((D_MODEL, IN_COLS), fixed)],
        out_specs=[pl.BlockSpec((tm, POOL_WIDTH), row), pl.BlockSpec((tm, SSD_WIDTH), row),
                   pl.BlockSpec((tm, CONV_DIM), row), pl.BlockSpec((tm, LANES), row)],
        out_shape=[jax.ShapeDtypeStruct((t, POOL_WIDTH), F32), jax.ShapeDtypeStruct((t, SSD_WIDTH), F32),
                   jax.ShapeDtypeStruct((t, CONV_DIM), F32), jax.ShapeDtypeStruct((t, LANES), F32)],
        compiler_params=_cparams("parallel"),
        name="in_proj",
    )(x, g, w16)


def _route(lg):
    tm = lg.shape[1]
    gl = lg[0:N_EXPERT_GROUPS, :]
    gmax = jnp.max(gl, axis=0, keepdims=True)
    gsum = jnp.sum(jnp.exp(gl - gmax), axis=0, keepdims=True)
    g_w = 1.0 / gsum
    gi = lax.broadcasted_iota(jnp.int32, gl.shape, 0)
    g_idx = jnp.min(jnp.where(gl == gmax, gi, N_EXPERT_GROUPS), axis=0, keepdims=True)
    sel = jnp.zeros((EXPERTS_PER_GROUP, tm), F32)
    for g in range(N_EXPERT_GROUPS):
        blk = lg[8 + g * EXPERTS_PER_GROUP:8 + (g + 1) * EXPERTS_PER_GROUP, :]
        sel = jnp.where(g_idx == g, blk, sel)
    ei = lax.broadcasted_iota(jnp.int32, sel.shape, 0)
    m1 = jnp.max(sel, axis=0, keepdims=True)
    i1 = jnp.min(jnp.where(sel == m1, ei, EXPERTS_PER_GROUP), axis=0, keepdims=True)
    rest = jnp.where(ei == i1, -jnp.inf, sel)
    m2 = jnp.max(rest, axis=0, keepdims=True)
    i2 = jnp.min(jnp.where(rest == m2, ei, EXPERTS_PER_GROUP), axis=0, keepdims=True)
    p2 = jnp.exp(m2 - m1)
    w1 = g_w / (1.0 + p2)
    w2 = g_w * p2 / (1.0 + p2)
    return g_idx * EXPERTS_PER_GROUP + i1, g_idx * EXPERTS_PER_GROUP + i2, w1, w2


def _out_proj_body(x_ref, mix_ref, w_ref, g_ref, rh_ref, rl_ref, rb_ref, h_ref, u_ref, rt_ref, wt_ref, cnt_ref):
    h = x_ref[...] + _dot(mix_ref[...], w_ref[...])
    h_ref[...] = h
    u = _rms(h, g_ref[...])
    u_hi = u.astype(BF16)
    u_ref[...] = u_hi
    u_lo = (u - u_hi.astype(F32)).astype(BF16)
    lg = _dot_nt(rh_ref[...], u_hi) + _dot_nt(rh_ref[...], u_lo) + _dot_nt(rl_ref[...], u_hi) + rb_ref[...]
    b1, b2, w1, w2 = _route(lg)
    tm = lg.shape[1]
    r8 = lax.broadcasted_iota(jnp.int32, (8, tm), 0)
    rt_ref[0] = jnp.where(r8 == 0, b1.astype(F32), jnp.where(r8 == 1, b2.astype(F32), 0.0))
    wt_ref[...] = jnp.concatenate([jnp.broadcast_to(w1, (LANES, tm)).T, jnp.broadcast_to(w2, (LANES, tm)).T], axis=1)
    kio = lax.broadcasted_iota(jnp.int32, (N_EXPERTS, tm), 0)
    hits = ((kio == b1) | (kio == b2)).astype(F32)
    cnt_ref[0] = jnp.broadcast_to(jnp.sum(hits, axis=1, keepdims=True), (N_EXPERTS, LANES))


def _out_proj(x, mix16, w16, g, r_hi, r_lo, r_bias, tm):
    t = x.shape[0]
    nt = t // tm
    row = lambda i: (i, 0)
    fixed = lambda i: (0, 0)
    tile3 = lambda i: (i, 0, 0)
    return pl.pallas_call(
        _out_proj_body,
        grid=(nt,),
        in_specs=[pl.BlockSpec((tm, D_MODEL), row), pl.BlockSpec((tm, D_MODEL), row),
                  pl.BlockSpec((D_MODEL, D_MODEL), fixed), pl.BlockSpec((1, D_MODEL), fixed),
                  pl.BlockSpec((LANES, D_MODEL), fixed), pl.BlockSpec((LANES, D_MODEL), fixed),
                  pl.BlockSpec((LANES, tm), fixed)],
        out_specs=[pl.BlockSpec((tm, D_MODEL), row), pl.BlockSpec((tm, D_MODEL), row),
                   pl.BlockSpec((1, 8, tm), tile3), pl.BlockSpec((tm, 2 * LANES), row),
                   pl.BlockSpec((1, N_EXPERTS, LANES), tile3)],
        out_shape=[jax.ShapeDtypeStruct((t, D_MODEL), F32), jax.ShapeDtypeStruct((t, D_MODEL), BF16),
                   jax.ShapeDtypeStruct((nt, 8, tm), F32), jax.ShapeDtypeStruct((t, 2 * LANES), F32),
                   jax.ShapeDtypeStruct((nt, N_EXPERTS, LANES), F32)],
        compiler_params=_cparams("parallel"),
        name="out_proj_router",
    )(x, mix16, w16, g, r_hi, r_lo, r_bias)


RUN_PAD = 8
MOE_TILE = 512


def _tile_lanes(v, width):
    reps = width // LANES
    return v if reps == 1 else jnp.concatenate([v] * reps, axis=1)


def _sorted_rows_per_tile(tm):
    need = 2 * tm + N_EXPERTS * (RUN_PAD - 1)
    return -(-need // LANES) * LANES


def _sort_body(cd_ref, nch_ref, u_ref, rt_ref, off_ref, *rest):
    xs_ref, dt_ref, loc, sem = rest[-4:]
    i = pl.program_id(0)
    tm = u_ref.shape[0]
    rows = loc.shape[0]
    rt = rt_ref[0]
    b1 = rt[0:1, :].astype(jnp.int32)
    b2 = rt[1:2, :].astype(jnp.int32)
    kio = lax.broadcasted_iota(jnp.int32, (N_EXPERTS, tm), 0)
    o1 = kio == b1
    o2 = kio == b2
    before = (lax.broadcasted_iota(jnp.int32, (tm, tm), 0) < lax.broadcasted_iota(jnp.int32, (tm, tm), 1))
    start = _dot((o1 | o2).astype(BF16), before.astype(BF16)) + _tile_lanes(off_ref[0], tm)
    d1 = jnp.sum(jnp.where(o1, start, 0.0), axis=0, keepdims=True)
    d2 = jnp.sum(jnp.where(o2, start, 0.0), axis=0, keepdims=True)
    dt_ref[...] = jnp.concatenate([jnp.broadcast_to(d1, (LANES, tm)).T, jnp.broadcast_to(d2, (LANES, tm)).T], axis=1)
    rio = lax.broadcasted_iota(jnp.int32, (rows, tm), 0)
    perm = ((rio == d1.astype(jnp.int32)) | (rio == d2.astype(jnp.int32))).astype(BF16)
    loc[...] = _dot(perm, u_ref[...])

    def chunk_copy(c, d):
        return pltpu.make_async_copy(loc.at[pl.ds(pl.multiple_of(c * RUN_PAD, RUN_PAD), RUN_PAD), :],
                                     xs_ref.at[pl.ds(pl.multiple_of(d, RUN_PAD), RUN_PAD), :], sem)

    def issue(c, carry):
        chunk_copy(c, cd_ref[i, c]).start()
        return carry

    def drain(c, carry):
        chunk_copy(c, cd_ref[i, c]).wait()
        return carry

    lax.fori_loop(0, nch_ref[i], issue, 0)
    lax.fori_loop(0, nch_ref[i], drain, 0)


def _sort_tokens(chunk_dst, n_chunks, u16, rt, off_v, xs_prev, tm, total_rows):
    t = u16.shape[0]
    rows = _sorted_rows_per_tile(tm)
    in_specs = [pl.BlockSpec((tm, D_MODEL), lambda i, cd, nc: (i, 0)),
                pl.BlockSpec((1, 8, tm), lambda i, cd, nc: (i, 0, 0)),
                pl.BlockSpec((1, N_EXPERTS, LANES), lambda i, cd, nc: (i, 0, 0))]
    args = [chunk_dst, n_chunks, u16, rt, off_v]
    aliases = {}
    if xs_prev is not None:
        in_specs.append(pl.BlockSpec(memory_space=pl.ANY))
        aliases = {len(args): 0}
        args.append(xs_prev)
    return pl.pallas_call(
        _sort_body,
        grid_spec=pltpu.PrefetchScalarGridSpec(
            num_scalar_prefetch=2,
            grid=(t // tm,),
            in_specs=in_specs,
            out_specs=[pl.BlockSpec(memory_space=pl.ANY),
                       pl.BlockSpec((tm, 2 * LANES), lambda i, cd, nc: (i, 0))],
            scratch_shapes=[pltpu.VMEM((rows, D_MODEL), F32), pltpu.SemaphoreType.DMA(())]),
        out_shape=[jax.ShapeDtypeStruct((total_rows, D_MODEL), F32),
                   jax.ShapeDtypeStruct((t, 2 * LANES), F32)],
        input_output_aliases=aliases,
        compiler_params=_cparams("arbitrary"),
        name="sort_tokens",
    )(*args)


def _moe_body(te_ref, na_ref, x_ref, wg_ref, wu_ref, wd_ref, o_ref, wg16, wu16, wd16):
    j = pl.program_id(0)

    @pl.when(j < na_ref[0])
    def _():
        @pl.when((j == 0) | (te_ref[j] != te_ref[jnp.maximum(j - 1, 0)]))
        def _():
            wg16[...] = wg_ref[0].astype(BF16)
            wu16[...] = wu_ref[0].astype(BF16)
            wd16[...] = wd_ref[0].astype(BF16)

        x = x_ref[...].astype(BF16)
        act = _silu(_dot(x, wg16[...])) * _dot(x, wu16[...])
        o_ref[...] = _dot(act.astype(BF16), wd16[...])


def _moe_sorted(tile_expert, n_active, xs, wg, wu, wd):
    total_rows = xs.shape[0]
    n_tiles = total_rows // MOE_TILE
    x_map = lambda j, te, na: (jnp.minimum(j, na[0] - 1), 0)
    w_map = lambda j, te, na: (te[j], 0, 0)
    return pl.pallas_call(
        _moe_body,
        grid_spec=pltpu.PrefetchScalarGridSpec(
            num_scalar_prefetch=2,
            grid=(n_tiles,),
            in_specs=[pl.BlockSpec((MOE_TILE, D_MODEL), x_map),
                      pl.BlockSpec((1, D_MODEL, EXPERT_FF), w_map),
                      pl.BlockSpec((1, D_MODEL, EXPERT_FF), w_map),
                      pl.BlockSpec((1, EXPERT_FF, D_MODEL), w_map)],
            out_specs=pl.BlockSpec((MOE_TILE, D_MODEL), lambda j, te, na: (j, 0)),
            scratch_shapes=[pltpu.VMEM((D_MODEL, EXPERT_FF), BF16), pltpu.VMEM((D_MODEL, EXPERT_FF), BF16),
                            pltpu.VMEM((EXPERT_FF, D_MODEL), BF16)]),
        out_shape=jax.ShapeDtypeStruct((total_rows, D_MODEL), F32),
        compiler_params=_cparams("arbitrary"),
        name="moe_sorted",
    )(tile_expert, n_active, xs, wg, wu, wd)


def _ple_body(cd_ref, nch_ref, h_ref, dt_ref, wt_ref, p_ref, gn_ref, wg_ref, bg_ref, wp_ref, fn_ref, ys_ref,
              o_ref, loc, sem):
    i = pl.program_id(0)
    tm = h_ref.shape[0]
    rows = loc.shape[0]

    @pl.when(i == 0)
    def _():
        loc[...] = jnp.zeros_like(loc)

    def chunk_copy(c, d):
        return pltpu.make_async_copy(ys_ref.at[pl.ds(pl.multiple_of(d, RUN_PAD), RUN_PAD), :],
                                     loc.at[pl.ds(pl.multiple_of(c * RUN_PAD, RUN_PAD), RUN_PAD), :], sem)

    def issue(c, carry):
        chunk_copy(c, cd_ref[i, c]).start()
        return carry

    def drain(c, carry):
        chunk_copy(c, cd_ref[i, c]).wait()
        return carry

    lax.fori_loop(0, nch_ref[i], issue, 0)
    lax.fori_loop(0, nch_ref[i], drain, 0)

    ys16 = loc[...].astype(BF16)
    ci = lax.broadcasted_iota(jnp.int32, (tm, LANES), 1)
    picked = []
    for k in range(2):
        dest = dt_ref[:, k * LANES:(k + 1) * LANES].astype(jnp.int32)
        sel = jnp.concatenate([(dest == ci + m * LANES) for m in range(rows // LANES)], axis=1).astype(BF16)
        picked.append(_dot(sel, ys16) * _tile_lanes(wt_ref[:, k * LANES:(k + 1) * LANES], D_MODEL))
    h = h_ref[...] + (picked[0] + picked[1])
    a16 = _rms(h, gn_ref[...]).astype(BF16)
    gate = _sigmoid(_dot(a16, wg_ref[...]) + bg_ref[...])
    pp = _dot(p_ref[...].astype(BF16), wp_ref[...])
    h = h + gate * pp
    o_ref[...] = _rms(h, fn_ref[...])


def _ple(chunk_dst, n_chunks, h1, dest_t, gate_t, p, g_ple, wg16, bg, wp16, g_final, ys, tm):
    t = h1.shape[0]
    row = lambda i, cd, nc: (i, 0)
    fixed = lambda i, cd, nc: (0, 0)
    return pl.pallas_call(
        _ple_body,
        grid_spec=pltpu.PrefetchScalarGridSpec(
            num_scalar_prefetch=2,
            grid=(t // tm,),
            in_specs=[pl.BlockSpec((tm, D_MODEL), row), pl.BlockSpec((tm, 2 * LANES), row),
                      pl.BlockSpec((tm, 2 * LANES), row), pl.BlockSpec((tm, PLE_DIM), row),
                      pl.BlockSpec((1, D_MODEL), fixed), pl.BlockSpec((D_MODEL, D_MODEL), fixed),
                      pl.BlockSpec((1, D_MODEL), fixed), pl.BlockSpec((PLE_DIM, D_MODEL), fixed),
                      pl.BlockSpec((1, D_MODEL), fixed), pl.BlockSpec(memory_space=pl.ANY)],
            out_specs=pl.BlockSpec((tm, D_MODEL), row),
            scratch_shapes=[pltpu.VMEM((_sorted_rows_per_tile(tm), D_MODEL), F32), pltpu.SemaphoreType.DMA(())]),
        out_shape=jax.ShapeDtypeStruct((t, D_MODEL), F32),
        compiler_params=_cparams("arbitrary"),
        name="ple_final",
    )(chunk_dst, n_chunks, h1, dest_t, gate_t, p, g_ple, wg16, bg, wp16, g_final, ys)


MIX_ROWS = 512
SSD_Q = 128
POOL_HDR = 128
CONV_HDR = 8


def _softplus(x):
    return jnp.maximum(x, 0.0) + jnp.log1p(jnp.exp(-jnp.abs(x)))


def _head_expand_matrix():
    r = lax.broadcasted_iota(jnp.int32, (LANES, SSD_WIDTH), 0)
    c = lax.broadcasted_iota(jnp.int32, (LANES, SSD_WIDTH), 1)
    return (lax.shift_right_logical(c, 6) == r).astype(BF16)


def _state_block_mask():
    r = lax.broadcasted_iota(jnp.int32, (SSD_BC, SSD_WIDTH), 0)
    c = lax.broadcasted_iota(jnp.int32, (SSD_BC, SSD_WIDTH), 1)
    return (lax.shift_right_logical(r, 6) == lax.shift_right_logical(c, 8)).astype(F32)


def _mix_body(vp_ref, z_ref, xbc_ref, dt_ref, cw_ref, cb_ref, dtb_ref, alog_ref, dsk_ref, nw_ref,
              plw_ref, plb_ref, psc_ref, mix_ref, st_ref, pool_ext, conv_ext, s_ref):
    c = pl.program_id(1)
    rows = MIX_ROWS
    q_len = SSD_Q

    @pl.when(c == 0)
    def _():
        pool_ext[0:POOL_HDR, :] = jnp.zeros((POOL_HDR, POOL_WIDTH), F32)
        conv_ext[0:CONV_HDR, :] = jnp.zeros((CONV_HDR, CONV_DIM), F32)
        s_ref[...] = jnp.zeros_like(s_ref)

    @pl.when(c > 0)
    def _():
        pool_ext[0:POOL_HDR, :] = pool_ext[rows:rows + POOL_HDR, :]
        conv_ext[0:CONV_HDR, :] = conv_ext[rows:rows + CONV_HDR, :]

    pool_ext[POOL_HDR:POOL_HDR + rows, :] = vp_ref[...]
    conv_ext[CONV_HDR:CONV_HDR + rows, :] = xbc_ref[...]

    li = lax.broadcasted_iota(jnp.int32, (q_len, q_len), 0)
    si = lax.broadcasted_iota(jnp.int32, (q_len, q_len), 1)
    causal = li >= si
    tri = causal.astype(BF16)
    expand = _head_expand_matrix()
    blockmask = _state_block_mask()
    lane = lax.broadcasted_iota(jnp.int32, (q_len, LANES), 1)
    left = lane < SSD_HEAD_DIM
    a_neg = -jnp.exp(alog_ref[...])
    wl = lax.broadcasted_iota(jnp.int32, (q_len, 2 * q_len), 0)
    wj = lax.broadcasted_iota(jnp.int32, (q_len, 2 * q_len), 1)
    rowi = lax.broadcasted_iota(jnp.int32, (q_len, LANES), 0)

    for q in range(rows // q_len):
        r0 = q * q_len
        base = CONV_HDR + r0
        conv = cb_ref[...] + conv_ext[base - 3:base - 3 + q_len, :] * cw_ref[0:1, :]
        conv = conv + conv_ext[base - 2:base - 2 + q_len, :] * cw_ref[1:2, :]
        conv = conv + conv_ext[base - 1:base - 1 + q_len, :] * cw_ref[2:3, :]
        conv = conv + conv_ext[base:base + q_len, :] * cw_ref[3:4, :]
        conv = _silu(conv)
        xs = conv[:, 0:SSD_WIDTH]
        b16 = conv[:, SSD_WIDTH:SSD_WIDTH + SSD_BC].astype(BF16)
        c_all = conv[:, SSD_WIDTH + SSD_BC:CONV_DIM]
        c16 = c_all.astype(BF16)
        dt = _softplus(dt_ref[r0:r0 + q_len, :] + dtb_ref[...])
        acs = _dot_exact_lhs(tri, dt * a_neg)
        acs_t = acs.T
        dtx = _dot_exact_rhs(dt, expand, terms=2)
        acsx = _dot_exact_rhs(acs, expand)
        last_x = acsx[q_len - 1:q_len, :]
        xdt = xs * dtx
        xdt16 = xdt.astype(BF16)
        s_old = s_ref[...]
        y_off = _dot(c16, s_old.astype(BF16)) * jnp.exp(acsx)
        contrib = _dot_tn(b16, (xdt * jnp.exp(last_x - acsx)).astype(BF16)) * blockmask
        s_ref[...] = s_old * jnp.exp(last_x) + contrib
        cb = []
        for g in range(SSD_GROUPS):
            cg = jnp.where(lax.shift_right_logical(lane, 6) == g, c_all, 0.0).astype(BF16)
            cb.append(_dot_nt(cg, b16))
        z = z_ref[r0:r0 + q_len, :]
        y_blocks = []
        for j in range(SSD_HEADS // 2):
            blk = slice(j * LANES, (j + 1) * LANES)
            zs = []
            for h in (2 * j, 2 * j + 1):
                seg = acs[:, h:h + 1] - acs_t[h:h + 1, :]
                decay = jnp.where(causal, jnp.exp(seg), 0.0)
                scores = (cb[h // (SSD_HEADS // SSD_GROUPS)] * decay).astype(BF16)
                zs.append(_dot(scores, xdt16[:, blk]))
            y = jnp.where(left, zs[0], zs[1]) + y_off[:, blk]
            y = y + dsk_ref[:, blk] * xs[:, blk]
            y_blocks.append(y * _silu(z[:, blk]))
        for g in range(SSD_GROUPS):
            y0, y1 = y_blocks[2 * g], y_blocks[2 * g + 1]
            ss = jnp.sum(y0 * y0, axis=-1, keepdims=True) + jnp.sum(y1 * y1, axis=-1, keepdims=True)
            rs = lax.rsqrt(ss * (1.0 / (2 * LANES)) + EPS)
            for k, yk in ((2 * g, y0), (2 * g + 1, y1)):
                blk = slice(k * LANES, (k + 1) * LANES)
                out = yk * rs * nw_ref[:, blk]
                mix_ref[r0:r0 + q_len, POOL_WIDTH + k * LANES:POOL_WIDTH + (k + 1) * LANES] = out.astype(BF16)
        pos = c * rows + r0 + rowi
        for g, w in enumerate(POOL_WINDOWS):
            blk = slice(g * POOL_GROUP_DIM, (g + 1) * POOL_GROUP_DIM)
            pe = pool_ext[r0:r0 + 2 * q_len, blk]
            band = ((wj <= wl + POOL_HDR) & (wj > wl + POOL_HDR - w)).astype(BF16)
            winsum = _dot_exact_lhs(band, pe, terms=2)
            cnt = jnp.minimum(pos + 1, w).astype(F32)
            m = winsum / cnt - pe[q_len:2 * q_len, :]
            yg = _dot(m.astype(BF16), plw_ref[g]) + plb_ref[:, blk]
            mix_ref[r0:r0 + q_len, blk] = (yg * psc_ref[:, blk]).astype(BF16)

    @pl.when(c == pl.num_programs(1) - 1)
    def _():
        st_ref[0] = s_ref[...]


def _mix_prompt(vp, z, xbc, dt, cw, cb, dtb, alog, dsk, nw, plw16, plb, psc, nb, seq):
    steps = seq // MIX_ROWS
    row = lambda b, c: (b * steps + c, 0)
    fixed2 = lambda b, c: (0, 0)
    return pl.pallas_call(
        _mix_body,
        grid=(nb, steps),
        in_specs=[pl.BlockSpec((MIX_ROWS, POOL_WIDTH), row), pl.BlockSpec((MIX_ROWS, SSD_WIDTH), row),
                  pl.BlockSpec((MIX_ROWS, CONV_DIM), row), pl.BlockSpec((MIX_ROWS, LANES), row),
                  pl.BlockSpec((SSD_CONV, CONV_DIM), fixed2), pl.BlockSpec((1, CONV_DIM), fixed2),
                  pl.BlockSpec((1, LANES), fixed2), pl.BlockSpec((1, LANES), fixed2),
                  pl.BlockSpec((1, SSD_WIDTH), fixed2), pl.BlockSpec((1, SSD_WIDTH), fixed2),
                  pl.BlockSpec((len(POOL_WINDOWS), POOL_GROUP_DIM, POOL_GROUP_DIM), lambda b, c: (0, 0, 0)),
                  pl.BlockSpec((1, POOL_WIDTH), fixed2), pl.BlockSpec((1, POOL_WIDTH), fixed2)],
        out_specs=[pl.BlockSpec((MIX_ROWS, D_MODEL), row),
                   pl.BlockSpec((1, SSD_BC, SSD_WIDTH), lambda b, c: (b, 0, 0))],
        out_shape=[jax.ShapeDtypeStruct((nb * seq, D_MODEL), BF16),
                   jax.ShapeDtypeStruct((nb, SSD_BC, SSD_WIDTH), F32)],
        scratch_shapes=[pltpu.VMEM((POOL_HDR + MIX_ROWS, POOL_WIDTH), F32),
                        pltpu.VMEM((CONV_HDR + MIX_ROWS, CONV_DIM), F32),
                        pltpu.VMEM((SSD_BC, SSD_WIDTH), F32)],
        compiler_params=_cparams("parallel", "arbitrary"),
        name="mix_prompt",
    )(vp, z, xbc, dt, cw, cb, dtb, alog, dsk, nw, plw16, plb, psc)


STEP_SEQS = 16


def _mix_step_body(vp_ref, z_ref, xbc_ref, dt_ref, sp_ref, sc_ref, st_ref, cw_ref, cb_ref, dtb_ref, alog_ref,
                   dsk_ref, nw_ref, plw_ref, plb_ref, psc_ref, mix_ref, so_ref,
                   xdt_t, dec_t, bc_rows, xs_keep, y_t):
    i = pl.program_id(0)
    n_seq = vp_ref.shape[0]

    @pl.when(i == 0)
    def _():
        conv = cb_ref[...] + sc_ref[:, 0, :] * cw_ref[0:1, :]
        conv = conv + sc_ref[:, 1, :] * cw_ref[1:2, :]
        conv = conv + sc_ref[:, 2, :] * cw_ref[2:3, :]
        conv = conv + xbc_ref[...] * cw_ref[3:4, :]
        conv = _silu(conv)
        xs = conv[:, 0:SSD_WIDTH]
        b_all = conv[:, SSD_WIDTH:SSD_WIDTH + SSD_BC]
        c_all = conv[:, SSD_WIDTH + SSD_BC:CONV_DIM]
        expand = _head_expand_matrix()
        dt = _softplus(dt_ref[...] + dtb_ref[...])
        d_a = dt * (-jnp.exp(alog_ref[...]))
        dtx = _dot_exact_rhs(dt, expand, terms=2)
        decx = jnp.exp(_dot_exact_rhs(d_a, expand))
        xdt_t[...] = (xs * dtx).T
        dec_t[...] = decx.T
        bc_rows[0] = b_all
        bc_rows[1] = pltpu.roll(b_all, SSD_STATE, 1)
        bc_rows[2] = c_all
        bc_rows[3] = pltpu.roll(c_all, SSD_STATE, 1)
        xs_keep[...] = xs
        y_t[...] = jnp.zeros_like(y_t)
        v = vp_ref[...]
        for g, w in enumerate(POOL_WINDOWS):
            blk = slice(g * POOL_GROUP_DIM, (g + 1) * POOL_GROUP_DIM)
            acc = sp_ref[:, POOL_BUF - (w - 1), blk]
            for k in range(w - 2, 0, -1):
                acc = acc + sp_ref[:, POOL_BUF - k, blk]
            acc = acc + v[:, blk]
            m = acc / float(min(PAST_LEN + 1, w)) - v[:, blk]
            yg = _dot(m.astype(BF16), plw_ref[g]) + plb_ref[:, blk]
            mix_ref[:, blk] = (yg * psc_ref[:, blk]).astype(BF16)

    lane = lax.broadcasted_iota(jnp.int32, (1, LANES), 1)
    top = lax.broadcasted_iota(jnp.int32, (SSD_WIDTH, SSD_STATE), 0) < SSD_WIDTH // SSD_GROUPS

    def one_seq(bb, carry):
        b = i * STEP_SEQS + bb
        pick = lane == b
        xcol = jnp.sum(jnp.where(pick, xdt_t[...], 0.0), axis=1, keepdims=True)
        dcol = jnp.sum(jnp.where(pick, dec_t[...], 0.0), axis=1, keepdims=True)
        rows = [bc_rows[k, pl.ds(b, 1), 0:SSD_STATE] for k in range(4)]
        bsel = jnp.where(top, rows[0], rows[1])
        csel = jnp.where(top, rows[2], rows[3])
        s_new = st_ref[bb].reshape(SSD_WIDTH, SSD_STATE) * dcol + xcol * bsel
        so_ref[bb] = s_new.reshape(SSD_HEADS, SSD_HEAD_DIM, SSD_STATE)
        ycol = jnp.sum(s_new * csel, axis=1, keepdims=True)
        y_t[...] = jnp.where(pick, ycol, y_t[...])
        return carry

    lax.fori_loop(0, STEP_SEQS, one_seq, 0)

    @pl.when(i == pl.num_programs(0) - 1)
    def _():
        xs = xs_keep[...]
        y = y_t[...].T + dsk_ref[...] * xs
        y = y * _silu(z_ref[...])
        width = SSD_WIDTH // SSD_GROUPS
        for g in range(SSD_GROUPS):
            blk = slice(g * width, (g + 1) * width)
            yg = y[:, blk]
            rs = lax.rsqrt(jnp.mean(yg * yg, axis=-1, keepdims=True) + EPS)
            mix_ref[:, POOL_WIDTH + g * width:POOL_WIDTH + (g + 1) * width] = (yg * rs * nw_ref[:, blk]).astype(BF16)


def _mix_step(vp, z, xbc, dt, state_pool, state_conv, state_ssm, cw, cb, dtb, alog, dsk, nw, plw16, plb, psc):
    n = vp.shape[0]
    fixed2 = lambda i: (0, 0)
    fixed3 = lambda i: (0, 0, 0)
    st_spec = pl.BlockSpec((STEP_SEQS, SSD_HEADS, SSD_HEAD_DIM, SSD_STATE), lambda i: (i, 0, 0, 0))
    return pl.pallas_call(
        _mix_step_body,
        grid=(n // STEP_SEQS,),
        in_specs=[pl.BlockSpec((n, POOL_WIDTH), fixed2), pl.BlockSpec((n, SSD_WIDTH), fixed2),
                  pl.BlockSpec((n, CONV_DIM), fixed2), pl.BlockSpec((n, LANES), fixed2),
                  pl.BlockSpec((n, POOL_BUF, POOL_WIDTH), fixed3),
                  pl.BlockSpec((n, SSD_CONV - 1, CONV_DIM), fixed3),
                  st_spec,
                  pl.BlockSpec((SSD_CONV, CONV_DIM), fixed2), pl.BlockSpec((1, CONV_DIM), fixed2),
                  pl.BlockSpec((1, LANES), fixed2), pl.BlockSpec((1, LANES), fixed2),
                  pl.BlockSpec((1, SSD_WIDTH), fixed2), pl.BlockSpec((1, SSD_WIDTH), fixed2),
                  pl.BlockSpec((len(POOL_WINDOWS), POOL_GROUP_DIM, POOL_GROUP_DIM), fixed3),
                  pl.BlockSpec((1, POOL_WIDTH), fixed2), pl.BlockSpec((1, POOL_WIDTH), fixed2)],
        out_specs=[pl.BlockSpec((n, D_MODEL), fixed2), st_spec],
        out_shape=[jax.ShapeDtypeStruct((n, D_MODEL), BF16),
                   jax.ShapeDtypeStruct(state_ssm.shape, F32)],
        scratch_shapes=[pltpu.VMEM((SSD_WIDTH, LANES), F32), pltpu.VMEM((SSD_WIDTH, LANES), F32),
                        pltpu.VMEM((4, n, LANES), F32), pltpu.VMEM((n, SSD_WIDTH), F32),
                        pltpu.VMEM((SSD_WIDTH, LANES), F32)],
        compiler_params=_cparams("arbitrary"),
        name="mix_step",
    )(vp, z, xbc, dt, state_pool, state_conv, state_ssm, cw, cb, dtb, alog, dsk, nw, plw16, plb, psc)


PROMPT_TILE = 512


def _sort_tables(counts, tile_tokens):
    cnt = jnp.concatenate(counts, axis=0)
    pc = (cnt + RUN_PAD - 1) // RUN_PAD * RUN_PAD
    off_local = jnp.cumsum(pc, axis=1) - pc
    tile_rows = jnp.sum(pc, axis=1)
    region = (jnp.sum(pc, axis=0) + MOE_TILE - 1) // MOE_TILE * MOE_TILE
    base = jnp.cumsum(region) - region
    dst = base[None, :] + jnp.cumsum(pc, axis=0) - pc
    per_pass = []
    lo = 0
    for c, tm in zip(counts, tile_tokens):
        hi = lo + c.shape[0]
        n_chunk = _sorted_rows_per_tile(tm) // RUN_PAD
        s = jnp.arange(n_chunk, dtype=jnp.int32) * RUN_PAD
        ends = off_local[lo:hi] + pc[lo:hi]
        k = jnp.minimum(jnp.sum((ends[:, None, :] <= s[None, :, None]).astype(jnp.int32), axis=2), N_EXPERTS - 1)
        d = jnp.take_along_axis(dst[lo:hi], k, axis=1) + s[None, :] - jnp.take_along_axis(off_local[lo:hi], k, axis=1)
        chunk_dst = jnp.where(s[None, :] < tile_rows[lo:hi, None], d, 0).astype(jnp.int32)
        off_v = jnp.broadcast_to(off_local[lo:hi, :, None].astype(F32), (hi - lo, N_EXPERTS, LANES))
        per_pass.append((chunk_dst, (tile_rows[lo:hi] // RUN_PAD).astype(jnp.int32), off_v))
        lo = hi
    tiles_cum = jnp.cumsum(region // MOE_TILE)
    n_active = tiles_cum[-1]
    return per_pass, tiles_cum, n_active


def _max_sorted_rows(tile_counts, tile_tokens):
    rows = sum(n * (2 * tm + N_EXPERTS * (RUN_PAD - 1)) for n, tm in zip(tile_counts, tile_tokens))
    return (-(-rows // MOE_TILE) + N_EXPERTS) * MOE_TILE


def _row(v):
    return v.reshape(1, -1).astype(F32)


def _pad_lanes(v):
    return jnp.pad(v.reshape(1, -1).astype(F32), ((0, 0), (0, LANES - v.size)))


def _state_from_blocks(st):
    n = st.shape[0]
    hpg = SSD_HEADS // SSD_GROUPS
    s6 = st.reshape(n, SSD_GROUPS, SSD_STATE, SSD_GROUPS, hpg, SSD_HEAD_DIM)
    per_group = [s6[:, g, :, g] for g in range(SSD_GROUPS)]
    s = jnp.stack(per_group, axis=1)
    return jnp.transpose(s, (0, 1, 3, 4, 2)).reshape(n, SSD_HEADS, SSD_HEAD_DIM, SSD_STATE)


def kernel(x_prompt, x_sample, p_prompt, p_sample, state_pool, state_conv, state_ssm, norm_mix, w_in, pool_lin_w, pool_lin_b, pool_scale, conv_w, conv_b, dt_bias, a_log, d_skip, ssd_norm, w_out, norm_ffn, router_grp_w, router_grp_b, router_exp_w, router_exp_b, exp_w_gate, exp_w_up, exp_w_down, norm_ple, ple_gate_w, ple_gate_b, ple_proj_w, norm_final):
    nb, seq, _ = x_prompt.shape
    ns = x_sample.shape[0]
    assert ns == LANES and x_sample.shape[1] == 1 and seq % MIX_ROWS == 0 and seq >= POOL_BUF

    w_in16 = jnp.pad(w_in[0], ((0, 0), (0, IN_COLS - w_in.shape[2]))).astype(BF16)
    w_out16 = w_out[0].astype(BF16)
    g_mix, g_ffn, g_ple, g_fin = _row(norm_mix[0]), _row(norm_ffn[0]), _row(norm_ple[0]), _row(norm_final)
    cw, cb = conv_w[0].astype(F32), _row(conv_b[0])
    dtb, alog = _pad_lanes(dt_bias[0]), _pad_lanes(a_log[0])
    dsk = _row(jnp.repeat(d_skip[0], SSD_HEAD_DIM))
    nw = _row(ssd_norm[0])
    plw16 = pool_lin_w[0].astype(BF16)
    plb, psc = _row(pool_lin_b[0]), _row(pool_scale[0])
    zeros4 = jnp.zeros((D_MODEL, 8 - N_EXPERT_GROUPS), F32)
    r_w = jnp.concatenate([router_grp_w[0], zeros4, router_exp_w[0],
                           jnp.zeros((D_MODEL, LANES - 8 - N_EXPERTS), F32)], axis=1).T
    r_hi = r_w.astype(BF16)
    r_lo = (r_w - r_hi.astype(F32)).astype(BF16)
    r_b = jnp.concatenate([router_grp_b[0], jnp.zeros((8 - N_EXPERT_GROUPS,), F32), router_exp_b[0],
                           jnp.zeros((LANES - 8 - N_EXPERTS,), F32)])
    wg = exp_w_gate[0].reshape(N_EXPERTS, D_MODEL, EXPERT_FF)
    wu = exp_w_up[0].reshape(N_EXPERTS, D_MODEL, EXPERT_FF)
    wd = exp_w_down[0].reshape(N_EXPERTS, EXPERT_FF, D_MODEL)
    pg16 = ple_gate_w[0].astype(BF16)
    pgb = _row(ple_gate_b[0])
    pp16 = ple_proj_w[0].astype(BF16)

    def route(x, mix16, tm):
        r_bias = jnp.broadcast_to(r_b[:, None], (LANES, tm))
        return _out_proj(x, mix16, w_out16, g_ffn, r_hi, r_lo, r_bias, tm)

    xp = x_prompt.reshape(nb * seq, D_MODEL)
    vp, z, xbc, dt = _in_proj(xp, g_mix, w_in16, PROMPT_TILE)
    mix16, st = _mix_prompt(vp, z, xbc, dt, cw, cb, dtb, alog, dsk, nw, plw16, plb, psc, nb, seq)
    h1_p, u_p, rt_p, gate_p, cnt_p = route(xp, mix16, PROMPT_TILE)
    pool_p = vp.reshape(nb, seq, POOL_WIDTH)[:, seq - POOL_BUF:]
    conv_p = xbc.reshape(nb, seq, CONV_DIM)[:, seq - (SSD_CONV - 1):]
    ssm_p = _state_from_blocks(st)

    xs_ = x_sample.reshape(ns, D_MODEL)
    vp_s, z_s, xbc_s, dt_s = _in_proj(xs_, g_mix, w_in16, ns)
    mix_s, ssm_s = _mix_step(vp_s, z_s, xbc_s, dt_s, state_pool[0], state_conv[0], state_ssm[0],
                             cw, cb, dtb, alog, dsk, nw, plw16, plb, psc)
    h1_s, u_s, rt_s, gate_s, cnt_s = route(xs_, mix_s, ns)
    pool_s = jnp.concatenate([state_pool[0][:, 1:], vp_s[:, None, :]], axis=1)
    conv_s = jnp.concatenate([state_conv[0][:, 1:], xbc_s[:, None, :]], axis=1)

    counts = [cnt_p[:, :, 0].astype(jnp.int32), cnt_s[:, :, 0].astype(jnp.int32)]
    tiles = (PROMPT_TILE, ns)
    (tab_p, tab_s), tiles_cum, n_active = _sort_tables(counts, tiles)
    total_rows = _max_sorted_rows([c.shape[0] for c in counts], tiles)
    xs_sorted, dest_p = _sort_tokens(tab_p[0], tab_p[1], u_p, rt_p, tab_p[2], None, PROMPT_TILE, total_rows)
    xs_sorted, dest_s = _sort_tokens(tab_s[0], tab_s[1], u_s, rt_s, tab_s[2], xs_sorted, ns, total_rows)
    jj = jnp.minimum(jnp.arange(total_rows // MOE_TILE, dtype=jnp.int32), n_active - 1)
    tile_expert = jnp.minimum(jnp.sum((tiles_cum[None, :] <= jj[:, None]).astype(jnp.int32), axis=1), N_EXPERTS - 1)
    ys_sorted = _moe_sorted(tile_expert, n_active.reshape(1).astype(jnp.int32), xs_sorted, wg, wu, wd)

    y_prompt = _ple(tab_p[0], tab_p[1], h1_p, dest_p, gate_p, p_prompt[0].reshape(nb * seq, PLE_DIM),
                    g_ple, pg16, pgb, pp16, g_fin, ys_sorted, PROMPT_TILE)
    y_sample = _ple(tab_s[0], tab_s[1], h1_s, dest_s, gate_s, p_sample[0].reshape(ns, PLE_DIM),
                    g_ple, pg16, pgb, pp16, g_fin, ys_sorted, ns)

    return (y_prompt.reshape(nb, seq, D_MODEL), y_sample.reshape(ns, 1, D_MODEL),
            pool_p[None], conv_p[None], ssm_p[None], pool_s[None], conv_s[None], ssm_s[None])
```

```python
import functools

import jax
import jax.numpy as jnp
from jax import lax
from jax.experimental import pallas as pl
from jax.experimental.pallas import tpu as pltpu

F32 = jnp.float32
BF16 = jnp.bfloat16

D_MODEL = 1024
POOL_WIDTH = 512
POOL_WINDOWS = (2, 4, 8, 16)
POOL_GROUP_DIM = 128
POOL_BUF = 15
SSD_WIDTH = 512
SSD_HEAD_DIM = 64
SSD_HEADS = 8
SSD_GROUPS = 2
SSD_STATE = 64
SSD_CONV = 4
SSD_BC = SSD_GROUPS * SSD_STATE
CONV_DIM = SSD_WIDTH + 2 * SSD_BC
N_EXPERT_GROUPS = 4
EXPERTS_PER_GROUP = 8
N_EXPERTS = N_EXPERT_GROUPS * EXPERTS_PER_GROUP
EXPERT_FF = 256
PLE_DIM = 256
PAST_LEN = 16384
EPS = 1e-6

LANES = 128
IN_COLS = 1920
DT_OFF = POOL_WIDTH + SSD_WIDTH + CONV_DIM
VMEM_LIMIT = 56 * 1024 * 1024


def _cparams(*sem):
    return pltpu.CompilerParams(dimension_semantics=sem, vmem_limit_bytes=VMEM_LIMIT)


def _rms(x, g):
    return x * lax.rsqrt(jnp.mean(x * x, axis=-1, keepdims=True) + EPS) * g


def _sigmoid(x):
    return 1.0 / (1.0 + jnp.exp(-x))


def _silu(x):
    return x * _sigmoid(x)


def _split3(v):
    hi = v.astype(BF16)
    r = v - hi.astype(F32)
    mid = r.astype(BF16)
    lo = (r - mid.astype(F32)).astype(BF16)
    return hi, mid, lo


def _dot(a, b):
    return jnp.dot(a, b, preferred_element_type=F32)


def _dot_nt(a, b):
    return lax.dot_general(a, b, (((1,), (1,)), ((), ())), preferred_element_type=F32)


def _dot_tn(a, b):
    return lax.dot_general(a, b, (((0,), (0,)), ((), ())), preferred_element_type=F32)


def _dot_exact_lhs(sel, v, terms=3):
    acc = None
    for t in _split3(v)[:terms]:
        p = _dot(sel, t)
        acc = p if acc is None else acc + p
    return acc


def _dot_exact_rhs(v, sel, terms=3):
    acc = None
    for t in _split3(v)[:terms]:
        p = _dot(t, sel)
        acc = p if acc is None else acc + p
    return acc


def _in_proj_body(x_ref, g_ref, w_ref, vp_ref, z_ref, xbc_ref, dt_ref):
    a16 = _rms(x_ref[...], g_ref[...]).astype(BF16)
    vp_ref[...] = _dot(a16, w_ref[:, 0:POOL_WIDTH])
    z_ref[...] = _dot(a16, w_ref[:, POOL_WIDTH:POOL_WIDTH + SSD_WIDTH])
    xbc_ref[...] = _dot(a16, w_ref[:, POOL_WIDTH + SSD_WIDTH:DT_OFF])
    dt_ref[...] = _dot(a16, w_ref[:, DT_OFF:IN_COLS])


def _in_proj(x, g, w16, tm):
    t = x.shape[0]
    row = lambda i: (i, 0)
    fixed = lambda i: (0, 0)
    return pl.pallas_call(
        _in_proj_body,
        grid=(t // tm,),
        in_specs=[pl.BlockSpec((tm, D_MODEL), row), pl.BlockSpec((1, D_MODEL), fixed),
                  pl.BlockSpec((D_MODEL, IN_COLS), fixed)],
        out_specs=[pl.BlockSpec((tm, POOL_WIDTH), row), pl.BlockSpec((tm, SSD_WIDTH), row),
                   pl.BlockSpec((tm, CONV_DIM), row), pl.BlockSpec((tm, LANES), row)],
        out_shape=[jax.ShapeDtypeStruct((t, POOL_WIDTH), F32), jax.ShapeDtypeStruct((t, SSD_WIDTH), F32),
                   jax.ShapeDtypeStruct((t, CONV_DIM), F32), jax.ShapeDtypeStruct((t, LANES), F32)],
        compiler_params=_cparams("parallel"),
        name="in_proj",
    )(x, g, w16)


def _route(lg):
    tm = lg.shape[1]
    gl = lg[0:N_EXPERT_GROUPS, :]
    gmax = jnp.max(gl, axis=0, keepdims=True)
    gsum = jnp.sum(jnp.exp(gl - gmax), axis=0, keepdims=True)
    g_w = 1.0 / gsum
    gi = lax.broadcasted_iota(jnp.int32, gl.shape, 0)
    g_idx = jnp.min(jnp.where(gl == gmax, gi, N_EXPERT_GROUPS), axis=0, keepdims=True)
    sel = jnp.zeros((EXPERTS_PER_GROUP, tm), F32)
    for g in range(N_EXPERT_GROUPS):
        blk = lg[8 + g * EXPERTS_PER_GROUP:8 + (g + 1) * EXPERTS_PER_GROUP, :]
        sel = jnp.where(g_idx == g, blk, sel)
    ei = lax.broadcasted_iota(jnp.int32, sel.shape, 0)
    m1 = jnp.max(sel, axis=0, keepdims=True)
    i1 = jnp.min(jnp.where(sel == m1, ei, EXPERTS_PER_GROUP), axis=0, keepdims=True)
    rest = jnp.where(ei == i1, -jnp.inf, sel)
    m2 = jnp.max(rest, axis=0, keepdims=True)
    i2 = jnp.min(jnp.where(rest == m2, ei, EXPERTS_PER_GROUP), axis=0, keepdims=True)
    p2 = jnp.exp(m2 - m1)
    w1 = g_w / (1.0 + p2)
    w2 = g_w * p2 / (1.0 + p2)
    return g_idx * EXPERTS_PER_GROUP + i1, g_idx * EXPERTS_PER_GROUP + i2, w1, w2


def _out_proj_body(x_ref, mix_ref, w_ref, g_ref, rh_ref, rl_ref, rb_ref, h_ref, u_ref, rt_ref, wt_ref, cnt_ref):
    h = x_ref[...] + _dot(mix_ref[...], w_ref[...])
    h_ref[...] = h
    u = _rms(h, g_ref[...])
    u_hi = u.astype(BF16)
    u_ref[...] = u_hi
    u_lo = (u - u_hi.astype(F32)).astype(BF16)
    lg = _dot_nt(rh_ref[...], u_hi) + _dot_nt(rh_ref[...], u_lo) + _dot_nt(rl_ref[...], u_hi) + rb_ref[...]
    b1, b2, w1, w2 = _route(lg)
    tm = lg.shape[1]
    r8 = lax.broadcasted_iota(jnp.int32, (8, tm), 0)
    rt_ref[0] = jnp.where(r8 == 0, b1.astype(F32), jnp.where(r8 == 1, b2.astype(F32), 0.0))
    wt_ref[...] = jnp.concatenate([jnp.broadcast_to(w1, (LANES, tm)).T, jnp.broadcast_to(w2, (LANES, tm)).T], axis=1)
    kio = lax.broadcasted_iota(jnp.int32, (N_EXPERTS, tm), 0)
    hits = ((kio == b1) | (kio == b2)).astype(F32)
    cnt_ref[0] = jnp.broadcast_to(jnp.sum(hits, axis=1, keepdims=True), (N_EXPERTS, LANES))


def _out_proj(x, mix16, w16, g, r_hi, r_lo, r_bias, tm):
    t = x.shape[0]
    nt = t // tm
    row = lambda i: (i, 0)
    fixed = lambda i: (0, 0)
    tile3 = lambda i: (i, 0, 0)
    return pl.pallas_call(
        _out_proj_body,
        grid=(nt,),
        in_specs=[pl.BlockSpec((tm, D_MODEL), row), pl.BlockSpec((tm, D_MODEL), row),
                  pl.BlockSpec((D_MODEL, D_MODEL), fixed), pl.BlockSpec((1, D_MODEL), fixed),
                  pl.BlockSpec((LANES, D_MODEL), fixed), pl.BlockSpec((LANES, D_MODEL), fixed),
                  pl.BlockSpec((LANES, tm), fixed)],
        out_specs=[pl.BlockSpec((tm, D_MODEL), row), pl.BlockSpec((tm, D_MODEL), row),
                   pl.BlockSpec((1, 8, tm), tile3), pl.BlockSpec((tm, 2 * LANES), row),
                   pl.BlockSpec((1, N_EXPERTS, LANES), tile3)],
        out_shape=[jax.ShapeDtypeStruct((t, D_MODEL), F32), jax.ShapeDtypeStruct((t, D_MODEL), BF16),
                   jax.ShapeDtypeStruct((nt, 8, tm), F32), jax.ShapeDtypeStruct((t, 2 * LANES), F32),
                   jax.ShapeDtypeStruct((nt, N_EXPERTS, LANES), F32)],
        compiler_params=_cparams("parallel"),
        name="out_proj_router",
    )(x, mix16, w16, g, r_hi, r_lo, r_bias)


RUN_PAD = 8
MOE_TILE = 512


def _tile_lanes(v, width):
    reps = width // LANES
    return v if reps == 1 else jnp.concatenate([v] * reps, axis=1)


def _sorted_rows_per_tile(tm):
    need = 2 * tm + N_EXPERTS * (RUN_PAD - 1)
    return -(-need // LANES) * LANES


def _sort_body(cd_ref, nch_ref, u_ref, rt_ref, off_ref, *rest):
    xs_ref, dt_ref, loc, sem = rest[-4:]
    i = pl.program_id(0)
    tm = u_ref.shape[0]
    rows = loc.shape[1]
    rt = rt_ref[0]
    b1 = rt[0:1, :].astype(jnp.int32)
    b2 = rt[1:2, :].astype(jnp.int32)
    kio = lax.broadcasted_iota(jnp.int32, (N_EXPERTS, tm), 0)
    o1 = kio == b1
    o2 = kio == b2
    before = (lax.broadcasted_iota(jnp.int32, (tm, tm), 0) < lax.broadcasted_iota(jnp.int32, (tm, tm), 1))
    start = _dot((o1 | o2).astype(BF16), before.astype(BF16)) + _tile_lanes(off_ref[0], tm)
    d1 = jnp.sum(jnp.where(o1, start, 0.0), axis=0, keepdims=True)
    d2 = jnp.sum(jnp.where(o2, start, 0.0), axis=0, keepdims=True)
    dt_ref[...] = jnp.concatenate([jnp.broadcast_to(d1, (LANES, tm)).T, jnp.broadcast_to(d2, (LANES, tm)).T], axis=1)
    rio = lax.broadcasted_iota(jnp.int32, (rows, tm), 0)
    perm = ((rio == d1.astype(jnp.int32)) | (rio == d2.astype(jnp.int32))).astype(BF16)
    slot = i % 2
    loc[slot] = _dot(perm, u_ref[...])

    def chunk_copy(step, sl, c):
        return pltpu.make_async_copy(
            loc.at[sl, pl.ds(pl.multiple_of(c * RUN_PAD, RUN_PAD), RUN_PAD), :],
            xs_ref.at[pl.ds(pl.multiple_of(cd_ref[step, c], RUN_PAD), RUN_PAD), :], sem.at[sl])

    def issue(c, carry):
        chunk_copy(i, slot, c).start()
        return carry

    lax.fori_loop(0, nch_ref[i], issue, 0)

    @pl.when(i > 0)
    def _():
        def drain(c, carry):
            chunk_copy(i - 1, 1 - slot, c).wait()
            return carry
        lax.fori_loop(0, nch_ref[i - 1], drain, 0)

    @pl.when(i == pl.num_programs(0) - 1)
    def _():
        def drain(c, carry):
            chunk_copy(i, slot, c).wait()
            return carry
        lax.fori_loop(0, nch_ref[i], drain, 0)


def _sort_tokens(chunk_dst, n_chunks, u16, rt, off_v, xs_prev, tm, total_rows):
    t = u16.shape[0]
    rows = _sorted_rows_per_tile(tm)
    in_specs = [pl.BlockSpec((tm, D_MODEL), lambda i, cd, nc: (i, 0)),
                pl.BlockSpec((1, 8, tm), lambda i, cd, nc: (i, 0, 0)),
                pl.BlockSpec((1, N_EXPERTS, LANES), lambda i, cd, nc: (i, 0, 0))]
    args = [chunk_dst, n_chunks, u16, rt, off_v]
    aliases = {}
    if xs_prev is not None:
        in_specs.append(pl.BlockSpec(memory_space=pl.ANY))
        aliases = {len(args): 0}
        args.append(xs_prev)
    return pl.pallas_call(
        _sort_body,
        grid_spec=pltpu.PrefetchScalarGridSpec(
            num_scalar_prefetch=2,
            grid=(t // tm,),
            in_specs=in_specs,
            out_specs=[pl.BlockSpec(memory_space=pl.ANY),
                       pl.BlockSpec((tm, 2 * LANES), lambda i, cd, nc: (i, 0))],
            scratch_shapes=[pltpu.VMEM((2, rows, D_MODEL), F32), pltpu.SemaphoreType.DMA((2,))]),
        out_shape=[jax.ShapeDtypeStruct((total_rows, D_MODEL), F32),
                   jax.ShapeDtypeStruct((t, 2 * LANES), F32)],
        input_output_aliases=aliases,
        compiler_params=_cparams("arbitrary"),
        name="sort_tokens",
    )(*args)


def _moe_body(te_ref, na_ref, x_ref, wg_ref, wu_ref, wd_ref, o_ref, wg16, wu16, wd16):
    j = pl.program_id(0)

    @pl.when(j < na_ref[0])
    def _():
        @pl.when((j == 0) | (te_ref[j] != te_ref[jnp.maximum(j - 1, 0)]))
        def _():
            wg16[...] = wg_ref[0].astype(BF16)
            wu16[...] = wu_ref[0].astype(BF16)
            wd16[...] = wd_ref[0].astype(BF16)

        x = x_ref[...].astype(BF16)
        act = _silu(_dot(x, wg16[...])) * _dot(x, wu16[...])
        o_ref[...] = _dot(act.astype(BF16), wd16[...])


def _moe_sorted(tile_expert, n_active, xs, wg, wu, wd):
    total_rows = xs.shape[0]
    n_tiles = total_rows // MOE_TILE
    x_map = lambda j, te, na: (jnp.minimum(j, na[0] - 1), 0)
    w_map = lambda j, te, na: (te[j], 0, 0)
    return pl.pallas_call(
        _moe_body,
        grid_spec=pltpu.PrefetchScalarGridSpec(
            num_scalar_prefetch=2,
            grid=(n_tiles,),
            in_specs=[pl.BlockSpec((MOE_TILE, D_MODEL), x_map),
                      pl.BlockSpec((1, D_MODEL, EXPERT_FF), w_map),
                      pl.BlockSpec((1, D_MODEL, EXPERT_FF), w_map),
                      pl.BlockSpec((1, EXPERT_FF, D_MODEL), w_map)],
            out_specs=pl.BlockSpec((MOE_TILE, D_MODEL), lambda j, te, na: (j, 0)),
            scratch_shapes=[pltpu.VMEM((D_MODEL, EXPERT_FF), BF16), pltpu.VMEM((D_MODEL, EXPERT_FF), BF16),
                            pltpu.VMEM((EXPERT_FF, D_MODEL), BF16)]),
        out_shape=jax.ShapeDtypeStruct((total_rows, D_MODEL), F32),
        compiler_params=_cparams("arbitrary"),
        name="moe_sorted",
    )(tile_expert, n_active, xs, wg, wu, wd)


def _ple_body(cd_ref, nch_ref, h_ref, dt_ref, wt_ref, p_ref, gn_ref, wg_ref, bg_ref, wp_ref, fn_ref, ys_ref,
              o_ref, loc, sem):
    i = pl.program_id(0)
    tm = h_ref.shape[0]
    rows = loc.shape[1]
    slot = i % 2

    def chunk_copy(step, sl, c):
        return pltpu.make_async_copy(
            ys_ref.at[pl.ds(pl.multiple_of(cd_ref[step, c], RUN_PAD), RUN_PAD), :],
            loc.at[sl, pl.ds(pl.multiple_of(c * RUN_PAD, RUN_PAD), RUN_PAD), :], sem.at[sl])

    def fetch(step, sl):
        def issue(c, carry):
            chunk_copy(step, sl, c).start()
            return carry
        lax.fori_loop(0, nch_ref[step], issue, 0)

    @pl.when(i == 0)
    def _():
        loc[...] = jnp.zeros_like(loc)
        fetch(0, 0)

    @pl.when(i + 1 < pl.num_programs(0))
    def _():
        fetch(i + 1, 1 - slot)

    def drain(c, carry):
        chunk_copy(i, slot, c).wait()
        return carry

    lax.fori_loop(0, nch_ref[i], drain, 0)

    ys16 = loc[slot].astype(BF16)
    ci = lax.broadcasted_iota(jnp.int32, (tm, LANES), 1)
    picked = []
    for k in range(2):
        dest = dt_ref[:, k * LANES:(k + 1) * LANES].astype(jnp.int32)
        sel = jnp.concatenate([(dest == ci + m * LANES) for m in range(rows // LANES)], axis=1).astype(BF16)
        picked.append(_dot(sel, ys16) * _tile_lanes(wt_ref[:, k * LANES:(k + 1) * LANES], D_MODEL))
    h = h_ref[...] + (picked[0] + picked[1])
    a16 = _rms(h, gn_ref[...]).astype(BF16)
    gate = _sigmoid(_dot(a16, wg_ref[...]) + bg_ref[...])
    pp = _dot(p_ref[...].astype(BF16), wp_ref[...])
    h = h + gate * pp
    o_ref[...] = _rms(h, fn_ref[...])


def _ple(chunk_dst, n_chunks, h1, dest_t, gate_t, p, g_ple, wg16, bg, wp16, g_final, ys, tm):
    t = h1.shape[0]
    row = lambda i, cd, nc: (i, 0)
    fixed = lambda i, cd, nc: (0, 0)
    return pl.pallas_call(
        _ple_body,
        grid_spec=pltpu.PrefetchScalarGridSpec(
            num_scalar_prefetch=2,
            grid=(t // tm,),
            in_specs=[pl.BlockSpec((tm, D_MODEL), row), pl.BlockSpec((tm, 2 * LANES), row),
                      pl.BlockSpec((tm, 2 * LANES), row), pl.BlockSpec((tm, PLE_DIM), row),
                      pl.BlockSpec((1, D_MODEL), fixed), pl.BlockSpec((D_MODEL, D_MODEL), fixed),
                      pl.BlockSpec((1, D_MODEL), fixed), pl.BlockSpec((PLE_DIM, D_MODEL), fixed),
                      pl.BlockSpec((1, D_MODEL), fixed), pl.BlockSpec(memory_space=pl.ANY)],
            out_specs=pl.BlockSpec((tm, D_MODEL), row),
            scratch_shapes=[pltpu.VMEM((2, _sorted_rows_per_tile(tm), D_MODEL), F32),
                            pltpu.SemaphoreType.DMA((2,))]),
        out_shape=jax.ShapeDtypeStruct((t, D_MODEL), F32),
        compiler_params=_cparams("arbitrary"),
        name="ple_final",
    )(chunk_dst, n_chunks, h1, dest_t, gate_t, p, g_ple, wg16, bg, wp16, g_final, ys)


MIX_ROWS = 512
SSD_Q = 128
POOL_HDR = 128
CONV_HDR = 8


def _softplus(x):
    return jnp.maximum(x, 0.0) + jnp.log1p(jnp.exp(-jnp.abs(x)))


def _head_expand_matrix():
    r = lax.broadcasted_iota(jnp.int32, (LANES, SSD_WIDTH), 0)
    c = lax.broadcasted_iota(jnp.int32, (LANES, SSD_WIDTH), 1)
    return (lax.shift_right_logical(c, 6) == r).astype(BF16)


def _state_block_mask():
    r = lax.broadcasted_iota(jnp.int32, (SSD_BC, SSD_WIDTH), 0)
    c = lax.broadcasted_iota(jnp.int32, (SSD_BC, SSD_WIDTH), 1)
    return (lax.shift_right_logical(r, 6) == lax.shift_right_logical(c, 8)).astype(F32)


def _mix_body(vp_ref, z_ref, xbc_ref, dt_ref, cw_ref, cb_ref, dtb_ref, alog_ref, dsk_ref, nw_ref,
              plw_ref, plb_ref, psc_ref, mix_ref, st_ref, pool_ext, conv_ext, s_ref):
    c = pl.program_id(1)
    rows = MIX_ROWS
    q_len = SSD_Q

    @pl.when(c == 0)
    def _():
        pool_ext[0:POOL_HDR, :] = jnp.zeros((POOL_HDR, POOL_WIDTH), F32)
        conv_ext[0:CONV_HDR, :] = jnp.zeros((CONV_HDR, CONV_DIM), F32)
        s_ref[...] = jnp.zeros_like(s_ref)

    @pl.when(c > 0)
    def _():
        pool_ext[0:POOL_HDR, :] = pool_ext[rows:rows + POOL_HDR, :]
        conv_ext[0:CONV_HDR, :] = conv_ext[rows:rows + CONV_HDR, :]

    pool_ext[POOL_HDR:POOL_HDR + rows, :] = vp_ref[...]
    conv_ext[CONV_HDR:CONV_HDR + rows, :] = xbc_ref[...]

    li = lax.broadcasted_iota(jnp.int32, (q_len, q_len), 0)
    si = lax.broadcasted_iota(jnp.int32, (q_len, q_len), 1)
    causal = li >= si
    tri = causal.astype(BF16)
    expand = _head_expand_matrix()
    blockmask = _state_block_mask()
    lane = lax.broadcasted_iota(jnp.int32, (q_len, LANES), 1)
    left = lane < SSD_HEAD_DIM
    a_neg = -jnp.exp(alog_ref[...])
    wl = lax.broadcasted_iota(jnp.int32, (q_len, 2 * q_len), 0)
    wj = lax.broadcasted_iota(jnp.int32, (q_len, 2 * q_len), 1)
    rowi = lax.broadcasted_iota(jnp.int32, (q_len, LANES), 0)

    for q in range(rows // q_len):
        r0 = q * q_len
        base = CONV_HDR + r0
        conv = cb_ref[...] + conv_ext[base - 3:base - 3 + q_len, :] * cw_ref[0:1, :]
        conv = conv + conv_ext[base - 2:base - 2 + q_len, :] * cw_ref[1:2, :]
        conv = conv + conv_ext[base - 1:base - 1 + q_len, :] * cw_ref[2:3, :]
        conv = conv + conv_ext[base:base + q_len, :] * cw_ref[3:4, :]
        conv = _silu(conv)
        xs = conv[:, 0:SSD_WIDTH]
        b16 = conv[:, SSD_WIDTH:SSD_WIDTH + SSD_BC].astype(BF16)
        c_all = conv[:, SSD_WIDTH + SSD_BC:CONV_DIM]
        c16 = c_all.astype(BF16)
        dt = _softplus(dt_ref[r0:r0 + q_len, :] + dtb_ref[...])
        acs = _dot_exact_lhs(tri, dt * a_neg)
        acs_t = acs.T
        dtx = _dot_exact_rhs(dt, expand, terms=2)
        acsx = _dot_exact_rhs(acs, expand)
        last_x = acsx[q_len - 1:q_len, :]
        xdt = xs * dtx
        xdt16 = xdt.astype(BF16)
        s_old = s_ref[...]
        y_off = _dot(c16, s_old.astype(BF16)) * jnp.exp(acsx)
        contrib = _dot_tn(b16, (xdt * jnp.exp(last_x - acsx)).astype(BF16)) * blockmask
        s_ref[...] = s_old * jnp.exp(last_x) + contrib
        cb = []
        for g in range(SSD_GROUPS):
            cg = jnp.where(lax.shift_right_logical(lane, 6) == g, c_all, 0.0).astype(BF16)
            cb.append(_dot_nt(cg, b16))
        z = z_ref[r0:r0 + q_len, :]
        y_blocks = []
        for j in range(SSD_HEADS // 2):
            blk = slice(j * LANES, (j + 1) * LANES)
            zs = []
            for h in (2 * j, 2 * j + 1):
                seg = acs[:, h:h + 1] - acs_t[h:h + 1, :]
                decay = jnp.where(causal, jnp.exp(seg), 0.0)
                scores = (cb[h // (SSD_HEADS // SSD_GROUPS)] * decay).astype(BF16)
                zs.append(_dot(scores, xdt16[:, blk]))
            y = jnp.where(left, zs[0], zs[1]) + y_off[:, blk]
            y = y + dsk_ref[:, blk] * xs[:, blk]
            y_blocks.append(y * _silu(z[:, blk]))
        for g in range(SSD_GROUPS):
            y0, y1 = y_blocks[2 * g], y_blocks[2 * g + 1]
            ss = jnp.sum(y0 * y0, axis=-1, keepdims=True) + jnp.sum(y1 * y1, axis=-1, keepdims=True)
            rs = lax.rsqrt(ss * (1.0 / (2 * LANES)) + EPS)
            for k, yk in ((2 * g, y0), (2 * g + 1, y1)):
                blk = slice(k * LANES, (k + 1) * LANES)
                out = yk * rs * nw_ref[:, blk]
                mix_ref[r0:r0 + q_len, POOL_WIDTH + k * LANES:POOL_WIDTH + (k + 1) * LANES] = out.astype(BF16)
        pos = c * rows + r0 + rowi
        for g, w in enumerate(POOL_WINDOWS):
            blk = slice(g * POOL_GROUP_DIM, (g + 1) * POOL_GROUP_DIM)
            pe = pool_ext[r0:r0 + 2 * q_len, blk]
            band = ((wj <= wl + POOL_HDR) & (wj > wl + POOL_HDR - w)).astype(BF16)
            winsum = _dot_exact_lhs(band, pe, terms=2)
            cnt = jnp.minimum(pos + 1, w).astype(F32)
            m = winsum / cnt - pe[q_len:2 * q_len, :]
            yg = _dot(m.astype(BF16), plw_ref[g]) + plb_ref[:, blk]
            mix_ref[r0:r0 + q_len, blk] = (yg * psc_ref[:, blk]).astype(BF16)

    @pl.when(c == pl.num_programs(1) - 1)
    def _():
        st_ref[0] = s_ref[...]


def _mix_prompt(vp, z, xbc, dt, cw, cb, dtb, alog, dsk, nw, plw16, plb, psc, nb, seq):
    steps = seq // MIX_ROWS
    row = lambda b, c: (b * steps + c, 0)
    fixed2 = lambda b, c: (0, 0)
    return pl.pallas_call(
        _mix_body,
        grid=(nb, steps),
        in_specs=[pl.BlockSpec((MIX_ROWS, POOL_WIDTH), row), pl.BlockSpec((MIX_ROWS, SSD_WIDTH), row),
                  pl.BlockSpec((MIX_ROWS, CONV_DIM), row), pl.BlockSpec((MIX_ROWS, LANES), row),
                  pl.BlockSpec((SSD_CONV, CONV_DIM), fixed2), pl.BlockSpec((1, CONV_DIM), fixed2),
                  pl.BlockSpec((1, LANES), fixed2), pl.BlockSpec((1, LANES), fixed2),
                  pl.BlockSpec((1, SSD_WIDTH), fixed2), pl.BlockSpec((1, SSD_WIDTH), fixed2),
                  pl.BlockSpec((len(POOL_WINDOWS), POOL_GROUP_DIM, POOL_GROUP_DIM), lambda b, c: (0, 0, 0)),
                  pl.BlockSpec((1, POOL_WIDTH), fixed2), pl.BlockSpec((1, POOL_WIDTH), fixed2)],
        out_specs=[pl.BlockSpec((MIX_ROWS, D_MODEL), row),
                   pl.BlockSpec((1, SSD_BC, SSD_WIDTH), lambda b, c: (b, 0, 0))],
        out_shape=[jax.ShapeDtypeStruct((nb * seq, D_MODEL), BF16),
                   jax.ShapeDtypeStruct((nb, SSD_BC, SSD_WIDTH), F32)],
        scratch_shapes=[pltpu.VMEM((POOL_HDR + MIX_ROWS, POOL_WIDTH), F32),
                        pltpu.VMEM((CONV_HDR + MIX_ROWS, CONV_DIM), F32),
                        pltpu.VMEM((SSD_BC, SSD_WIDTH), F32)],
        compiler_params=_cparams("parallel", "arbitrary"),
        name="mix_prompt",
    )(vp, z, xbc, dt, cw, cb, dtb, alog, dsk, nw, plw16, plb, psc)


def _mix_step_body(vp_ref, z_ref, xbc_ref, dt_ref, sp_ref, sc_ref, st_ref, cw_ref, cb_ref, dtb_ref, alog_ref,
                   dsk_ref, nw_ref, plw_ref, plb_ref, psc_ref, mix_ref, po_ref, co_ref, so_ref,
                   xdt_t, dec_t, b_t, c_t, xs_keep, y_t):
    h = pl.program_id(0)

    @pl.when(h == 0)
    def _():
        xbc = xbc_ref[...]
        conv = cb_ref[...] + sc_ref[0] * cw_ref[0:1, :]
        conv = conv + sc_ref[1] * cw_ref[1:2, :]
        conv = conv + sc_ref[2] * cw_ref[2:3, :]
        conv = conv + xbc * cw_ref[3:4, :]
        conv = _silu(conv)
        co_ref[0] = sc_ref[1]
        co_ref[1] = sc_ref[2]
        co_ref[2] = xbc
        xs = conv[:, 0:SSD_WIDTH]
        xs_keep[...] = xs
        b_t[...] = conv[:, SSD_WIDTH:SSD_WIDTH + SSD_BC].T
        c_t[...] = conv[:, SSD_WIDTH + SSD_BC:CONV_DIM].T
        dt = _softplus(dt_ref[...] + dtb_ref[...])
        d_a = dt * (-jnp.exp(alog_ref[...]))
        dt_t = dt.T
        dec_t[...] = jnp.exp(d_a).T
        xs_t = xs.T
        for k in range(SSD_HEADS):
            blk = slice(k * SSD_HEAD_DIM, (k + 1) * SSD_HEAD_DIM)
            xdt_t[blk, :] = xs_t[blk, :] * dt_t[k:k + 1, :]
        v = vp_ref[...]
        for k in range(POOL_BUF - 1):
            po_ref[k] = sp_ref[k + 1]
        po_ref[POOL_BUF - 1] = v
        for g, w in enumerate(POOL_WINDOWS):
            blk = slice(g * POOL_GROUP_DIM, (g + 1) * POOL_GROUP_DIM)
            acc = sp_ref[POOL_BUF - (w - 1), :, blk]
            for k in range(w - 2, 0, -1):
                acc = acc + sp_ref[POOL_BUF - k, :, blk]
            acc = acc + v[:, blk]
            m = acc / float(min(PAST_LEN + 1, w)) - v[:, blk]
            yg = _dot(m.astype(BF16), plw_ref[g]) + plb_ref[:, blk]
            mix_ref[:, blk] = (yg * psc_ref[:, blk]).astype(BF16)

    g_off = pl.multiple_of((h // (SSD_HEADS // SSD_GROUPS)) * SSD_STATE, SSD_STATE)
    h_off = pl.multiple_of(h * SSD_HEAD_DIM, SSD_HEAD_DIM)
    b_g = b_t[pl.ds(g_off, SSD_STATE), :]
    c_g = c_t[pl.ds(g_off, SSD_STATE), :]
    dec = dec_t[pl.ds(h, 1), :]
    xdt = xdt_t[pl.ds(h_off, SSD_HEAD_DIM), :]
    y_rows = []
    for p in range(SSD_HEAD_DIM):
        s_new = st_ref[0, p] * dec + xdt[p:p + 1, :] * b_g
        so_ref[0, p] = s_new
        y_rows.append(jnp.sum(s_new * c_g, axis=0, keepdims=True))
    y_t[pl.ds(h_off, SSD_HEAD_DIM), :] = jnp.concatenate(y_rows, axis=0)

    @pl.when(h == pl.num_programs(0) - 1)
    def _():
        xs = xs_keep[...]
        y = y_t[...].T + dsk_ref[...] * xs
        y = y * _silu(z_ref[...])
        width = SSD_WIDTH // SSD_GROUPS
        for g in range(SSD_GROUPS):
            blk = slice(g * width, (g + 1) * width)
            yg = y[:, blk]
            rs = lax.rsqrt(jnp.mean(yg * yg, axis=-1, keepdims=True) + EPS)
            mix_ref[:, POOL_WIDTH + g * width:POOL_WIDTH + (g + 1) * width] = (yg * rs * nw_ref[:, blk]).astype(BF16)


def _mix_step(vp, z, xbc, dt, pool_t, conv_t, ssm_t, cw, cb, dtb, alog, dsk, nw, plw16, plb, psc):
    n = vp.shape[0]
    fixed2 = lambda h: (0, 0)
    fixed3 = lambda h: (0, 0, 0)
    st_spec = pl.BlockSpec((1, SSD_HEAD_DIM, SSD_STATE, n), lambda h: (h, 0, 0, 0))
    return pl.pallas_call(
        _mix_step_body,
        grid=(SSD_HEADS,),
        in_specs=[pl.BlockSpec((n, POOL_WIDTH), fixed2), pl.BlockSpec((n, SSD_WIDTH), fixed2),
                  pl.BlockSpec((n, CONV_DIM), fixed2), pl.BlockSpec((n, LANES), fixed2),
                  pl.BlockSpec((POOL_BUF, n, POOL_WIDTH), fixed3),
                  pl.BlockSpec((SSD_CONV - 1, n, CONV_DIM), fixed3),
                  st_spec,
                  pl.BlockSpec((SSD_CONV, CONV_DIM), fixed2), pl.BlockSpec((1, CONV_DIM), fixed2),
                  pl.BlockSpec((1, LANES), fixed2), pl.BlockSpec((1, LANES), fixed2),
                  pl.BlockSpec((1, SSD_WIDTH), fixed2), pl.BlockSpec((1, SSD_WIDTH), fixed2),
                  pl.BlockSpec((len(POOL_WINDOWS), POOL_GROUP_DIM, POOL_GROUP_DIM), fixed3),
                  pl.BlockSpec((1, POOL_WIDTH), fixed2), pl.BlockSpec((1, POOL_WIDTH), fixed2)],
        out_specs=[pl.BlockSpec((n, D_MODEL), fixed2), pl.BlockSpec((POOL_BUF, n, POOL_WIDTH), fixed3),
                   pl.BlockSpec((SSD_CONV - 1, n, CONV_DIM), fixed3), st_spec],
        out_shape=[jax.ShapeDtypeStruct((n, D_MODEL), BF16), jax.ShapeDtypeStruct(pool_t.shape, F32),
                   jax.ShapeDtypeStruct(conv_t.shape, F32), jax.ShapeDtypeStruct(ssm_t.shape, F32)],
        scratch_shapes=[pltpu.VMEM((SSD_WIDTH, n), F32), pltpu.VMEM((LANES, n), F32),
                        pltpu.VMEM((SSD_BC, n), F32), pltpu.VMEM((SSD_BC, n), F32),
                        pltpu.VMEM((n, SSD_WIDTH), F32), pltpu.VMEM((SSD_WIDTH, n), F32)],
        compiler_params=_cparams("arbitrary"),
        name="mix_step",
    )(vp, z, xbc, dt, pool_t, conv_t, ssm_t, cw, cb, dtb, alog, dsk, nw, plw16, plb, psc)


PROMPT_TILE = 512


def _sort_tables(counts, tile_tokens):
    cnt = jnp.concatenate(counts, axis=0)
    pc = (cnt + RUN_PAD - 1) // RUN_PAD * RUN_PAD
    off_local = jnp.cumsum(pc, axis=1) - pc
    tile_rows = jnp.sum(pc, axis=1)
    region = (jnp.sum(pc, axis=0) + MOE_TILE - 1) // MOE_TILE * MOE_TILE
    base = jnp.cumsum(region) - region
    dst = base[None, :] + jnp.cumsum(pc, axis=0) - pc
    per_pass = []
    lo = 0
    for c, tm in zip(counts, tile_tokens):
        hi = lo + c.shape[0]
        n_chunk = _sorted_rows_per_tile(tm) // RUN_PAD
        s = jnp.arange(n_chunk, dtype=jnp.int32) * RUN_PAD
        begins = off_local[lo:hi, None, :]
        ends = begins + pc[lo:hi, None, :]
        inside = (begins <= s[None, :, None]) & (s[None, :, None] < ends)
        shift = jnp.sum(jnp.where(inside, dst[lo:hi, None, :] - begins, 0), axis=2)
        chunk_dst = jnp.where(s[None, :] < tile_rows[lo:hi, None], s[None, :] + shift, 0).astype(jnp.int32)
        off_v = jnp.broadcast_to(off_local[lo:hi, :, None].astype(F32), (hi - lo, N_EXPERTS, LANES))
        per_pass.append((chunk_dst, (tile_rows[lo:hi] // RUN_PAD).astype(jnp.int32), off_v))
        lo = hi
    tiles_cum = jnp.cumsum(region // MOE_TILE)
    n_active = tiles_cum[-1]
    return per_pass, tiles_cum, n_active


def _max_sorted_rows(tile_counts, tile_tokens):
    rows = sum(n * (2 * tm + N_EXPERTS * (RUN_PAD - 1)) for n, tm in zip(tile_counts, tile_tokens))
    return (-(-rows // MOE_TILE) + N_EXPERTS) * MOE_TILE


def _row(v):
    return v.reshape(1, -1).astype(F32)


def _pad_lanes(v):
    return jnp.pad(v.reshape(1, -1).astype(F32), ((0, 0), (0, LANES - v.size)))


def _state_from_blocks(st):
    n = st.shape[0]
    hpg = SSD_HEADS // SSD_GROUPS
    s6 = st.reshape(n, SSD_GROUPS, SSD_STATE, SSD_GROUPS, hpg, SSD_HEAD_DIM)
    per_group = [s6[:, g, :, g] for g in range(SSD_GROUPS)]
    s = jnp.stack(per_group, axis=1)
    return jnp.transpose(s, (0, 1, 3, 4, 2)).reshape(n, SSD_HEADS, SSD_HEAD_DIM, SSD_STATE)


def kernel(x_prompt, x_sample, p_prompt, p_sample, state_pool, state_conv, state_ssm, norm_mix, w_in, pool_lin_w, pool_lin_b, pool_scale, conv_w, conv_b, dt_bias, a_log, d_skip, ssd_norm, w_out, norm_ffn, router_grp_w, router_grp_b, router_exp_w, router_exp_b, exp_w_gate, exp_w_up, exp_w_down, norm_ple, ple_gate_w, ple_gate_b, ple_proj_w, norm_final):
    nb, seq, _ = x_prompt.shape
    ns = x_sample.shape[0]
    assert ns == LANES and x_sample.shape[1] == 1 and seq % MIX_ROWS == 0 and seq >= POOL_BUF

    w_in16 = jnp.pad(w_in[0], ((0, 0), (0, IN_COLS - w_in.shape[2]))).astype(BF16)
    w_out16 = w_out[0].astype(BF16)
    g_mix, g_ffn, g_ple, g_fin = _row(norm_mix[0]), _row(norm_ffn[0]), _row(norm_ple[0]), _row(norm_final)
    cw, cb = conv_w[0].astype(F32), _row(conv_b[0])
    dtb, alog = _pad_lanes(dt_bias[0]), _pad_lanes(a_log[0])
    dsk = _row(jnp.repeat(d_skip[0], SSD_HEAD_DIM))
    nw = _row(ssd_norm[0])
    plw16 = pool_lin_w[0].astype(BF16)
    plb, psc = _row(pool_lin_b[0]), _row(pool_scale[0])
    zeros4 = jnp.zeros((D_MODEL, 8 - N_EXPERT_GROUPS), F32)
    r_w = jnp.concatenate([router_grp_w[0], zeros4, router_exp_w[0],
                           jnp.zeros((D_MODEL, LANES - 8 - N_EXPERTS), F32)], axis=1).T
    r_hi = r_w.astype(BF16)
    r_lo = (r_w - r_hi.astype(F32)).astype(BF16)
    r_b = jnp.concatenate([router_grp_b[0], jnp.zeros((8 - N_EXPERT_GROUPS,), F32), router_exp_b[0],
                           jnp.zeros((LANES - 8 - N_EXPERTS,), F32)])
    wg = exp_w_gate[0].reshape(N_EXPERTS, D_MODEL, EXPERT_FF)
    wu = exp_w_up[0].reshape(N_EXPERTS, D_MODEL, EXPERT_FF)
    wd = exp_w_down[0].reshape(N_EXPERTS, EXPERT_FF, D_MODEL)
    pg16 = ple_gate_w[0].astype(BF16)
    pgb = _row(ple_gate_b[0])
    pp16 = ple_proj_w[0].astype(BF16)

    def route(x, mix16, tm):
        r_bias = jnp.broadcast_to(r_b[:, None], (LANES, tm))
        return _out_proj(x, mix16, w_out16, g_ffn, r_hi, r_lo, r_bias, tm)

    xp = x_prompt.reshape(nb * seq, D_MODEL)
    vp, z, xbc, dt = _in_proj(xp, g_mix, w_in16, PROMPT_TILE)
    mix16, st = _mix_prompt(vp, z, xbc, dt, cw, cb, dtb, alog, dsk, nw, plw16, plb, psc, nb, seq)
    h1_p, u_p, rt_p, gate_p, cnt_p = route(xp, mix16, PROMPT_TILE)
    pool_p = vp.reshape(nb, seq, POOL_WIDTH)[:, seq - POOL_BUF:]
    conv_p = xbc.reshape(nb, seq, CONV_DIM)[:, seq - (SSD_CONV - 1):]
    ssm_p = _state_from_blocks(st)

    xs_ = x_sample.reshape(ns, D_MODEL)
    vp_s, z_s, xbc_s, dt_s = _in_proj(xs_, g_mix, w_in16, ns)
    mix_s, pool_t, conv_t, ssm_t = _mix_step(
        vp_s, z_s, xbc_s, dt_s, jnp.transpose(state_pool[0], (1, 0, 2)), jnp.transpose(state_conv[0], (1, 0, 2)),
        jnp.transpose(state_ssm[0], (1, 2, 3, 0)), cw, cb, dtb, alog, dsk, nw, plw16, plb, psc)
    h1_s, u_s, rt_s, gate_s, cnt_s = route(xs_, mix_s, ns)
    pool_s = jnp.transpose(pool_t, (1, 0, 2))
    conv_s = jnp.transpose(conv_t, (1, 0, 2))
    ssm_s = jnp.transpose(ssm_t, (3, 0, 1, 2))

    counts = [cnt_p[:, :, 0].astype(jnp.int32), cnt_s[:, :, 0].astype(jnp.int32)]
    tiles = (PROMPT_TILE, ns)
    (tab_p, tab_s), tiles_cum, n_active = _sort_tables(counts, tiles)
    total_rows = _max_sorted_rows([c.shape[0] for c in counts], tiles)
    xs_sorted, dest_p = _sort_tokens(tab_p[0], tab_p[1], u_p, rt_p, tab_p[2], None, PROMPT_TILE, total_rows)
    xs_sorted, dest_s = _sort_tokens(tab_s[0], tab_s[1], u_s, rt_s, tab_s[2], xs_sorted, ns, total_rows)
    jj = jnp.minimum(jnp.arange(total_rows // MOE_TILE, dtype=jnp.int32), n_active - 1)
    tile_expert = jnp.minimum(jnp.sum((tiles_cum[None, :] <= jj[:, None]).astype(jnp.int32), axis=1), N_EXPERTS - 1)
    ys_sorted = _moe_sorted(tile_expert, n_active.reshape(1).astype(jnp.int32), xs_sorted, wg, wu, wd)

    y_prompt = _ple(tab_p[0], tab_p[1], h1_p, dest_p, gate_p, p_prompt[0].reshape(nb * seq, PLE_DIM),
                    g_ple, pg16, pgb, pp16, g_fin, ys_sorted, PROMPT_TILE)
    y_sample = _ple(tab_s[0], tab_s[1], h1_s, dest_s, gate_s, p_sample[0].reshape(ns, PLE_DIM),
                    g_ple, pg16, pgb, pp16, g_fin, ys_sorted, ns)

    return (y_prompt.reshape(nb, seq, D_MODEL), y_sample.reshape(ns, 1, D_MODEL),
            pool_p[None], conv_p[None], ssm_p[None], pool_s[None], conv_s[None], ssm_s[None])
```

```python
import functools

import jax
import jax.numpy as jnp
from jax import lax
from jax.experimental import pallas as pl
from jax.experimental.pallas import tpu as pltpu

F32 = jnp.float32
BF16 = jnp.bfloat16

D_MODEL = 1024
POOL_WIDTH = 512
POOL_WINDOWS = (2, 4, 8, 16)
POOL_GROUP_DIM = 128
POOL_BUF = 15
SSD_WIDTH = 512
SSD_HEAD_DIM = 64
SSD_HEADS = 8
SSD_GROUPS = 2
SSD_STATE = 64
SSD_CONV = 4
SSD_BC = SSD_GROUPS * SSD_STATE
CONV_DIM = SSD_WIDTH + 2 * SSD_BC
N_EXPERT_GROUPS = 4
EXPERTS_PER_GROUP = 8
N_EXPERTS = N_EXPERT_GROUPS * EXPERTS_PER_GROUP
EXPERT_FF = 256
PLE_DIM = 256
PAST_LEN = 16384
EPS = 1e-6

LANES = 128
IN_COLS = 1920
DT_OFF = POOL_WIDTH + SSD_WIDTH + CONV_DIM
VMEM_LIMIT = 56 * 1024 * 1024


def _cparams(*sem):
    return pltpu.CompilerParams(dimension_semantics=sem, vmem_limit_bytes=VMEM_LIMIT)


def _rms(x, g):
    return x * lax.rsqrt(jnp.mean(x * x, axis=-1, keepdims=True) + EPS) * g


def _sigmoid(x):
    return 1.0 / (1.0 + jnp.exp(-x))


def _silu(x):
    return x * _sigmoid(x)


def _split3(v):
    hi = v.astype(BF16)
    r = v - hi.astype(F32)
    mid = r.astype(BF16)
    lo = (r - mid.astype(F32)).astype(BF16)
    return hi, mid, lo


def _dot(a, b):
    return jnp.dot(a, b, preferred_element_type=F32)


def _dot_nt(a, b):
    return lax.dot_general(a, b, (((1,), (1,)), ((), ())), preferred_element_type=F32)


def _dot_tn(a, b):
    return lax.dot_general(a, b, (((0,), (0,)), ((), ())), preferred_element_type=F32)


def _dot_exact_lhs(sel, v, terms=3):
    acc = None
    for t in _split3(v)[:terms]:
        p = _dot(sel, t)
        acc = p if acc is None else acc + p
    return acc


def _dot_exact_rhs(v, sel, terms=3):
    acc = None
    for t in _split3(v)[:terms]:
        p = _dot(t, sel)
        acc = p if acc is None else acc + p
    return acc


def _in_proj_body(x_ref, g_ref, w_ref, vp_ref, z_ref, xbc_ref, dt_ref):
    a16 = _rms(x_ref[...], g_ref[...]).astype(BF16)
    vp_ref[...] = _dot(a16, w_ref[:, 0:POOL_WIDTH])
    z_ref[...] = _dot(a16, w_ref[:, POOL_WIDTH:POOL_WIDTH + SSD_WIDTH])
    xbc_ref[...] = _dot(a16, w_ref[:, POOL_WIDTH + SSD_WIDTH:DT_OFF])
    dt_ref[...] = _dot(a16, w_ref[:, DT_OFF:IN_COLS])


def _in_proj(x, g, w16, tm):
    t = x.shape[0]
    row = lambda i: (i, 0)
    fixed = lambda i: (0, 0)
    return pl.pallas_call(
        _in_proj_body,
        grid=(t // tm,),
        in_specs=[pl.BlockSpec((tm, D_MODEL), row), pl.BlockSpec((1, D_MODEL), fixed),
                  pl.BlockSpec((D_MODEL, IN_COLS), fixed)],
        out_specs=[pl.BlockSpec((tm, POOL_WIDTH), row), pl.BlockSpec((tm, SSD_WIDTH), row),
                   pl.BlockSpec((tm, CONV_DIM), row), pl.BlockSpec((tm, LANES), row)],
        out_shape=[jax.ShapeDtypeStruct((t, POOL_WIDTH), F32), jax.ShapeDtypeStruct((t, SSD_WIDTH), F32),
                   jax.ShapeDtypeStruct((t, CONV_DIM), F32), jax.ShapeDtypeStruct((t, LANES), F32)],
        compiler_params=_cparams("parallel"),
        name="in_proj",
    )(x, g, w16)


def _route(lg):
    tm = lg.shape[1]
    gl = lg[0:N_EXPERT_GROUPS, :]
    gmax = jnp.max(gl, axis=0, keepdims=True)
    gsum = jnp.sum(jnp.exp(gl - gmax), axis=0, keepdims=True)
    g_w = 1.0 / gsum
    gi = lax.broadcasted_iota(jnp.int32, gl.shape, 0)
    g_idx = jnp.min(jnp.where(gl == gmax, gi, N_EXPERT_GROUPS), axis=0, keepdims=True)
    sel = jnp.zeros((EXPERTS_PER_GROUP, tm), F32)
    for g in range(N_EXPERT_GROUPS):
        blk = lg[8 + g * EXPERTS_PER_GROUP:8 + (g + 1) * EXPERTS_PER_GROUP, :]
        sel = jnp.where(g_idx == g, blk, sel)
    ei = lax.broadcasted_iota(jnp.int32, sel.shape, 0)
    m1 = jnp.max(sel, axis=0, keepdims=True)
    i1 = jnp.min(jnp.where(sel == m1, ei, EXPERTS_PER_GROUP), axis=0, keepdims=True)
    rest = jnp.where(ei == i1, -jnp.inf, sel)
    m2 = jnp.max(rest, axis=0, keepdims=True)
    i2 = jnp.min(jnp.where(rest == m2, ei, EXPERTS_PER_GROUP), axis=0, keepdims=True)
    p2 = jnp.exp(m2 - m1)
    w1 = g_w / (1.0 + p2)
    w2 = g_w * p2 / (1.0 + p2)
    return g_idx * EXPERTS_PER_GROUP + i1, g_idx * EXPERTS_PER_GROUP + i2, w1, w2


def _out_proj_body(x_ref, mix_ref, w_ref, g_ref, rh_ref, rl_ref, rb_ref, h_ref, u_ref, rt_ref, wt_ref, cnt_ref):
    h = x_ref[...] + _dot(mix_ref[...], w_ref[...])
    h_ref[...] = h
    u = _rms(h, g_ref[...])
    u_hi = u.astype(BF16)
    u_ref[...] = u_hi
    u_lo = (u - u_hi.astype(F32)).astype(BF16)
    lg = _dot_nt(rh_ref[...], u_hi) + _dot_nt(rh_ref[...], u_lo) + _dot_nt(rl_ref[...], u_hi) + rb_ref[...]
    b1, b2, w1, w2 = _route(lg)
    tm = lg.shape[1]
    r8 = lax.broadcasted_iota(jnp.int32, (8, tm), 0)
    rt_ref[0] = jnp.where(r8 == 0, b1.astype(F32), jnp.where(r8 == 1, b2.astype(F32), 0.0))
    wt_ref[...] = jnp.concatenate([jnp.broadcast_to(w1, (LANES, tm)).T, jnp.broadcast_to(w2, (LANES, tm)).T], axis=1)
    kio = lax.broadcasted_iota(jnp.int32, (N_EXPERTS, tm), 0)
    hits = ((kio == b1) | (kio == b2)).astype(F32)
    cnt_ref[0] = jnp.broadcast_to(jnp.sum(hits, axis=1, keepdims=True), (N_EXPERTS, LANES))


def _out_proj(x, mix16, w16, g, r_hi, r_lo, r_bias, tm):
    t = x.shape[0]
    nt = t // tm
    row = lambda i: (i, 0)
    fixed = lambda i: (0, 0)
    tile3 = lambda i: (i, 0, 0)
    return pl.pallas_call(
        _out_proj_body,
        grid=(nt,),
        in_specs=[pl.BlockSpec((tm, D_MODEL), row), pl.BlockSpec((tm, D_MODEL), row),
                  pl.BlockSpec((D_MODEL, D_MODEL), fixed), pl.BlockSpec((1, D_MODEL), fixed),
                  pl.BlockSpec((LANES, D_MODEL), fixed), pl.BlockSpec((LANES, D_MODEL), fixed),
                  pl.BlockSpec((LANES, tm), fixed)],
        out_specs=[pl.BlockSpec((tm, D_MODEL), row), pl.BlockSpec((tm, D_MODEL), row),
                   pl.BlockSpec((1, 8, tm), tile3), pl.BlockSpec((tm, 2 * LANES), row),
                   pl.BlockSpec((1, N_EXPERTS, LANES), tile3)],
        out_shape=[jax.ShapeDtypeStruct((t, D_MODEL), F32), jax.ShapeDtypeStruct((t, D_MODEL), BF16),
                   jax.ShapeDtypeStruct((nt, 8, tm), F32), jax.ShapeDtypeStruct((t, 2 * LANES), F32),
                   jax.ShapeDtypeStruct((nt, N_EXPERTS, LANES), F32)],
        compiler_params=_cparams("parallel"),
        name="out_proj_router",
    )(x, mix16, w16, g, r_hi, r_lo, r_bias)


RUN_PAD = 16
MOE_TILE = 512


def _tile_lanes(v, width):
    reps = width // LANES
    return v if reps == 1 else jnp.concatenate([v] * reps, axis=1)


def _sorted_rows_per_tile(tm):
    need = 2 * tm + N_EXPERTS * (RUN_PAD - 1)
    return -(-need // LANES) * LANES


def _sort_body(cd_ref, u_ref, rt_ref, off_ref, *rest):
    xs_ref, dt_ref, loc, sem = rest[-4:]
    i = pl.program_id(0)
    tm = u_ref.shape[0]
    rows = loc.shape[1]
    rt = rt_ref[0]
    b1 = rt[0:1, :].astype(jnp.int32)
    b2 = rt[1:2, :].astype(jnp.int32)
    kio = lax.broadcasted_iota(jnp.int32, (N_EXPERTS, tm), 0)
    o1 = kio == b1
    o2 = kio == b2
    before = (lax.broadcasted_iota(jnp.int32, (tm, tm), 0) < lax.broadcasted_iota(jnp.int32, (tm, tm), 1))
    start = _dot((o1 | o2).astype(BF16), before.astype(BF16)) + _tile_lanes(off_ref[0], tm)
    d1 = jnp.sum(jnp.where(o1, start, 0.0), axis=0, keepdims=True)
    d2 = jnp.sum(jnp.where(o2, start, 0.0), axis=0, keepdims=True)
    dt_ref[...] = jnp.concatenate([jnp.broadcast_to(d1, (LANES, tm)).T, jnp.broadcast_to(d2, (LANES, tm)).T], axis=1)
    rio = lax.broadcasted_iota(jnp.int32, (rows, tm), 0)
    perm = ((rio == d1.astype(jnp.int32)) | (rio == d2.astype(jnp.int32))).astype(BF16)
    slot = i % 2
    loc[slot] = _dot(perm, u_ref[...]).astype(BF16)

    def chunk_copy(step, sl, c):
        return pltpu.make_async_copy(
            loc.at[sl, pl.ds(c * RUN_PAD, RUN_PAD), :],
            xs_ref.at[pl.ds(pl.multiple_of(cd_ref[step, c], RUN_PAD), RUN_PAD), :], sem.at[sl])

    n_chunk = rows // RUN_PAD
    for c in range(n_chunk):
        chunk_copy(i, slot, c).start()

    @pl.when(i > 0)
    def _():
        for c in range(n_chunk):
            chunk_copy(i - 1, 1 - slot, c).wait()

    @pl.when(i == pl.num_programs(0) - 1)
    def _():
        for c in range(n_chunk):
            chunk_copy(i, slot, c).wait()


def _sort_tokens(chunk_dst, u16, rt, off_v, xs_prev, tm, total_rows):
    t = u16.shape[0]
    rows = _sorted_rows_per_tile(tm)
    in_specs = [pl.BlockSpec((tm, D_MODEL), lambda i, cd: (i, 0)),
                pl.BlockSpec((1, 8, tm), lambda i, cd: (i, 0, 0)),
                pl.BlockSpec((1, N_EXPERTS, LANES), lambda i, cd: (i, 0, 0))]
    args = [chunk_dst, u16, rt, off_v]
    aliases = {}
    if xs_prev is not None:
        in_specs.append(pl.BlockSpec(memory_space=pl.ANY))
        aliases = {len(args): 0}
        args.append(xs_prev)
    return pl.pallas_call(
        _sort_body,
        grid_spec=pltpu.PrefetchScalarGridSpec(
            num_scalar_prefetch=1,
            grid=(t // tm,),
            in_specs=in_specs,
            out_specs=[pl.BlockSpec(memory_space=pl.ANY),
                       pl.BlockSpec((tm, 2 * LANES), lambda i, cd: (i, 0))],
            scratch_shapes=[pltpu.VMEM((2, rows, D_MODEL), BF16), pltpu.SemaphoreType.DMA((2,))]),
        out_shape=[jax.ShapeDtypeStruct((total_rows, D_MODEL), BF16),
                   jax.ShapeDtypeStruct((t, 2 * LANES), F32)],
        input_output_aliases=aliases,
        compiler_params=_cparams("arbitrary"),
        name="sort_tokens",
    )(*args)


def _moe_body(te_ref, na_ref, x_ref, wg_ref, wu_ref, wd_ref, o_ref, wg16, wu16, wd16):
    j = pl.program_id(0)

    @pl.when(j < na_ref[0])
    def _():
        @pl.when((j == 0) | (te_ref[j] != te_ref[jnp.maximum(j - 1, 0)]))
        def _():
            wg16[...] = wg_ref[0].astype(BF16)
            wu16[...] = wu_ref[0].astype(BF16)
            wd16[...] = wd_ref[0].astype(BF16)

        x = x_ref[...]
        act = _silu(_dot(x, wg16[...])) * _dot(x, wu16[...])
        o_ref[...] = _dot(act.astype(BF16), wd16[...]).astype(BF16)


def _moe_sorted(tile_expert, n_active, xs, wg, wu, wd, data_rows):
    total_rows = xs.shape[0]
    n_tiles = data_rows // MOE_TILE
    x_map = lambda j, te, na: (jnp.minimum(j, na[0] - 1), 0)
    w_map = lambda j, te, na: (te[j], 0, 0)
    return pl.pallas_call(
        _moe_body,
        grid_spec=pltpu.PrefetchScalarGridSpec(
            num_scalar_prefetch=2,
            grid=(n_tiles,),
            in_specs=[pl.BlockSpec((MOE_TILE, D_MODEL), x_map),
                      pl.BlockSpec((1, D_MODEL, EXPERT_FF), w_map),
                      pl.BlockSpec((1, D_MODEL, EXPERT_FF), w_map),
                      pl.BlockSpec((1, EXPERT_FF, D_MODEL), w_map)],
            out_specs=pl.BlockSpec((MOE_TILE, D_MODEL), lambda j, te, na: (j, 0)),
            scratch_shapes=[pltpu.VMEM((D_MODEL, EXPERT_FF), BF16), pltpu.VMEM((D_MODEL, EXPERT_FF), BF16),
                            pltpu.VMEM((EXPERT_FF, D_MODEL), BF16)]),
        out_shape=jax.ShapeDtypeStruct((total_rows, D_MODEL), BF16),
        compiler_params=_cparams("arbitrary"),
        name="moe_sorted",
    )(tile_expert, n_active, xs, wg, wu, wd)


def _ple_body(cd_ref, h_ref, dt_ref, wt_ref, p_ref, gn_ref, wg_ref, bg_ref, wp_ref, fn_ref, ys_ref,
              o_ref, loc, sem):
    i = pl.program_id(0)
    tm = h_ref.shape[0]
    rows = loc.shape[1]
    n_chunk = rows // RUN_PAD
    slot = i % 2

    def chunk_copy(step, sl, c):
        return pltpu.make_async_copy(
            ys_ref.at[pl.ds(pl.multiple_of(cd_ref[step, c], RUN_PAD), RUN_PAD), :],
            loc.at[sl, pl.ds(c * RUN_PAD, RUN_PAD), :], sem.at[sl])

    def fetch(step, sl):
        for c in range(n_chunk):
            chunk_copy(step, sl, c).start()

    @pl.when(i == 0)
    def _():
        fetch(0, 0)

    @pl.when(i + 1 < pl.num_programs(0))
    def _():
        fetch(i + 1, 1 - slot)

    for c in range(n_chunk):
        chunk_copy(i, slot, c).wait()

    ys16 = loc[slot]
    ci = lax.broadcasted_iota(jnp.int32, (tm, LANES), 1)
    picked = []
    for k in range(2):
        dest = dt_ref[:, k * LANES:(k + 1) * LANES].astype(jnp.int32)
        sel = jnp.concatenate([(dest == ci + m * LANES) for m in range(rows // LANES)], axis=1).astype(BF16)
        picked.append(_dot(sel, ys16) * _tile_lanes(wt_ref[:, k * LANES:(k + 1) * LANES], D_MODEL))
    h = h_ref[...] + (picked[0] + picked[1])
    a16 = _rms(h, gn_ref[...]).astype(BF16)
    gate = _sigmoid(_dot(a16, wg_ref[...]) + bg_ref[...])
    pp = _dot(p_ref[...].astype(BF16), wp_ref[...])
    h = h + gate * pp
    o_ref[...] = _rms(h, fn_ref[...])


def _ple(chunk_src, h1, dest_t, gate_t, p, g_ple, wg16, bg, wp16, g_final, ys, tm):
    t = h1.shape[0]
    row = lambda i, cd: (i, 0)
    fixed = lambda i, cd: (0, 0)
    return pl.pallas_call(
        _ple_body,
        grid_spec=pltpu.PrefetchScalarGridSpec(
            num_scalar_prefetch=1,
            grid=(t // tm,),
            in_specs=[pl.BlockSpec((tm, D_MODEL), row), pl.BlockSpec((tm, 2 * LANES), row),
                      pl.BlockSpec((tm, 2 * LANES), row), pl.BlockSpec((tm, PLE_DIM), row),
                      pl.BlockSpec((1, D_MODEL), fixed), pl.BlockSpec((D_MODEL, D_MODEL), fixed),
                      pl.BlockSpec((1, D_MODEL), fixed), pl.BlockSpec((PLE_DIM, D_MODEL), fixed),
                      pl.BlockSpec((1, D_MODEL), fixed), pl.BlockSpec(memory_space=pl.ANY)],
            out_specs=pl.BlockSpec((tm, D_MODEL), row),
            scratch_shapes=[pltpu.VMEM((2, _sorted_rows_per_tile(tm), D_MODEL), BF16),
                            pltpu.SemaphoreType.DMA((2,))]),
        out_shape=jax.ShapeDtypeStruct((t, D_MODEL), F32),
        compiler_params=_cparams("arbitrary"),
        name="ple_final",
    )(chunk_src, h1, dest_t, gate_t, p, g_ple, wg16, bg, wp16, g_final, ys)


MIX_ROWS = 512
SSD_Q = 128
POOL_HDR = 128
CONV_HDR = 8


def _softplus(x):
    return jnp.maximum(x, 0.0) + jnp.log1p(jnp.exp(-jnp.abs(x)))


def _head_expand_matrix():
    r = lax.broadcasted_iota(jnp.int32, (LANES, SSD_WIDTH), 0)
    c = lax.broadcasted_iota(jnp.int32, (LANES, SSD_WIDTH), 1)
    return (lax.shift_right_logical(c, 6) == r).astype(BF16)


def _state_block_mask():
    r = lax.broadcasted_iota(jnp.int32, (SSD_BC, SSD_WIDTH), 0)
    c = lax.broadcasted_iota(jnp.int32, (SSD_BC, SSD_WIDTH), 1)
    return (lax.shift_right_logical(r, 6) == lax.shift_right_logical(c, 8)).astype(F32)


def _mix_body(vp_ref, z_ref, xbc_ref, dt_ref, cw_ref, cb_ref, dtb_ref, alog_ref, dsk_ref, nw_ref,
              plw_ref, plb_ref, psc_ref, mix_ref, st_ref, pool_ext, conv_ext, s_ref):
    c = pl.program_id(1)
    rows = MIX_ROWS
    q_len = SSD_Q

    @pl.when(c == 0)
    def _():
        pool_ext[0:POOL_HDR, :] = jnp.zeros((POOL_HDR, POOL_WIDTH), F32)
        conv_ext[0:CONV_HDR, :] = jnp.zeros((CONV_HDR, CONV_DIM), F32)
        s_ref[...] = jnp.zeros_like(s_ref)

    @pl.when(c > 0)
    def _():
        pool_ext[0:POOL_HDR, :] = pool_ext[rows:rows + POOL_HDR, :]
        conv_ext[0:CONV_HDR, :] = conv_ext[rows:rows + CONV_HDR, :]

    pool_ext[POOL_HDR:POOL_HDR + rows, :] = vp_ref[...]
    conv_ext[CONV_HDR:CONV_HDR + rows, :] = xbc_ref[...]

    li = lax.broadcasted_iota(jnp.int32, (q_len, q_len), 0)
    si = lax.broadcasted_iota(jnp.int32, (q_len, q_len), 1)
    causal = li >= si
    tri = causal.astype(BF16)
    expand = _head_expand_matrix()
    blockmask = _state_block_mask()
    lane = lax.broadcasted_iota(jnp.int32, (q_len, LANES), 1)
    left = lane < SSD_HEAD_DIM
    a_neg = -jnp.exp(alog_ref[...])
    wl = lax.broadcasted_iota(jnp.int32, (q_len, 2 * q_len), 0)
    wj = lax.broadcasted_iota(jnp.int32, (q_len, 2 * q_len), 1)
    rowi = lax.broadcasted_iota(jnp.int32, (q_len, LANES), 0)

    for q in range(rows // q_len):
        r0 = q * q_len
        base = CONV_HDR + r0
        conv = cb_ref[...] + conv_ext[base - 3:base - 3 + q_len, :] * cw_ref[0:1, :]
        conv = conv + conv_ext[base - 2:base - 2 + q_len, :] * cw_ref[1:2, :]
        conv = conv + conv_ext[base - 1:base - 1 + q_len, :] * cw_ref[2:3, :]
        conv = conv + conv_ext[base:base + q_len, :] * cw_ref[3:4, :]
        conv = _silu(conv)
        xs = conv[:, 0:SSD_WIDTH]
        b16 = conv[:, SSD_WIDTH:SSD_WIDTH + SSD_BC].astype(BF16)
        c_all = conv[:, SSD_WIDTH + SSD_BC:CONV_DIM]
        c16 = c_all.astype(BF16)
        dt = _softplus(dt_ref[r0:r0 + q_len, :] + dtb_ref[...])
        acs = _dot_exact_lhs(tri, dt * a_neg)
        acs_t = acs.T
        dtx = _dot_exact_rhs(dt, expand, terms=2)
        acsx = _dot_exact_rhs(acs, expand)
        last_x = acsx[q_len - 1:q_len, :]
        xdt = xs * dtx
        xdt16 = xdt.astype(BF16)
        s_old = s_ref[...]
        y_off = _dot(c16, s_old.astype(BF16)) * jnp.exp(acsx)
        contrib = _dot_tn(b16, (xdt * jnp.exp(last_x - acsx)).astype(BF16)) * blockmask
        s_ref[...] = s_old * jnp.exp(last_x) + contrib
        cb = []
        for g in range(SSD_GROUPS):
            cg = jnp.where(lax.shift_right_logical(lane, 6) == g, c_all, 0.0).astype(BF16)
            cb.append(_dot_nt(cg, b16))
        z = z_ref[r0:r0 + q_len, :]
        y_blocks = []
        for j in range(SSD_HEADS // 2):
            blk = slice(j * LANES, (j + 1) * LANES)
            zs = []
            for h in (2 * j, 2 * j + 1):
                seg = acs[:, h:h + 1] - acs_t[h:h + 1, :]
                decay = jnp.where(causal, jnp.exp(seg), 0.0)
                scores = (cb[h // (SSD_HEADS // SSD_GROUPS)] * decay).astype(BF16)
                zs.append(_dot(scores, xdt16[:, blk]))
            y = jnp.where(left, zs[0], zs[1]) + y_off[:, blk]
            y = y + dsk_ref[:, blk] * xs[:, blk]
            y_blocks.append(y * _silu(z[:, blk]))
        for g in range(SSD_GROUPS):
            y0, y1 = y_blocks[2 * g], y_blocks[2 * g + 1]
            ss = jnp.sum(y0 * y0, axis=-1, keepdims=True) + jnp.sum(y1 * y1, axis=-1, keepdims=True)
            rs = lax.rsqrt(ss * (1.0 / (2 * LANES)) + EPS)
            for k, yk in ((2 * g, y0), (2 * g + 1, y1)):
                blk = slice(k * LANES, (k + 1) * LANES)
                out = yk * rs * nw_ref[:, blk]
                mix_ref[r0:r0 + q_len, POOL_WIDTH + k * LANES:POOL_WIDTH + (k + 1) * LANES] = out.astype(BF16)
        pos = c * rows + r0 + rowi
        for g, w in enumerate(POOL_WINDOWS):
            blk = slice(g * POOL_GROUP_DIM, (g + 1) * POOL_GROUP_DIM)
            pe = pool_ext[r0:r0 + 2 * q_len, blk]
            band = ((wj <= wl + POOL_HDR) & (wj > wl + POOL_HDR - w)).astype(BF16)
            winsum = _dot_exact_lhs(band, pe, terms=2)
            cnt = jnp.minimum(pos + 1, w).astype(F32)
            m = winsum / cnt - pe[q_len:2 * q_len, :]
            yg = _dot(m.astype(BF16), plw_ref[g]) + plb_ref[:, blk]
            mix_ref[r0:r0 + q_len, blk] = (yg * psc_ref[:, blk]).astype(BF16)

    @pl.when(c == pl.num_programs(1) - 1)
    def _():
        st_ref[0] = s_ref[...]


def _mix_prompt(vp, z, xbc, dt, cw, cb, dtb, alog, dsk, nw, plw16, plb, psc, nb, seq):
    steps = seq // MIX_ROWS
    row = lambda b, c: (b * steps + c, 0)
    fixed2 = lambda b, c: (0, 0)
    return pl.pallas_call(
        _mix_body,
        grid=(nb, steps),
        in_specs=[pl.BlockSpec((MIX_ROWS, POOL_WIDTH), row), pl.BlockSpec((MIX_ROWS, SSD_WIDTH), row),
                  pl.BlockSpec((MIX_ROWS, CONV_DIM), row), pl.BlockSpec((MIX_ROWS, LANES), row),
                  pl.BlockSpec((SSD_CONV, CONV_DIM), fixed2), pl.BlockSpec((1, CONV_DIM), fixed2),
                  pl.BlockSpec((1, LANES), fixed2), pl.BlockSpec((1, LANES), fixed2),
                  pl.BlockSpec((1, SSD_WIDTH), fixed2), pl.BlockSpec((1, SSD_WIDTH), fixed2),
                  pl.BlockSpec((len(POOL_WINDOWS), POOL_GROUP_DIM, POOL_GROUP_DIM), lambda b, c: (0, 0, 0)),
                  pl.BlockSpec((1, POOL_WIDTH), fixed2), pl.BlockSpec((1, POOL_WIDTH), fixed2)],
        out_specs=[pl.BlockSpec((MIX_ROWS, D_MODEL), row),
                   pl.BlockSpec((1, SSD_BC, SSD_WIDTH), lambda b, c: (b, 0, 0))],
        out_shape=[jax.ShapeDtypeStruct((nb * seq, D_MODEL), BF16),
                   jax.ShapeDtypeStruct((nb, SSD_BC, SSD_WIDTH), F32)],
        scratch_shapes=[pltpu.VMEM((POOL_HDR + MIX_ROWS, POOL_WIDTH), F32),
                        pltpu.VMEM((CONV_HDR + MIX_ROWS, CONV_DIM), F32),
                        pltpu.VMEM((SSD_BC, SSD_WIDTH), F32)],
        compiler_params=_cparams("parallel", "arbitrary"),
        name="mix_prompt",
    )(vp, z, xbc, dt, cw, cb, dtb, alog, dsk, nw, plw16, plb, psc)


def _mix_step_body(vp_ref, z_ref, xbc_ref, dt_ref, sp_ref, sc_ref, st_ref, cw_ref, cb_ref, dtb_ref, alog_ref,
                   dsk_ref, nw_ref, plw_ref, plb_ref, psc_ref, mix_ref, po_ref, co_ref, so_ref,
                   xdt_t, dec_t, b_t, c_t, xs_keep, y_t):
    h = pl.program_id(0)

    @pl.when(h == 0)
    def _():
        xbc = xbc_ref[...]
        conv = cb_ref[...] + sc_ref[0] * cw_ref[0:1, :]
        conv = conv + sc_ref[1] * cw_ref[1:2, :]
        conv = conv + sc_ref[2] * cw_ref[2:3, :]
        conv = conv + xbc * cw_ref[3:4, :]
        conv = _silu(conv)
        co_ref[0] = sc_ref[1]
        co_ref[1] = sc_ref[2]
        co_ref[2] = xbc
        xs = conv[:, 0:SSD_WIDTH]
        xs_keep[...] = xs
        b_t[...] = conv[:, SSD_WIDTH:SSD_WIDTH + SSD_BC].T
        c_t[...] = conv[:, SSD_WIDTH + SSD_BC:CONV_DIM].T
        dt = _softplus(dt_ref[...] + dtb_ref[...])
        d_a = dt * (-jnp.exp(alog_ref[...]))
        dt_t = dt.T
        dec_t[...] = jnp.exp(d_a).T
        xs_t = xs.T
        for k in range(SSD_HEADS):
            blk = slice(k * SSD_HEAD_DIM, (k + 1) * SSD_HEAD_DIM)
            xdt_t[blk, :] = xs_t[blk, :] * dt_t[k:k + 1, :]
        v = vp_ref[...]
        for k in range(POOL_BUF - 1):
            po_ref[k] = sp_ref[k + 1]
        po_ref[POOL_BUF - 1] = v
        for g, w in enumerate(POOL_WINDOWS):
            blk = slice(g * POOL_GROUP_DIM, (g + 1) * POOL_GROUP_DIM)
            acc = sp_ref[POOL_BUF - (w - 1), :, blk]
            for k in range(w - 2, 0, -1):
                acc = acc + sp_ref[POOL_BUF - k, :, blk]
            acc = acc + v[:, blk]
            m = acc / float(min(PAST_LEN + 1, w)) - v[:, blk]
            yg = _dot(m.astype(BF16), plw_ref[g]) + plb_ref[:, blk]
            mix_ref[:, blk] = (yg * psc_ref[:, blk]).astype(BF16)

    g_off = pl.multiple_of((h // (SSD_HEADS // SSD_GROUPS)) * SSD_STATE, SSD_STATE)
    h_off = pl.multiple_of(h * SSD_HEAD_DIM, SSD_HEAD_DIM)
    b_g = b_t[pl.ds(g_off, SSD_STATE), :]
    c_g = c_t[pl.ds(g_off, SSD_STATE), :]
    dec = dec_t[pl.ds(h, 1), :]
    xdt = xdt_t[pl.ds(h_off, SSD_HEAD_DIM), :]
    y_rows = []
    for p in range(SSD_HEAD_DIM):
        s_new = st_ref[0, p] * dec + xdt[p:p + 1, :] * b_g
        so_ref[0, p] = s_new
        y_rows.append(jnp.sum(s_new * c_g, axis=0, keepdims=True))
    y_t[pl.ds(h_off, SSD_HEAD_DIM), :] = jnp.concatenate(y_rows, axis=0)

    @pl.when(h == pl.num_programs(0) - 1)
    def _():
        xs = xs_keep[...]
        y = y_t[...].T + dsk_ref[...] * xs
        y = y * _silu(z_ref[...])
        width = SSD_WIDTH // SSD_GROUPS
        for g in range(SSD_GROUPS):
            blk = slice(g * width, (g + 1) * width)
            yg = y[:, blk]
            rs = lax.rsqrt(jnp.mean(yg * yg, axis=-1, keepdims=True) + EPS)
            mix_ref[:, POOL_WIDTH + g * width:POOL_WIDTH + (g + 1) * width] = (yg * rs * nw_ref[:, blk]).astype(BF16)


def _mix_step(vp, z, xbc, dt, pool_t, conv_t, ssm_t, cw, cb, dtb, alog, dsk, nw, plw16, plb, psc):
    n = vp.shape[0]
    fixed2 = lambda h: (0, 0)
    fixed3 = lambda h: (0, 0, 0)
    st_spec = pl.BlockSpec((1, SSD_HEAD_DIM, SSD_STATE, n), lambda h: (h, 0, 0, 0))
    return pl.pallas_call(
        _mix_step_body,
        grid=(SSD_HEADS,),
        in_specs=[pl.BlockSpec((n, POOL_WIDTH), fixed2), pl.BlockSpec((n, SSD_WIDTH), fixed2),
                  pl.BlockSpec((n, CONV_DIM), fixed2), pl.BlockSpec((n, LANES), fixed2),
                  pl.BlockSpec((POOL_BUF, n, POOL_WIDTH), fixed3),
                  pl.BlockSpec((SSD_CONV - 1, n, CONV_DIM), fixed3),
                  st_spec,
                  pl.BlockSpec((SSD_CONV, CONV_DIM), fixed2), pl.BlockSpec((1, CONV_DIM), fixed2),
                  pl.BlockSpec((1, LANES), fixed2), pl.BlockSpec((1, LANES), fixed2),
                  pl.BlockSpec((1, SSD_WIDTH), fixed2), pl.BlockSpec((1, SSD_WIDTH), fixed2),
                  pl.BlockSpec((len(POOL_WINDOWS), POOL_GROUP_DIM, POOL_GROUP_DIM), fixed3),
                  pl.BlockSpec((1, POOL_WIDTH), fixed2), pl.BlockSpec((1, POOL_WIDTH), fixed2)],
        out_specs=[pl.BlockSpec((n, D_MODEL), fixed2), pl.BlockSpec((POOL_BUF, n, POOL_WIDTH), fixed3),
                   pl.BlockSpec((SSD_CONV - 1, n, CONV_DIM), fixed3), st_spec],
        out_shape=[jax.ShapeDtypeStruct((n, D_MODEL), BF16), jax.ShapeDtypeStruct(pool_t.shape, F32),
                   jax.ShapeDtypeStruct(conv_t.shape, F32), jax.ShapeDtypeStruct(ssm_t.shape, F32)],
        scratch_shapes=[pltpu.VMEM((SSD_WIDTH, n), F32), pltpu.VMEM((LANES, n), F32),
                        pltpu.VMEM((SSD_BC, n), F32), pltpu.VMEM((SSD_BC, n), F32),
                        pltpu.VMEM((n, SSD_WIDTH), F32), pltpu.VMEM((SSD_WIDTH, n), F32)],
        compiler_params=_cparams("arbitrary"),
        name="mix_step",
    )(vp, z, xbc, dt, pool_t, conv_t, ssm_t, cw, cb, dtb, alog, dsk, nw, plw16, plb, psc)


PROMPT_TILE = 512


def _sort_tables(counts, tile_tokens, data_rows):
    cnt = jnp.concatenate(counts, axis=0)
    pc = (cnt + RUN_PAD - 1) // RUN_PAD * RUN_PAD
    off_local = jnp.cumsum(pc, axis=1) - pc
    tile_rows = jnp.sum(pc, axis=1)
    region = (jnp.sum(pc, axis=0) + MOE_TILE - 1) // MOE_TILE * MOE_TILE
    base = jnp.cumsum(region) - region
    dst = base[None, :] + jnp.cumsum(pc, axis=0) - pc
    per_pass = []
    lo = 0
    for c, tm in zip(counts, tile_tokens):
        hi = lo + c.shape[0]
        n_chunk = _sorted_rows_per_tile(tm) // RUN_PAD
        s = jnp.arange(n_chunk, dtype=jnp.int32) * RUN_PAD
        begins = off_local[lo:hi, None, :]
        ends = begins + pc[lo:hi, None, :]
        inside = (begins <= s[None, :, None]) & (s[None, :, None] < ends)
        shift = jnp.sum(jnp.where(inside, dst[lo:hi, None, :] - begins, 0), axis=2)
        valid = s[None, :] < tile_rows[lo:hi, None]
        parity = (jnp.arange(hi - lo, dtype=jnp.int32) % 2)[:, None]
        spare = data_rows + parity * _sorted_rows_per_tile(max(tile_tokens)) + s[None, :]
        scatter_dst = jnp.where(valid, s[None, :] + shift, spare).astype(jnp.int32)
        gather_src = jnp.where(valid, s[None, :] + shift, 0).astype(jnp.int32)
        off_v = jnp.broadcast_to(off_local[lo:hi, :, None].astype(F32), (hi - lo, N_EXPERTS, LANES))
        per_pass.append((scatter_dst, gather_src, off_v))
        lo = hi
    tiles_cum = jnp.cumsum(region // MOE_TILE)
    n_active = tiles_cum[-1]
    return per_pass, tiles_cum, n_active


def _max_sorted_rows(tile_counts, tile_tokens):
    rows = sum(n * (2 * tm + N_EXPERTS * (RUN_PAD - 1)) for n, tm in zip(tile_counts, tile_tokens))
    data_rows = (-(-rows // MOE_TILE) + N_EXPERTS) * MOE_TILE
    return data_rows, data_rows + 2 * _sorted_rows_per_tile(max(tile_tokens))


def _row(v):
    return v.reshape(1, -1).astype(F32)


def _pad_lanes(v):
    return jnp.pad(v.reshape(1, -1).astype(F32), ((0, 0), (0, LANES - v.size)))


def _state_from_blocks(st):
    n = st.shape[0]
    hpg = SSD_HEADS // SSD_GROUPS
    s6 = st.reshape(n, SSD_GROUPS, SSD_STATE, SSD_GROUPS, hpg, SSD_HEAD_DIM)
    per_group = [s6[:, g, :, g] for g in range(SSD_GROUPS)]
    s = jnp.stack(per_group, axis=1)
    return jnp.transpose(s, (0, 1, 3, 4, 2)).reshape(n, SSD_HEADS, SSD_HEAD_DIM, SSD_STATE)


def kernel(x_prompt, x_sample, p_prompt, p_sample, state_pool, state_conv, state_ssm, norm_mix, w_in, pool_lin_w, pool_lin_b, pool_scale, conv_w, conv_b, dt_bias, a_log, d_skip, ssd_norm, w_out, norm_ffn, router_grp_w, router_grp_b, router_exp_w, router_exp_b, exp_w_gate, exp_w_up, exp_w_down, norm_ple, ple_gate_w, ple_gate_b, ple_proj_w, norm_final):
    nb, seq, _ = x_prompt.shape
    ns = x_sample.shape[0]
    assert ns == LANES and x_sample.shape[1] == 1 and seq % MIX_ROWS == 0 and seq >= POOL_BUF

    w_in16 = jnp.pad(w_in[0], ((0, 0), (0, IN_COLS - w_in.shape[2]))).astype(BF16)
    w_out16 = w_out[0].astype(BF16)
    g_mix, g_ffn, g_ple, g_fin = _row(norm_mix[0]), _row(norm_ffn[0]), _row(norm_ple[0]), _row(norm_final)
    cw, cb = conv_w[0].astype(F32), _row(conv_b[0])
    dtb, alog = _pad_lanes(dt_bias[0]), _pad_lanes(a_log[0])
    dsk = _row(jnp.repeat(d_skip[0], SSD_HEAD_DIM))
    nw = _row(ssd_norm[0])
    plw16 = pool_lin_w[0].astype(BF16)
    plb, psc = _row(pool_lin_b[0]), _row(pool_scale[0])
    zeros4 = jnp.zeros((D_MODEL, 8 - N_EXPERT_GROUPS), F32)
    r_w = jnp.concatenate([router_grp_w[0], zeros4, router_exp_w[0],
                           jnp.zeros((D_MODEL, LANES - 8 - N_EXPERTS), F32)], axis=1).T
    r_hi = r_w.astype(BF16)
    r_lo = (r_w - r_hi.astype(F32)).astype(BF16)
    r_b = jnp.concatenate([router_grp_b[0], jnp.zeros((8 - N_EXPERT_GROUPS,), F32), router_exp_b[0],
                           jnp.zeros((LANES - 8 - N_EXPERTS,), F32)])
    wg = exp_w_gate[0].reshape(N_EXPERTS, D_MODEL, EXPERT_FF)
    wu = exp_w_up[0].reshape(N_EXPERTS, D_MODEL, EXPERT_FF)
    wd = exp_w_down[0].reshape(N_EXPERTS, EXPERT_FF, D_MODEL)
    pg16 = ple_gate_w[0].astype(BF16)
    pgb = _row(ple_gate_b[0])
    pp16 = ple_proj_w[0].astype(BF16)

    def route(x, mix16, tm):
        r_bias = jnp.broadcast_to(r_b[:, None], (LANES, tm))
        return _out_proj(x, mix16, w_out16, g_ffn, r_hi, r_lo, r_bias, tm)

    xp = x_prompt.reshape(nb * seq, D_MODEL)
    vp, z, xbc, dt = _in_proj(xp, g_mix, w_in16, PROMPT_TILE)
    mix16, st = _mix_prompt(vp, z, xbc, dt, cw, cb, dtb, alog, dsk, nw, plw16, plb, psc, nb, seq)
    h1_p, u_p, rt_p, gate_p, cnt_p = route(xp, mix16, PROMPT_TILE)
    pool_p = vp.reshape(nb, seq, POOL_WIDTH)[:, seq - POOL_BUF:]
    conv_p = xbc.reshape(nb, seq, CONV_DIM)[:, seq - (SSD_CONV - 1):]
    ssm_p = _state_from_blocks(st)

    xs_ = x_sample.reshape(ns, D_MODEL)
    vp_s, z_s, xbc_s, dt_s = _in_proj(xs_, g_mix, w_in16, ns)
    mix_s, pool_t, conv_t, ssm_t = _mix_step(
        vp_s, z_s, xbc_s, dt_s, jnp.transpose(state_pool[0], (1, 0, 2)), jnp.transpose(state_conv[0], (1, 0, 2)),
        jnp.transpose(state_ssm[0], (1, 2, 3, 0)), cw, cb, dtb, alog, dsk, nw, plw16, plb, psc)
    h1_s, u_s, rt_s, gate_s, cnt_s = route(xs_, mix_s, ns)
    pool_s = jnp.transpose(pool_t, (1, 0, 2))
    conv_s = jnp.transpose(conv_t, (1, 0, 2))
    ssm_s = jnp.transpose(ssm_t, (3, 0, 1, 2))

    counts = [cnt_p[:, :, 0].astype(jnp.int32), cnt_s[:, :, 0].astype(jnp.int32)]
    tiles = (PROMPT_TILE, ns)
    data_rows, total_rows = _max_sorted_rows([c.shape[0] for c in counts], tiles)
    (tab_p, tab_s), tiles_cum, n_active = _sort_tables(counts, tiles, data_rows)
    xs_sorted, dest_p = _sort_tokens(tab_p[0], u_p, rt_p, tab_p[2], None, PROMPT_TILE, total_rows)
    xs_sorted, dest_s = _sort_tokens(tab_s[0], u_s, rt_s, tab_s[2], xs_sorted, ns, total_rows)
    jj = jnp.minimum(jnp.arange(data_rows // MOE_TILE, dtype=jnp.int32), n_active - 1)
    tile_expert = jnp.minimum(jnp.sum((tiles_cum[None, :] <= jj[:, None]).astype(jnp.int32), axis=1), N_EXPERTS - 1)
    ys_sorted = _moe_sorted(tile_expert, n_active.reshape(1).astype(jnp.int32), xs_sorted, wg, wu, wd, data_rows)

    y_prompt = _ple(tab_p[1], h1_p, dest_p, gate_p, p_prompt[0].reshape(nb * seq, PLE_DIM),
                    g_ple, pg16, pgb, pp16, g_fin, ys_sorted, PROMPT_TILE)
    y_sample = _ple(tab_s[1], h1_s, dest_s, gate_s, p_sample[0].reshape(ns, PLE_DIM),
                    g_ple, pg16, pgb, pp16, g_fin, ys_sorted, ns)

    return (y_prompt.reshape(nb, seq, D_MODEL), y_sample.reshape(ns, 1, D_MODEL),
            pool_p[None], conv_p[None], ssm_p[None], pool_s[None], conv_s[None], ssm_s[None])
```

```python
import functools

import jax
import jax.numpy as jnp
from jax import lax
from jax.experimental import pallas as pl
from jax.experimental.pallas import tpu as pltpu

F32 = jnp.float32
BF16 = jnp.bfloat16

D_MODEL = 1024
POOL_WIDTH = 512
POOL_WINDOWS = (2, 4, 8, 16)
POOL_GROUP_DIM = 128
POOL_BUF = 15
SSD_WIDTH = 512
SSD_HEAD_DIM = 64
SSD_HEADS = 8
SSD_GROUPS = 2
SSD_STATE = 64
SSD_CONV = 4
SSD_BC = SSD_GROUPS * SSD_STATE
CONV_DIM = SSD_WIDTH + 2 * SSD_BC
N_EXPERT_GROUPS = 4
EXPERTS_PER_GROUP = 8
N_EXPERTS = N_EXPERT_GROUPS * EXPERTS_PER_GROUP
EXPERT_FF = 256
PLE_DIM = 256
PAST_LEN = 16384
EPS = 1e-6

LANES = 128
IN_COLS = 1920
DT_OFF = POOL_WIDTH + SSD_WIDTH + CONV_DIM
VMEM_LIMIT = 56 * 1024 * 1024


def _cparams(*sem):
    return pltpu.CompilerParams(dimension_semantics=sem, vmem_limit_bytes=VMEM_LIMIT)


def _rms(x, g):
    return x * lax.rsqrt(jnp.mean(x * x, axis=-1, keepdims=True) + EPS) * g


def _sigmoid(x):
    return 1.0 / (1.0 + jnp.exp(-x))


def _silu(x):
    return x * _sigmoid(x)


def _split3(v):
    hi = v.astype(BF16)
    r = v - hi.astype(F32)
    mid = r.astype(BF16)
    lo = (r - mid.astype(F32)).astype(BF16)
    return hi, mid, lo


def _dot(a, b):
    return jnp.dot(a, b, preferred_element_type=F32)


def _dot_nt(a, b):
    return lax.dot_general(a, b, (((1,), (1,)), ((), ())), preferred_element_type=F32)


def _dot_tn(a, b):
    return lax.dot_general(a, b, (((0,), (0,)), ((), ())), preferred_element_type=F32)


def _dot_exact_lhs(sel, v, terms=3):
    acc = None
    for t in _split3(v)[:terms]:
        p = _dot(sel, t)
        acc = p if acc is None else acc + p
    return acc


def _dot_exact_rhs(v, sel, terms=3):
    acc = None
    for t in _split3(v)[:terms]:
        p = _dot(t, sel)
        acc = p if acc is None else acc + p
    return acc


def _in_proj_body(x_ref, g_ref, w_ref, vp_ref, z_ref, xbc_ref, dt_ref):
    a16 = _rms(x_ref[...], g_ref[...]).astype(BF16)
    vp_ref[...] = _dot(a16, w_ref[:, 0:POOL_WIDTH])
    z_ref[...] = _dot(a16, w_ref[:, POOL_WIDTH:POOL_WIDTH + SSD_WIDTH])
    xbc_ref[...] = _dot(a16, w_ref[:, POOL_WIDTH + SSD_WIDTH:DT_OFF])
    dt_ref[...] = _dot(a16, w_ref[:, DT_OFF:IN_COLS])


def _in_proj(x, g, w16, tm):
    t = x.shape[0]
    row = lambda i: (i, 0)
    fixed = lambda i: (0, 0)
    return pl.pallas_call(
        _in_proj_body,
        grid=(t // tm,),
        in_specs=[pl.BlockSpec((tm, D_MODEL), row), pl.BlockSpec((1, D_MODEL), fixed),
                  pl.BlockSpec((D_MODEL, IN_COLS), fixed)],
        out_specs=[pl.BlockSpec((tm, POOL_WIDTH), row), pl.BlockSpec((tm, SSD_WIDTH), row),
                   pl.BlockSpec((tm, CONV_DIM), row), pl.BlockSpec((tm, LANES), row)],
        out_shape=[jax.ShapeDtypeStruct((t, POOL_WIDTH), F32), jax.ShapeDtypeStruct((t, SSD_WIDTH), F32),
                   jax.ShapeDtypeStruct((t, CONV_DIM), F32), jax.ShapeDtypeStruct((t, LANES), F32)],
        compiler_params=_cparams("parallel"),
        name="in_proj",
    )(x, g, w16)


def _route(lg):
    tm = lg.shape[1]
    gl = lg[0:N_EXPERT_GROUPS, :]
    gmax = jnp.max(gl, axis=0, keepdims=True)
    gsum = jnp.sum(jnp.exp(gl - gmax), axis=0, keepdims=True)
    g_w = 1.0 / gsum
    gi = lax.broadcasted_iota(jnp.int32, gl.shape, 0)
    g_idx = jnp.min(jnp.where(gl == gmax, gi, N_EXPERT_GROUPS), axis=0, keepdims=True)
    sel = jnp.zeros((EXPERTS_PER_GROUP, tm), F32)
    for g in range(N_EXPERT_GROUPS):
        blk = lg[8 + g * EXPERTS_PER_GROUP:8 + (g + 1) * EXPERTS_PER_GROUP, :]
        sel = jnp.where(g_idx == g, blk, sel)
    ei = lax.broadcasted_iota(jnp.int32, sel.shape, 0)
    m1 = jnp.max(sel, axis=0, keepdims=True)
    i1 = jnp.min(jnp.where(sel == m1, ei, EXPERTS_PER_GROUP), axis=0, keepdims=True)
    rest = jnp.where(ei == i1, -jnp.inf, sel)
    m2 = jnp.max(rest, axis=0, keepdims=True)
    i2 = jnp.min(jnp.where(rest == m2, ei, EXPERTS_PER_GROUP), axis=0, keepdims=True)
    p2 = jnp.exp(m2 - m1)
    w1 = g_w / (1.0 + p2)
    w2 = g_w * p2 / (1.0 + p2)
    return g_idx * EXPERTS_PER_GROUP + i1, g_idx * EXPERTS_PER_GROUP + i2, w1, w2


def _out_proj_body(x_ref, mix_ref, w_ref, g_ref, rh_ref, rl_ref, rb_ref, h_ref, u_ref, rt_ref, wt_ref, cnt_ref):
    _out_proj_and_route(x_ref[...], mix_ref[...], w_ref, g_ref, rh_ref, rl_ref, rb_ref,
                        h_ref, u_ref, rt_ref, wt_ref, cnt_ref)


def _out_proj_and_route(x, mix16, w_ref, g_ref, rh_ref, rl_ref, rb_ref, h_ref, u_ref, rt_ref, wt_ref, cnt_ref):
    h = x + _dot(mix16, w_ref[...])
    h_ref[...] = h
    u = _rms(h, g_ref[...])
    u_hi = u.astype(BF16)
    u_ref[...] = u_hi
    u_lo = (u - u_hi.astype(F32)).astype(BF16)
    lg = _dot_nt(rh_ref[...], u_hi) + _dot_nt(rh_ref[...], u_lo) + _dot_nt(rl_ref[...], u_hi) + rb_ref[...]
    b1, b2, w1, w2 = _route(lg)
    tm = lg.shape[1]
    r8 = lax.broadcasted_iota(jnp.int32, (8, tm), 0)
    rt_ref[0] = jnp.where(r8 == 0, b1.astype(F32), jnp.where(r8 == 1, b2.astype(F32), 0.0))
    wt_ref[...] = jnp.concatenate([jnp.broadcast_to(w1, (LANES, tm)).T, jnp.broadcast_to(w2, (LANES, tm)).T], axis=1)
    kio = lax.broadcasted_iota(jnp.int32, (N_EXPERTS, tm), 0)
    hits = ((kio == b1) | (kio == b2)).astype(F32)
    cnt_ref[0] = jnp.broadcast_to(jnp.sum(hits, axis=1, keepdims=True), (N_EXPERTS, LANES))


def _out_proj(x, mix16, w16, g, r_hi, r_lo, r_bias, tm):
    t = x.shape[0]
    nt = t // tm
    row = lambda i: (i, 0)
    fixed = lambda i: (0, 0)
    tile3 = lambda i: (i, 0, 0)
    return pl.pallas_call(
        _out_proj_body,
        grid=(nt,),
        in_specs=[pl.BlockSpec((tm, D_MODEL), row), pl.BlockSpec((tm, D_MODEL), row),
                  pl.BlockSpec((D_MODEL, D_MODEL), fixed), pl.BlockSpec((1, D_MODEL), fixed),
                  pl.BlockSpec((LANES, D_MODEL), fixed), pl.BlockSpec((LANES, D_MODEL), fixed),
                  pl.BlockSpec((LANES, tm), fixed)],
        out_specs=[pl.BlockSpec((tm, D_MODEL), row), pl.BlockSpec((tm, D_MODEL), row),
                   pl.BlockSpec((1, 8, tm), tile3), pl.BlockSpec((tm, 2 * LANES), row),
                   pl.BlockSpec((1, N_EXPERTS, LANES), tile3)],
        out_shape=[jax.ShapeDtypeStruct((t, D_MODEL), F32), jax.ShapeDtypeStruct((t, D_MODEL), BF16),
                   jax.ShapeDtypeStruct((nt, 8, tm), F32), jax.ShapeDtypeStruct((t, 2 * LANES), F32),
                   jax.ShapeDtypeStruct((nt, N_EXPERTS, LANES), F32)],
        compiler_params=_cparams("parallel"),
        name="out_proj_router",
    )(x, mix16, w16, g, r_hi, r_lo, r_bias)


RUN_PAD = 16
MOE_TILE = 256


def _tile_lanes(v, width):
    reps = width // LANES
    return v if reps == 1 else jnp.concatenate([v] * reps, axis=1)


def _sorted_rows_per_tile(tm):
    need = 2 * tm + N_EXPERTS * (RUN_PAD - 1)
    return -(-need // LANES) * LANES


def _sort_body(cd_ref, u_ref, rt_ref, off_ref, *rest):
    xs_ref, dt_ref, loc, sem = rest[-4:]
    i = pl.program_id(0)
    tm = u_ref.shape[0]
    rows = loc.shape[1]
    rt = rt_ref[0]
    b1 = rt[0:1, :].astype(jnp.int32)
    b2 = rt[1:2, :].astype(jnp.int32)
    kio = lax.broadcasted_iota(jnp.int32, (N_EXPERTS, tm), 0)
    o1 = kio == b1
    o2 = kio == b2
    before = (lax.broadcasted_iota(jnp.int32, (tm, tm), 0) < lax.broadcasted_iota(jnp.int32, (tm, tm), 1))
    start = _dot((o1 | o2).astype(BF16), before.astype(BF16)) + _tile_lanes(off_ref[0], tm)
    d1 = jnp.sum(jnp.where(o1, start, 0.0), axis=0, keepdims=True)
    d2 = jnp.sum(jnp.where(o2, start, 0.0), axis=0, keepdims=True)
    dt_ref[...] = jnp.concatenate([jnp.broadcast_to(d1, (LANES, tm)).T, jnp.broadcast_to(d2, (LANES, tm)).T], axis=1)
    rio = lax.broadcasted_iota(jnp.int32, (rows, tm), 0)
    perm = ((rio == d1.astype(jnp.int32)) | (rio == d2.astype(jnp.int32))).astype(BF16)
    slot = i % 2
    loc[slot] = _dot(perm, u_ref[...]).astype(BF16)

    def chunk_copy(step, sl, c):
        return pltpu.make_async_copy(
            loc.at[sl, pl.ds(c * RUN_PAD, RUN_PAD), :],
            xs_ref.at[pl.ds(pl.multiple_of(cd_ref[step, c], RUN_PAD), RUN_PAD), :], sem.at[sl])

    n_chunk = rows // RUN_PAD
    for c in range(n_chunk):
        chunk_copy(i, slot, c).start()

    @pl.when(i > 0)
    def _():
        for c in range(n_chunk):
            chunk_copy(i - 1, 1 - slot, c).wait()

    @pl.when(i == pl.num_programs(0) - 1)
    def _():
        for c in range(n_chunk):
            chunk_copy(i, slot, c).wait()


def _sort_tokens(chunk_dst, u16, rt, off_v, xs_prev, tm, total_rows):
    t = u16.shape[0]
    rows = _sorted_rows_per_tile(tm)
    in_specs = [pl.BlockSpec((tm, D_MODEL), lambda i, cd: (i, 0)),
                pl.BlockSpec((1, 8, tm), lambda i, cd: (i, 0, 0)),
                pl.BlockSpec((1, N_EXPERTS, LANES), lambda i, cd: (i, 0, 0))]
    args = [chunk_dst, u16, rt, off_v]
    aliases = {}
    if xs_prev is not None:
        in_specs.append(pl.BlockSpec(memory_space=pl.ANY))
        aliases = {len(args): 0}
        args.append(xs_prev)
    return pl.pallas_call(
        _sort_body,
        grid_spec=pltpu.PrefetchScalarGridSpec(
            num_scalar_prefetch=1,
            grid=(t // tm,),
            in_specs=in_specs,
            out_specs=[pl.BlockSpec(memory_space=pl.ANY),
                       pl.BlockSpec((tm, 2 * LANES), lambda i, cd: (i, 0))],
            scratch_shapes=[pltpu.VMEM((2, rows, D_MODEL), BF16), pltpu.SemaphoreType.DMA((2,))]),
        out_shape=[jax.ShapeDtypeStruct((total_rows, D_MODEL), BF16),
                   jax.ShapeDtypeStruct((t, 2 * LANES), F32)],
        input_output_aliases=aliases,
        compiler_params=_cparams("arbitrary"),
        name="sort_tokens",
    )(*args)


def _moe_body(ts_ref, na_ref, xs_ref, wg_ref, wu_ref, wd_ref, ys_ref, xbuf, ybuf, wg16, wu16, wd16, sem_in, sem_out):
    k = pl.program_id(0)
    n_act = na_ref[0]

    def x_copy(g, sl):
        return pltpu.make_async_copy(xs_ref.at[pl.ds(pl.multiple_of(g * MOE_TILE, MOE_TILE), MOE_TILE), :],
                                     xbuf.at[sl], sem_in.at[sl])

    def y_copy(g, sl):
        return pltpu.make_async_copy(ybuf.at[sl],
                                     ys_ref.at[pl.ds(pl.multiple_of(g * MOE_TILE, MOE_TILE), MOE_TILE), :],
                                     sem_out.at[sl])

    @pl.when(k == 0)
    def _():
        x_copy(0, 0).start()

    wg16[...] = wg_ref[0].astype(BF16)
    wu16[...] = wu_ref[0].astype(BF16)
    wd16[...] = wd_ref[0].astype(BF16)

    def tile(g, carry):
        sl = g % 2

        @pl.when(g + 1 < n_act)
        def _():
            x_copy(g + 1, 1 - sl).start()

        x_copy(g, sl).wait()

        @pl.when(g >= 2)
        def _():
            y_copy(g - 2, sl).wait()

        x = xbuf[sl]
        act = _silu(_dot(x, wg16[...])) * _dot(x, wu16[...])
        ybuf[sl] = _dot(act.astype(BF16), wd16[...]).astype(BF16)
        y_copy(g, sl).start()
        return carry

    lax.fori_loop(ts_ref[k], ts_ref[k + 1], tile, 0)

    @pl.when(k == pl.num_programs(0) - 1)
    def _():
        @pl.when(n_act >= 2)
        def _():
            y_copy(n_act - 2, n_act % 2).wait()
        y_copy(n_act - 1, (n_act - 1) % 2).wait()


def _moe_sorted(tile_start, n_active, xs, wg, wu, wd):
    w_map = lambda k, ts, na: (k, 0, 0)
    return pl.pallas_call(
        _moe_body,
        grid_spec=pltpu.PrefetchScalarGridSpec(
            num_scalar_prefetch=2,
            grid=(N_EXPERTS,),
            in_specs=[pl.BlockSpec(memory_space=pl.ANY),
                      pl.BlockSpec((1, D_MODEL, EXPERT_FF), w_map),
                      pl.BlockSpec((1, D_MODEL, EXPERT_FF), w_map),
                      pl.BlockSpec((1, EXPERT_FF, D_MODEL), w_map)],
            out_specs=pl.BlockSpec(memory_space=pl.ANY),
            scratch_shapes=[pltpu.VMEM((2, MOE_TILE, D_MODEL), BF16), pltpu.VMEM((2, MOE_TILE, D_MODEL), BF16),
                            pltpu.VMEM((D_MODEL, EXPERT_FF), BF16), pltpu.VMEM((D_MODEL, EXPERT_FF), BF16),
                            pltpu.VMEM((EXPERT_FF, D_MODEL), BF16),
                            pltpu.SemaphoreType.DMA((2,)), pltpu.SemaphoreType.DMA((2,))]),
        out_shape=jax.ShapeDtypeStruct(xs.shape, BF16),
        compiler_params=_cparams("arbitrary"),
        name="moe_sorted",
    )(tile_start, n_active, xs, wg, wu, wd)


def _ple_body(cd_ref, h_ref, dt_ref, wt_ref, p_ref, gn_ref, wg_ref, bg_ref, wp_ref, fn_ref, ys_ref,
              o_ref, loc, sem):
    i = pl.program_id(0)
    tm = h_ref.shape[0]
    rows = loc.shape[1]
    n_chunk = rows // RUN_PAD
    slot = i % 2

    def chunk_copy(step, sl, c):
        return pltpu.make_async_copy(
            ys_ref.at[pl.ds(pl.multiple_of(cd_ref[step, c], RUN_PAD), RUN_PAD), :],
            loc.at[sl, pl.ds(c * RUN_PAD, RUN_PAD), :], sem.at[sl])

    def fetch(step, sl):
        for c in range(n_chunk):
            chunk_copy(step, sl, c).start()

    @pl.when(i == 0)
    def _():
        fetch(0, 0)

    @pl.when(i + 1 < pl.num_programs(0))
    def _():
        fetch(i + 1, 1 - slot)

    for c in range(n_chunk):
        chunk_copy(i, slot, c).wait()

    ys16 = loc[slot]
    ci = lax.broadcasted_iota(jnp.int32, (tm, LANES), 1)
    picked = []
    for k in range(2):
        dest = dt_ref[:, k * LANES:(k + 1) * LANES].astype(jnp.int32)
        sel = jnp.concatenate([(dest == ci + m * LANES) for m in range(rows // LANES)], axis=1).astype(BF16)
        picked.append(_dot(sel, ys16) * _tile_lanes(wt_ref[:, k * LANES:(k + 1) * LANES], D_MODEL))
    h = h_ref[...] + (picked[0] + picked[1])
    a16 = _rms(h, gn_ref[...]).astype(BF16)
    gate = _sigmoid(_dot(a16, wg_ref[...]) + bg_ref[...])
    pp = _dot(p_ref[...].astype(BF16), wp_ref[...])
    h = h + gate * pp
    o_ref[...] = _rms(h, fn_ref[...])


def _ple(chunk_src, h1, dest_t, gate_t, p, g_ple, wg16, bg, wp16, g_final, ys, tm):
    t = h1.shape[0]
    row = lambda i, cd: (i, 0)
    fixed = lambda i, cd: (0, 0)
    return pl.pallas_call(
        _ple_body,
        grid_spec=pltpu.PrefetchScalarGridSpec(
            num_scalar_prefetch=1,
            grid=(t // tm,),
            in_specs=[pl.BlockSpec((tm, D_MODEL), row), pl.BlockSpec((tm, 2 * LANES), row),
                      pl.BlockSpec((tm, 2 * LANES), row), pl.BlockSpec((tm, PLE_DIM), row),
                      pl.BlockSpec((1, D_MODEL), fixed), pl.BlockSpec((D_MODEL, D_MODEL), fixed),
                      pl.BlockSpec((1, D_MODEL), fixed), pl.BlockSpec((PLE_DIM, D_MODEL), fixed),
                      pl.BlockSpec((1, D_MODEL), fixed), pl.BlockSpec(memory_space=pl.ANY)],
            out_specs=pl.BlockSpec((tm, D_MODEL), row),
            scratch_shapes=[pltpu.VMEM((2, _sorted_rows_per_tile(tm), D_MODEL), BF16),
                            pltpu.SemaphoreType.DMA((2,))]),
        out_shape=jax.ShapeDtypeStruct((t, D_MODEL), F32),
        compiler_params=_cparams("arbitrary"),
        name="ple_final",
    )(chunk_src, h1, dest_t, gate_t, p, g_ple, wg16, bg, wp16, g_final, ys)


MIX_ROWS = 512
SSD_Q = 128
POOL_HDR = 128
CONV_HDR = 8
TAIL_ROWS = 16


def _softplus(x):
    return jnp.maximum(x, 0.0) + jnp.log1p(jnp.exp(-jnp.abs(x)))


def _head_expand_matrix():
    r = lax.broadcasted_iota(jnp.int32, (LANES, SSD_WIDTH), 0)
    c = lax.broadcasted_iota(jnp.int32, (LANES, SSD_WIDTH), 1)
    return (lax.shift_right_logical(c, 6) == r).astype(BF16)


def _state_block_mask():
    r = lax.broadcasted_iota(jnp.int32, (SSD_BC, SSD_WIDTH), 0)
    c = lax.broadcasted_iota(jnp.int32, (SSD_BC, SSD_WIDTH), 1)
    return (lax.shift_right_logical(r, 6) == lax.shift_right_logical(c, 8)).astype(F32)


def _front_body(x_ref, gm_ref, win_ref, cw_ref, cb_ref, dtb_ref, alog_ref, dsk_ref, nw_ref,
                plw_ref, plb_ref, psc_ref, wout_ref, gf_ref, rh_ref, rl_ref, rb_ref,
                h_ref, u_ref, rt_ref, wt_ref, cnt_ref, st_ref, ptail_ref, ctail_ref,
                pool_ext, conv_ext, s_ref, z_ref, dt_ref, mix_ref):
    c = pl.program_id(1)
    rows = MIX_ROWS
    q_len = SSD_Q

    @pl.when(c == 0)
    def _():
        pool_ext[0:POOL_HDR, :] = jnp.zeros((POOL_HDR, POOL_WIDTH), F32)
        conv_ext[0:CONV_HDR, :] = jnp.zeros((CONV_HDR, CONV_DIM), F32)
        s_ref[...] = jnp.zeros_like(s_ref)

    @pl.when(c > 0)
    def _():
        pool_ext[0:POOL_HDR, :] = pool_ext[rows:rows + POOL_HDR, :]
        conv_ext[0:CONV_HDR, :] = conv_ext[rows:rows + CONV_HDR, :]

    a16 = _rms(x_ref[...], gm_ref[...]).astype(BF16)
    pool_ext[POOL_HDR:POOL_HDR + rows, :] = _dot(a16, win_ref[:, 0:POOL_WIDTH])
    z_ref[...] = _dot(a16, win_ref[:, POOL_WIDTH:POOL_WIDTH + SSD_WIDTH])
    conv_ext[CONV_HDR:CONV_HDR + rows, :] = _dot(a16, win_ref[:, POOL_WIDTH + SSD_WIDTH:DT_OFF])
    dt_ref[...] = _dot(a16, win_ref[:, DT_OFF:IN_COLS])

    li = lax.broadcasted_iota(jnp.int32, (q_len, q_len), 0)
    si = lax.broadcasted_iota(jnp.int32, (q_len, q_len), 1)
    causal = li >= si
    tri = causal.astype(BF16)
    expand = _head_expand_matrix()
    blockmask = _state_block_mask()
    lane = lax.broadcasted_iota(jnp.int32, (q_len, LANES), 1)
    left = lane < SSD_HEAD_DIM
    a_neg = -jnp.exp(alog_ref[...])
    wl = lax.broadcasted_iota(jnp.int32, (q_len, 2 * q_len), 0)
    wj = lax.broadcasted_iota(jnp.int32, (q_len, 2 * q_len), 1)
    rowi = lax.broadcasted_iota(jnp.int32, (q_len, LANES), 0)

    for q in range(rows // q_len):
        r0 = q * q_len
        base = CONV_HDR + r0
        conv = cb_ref[...] + conv_ext[base - 3:base - 3 + q_len, :] * cw_ref[0:1, :]
        conv = conv + conv_ext[base - 2:base - 2 + q_len, :] * cw_ref[1:2, :]
        conv = conv + conv_ext[base - 1:base - 1 + q_len, :] * cw_ref[2:3, :]
        conv = conv + conv_ext[base:base + q_len, :] * cw_ref[3:4, :]
        conv = _silu(conv)
        xs = conv[:, 0:SSD_WIDTH]
        b16 = conv[:, SSD_WIDTH:SSD_WIDTH + SSD_BC].astype(BF16)
        c_all = conv[:, SSD_WIDTH + SSD_BC:CONV_DIM]
        c16 = c_all.astype(BF16)
        dt = _softplus(dt_ref[r0:r0 + q_len, :] + dtb_ref[...])
        acs = _dot_exact_lhs(tri, dt * a_neg)
        acs_t = acs.T
        dtx = _dot_exact_rhs(dt, expand, terms=2)
        acsx = _dot_exact_rhs(acs, expand)
        last_x = acsx[q_len - 1:q_len, :]
        xdt = xs * dtx
        xdt16 = xdt.astype(BF16)
        s_old = s_ref[...]
        y_off = _dot(c16, s_old.astype(BF16)) * jnp.exp(acsx)
        contrib = _dot_tn(b16, (xdt * jnp.exp(last_x - acsx)).astype(BF16)) * blockmask
        s_ref[...] = s_old * jnp.exp(last_x) + contrib
        cb = []
        for g in range(SSD_GROUPS):
            cg = jnp.where(lax.shift_right_logical(lane, 6) == g, c_all, 0.0).astype(BF16)
            cb.append(_dot_nt(cg, b16))
        z = z_ref[r0:r0 + q_len, :]
        y_blocks = []
        for j in range(SSD_HEADS // 2):
            blk = slice(j * LANES, (j + 1) * LANES)
            zs = []
            for h in (2 * j, 2 * j + 1):
                seg = acs[:, h:h + 1] - acs_t[h:h + 1, :]
                decay = jnp.where(causal, jnp.exp(seg), 0.0)
                scores = (cb[h // (SSD_HEADS // SSD_GROUPS)] * decay).astype(BF16)
                zs.append(_dot(scores, xdt16[:, blk]))
            y = jnp.where(left, zs[0], zs[1]) + y_off[:, blk]
            y = y + dsk_ref[:, blk] * xs[:, blk]
            y_blocks.append(y * _silu(z[:, blk]))
        for g in range(SSD_GROUPS):
            y0, y1 = y_blocks[2 * g], y_blocks[2 * g + 1]
            ss = jnp.sum(y0 * y0, axis=-1, keepdims=True) + jnp.sum(y1 * y1, axis=-1, keepdims=True)
            rs = lax.rsqrt(ss * (1.0 / (2 * LANES)) + EPS)
            for k, yk in ((2 * g, y0), (2 * g + 1, y1)):
                blk = slice(k * LANES, (k + 1) * LANES)
                out = yk * rs * nw_ref[:, blk]
                mix_ref[r0:r0 + q_len, POOL_WIDTH + k * LANES:POOL_WIDTH + (k + 1) * LANES] = out.astype(BF16)
        pos = c * rows + r0 + rowi
        for g, w in enumerate(POOL_WINDOWS):
            blk = slice(g * POOL_GROUP_DIM, (g + 1) * POOL_GROUP_DIM)
            pe = pool_ext[r0:r0 + 2 * q_len, blk]
            band = ((wj <= wl + POOL_HDR) & (wj > wl + POOL_HDR - w)).astype(BF16)
            winsum = _dot_exact_lhs(band, pe, terms=2)
            cnt = jnp.minimum(pos + 1, w).astype(F32)
            m = winsum / cnt - pe[q_len:2 * q_len, :]
            yg = _dot(m.astype(BF16), plw_ref[g]) + plb_ref[:, blk]
            mix_ref[r0:r0 + q_len, blk] = (yg * psc_ref[:, blk]).astype(BF16)

    _out_proj_and_route(x_ref[...], mix_ref[...], wout_ref, gf_ref, rh_ref, rl_ref, rb_ref,
                        h_ref, u_ref, rt_ref, wt_ref, cnt_ref)

    @pl.when(c == pl.num_programs(1) - 1)
    def _():
        st_ref[0] = s_ref[...]
        ptail_ref[0] = pool_ext[POOL_HDR + rows - TAIL_ROWS:POOL_HDR + rows, :]
        ctail_ref[0] = conv_ext[rows:rows + CONV_HDR, :]


def _front_prompt(x, g_mix, w_in16, cw, cb, dtb, alog, dsk, nw, plw16, plb, psc, w_out16, g_ffn, r_hi, r_lo, r_bias,
                  nb, seq):
    steps = seq // MIX_ROWS
    nt = nb * steps
    row = lambda b, c: (b * steps + c, 0)
    tile3 = lambda b, c: (b * steps + c, 0, 0)
    per_seq = lambda b, c: (b, 0, 0)
    fixed2 = lambda b, c: (0, 0)
    return pl.pallas_call(
        _front_body,
        grid=(nb, steps),
        in_specs=[pl.BlockSpec((MIX_ROWS, D_MODEL), row), pl.BlockSpec((1, D_MODEL), fixed2),
                  pl.BlockSpec((D_MODEL, IN_COLS), fixed2),
                  pl.BlockSpec((SSD_CONV, CONV_DIM), fixed2), pl.BlockSpec((1, CONV_DIM), fixed2),
                  pl.BlockSpec((1, LANES), fixed2), pl.BlockSpec((1, LANES), fixed2),
                  pl.BlockSpec((1, SSD_WIDTH), fixed2), pl.BlockSpec((1, SSD_WIDTH), fixed2),
                  pl.BlockSpec((len(POOL_WINDOWS), POOL_GROUP_DIM, POOL_GROUP_DIM), lambda b, c: (0, 0, 0)),
                  pl.BlockSpec((1, POOL_WIDTH), fixed2), pl.BlockSpec((1, POOL_WIDTH), fixed2),
                  pl.BlockSpec((D_MODEL, D_MODEL), fixed2), pl.BlockSpec((1, D_MODEL), fixed2),
                  pl.BlockSpec((LANES, D_MODEL), fixed2), pl.BlockSpec((LANES, D_MODEL), fixed2),
                  pl.BlockSpec((LANES, MIX_ROWS), fixed2)],
        out_specs=[pl.BlockSpec((MIX_ROWS, D_MODEL), row), pl.BlockSpec((MIX_ROWS, D_MODEL), row),
                   pl.BlockSpec((1, 8, MIX_ROWS), tile3), pl.BlockSpec((MIX_ROWS, 2 * LANES), row),
                   pl.BlockSpec((1, N_EXPERTS, LANES), tile3),
                   pl.BlockSpec((1, SSD_BC, SSD_WIDTH), per_seq),
                   pl.BlockSpec((1, TAIL_ROWS, POOL_WIDTH), per_seq),
                   pl.BlockSpec((1, CONV_HDR, CONV_DIM), per_seq)],
        out_shape=[jax.ShapeDtypeStruct((nb * seq, D_MODEL), F32), jax.ShapeDtypeStruct((nb * seq, D_MODEL), BF16),
                   jax.ShapeDtypeStruct((nt, 8, MIX_ROWS), F32), jax.ShapeDtypeStruct((nb * seq, 2 * LANES), F32),
                   jax.ShapeDtypeStruct((nt, N_EXPERTS, LANES), F32),
                   jax.ShapeDtypeStruct((nb, SSD_BC, SSD_WIDTH), F32),
                   jax.ShapeDtypeStruct((nb, TAIL_ROWS, POOL_WIDTH), F32),
                   jax.ShapeDtypeStruct((nb, CONV_HDR, CONV_DIM), F32)],
        scratch_shapes=[pltpu.VMEM((POOL_HDR + MIX_ROWS, POOL_WIDTH), F32),
                        pltpu.VMEM((CONV_HDR + MIX_ROWS, CONV_DIM), F32),
                        pltpu.VMEM((SSD_BC, SSD_WIDTH), F32),
                        pltpu.VMEM((MIX_ROWS, SSD_WIDTH), F32), pltpu.VMEM((MIX_ROWS, LANES), F32),
                        pltpu.VMEM((MIX_ROWS, D_MODEL), BF16)],
        compiler_params=_cparams("parallel", "arbitrary"),
        name="front_prompt",
    )(x, g_mix, w_in16, cw, cb, dtb, alog, dsk, nw, plw16, plb, psc, w_out16, g_ffn, r_hi, r_lo, r_bias)


def _mix_step_body(vp_ref, z_ref, xbc_ref, dt_ref, sp_ref, sc_ref, st_ref, cw_ref, cb_ref, dtb_ref, alog_ref,
                   dsk_ref, nw_ref, plw_ref, plb_ref, psc_ref, mix_ref, po_ref, co_ref, so_ref,
                   xdt_t, dec_t, b_t, c_t, xs_keep, y_t):
    h = pl.program_id(0)

    @pl.when(h == 0)
    def _():
        xbc = xbc_ref[...]
        conv = cb_ref[...] + sc_ref[0] * cw_ref[0:1, :]
        conv = conv + sc_ref[1] * cw_ref[1:2, :]
        conv = conv + sc_ref[2] * cw_ref[2:3, :]
        conv = conv + xbc * cw_ref[3:4, :]
        conv = _silu(conv)
        co_ref[0] = sc_ref[1]
        co_ref[1] = sc_ref[2]
        co_ref[2] = xbc
        xs = conv[:, 0:SSD_WIDTH]
        xs_keep[...] = xs
        b_t[...] = conv[:, SSD_WIDTH:SSD_WIDTH + SSD_BC].T
        c_t[...] = conv[:, SSD_WIDTH + SSD_BC:CONV_DIM].T
        dt = _softplus(dt_ref[...] + dtb_ref[...])
        d_a = dt * (-jnp.exp(alog_ref[...]))
        dt_t = dt.T
        dec_t[...] = jnp.exp(d_a).T
        xs_t = xs.T
        for k in range(SSD_HEADS):
            blk = slice(k * SSD_HEAD_DIM, (k + 1) * SSD_HEAD_DIM)
            xdt_t[blk, :] = xs_t[blk, :] * dt_t[k:k + 1, :]
        v = vp_ref[...]
        for k in range(POOL_BUF - 1):
            po_ref[k] = sp_ref[k + 1]
        po_ref[POOL_BUF - 1] = v
        for g, w in enumerate(POOL_WINDOWS):
            blk = slice(g * POOL_GROUP_DIM, (g + 1) * POOL_GROUP_DIM)
            acc = sp_ref[POOL_BUF - (w - 1), :, blk]
            for k in range(w - 2, 0, -1):
                acc = acc + sp_ref[POOL_BUF - k, :, blk]
            acc = acc + v[:, blk]
            m = acc / float(min(PAST_LEN + 1, w)) - v[:, blk]
            yg = _dot(m.astype(BF16), plw_ref[g]) + plb_ref[:, blk]
            mix_ref[:, blk] = (yg * psc_ref[:, blk]).astype(BF16)

    g_off = pl.multiple_of((h // (SSD_HEADS // SSD_GROUPS)) * SSD_STATE, SSD_STATE)
    h_off = pl.multiple_of(h * SSD_HEAD_DIM, SSD_HEAD_DIM)
    b_g = b_t[pl.ds(g_off, SSD_STATE), :]
    c_g = c_t[pl.ds(g_off, SSD_STATE), :]
    dec = dec_t[pl.ds(h, 1), :]
    xdt = xdt_t[pl.ds(h_off, SSD_HEAD_DIM), :]
    y_rows = []
    for p in range(SSD_HEAD_DIM):
        s_new = st_ref[0, p] * dec + xdt[p:p + 1, :] * b_g
        so_ref[0, p] = s_new
        y_rows.append(jnp.sum(s_new * c_g, axis=0, keepdims=True))
    y_t[pl.ds(h_off, SSD_HEAD_DIM), :] = jnp.concatenate(y_rows, axis=0)

    @pl.when(h == pl.num_programs(0) - 1)
    def _():
        xs = xs_keep[...]
        y = y_t[...].T + dsk_ref[...] * xs
        y = y * _silu(z_ref[...])
        width = SSD_WIDTH // SSD_GROUPS
        for g in range(SSD_GROUPS):
            blk = slice(g * width, (g + 1) * width)
            yg = y[:, blk]
            rs = lax.rsqrt(jnp.mean(yg * yg, axis=-1, keepdims=True) + EPS)
            mix_ref[:, POOL_WIDTH + g * width:POOL_WIDTH + (g + 1) * width] = (yg * rs * nw_ref[:, blk]).astype(BF16)


def _mix_step(vp, z, xbc, dt, pool_t, conv_t, ssm_t, cw, cb, dtb, alog, dsk, nw, plw16, plb, psc):
    n = vp.shape[0]
    fixed2 = lambda h: (0, 0)
    fixed3 = lambda h: (0, 0, 0)
    st_spec = pl.BlockSpec((1, SSD_HEAD_DIM, SSD_STATE, n), lambda h: (h, 0, 0, 0))
    return pl.pallas_call(
        _mix_step_body,
        grid=(SSD_HEADS,),
        in_specs=[pl.BlockSpec((n, POOL_WIDTH), fixed2), pl.BlockSpec((n, SSD_WIDTH), fixed2),
                  pl.BlockSpec((n, CONV_DIM), fixed2), pl.BlockSpec((n, LANES), fixed2),
                  pl.BlockSpec((POOL_BUF, n, POOL_WIDTH), fixed3),
                  pl.BlockSpec((SSD_CONV - 1, n, CONV_DIM), fixed3),
                  st_spec,
                  pl.BlockSpec((SSD_CONV, CONV_DIM), fixed2), pl.BlockSpec((1, CONV_DIM), fixed2),
                  pl.BlockSpec((1, LANES), fixed2), pl.BlockSpec((1, LANES), fixed2),
                  pl.BlockSpec((1, SSD_WIDTH), fixed2), pl.BlockSpec((1, SSD_WIDTH), fixed2),
                  pl.BlockSpec((len(POOL_WINDOWS), POOL_GROUP_DIM, POOL_GROUP_DIM), fixed3),
                  pl.BlockSpec((1, POOL_WIDTH), fixed2), pl.BlockSpec((1, POOL_WIDTH), fixed2)],
        out_specs=[pl.BlockSpec((n, D_MODEL), fixed2), pl.BlockSpec((POOL_BUF, n, POOL_WIDTH), fixed3),
                   pl.BlockSpec((SSD_CONV - 1, n, CONV_DIM), fixed3), st_spec],
        out_shape=[jax.ShapeDtypeStruct((n, D_MODEL), BF16), jax.ShapeDtypeStruct(pool_t.shape, F32),
                   jax.ShapeDtypeStruct(conv_t.shape, F32), jax.ShapeDtypeStruct(ssm_t.shape, F32)],
        scratch_shapes=[pltpu.VMEM((SSD_WIDTH, n), F32), pltpu.VMEM((LANES, n), F32),
                        pltpu.VMEM((SSD_BC, n), F32), pltpu.VMEM((SSD_BC, n), F32),
                        pltpu.VMEM((n, SSD_WIDTH), F32), pltpu.VMEM((SSD_WIDTH, n), F32)],
        compiler_params=_cparams("arbitrary"),
        name="mix_step",
    )(vp, z, xbc, dt, pool_t, conv_t, ssm_t, cw, cb, dtb, alog, dsk, nw, plw16, plb, psc)


PROMPT_TILE = 512


def _sort_tables(counts, tile_tokens, data_rows):
    cnt = jnp.concatenate(counts, axis=0)
    pc = (cnt + RUN_PAD - 1) // RUN_PAD * RUN_PAD
    off_local = jnp.cumsum(pc, axis=1) - pc
    tile_rows = jnp.sum(pc, axis=1)
    region = (jnp.sum(pc, axis=0) + MOE_TILE - 1) // MOE_TILE * MOE_TILE
    base = jnp.cumsum(region) - region
    dst = base[None, :] + jnp.cumsum(pc, axis=0) - pc
    per_pass = []
    lo = 0
    for c, tm in zip(counts, tile_tokens):
        hi = lo + c.shape[0]
        n_chunk = _sorted_rows_per_tile(tm) // RUN_PAD
        s = jnp.arange(n_chunk, dtype=jnp.int32) * RUN_PAD
        begins = off_local[lo:hi, None, :]
        ends = begins + pc[lo:hi, None, :]
        inside = (begins <= s[None, :, None]) & (s[None, :, None] < ends)
        shift = jnp.sum(jnp.where(inside, dst[lo:hi, None, :] - begins, 0), axis=2)
        valid = s[None, :] < tile_rows[lo:hi, None]
        parity = (jnp.arange(hi - lo, dtype=jnp.int32) % 2)[:, None]
        spare = data_rows + parity * _sorted_rows_per_tile(max(tile_tokens)) + s[None, :]
        scatter_dst = jnp.where(valid, s[None, :] + shift, spare).astype(jnp.int32)
        gather_src = jnp.where(valid, s[None, :] + shift, 0).astype(jnp.int32)
        off_v = jnp.broadcast_to(off_local[lo:hi, :, None].astype(F32), (hi - lo, N_EXPERTS, LANES))
        per_pass.append((scatter_dst, gather_src, off_v))
        lo = hi
    tiles_cum = jnp.cumsum(region // MOE_TILE)
    n_active = tiles_cum[-1]
    return per_pass, tiles_cum, n_active


def _max_sorted_rows(tile_counts, tile_tokens):
    rows = sum(n * (2 * tm + N_EXPERTS * (RUN_PAD - 1)) for n, tm in zip(tile_counts, tile_tokens))
    data_rows = (-(-rows // MOE_TILE) + N_EXPERTS) * MOE_TILE
    return data_rows, data_rows + 2 * _sorted_rows_per_tile(max(tile_tokens))


def _row(v):
    return v.reshape(1, -1).astype(F32)


def _pad_lanes(v):
    return jnp.pad(v.reshape(1, -1).astype(F32), ((0, 0), (0, LANES - v.size)))


def _state_from_blocks(st):
    n = st.shape[0]
    hpg = SSD_HEADS // SSD_GROUPS
    s6 = st.reshape(n, SSD_GROUPS, SSD_STATE, SSD_GROUPS, hpg, SSD_HEAD_DIM)
    per_group = [s6[:, g, :, g] for g in range(SSD_GROUPS)]
    s = jnp.stack(per_group, axis=1)
    return jnp.transpose(s, (0, 1, 3, 4, 2)).reshape(n, SSD_HEADS, SSD_HEAD_DIM, SSD_STATE)


def kernel(x_prompt, x_sample, p_prompt, p_sample, state_pool, state_conv, state_ssm, norm_mix, w_in, pool_lin_w, pool_lin_b, pool_scale, conv_w, conv_b, dt_bias, a_log, d_skip, ssd_norm, w_out, norm_ffn, router_grp_w, router_grp_b, router_exp_w, router_exp_b, exp_w_gate, exp_w_up, exp_w_down, norm_ple, ple_gate_w, ple_gate_b, ple_proj_w, norm_final):
    nb, seq, _ = x_prompt.shape
    ns = x_sample.shape[0]
    assert ns == LANES and x_sample.shape[1] == 1 and seq % MIX_ROWS == 0 and seq >= POOL_BUF

    w_in16 = jnp.pad(w_in[0], ((0, 0), (0, IN_COLS - w_in.shape[2]))).astype(BF16)
    w_out16 = w_out[0].astype(BF16)
    g_mix, g_ffn, g_ple, g_fin = _row(norm_mix[0]), _row(norm_ffn[0]), _row(norm_ple[0]), _row(norm_final)
    cw, cb = conv_w[0].astype(F32), _row(conv_b[0])
    dtb, alog = _pad_lanes(dt_bias[0]), _pad_lanes(a_log[0])
    dsk = _row(jnp.repeat(d_skip[0], SSD_HEAD_DIM))
    nw = _row(ssd_norm[0])
    plw16 = pool_lin_w[0].astype(BF16)
    plb, psc = _row(pool_lin_b[0]), _row(pool_scale[0])
    zeros4 = jnp.zeros((D_MODEL, 8 - N_EXPERT_GROUPS), F32)
    r_w = jnp.concatenate([router_grp_w[0], zeros4, router_exp_w[0],
                           jnp.zeros((D_MODEL, LANES - 8 - N_EXPERTS), F32)], axis=1).T
    r_hi = r_w.astype(BF16)
    r_lo = (r_w - r_hi.astype(F32)).astype(BF16)
    r_b = jnp.concatenate([router_grp_b[0], jnp.zeros((8 - N_EXPERT_GROUPS,), F32), router_exp_b[0],
                           jnp.zeros((LANES - 8 - N_EXPERTS,), F32)])
    wg = exp_w_gate[0].reshape(N_EXPERTS, D_MODEL, EXPERT_FF)
    wu = exp_w_up[0].reshape(N_EXPERTS, D_MODEL, EXPERT_FF)
    wd = exp_w_down[0].reshape(N_EXPERTS, EXPERT_FF, D_MODEL)
    pg16 = ple_gate_w[0].astype(BF16)
    pgb = _row(ple_gate_b[0])
    pp16 = ple_proj_w[0].astype(BF16)

    def route(x, mix16, tm):
        r_bias = jnp.broadcast_to(r_b[:, None], (LANES, tm))
        return _out_proj(x, mix16, w_out16, g_ffn, r_hi, r_lo, r_bias, tm)

    xp = x_prompt.reshape(nb * seq, D_MODEL)
    h1_p, u_p, rt_p, gate_p, cnt_p, st, pool_tail, conv_tail = _front_prompt(
        xp, g_mix, w_in16, cw, cb, dtb, alog, dsk, nw, plw16, plb, psc, w_out16, g_ffn, r_hi, r_lo,
        jnp.broadcast_to(r_b[:, None], (LANES, MIX_ROWS)), nb, seq)
    pool_p = pool_tail[:, TAIL_ROWS - POOL_BUF:]
    conv_p = conv_tail[:, CONV_HDR - (SSD_CONV - 1):]
    ssm_p = _state_from_blocks(st)

    xs_ = x_sample.reshape(ns, D_MODEL)
    vp_s, z_s, xbc_s, dt_s = _in_proj(xs_, g_mix, w_in16, ns)
    mix_s, pool_t, conv_t, ssm_t = _mix_step(
        vp_s, z_s, xbc_s, dt_s, jnp.transpose(state_pool[0], (1, 0, 2)), jnp.transpose(state_conv[0], (1, 0, 2)),
        jnp.transpose(state_ssm[0], (1, 2, 3, 0)), cw, cb, dtb, alog, dsk, nw, plw16, plb, psc)
    h1_s, u_s, rt_s, gate_s, cnt_s = route(xs_, mix_s, ns)
    pool_s = jnp.transpose(pool_t, (1, 0, 2))
    conv_s = jnp.transpose(conv_t, (1, 0, 2))
    ssm_s = jnp.transpose(ssm_t, (3, 0, 1, 2))

    counts = [cnt_p[:, :, 0].astype(jnp.int32), cnt_s[:, :, 0].astype(jnp.int32)]
    tiles = (PROMPT_TILE, ns)
    data_rows, total_rows = _max_sorted_rows([c.shape[0] for c in counts], tiles)
    (tab_p, tab_s), tiles_cum, n_active = _sort_tables(counts, tiles, data_rows)
    xs_sorted, dest_p = _sort_tokens(tab_p[0], u_p, rt_p, tab_p[2], None, PROMPT_TILE, total_rows)
    xs_sorted, dest_s = _sort_tokens(tab_s[0], u_s, rt_s, tab_s[2], xs_sorted, ns, total_rows)
    tile_start = jnp.concatenate([jnp.zeros((1,), jnp.int32), tiles_cum.astype(jnp.int32)])
    ys_sorted = _moe_sorted(tile_start, n_active.reshape(1).astype(jnp.int32), xs_sorted, wg, wu, wd)

    y_prompt = _ple(tab_p[1], h1_p, dest_p, gate_p, p_prompt[0].reshape(nb * seq, PLE_DIM),
                    g_ple, pg16, pgb, pp16, g_fin, ys_sorted, PROMPT_TILE)
    y_sample = _ple(tab_s[1], h1_s, dest_s, gate_s, p_sample[0].reshape(ns, PLE_DIM),
                    g_ple, pg16, pgb, pp16, g_fin, ys_sorted, ns)

    return (y_prompt.reshape(nb, seq, D_MODEL), y_sample.reshape(ns, 1, D_MODEL),
            pool_p[None], conv_p[None], ssm_p[None], pool_s[None], conv_s[None], ssm_s[None])
```

```python
import functools

import jax
import jax.numpy as jnp
from jax import lax
from jax.experimental import pallas as pl
from jax.experimental.pallas import tpu as pltpu

F32 = jnp.float32
BF16 = jnp.bfloat16

D_MODEL = 1024
POOL_WIDTH = 512
POOL_WINDOWS = (2, 4, 8, 16)
POOL_GROUP_DIM = 128
POOL_BUF = 15
SSD_WIDTH = 512
SSD_HEAD_DIM = 64
SSD_HEADS = 8
SSD_GROUPS = 2
SSD_STATE = 64
SSD_CONV = 4
SSD_BC = SSD_GROUPS * SSD_STATE
CONV_DIM = SSD_WIDTH + 2 * SSD_BC
N_EXPERT_GROUPS = 4
EXPERTS_PER_GROUP = 8
N_EXPERTS = N_EXPERT_GROUPS * EXPERTS_PER_GROUP
EXPERT_FF = 256
PLE_DIM = 256
PAST_LEN = 16384
EPS = 1e-6

LANES = 128
IN_COLS = 1920
DT_OFF = POOL_WIDTH + SSD_WIDTH + CONV_DIM
VMEM_LIMIT = 56 * 1024 * 1024


def _cparams(*sem):
    return pltpu.CompilerParams(dimension_semantics=sem, vmem_limit_bytes=VMEM_LIMIT)


def _rms(x, g):
    return x * lax.rsqrt(jnp.mean(x * x, axis=-1, keepdims=True) + EPS) * g


def _sigmoid(x):
    return 1.0 / (1.0 + jnp.exp(-x))


def _silu(x):
    return x * _sigmoid(x)


def _split3(v):
    hi = v.astype(BF16)
    r = v - hi.astype(F32)
    mid = r.astype(BF16)
    lo = (r - mid.astype(F32)).astype(BF16)
    return hi, mid, lo


def _dot(a, b):
    return jnp.dot(a, b, preferred_element_type=F32)


def _dot_nt(a, b):
    return lax.dot_general(a, b, (((1,), (1,)), ((), ())), preferred_element_type=F32)


def _dot_tn(a, b):
    return lax.dot_general(a, b, (((0,), (0,)), ((), ())), preferred_element_type=F32)


def _dot_exact_lhs(sel, v, terms=3):
    acc = None
    for t in _split3(v)[:terms]:
        p = _dot(sel, t)
        acc = p if acc is None else acc + p
    return acc


def _dot_exact_rhs(v, sel, terms=3):
    acc = None
    for t in _split3(v)[:terms]:
        p = _dot(t, sel)
        acc = p if acc is None else acc + p
    return acc


def _in_proj_body(x_ref, g_ref, w_ref, vp_ref, z_ref, xbc_ref, dt_ref):
    a16 = _rms(x_ref[...], g_ref[...]).astype(BF16)
    vp_ref[...] = _dot(a16, w_ref[:, 0:POOL_WIDTH])
    z_ref[...] = _dot(a16, w_ref[:, POOL_WIDTH:POOL_WIDTH + SSD_WIDTH])
    xbc_ref[...] = _dot(a16, w_ref[:, POOL_WIDTH + SSD_WIDTH:DT_OFF])
    dt_ref[...] = _dot(a16, w_ref[:, DT_OFF:IN_COLS])


def _in_proj(x, g, w16, tm):
    t = x.shape[0]
    row = lambda i: (i, 0)
    fixed = lambda i: (0, 0)
    return pl.pallas_call(
        _in_proj_body,
        grid=(t // tm,),
        in_specs=[pl.BlockSpec((tm, D_MODEL), row), pl.BlockSpec((1, D_MODEL), fixed),
                  pl.BlockSpec((D_MODEL, IN_COLS), fixed)],
        out_specs=[pl.BlockSpec((tm, POOL_WIDTH), row), pl.BlockSpec((tm, SSD_WIDTH), row),
                   pl.BlockSpec((tm, CONV_DIM), row), pl.BlockSpec((tm, LANES), row)],
        out_shape=[jax.ShapeDtypeStruct((t, POOL_WIDTH), F32), jax.ShapeDtypeStruct((t, SSD_WIDTH), F32),
                   jax.ShapeDtypeStruct((t, CONV_DIM), F32), jax.ShapeDtypeStruct((t, LANES), F32)],
        compiler_params=_cparams("parallel"),
        name="in_proj",
    )(x, g, w16)


def _route(lg):
    tm = lg.shape[1]
    gl = lg[0:N_EXPERT_GROUPS, :]
    gmax = jnp.max(gl, axis=0, keepdims=True)
    gsum = jnp.sum(jnp.exp(gl - gmax), axis=0, keepdims=True)
    g_w = 1.0 / gsum
    gi = lax.broadcasted_iota(jnp.int32, gl.shape, 0)
    g_idx = jnp.min(jnp.where(gl == gmax, gi, N_EXPERT_GROUPS), axis=0, keepdims=True)
    sel = jnp.zeros((EXPERTS_PER_GROUP, tm), F32)
    for g in range(N_EXPERT_GROUPS):
        blk = lg[8 + g * EXPERTS_PER_GROUP:8 + (g + 1) * EXPERTS_PER_GROUP, :]
        sel = jnp.where(g_idx == g, blk, sel)
    ei = lax.broadcasted_iota(jnp.int32, sel.shape, 0)
    m1 = jnp.max(sel, axis=0, keepdims=True)
    i1 = jnp.min(jnp.where(sel == m1, ei, EXPERTS_PER_GROUP), axis=0, keepdims=True)
    rest = jnp.where(ei == i1, -jnp.inf, sel)
    m2 = jnp.max(rest, axis=0, keepdims=True)
    i2 = jnp.min(jnp.where(rest == m2, ei, EXPERTS_PER_GROUP), axis=0, keepdims=True)
    p2 = jnp.exp(m2 - m1)
    w1 = g_w / (1.0 + p2)
    w2 = g_w * p2 / (1.0 + p2)
    return g_idx * EXPERTS_PER_GROUP + i1, g_idx * EXPERTS_PER_GROUP + i2, w1, w2


def _out_proj_body(x_ref, mix_ref, w_ref, g_ref, rh_ref, rl_ref, rb_ref, h_ref, u_ref, rt_ref, wt_ref, cnt_ref):
    _out_proj_and_route(x_ref[...], mix_ref[...], w_ref, g_ref, rh_ref, rl_ref, rb_ref,
                        h_ref, u_ref, rt_ref, wt_ref, cnt_ref)


def _out_proj_and_route(x, mix16, w_ref, g_ref, rh_ref, rl_ref, rb_ref, h_ref, u_ref, rt_ref, wt_ref, cnt_ref):
    h = x + _dot(mix16, w_ref[...])
    h_ref[...] = h
    u = _rms(h, g_ref[...])
    u_hi = u.astype(BF16)
    u_ref[...] = u_hi
    u_lo = (u - u_hi.astype(F32)).astype(BF16)
    lg = _dot_nt(rh_ref[...], u_hi) + _dot_nt(rh_ref[...], u_lo) + _dot_nt(rl_ref[...], u_hi) + rb_ref[...]
    b1, b2, w1, w2 = _route(lg)
    tm = lg.shape[1]
    r8 = lax.broadcasted_iota(jnp.int32, (8, tm), 0)
    rt_ref[0] = jnp.where(r8 == 0, b1.astype(F32), jnp.where(r8 == 1, b2.astype(F32), 0.0))
    wt_ref[...] = jnp.concatenate([jnp.broadcast_to(w1, (LANES, tm)).T, jnp.broadcast_to(w2, (LANES, tm)).T], axis=1)
    kio = lax.broadcasted_iota(jnp.int32, (N_EXPERTS, tm), 0)
    hits = ((kio == b1) | (kio == b2)).astype(F32)
    cnt_ref[0] = jnp.broadcast_to(jnp.sum(hits, axis=1, keepdims=True), (N_EXPERTS, LANES))


def _out_proj(x, mix16, w16, g, r_hi, r_lo, r_bias, tm):
    t = x.shape[0]
    nt = t // tm
    row = lambda i: (i, 0)
    fixed = lambda i: (0, 0)
    tile3 = lambda i: (i, 0, 0)
    return pl.pallas_call(
        _out_proj_body,
        grid=(nt,),
        in_specs=[pl.BlockSpec((tm, D_MODEL), row), pl.BlockSpec((tm, D_MODEL), row),
                  pl.BlockSpec((D_MODEL, D_MODEL), fixed), pl.BlockSpec((1, D_MODEL), fixed),
                  pl.BlockSpec((LANES, D_MODEL), fixed), pl.BlockSpec((LANES, D_MODEL), fixed),
                  pl.BlockSpec((LANES, tm), fixed)],
        out_specs=[pl.BlockSpec((tm, D_MODEL), row), pl.BlockSpec((tm, D_MODEL), row),
                   pl.BlockSpec((1, 8, tm), tile3), pl.BlockSpec((tm, 2 * LANES), row),
                   pl.BlockSpec((1, N_EXPERTS, LANES), tile3)],
        out_shape=[jax.ShapeDtypeStruct((t, D_MODEL), F32), jax.ShapeDtypeStruct((t, D_MODEL), BF16),
                   jax.ShapeDtypeStruct((nt, 8, tm), F32), jax.ShapeDtypeStruct((t, 2 * LANES), F32),
                   jax.ShapeDtypeStruct((nt, N_EXPERTS, LANES), F32)],
        compiler_params=_cparams("parallel"),
        name="out_proj_router",
    )(x, mix16, w16, g, r_hi, r_lo, r_bias)


RUN_PAD = 16
MOE_TILE = 256
MOE_BUFS = 4
MOE_AHEAD = 3


def _tile_lanes(v, width):
    reps = width // LANES
    return v if reps == 1 else jnp.concatenate([v] * reps, axis=1)


def _sorted_rows_per_tile(tm):
    need = 2 * tm + N_EXPERTS * (RUN_PAD - 1)
    return -(-need // LANES) * LANES


def _sort_body(cd_ref, u_ref, rt_ref, off_ref, *rest):
    xs_ref, dt_ref, loc, sem = rest[-4:]
    i = pl.program_id(0)
    tm = u_ref.shape[0]
    rows = loc.shape[1]
    rt = rt_ref[0]
    b1 = rt[0:1, :].astype(jnp.int32)
    b2 = rt[1:2, :].astype(jnp.int32)
    kio = lax.broadcasted_iota(jnp.int32, (N_EXPERTS, tm), 0)
    o1 = kio == b1
    o2 = kio == b2
    before = (lax.broadcasted_iota(jnp.int32, (tm, tm), 0) < lax.broadcasted_iota(jnp.int32, (tm, tm), 1))
    start = _dot((o1 | o2).astype(BF16), before.astype(BF16)) + _tile_lanes(off_ref[0], tm)
    d1 = jnp.sum(jnp.where(o1, start, 0.0), axis=0, keepdims=True)
    d2 = jnp.sum(jnp.where(o2, start, 0.0), axis=0, keepdims=True)
    dt_ref[...] = jnp.concatenate([jnp.broadcast_to(d1, (LANES, tm)).T, jnp.broadcast_to(d2, (LANES, tm)).T], axis=1)
    rio = lax.broadcasted_iota(jnp.int32, (rows, tm), 0)
    perm = ((rio == d1.astype(jnp.int32)) | (rio == d2.astype(jnp.int32))).astype(BF16)
    slot = i % 2
    loc[slot] = _dot(perm, u_ref[...]).astype(BF16)

    def chunk_copy(step, sl, c):
        return pltpu.make_async_copy(
            loc.at[sl, pl.ds(c * RUN_PAD, RUN_PAD), :],
            xs_ref.at[pl.ds(pl.multiple_of(cd_ref[step, c], RUN_PAD), RUN_PAD), :], sem.at[sl])

    n_chunk = rows // RUN_PAD
    for c in range(n_chunk):
        chunk_copy(i, slot, c).start()

    @pl.when(i > 0)
    def _():
        for c in range(n_chunk):
            chunk_copy(i - 1, 1 - slot, c).wait()

    @pl.when(i == pl.num_programs(0) - 1)
    def _():
        for c in range(n_chunk):
            chunk_copy(i, slot, c).wait()


def _sort_tokens(chunk_dst, u16, rt, off_v, xs_prev, tm, total_rows):
    t = u16.shape[0]
    rows = _sorted_rows_per_tile(tm)
    in_specs = [pl.BlockSpec((tm, D_MODEL), lambda i, cd: (i, 0)),
                pl.BlockSpec((1, 8, tm), lambda i, cd: (i, 0, 0)),
                pl.BlockSpec((1, N_EXPERTS, LANES), lambda i, cd: (i, 0, 0))]
    args = [chunk_dst, u16, rt, off_v]
    aliases = {}
    if xs_prev is not None:
        in_specs.append(pl.BlockSpec(memory_space=pl.ANY))
        aliases = {len(args): 0}
        args.append(xs_prev)
    return pl.pallas_call(
        _sort_body,
        grid_spec=pltpu.PrefetchScalarGridSpec(
            num_scalar_prefetch=1,
            grid=(t // tm,),
            in_specs=in_specs,
            out_specs=[pl.BlockSpec(memory_space=pl.ANY),
                       pl.BlockSpec((tm, 2 * LANES), lambda i, cd: (i, 0))],
            scratch_shapes=[pltpu.VMEM((2, rows, D_MODEL), BF16), pltpu.SemaphoreType.DMA((2,))]),
        out_shape=[jax.ShapeDtypeStruct((total_rows, D_MODEL), BF16),
                   jax.ShapeDtypeStruct((t, 2 * LANES), F32)],
        input_output_aliases=aliases,
        compiler_params=_cparams("arbitrary"),
        name="sort_tokens",
    )(*args)


def _moe_body(ts_ref, na_ref, xs_ref, wg_ref, wu_ref, wd_ref, ys_ref, xbuf, ybuf, wg16, wu16, wd16, sem_in, sem_out):
    k = pl.program_id(0)
    n_act = na_ref[0]

    def x_copy(g, sl):
        return pltpu.make_async_copy(xs_ref.at[pl.ds(pl.multiple_of(g * MOE_TILE, MOE_TILE), MOE_TILE), :],
                                     xbuf.at[sl], sem_in.at[sl])

    def y_copy(g, sl):
        return pltpu.make_async_copy(ybuf.at[sl],
                                     ys_ref.at[pl.ds(pl.multiple_of(g * MOE_TILE, MOE_TILE), MOE_TILE), :],
                                     sem_out.at[sl])

    @pl.when(k == 0)
    def _():
        for j in range(MOE_AHEAD):
            @pl.when(j < n_act)
            def _():
                x_copy(j, j).start()

    wg16[...] = wg_ref[0].astype(BF16)
    wu16[...] = wu_ref[0].astype(BF16)
    wd16[...] = wd_ref[0].astype(BF16)

    def tile(g, carry):
        sl = g % MOE_BUFS

        @pl.when(g + MOE_AHEAD < n_act)
        def _():
            x_copy(g + MOE_AHEAD, (g + MOE_AHEAD) % MOE_BUFS).start()

        x_copy(g, sl).wait()

        @pl.when(g >= MOE_BUFS)
        def _():
            y_copy(g - MOE_BUFS, sl).wait()

        x = xbuf[sl]
        act = _silu(_dot(x, wg16[...])) * _dot(x, wu16[...])
        ybuf[sl] = _dot(act.astype(BF16), wd16[...]).astype(BF16)
        y_copy(g, sl).start()
        return carry

    lax.fori_loop(ts_ref[k], ts_ref[k + 1], tile, 0)

    @pl.when(k == pl.num_programs(0) - 1)
    def _():
        for j in range(1, MOE_BUFS + 1):
            @pl.when(n_act >= j)
            def _():
                y_copy(n_act - j, (n_act - j) % MOE_BUFS).wait()


def _moe_sorted(tile_start, n_active, xs, wg, wu, wd):
    w_map = lambda k, ts, na: (k, 0, 0)
    return pl.pallas_call(
        _moe_body,
        grid_spec=pltpu.PrefetchScalarGridSpec(
            num_scalar_prefetch=2,
            grid=(N_EXPERTS,),
            in_specs=[pl.BlockSpec(memory_space=pl.ANY),
                      pl.BlockSpec((1, D_MODEL, EXPERT_FF), w_map),
                      pl.BlockSpec((1, D_MODEL, EXPERT_FF), w_map),
                      pl.BlockSpec((1, EXPERT_FF, D_MODEL), w_map)],
            out_specs=pl.BlockSpec(memory_space=pl.ANY),
            scratch_shapes=[pltpu.VMEM((MOE_BUFS, MOE_TILE, D_MODEL), BF16),
                            pltpu.VMEM((MOE_BUFS, MOE_TILE, D_MODEL), BF16),
                            pltpu.VMEM((D_MODEL, EXPERT_FF), BF16), pltpu.VMEM((D_MODEL, EXPERT_FF), BF16),
                            pltpu.VMEM((EXPERT_FF, D_MODEL), BF16),
                            pltpu.SemaphoreType.DMA((MOE_BUFS,)), pltpu.SemaphoreType.DMA((MOE_BUFS,))]),
        out_shape=jax.ShapeDtypeStruct(xs.shape, BF16),
        compiler_params=_cparams("arbitrary"),
        name="moe_sorted",
    )(tile_start, n_active, xs, wg, wu, wd)


def _ple_body(cd_ref, h_ref, dt_ref, wt_ref, p_ref, gn_ref, wg_ref, bg_ref, wp_ref, fn_ref, ys_ref,
              o_ref, loc, sem):
    i = pl.program_id(0)
    tm = h_ref.shape[0]
    rows = loc.shape[1]
    n_chunk = rows // RUN_PAD
    slot = i % 2

    def chunk_copy(step, sl, c):
        return pltpu.make_async_copy(
            ys_ref.at[pl.ds(pl.multiple_of(cd_ref[step, c], RUN_PAD), RUN_PAD), :],
            loc.at[sl, pl.ds(c * RUN_PAD, RUN_PAD), :], sem.at[sl])

    def fetch(step, sl):
        for c in range(n_chunk):
            chunk_copy(step, sl, c).start()

    @pl.when(i == 0)
    def _():
        fetch(0, 0)

    @pl.when(i + 1 < pl.num_programs(0))
    def _():
        fetch(i + 1, 1 - slot)

    for c in range(n_chunk):
        chunk_copy(i, slot, c).wait()

    ci = lax.broadcasted_iota(jnp.int32, (tm, LANES), 1)
    d1 = dt_ref[:, 0:LANES].astype(jnp.int32)
    d2 = dt_ref[:, LANES:2 * LANES].astype(jnp.int32)
    w1 = wt_ref[:, 0:LANES]
    w2 = wt_ref[:, LANES:2 * LANES]
    sel = jnp.concatenate(
        [jnp.where(d1 == ci + m * LANES, w1, jnp.where(d2 == ci + m * LANES, w2, 0.0)) for m in range(rows // LANES)],
        axis=1).astype(BF16)
    h = h_ref[...] + _dot(sel, loc[slot])
    a16 = _rms(h, gn_ref[...]).astype(BF16)
    gate = _sigmoid(_dot(a16, wg_ref[...]) + bg_ref[...])
    pp = _dot(p_ref[...].astype(BF16), wp_ref[...])
    h = h + gate * pp
    o_ref[...] = _rms(h, fn_ref[...])


def _ple(chunk_src, h1, dest_t, gate_t, p, g_ple, wg16, bg, wp16, g_final, ys, tm):
    t = h1.shape[0]
    row = lambda i, cd: (i, 0)
    fixed = lambda i, cd: (0, 0)
    return pl.pallas_call(
        _ple_body,
        grid_spec=pltpu.PrefetchScalarGridSpec(
            num_scalar_prefetch=1,
            grid=(t // tm,),
            in_specs=[pl.BlockSpec((tm, D_MODEL), row), pl.BlockSpec((tm, 2 * LANES), row),
                      pl.BlockSpec((tm, 2 * LANES), row), pl.BlockSpec((tm, PLE_DIM), row),
                      pl.BlockSpec((1, D_MODEL), fixed), pl.BlockSpec((D_MODEL, D_MODEL), fixed),
                      pl.BlockSpec((1, D_MODEL), fixed), pl.BlockSpec((PLE_DIM, D_MODEL), fixed),
                      pl.BlockSpec((1, D_MODEL), fixed), pl.BlockSpec(memory_space=pl.ANY)],
            out_specs=pl.BlockSpec((tm, D_MODEL), row),
            scratch_shapes=[pltpu.VMEM((2, _sorted_rows_per_tile(tm), D_MODEL), BF16),
                            pltpu.SemaphoreType.DMA((2,))]),
        out_shape=jax.ShapeDtypeStruct((t, D_MODEL), F32),
        compiler_params=_cparams("arbitrary"),
        name="ple_final",
    )(chunk_src, h1, dest_t, gate_t, p, g_ple, wg16, bg, wp16, g_final, ys)


MIX_ROWS = 512
SSD_Q = 128
POOL_HDR = 128
CONV_HDR = 8
TAIL_ROWS = 16


def _softplus(x):
    return jnp.maximum(x, 0.0) + jnp.log1p(jnp.exp(-jnp.abs(x)))


def _head_expand_matrix():
    r = lax.broadcasted_iota(jnp.int32, (LANES, SSD_WIDTH), 0)
    c = lax.broadcasted_iota(jnp.int32, (LANES, SSD_WIDTH), 1)
    return (lax.shift_right_logical(c, 6) == r).astype(BF16)


def _state_block_mask():
    r = lax.broadcasted_iota(jnp.int32, (SSD_BC, SSD_WIDTH), 0)
    c = lax.broadcasted_iota(jnp.int32, (SSD_BC, SSD_WIDTH), 1)
    return (lax.shift_right_logical(r, 6) == lax.shift_right_logical(c, 8)).astype(F32)


def _front_body(x_ref, gm_ref, win_ref, cw_ref, cb_ref, dtb_ref, alog_ref, dsk_ref, nw_ref,
                plw_ref, plb_ref, psc_ref, wout_ref, gf_ref, rh_ref, rl_ref, rb_ref,
                h_ref, u_ref, rt_ref, wt_ref, cnt_ref, st_ref, ptail_ref, ctail_ref,
                pool_ext, conv_ext, s_ref, z_ref, dt_ref, mix_ref):
    c = pl.program_id(1)
    rows = MIX_ROWS
    q_len = SSD_Q

    @pl.when(c == 0)
    def _():
        pool_ext[0:POOL_HDR, :] = jnp.zeros((POOL_HDR, POOL_WIDTH), F32)
        conv_ext[0:CONV_HDR, :] = jnp.zeros((CONV_HDR, CONV_DIM), F32)
        s_ref[...] = jnp.zeros_like(s_ref)

    @pl.when(c > 0)
    def _():
        pool_ext[0:POOL_HDR, :] = pool_ext[rows:rows + POOL_HDR, :]
        conv_ext[0:CONV_HDR, :] = conv_ext[rows:rows + CONV_HDR, :]

    a16 = _rms(x_ref[...], gm_ref[...]).astype(BF16)
    pool_ext[POOL_HDR:POOL_HDR + rows, :] = _dot(a16, win_ref[:, 0:POOL_WIDTH])
    z_ref[...] = _dot(a16, win_ref[:, POOL_WIDTH:POOL_WIDTH + SSD_WIDTH])
    conv_ext[CONV_HDR:CONV_HDR + rows, :] = _dot(a16, win_ref[:, POOL_WIDTH + SSD_WIDTH:DT_OFF])
    dt_ref[...] = _dot(a16, win_ref[:, DT_OFF:IN_COLS])

    li = lax.broadcasted_iota(jnp.int32, (q_len, q_len), 0)
    si = lax.broadcasted_iota(jnp.int32, (q_len, q_len), 1)
    causal = li >= si
    tri = causal.astype(BF16)
    expand = _head_expand_matrix()
    blockmask = _state_block_mask()
    lane = lax.broadcasted_iota(jnp.int32, (q_len, LANES), 1)
    left = lane < SSD_HEAD_DIM
    a_neg = -jnp.exp(alog_ref[...])
    wl = lax.broadcasted_iota(jnp.int32, (q_len, 2 * q_len), 0)
    wj = lax.broadcasted_iota(jnp.int32, (q_len, 2 * q_len), 1)
    rowi = lax.broadcasted_iota(jnp.int32, (q_len, LANES), 0)

    for q in range(rows // q_len):
        r0 = q * q_len
        base = CONV_HDR + r0
        conv = cb_ref[...] + conv_ext[base - 3:base - 3 + q_len, :] * cw_ref[0:1, :]
        conv = conv + conv_ext[base - 2:base - 2 + q_len, :] * cw_ref[1:2, :]
        conv = conv + conv_ext[base - 1:base - 1 + q_len, :] * cw_ref[2:3, :]
        conv = conv + conv_ext[base:base + q_len, :] * cw_ref[3:4, :]
        conv = _silu(conv)
        xs = conv[:, 0:SSD_WIDTH]
        b16 = conv[:, SSD_WIDTH:SSD_WIDTH + SSD_BC].astype(BF16)
        c_all = conv[:, SSD_WIDTH + SSD_BC:CONV_DIM]
        c16 = c_all.astype(BF16)
        dt = _softplus(dt_ref[r0:r0 + q_len, :] + dtb_ref[...])
        acs = _dot_exact_lhs(tri, dt * a_neg)
        acs_t = acs.T
        dtx = _dot_exact_rhs(dt, expand, terms=2)
        acsx = _dot_exact_rhs(acs, expand)
        last_x = acsx[q_len - 1:q_len, :]
        xdt = xs * dtx
        xdt16 = xdt.astype(BF16)
        s_old = s_ref[...]
        y_off = _dot(c16, s_old.astype(BF16)) * jnp.exp(acsx)
        contrib = _dot_tn(b16, (xdt * jnp.exp(last_x - acsx)).astype(BF16)) * blockmask
        s_ref[...] = s_old * jnp.exp(last_x) + contrib
        cb = []
        for g in range(SSD_GROUPS):
            cg = jnp.where(lax.shift_right_logical(lane, 6) == g, c_all, 0.0).astype(BF16)
            cb.append(_dot_nt(cg, b16))
        z = z_ref[r0:r0 + q_len, :]
        y_blocks = []
        for j in range(SSD_HEADS // 2):
            blk = slice(j * LANES, (j + 1) * LANES)
            zs = []
            for h in (2 * j, 2 * j + 1):
                seg = acs[:, h:h + 1] - acs_t[h:h + 1, :]
                decay = jnp.where(causal, jnp.exp(seg), 0.0)
                scores = (cb[h // (SSD_HEADS // SSD_GROUPS)] * decay).astype(BF16)
                zs.append(_dot(scores, xdt16[:, blk]))
            y = jnp.where(left, zs[0], zs[1]) + y_off[:, blk]
            y = y + dsk_ref[:, blk] * xs[:, blk]
            y_blocks.append(y * _silu(z[:, blk]))
        for g in range(SSD_GROUPS):
            y0, y1 = y_blocks[2 * g], y_blocks[2 * g + 1]
            ss = jnp.sum(y0 * y0, axis=-1, keepdims=True) + jnp.sum(y1 * y1, axis=-1, keepdims=True)
            rs = lax.rsqrt(ss * (1.0 / (2 * LANES)) + EPS)
            for k, yk in ((2 * g, y0), (2 * g + 1, y1)):
                blk = slice(k * LANES, (k + 1) * LANES)
                out = yk * rs * nw_ref[:, blk]
                mix_ref[r0:r0 + q_len, POOL_WIDTH + k * LANES:POOL_WIDTH + (k + 1) * LANES] = out.astype(BF16)
        pos = c * rows + r0 + rowi
        for g, w in enumerate(POOL_WINDOWS):
            blk = slice(g * POOL_GROUP_DIM, (g + 1) * POOL_GROUP_DIM)
            pe = pool_ext[r0:r0 + 2 * q_len, blk]
            band = ((wj <= wl + POOL_HDR) & (wj > wl + POOL_HDR - w)).astype(BF16)
            winsum = _dot_exact_lhs(band, pe, terms=2)
            cnt = jnp.minimum(pos + 1, w).astype(F32)
            m = winsum / cnt - pe[q_len:2 * q_len, :]
            yg = _dot(m.astype(BF16), plw_ref[g]) + plb_ref[:, blk]
            mix_ref[r0:r0 + q_len, blk] = (yg * psc_ref[:, blk]).astype(BF16)

    _out_proj_and_route(x_ref[...], mix_ref[...], wout_ref, gf_ref, rh_ref, rl_ref, rb_ref,
                        h_ref, u_ref, rt_ref, wt_ref, cnt_ref)

    @pl.when(c == pl.num_programs(1) - 1)
    def _():
        st_ref[0] = s_ref[...]
        ptail_ref[0] = pool_ext[POOL_HDR + rows - TAIL_ROWS:POOL_HDR + rows, :]
        ctail_ref[0] = conv_ext[rows:rows + CONV_HDR, :]


def _front_prompt(x, g_mix, w_in16, cw, cb, dtb, alog, dsk, nw, plw16, plb, psc, w_out16, g_ffn, r_hi, r_lo, r_bias,
                  nb, seq):
    steps = seq // MIX_ROWS
    nt = nb * steps
    row = lambda b, c: (b * steps + c, 0)
    tile3 = lambda b, c: (b * steps + c, 0, 0)
    per_seq = lambda b, c: (b, 0, 0)
    fixed2 = lambda b, c: (0, 0)
    return pl.pallas_call(
        _front_body,
        grid=(nb, steps),
        in_specs=[pl.BlockSpec((MIX_ROWS, D_MODEL), row), pl.BlockSpec((1, D_MODEL), fixed2),
                  pl.BlockSpec((D_MODEL, IN_COLS), fixed2),
                  pl.BlockSpec((SSD_CONV, CONV_DIM), fixed2), pl.BlockSpec((1, CONV_DIM), fixed2),
                  pl.BlockSpec((1, LANES), fixed2), pl.BlockSpec((1, LANES), fixed2),
                  pl.BlockSpec((1, SSD_WIDTH), fixed2), pl.BlockSpec((1, SSD_WIDTH), fixed2),
                  pl.BlockSpec((len(POOL_WINDOWS), POOL_GROUP_DIM, POOL_GROUP_DIM), lambda b, c: (0, 0, 0)),
                  pl.BlockSpec((1, POOL_WIDTH), fixed2), pl.BlockSpec((1, POOL_WIDTH), fixed2),
                  pl.BlockSpec((D_MODEL, D_MODEL), fixed2), pl.BlockSpec((1, D_MODEL), fixed2),
                  pl.BlockSpec((LANES, D_MODEL), fixed2), pl.BlockSpec((LANES, D_MODEL), fixed2),
                  pl.BlockSpec((LANES, MIX_ROWS), fixed2)],
        out_specs=[pl.BlockSpec((MIX_ROWS, D_MODEL), row), pl.BlockSpec((MIX_ROWS, D_MODEL), row),
                   pl.BlockSpec((1, 8, MIX_ROWS), tile3), pl.BlockSpec((MIX_ROWS, 2 * LANES), row),
                   pl.BlockSpec((1, N_EXPERTS, LANES), tile3),
                   pl.BlockSpec((1, SSD_BC, SSD_WIDTH), per_seq),
                   pl.BlockSpec((1, TAIL_ROWS, POOL_WIDTH), per_seq),
                   pl.BlockSpec((1, CONV_HDR, CONV_DIM), per_seq)],
        out_shape=[jax.ShapeDtypeStruct((nb * seq, D_MODEL), F32), jax.ShapeDtypeStruct((nb * seq, D_MODEL), BF16),
                   jax.ShapeDtypeStruct((nt, 8, MIX_ROWS), F32), jax.ShapeDtypeStruct((nb * seq, 2 * LANES), F32),
                   jax.ShapeDtypeStruct((nt, N_EXPERTS, LANES), F32),
                   jax.ShapeDtypeStruct((nb, SSD_BC, SSD_WIDTH), F32),
                   jax.ShapeDtypeStruct((nb, TAIL_ROWS, POOL_WIDTH), F32),
                   jax.ShapeDtypeStruct((nb, CONV_HDR, CONV_DIM), F32)],
        scratch_shapes=[pltpu.VMEM((POOL_HDR + MIX_ROWS, POOL_WIDTH), F32),
                        pltpu.VMEM((CONV_HDR + MIX_ROWS, CONV_DIM), F32),
                        pltpu.VMEM((SSD_BC, SSD_WIDTH), F32),
                        pltpu.VMEM((MIX_ROWS, SSD_WIDTH), F32), pltpu.VMEM((MIX_ROWS, LANES), F32),
                        pltpu.VMEM((MIX_ROWS, D_MODEL), BF16)],
        compiler_params=_cparams("parallel", "arbitrary"),
        name="front_prompt",
    )(x, g_mix, w_in16, cw, cb, dtb, alog, dsk, nw, plw16, plb, psc, w_out16, g_ffn, r_hi, r_lo, r_bias)


def _mix_step_body(vp_ref, z_ref, xbc_ref, dt_ref, sp_ref, sc_ref, st_ref, cw_ref, cb_ref, dtb_ref, alog_ref,
                   dsk_ref, nw_ref, plw_ref, plb_ref, psc_ref, mix_ref, po_ref, co_ref, so_ref,
                   xdt_t, dec_t, b_t, c_t, xs_keep, y_t):
    h = pl.program_id(0)

    @pl.when(h == 0)
    def _():
        xbc = xbc_ref[...]
        conv = cb_ref[...] + sc_ref[0] * cw_ref[0:1, :]
        conv = conv + sc_ref[1] * cw_ref[1:2, :]
        conv = conv + sc_ref[2] * cw_ref[2:3, :]
        conv = conv + xbc * cw_ref[3:4, :]
        conv = _silu(conv)
        co_ref[0] = sc_ref[1]
        co_ref[1] = sc_ref[2]
        co_ref[2] = xbc
        xs = conv[:, 0:SSD_WIDTH]
        xs_keep[...] = xs
        b_t[...] = conv[:, SSD_WIDTH:SSD_WIDTH + SSD_BC].T
        c_t[...] = conv[:, SSD_WIDTH + SSD_BC:CONV_DIM].T
        dt = _softplus(dt_ref[...] + dtb_ref[...])
        d_a = dt * (-jnp.exp(alog_ref[...]))
        dt_t = dt.T
        dec_t[...] = jnp.exp(d_a).T
        xs_t = xs.T
        for k in range(SSD_HEADS):
            blk = slice(k * SSD_HEAD_DIM, (k + 1) * SSD_HEAD_DIM)
            xdt_t[blk, :] = xs_t[blk, :] * dt_t[k:k + 1, :]
        v = vp_ref[...]
        for k in range(POOL_BUF - 1):
            po_ref[k] = sp_ref[k + 1]
        po_ref[POOL_BUF - 1] = v
        for g, w in enumerate(POOL_WINDOWS):
            blk = slice(g * POOL_GROUP_DIM, (g + 1) * POOL_GROUP_DIM)
            acc = sp_ref[POOL_BUF - (w - 1), :, blk]
            for k in range(w - 2, 0, -1):
                acc = acc + sp_ref[POOL_BUF - k, :, blk]
            acc = acc + v[:, blk]
            m = acc / float(min(PAST_LEN + 1, w)) - v[:, blk]
            yg = _dot(m.astype(BF16), plw_ref[g]) + plb_ref[:, blk]
            mix_ref[:, blk] = (yg * psc_ref[:, blk]).astype(BF16)

    g_off = pl.multiple_of((h // (SSD_HEADS // SSD_GROUPS)) * SSD_STATE, SSD_STATE)
    h_off = pl.multiple_of(h * SSD_HEAD_DIM, SSD_HEAD_DIM)
    b_g = b_t[pl.ds(g_off, SSD_STATE), :]
    c_g = c_t[pl.ds(g_off, SSD_STATE), :]
    dec = dec_t[pl.ds(h, 1), :]
    xdt = xdt_t[pl.ds(h_off, SSD_HEAD_DIM), :]
    y_rows = []
    for p in range(SSD_HEAD_DIM):
        s_new = st_ref[0, p] * dec + xdt[p:p + 1, :] * b_g
        so_ref[0, p] = s_new
        y_rows.append(jnp.sum(s_new * c_g, axis=0, keepdims=True))
    y_t[pl.ds(h_off, SSD_HEAD_DIM), :] = jnp.concatenate(y_rows, axis=0)

    @pl.when(h == pl.num_programs(0) - 1)
    def _():
        xs = xs_keep[...]
        y = y_t[...].T + dsk_ref[...] * xs
        y = y * _silu(z_ref[...])
        width = SSD_WIDTH // SSD_GROUPS
        for g in range(SSD_GROUPS):
            blk = slice(g * width, (g + 1) * width)
            yg = y[:, blk]
            rs = lax.rsqrt(jnp.mean(yg * yg, axis=-1, keepdims=True) + EPS)
            mix_ref[:, POOL_WIDTH + g * width:POOL_WIDTH + (g + 1) * width] = (yg * rs * nw_ref[:, blk]).astype(BF16)


def _mix_step(vp, z, xbc, dt, pool_t, conv_t, ssm_t, cw, cb, dtb, alog, dsk, nw, plw16, plb, psc):
    n = vp.shape[0]
    fixed2 = lambda h: (0, 0)
    fixed3 = lambda h: (0, 0, 0)
    st_spec = pl.BlockSpec((1, SSD_HEAD_DIM, SSD_STATE, n), lambda h: (h, 0, 0, 0))
    return pl.pallas_call(
        _mix_step_body,
        grid=(SSD_HEADS,),
        in_specs=[pl.BlockSpec((n, POOL_WIDTH), fixed2), pl.BlockSpec((n, SSD_WIDTH), fixed2),
                  pl.BlockSpec((n, CONV_DIM), fixed2), pl.BlockSpec((n, LANES), fixed2),
                  pl.BlockSpec((POOL_BUF, n, POOL_WIDTH), fixed3),
                  pl.BlockSpec((SSD_CONV - 1, n, CONV_DIM), fixed3),
                  st_spec,
                  pl.BlockSpec((SSD_CONV, CONV_DIM), fixed2), pl.BlockSpec((1, CONV_DIM), fixed2),
                  pl.BlockSpec((1, LANES), fixed2), pl.BlockSpec((1, LANES), fixed2),
                  pl.BlockSpec((1, SSD_WIDTH), fixed2), pl.BlockSpec((1, SSD_WIDTH), fixed2),
                  pl.BlockSpec((len(POOL_WINDOWS), POOL_GROUP_DIM, POOL_GROUP_DIM), fixed3),
                  pl.BlockSpec((1, POOL_WIDTH), fixed2), pl.BlockSpec((1, POOL_WIDTH), fixed2)],
        out_specs=[pl.BlockSpec((n, D_MODEL), fixed2), pl.BlockSpec((POOL_BUF, n, POOL_WIDTH), fixed3),
                   pl.BlockSpec((SSD_CONV - 1, n, CONV_DIM), fixed3), st_spec],
        out_shape=[jax.ShapeDtypeStruct((n, D_MODEL), BF16), jax.ShapeDtypeStruct(pool_t.shape, F32),
                   jax.ShapeDtypeStruct(conv_t.shape, F32), jax.ShapeDtypeStruct(ssm_t.shape, F32)],
        scratch_shapes=[pltpu.VMEM((SSD_WIDTH, n), F32), pltpu.VMEM((LANES, n), F32),
                        pltpu.VMEM((SSD_BC, n), F32), pltpu.VMEM((SSD_BC, n), F32),
                        pltpu.VMEM((n, SSD_WIDTH), F32), pltpu.VMEM((SSD_WIDTH, n), F32)],
        compiler_params=_cparams("arbitrary"),
        name="mix_step",
    )(vp, z, xbc, dt, pool_t, conv_t, ssm_t, cw, cb, dtb, alog, dsk, nw, plw16, plb, psc)


PROMPT_TILE = 512


def _sort_tables(counts, tile_tokens, data_rows):
    cnt = jnp.concatenate(counts, axis=0)
    pc = (cnt + RUN_PAD - 1) // RUN_PAD * RUN_PAD
    off_local = jnp.cumsum(pc, axis=1) - pc
    tile_rows = jnp.sum(pc, axis=1)
    region = (jnp.sum(pc, axis=0) + MOE_TILE - 1) // MOE_TILE * MOE_TILE
    base = jnp.cumsum(region) - region
    dst = base[None, :] + jnp.cumsum(pc, axis=0) - pc
    per_pass = []
    lo = 0
    for c, tm in zip(counts, tile_tokens):
        hi = lo + c.shape[0]
        n_chunk = _sorted_rows_per_tile(tm) // RUN_PAD
        s = jnp.arange(n_chunk, dtype=jnp.int32) * RUN_PAD
        begins = off_local[lo:hi, None, :]
        ends = begins + pc[lo:hi, None, :]
        inside = (begins <= s[None, :, None]) & (s[None, :, None] < ends)
        shift = jnp.sum(jnp.where(inside, dst[lo:hi, None, :] - begins, 0), axis=2)
        valid = s[None, :] < tile_rows[lo:hi, None]
        parity = (jnp.arange(hi - lo, dtype=jnp.int32) % 2)[:, None]
        spare = data_rows + parity * _sorted_rows_per_tile(max(tile_tokens)) + s[None, :]
        scatter_dst = jnp.where(valid, s[None, :] + shift, spare).astype(jnp.int32)
        gather_src = jnp.where(valid, s[None, :] + shift, 0).astype(jnp.int32)
        off_v = jnp.broadcast_to(off_local[lo:hi, :, None].astype(F32), (hi - lo, N_EXPERTS, LANES))
        per_pass.append((scatter_dst, gather_src, off_v))
        lo = hi
    tiles_cum = jnp.cumsum(region // MOE_TILE)
    n_active = tiles_cum[-1]
    return per_pass, tiles_cum, n_active


def _max_sorted_rows(tile_counts, tile_tokens):
    rows = sum(n * (2 * tm + N_EXPERTS * (RUN_PAD - 1)) for n, tm in zip(tile_counts, tile_tokens))
    data_rows = (-(-rows // MOE_TILE) + N_EXPERTS) * MOE_TILE
    return data_rows, data_rows + 2 * _sorted_rows_per_tile(max(tile_tokens))


def _row(v):
    return v.reshape(1, -1).astype(F32)


def _pad_lanes(v):
    return jnp.pad(v.reshape(1, -1).astype(F32), ((0, 0), (0, LANES - v.size)))


def _state_from_blocks(st):
    n = st.shape[0]
    hpg = SSD_HEADS // SSD_GROUPS
    s6 = st.reshape(n, SSD_GROUPS, SSD_STATE, SSD_GROUPS, hpg, SSD_HEAD_DIM)
    per_group = [s6[:, g, :, g] for g in range(SSD_GROUPS)]
    s = jnp.stack(per_group, axis=1)
    return jnp.transpose(s, (0, 1, 3, 4, 2)).reshape(n, SSD_HEADS, SSD_HEAD_DIM, SSD_STATE)


def kernel(x_prompt, x_sample, p_prompt, p_sample, state_pool, state_conv, state_ssm, norm_mix, w_in, pool_lin_w, pool_lin_b, pool_scale, conv_w, conv_b, dt_bias, a_log, d_skip, ssd_norm, w_out, norm_ffn, router_grp_w, router_grp_b, router_exp_w, router_exp_b, exp_w_gate, exp_w_up, exp_w_down, norm_ple, ple_gate_w, ple_gate_b, ple_proj_w, norm_final):
    nb, seq, _ = x_prompt.shape
    ns = x_sample.shape[0]
    assert ns == LANES and x_sample.shape[1] == 1 and seq % MIX_ROWS == 0 and seq >= POOL_BUF

    w_in16 = jnp.pad(w_in[0], ((0, 0), (0, IN_COLS - w_in.shape[2]))).astype(BF16)
    w_out16 = w_out[0].astype(BF16)
    g_mix, g_ffn, g_ple, g_fin = _row(norm_mix[0]), _row(norm_ffn[0]), _row(norm_ple[0]), _row(norm_final)
    cw, cb = conv_w[0].astype(F32), _row(conv_b[0])
    dtb, alog = _pad_lanes(dt_bias[0]), _pad_lanes(a_log[0])
    dsk = _row(jnp.repeat(d_skip[0], SSD_HEAD_DIM))
    nw = _row(ssd_norm[0])
    plw16 = pool_lin_w[0].astype(BF16)
    plb, psc = _row(pool_lin_b[0]), _row(pool_scale[0])
    zeros4 = jnp.zeros((D_MODEL, 8 - N_EXPERT_GROUPS), F32)
    r_w = jnp.concatenate([router_grp_w[0], zeros4, router_exp_w[0],
                           jnp.zeros((D_MODEL, LANES - 8 - N_EXPERTS), F32)], axis=1).T
    r_hi = r_w.astype(BF16)
    r_lo = (r_w - r_hi.astype(F32)).astype(BF16)
    r_b = jnp.concatenate([router_grp_b[0], jnp.zeros((8 - N_EXPERT_GROUPS,), F32), router_exp_b[0],
                           jnp.zeros((LANES - 8 - N_EXPERTS,), F32)])
    wg = exp_w_gate[0].reshape(N_EXPERTS, D_MODEL, EXPERT_FF)
    wu = exp_w_up[0].reshape(N_EXPERTS, D_MODEL, EXPERT_FF)
    wd = exp_w_down[0].reshape(N_EXPERTS, EXPERT_FF, D_MODEL)
    pg16 = ple_gate_w[0].astype(BF16)
    pgb = _row(ple_gate_b[0])
    pp16 = ple_proj_w[0].astype(BF16)

    def route(x, mix16, tm):
        r_bias = jnp.broadcast_to(r_b[:, None], (LANES, tm))
        return _out_proj(x, mix16, w_out16, g_ffn, r_hi, r_lo, r_bias, tm)

    xp = x_prompt.reshape(nb * seq, D_MODEL)
    h1_p, u_p, rt_p, gate_p, cnt_p, st, pool_tail, conv_tail = _front_prompt(
        xp, g_mix, w_in16, cw, cb, dtb, alog, dsk, nw, plw16, plb, psc, w_out16, g_ffn, r_hi, r_lo,
        jnp.broadcast_to(r_b[:, None], (LANES, MIX_ROWS)), nb, seq)
    pool_p = pool_tail[:, TAIL_ROWS - POOL_BUF:]
    conv_p = conv_tail[:, CONV_HDR - (SSD_CONV - 1):]
    ssm_p = _state_from_blocks(st)

    xs_ = x_sample.reshape(ns, D_MODEL)
    vp_s, z_s, xbc_s, dt_s = _in_proj(xs_, g_mix, w_in16, ns)
    mix_s, pool_t, conv_t, ssm_t = _mix_step(
        vp_s, z_s, xbc_s, dt_s, jnp.transpose(state_pool[0], (1, 0, 2)), jnp.transpose(state_conv[0], (1, 0, 2)),
        jnp.transpose(state_ssm[0], (1, 2, 3, 0)), cw, cb, dtb, alog, dsk, nw, plw16, plb, psc)
    h1_s, u_s, rt_s, gate_s, cnt_s = route(xs_, mix_s, ns)
    pool_s = jnp.transpose(pool_t, (1, 0, 2))
    conv_s = jnp.transpose(conv_t, (1, 0, 2))
    ssm_s = jnp.transpose(ssm_t, (3, 0, 1, 2))

    counts = [cnt_p[:, :, 0].astype(jnp.int32), cnt_s[:, :, 0].astype(jnp.int32)]
    tiles = (PROMPT_TILE, ns)
    data_rows, total_rows = _max_sorted_rows([c.shape[0] for c in counts], tiles)
    (tab_p, tab_s), tiles_cum, n_active = _sort_tables(counts, tiles, data_rows)
    xs_sorted, dest_p = _sort_tokens(tab_p[0], u_p, rt_p, tab_p[2], None, PROMPT_TILE, total_rows)
    xs_sorted, dest_s = _sort_tokens(tab_s[0], u_s, rt_s, tab_s[2], xs_sorted, ns, total_rows)
    tile_start = jnp.concatenate([jnp.zeros((1,), jnp.int32), tiles_cum.astype(jnp.int32)])
    ys_sorted = _moe_sorted(tile_start, n_active.reshape(1).astype(jnp.int32), xs_sorted, wg, wu, wd)

    y_prompt = _ple(tab_p[1], h1_p, dest_p, gate_p, p_prompt[0].reshape(nb * seq, PLE_DIM),
                    g_ple, pg16, pgb, pp16, g_fin, ys_sorted, PROMPT_TILE)
    y_sample = _ple(tab_s[1], h1_s, dest_s, gate_s, p_sample[0].reshape(ns, PLE_DIM),
                    g_ple, pg16, pgb, pp16, g_fin, ys_sorted, ns)

    return (y_prompt.reshape(nb, seq, D_MODEL), y_sample.reshape(ns, 1, D_MODEL),
            pool_p[None], conv_p[None], ssm_p[None], pool_s[None], conv_s[None], ssm_s[None])
```

```python
import functools

import jax
import jax.numpy as jnp
from jax import lax
from jax.experimental import pallas as pl
from jax.experimental.pallas import tpu as pltpu

F32 = jnp.float32
BF16 = jnp.bfloat16

D_MODEL = 1024
POOL_WIDTH = 512
POOL_WINDOWS = (2, 4, 8, 16)
POOL_GROUP_DIM = 128
POOL_BUF = 15
SSD_WIDTH = 512
SSD_HEAD_DIM = 64
SSD_HEADS = 8
SSD_GROUPS = 2
SSD_STATE = 64
SSD_CONV = 4
SSD_BC = SSD_GROUPS * SSD_STATE
CONV_DIM = SSD_WIDTH + 2 * SSD_BC
N_EXPERT_GROUPS = 4
EXPERTS_PER_GROUP = 8
N_EXPERTS = N_EXPERT_GROUPS * EXPERTS_PER_GROUP
EXPERT_FF = 256
PLE_DIM = 256
PAST_LEN = 16384
EPS = 1e-6

LANES = 128
IN_COLS = 1920
DT_OFF = POOL_WIDTH + SSD_WIDTH + CONV_DIM
VMEM_LIMIT = 56 * 1024 * 1024


def _cparams(*sem):
    return pltpu.CompilerParams(dimension_semantics=sem, vmem_limit_bytes=VMEM_LIMIT)


def _rms(x, g):
    return x * lax.rsqrt(jnp.mean(x * x, axis=-1, keepdims=True) + EPS) * g


def _sigmoid(x):
    return 1.0 / (1.0 + jnp.exp(-x))


def _silu(x):
    return x * _sigmoid(x)


def _split3(v):
    hi = v.astype(BF16)
    r = v - hi.astype(F32)
    mid = r.astype(BF16)
    lo = (r - mid.astype(F32)).astype(BF16)
    return hi, mid, lo


def _dot(a, b):
    return jnp.dot(a, b, preferred_element_type=F32)


def _dot_nt(a, b):
    return lax.dot_general(a, b, (((1,), (1,)), ((), ())), preferred_element_type=F32)


def _dot_tn(a, b):
    return lax.dot_general(a, b, (((0,), (0,)), ((), ())), preferred_element_type=F32)


def _dot_exact_lhs(sel, v, terms=3):
    acc = None
    for t in _split3(v)[:terms]:
        p = _dot(sel, t)
        acc = p if acc is None else acc + p
    return acc


def _dot_exact_rhs(v, sel, terms=3):
    acc = None
    for t in _split3(v)[:terms]:
        p = _dot(t, sel)
        acc = p if acc is None else acc + p
    return acc


def _in_proj_body(x_ref, g_ref, w_ref, vp_ref, z_ref, xbc_ref, dt_ref):
    a16 = _rms(x_ref[...], g_ref[...]).astype(BF16)
    vp_ref[...] = _dot(a16, w_ref[:, 0:POOL_WIDTH])
    z_ref[...] = _dot(a16, w_ref[:, POOL_WIDTH:POOL_WIDTH + SSD_WIDTH])
    xbc_ref[...] = _dot(a16, w_ref[:, POOL_WIDTH + SSD_WIDTH:DT_OFF])
    dt_ref[...] = _dot(a16, w_ref[:, DT_OFF:IN_COLS])


def _in_proj(x, g, w16, tm):
    t = x.shape[0]
    row = lambda i: (i, 0)
    fixed = lambda i: (0, 0)
    return pl.pallas_call(
        _in_proj_body,
        grid=(t // tm,),
        in_specs=[pl.BlockSpec((tm, D_MODEL), row), pl.BlockSpec((1, D_MODEL), fixed),
                  pl.BlockSpec((D_MODEL, IN_COLS), fixed)],
        out_specs=[pl.BlockSpec((tm, POOL_WIDTH), row), pl.BlockSpec((tm, SSD_WIDTH), row),
                   pl.BlockSpec((tm, CONV_DIM), row), pl.BlockSpec((tm, LANES), row)],
        out_shape=[jax.ShapeDtypeStruct((t, POOL_WIDTH), F32), jax.ShapeDtypeStruct((t, SSD_WIDTH), F32),
                   jax.ShapeDtypeStruct((t, CONV_DIM), F32), jax.ShapeDtypeStruct((t, LANES), F32)],
        compiler_params=_cparams("parallel"),
        name="in_proj",
    )(x, g, w16)


def _route(lg):
    tm = lg.shape[1]
    gl = lg[0:N_EXPERT_GROUPS, :]
    gmax = jnp.max(gl, axis=0, keepdims=True)
    gsum = jnp.sum(jnp.exp(gl - gmax), axis=0, keepdims=True)
    g_w = 1.0 / gsum
    gi = lax.broadcasted_iota(jnp.int32, gl.shape, 0)
    g_idx = jnp.min(jnp.where(gl == gmax, gi, N_EXPERT_GROUPS), axis=0, keepdims=True)
    sel = jnp.zeros((EXPERTS_PER_GROUP, tm), F32)
    for g in range(N_EXPERT_GROUPS):
        blk = lg[8 + g * EXPERTS_PER_GROUP:8 + (g + 1) * EXPERTS_PER_GROUP, :]
        sel = jnp.where(g_idx == g, blk, sel)
    ei = lax.broadcasted_iota(jnp.int32, sel.shape, 0)
    m1 = jnp.max(sel, axis=0, keepdims=True)
    i1 = jnp.min(jnp.where(sel == m1, ei, EXPERTS_PER_GROUP), axis=0, keepdims=True)
    rest = jnp.where(ei == i1, -jnp.inf, sel)
    m2 = jnp.max(rest, axis=0, keepdims=True)
    i2 = jnp.min(jnp.where(rest == m2, ei, EXPERTS_PER_GROUP), axis=0, keepdims=True)
    p2 = jnp.exp(m2 - m1)
    w1 = g_w / (1.0 + p2)
    w2 = g_w * p2 / (1.0 + p2)
    return g_idx * EXPERTS_PER_GROUP + i1, g_idx * EXPERTS_PER_GROUP + i2, w1, w2


def _out_proj_body(x_ref, mix_ref, w_ref, g_ref, rh_ref, rl_ref, rb_ref, h_ref, u_ref, rt_ref, wt_ref, cnt_ref):
    _out_proj_and_route(x_ref[...], mix_ref[...], w_ref, g_ref, rh_ref, rl_ref, rb_ref,
                        h_ref, u_ref, rt_ref, wt_ref, cnt_ref)


def _out_proj_and_route(x, mix16, w_ref, g_ref, rh_ref, rl_ref, rb_ref, h_ref, u_ref, rt_ref, wt_ref, cnt_ref,
                        r0=0, accumulate=False):
    tm = x.shape[0]
    h = x + _dot(mix16, w_ref[...])
    h_ref[r0:r0 + tm, :] = h
    _route_rows(h, g_ref, rh_ref, rl_ref, rb_ref, u_ref, rt_ref, wt_ref, cnt_ref, r0=r0, accumulate=accumulate)


def _route_rows(h, g_ref, rh_ref, rl_ref, rb_ref, u_ref, rt_ref, wt_ref, cnt_ref, r0=0, accumulate=False):
    tm = h.shape[0]
    u = _rms(h, g_ref[...])
    u_hi = u.astype(BF16)
    u_ref[r0:r0 + tm, :] = u_hi
    u_lo = (u - u_hi.astype(F32)).astype(BF16)
    lg = (_dot_nt(rh_ref[...], u_hi) + _dot_nt(rh_ref[...], u_lo) + _dot_nt(rl_ref[...], u_hi)
          + rb_ref[:, r0:r0 + tm])
    b1, b2, w1, w2 = _route(lg)
    r8 = lax.broadcasted_iota(jnp.int32, (8, tm), 0)
    rt_ref[0, :, r0:r0 + tm] = jnp.where(r8 == 0, b1.astype(F32), jnp.where(r8 == 1, b2.astype(F32), 0.0))
    wt_ref[r0:r0 + tm, :] = jnp.concatenate(
        [jnp.broadcast_to(w1, (LANES, tm)).T, jnp.broadcast_to(w2, (LANES, tm)).T], axis=1)
    kio = lax.broadcasted_iota(jnp.int32, (N_EXPERTS, tm), 0)
    hits = ((kio == b1) | (kio == b2)).astype(F32)
    cnt = jnp.broadcast_to(jnp.sum(hits, axis=1, keepdims=True), (N_EXPERTS, LANES))
    cnt_ref[0] = cnt_ref[0] + cnt if accumulate else cnt


def _out_proj(x, mix16, w16, g, r_hi, r_lo, r_bias, tm):
    t = x.shape[0]
    nt = t // tm
    row = lambda i: (i, 0)
    fixed = lambda i: (0, 0)
    tile3 = lambda i: (i, 0, 0)
    return pl.pallas_call(
        _out_proj_body,
        grid=(nt,),
        in_specs=[pl.BlockSpec((tm, D_MODEL), row), pl.BlockSpec((tm, D_MODEL), row),
                  pl.BlockSpec((D_MODEL, D_MODEL), fixed), pl.BlockSpec((1, D_MODEL), fixed),
                  pl.BlockSpec((LANES, D_MODEL), fixed), pl.BlockSpec((LANES, D_MODEL), fixed),
                  pl.BlockSpec((LANES, tm), fixed)],
        out_specs=[pl.BlockSpec((tm, D_MODEL), row), pl.BlockSpec((tm, D_MODEL), row),
                   pl.BlockSpec((1, 8, tm), tile3), pl.BlockSpec((tm, 2 * LANES), row),
                   pl.BlockSpec((1, N_EXPERTS, LANES), tile3)],
        out_shape=[jax.ShapeDtypeStruct((t, D_MODEL), F32), jax.ShapeDtypeStruct((t, D_MODEL), BF16),
                   jax.ShapeDtypeStruct((nt, 8, tm), F32), jax.ShapeDtypeStruct((t, 2 * LANES), F32),
                   jax.ShapeDtypeStruct((nt, N_EXPERTS, LANES), F32)],
        compiler_params=_cparams("parallel"),
        name="out_proj_router",
    )(x, mix16, w16, g, r_hi, r_lo, r_bias)


RUN_PAD = 16
MOE_TILE = 256
MOE_BUFS = 4
MOE_AHEAD = 3


def _tile_lanes(v, width):
    reps = width // LANES
    return v if reps == 1 else jnp.concatenate([v] * reps, axis=1)


def _sorted_rows_per_tile(tm):
    need = 2 * tm + N_EXPERTS * (RUN_PAD - 1)
    return -(-need // LANES) * LANES


def _sort_body(cd_ref, u_ref, rt_ref, off_ref, *rest):
    xs_ref, dt_ref, loc, sem = rest[-4:]
    i = pl.program_id(0)
    tm = u_ref.shape[0]
    rows = loc.shape[1]
    rt = rt_ref[0]
    b1 = rt[0:1, :].astype(jnp.int32)
    b2 = rt[1:2, :].astype(jnp.int32)
    kio = lax.broadcasted_iota(jnp.int32, (N_EXPERTS, tm), 0)
    o1 = kio == b1
    o2 = kio == b2
    before = (lax.broadcasted_iota(jnp.int32, (tm, tm), 0) < lax.broadcasted_iota(jnp.int32, (tm, tm), 1))
    start = _dot((o1 | o2).astype(BF16), before.astype(BF16)) + _tile_lanes(off_ref[0], tm)
    d1 = jnp.sum(jnp.where(o1, start, 0.0), axis=0, keepdims=True)
    d2 = jnp.sum(jnp.where(o2, start, 0.0), axis=0, keepdims=True)
    dt_ref[...] = jnp.concatenate([jnp.broadcast_to(d1, (LANES, tm)).T, jnp.broadcast_to(d2, (LANES, tm)).T], axis=1)
    rio = lax.broadcasted_iota(jnp.int32, (rows, tm), 0)
    perm = ((rio == d1.astype(jnp.int32)) | (rio == d2.astype(jnp.int32))).astype(BF16)
    slot = i % 2
    loc[slot] = _dot(perm, u_ref[...]).astype(BF16)

    def chunk_copy(step, sl, c):
        return pltpu.make_async_copy(
            loc.at[sl, pl.ds(c * RUN_PAD, RUN_PAD), :],
            xs_ref.at[pl.ds(pl.multiple_of(cd_ref[step, c], RUN_PAD), RUN_PAD), :], sem.at[sl])

    n_chunk = rows // RUN_PAD
    for c in range(n_chunk):
        chunk_copy(i, slot, c).start()

    @pl.when(i > 0)
    def _():
        for c in range(n_chunk):
            chunk_copy(i - 1, 1 - slot, c).wait()

    @pl.when(i == pl.num_programs(0) - 1)
    def _():
        for c in range(n_chunk):
            chunk_copy(i, slot, c).wait()


def _sort_tokens(chunk_dst, u16, rt, off_v, xs_prev, tm, total_rows):
    t = u16.shape[0]
    rows = _sorted_rows_per_tile(tm)
    in_specs = [pl.BlockSpec((tm, D_MODEL), lambda i, cd: (i, 0)),
                pl.BlockSpec((1, 8, tm), lambda i, cd: (i, 0, 0)),
                pl.BlockSpec((1, N_EXPERTS, LANES), lambda i, cd: (i, 0, 0))]
    args = [chunk_dst, u16, rt, off_v]
    aliases = {}
    if xs_prev is not None:
        in_specs.append(pl.BlockSpec(memory_space=pl.ANY))
        aliases = {len(args): 0}
        args.append(xs_prev)
    return pl.pallas_call(
        _sort_body,
        grid_spec=pltpu.PrefetchScalarGridSpec(
            num_scalar_prefetch=1,
            grid=(t // tm,),
            in_specs=in_specs,
            out_specs=[pl.BlockSpec(memory_space=pl.ANY),
                       pl.BlockSpec((tm, 2 * LANES), lambda i, cd: (i, 0))],
            scratch_shapes=[pltpu.VMEM((2, rows, D_MODEL), BF16), pltpu.SemaphoreType.DMA((2,))]),
        out_shape=[jax.ShapeDtypeStruct((total_rows, D_MODEL), BF16),
                   jax.ShapeDtypeStruct((t, 2 * LANES), F32)],
        input_output_aliases=aliases,
        compiler_params=_cparams("arbitrary"),
        name="sort_tokens",
    )(*args)


def _moe_body(ts_ref, na_ref, xs_ref, wg_ref, wu_ref, wd_ref, ys_ref, xbuf, ybuf, wg16, wu16, wd16, sem_in, sem_out):
    k = pl.program_id(0)
    n_act = na_ref[0]

    def x_copy(g, sl):
        return pltpu.make_async_copy(xs_ref.at[pl.ds(pl.multiple_of(g * MOE_TILE, MOE_TILE), MOE_TILE), :],
                                     xbuf.at[sl], sem_in.at[sl])

    def y_copy(g, sl):
        return pltpu.make_async_copy(ybuf.at[sl],
                                     ys_ref.at[pl.ds(pl.multiple_of(g * MOE_TILE, MOE_TILE), MOE_TILE), :],
                                     sem_out.at[sl])

    @pl.when(k == 0)
    def _():
        for j in range(MOE_AHEAD):
            @pl.when(j < n_act)
            def _():
                x_copy(j, j).start()

    wg16[...] = wg_ref[0].astype(BF16)
    wu16[...] = wu_ref[0].astype(BF16)
    wd16[...] = wd_ref[0].astype(BF16)

    def tile(g, carry):
        sl = g % MOE_BUFS

        @pl.when(g + MOE_AHEAD < n_act)
        def _():
            x_copy(g + MOE_AHEAD, (g + MOE_AHEAD) % MOE_BUFS).start()

        x_copy(g, sl).wait()

        @pl.when(g >= MOE_BUFS)
        def _():
            y_copy(g - MOE_BUFS, sl).wait()

        x = xbuf[sl]
        act = _silu(_dot(x, wg16[...])) * _dot(x, wu16[...])
        ybuf[sl] = _dot(act.astype(BF16), wd16[...]).astype(BF16)
        y_copy(g, sl).start()
        return carry

    lax.fori_loop(ts_ref[k], ts_ref[k + 1], tile, 0)

    @pl.when(k == pl.num_programs(0) - 1)
    def _():
        for j in range(1, MOE_BUFS + 1):
            @pl.when(n_act >= j)
            def _():
                y_copy(n_act - j, (n_act - j) % MOE_BUFS).wait()


def _moe_sorted(tile_start, n_active, xs, wg, wu, wd):
    w_map = lambda k, ts, na: (k, 0, 0)
    return pl.pallas_call(
        _moe_body,
        grid_spec=pltpu.PrefetchScalarGridSpec(
            num_scalar_prefetch=2,
            grid=(N_EXPERTS,),
            in_specs=[pl.BlockSpec(memory_space=pl.ANY),
                      pl.BlockSpec((1, D_MODEL, EXPERT_FF), w_map),
                      pl.BlockSpec((1, D_MODEL, EXPERT_FF), w_map),
                      pl.BlockSpec((1, EXPERT_FF, D_MODEL), w_map)],
            out_specs=pl.BlockSpec(memory_space=pl.ANY),
            scratch_shapes=[pltpu.VMEM((MOE_BUFS, MOE_TILE, D_MODEL), BF16),
                            pltpu.VMEM((MOE_BUFS, MOE_TILE, D_MODEL), BF16),
                            pltpu.VMEM((D_MODEL, EXPERT_FF), BF16), pltpu.VMEM((D_MODEL, EXPERT_FF), BF16),
                            pltpu.VMEM((EXPERT_FF, D_MODEL), BF16),
                            pltpu.SemaphoreType.DMA((MOE_BUFS,)), pltpu.SemaphoreType.DMA((MOE_BUFS,))]),
        out_shape=jax.ShapeDtypeStruct(xs.shape, BF16),
        compiler_params=_cparams("arbitrary"),
        name="moe_sorted",
    )(tile_start, n_active, xs, wg, wu, wd)


def _ple_body(cd_ref, h_ref, dt_ref, wt_ref, p_ref, gn_ref, wg_ref, bg_ref, wp_ref, fn_ref, ys_ref,
              o_ref, loc, sem):
    i = pl.program_id(0)
    tm = h_ref.shape[0]
    rows = loc.shape[1]
    n_chunk = rows // RUN_PAD
    slot = i % 2

    def chunk_copy(step, sl, c):
        return pltpu.make_async_copy(
            ys_ref.at[pl.ds(pl.multiple_of(cd_ref[step, c], RUN_PAD), RUN_PAD), :],
            loc.at[sl, pl.ds(c * RUN_PAD, RUN_PAD), :], sem.at[sl])

    def fetch(step, sl):
        for c in range(n_chunk):
            chunk_copy(step, sl, c).start()

    @pl.when(i == 0)
    def _():
        fetch(0, 0)

    @pl.when(i + 1 < pl.num_programs(0))
    def _():
        fetch(i + 1, 1 - slot)

    for c in range(n_chunk):
        chunk_copy(i, slot, c).wait()

    ci = lax.broadcasted_iota(jnp.int32, (tm, LANES), 1)
    d1 = dt_ref[:, 0:LANES].astype(jnp.int32)
    d2 = dt_ref[:, LANES:2 * LANES].astype(jnp.int32)
    w1 = wt_ref[:, 0:LANES]
    w2 = wt_ref[:, LANES:2 * LANES]
    sel = jnp.concatenate(
        [jnp.where(d1 == ci + m * LANES, w1, jnp.where(d2 == ci + m * LANES, w2, 0.0)) for m in range(rows // LANES)],
        axis=1).astype(BF16)
    h = h_ref[...] + _dot(sel, loc[slot])
    a16 = _rms(h, gn_ref[...]).astype(BF16)
    gate = _sigmoid(_dot(a16, wg_ref[...]) + bg_ref[...])
    pp = _dot(p_ref[...].astype(BF16), wp_ref[...])
    h = h + gate * pp
    o_ref[...] = _rms(h, fn_ref[...])


def _ple(chunk_src, h1, dest_t, gate_t, p, g_ple, wg16, bg, wp16, g_final, ys, tm):
    t = h1.shape[0]
    row = lambda i, cd: (i, 0)
    fixed = lambda i, cd: (0, 0)
    return pl.pallas_call(
        _ple_body,
        grid_spec=pltpu.PrefetchScalarGridSpec(
            num_scalar_prefetch=1,
            grid=(t // tm,),
            in_specs=[pl.BlockSpec((tm, D_MODEL), row), pl.BlockSpec((tm, 2 * LANES), row),
                      pl.BlockSpec((tm, 2 * LANES), row), pl.BlockSpec((tm, PLE_DIM), row),
                      pl.BlockSpec((1, D_MODEL), fixed), pl.BlockSpec((D_MODEL, D_MODEL), fixed),
                      pl.BlockSpec((1, D_MODEL), fixed), pl.BlockSpec((PLE_DIM, D_MODEL), fixed),
                      pl.BlockSpec((1, D_MODEL), fixed), pl.BlockSpec(memory_space=pl.ANY)],
            out_specs=pl.BlockSpec((tm, D_MODEL), row),
            scratch_shapes=[pltpu.VMEM((2, _sorted_rows_per_tile(tm), D_MODEL), BF16),
                            pltpu.SemaphoreType.DMA((2,))]),
        out_shape=jax.ShapeDtypeStruct((t, D_MODEL), F32),
        compiler_params=_cparams("arbitrary"),
        name="ple_final",
    )(chunk_src, h1, dest_t, gate_t, p, g_ple, wg16, bg, wp16, g_final, ys)


MIX_ROWS = 512
SSD_Q = 128
POOL_HDR = 128
CONV_HDR = 8
TAIL_ROWS = 16


def _softplus(x):
    return jnp.maximum(x, 0.0) + jnp.log1p(jnp.exp(-jnp.abs(x)))


def _mixer_constants():
    q = SSD_Q
    li = jnp.arange(q)[:, None]
    tri = (li >= jnp.arange(q)[None, :])
    expand = (jnp.arange(SSD_WIDTH)[None, :] // SSD_HEAD_DIM) == jnp.arange(LANES)[:, None]
    hpg = SSD_HEADS // SSD_GROUPS
    block = (jnp.arange(SSD_BC)[:, None] // SSD_STATE) == (jnp.arange(SSD_WIDTH)[None, :] // (SSD_HEAD_DIM * hpg))
    wj = jnp.arange(2 * q)[None, :]
    band = jnp.stack([(wj <= li + POOL_HDR) & (wj > li + POOL_HDR - w) for w in POOL_WINDOWS])
    return tri.astype(F32), tri.astype(BF16), expand.astype(BF16), block.astype(F32), band.astype(BF16)


def _front_body(x_ref, gm_ref, win_ref, trif_ref, tri_ref, exp_ref, bmask_ref, band_ref,
                cw_ref, cb_ref, dtb_ref, alog_ref, dsk_ref, nw_ref,
                plw_ref, plb_ref, psc_ref, wout_ref, gf_ref, rh_ref, rl_ref, rb_ref,
                h_ref, u_ref, rt_ref, wt_ref, cnt_ref, st_ref, ptail_ref, ctail_ref,
                pool_ext, conv_ext, s_ref, z_ref, dt_ref, mix_ref):
    c = pl.program_id(1)
    rows = MIX_ROWS
    q_len = SSD_Q

    @pl.when(c == 0)
    def _():
        pool_ext[0:POOL_HDR, :] = jnp.zeros((POOL_HDR, POOL_WIDTH), F32)
        conv_ext[0:CONV_HDR, :] = jnp.zeros((CONV_HDR, CONV_DIM), F32)
        s_ref[...] = jnp.zeros_like(s_ref)

    @pl.when(c > 0)
    def _():
        pool_ext[0:POOL_HDR, :] = pool_ext[rows:rows + POOL_HDR, :]
        conv_ext[0:CONV_HDR, :] = conv_ext[rows:rows + CONV_HDR, :]

    causal = trif_ref[...] > 0.5
    tri = tri_ref[...]
    expand = exp_ref[...]
    blockmask = bmask_ref[...]
    lane = lax.broadcasted_iota(jnp.int32, (q_len, LANES), 1)
    left = lane < SSD_HEAD_DIM
    a_neg = -jnp.exp(alog_ref[...])
    rowi = lax.broadcasted_iota(jnp.int32, (q_len, LANES), 0)

    half = rows // 2
    n_cols = 256

    def in_proj_chunks(r0):
        a16 = _rms(x_ref[r0:r0 + half, :], gm_ref[...]).astype(BF16)
        dests = ([(pool_ext, POOL_HDR + r0, k, 0) for k in range(0, POOL_WIDTH, n_cols)]
                 + [(z_ref, r0, k, POOL_WIDTH) for k in range(0, SSD_WIDTH, n_cols)]
                 + [(conv_ext, CONV_HDR + r0, k, POOL_WIDTH + SSD_WIDTH) for k in range(0, CONV_DIM, n_cols)])

        def chunk(ref, row, k, col0):
            def run():
                ref[row:row + half, k:k + n_cols] = _dot(a16, win_ref[:, col0 + k:col0 + k + n_cols])
            return run

        def dt_chunk():
            dt_ref[r0:r0 + half, :] = _dot(a16, win_ref[:, DT_OFF:IN_COLS])

        return [chunk(*d) for d in dests] + [dt_chunk]

    def out_proj_chunks(r0):
        def chunk(k):
            def run():
                h_ref[r0:r0 + half, k:k + n_cols] = (x_ref[r0:r0 + half, k:k + n_cols]
                                                    + _dot(mix_ref[r0:r0 + half, :], wout_ref[:, k:k + n_cols]))
            return run

        def route():
            _route_rows(h_ref[r0:r0 + half, :], gf_ref, rh_ref, rl_ref, rb_ref, u_ref, rt_ref, wt_ref, cnt_ref,
                        r0=r0, accumulate=r0 > 0)

        return [chunk(k) for k in range(0, D_MODEL, n_cols)] + [route]

    def mix_pair(q_first, fillers):
        fillers = list(fillers)

        def fill():
            if fillers:
                fillers.pop(0)()

        subs = range(q_first, q_first + 2)
        xs_l, b16_l, call_l, c16_l = {}, {}, {}, {}
        for q in subs:
            base = CONV_HDR + q * q_len
            conv = cb_ref[...] + conv_ext[base - 3:base - 3 + q_len, :] * cw_ref[0:1, :]
            conv = conv + conv_ext[base - 2:base - 2 + q_len, :] * cw_ref[1:2, :]
            conv = conv + conv_ext[base - 1:base - 1 + q_len, :] * cw_ref[2:3, :]
            conv = conv + conv_ext[base:base + q_len, :] * cw_ref[3:4, :]
            conv = _silu(conv)
            xs_l[q] = conv[:, 0:SSD_WIDTH]
            b16_l[q] = conv[:, SSD_WIDTH:SSD_WIDTH + SSD_BC].astype(BF16)
            call_l[q] = conv[:, SSD_WIDTH + SSD_BC:CONV_DIM]
            c16_l[q] = call_l[q].astype(BF16)
            fill()
        dt_l = {q: _softplus(dt_ref[q * q_len:(q + 1) * q_len, :] + dtb_ref[...]) for q in subs}
        acs_l = {q: _dot_exact_lhs(tri, dt_l[q] * a_neg) for q in subs}
        acst_l = {q: acs_l[q].T for q in subs}
        fill()
        dtx_l = {q: _dot_exact_rhs(dt_l[q], expand, terms=2) for q in subs}
        acsx_l = {q: _dot_exact_rhs(acs_l[q], expand) for q in subs}
        lastx_l = {q: acsx_l[q][q_len - 1:q_len, :] for q in subs}
        xdt_l = {q: xs_l[q] * dtx_l[q] for q in subs}
        xdt16_l = {q: xdt_l[q].astype(BF16) for q in subs}
        fill()
        contrib_l = {q: _dot_tn(b16_l[q], (xdt_l[q] * jnp.exp(lastx_l[q] - acsx_l[q])).astype(BF16)) * blockmask
                     for q in subs}
        cb_l = {q: [_dot_nt(jnp.where(lax.shift_right_logical(lane, 6) == g, call_l[q], 0.0).astype(BF16), b16_l[q])
                    for g in range(SSD_GROUPS)] for q in subs}
        fill()
        zs_l = {}
        for q in subs:
            zs = []
            for h in range(SSD_HEADS):
                blk = slice((h // 2) * LANES, (h // 2 + 1) * LANES)
                seg = acs_l[q][:, h:h + 1] - acst_l[q][h:h + 1, :]
                decay = jnp.where(causal, jnp.exp(seg), 0.0)
                scores = (cb_l[q][h // (SSD_HEADS // SSD_GROUPS)] * decay).astype(BF16)
                zs.append(_dot(scores, xdt16_l[q][:, blk]))
                if h % 4 == 3:
                    fill()
            zs_l[q] = zs
        yoff_l = {}
        for q in subs:
            s_old = s_ref[...]
            yoff_l[q] = _dot(c16_l[q], s_old.astype(BF16)) * jnp.exp(acsx_l[q])
            s_ref[...] = s_old * jnp.exp(lastx_l[q]) + contrib_l[q]
        fill()
        for q in subs:
            r0 = q * q_len
            z = z_ref[r0:r0 + q_len, :]
            y_blocks = []
            for j in range(SSD_HEADS // 2):
                blk = slice(j * LANES, (j + 1) * LANES)
                y = jnp.where(left, zs_l[q][2 * j], zs_l[q][2 * j + 1]) + yoff_l[q][:, blk]
                y = y + dsk_ref[:, blk] * xs_l[q][:, blk]
                y_blocks.append(y * _silu(z[:, blk]))
            for g in range(SSD_GROUPS):
                y0, y1 = y_blocks[2 * g], y_blocks[2 * g + 1]
                ss = jnp.sum(y0 * y0, axis=-1, keepdims=True) + jnp.sum(y1 * y1, axis=-1, keepdims=True)
                rs = lax.rsqrt(ss * (1.0 / (2 * LANES)) + EPS)
                for k, yk in ((2 * g, y0), (2 * g + 1, y1)):
                    blk = slice(k * LANES, (k + 1) * LANES)
                    out = yk * rs * nw_ref[:, blk]
                    mix_ref[r0:r0 + q_len, POOL_WIDTH + k * LANES:POOL_WIDTH + (k + 1) * LANES] = out.astype(BF16)
            fill()
        for q in subs:
            r0 = q * q_len
            pos = c * rows + r0 + rowi
            for g, w in enumerate(POOL_WINDOWS):
                blk = slice(g * POOL_GROUP_DIM, (g + 1) * POOL_GROUP_DIM)
                pe = pool_ext[r0:r0 + 2 * q_len, blk]
                winsum = _dot_exact_lhs(band_ref[g], pe, terms=2)
                cnt = jnp.minimum(pos + 1, w).astype(F32)
                m = winsum / cnt - pe[q_len:2 * q_len, :]
                yg = _dot(m.astype(BF16), plw_ref[g]) + plb_ref[:, blk]
                mix_ref[r0:r0 + q_len, blk] = (yg * psc_ref[:, blk]).astype(BF16)
            fill()
        while fillers:
            fill()

    for run in in_proj_chunks(0):
        run()
    mix_pair(0, in_proj_chunks(half))
    mix_pair(2, out_proj_chunks(0))
    for run in out_proj_chunks(half):
        run()

    @pl.when(c == pl.num_programs(1) - 1)
    def _():
        st_ref[0] = s_ref[...]
        ptail_ref[0] = pool_ext[POOL_HDR + rows - TAIL_ROWS:POOL_HDR + rows, :]
        ctail_ref[0] = conv_ext[rows:rows + CONV_HDR, :]


def _front_prompt(x, g_mix, w_in16, cw, cb, dtb, alog, dsk, nw, plw16, plb, psc, w_out16, g_ffn, r_hi, r_lo, r_bias,
                  nb, seq):
    steps = seq // MIX_ROWS
    nt = nb * steps
    row = lambda b, c: (b * steps + c, 0)
    tile3 = lambda b, c: (b * steps + c, 0, 0)
    per_seq = lambda b, c: (b, 0, 0)
    fixed2 = lambda b, c: (0, 0)
    return pl.pallas_call(
        _front_body,
        grid=(nb, steps),
        in_specs=[pl.BlockSpec((MIX_ROWS, D_MODEL), row), pl.BlockSpec((1, D_MODEL), fixed2),
                  pl.BlockSpec((D_MODEL, IN_COLS), fixed2),
                  pl.BlockSpec((SSD_Q, SSD_Q), fixed2), pl.BlockSpec((SSD_Q, SSD_Q), fixed2),
                  pl.BlockSpec((LANES, SSD_WIDTH), fixed2), pl.BlockSpec((SSD_BC, SSD_WIDTH), fixed2),
                  pl.BlockSpec((len(POOL_WINDOWS), SSD_Q, 2 * SSD_Q), lambda b, c: (0, 0, 0)),
                  pl.BlockSpec((SSD_CONV, CONV_DIM), fixed2), pl.BlockSpec((1, CONV_DIM), fixed2),
                  pl.BlockSpec((1, LANES), fixed2), pl.BlockSpec((1, LANES), fixed2),
                  pl.BlockSpec((1, SSD_WIDTH), fixed2), pl.BlockSpec((1, SSD_WIDTH), fixed2),
                  pl.BlockSpec((len(POOL_WINDOWS), POOL_GROUP_DIM, POOL_GROUP_DIM), lambda b, c: (0, 0, 0)),
                  pl.BlockSpec((1, POOL_WIDTH), fixed2), pl.BlockSpec((1, POOL_WIDTH), fixed2),
                  pl.BlockSpec((D_MODEL, D_MODEL), fixed2), pl.BlockSpec((1, D_MODEL), fixed2),
                  pl.BlockSpec((LANES, D_MODEL), fixed2), pl.BlockSpec((LANES, D_MODEL), fixed2),
                  pl.BlockSpec((LANES, MIX_ROWS), fixed2)],
        out_specs=[pl.BlockSpec((MIX_ROWS, D_MODEL), row), pl.BlockSpec((MIX_ROWS, D_MODEL), row),
                   pl.BlockSpec((1, 8, MIX_ROWS), tile3), pl.BlockSpec((MIX_ROWS, 2 * LANES), row),
                   pl.BlockSpec((1, N_EXPERTS, LANES), tile3),
                   pl.BlockSpec((1, SSD_BC, SSD_WIDTH), per_seq),
                   pl.BlockSpec((1, TAIL_ROWS, POOL_WIDTH), per_seq),
                   pl.BlockSpec((1, CONV_HDR, CONV_DIM), per_seq)],
        out_shape=[jax.ShapeDtypeStruct((nb * seq, D_MODEL), F32), jax.ShapeDtypeStruct((nb * seq, D_MODEL), BF16),
                   jax.ShapeDtypeStruct((nt, 8, MIX_ROWS), F32), jax.ShapeDtypeStruct((nb * seq, 2 * LANES), F32),
                   jax.ShapeDtypeStruct((nt, N_EXPERTS, LANES), F32),
                   jax.ShapeDtypeStruct((nb, SSD_BC, SSD_WIDTH), F32),
                   jax.ShapeDtypeStruct((nb, TAIL_ROWS, POOL_WIDTH), F32),
                   jax.ShapeDtypeStruct((nb, CONV_HDR, CONV_DIM), F32)],
        scratch_shapes=[pltpu.VMEM((POOL_HDR + MIX_ROWS, POOL_WIDTH), F32),
                        pltpu.VMEM((CONV_HDR + MIX_ROWS, CONV_DIM), F32),
                        pltpu.VMEM((SSD_BC, SSD_WIDTH), F32),
                        pltpu.VMEM((MIX_ROWS, SSD_WIDTH), F32), pltpu.VMEM((MIX_ROWS, LANES), F32),
                        pltpu.VMEM((MIX_ROWS, D_MODEL), BF16)],
        compiler_params=_cparams("parallel", "arbitrary"),
        name="front_prompt",
    )(x, g_mix, w_in16, *_mixer_constants(), cw, cb, dtb, alog, dsk, nw, plw16, plb, psc, w_out16, g_ffn,
      r_hi, r_lo, r_bias)


def _mix_step_body(vp_ref, z_ref, xbc_ref, dt_ref, sp_ref, sc_ref, st_ref, cw_ref, cb_ref, dtb_ref, alog_ref,
                   dsk_ref, nw_ref, plw_ref, plb_ref, psc_ref, mix_ref, po_ref, co_ref, so_ref,
                   xdt_t, dec_t, b_t, c_t, xs_keep, y_t):
    h = pl.program_id(0)

    @pl.when(h == 0)
    def _():
        xbc = xbc_ref[...]
        conv = cb_ref[...] + sc_ref[0] * cw_ref[0:1, :]
        conv = conv + sc_ref[1] * cw_ref[1:2, :]
        conv = conv + sc_ref[2] * cw_ref[2:3, :]
        conv = conv + xbc * cw_ref[3:4, :]
        conv = _silu(conv)
        co_ref[0] = sc_ref[1]
        co_ref[1] = sc_ref[2]
        co_ref[2] = xbc
        xs = conv[:, 0:SSD_WIDTH]
        xs_keep[...] = xs
        b_t[...] = conv[:, SSD_WIDTH:SSD_WIDTH + SSD_BC].T
        c_t[...] = conv[:, SSD_WIDTH + SSD_BC:CONV_DIM].T
        dt = _softplus(dt_ref[...] + dtb_ref[...])
        d_a = dt * (-jnp.exp(alog_ref[...]))
        dt_t = dt.T
        dec_t[...] = jnp.exp(d_a).T
        xs_t = xs.T
        for k in range(SSD_HEADS):
            blk = slice(k * SSD_HEAD_DIM, (k + 1) * SSD_HEAD_DIM)
            xdt_t[blk, :] = xs_t[blk, :] * dt_t[k:k + 1, :]
        v = vp_ref[...]
        for k in range(POOL_BUF - 1):
            po_ref[k] = sp_ref[k + 1]
        po_ref[POOL_BUF - 1] = v
        for g, w in enumerate(POOL_WINDOWS):
            blk = slice(g * POOL_GROUP_DIM, (g + 1) * POOL_GROUP_DIM)
            acc = sp_ref[POOL_BUF - (w - 1), :, blk]
            for k in range(w - 2, 0, -1):
                acc = acc + sp_ref[POOL_BUF - k, :, blk]
            acc = acc + v[:, blk]
            m = acc / float(min(PAST_LEN + 1, w)) - v[:, blk]
            yg = _dot(m.astype(BF16), plw_ref[g]) + plb_ref[:, blk]
            mix_ref[:, blk] = (yg * psc_ref[:, blk]).astype(BF16)

    g_off = pl.multiple_of((h // (SSD_HEADS // SSD_GROUPS)) * SSD_STATE, SSD_STATE)
    h_off = pl.multiple_of(h * SSD_HEAD_DIM, SSD_HEAD_DIM)
    b_g = b_t[pl.ds(g_off, SSD_STATE), :]
    c_g = c_t[pl.ds(g_off, SSD_STATE), :]
    dec = dec_t[pl.ds(h, 1), :]
    xdt = xdt_t[pl.ds(h_off, SSD_HEAD_DIM), :]
    y_rows = []
    for p in range(SSD_HEAD_DIM):
        s_new = st_ref[0, p] * dec + xdt[p:p + 1, :] * b_g
        so_ref[0, p] = s_new
        y_rows.append(jnp.sum(s_new * c_g, axis=0, keepdims=True))
    y_t[pl.ds(h_off, SSD_HEAD_DIM), :] = jnp.concatenate(y_rows, axis=0)

    @pl.when(h == pl.num_programs(0) - 1)
    def _():
        xs = xs_keep[...]
        y = y_t[...].T + dsk_ref[...] * xs
        y = y * _silu(z_ref[...])
        width = SSD_WIDTH // SSD_GROUPS
        for g in range(SSD_GROUPS):
            blk = slice(g * width, (g + 1) * width)
            yg = y[:, blk]
            rs = lax.rsqrt(jnp.mean(yg * yg, axis=-1, keepdims=True) + EPS)
            mix_ref[:, POOL_WIDTH + g * width:POOL_WIDTH + (g + 1) * width] = (yg * rs * nw_ref[:, blk]).astype(BF16)


def _mix_step(vp, z, xbc, dt, pool_t, conv_t, ssm_t, cw, cb, dtb, alog, dsk, nw, plw16, plb, psc):
    n = vp.shape[0]
    fixed2 = lambda h: (0, 0)
    fixed3 = lambda h: (0, 0, 0)
    st_spec = pl.BlockSpec((1, SSD_HEAD_DIM, SSD_STATE, n), lambda h: (h, 0, 0, 0))
    return pl.pallas_call(
        _mix_step_body,
        grid=(SSD_HEADS,),
        in_specs=[pl.BlockSpec((n, POOL_WIDTH), fixed2), pl.BlockSpec((n, SSD_WIDTH), fixed2),
                  pl.BlockSpec((n, CONV_DIM), fixed2), pl.BlockSpec((n, LANES), fixed2),
                  pl.BlockSpec((POOL_BUF, n, POOL_WIDTH), fixed3),
                  pl.BlockSpec((SSD_CONV - 1, n, CONV_DIM), fixed3),
                  st_spec,
                  pl.BlockSpec((SSD_CONV, CONV_DIM), fixed2), pl.BlockSpec((1, CONV_DIM), fixed2),
                  pl.BlockSpec((1, LANES), fixed2), pl.BlockSpec((1, LANES), fixed2),
                  pl.BlockSpec((1, SSD_WIDTH), fixed2), pl.BlockSpec((1, SSD_WIDTH), fixed2),
                  pl.BlockSpec((len(POOL_WINDOWS), POOL_GROUP_DIM, POOL_GROUP_DIM), fixed3),
                  pl.BlockSpec((1, POOL_WIDTH), fixed2), pl.BlockSpec((1, POOL_WIDTH), fixed2)],
        out_specs=[pl.BlockSpec((n, D_MODEL), fixed2), pl.BlockSpec((POOL_BUF, n, POOL_WIDTH), fixed3),
                   pl.BlockSpec((SSD_CONV - 1, n, CONV_DIM), fixed3), st_spec],
        out_shape=[jax.ShapeDtypeStruct((n, D_MODEL), BF16), jax.ShapeDtypeStruct(pool_t.shape, F32),
                   jax.ShapeDtypeStruct(conv_t.shape, F32), jax.ShapeDtypeStruct(ssm_t.shape, F32)],
        scratch_shapes=[pltpu.VMEM((SSD_WIDTH, n), F32), pltpu.VMEM((LANES, n), F32),
                        pltpu.VMEM((SSD_BC, n), F32), pltpu.VMEM((SSD_BC, n), F32),
                        pltpu.VMEM((n, SSD_WIDTH), F32), pltpu.VMEM((SSD_WIDTH, n), F32)],
        compiler_params=_cparams("arbitrary"),
        name="mix_step",
    )(vp, z, xbc, dt, pool_t, conv_t, ssm_t, cw, cb, dtb, alog, dsk, nw, plw16, plb, psc)


PROMPT_TILE = 512


def _sort_tables(counts, tile_tokens, data_rows):
    cnt = jnp.concatenate(counts, axis=0)
    pc = (cnt + RUN_PAD - 1) // RUN_PAD * RUN_PAD
    off_local = jnp.cumsum(pc, axis=1) - pc
    tile_rows = jnp.sum(pc, axis=1)
    region = (jnp.sum(pc, axis=0) + MOE_TILE - 1) // MOE_TILE * MOE_TILE
    base = jnp.cumsum(region) - region
    dst = base[None, :] + jnp.cumsum(pc, axis=0) - pc
    per_pass = []
    lo = 0
    for c, tm in zip(counts, tile_tokens):
        hi = lo + c.shape[0]
        n_chunk = _sorted_rows_per_tile(tm) // RUN_PAD
        s = jnp.arange(n_chunk, dtype=jnp.int32) * RUN_PAD
        begins = off_local[lo:hi, None, :]
        ends = begins + pc[lo:hi, None, :]
        inside = (begins <= s[None, :, None]) & (s[None, :, None] < ends)
        shift = jnp.sum(jnp.where(inside, dst[lo:hi, None, :] - begins, 0), axis=2)
        valid = s[None, :] < tile_rows[lo:hi, None]
        parity = (jnp.arange(hi - lo, dtype=jnp.int32) % 2)[:, None]
        spare = data_rows + parity * _sorted_rows_per_tile(max(tile_tokens)) + s[None, :]
        scatter_dst = jnp.where(valid, s[None, :] + shift, spare).astype(jnp.int32)
        gather_src = jnp.where(valid, s[None, :] + shift, 0).astype(jnp.int32)
        off_v = jnp.broadcast_to(off_local[lo:hi, :, None].astype(F32), (hi - lo, N_EXPERTS, LANES))
        per_pass.append((scatter_dst, gather_src, off_v))
        lo = hi
    tiles_cum = jnp.cumsum(region // MOE_TILE)
    n_active = tiles_cum[-1]
    return per_pass, tiles_cum, n_active


def _max_sorted_rows(tile_counts, tile_tokens):
    rows = sum(n * (2 * tm + N_EXPERTS * (RUN_PAD - 1)) for n, tm in zip(tile_counts, tile_tokens))
    data_rows = (-(-rows // MOE_TILE) + N_EXPERTS) * MOE_TILE
    return data_rows, data_rows + 2 * _sorted_rows_per_tile(max(tile_tokens))


def _row(v):
    return v.reshape(1, -1).astype(F32)


def _pad_lanes(v):
    return jnp.pad(v.reshape(1, -1).astype(F32), ((0, 0), (0, LANES - v.size)))


def _state_from_blocks(st):
    n = st.shape[0]
    hpg = SSD_HEADS // SSD_GROUPS
    s6 = st.reshape(n, SSD_GROUPS, SSD_STATE, SSD_GROUPS, hpg, SSD_HEAD_DIM)
    per_group = [s6[:, g, :, g] for g in range(SSD_GROUPS)]
    s = jnp.stack(per_group, axis=1)
    return jnp.transpose(s, (0, 1, 3, 4, 2)).reshape(n, SSD_HEADS, SSD_HEAD_DIM, SSD_STATE)


def kernel(x_prompt, x_sample, p_prompt, p_sample, state_pool, state_conv, state_ssm, norm_mix, w_in, pool_lin_w, pool_lin_b, pool_scale, conv_w, conv_b, dt_bias, a_log, d_skip, ssd_norm, w_out, norm_ffn, router_grp_w, router_grp_b, router_exp_w, router_exp_b, exp_w_gate, exp_w_up, exp_w_down, norm_ple, ple_gate_w, ple_gate_b, ple_proj_w, norm_final):
    nb, seq, _ = x_prompt.shape
    ns = x_sample.shape[0]
    assert ns == LANES and x_sample.shape[1] == 1 and seq % MIX_ROWS == 0 and seq >= POOL_BUF

    w_in16 = jnp.pad(w_in[0], ((0, 0), (0, IN_COLS - w_in.shape[2]))).astype(BF16)
    w_out16 = w_out[0].astype(BF16)
    g_mix, g_ffn, g_ple, g_fin = _row(norm_mix[0]), _row(norm_ffn[0]), _row(norm_ple[0]), _row(norm_final)
    cw, cb = conv_w[0].astype(F32), _row(conv_b[0])
    dtb, alog = _pad_lanes(dt_bias[0]), _pad_lanes(a_log[0])
    dsk = _row(jnp.repeat(d_skip[0], SSD_HEAD_DIM))
    nw = _row(ssd_norm[0])
    plw16 = pool_lin_w[0].astype(BF16)
    plb, psc = _row(pool_lin_b[0]), _row(pool_scale[0])
    zeros4 = jnp.zeros((D_MODEL, 8 - N_EXPERT_GROUPS), F32)
    r_w = jnp.concatenate([router_grp_w[0], zeros4, router_exp_w[0],
                           jnp.zeros((D_MODEL, LANES - 8 - N_EXPERTS), F32)], axis=1).T
    r_hi = r_w.astype(BF16)
    r_lo = (r_w - r_hi.astype(F32)).astype(BF16)
    r_b = jnp.concatenate([router_grp_b[0], jnp.zeros((8 - N_EXPERT_GROUPS,), F32), router_exp_b[0],
                           jnp.zeros((LANES - 8 - N_EXPERTS,), F32)])
    wg = exp_w_gate[0].reshape(N_EXPERTS, D_MODEL, EXPERT_FF)
    wu = exp_w_up[0].reshape(N_EXPERTS, D_MODEL, EXPERT_FF)
    wd = exp_w_down[0].reshape(N_EXPERTS, EXPERT_FF, D_MODEL)
    pg16 = ple_gate_w[0].astype(BF16)
    pgb = _row(ple_gate_b[0])
    pp16 = ple_proj_w[0].astype(BF16)

    def route(x, mix16, tm):
        r_bias = jnp.broadcast_to(r_b[:, None], (LANES, tm))
        return _out_proj(x, mix16, w_out16, g_ffn, r_hi, r_lo, r_bias, tm)

    xp = x_prompt.reshape(nb * seq, D_MODEL)
    h1_p, u_p, rt_p, gate_p, cnt_p, st, pool_tail, conv_tail = _front_prompt(
        xp, g_mix, w_in16, cw, cb, dtb, alog, dsk, nw, plw16, plb, psc, w_out16, g_ffn, r_hi, r_lo,
        jnp.broadcast_to(r_b[:, None], (LANES, MIX_ROWS)), nb, seq)
    pool_p = pool_tail[:, TAIL_ROWS - POOL_BUF:]
    conv_p = conv_tail[:, CONV_HDR - (SSD_CONV - 1):]
    ssm_p = _state_from_blocks(st)

    xs_ = x_sample.reshape(ns, D_MODEL)
    vp_s, z_s, xbc_s, dt_s = _in_proj(xs_, g_mix, w_in16, ns)
    mix_s, pool_t, conv_t, ssm_t = _mix_step(
        vp_s, z_s, xbc_s, dt_s, jnp.transpose(state_pool[0], (1, 0, 2)), jnp.transpose(state_conv[0], (1, 0, 2)),
        jnp.transpose(state_ssm[0], (1, 2, 3, 0)), cw, cb, dtb, alog, dsk, nw, plw16, plb, psc)
    h1_s, u_s, rt_s, gate_s, cnt_s = route(xs_, mix_s, ns)
    pool_s = jnp.transpose(pool_t, (1, 0, 2))
    conv_s = jnp.transpose(conv_t, (1, 0, 2))
    ssm_s = jnp.transpose(ssm_t, (3, 0, 1, 2))

    counts = [cnt_p[:, :, 0].astype(jnp.int32), cnt_s[:, :, 0].astype(jnp.int32)]
    tiles = (PROMPT_TILE, ns)
    data_rows, total_rows = _max_sorted_rows([c.shape[0] for c in counts], tiles)
    (tab_p, tab_s), tiles_cum, n_active = _sort_tables(counts, tiles, data_rows)
    xs_sorted, dest_p = _sort_tokens(tab_p[0], u_p, rt_p, tab_p[2], None, PROMPT_TILE, total_rows)
    xs_sorted, dest_s = _sort_tokens(tab_s[0], u_s, rt_s, tab_s[2], xs_sorted, ns, total_rows)
    tile_start = jnp.concatenate([jnp.zeros((1,), jnp.int32), tiles_cum.astype(jnp.int32)])
    ys_sorted = _moe_sorted(tile_start, n_active.reshape(1).astype(jnp.int32), xs_sorted, wg, wu, wd)

    y_prompt = _ple(tab_p[1], h1_p, dest_p, gate_p, p_prompt[0].reshape(nb * seq, PLE_DIM),
                    g_ple, pg16, pgb, pp16, g_fin, ys_sorted, PROMPT_TILE)
    y_sample = _ple(tab_s[1], h1_s, dest_s, gate_s, p_sample[0].reshape(ns, PLE_DIM),
                    g_ple, pg16, pgb, pp16, g_fin, ys_sorted, ns)

    return (y_prompt.reshape(nb, seq, D_MODEL), y_sample.reshape(ns, 1, D_MODEL),
            pool_p[None], conv_p[None], ssm_p[None], pool_s[None], conv_s[None], ssm_s[None])
```

```python
import functools

import jax
import jax.numpy as jnp
from jax import lax
from jax.experimental import pallas as pl
from jax.experimental.pallas import tpu as pltpu

F32 = jnp.float32
BF16 = jnp.bfloat16

D_MODEL = 1024
POOL_WIDTH = 512
POOL_WINDOWS = (2, 4, 8, 16)
POOL_GROUP_DIM = 128
POOL_BUF = 15
SSD_WIDTH = 512
SSD_HEAD_DIM = 64
SSD_HEADS = 8
SSD_GROUPS = 2
SSD_STATE = 64
SSD_CONV = 4
SSD_BC = SSD_GROUPS * SSD_STATE
CONV_DIM = SSD_WIDTH + 2 * SSD_BC
N_EXPERT_GROUPS = 4
EXPERTS_PER_GROUP = 8
N_EXPERTS = N_EXPERT_GROUPS * EXPERTS_PER_GROUP
EXPERT_FF = 256
PLE_DIM = 256
PAST_LEN = 16384
EPS = 1e-6

LANES = 128
IN_COLS = 1920
DT_OFF = POOL_WIDTH + SSD_WIDTH + CONV_DIM
VMEM_LIMIT = 56 * 1024 * 1024


def _cparams(*sem):
    return pltpu.CompilerParams(dimension_semantics=sem, vmem_limit_bytes=VMEM_LIMIT)


def _rms(x, g):
    return x * lax.rsqrt(jnp.mean(x * x, axis=-1, keepdims=True) + EPS) * g


def _sigmoid(x):
    return 1.0 / (1.0 + jnp.exp(-x))


def _silu(x):
    return x * _sigmoid(x)


def _split3(v):
    hi = v.astype(BF16)
    r = v - hi.astype(F32)
    mid = r.astype(BF16)
    lo = (r - mid.astype(F32)).astype(BF16)
    return hi, mid, lo


def _dot(a, b):
    return jnp.dot(a, b, preferred_element_type=F32)


def _dot_nt(a, b):
    return lax.dot_general(a, b, (((1,), (1,)), ((), ())), preferred_element_type=F32)


def _dot_tn(a, b):
    return lax.dot_general(a, b, (((0,), (0,)), ((), ())), preferred_element_type=F32)


def _dot_exact_lhs(sel, v, terms=3):
    acc = None
    for t in _split3(v)[:terms]:
        p = _dot(sel, t)
        acc = p if acc is None else acc + p
    return acc


def _dot_exact_rhs(v, sel, terms=3):
    acc = None
    for t in _split3(v)[:terms]:
        p = _dot(t, sel)
        acc = p if acc is None else acc + p
    return acc


def _in_proj_body(x_ref, g_ref, w_ref, vp_ref, z_ref, xbc_ref, dt_ref):
    a16 = _rms(x_ref[...], g_ref[...]).astype(BF16)
    vp_ref[...] = _dot(a16, w_ref[:, 0:POOL_WIDTH])
    z_ref[...] = _dot(a16, w_ref[:, POOL_WIDTH:POOL_WIDTH + SSD_WIDTH])
    xbc_ref[...] = _dot(a16, w_ref[:, POOL_WIDTH + SSD_WIDTH:DT_OFF])
    dt_ref[...] = _dot(a16, w_ref[:, DT_OFF:IN_COLS])


def _in_proj(x, g, w16, tm):
    t = x.shape[0]
    row = lambda i: (i, 0)
    fixed = lambda i: (0, 0)
    return pl.pallas_call(
        _in_proj_body,
        grid=(t // tm,),
        in_specs=[pl.BlockSpec((tm, D_MODEL), row), pl.BlockSpec((1, D_MODEL), fixed),
                  pl.BlockSpec((D_MODEL, IN_COLS), fixed)],
        out_specs=[pl.BlockSpec((tm, POOL_WIDTH), row), pl.BlockSpec((tm, SSD_WIDTH), row),
                   pl.BlockSpec((tm, CONV_DIM), row), pl.BlockSpec((tm, LANES), row)],
        out_shape=[jax.ShapeDtypeStruct((t, POOL_WIDTH), F32), jax.ShapeDtypeStruct((t, SSD_WIDTH), F32),
                   jax.ShapeDtypeStruct((t, CONV_DIM), F32), jax.ShapeDtypeStruct((t, LANES), F32)],
        compiler_params=_cparams("parallel"),
        name="in_proj",
    )(x, g, w16)


def _route(lg):
    tm = lg.shape[1]
    gl = lg[0:N_EXPERT_GROUPS, :]
    gmax = jnp.max(gl, axis=0, keepdims=True)
    gsum = jnp.sum(jnp.exp(gl - gmax), axis=0, keepdims=True)
    g_w = 1.0 / gsum
    gi = lax.broadcasted_iota(jnp.int32, gl.shape, 0)
    g_idx = jnp.min(jnp.where(gl == gmax, gi, N_EXPERT_GROUPS), axis=0, keepdims=True)
    sel = jnp.zeros((EXPERTS_PER_GROUP, tm), F32)
    for g in range(N_EXPERT_GROUPS):
        blk = lg[8 + g * EXPERTS_PER_GROUP:8 + (g + 1) * EXPERTS_PER_GROUP, :]
        sel = jnp.where(g_idx == g, blk, sel)
    ei = lax.broadcasted_iota(jnp.int32, sel.shape, 0)
    m1 = jnp.max(sel, axis=0, keepdims=True)
    i1 = jnp.min(jnp.where(sel == m1, ei, EXPERTS_PER_GROUP), axis=0, keepdims=True)
    rest = jnp.where(ei == i1, -jnp.inf, sel)
    m2 = jnp.max(rest, axis=0, keepdims=True)
    i2 = jnp.min(jnp.where(rest == m2, ei, EXPERTS_PER_GROUP), axis=0, keepdims=True)
    p2 = jnp.exp(m2 - m1)
    w1 = g_w / (1.0 + p2)
    w2 = g_w * p2 / (1.0 + p2)
    return g_idx * EXPERTS_PER_GROUP + i1, g_idx * EXPERTS_PER_GROUP + i2, w1, w2


def _out_proj_body(x_ref, mix_ref, w_ref, g_ref, rh_ref, rl_ref, rb_ref, h_ref, u_ref, rt_ref, wt_ref, cnt_ref):
    _out_proj_and_route(x_ref[...], mix_ref[...], w_ref, g_ref, rh_ref, rl_ref, rb_ref,
                        h_ref, u_ref, rt_ref, wt_ref, cnt_ref)


def _out_proj_and_route(x, mix16, w_ref, g_ref, rh_ref, rl_ref, rb_ref, h_ref, u_ref, rt_ref, wt_ref, cnt_ref,
                        r0=0, accumulate=False):
    tm = x.shape[0]
    h = x + _dot(mix16, w_ref[...])
    h_ref[r0:r0 + tm, :] = h
    _route_rows(h, g_ref, rh_ref, rl_ref, rb_ref, u_ref, rt_ref, wt_ref, cnt_ref, r0=r0, accumulate=accumulate)


def _route_rows(h, g_ref, rh_ref, rl_ref, rb_ref, u_ref, rt_ref, wt_ref, cnt_ref, r0=0, accumulate=False):
    tm = h.shape[0]
    u = _rms(h, g_ref[...])
    u_hi = u.astype(BF16)
    u_ref[r0:r0 + tm, :] = u_hi
    u_lo = (u - u_hi.astype(F32)).astype(BF16)
    lg = (_dot_nt(rh_ref[...], u_hi) + _dot_nt(rh_ref[...], u_lo) + _dot_nt(rl_ref[...], u_hi)
          + rb_ref[:, r0:r0 + tm])
    b1, b2, w1, w2 = _route(lg)
    r8 = lax.broadcasted_iota(jnp.int32, (8, tm), 0)
    rt_ref[0, :, r0:r0 + tm] = jnp.where(r8 == 0, b1.astype(F32), jnp.where(r8 == 1, b2.astype(F32), 0.0))
    wt_ref[r0:r0 + tm, :] = jnp.concatenate(
        [jnp.broadcast_to(w1, (LANES, tm)).T, jnp.broadcast_to(w2, (LANES, tm)).T], axis=1)
    kio = lax.broadcasted_iota(jnp.int32, (N_EXPERTS, tm), 0)
    hits = ((kio == b1) | (kio == b2)).astype(F32)
    cnt = jnp.broadcast_to(jnp.sum(hits, axis=1, keepdims=True), (N_EXPERTS, LANES))
    cnt_ref[0] = cnt_ref[0] + cnt if accumulate else cnt


def _out_proj(x, mix16, w16, g, r_hi, r_lo, r_bias, tm):
    t = x.shape[0]
    nt = t // tm
    row = lambda i: (i, 0)
    fixed = lambda i: (0, 0)
    tile3 = lambda i: (i, 0, 0)
    return pl.pallas_call(
        _out_proj_body,
        grid=(nt,),
        in_specs=[pl.BlockSpec((tm, D_MODEL), row), pl.BlockSpec((tm, D_MODEL), row),
                  pl.BlockSpec((D_MODEL, D_MODEL), fixed), pl.BlockSpec((1, D_MODEL), fixed),
                  pl.BlockSpec((LANES, D_MODEL), fixed), pl.BlockSpec((LANES, D_MODEL), fixed),
                  pl.BlockSpec((LANES, tm), fixed)],
        out_specs=[pl.BlockSpec((tm, D_MODEL), row), pl.BlockSpec((tm, D_MODEL), row),
                   pl.BlockSpec((1, 8, tm), tile3), pl.BlockSpec((tm, 2 * LANES), row),
                   pl.BlockSpec((1, N_EXPERTS, LANES), tile3)],
        out_shape=[jax.ShapeDtypeStruct((t, D_MODEL), F32), jax.ShapeDtypeStruct((t, D_MODEL), BF16),
                   jax.ShapeDtypeStruct((nt, 8, tm), F32), jax.ShapeDtypeStruct((t, 2 * LANES), F32),
                   jax.ShapeDtypeStruct((nt, N_EXPERTS, LANES), F32)],
        compiler_params=_cparams("parallel"),
        name="out_proj_router",
    )(x, mix16, w16, g, r_hi, r_lo, r_bias)


RUN_PAD = 16
MOE_TILE = 256
MOE_BUFS = 6
MOE_AHEAD = 4


def _tile_lanes(v, width):
    reps = width // LANES
    return v if reps == 1 else jnp.concatenate([v] * reps, axis=1)


def _sorted_rows_per_tile(tm):
    need = 2 * tm + N_EXPERTS * (RUN_PAD - 1)
    return -(-need // LANES) * LANES


def _sort_body(cd_ref, u_ref, rt_ref, off_ref, *rest):
    xs_ref, dt_ref, loc, sem = rest[-4:]
    i = pl.program_id(0)
    tm = u_ref.shape[0]
    rows = loc.shape[1]
    rt = rt_ref[0]
    b1 = rt[0:1, :].astype(jnp.int32)
    b2 = rt[1:2, :].astype(jnp.int32)
    kio = lax.broadcasted_iota(jnp.int32, (N_EXPERTS, tm), 0)
    o1 = kio == b1
    o2 = kio == b2
    before = (lax.broadcasted_iota(jnp.int32, (tm, tm), 0) < lax.broadcasted_iota(jnp.int32, (tm, tm), 1))
    start = _dot((o1 | o2).astype(BF16), before.astype(BF16)) + _tile_lanes(off_ref[0], tm)
    d1 = jnp.sum(jnp.where(o1, start, 0.0), axis=0, keepdims=True)
    d2 = jnp.sum(jnp.where(o2, start, 0.0), axis=0, keepdims=True)
    dt_ref[...] = jnp.concatenate([jnp.broadcast_to(d1, (LANES, tm)).T, jnp.broadcast_to(d2, (LANES, tm)).T], axis=1)
    rio = lax.broadcasted_iota(jnp.int32, (rows, tm), 0)
    perm = ((rio == d1.astype(jnp.int32)) | (rio == d2.astype(jnp.int32))).astype(BF16)
    slot = i % 2
    loc[slot] = _dot(perm, u_ref[...]).astype(BF16)

    def chunk_copy(step, sl, c):
        return pltpu.make_async_copy(
            loc.at[sl, pl.ds(c * RUN_PAD, RUN_PAD), :],
            xs_ref.at[pl.ds(pl.multiple_of(cd_ref[step, c], RUN_PAD), RUN_PAD), :], sem.at[sl])

    n_chunk = rows // RUN_PAD
    for c in range(n_chunk):
        chunk_copy(i, slot, c).start()

    @pl.when(i > 0)
    def _():
        for c in range(n_chunk):
            chunk_copy(i - 1, 1 - slot, c).wait()

    @pl.when(i == pl.num_programs(0) - 1)
    def _():
        for c in range(n_chunk):
            chunk_copy(i, slot, c).wait()


def _sort_tokens(chunk_dst, u16, rt, off_v, xs_prev, tm, total_rows):
    t = u16.shape[0]
    rows = _sorted_rows_per_tile(tm)
    in_specs = [pl.BlockSpec((tm, D_MODEL), lambda i, cd: (i, 0)),
                pl.BlockSpec((1, 8, tm), lambda i, cd: (i, 0, 0)),
                pl.BlockSpec((1, N_EXPERTS, LANES), lambda i, cd: (i, 0, 0))]
    args = [chunk_dst, u16, rt, off_v]
    aliases = {}
    if xs_prev is not None:
        in_specs.append(pl.BlockSpec(memory_space=pl.ANY))
        aliases = {len(args): 0}
        args.append(xs_prev)
    return pl.pallas_call(
        _sort_body,
        grid_spec=pltpu.PrefetchScalarGridSpec(
            num_scalar_prefetch=1,
            grid=(t // tm,),
            in_specs=in_specs,
            out_specs=[pl.BlockSpec(memory_space=pl.ANY),
                       pl.BlockSpec((tm, 2 * LANES), lambda i, cd: (i, 0))],
            scratch_shapes=[pltpu.VMEM((2, rows, D_MODEL), BF16), pltpu.SemaphoreType.DMA((2,))]),
        out_shape=[jax.ShapeDtypeStruct((total_rows, D_MODEL), BF16),
                   jax.ShapeDtypeStruct((t, 2 * LANES), F32)],
        input_output_aliases=aliases,
        compiler_params=_cparams("arbitrary"),
        name="sort_tokens",
    )(*args)


def _moe_body(ts_ref, na_ref, xs_ref, wg_ref, wu_ref, wd_ref, ys_ref, xbuf, ybuf, wg16, wu16, wd16, sem_in, sem_out):
    k = pl.program_id(0)
    n_act = na_ref[0]

    def x_copy(g, sl):
        return pltpu.make_async_copy(xs_ref.at[pl.ds(pl.multiple_of(g * MOE_TILE, MOE_TILE), MOE_TILE), :],
                                     xbuf.at[sl], sem_in.at[sl])

    def y_copy(g, sl):
        return pltpu.make_async_copy(ybuf.at[sl],
                                     ys_ref.at[pl.ds(pl.multiple_of(g * MOE_TILE, MOE_TILE), MOE_TILE), :],
                                     sem_out.at[sl])

    @pl.when(k == 0)
    def _():
        for j in range(MOE_AHEAD):
            @pl.when(j < n_act)
            def _():
                x_copy(j, j).start()

    wg16[...] = wg_ref[0].astype(BF16)
    wu16[...] = wu_ref[0].astype(BF16)
    wd16[...] = wd_ref[0].astype(BF16)

    def arrive(g):
        sl = g % MOE_BUFS

        @pl.when(g + MOE_AHEAD < n_act)
        def _():
            x_copy(g + MOE_AHEAD, (g + MOE_AHEAD) % MOE_BUFS).start()

        x_copy(g, sl).wait()

        @pl.when(g >= MOE_BUFS)
        def _():
            y_copy(g - MOE_BUFS, sl).wait()

    def tiles(g, count):
        for j in range(count):
            arrive(g + j)
        acts = []
        for j in range(count):
            x = xbuf[(g + j) % MOE_BUFS]
            acts.append((_silu(_dot(x, wg16[...])) * _dot(x, wu16[...])).astype(BF16))
        for j in range(count):
            ybuf[(g + j) % MOE_BUFS] = _dot(acts[j], wd16[...]).astype(BF16)
        for j in range(count):
            y_copy(g + j, (g + j) % MOE_BUFS).start()

    g0 = ts_ref[k]
    n_here = ts_ref[k + 1] - g0

    def pair(p, carry):
        tiles(g0 + 2 * p, 2)
        return carry

    lax.fori_loop(0, n_here // 2, pair, 0)

    @pl.when(n_here % 2 == 1)
    def _():
        tiles(g0 + n_here - 1, 1)

    @pl.when(k == pl.num_programs(0) - 1)
    def _():
        for j in range(1, MOE_BUFS + 1):
            @pl.when(n_act >= j)
            def _():
                y_copy(n_act - j, (n_act - j) % MOE_BUFS).wait()


def _moe_sorted(tile_start, n_active, xs, wg, wu, wd):
    w_map = lambda k, ts, na: (k, 0, 0)
    return pl.pallas_call(
        _moe_body,
        grid_spec=pltpu.PrefetchScalarGridSpec(
            num_scalar_prefetch=2,
            grid=(N_EXPERTS,),
            in_specs=[pl.BlockSpec(memory_space=pl.ANY),
                      pl.BlockSpec((1, D_MODEL, EXPERT_FF), w_map),
                      pl.BlockSpec((1, D_MODEL, EXPERT_FF), w_map),
                      pl.BlockSpec((1, EXPERT_FF, D_MODEL), w_map)],
            out_specs=pl.BlockSpec(memory_space=pl.ANY),
            scratch_shapes=[pltpu.VMEM((MOE_BUFS, MOE_TILE, D_MODEL), BF16),
                            pltpu.VMEM((MOE_BUFS, MOE_TILE, D_MODEL), BF16),
                            pltpu.VMEM((D_MODEL, EXPERT_FF), BF16), pltpu.VMEM((D_MODEL, EXPERT_FF), BF16),
                            pltpu.VMEM((EXPERT_FF, D_MODEL), BF16),
                            pltpu.SemaphoreType.DMA((MOE_BUFS,)), pltpu.SemaphoreType.DMA((MOE_BUFS,))]),
        out_shape=jax.ShapeDtypeStruct(xs.shape, BF16),
        compiler_params=_cparams("arbitrary"),
        name="moe_sorted",
    )(tile_start, n_active, xs, wg, wu, wd)


def _ple_body(cd_ref, h_ref, dt_ref, wt_ref, p_ref, gn_ref, wg_ref, bg_ref, wp_ref, fn_ref, ys_ref,
              o_ref, loc, sem):
    i = pl.program_id(0)
    tm = h_ref.shape[0]
    rows = loc.shape[1]
    n_chunk = rows // RUN_PAD
    slot = i % 2

    def chunk_copy(step, sl, c):
        return pltpu.make_async_copy(
            ys_ref.at[pl.ds(pl.multiple_of(cd_ref[step, c], RUN_PAD), RUN_PAD), :],
            loc.at[sl, pl.ds(c * RUN_PAD, RUN_PAD), :], sem.at[sl])

    def fetch(step, sl):
        for c in range(n_chunk):
            chunk_copy(step, sl, c).start()

    @pl.when(i == 0)
    def _():
        fetch(0, 0)

    @pl.when(i + 1 < pl.num_programs(0))
    def _():
        fetch(i + 1, 1 - slot)

    for c in range(n_chunk):
        chunk_copy(i, slot, c).wait()

    ci = lax.broadcasted_iota(jnp.int32, (tm, LANES), 1)
    d1 = dt_ref[:, 0:LANES].astype(jnp.int32)
    d2 = dt_ref[:, LANES:2 * LANES].astype(jnp.int32)
    w1 = wt_ref[:, 0:LANES]
    w2 = wt_ref[:, LANES:2 * LANES]
    sel = jnp.concatenate(
        [jnp.where(d1 == ci + m * LANES, w1, jnp.where(d2 == ci + m * LANES, w2, 0.0)) for m in range(rows // LANES)],
        axis=1).astype(BF16)
    h = h_ref[...] + _dot(sel, loc[slot])
    a16 = _rms(h, gn_ref[...]).astype(BF16)
    gate = _sigmoid(_dot(a16, wg_ref[...]) + bg_ref[...])
    pp = _dot(p_ref[...].astype(BF16), wp_ref[...])
    h = h + gate * pp
    o_ref[...] = _rms(h, fn_ref[...])


def _ple(chunk_src, h1, dest_t, gate_t, p, g_ple, wg16, bg, wp16, g_final, ys, tm):
    t = h1.shape[0]
    row = lambda i, cd: (i, 0)
    fixed = lambda i, cd: (0, 0)
    return pl.pallas_call(
        _ple_body,
        grid_spec=pltpu.PrefetchScalarGridSpec(
            num_scalar_prefetch=1,
            grid=(t // tm,),
            in_specs=[pl.BlockSpec((tm, D_MODEL), row), pl.BlockSpec((tm, 2 * LANES), row),
                      pl.BlockSpec((tm, 2 * LANES), row), pl.BlockSpec((tm, PLE_DIM), row),
                      pl.BlockSpec((1, D_MODEL), fixed), pl.BlockSpec((D_MODEL, D_MODEL), fixed),
                      pl.BlockSpec((1, D_MODEL), fixed), pl.BlockSpec((PLE_DIM, D_MODEL), fixed),
                      pl.BlockSpec((1, D_MODEL), fixed), pl.BlockSpec(memory_space=pl.ANY)],
            out_specs=pl.BlockSpec((tm, D_MODEL), row),
            scratch_shapes=[pltpu.VMEM((2, _sorted_rows_per_tile(tm), D_MODEL), BF16),
                            pltpu.SemaphoreType.DMA((2,))]),
        out_shape=jax.ShapeDtypeStruct((t, D_MODEL), F32),
        compiler_params=_cparams("arbitrary"),
        name="ple_final",
    )(chunk_src, h1, dest_t, gate_t, p, g_ple, wg16, bg, wp16, g_final, ys)


MIX_ROWS = 512
SSD_Q = 128
POOL_HDR = 128
CONV_HDR = 8
TAIL_ROWS = 16


def _softplus(x):
    return jnp.maximum(x, 0.0) + jnp.log1p(jnp.exp(-jnp.abs(x)))


def _mixer_constants():
    q = SSD_Q
    li = jnp.arange(q)[:, None]
    tri = (li >= jnp.arange(q)[None, :])
    expand = (jnp.arange(SSD_WIDTH)[None, :] // SSD_HEAD_DIM) == jnp.arange(LANES)[:, None]
    hpg = SSD_HEADS // SSD_GROUPS
    block = (jnp.arange(SSD_BC)[:, None] // SSD_STATE) == (jnp.arange(SSD_WIDTH)[None, :] // (SSD_HEAD_DIM * hpg))
    wj = jnp.arange(2 * q)[None, :]
    band = jnp.stack([(wj <= li + POOL_HDR) & (wj > li + POOL_HDR - w) for w in POOL_WINDOWS])
    return tri.astype(F32), tri.astype(BF16), expand.astype(BF16), block.astype(F32), band.astype(BF16)


def _front_body(x_ref, gm_ref, win_ref, trif_ref, tri_ref, exp_ref, bmask_ref, band_ref,
                cw_ref, cb_ref, dtb_ref, alog_ref, dsk_ref, nw_ref,
                plw_ref, plb_ref, psc_ref, wout_ref, gf_ref, rh_ref, rl_ref, rb_ref,
                h_ref, u_ref, rt_ref, wt_ref, cnt_ref, st_ref, ptail_ref, ctail_ref,
                pool_ext, conv_ext, s_ref, z_ref, dt_ref, mix_ref):
    c = pl.program_id(1)
    rows = MIX_ROWS
    q_len = SSD_Q

    @pl.when(c == 0)
    def _():
        pool_ext[0:POOL_HDR, :] = jnp.zeros((POOL_HDR, POOL_WIDTH), F32)
        conv_ext[0:CONV_HDR, :] = jnp.zeros((CONV_HDR, CONV_DIM), F32)
        s_ref[...] = jnp.zeros_like(s_ref)

    @pl.when(c > 0)
    def _():
        pool_ext[0:POOL_HDR, :] = pool_ext[rows:rows + POOL_HDR, :]
        conv_ext[0:CONV_HDR, :] = conv_ext[rows:rows + CONV_HDR, :]

    causal = trif_ref[...] > 0.5
    tri = tri_ref[...]
    expand = exp_ref[...]
    blockmask = bmask_ref[...]
    lane = lax.broadcasted_iota(jnp.int32, (q_len, LANES), 1)
    left = lane < SSD_HEAD_DIM
    a_neg = -jnp.exp(alog_ref[...])
    rowi = lax.broadcasted_iota(jnp.int32, (q_len, LANES), 0)

    half = rows // 2
    n_cols = 256

    def in_proj_chunks(r0):
        a16 = _rms(x_ref[r0:r0 + half, :], gm_ref[...]).astype(BF16)
        dests = ([(pool_ext, POOL_HDR + r0, k, 0) for k in range(0, POOL_WIDTH, n_cols)]
                 + [(z_ref, r0, k, POOL_WIDTH) for k in range(0, SSD_WIDTH, n_cols)]
                 + [(conv_ext, CONV_HDR + r0, k, POOL_WIDTH + SSD_WIDTH) for k in range(0, CONV_DIM, n_cols)])

        def chunk(ref, row, k, col0):
            def run():
                ref[row:row + half, k:k + n_cols] = _dot(a16, win_ref[:, col0 + k:col0 + k + n_cols])
            return run

        def dt_chunk():
            dt_ref[r0:r0 + half, :] = _dot(a16, win_ref[:, DT_OFF:IN_COLS])

        return [chunk(*d) for d in dests] + [dt_chunk]

    def out_proj_chunks(r0):
        def chunk(k):
            def run():
                h_ref[r0:r0 + half, k:k + n_cols] = (x_ref[r0:r0 + half, k:k + n_cols]
                                                    + _dot(mix_ref[r0:r0 + half, :], wout_ref[:, k:k + n_cols]))
            return run

        def route():
            _route_rows(h_ref[r0:r0 + half, :], gf_ref, rh_ref, rl_ref, rb_ref, u_ref, rt_ref, wt_ref, cnt_ref,
                        r0=r0, accumulate=r0 > 0)

        return [chunk(k) for k in range(0, D_MODEL, n_cols)] + [route]

    def mix_pair(q_first, fillers):
        fillers = list(fillers)

        def fill():
            if fillers:
                fillers.pop(0)()

        subs = range(q_first, q_first + 2)
        xs_l, b16_l, call_l, c16_l = {}, {}, {}, {}
        for q in subs:
            base = CONV_HDR + q * q_len
            conv = cb_ref[...] + conv_ext[base - 3:base - 3 + q_len, :] * cw_ref[0:1, :]
            conv = conv + conv_ext[base - 2:base - 2 + q_len, :] * cw_ref[1:2, :]
            conv = conv + conv_ext[base - 1:base - 1 + q_len, :] * cw_ref[2:3, :]
            conv = conv + conv_ext[base:base + q_len, :] * cw_ref[3:4, :]
            conv = _silu(conv)
            xs_l[q] = conv[:, 0:SSD_WIDTH]
            b16_l[q] = conv[:, SSD_WIDTH:SSD_WIDTH + SSD_BC].astype(BF16)
            call_l[q] = conv[:, SSD_WIDTH + SSD_BC:CONV_DIM]
            c16_l[q] = call_l[q].astype(BF16)
            fill()
        dt_l = {q: _softplus(dt_ref[q * q_len:(q + 1) * q_len, :] + dtb_ref[...]) for q in subs}
        acs_l = {q: _dot_exact_lhs(tri, dt_l[q] * a_neg) for q in subs}
        acst_l = {q: acs_l[q].T for q in subs}
        fill()
        dtx_l = {q: _dot_exact_rhs(dt_l[q], expand, terms=2) for q in subs}
        acsx_l = {q: _dot_exact_rhs(acs_l[q], expand) for q in subs}
        lastx_l = {q: acsx_l[q][q_len - 1:q_len, :] for q in subs}
        xdt_l = {q: xs_l[q] * dtx_l[q] for q in subs}
        xdt16_l = {q: xdt_l[q].astype(BF16) for q in subs}
        fill()
        contrib_l = {q: _dot_tn(b16_l[q], (xdt_l[q] * jnp.exp(lastx_l[q] - acsx_l[q])).astype(BF16)) * blockmask
                     for q in subs}
        cb_l = {q: [_dot_nt(jnp.where(lax.shift_right_logical(lane, 6) == g, call_l[q], 0.0).astype(BF16), b16_l[q])
                    for g in range(SSD_GROUPS)] for q in subs}
        fill()
        zs_l = {}
        for q in subs:
            zs = []
            for h in range(SSD_HEADS):
                blk = slice((h // 2) * LANES, (h // 2 + 1) * LANES)
                seg = acs_l[q][:, h:h + 1] - acst_l[q][h:h + 1, :]
                decay = jnp.where(causal, jnp.exp(seg), 0.0)
                scores = (cb_l[q][h // (SSD_HEADS // SSD_GROUPS)] * decay).astype(BF16)
                zs.append(_dot(scores, xdt16_l[q][:, blk]))
                if h % 4 == 3:
                    fill()
            zs_l[q] = zs
        yoff_l = {}
        for q in subs:
            s_old = s_ref[...]
            yoff_l[q] = _dot(c16_l[q], s_old.astype(BF16)) * jnp.exp(acsx_l[q])
            s_ref[...] = s_old * jnp.exp(lastx_l[q]) + contrib_l[q]
        fill()
        for q in subs:
            r0 = q * q_len
            z = z_ref[r0:r0 + q_len, :]
            y_blocks = []
            for j in range(SSD_HEADS // 2):
                blk = slice(j * LANES, (j + 1) * LANES)
                y = jnp.where(left, zs_l[q][2 * j], zs_l[q][2 * j + 1]) + yoff_l[q][:, blk]
                y = y + dsk_ref[:, blk] * xs_l[q][:, blk]
                y_blocks.append(y * _silu(z[:, blk]))
            for g in range(SSD_GROUPS):
                y0, y1 = y_blocks[2 * g], y_blocks[2 * g + 1]
                ss = jnp.sum(y0 * y0, axis=-1, keepdims=True) + jnp.sum(y1 * y1, axis=-1, keepdims=True)
                rs = lax.rsqrt(ss * (1.0 / (2 * LANES)) + EPS)
                for k, yk in ((2 * g, y0), (2 * g + 1, y1)):
                    blk = slice(k * LANES, (k + 1) * LANES)
                    out = yk * rs * nw_ref[:, blk]
                    mix_ref[r0:r0 + q_len, POOL_WIDTH + k * LANES:POOL_WIDTH + (k + 1) * LANES] = out.astype(BF16)
            fill()
        for q in subs:
            r0 = q * q_len
            pos = c * rows + r0 + rowi
            for g, w in enumerate(POOL_WINDOWS):
                blk = slice(g * POOL_GROUP_DIM, (g + 1) * POOL_GROUP_DIM)
                pe = pool_ext[r0:r0 + 2 * q_len, blk]
                winsum = _dot_exact_lhs(band_ref[g], pe, terms=2)
                cnt = jnp.minimum(pos + 1, w).astype(F32)
                m = winsum / cnt - pe[q_len:2 * q_len, :]
                yg = _dot(m.astype(BF16), plw_ref[g]) + plb_ref[:, blk]
                mix_ref[r0:r0 + q_len, blk] = (yg * psc_ref[:, blk]).astype(BF16)
            fill()
        while fillers:
            fill()

    for run in in_proj_chunks(0):
        run()
    mix_pair(0, in_proj_chunks(half))
    mix_pair(2, out_proj_chunks(0))
    for run in out_proj_chunks(half):
        run()

    @pl.when(c == pl.num_programs(1) - 1)
    def _():
        st_ref[0] = s_ref[...]
        ptail_ref[0] = pool_ext[POOL_HDR + rows - TAIL_ROWS:POOL_HDR + rows, :]
        ctail_ref[0] = conv_ext[rows:rows + CONV_HDR, :]


def _front_prompt(x, g_mix, w_in16, cw, cb, dtb, alog, dsk, nw, plw16, plb, psc, w_out16, g_ffn, r_hi, r_lo, r_bias,
                  nb, seq):
    steps = seq // MIX_ROWS
    nt = nb * steps
    row = lambda b, c: (b * steps + c, 0)
    tile3 = lambda b, c: (b * steps + c, 0, 0)
    per_seq = lambda b, c: (b, 0, 0)
    fixed2 = lambda b, c: (0, 0)
    return pl.pallas_call(
        _front_body,
        grid=(nb, steps),
        in_specs=[pl.BlockSpec((MIX_ROWS, D_MODEL), row), pl.BlockSpec((1, D_MODEL), fixed2),
                  pl.BlockSpec((D_MODEL, IN_COLS), fixed2),
                  pl.BlockSpec((SSD_Q, SSD_Q), fixed2), pl.BlockSpec((SSD_Q, SSD_Q), fixed2),
                  pl.BlockSpec((LANES, SSD_WIDTH), fixed2), pl.BlockSpec((SSD_BC, SSD_WIDTH), fixed2),
                  pl.BlockSpec((len(POOL_WINDOWS), SSD_Q, 2 * SSD_Q), lambda b, c: (0, 0, 0)),
                  pl.BlockSpec((SSD_CONV, CONV_DIM), fixed2), pl.BlockSpec((1, CONV_DIM), fixed2),
                  pl.BlockSpec((1, LANES), fixed2), pl.BlockSpec((1, LANES), fixed2),
                  pl.BlockSpec((1, SSD_WIDTH), fixed2), pl.BlockSpec((1, SSD_WIDTH), fixed2),
                  pl.BlockSpec((len(POOL_WINDOWS), POOL_GROUP_DIM, POOL_GROUP_DIM), lambda b, c: (0, 0, 0)),
                  pl.BlockSpec((1, POOL_WIDTH), fixed2), pl.BlockSpec((1, POOL_WIDTH), fixed2),
                  pl.BlockSpec((D_MODEL, D_MODEL), fixed2), pl.BlockSpec((1, D_MODEL), fixed2),
                  pl.BlockSpec((LANES, D_MODEL), fixed2), pl.BlockSpec((LANES, D_MODEL), fixed2),
                  pl.BlockSpec((LANES, MIX_ROWS), fixed2)],
        out_specs=[pl.BlockSpec((MIX_ROWS, D_MODEL), row), pl.BlockSpec((MIX_ROWS, D_MODEL), row),
                   pl.BlockSpec((1, 8, MIX_ROWS), tile3), pl.BlockSpec((MIX_ROWS, 2 * LANES), row),
                   pl.BlockSpec((1, N_EXPERTS, LANES), tile3),
                   pl.BlockSpec((1, SSD_BC, SSD_WIDTH), per_seq),
                   pl.BlockSpec((1, TAIL_ROWS, POOL_WIDTH), per_seq),
                   pl.BlockSpec((1, CONV_HDR, CONV_DIM), per_seq)],
        out_shape=[jax.ShapeDtypeStruct((nb * seq, D_MODEL), F32), jax.ShapeDtypeStruct((nb * seq, D_MODEL), BF16),
                   jax.ShapeDtypeStruct((nt, 8, MIX_ROWS), F32), jax.ShapeDtypeStruct((nb * seq, 2 * LANES), F32),
                   jax.ShapeDtypeStruct((nt, N_EXPERTS, LANES), F32),
                   jax.ShapeDtypeStruct((nb, SSD_BC, SSD_WIDTH), F32),
                   jax.ShapeDtypeStruct((nb, TAIL_ROWS, POOL_WIDTH), F32),
                   jax.ShapeDtypeStruct((nb, CONV_HDR, CONV_DIM), F32)],
        scratch_shapes=[pltpu.VMEM((POOL_HDR + MIX_ROWS, POOL_WIDTH), F32),
                        pltpu.VMEM((CONV_HDR + MIX_ROWS, CONV_DIM), F32),
                        pltpu.VMEM((SSD_BC, SSD_WIDTH), F32),
                        pltpu.VMEM((MIX_ROWS, SSD_WIDTH), F32), pltpu.VMEM((MIX_ROWS, LANES), F32),
                        pltpu.VMEM((MIX_ROWS, D_MODEL), BF16)],
        compiler_params=_cparams("parallel", "arbitrary"),
        name="front_prompt",
    )(x, g_mix, w_in16, *_mixer_constants(), cw, cb, dtb, alog, dsk, nw, plw16, plb, psc, w_out16, g_ffn,
      r_hi, r_lo, r_bias)


def _mix_step_body(vp_ref, z_ref, xbc_ref, dt_ref, sp_ref, sc_ref, st_ref, cw_ref, cb_ref, dtb_ref, alog_ref,
                   dsk_ref, nw_ref, plw_ref, plb_ref, psc_ref, mix_ref, po_ref, co_ref, so_ref,
                   xdt_t, dec_t, b_t, c_t, xs_keep, y_t):
    h = pl.program_id(0)

    @pl.when(h == 0)
    def _():
        xbc = xbc_ref[...]
        conv = cb_ref[...] + sc_ref[0] * cw_ref[0:1, :]
        conv = conv + sc_ref[1] * cw_ref[1:2, :]
        conv = conv + sc_ref[2] * cw_ref[2:3, :]
        conv = conv + xbc * cw_ref[3:4, :]
        conv = _silu(conv)
        co_ref[0] = sc_ref[1]
        co_ref[1] = sc_ref[2]
        co_ref[2] = xbc
        xs = conv[:, 0:SSD_WIDTH]
        xs_keep[...] = xs
        b_t[...] = conv[:, SSD_WIDTH:SSD_WIDTH + SSD_BC].T
        c_t[...] = conv[:, SSD_WIDTH + SSD_BC:CONV_DIM].T
        dt = _softplus(dt_ref[...] + dtb_ref[...])
        d_a = dt * (-jnp.exp(alog_ref[...]))
        dt_t = dt.T
        dec_t[...] = jnp.exp(d_a).T
        xs_t = xs.T
        for k in range(SSD_HEADS):
            blk = slice(k * SSD_HEAD_DIM, (k + 1) * SSD_HEAD_DIM)
            xdt_t[blk, :] = xs_t[blk, :] * dt_t[k:k + 1, :]
        v = vp_ref[...]
        for k in range(POOL_BUF - 1):
            po_ref[k] = sp_ref[k + 1]
        po_ref[POOL_BUF - 1] = v
        for g, w in enumerate(POOL_WINDOWS):
            blk = slice(g * POOL_GROUP_DIM, (g + 1) * POOL_GROUP_DIM)
            acc = sp_ref[POOL_BUF - (w - 1), :, blk]
            for k in range(w - 2, 0, -1):
                acc = acc + sp_ref[POOL_BUF - k, :, blk]
            acc = acc + v[:, blk]
            m = acc / float(min(PAST_LEN + 1, w)) - v[:, blk]
            yg = _dot(m.astype(BF16), plw_ref[g]) + plb_ref[:, blk]
            mix_ref[:, blk] = (yg * psc_ref[:, blk]).astype(BF16)

    g_off = pl.multiple_of((h // (SSD_HEADS // SSD_GROUPS)) * SSD_STATE, SSD_STATE)
    h_off = pl.multiple_of(h * SSD_HEAD_DIM, SSD_HEAD_DIM)
    b_g = b_t[pl.ds(g_off, SSD_STATE), :]
    c_g = c_t[pl.ds(g_off, SSD_STATE), :]
    dec = dec_t[pl.ds(h, 1), :]
    xdt = xdt_t[pl.ds(h_off, SSD_HEAD_DIM), :]
    y_rows = []
    for p in range(SSD_HEAD_DIM):
        s_new = st_ref[0, p] * dec + xdt[p:p + 1, :] * b_g
        so_ref[0, p] = s_new
        y_rows.append(jnp.sum(s_new * c_g, axis=0, keepdims=True))
    y_t[pl.ds(h_off, SSD_HEAD_DIM), :] = jnp.concatenate(y_rows, axis=0)

    @pl.when(h == pl.num_programs(0) - 1)
    def _():
        xs = xs_keep[...]
        y = y_t[...].T + dsk_ref[...] * xs
        y = y * _silu(z_ref[...])
        width = SSD_WIDTH // SSD_GROUPS
        for g in range(SSD_GROUPS):
            blk = slice(g * width, (g + 1) * width)
            yg = y[:, blk]
            rs = lax.rsqrt(jnp.mean(yg * yg, axis=-1, keepdims=True) + EPS)
            mix_ref[:, POOL_WIDTH + g * width:POOL_WIDTH + (g + 1) * width] = (yg * rs * nw_ref[:, blk]).astype(BF16)


def _mix_step(vp, z, xbc, dt, pool_t, conv_t, ssm_t, cw, cb, dtb, alog, dsk, nw, plw16, plb, psc):
    n = vp.shape[0]
    fixed2 = lambda h: (0, 0)
    fixed3 = lambda h: (0, 0, 0)
    st_spec = pl.BlockSpec((1, SSD_HEAD_DIM, SSD_STATE, n), lambda h: (h, 0, 0, 0))
    return pl.pallas_call(
        _mix_step_body,
        grid=(SSD_HEADS,),
        in_specs=[pl.BlockSpec((n, POOL_WIDTH), fixed2), pl.BlockSpec((n, SSD_WIDTH), fixed2),
                  pl.BlockSpec((n, CONV_DIM), fixed2), pl.BlockSpec((n, LANES), fixed2),
                  pl.BlockSpec((POOL_BUF, n, POOL_WIDTH), fixed3),
                  pl.BlockSpec((SSD_CONV - 1, n, CONV_DIM), fixed3),
                  st_spec,
                  pl.BlockSpec((SSD_CONV, CONV_DIM), fixed2), pl.BlockSpec((1, CONV_DIM), fixed2),
                  pl.BlockSpec((1, LANES), fixed2), pl.BlockSpec((1, LANES), fixed2),
                  pl.BlockSpec((1, SSD_WIDTH), fixed2), pl.BlockSpec((1, SSD_WIDTH), fixed2),
                  pl.BlockSpec((len(POOL_WINDOWS), POOL_GROUP_DIM, POOL_GROUP_DIM), fixed3),
                  pl.BlockSpec((1, POOL_WIDTH), fixed2), pl.BlockSpec((1, POOL_WIDTH), fixed2)],
        out_specs=[pl.BlockSpec((n, D_MODEL), fixed2), pl.BlockSpec((POOL_BUF, n, POOL_WIDTH), fixed3),
                   pl.BlockSpec((SSD_CONV - 1, n, CONV_DIM), fixed3), st_spec],
        out_shape=[jax.ShapeDtypeStruct((n, D_MODEL), BF16), jax.ShapeDtypeStruct(pool_t.shape, F32),
                   jax.ShapeDtypeStruct(conv_t.shape, F32), jax.ShapeDtypeStruct(ssm_t.shape, F32)],
        scratch_shapes=[pltpu.VMEM((SSD_WIDTH, n), F32), pltpu.VMEM((LANES, n), F32),
                        pltpu.VMEM((SSD_BC, n), F32), pltpu.VMEM((SSD_BC, n), F32),
                        pltpu.VMEM((n, SSD_WIDTH), F32), pltpu.VMEM((SSD_WIDTH, n), F32)],
        compiler_params=_cparams("arbitrary"),
        name="mix_step",
    )(vp, z, xbc, dt, pool_t, conv_t, ssm_t, cw, cb, dtb, alog, dsk, nw, plw16, plb, psc)


PROMPT_TILE = 512


def _sort_tables(counts, tile_tokens, data_rows):
    cnt = jnp.concatenate(counts, axis=0)
    pc = (cnt + RUN_PAD - 1) // RUN_PAD * RUN_PAD
    off_local = jnp.cumsum(pc, axis=1) - pc
    tile_rows = jnp.sum(pc, axis=1)
    region = (jnp.sum(pc, axis=0) + MOE_TILE - 1) // MOE_TILE * MOE_TILE
    base = jnp.cumsum(region) - region
    dst = base[None, :] + jnp.cumsum(pc, axis=0) - pc
    per_pass = []
    lo = 0
    for c, tm in zip(counts, tile_tokens):
        hi = lo + c.shape[0]
        n_chunk = _sorted_rows_per_tile(tm) // RUN_PAD
        s = jnp.arange(n_chunk, dtype=jnp.int32) * RUN_PAD
        begins = off_local[lo:hi, None, :]
        ends = begins + pc[lo:hi, None, :]
        inside = (begins <= s[None, :, None]) & (s[None, :, None] < ends)
        shift = jnp.sum(jnp.where(inside, dst[lo:hi, None, :] - begins, 0), axis=2)
        valid = s[None, :] < tile_rows[lo:hi, None]
        parity = (jnp.arange(hi - lo, dtype=jnp.int32) % 2)[:, None]
        spare = data_rows + parity * _sorted_rows_per_tile(max(tile_tokens)) + s[None, :]
        scatter_dst = jnp.where(valid, s[None, :] + shift, spare).astype(jnp.int32)
        gather_src = jnp.where(valid, s[None, :] + shift, 0).astype(jnp.int32)
        off_v = jnp.broadcast_to(off_local[lo:hi, :, None].astype(F32), (hi - lo, N_EXPERTS, LANES))
        per_pass.append((scatter_dst, gather_src, off_v))
        lo = hi
    tiles_cum = jnp.cumsum(region // MOE_TILE)
    n_active = tiles_cum[-1]
    return per_pass, tiles_cum, n_active


def _max_sorted_rows(tile_counts, tile_tokens):
    rows = sum(n * (2 * tm + N_EXPERTS * (RUN_PAD - 1)) for n, tm in zip(tile_counts, tile_tokens))
    data_rows = (-(-rows // MOE_TILE) + N_EXPERTS) * MOE_TILE
    return data_rows, data_rows + 2 * _sorted_rows_per_tile(max(tile_tokens))


def _row(v):
    return v.reshape(1, -1).astype(F32)


def _pad_lanes(v):
    return jnp.pad(v.reshape(1, -1).astype(F32), ((0, 0), (0, LANES - v.size)))


def _state_from_blocks(st):
    n = st.shape[0]
    hpg = SSD_HEADS // SSD_GROUPS
    s6 = st.reshape(n, SSD_GROUPS, SSD_STATE, SSD_GROUPS, hpg, SSD_HEAD_DIM)
    per_group = [s6[:, g, :, g] for g in range(SSD_GROUPS)]
    s = jnp.stack(per_group, axis=1)
    return jnp.transpose(s, (0, 1, 3, 4, 2)).reshape(n, SSD_HEADS, SSD_HEAD_DIM, SSD_STATE)


def kernel(x_prompt, x_sample, p_prompt, p_sample, state_pool, state_conv, state_ssm, norm_mix, w_in, pool_lin_w, pool_lin_b, pool_scale, conv_w, conv_b, dt_bias, a_log, d_skip, ssd_norm, w_out, norm_ffn, router_grp_w, router_grp_b, router_exp_w, router_exp_b, exp_w_gate, exp_w_up, exp_w_down, norm_ple, ple_gate_w, ple_gate_b, ple_proj_w, norm_final):
    nb, seq, _ = x_prompt.shape
    ns = x_sample.shape[0]
    assert ns == LANES and x_sample.shape[1] == 1 and seq % MIX_ROWS == 0 and seq >= POOL_BUF

    w_in16 = jnp.pad(w_in[0], ((0, 0), (0, IN_COLS - w_in.shape[2]))).astype(BF16)
    w_out16 = w_out[0].astype(BF16)
    g_mix, g_ffn, g_ple, g_fin = _row(norm_mix[0]), _row(norm_ffn[0]), _row(norm_ple[0]), _row(norm_final)
    cw, cb = conv_w[0].astype(F32), _row(conv_b[0])
    dtb, alog = _pad_lanes(dt_bias[0]), _pad_lanes(a_log[0])
    dsk = _row(jnp.repeat(d_skip[0], SSD_HEAD_DIM))
    nw = _row(ssd_norm[0])
    plw16 = pool_lin_w[0].astype(BF16)
    plb, psc = _row(pool_lin_b[0]), _row(pool_scale[0])
    zeros4 = jnp.zeros((D_MODEL, 8 - N_EXPERT_GROUPS), F32)
    r_w = jnp.concatenate([router_grp_w[0], zeros4, router_exp_w[0],
                           jnp.zeros((D_MODEL, LANES - 8 - N_EXPERTS), F32)], axis=1).T
    r_hi = r_w.astype(BF16)
    r_lo = (r_w - r_hi.astype(F32)).astype(BF16)
    r_b = jnp.concatenate([router_grp_b[0], jnp.zeros((8 - N_EXPERT_GROUPS,), F32), router_exp_b[0],
                           jnp.zeros((LANES - 8 - N_EXPERTS,), F32)])
    wg = exp_w_gate[0].reshape(N_EXPERTS, D_MODEL, EXPERT_FF)
    wu = exp_w_up[0].reshape(N_EXPERTS, D_MODEL, EXPERT_FF)
    wd = exp_w_down[0].reshape(N_EXPERTS, EXPERT_FF, D_MODEL)
    pg16 = ple_gate_w[0].astype(BF16)
    pgb = _row(ple_gate_b[0])
    pp16 = ple_proj_w[0].astype(BF16)

    def route(x, mix16, tm):
        r_bias = jnp.broadcast_to(r_b[:, None], (LANES, tm))
        return _out_proj(x, mix16, w_out16, g_ffn, r_hi, r_lo, r_bias, tm)

    xp = x_prompt.reshape(nb * seq, D_MODEL)
    h1_p, u_p, rt_p, gate_p, cnt_p, st, pool_tail, conv_tail = _front_prompt(
        xp, g_mix, w_in16, cw, cb, dtb, alog, dsk, nw, plw16, plb, psc, w_out16, g_ffn, r_hi, r_lo,
        jnp.broadcast_to(r_b[:, None], (LANES, MIX_ROWS)), nb, seq)
    pool_p = pool_tail[:, TAIL_ROWS - POOL_BUF:]
    conv_p = conv_tail[:, CONV_HDR - (SSD_CONV - 1):]
    ssm_p = _state_from_blocks(st)

    xs_ = x_sample.reshape(ns, D_MODEL)
    vp_s, z_s, xbc_s, dt_s = _in_proj(xs_, g_mix, w_in16, ns)
    mix_s, pool_t, conv_t, ssm_t = _mix_step(
        vp_s, z_s, xbc_s, dt_s, jnp.transpose(state_pool[0], (1, 0, 2)), jnp.transpose(state_conv[0], (1, 0, 2)),
        jnp.transpose(state_ssm[0], (1, 2, 3, 0)), cw, cb, dtb, alog, dsk, nw, plw16, plb, psc)
    h1_s, u_s, rt_s, gate_s, cnt_s = route(xs_, mix_s, ns)
    pool_s = jnp.transpose(pool_t, (1, 0, 2))
    conv_s = jnp.transpose(conv_t, (1, 0, 2))
    ssm_s = jnp.transpose(ssm_t, (3, 0, 1, 2))

    counts = [cnt_p[:, :, 0].astype(jnp.int32), cnt_s[:, :, 0].astype(jnp.int32)]
    tiles = (PROMPT_TILE, ns)
    data_rows, total_rows = _max_sorted_rows([c.shape[0] for c in counts], tiles)
    (tab_p, tab_s), tiles_cum, n_active = _sort_tables(counts, tiles, data_rows)
    xs_sorted, dest_p = _sort_tokens(tab_p[0], u_p, rt_p, tab_p[2], None, PROMPT_TILE, total_rows)
    xs_sorted, dest_s = _sort_tokens(tab_s[0], u_s, rt_s, tab_s[2], xs_sorted, ns, total_rows)
    tile_start = jnp.concatenate([jnp.zeros((1,), jnp.int32), tiles_cum.astype(jnp.int32)])
    ys_sorted = _moe_sorted(tile_start, n_active.reshape(1).astype(jnp.int32), xs_sorted, wg, wu, wd)

    y_prompt = _ple(tab_p[1], h1_p, dest_p, gate_p, p_prompt[0].reshape(nb * seq, PLE_DIM),
                    g_ple, pg16, pgb, pp16, g_fin, ys_sorted, PROMPT_TILE)
    y_sample = _ple(tab_s[1], h1_s, dest_s, gate_s, p_sample[0].reshape(ns, PLE_DIM),
                    g_ple, pg16, pgb, pp16, g_fin, ys_sorted, ns)

    return (y_prompt.reshape(nb, seq, D_MODEL), y_sample.reshape(ns, 1, D_MODEL),
            pool_p[None], conv_p[None], ssm_p[None], pool_s[None], conv_s[None], ssm_s[None])
```

```python
import functools

import jax
import jax.numpy as jnp
from jax import lax
from jax.experimental import pallas as pl
from jax.experimental.pallas import tpu as pltpu

F32 = jnp.float32
BF16 = jnp.bfloat16

D_MODEL = 1024
POOL_WIDTH = 512
POOL_WINDOWS = (2, 4, 8, 16)
POOL_GROUP_DIM = 128
POOL_BUF = 15
SSD_WIDTH = 512
SSD_HEAD_DIM = 64
SSD_HEADS = 8
SSD_GROUPS = 2
SSD_STATE = 64
SSD_CONV = 4
SSD_BC = SSD_GROUPS * SSD_STATE
CONV_DIM = SSD_WIDTH + 2 * SSD_BC
N_EXPERT_GROUPS = 4
EXPERTS_PER_GROUP = 8
N_EXPERTS = N_EXPERT_GROUPS * EXPERTS_PER_GROUP
EXPERT_FF = 256
PLE_DIM = 256
PAST_LEN = 16384
EPS = 1e-6

LANES = 128
IN_COLS = 1920
DT_OFF = POOL_WIDTH + SSD_WIDTH + CONV_DIM
VMEM_LIMIT = 56 * 1024 * 1024


def _cparams(*sem):
    return pltpu.CompilerParams(dimension_semantics=sem, vmem_limit_bytes=VMEM_LIMIT)


def _rms(x, g):
    return x * lax.rsqrt(jnp.mean(x * x, axis=-1, keepdims=True) + EPS) * g


def _sigmoid(x):
    return 1.0 / (1.0 + jnp.exp(-x))


def _silu(x):
    return x * _sigmoid(x)


def _split3(v):
    hi = v.astype(BF16)
    r = v - hi.astype(F32)
    mid = r.astype(BF16)
    lo = (r - mid.astype(F32)).astype(BF16)
    return hi, mid, lo


def _dot(a, b):
    return jnp.dot(a, b, preferred_element_type=F32)


def _dot_nt(a, b):
    return lax.dot_general(a, b, (((1,), (1,)), ((), ())), preferred_element_type=F32)


def _dot_tn(a, b):
    return lax.dot_general(a, b, (((0,), (0,)), ((), ())), preferred_element_type=F32)


def _dot_exact_lhs(sel, v, terms=3):
    acc = None
    for t in _split3(v)[:terms]:
        p = _dot(sel, t)
        acc = p if acc is None else acc + p
    return acc


def _dot_exact_rhs(v, sel, terms=3):
    acc = None
    for t in _split3(v)[:terms]:
        p = _dot(t, sel)
        acc = p if acc is None else acc + p
    return acc


def _in_proj_body(x_ref, g_ref, w_ref, vp_ref, z_ref, xbc_ref, dt_ref):
    a16 = _rms(x_ref[...], g_ref[...]).astype(BF16)
    vp_ref[...] = _dot(a16, w_ref[:, 0:POOL_WIDTH])
    z_ref[...] = _dot(a16, w_ref[:, POOL_WIDTH:POOL_WIDTH + SSD_WIDTH])
    xbc_ref[...] = _dot(a16, w_ref[:, POOL_WIDTH + SSD_WIDTH:DT_OFF])
    dt_ref[...] = _dot(a16, w_ref[:, DT_OFF:IN_COLS])


def _in_proj(x, g, w16, tm):
    t = x.shape[0]
    row = lambda i: (i, 0)
    fixed = lambda i: (0, 0)
    return pl.pallas_call(
        _in_proj_body,
        grid=(t // tm,),
        in_specs=[pl.BlockSpec((tm, D_MODEL), row), pl.BlockSpec((1, D_MODEL), fixed),
                  pl.BlockSpec((D_MODEL, IN_COLS), fixed)],
        out_specs=[pl.BlockSpec((tm, POOL_WIDTH), row), pl.BlockSpec((tm, SSD_WIDTH), row),
                   pl.BlockSpec((tm, CONV_DIM), row), pl.BlockSpec((tm, LANES), row)],
        out_shape=[jax.ShapeDtypeStruct((t, POOL_WIDTH), F32), jax.ShapeDtypeStruct((t, SSD_WIDTH), F32),
                   jax.ShapeDtypeStruct((t, CONV_DIM), F32), jax.ShapeDtypeStruct((t, LANES), F32)],
        compiler_params=_cparams("parallel"),
        name="in_proj",
    )(x, g, w16)


def _route(lg):
    tm = lg.shape[1]
    gl = lg[0:N_EXPERT_GROUPS, :]
    gmax = jnp.max(gl, axis=0, keepdims=True)
    gsum = jnp.sum(jnp.exp(gl - gmax), axis=0, keepdims=True)
    g_w = 1.0 / gsum
    gi = lax.broadcasted_iota(jnp.int32, gl.shape, 0)
    g_idx = jnp.min(jnp.where(gl == gmax, gi, N_EXPERT_GROUPS), axis=0, keepdims=True)
    sel = jnp.zeros((EXPERTS_PER_GROUP, tm), F32)
    for g in range(N_EXPERT_GROUPS):
        blk = lg[8 + g * EXPERTS_PER_GROUP:8 + (g + 1) * EXPERTS_PER_GROUP, :]
        sel = jnp.where(g_idx == g, blk, sel)
    ei = lax.broadcasted_iota(jnp.int32, sel.shape, 0)
    m1 = jnp.max(sel, axis=0, keepdims=True)
    i1 = jnp.min(jnp.where(sel == m1, ei, EXPERTS_PER_GROUP), axis=0, keepdims=True)
    rest = jnp.where(ei == i1, -jnp.inf, sel)
    m2 = jnp.max(rest, axis=0, keepdims=True)
    i2 = jnp.min(jnp.where(rest == m2, ei, EXPERTS_PER_GROUP), axis=0, keepdims=True)
    p2 = jnp.exp(m2 - m1)
    w1 = g_w / (1.0 + p2)
    w2 = g_w * p2 / (1.0 + p2)
    return g_idx * EXPERTS_PER_GROUP + i1, g_idx * EXPERTS_PER_GROUP + i2, w1, w2


def _out_proj_body(x_ref, mix_ref, w_ref, g_ref, rh_ref, rl_ref, rb_ref, h_ref, u_ref, rt_ref, wt_ref, cnt_ref):
    _out_proj_and_route(x_ref[...], mix_ref[...], w_ref, g_ref, rh_ref, rl_ref, rb_ref,
                        h_ref, u_ref, rt_ref, wt_ref, cnt_ref)


def _out_proj_and_route(x, mix16, w_ref, g_ref, rh_ref, rl_ref, rb_ref, h_ref, u_ref, rt_ref, wt_ref, cnt_ref,
                        r0=0, accumulate=False):
    tm = x.shape[0]
    h = x + _dot(mix16, w_ref[...])
    h_ref[r0:r0 + tm, :] = h
    _route_rows(h, g_ref, rh_ref, rl_ref, rb_ref, u_ref, rt_ref, wt_ref, cnt_ref, r0=r0, accumulate=accumulate)


def _route_rows(h, g_ref, rh_ref, rl_ref, rb_ref, u_ref, rt_ref, wt_ref, cnt_ref, r0=0, accumulate=False):
    tm = h.shape[0]
    u = _rms(h, g_ref[...])
    u_hi = u.astype(BF16)
    u_ref[r0:r0 + tm, :] = u_hi
    u_lo = (u - u_hi.astype(F32)).astype(BF16)
    lg = (_dot_nt(rh_ref[...], u_hi) + _dot_nt(rh_ref[...], u_lo) + _dot_nt(rl_ref[...], u_hi)
          + rb_ref[:, r0:r0 + tm])
    b1, b2, w1, w2 = _route(lg)
    r8 = lax.broadcasted_iota(jnp.int32, (8, tm), 0)
    rt_ref[0, :, r0:r0 + tm] = jnp.where(r8 == 0, b1.astype(F32), jnp.where(r8 == 1, b2.astype(F32), 0.0))
    wt_ref[r0:r0 + tm, :] = jnp.concatenate(
        [jnp.broadcast_to(w1, (LANES, tm)).T, jnp.broadcast_to(w2, (LANES, tm)).T], axis=1)
    kio = lax.broadcasted_iota(jnp.int32, (N_EXPERTS, tm), 0)
    hits = ((kio == b1) | (kio == b2)).astype(F32)
    cnt = jnp.broadcast_to(jnp.sum(hits, axis=1, keepdims=True), (N_EXPERTS, LANES))
    cnt_ref[0] = cnt_ref[0] + cnt if accumulate else cnt


def _out_proj(x, mix16, w16, g, r_hi, r_lo, r_bias, tm):
    t = x.shape[0]
    nt = t // tm
    row = lambda i: (i, 0)
    fixed = lambda i: (0, 0)
    tile3 = lambda i: (i, 0, 0)
    return pl.pallas_call(
        _out_proj_body,
        grid=(nt,),
        in_specs=[pl.BlockSpec((tm, D_MODEL), row), pl.BlockSpec((tm, D_MODEL), row),
                  pl.BlockSpec((D_MODEL, D_MODEL), fixed), pl.BlockSpec((1, D_MODEL), fixed),
                  pl.BlockSpec((LANES, D_MODEL), fixed), pl.BlockSpec((LANES, D_MODEL), fixed),
                  pl.BlockSpec((LANES, tm), fixed)],
        out_specs=[pl.BlockSpec((tm, D_MODEL), row), pl.BlockSpec((tm, D_MODEL), row),
                   pl.BlockSpec((1, 8, tm), tile3), pl.BlockSpec((tm, 2 * LANES), row),
                   pl.BlockSpec((1, N_EXPERTS, LANES), tile3)],
        out_shape=[jax.ShapeDtypeStruct((t, D_MODEL), F32), jax.ShapeDtypeStruct((t, D_MODEL), BF16),
                   jax.ShapeDtypeStruct((nt, 8, tm), F32), jax.ShapeDtypeStruct((t, 2 * LANES), F32),
                   jax.ShapeDtypeStruct((nt, N_EXPERTS, LANES), F32)],
        compiler_params=_cparams("parallel"),
        name="out_proj_router",
    )(x, mix16, w16, g, r_hi, r_lo, r_bias)


RUN_PAD = 16
MOE_TILE = 256
MOE_BUFS = 6
MOE_AHEAD = 4


def _tile_lanes(v, width):
    reps = width // LANES
    return v if reps == 1 else jnp.concatenate([v] * reps, axis=1)


def _sorted_rows_per_tile(tm):
    need = 2 * tm + N_EXPERTS * (RUN_PAD - 1)
    return -(-need // LANES) * LANES


def _sort_body(cd_ref, u_ref, rt_ref, off_ref, *rest):
    xs_ref, dt_ref, loc, sem = rest[-4:]
    i = pl.program_id(0)
    tm = u_ref.shape[0]
    rows = loc.shape[1]
    rt = rt_ref[0]
    b1 = rt[0:1, :].astype(jnp.int32)
    b2 = rt[1:2, :].astype(jnp.int32)
    kio = lax.broadcasted_iota(jnp.int32, (N_EXPERTS, tm), 0)
    o1 = kio == b1
    o2 = kio == b2
    before = (lax.broadcasted_iota(jnp.int32, (tm, tm), 0) < lax.broadcasted_iota(jnp.int32, (tm, tm), 1))
    start = _dot((o1 | o2).astype(BF16), before.astype(BF16)) + _tile_lanes(off_ref[0], tm)
    d1 = jnp.sum(jnp.where(o1, start, 0.0), axis=0, keepdims=True)
    d2 = jnp.sum(jnp.where(o2, start, 0.0), axis=0, keepdims=True)
    dt_ref[...] = jnp.concatenate([jnp.broadcast_to(d1, (LANES, tm)).T, jnp.broadcast_to(d2, (LANES, tm)).T], axis=1)
    rio = lax.broadcasted_iota(jnp.int32, (rows, tm), 0)
    perm = ((rio == d1.astype(jnp.int32)) | (rio == d2.astype(jnp.int32))).astype(BF16)
    slot = i % 2
    loc[slot] = _dot(perm, u_ref[...]).astype(BF16)

    def chunk_copy(step, sl, c):
        return pltpu.make_async_copy(
            loc.at[sl, pl.ds(c * RUN_PAD, RUN_PAD), :],
            xs_ref.at[pl.ds(pl.multiple_of(cd_ref[step, c], RUN_PAD), RUN_PAD), :], sem.at[sl])

    n_chunk = rows // RUN_PAD
    for c in range(n_chunk):
        chunk_copy(i, slot, c).start()

    @pl.when(i > 0)
    def _():
        for c in range(n_chunk):
            chunk_copy(i - 1, 1 - slot, c).wait()

    @pl.when(i == pl.num_programs(0) - 1)
    def _():
        for c in range(n_chunk):
            chunk_copy(i, slot, c).wait()


def _sort_tokens(chunk_dst, u16, rt, off_v, xs_prev, tm, total_rows):
    t = u16.shape[0]
    rows = _sorted_rows_per_tile(tm)
    in_specs = [pl.BlockSpec((tm, D_MODEL), lambda i, cd: (i, 0)),
                pl.BlockSpec((1, 8, tm), lambda i, cd: (i, 0, 0)),
                pl.BlockSpec((1, N_EXPERTS, LANES), lambda i, cd: (i, 0, 0))]
    args = [chunk_dst, u16, rt, off_v]
    aliases = {}
    if xs_prev is not None:
        in_specs.append(pl.BlockSpec(memory_space=pl.ANY))
        aliases = {len(args): 0}
        args.append(xs_prev)
    return pl.pallas_call(
        _sort_body,
        grid_spec=pltpu.PrefetchScalarGridSpec(
            num_scalar_prefetch=1,
            grid=(t // tm,),
            in_specs=in_specs,
            out_specs=[pl.BlockSpec(memory_space=pl.ANY),
                       pl.BlockSpec((tm, 2 * LANES), lambda i, cd: (i, 0))],
            scratch_shapes=[pltpu.VMEM((2, rows, D_MODEL), BF16), pltpu.SemaphoreType.DMA((2,))]),
        out_shape=[jax.ShapeDtypeStruct((total_rows, D_MODEL), BF16),
                   jax.ShapeDtypeStruct((t, 2 * LANES), F32)],
        input_output_aliases=aliases,
        compiler_params=_cparams("arbitrary"),
        name="sort_tokens",
    )(*args)


def _moe_body(ts_ref, na_ref, xs_ref, wg_ref, wu_ref, wd_ref, ys_ref, xbuf, ybuf, wg16, wu16, wd16, sem_in, sem_out):
    k = pl.program_id(0)
    n_act = na_ref[0]

    def x_copy(g, sl):
        return pltpu.make_async_copy(xs_ref.at[pl.ds(pl.multiple_of(g * MOE_TILE, MOE_TILE), MOE_TILE), :],
                                     xbuf.at[sl], sem_in.at[sl])

    def y_copy(g, sl):
        return pltpu.make_async_copy(ybuf.at[sl],
                                     ys_ref.at[pl.ds(pl.multiple_of(g * MOE_TILE, MOE_TILE), MOE_TILE), :],
                                     sem_out.at[sl])

    @pl.when(k == 0)
    def _():
        for j in range(MOE_AHEAD):
            @pl.when(j < n_act)
            def _():
                x_copy(j, j).start()

    wg16[...] = wg_ref[0].astype(BF16)
    wu16[...] = wu_ref[0].astype(BF16)
    wd16[...] = wd_ref[0].astype(BF16)

    def arrive(g):
        sl = g % MOE_BUFS

        @pl.when(g + MOE_AHEAD < n_act)
        def _():
            x_copy(g + MOE_AHEAD, (g + MOE_AHEAD) % MOE_BUFS).start()

        x_copy(g, sl).wait()

        @pl.when(g >= MOE_BUFS)
        def _():
            y_copy(g - MOE_BUFS, sl).wait()

    def tiles(g, count):
        for j in range(count):
            arrive(g + j)
        acts = []
        for j in range(count):
            x = xbuf[(g + j) % MOE_BUFS]
            acts.append((_silu(_dot(x, wg16[...])) * _dot(x, wu16[...])).astype(BF16))
        for j in range(count):
            ybuf[(g + j) % MOE_BUFS] = _dot(acts[j], wd16[...]).astype(BF16)
        for j in range(count):
            y_copy(g + j, (g + j) % MOE_BUFS).start()

    g0 = ts_ref[k]
    n_here = ts_ref[k + 1] - g0

    def pair(p, carry):
        tiles(g0 + 2 * p, 2)
        return carry

    lax.fori_loop(0, n_here // 2, pair, 0)

    @pl.when(n_here % 2 == 1)
    def _():
        tiles(g0 + n_here - 1, 1)

    @pl.when(k == pl.num_programs(0) - 1)
    def _():
        for j in range(1, MOE_BUFS + 1):
            @pl.when(n_act >= j)
            def _():
                y_copy(n_act - j, (n_act - j) % MOE_BUFS).wait()


def _moe_sorted(tile_start, n_active, xs, wg, wu, wd):
    w_map = lambda k, ts, na: (k, 0, 0)
    return pl.pallas_call(
        _moe_body,
        grid_spec=pltpu.PrefetchScalarGridSpec(
            num_scalar_prefetch=2,
            grid=(N_EXPERTS,),
            in_specs=[pl.BlockSpec(memory_space=pl.ANY),
                      pl.BlockSpec((1, D_MODEL, EXPERT_FF), w_map),
                      pl.BlockSpec((1, D_MODEL, EXPERT_FF), w_map),
                      pl.BlockSpec((1, EXPERT_FF, D_MODEL), w_map)],
            out_specs=pl.BlockSpec(memory_space=pl.ANY),
            scratch_shapes=[pltpu.VMEM((MOE_BUFS, MOE_TILE, D_MODEL), BF16),
                            pltpu.VMEM((MOE_BUFS, MOE_TILE, D_MODEL), BF16),
                            pltpu.VMEM((D_MODEL, EXPERT_FF), BF16), pltpu.VMEM((D_MODEL, EXPERT_FF), BF16),
                            pltpu.VMEM((EXPERT_FF, D_MODEL), BF16),
                            pltpu.SemaphoreType.DMA((MOE_BUFS,)), pltpu.SemaphoreType.DMA((MOE_BUFS,))]),
        out_shape=jax.ShapeDtypeStruct(xs.shape, BF16),
        compiler_params=_cparams("arbitrary"),
        name="moe_sorted",
    )(tile_start, n_active, xs, wg, wu, wd)


def _ple_body(cd_ref, h_ref, dt_ref, wt_ref, p_ref, gn_ref, wg_ref, bg_ref, wp_ref, fn_ref, ys_ref,
              o_ref, loc, sem):
    i = pl.program_id(0)
    tm = h_ref.shape[0]
    rows = loc.shape[1]
    n_chunk = rows // RUN_PAD
    slot = i % 2

    def chunk_copy(step, sl, c):
        return pltpu.make_async_copy(
            ys_ref.at[pl.ds(pl.multiple_of(cd_ref[step, c], RUN_PAD), RUN_PAD), :],
            loc.at[sl, pl.ds(c * RUN_PAD, RUN_PAD), :], sem.at[sl])

    def fetch(step, sl):
        for c in range(n_chunk):
            chunk_copy(step, sl, c).start()

    @pl.when(i == 0)
    def _():
        fetch(0, 0)

    @pl.when(i + 1 < pl.num_programs(0))
    def _():
        fetch(i + 1, 1 - slot)

    for c in range(n_chunk):
        chunk_copy(i, slot, c).wait()

    ci = lax.broadcasted_iota(jnp.int32, (tm, LANES), 1)
    d1 = dt_ref[:, 0:LANES].astype(jnp.int32)
    d2 = dt_ref[:, LANES:2 * LANES].astype(jnp.int32)
    w1 = wt_ref[:, 0:LANES]
    w2 = wt_ref[:, LANES:2 * LANES]
    sel = jnp.concatenate(
        [jnp.where(d1 == ci + m * LANES, w1, jnp.where(d2 == ci + m * LANES, w2, 0.0)) for m in range(rows // LANES)],
        axis=1).astype(BF16)
    h = h_ref[...] + _dot(sel, loc[slot])
    a16 = _rms(h, gn_ref[...]).astype(BF16)
    gate = _sigmoid(_dot(a16, wg_ref[...]) + bg_ref[...])
    pp = _dot(p_ref[...].astype(BF16), wp_ref[...])
    h = h + gate * pp
    o_ref[...] = _rms(h, fn_ref[...])


def _ple(chunk_src, h1, dest_t, gate_t, p, g_ple, wg16, bg, wp16, g_final, ys, tm):
    t = h1.shape[0]
    row = lambda i, cd: (i, 0)
    fixed = lambda i, cd: (0, 0)
    return pl.pallas_call(
        _ple_body,
        grid_spec=pltpu.PrefetchScalarGridSpec(
            num_scalar_prefetch=1,
            grid=(t // tm,),
            in_specs=[pl.BlockSpec((tm, D_MODEL), row), pl.BlockSpec((tm, 2 * LANES), row),
                      pl.BlockSpec((tm, 2 * LANES), row), pl.BlockSpec((tm, PLE_DIM), row),
                      pl.BlockSpec((1, D_MODEL), fixed), pl.BlockSpec((D_MODEL, D_MODEL), fixed),
                      pl.BlockSpec((1, D_MODEL), fixed), pl.BlockSpec((PLE_DIM, D_MODEL), fixed),
                      pl.BlockSpec((1, D_MODEL), fixed), pl.BlockSpec(memory_space=pl.ANY)],
            out_specs=pl.BlockSpec((tm, D_MODEL), row),
            scratch_shapes=[pltpu.VMEM((2, _sorted_rows_per_tile(tm), D_MODEL), BF16),
                            pltpu.SemaphoreType.DMA((2,))]),
        out_shape=jax.ShapeDtypeStruct((t, D_MODEL), F32),
        compiler_params=_cparams("arbitrary"),
        name="ple_final",
    )(chunk_src, h1, dest_t, gate_t, p, g_ple, wg16, bg, wp16, g_final, ys)


MIX_ROWS = 512
SSD_Q = 128
POOL_HDR = 128
CONV_HDR = 8
TAIL_ROWS = 16


def _softplus(x):
    return jnp.maximum(x, 0.0) + jnp.log1p(jnp.exp(-jnp.abs(x)))


def _mixer_constants():
    q = SSD_Q
    li = jnp.arange(q)[:, None]
    tri = (li >= jnp.arange(q)[None, :])
    expand = (jnp.arange(SSD_WIDTH)[None, :] // SSD_HEAD_DIM) == jnp.arange(LANES)[:, None]
    hpg = SSD_HEADS // SSD_GROUPS
    block = (jnp.arange(SSD_BC)[:, None] // SSD_STATE) == (jnp.arange(SSD_WIDTH)[None, :] // (SSD_HEAD_DIM * hpg))
    wj = jnp.arange(2 * q)[None, :]
    band = jnp.stack([(wj <= li + POOL_HDR) & (wj > li + POOL_HDR - w) for w in POOL_WINDOWS])
    return tri.astype(F32), tri.astype(BF16), expand.astype(BF16), block.astype(F32), band.astype(BF16)


def _front_body(steps_per_seq, x_ref, xp_ref, gm_ref, win_ref, trif_ref, tri_ref, exp_ref, bmask_ref, band_ref,
                cw_ref, cb_ref, dtb_ref, alog_ref, dsk_ref, nw_ref,
                plw_ref, plb_ref, psc_ref, wout_ref, gf_ref, rh_ref, rl_ref, rb_ref,
                h_ref, u_ref, rt_ref, wt_ref, cnt_ref, st_ref, ptail_ref, ctail_ref,
                pool_ext, conv_ext, s_ref, z_ref, dt_ref, mix_ref, mixp_ref):
    s = pl.program_id(0)
    n_tiles = pl.num_programs(0) - 1
    real = s < n_tiles
    c = jnp.minimum(s, n_tiles - 1) % steps_per_seq
    rows = MIX_ROWS
    q_len = SSD_Q

    @pl.when(s == 0)
    def _():
        mixp_ref[...] = jnp.zeros_like(mixp_ref)

    @pl.when(c == 0)
    def _():
        pool_ext[0:POOL_HDR, :] = jnp.zeros((POOL_HDR, POOL_WIDTH), F32)
        conv_ext[0:CONV_HDR, :] = jnp.zeros((CONV_HDR, CONV_DIM), F32)
        s_ref[...] = jnp.zeros_like(s_ref)

    @pl.when(c > 0)
    def _():
        pool_ext[0:POOL_HDR, :] = pool_ext[rows:rows + POOL_HDR, :]
        conv_ext[0:CONV_HDR, :] = conv_ext[rows:rows + CONV_HDR, :]

    causal = trif_ref[...] > 0.5
    tri = tri_ref[...]
    expand = exp_ref[...]
    blockmask = bmask_ref[...]
    lane = lax.broadcasted_iota(jnp.int32, (q_len, LANES), 1)
    left = lane < SSD_HEAD_DIM
    a_neg = -jnp.exp(alog_ref[...])
    rowi = lax.broadcasted_iota(jnp.int32, (q_len, LANES), 0)

    half = rows // 2
    n_cols = 256

    def in_proj_chunks(r0):
        a16 = _rms(x_ref[r0:r0 + half, :], gm_ref[...]).astype(BF16)
        dests = ([(pool_ext, POOL_HDR + r0, k, 0) for k in range(0, POOL_WIDTH, n_cols)]
                 + [(z_ref, r0, k, POOL_WIDTH) for k in range(0, SSD_WIDTH, n_cols)]
                 + [(conv_ext, CONV_HDR + r0, k, POOL_WIDTH + SSD_WIDTH) for k in range(0, CONV_DIM, n_cols)])

        def chunk(ref, row, k, col0):
            def run():
                ref[row:row + half, k:k + n_cols] = _dot(a16, win_ref[:, col0 + k:col0 + k + n_cols])
            return run

        def dt_chunk():
            dt_ref[r0:r0 + half, :] = _dot(a16, win_ref[:, DT_OFF:IN_COLS])

        return [chunk(*d) for d in dests] + [dt_chunk]

    def out_proj_chunks(r0):
        def chunk(k):
            def run():
                h_ref[r0:r0 + half, k:k + n_cols] = (xp_ref[r0:r0 + half, k:k + n_cols]
                                                    + _dot(mixp_ref[r0:r0 + half, :], wout_ref[:, k:k + n_cols]))
            return run

        def route():
            _route_rows(h_ref[r0:r0 + half, :], gf_ref, rh_ref, rl_ref, rb_ref, u_ref, rt_ref, wt_ref, cnt_ref,
                        r0=r0, accumulate=r0 > 0)

        return [chunk(k) for k in range(0, D_MODEL, n_cols)] + [route]

    def mix_pair(q_first, fillers):
        fillers = list(fillers)

        def fill():
            if fillers:
                fillers.pop(0)()

        subs = range(q_first, q_first + 2)
        xs_l, b16_l, call_l, c16_l = {}, {}, {}, {}
        for q in subs:
            base = CONV_HDR + q * q_len
            conv = cb_ref[...] + conv_ext[base - 3:base - 3 + q_len, :] * cw_ref[0:1, :]
            conv = conv + conv_ext[base - 2:base - 2 + q_len, :] * cw_ref[1:2, :]
            conv = conv + conv_ext[base - 1:base - 1 + q_len, :] * cw_ref[2:3, :]
            conv = conv + conv_ext[base:base + q_len, :] * cw_ref[3:4, :]
            conv = _silu(conv)
            xs_l[q] = conv[:, 0:SSD_WIDTH]
            b16_l[q] = conv[:, SSD_WIDTH:SSD_WIDTH + SSD_BC].astype(BF16)
            call_l[q] = conv[:, SSD_WIDTH + SSD_BC:CONV_DIM]
            c16_l[q] = call_l[q].astype(BF16)
            fill()
        dt_l = {q: _softplus(dt_ref[q * q_len:(q + 1) * q_len, :] + dtb_ref[...]) for q in subs}
        acs_l = {q: _dot_exact_lhs(tri, dt_l[q] * a_neg) for q in subs}
        acst_l = {q: acs_l[q].T for q in subs}
        fill()
        dtx_l = {q: _dot_exact_rhs(dt_l[q], expand, terms=2) for q in subs}
        acsx_l = {q: _dot_exact_rhs(acs_l[q], expand) for q in subs}
        lastx_l = {q: acsx_l[q][q_len - 1:q_len, :] for q in subs}
        xdt_l = {q: xs_l[q] * dtx_l[q] for q in subs}
        xdt16_l = {q: xdt_l[q].astype(BF16) for q in subs}
        fill()
        contrib_l = {q: _dot_tn(b16_l[q], (xdt_l[q] * jnp.exp(lastx_l[q] - acsx_l[q])).astype(BF16)) * blockmask
                     for q in subs}
        cb_l = {q: [_dot_nt(jnp.where(lax.shift_right_logical(lane, 6) == g, call_l[q], 0.0).astype(BF16), b16_l[q])
                    for g in range(SSD_GROUPS)] for q in subs}
        fill()
        zs_l = {}
        for q in subs:
            zs = []
            for h in range(SSD_HEADS):
                blk = slice((h // 2) * LANES, (h // 2 + 1) * LANES)
                seg = acs_l[q][:, h:h + 1] - acst_l[q][h:h + 1, :]
                decay = jnp.where(causal, jnp.exp(seg), 0.0)
                scores = (cb_l[q][h // (SSD_HEADS // SSD_GROUPS)] * decay).astype(BF16)
                zs.append(_dot(scores, xdt16_l[q][:, blk]))
                if h % 4 == 3:
                    fill()
            zs_l[q] = zs
        yoff_l = {}
        for q in subs:
            s_old = s_ref[...]
            yoff_l[q] = _dot(c16_l[q], s_old.astype(BF16)) * jnp.exp(acsx_l[q])
            s_ref[...] = s_old * jnp.exp(lastx_l[q]) + contrib_l[q]
        fill()
        for q in subs:
            r0 = q * q_len
            z = z_ref[r0:r0 + q_len, :]
            y_blocks = []
            for j in range(SSD_HEADS // 2):
                blk = slice(j * LANES, (j + 1) * LANES)
                y = jnp.where(left, zs_l[q][2 * j], zs_l[q][2 * j + 1]) + yoff_l[q][:, blk]
                y = y + dsk_ref[:, blk] * xs_l[q][:, blk]
                y_blocks.append(y * _silu(z[:, blk]))
            for g in range(SSD_GROUPS):
                y0, y1 = y_blocks[2 * g], y_blocks[2 * g + 1]
                ss = jnp.sum(y0 * y0, axis=-1, keepdims=True) + jnp.sum(y1 * y1, axis=-1, keepdims=True)
                rs = lax.rsqrt(ss * (1.0 / (2 * LANES)) + EPS)
                for k, yk in ((2 * g, y0), (2 * g + 1, y1)):
                    blk = slice(k * LANES, (k + 1) * LANES)
                    out = yk * rs * nw_ref[:, blk]
                    mix_ref[r0:r0 + q_len, POOL_WIDTH + k * LANES:POOL_WIDTH + (k + 1) * LANES] = out.astype(BF16)
            fill()
        for q in subs:
            r0 = q * q_len
            pos = c * rows + r0 + rowi
            for g, w in enumerate(POOL_WINDOWS):
                blk = slice(g * POOL_GROUP_DIM, (g + 1) * POOL_GROUP_DIM)
                pe = pool_ext[r0:r0 + 2 * q_len, blk]
                winsum = _dot_exact_lhs(band_ref[g], pe, terms=2)
                cnt = jnp.minimum(pos + 1, w).astype(F32)
                m = winsum / cnt - pe[q_len:2 * q_len, :]
                yg = _dot(m.astype(BF16), plw_ref[g]) + plb_ref[:, blk]
                mix_ref[r0:r0 + q_len, blk] = (yg * psc_ref[:, blk]).astype(BF16)
            fill()
        while fillers:
            fill()

    for run in in_proj_chunks(0):
        run()
    mix_pair(0, in_proj_chunks(half) + out_proj_chunks(0))
    mix_pair(2, out_proj_chunks(half))
    mixp_ref[...] = mix_ref[...]

    @pl.when(real & (c == steps_per_seq - 1))
    def _():
        st_ref[0] = s_ref[...]
        ptail_ref[0] = pool_ext[POOL_HDR + rows - TAIL_ROWS:POOL_HDR + rows, :]
        ctail_ref[0] = conv_ext[rows:rows + CONV_HDR, :]


def _front_prompt(x, g_mix, w_in16, cw, cb, dtb, alog, dsk, nw, plw16, plb, psc, w_out16, g_ffn, r_hi, r_lo, r_bias,
                  nb, seq):
    steps = seq // MIX_ROWS
    nt = nb * steps
    cur = lambda s: (jnp.minimum(s, nt - 1), 0)
    prev = lambda s: (jnp.maximum(s - 1, 0), 0)
    prev3 = lambda s: (jnp.maximum(s - 1, 0), 0, 0)
    per_seq = lambda s: (jnp.minimum(s, nt - 1) // steps, 0, 0)
    fixed2 = lambda s: (0, 0)
    fixed3 = lambda s: (0, 0, 0)
    return pl.pallas_call(
        functools.partial(_front_body, steps),
        grid=(nt + 1,),
        in_specs=[pl.BlockSpec((MIX_ROWS, D_MODEL), cur), pl.BlockSpec((MIX_ROWS, D_MODEL), prev),
                  pl.BlockSpec((1, D_MODEL), fixed2), pl.BlockSpec((D_MODEL, IN_COLS), fixed2),
                  pl.BlockSpec((SSD_Q, SSD_Q), fixed2), pl.BlockSpec((SSD_Q, SSD_Q), fixed2),
                  pl.BlockSpec((LANES, SSD_WIDTH), fixed2), pl.BlockSpec((SSD_BC, SSD_WIDTH), fixed2),
                  pl.BlockSpec((len(POOL_WINDOWS), SSD_Q, 2 * SSD_Q), fixed3),
                  pl.BlockSpec((SSD_CONV, CONV_DIM), fixed2), pl.BlockSpec((1, CONV_DIM), fixed2),
                  pl.BlockSpec((1, LANES), fixed2), pl.BlockSpec((1, LANES), fixed2),
                  pl.BlockSpec((1, SSD_WIDTH), fixed2), pl.BlockSpec((1, SSD_WIDTH), fixed2),
                  pl.BlockSpec((len(POOL_WINDOWS), POOL_GROUP_DIM, POOL_GROUP_DIM), fixed3),
                  pl.BlockSpec((1, POOL_WIDTH), fixed2), pl.BlockSpec((1, POOL_WIDTH), fixed2),
                  pl.BlockSpec((D_MODEL, D_MODEL), fixed2), pl.BlockSpec((1, D_MODEL), fixed2),
                  pl.BlockSpec((LANES, D_MODEL), fixed2), pl.BlockSpec((LANES, D_MODEL), fixed2),
                  pl.BlockSpec((LANES, MIX_ROWS), fixed2)],
        out_specs=[pl.BlockSpec((MIX_ROWS, D_MODEL), prev), pl.BlockSpec((MIX_ROWS, D_MODEL), prev),
                   pl.BlockSpec((1, 8, MIX_ROWS), prev3), pl.BlockSpec((MIX_ROWS, 2 * LANES), prev),
                   pl.BlockSpec((1, N_EXPERTS, LANES), prev3),
                   pl.BlockSpec((1, SSD_BC, SSD_WIDTH), per_seq),
                   pl.BlockSpec((1, TAIL_ROWS, POOL_WIDTH), per_seq),
                   pl.BlockSpec((1, CONV_HDR, CONV_DIM), per_seq)],
        out_shape=[jax.ShapeDtypeStruct((nb * seq, D_MODEL), F32), jax.ShapeDtypeStruct((nb * seq, D_MODEL), BF16),
                   jax.ShapeDtypeStruct((nt, 8, MIX_ROWS), F32), jax.ShapeDtypeStruct((nb * seq, 2 * LANES), F32),
                   jax.ShapeDtypeStruct((nt, N_EXPERTS, LANES), F32),
                   jax.ShapeDtypeStruct((nb, SSD_BC, SSD_WIDTH), F32),
                   jax.ShapeDtypeStruct((nb, TAIL_ROWS, POOL_WIDTH), F32),
                   jax.ShapeDtypeStruct((nb, CONV_HDR, CONV_DIM), F32)],
        scratch_shapes=[pltpu.VMEM((POOL_HDR + MIX_ROWS, POOL_WIDTH), F32),
                        pltpu.VMEM((CONV_HDR + MIX_ROWS, CONV_DIM), F32),
                        pltpu.VMEM((SSD_BC, SSD_WIDTH), F32),
                        pltpu.VMEM((MIX_ROWS, SSD_WIDTH), F32), pltpu.VMEM((MIX_ROWS, LANES), F32),
                        pltpu.VMEM((MIX_ROWS, D_MODEL), BF16), pltpu.VMEM((MIX_ROWS, D_MODEL), BF16)],
        compiler_params=_cparams("arbitrary"),
        name="front_prompt",
    )(x, x, g_mix, w_in16, *_mixer_constants(), cw, cb, dtb, alog, dsk, nw, plw16, plb, psc, w_out16, g_ffn,
      r_hi, r_lo, r_bias)


def _mix_step_body(vp_ref, z_ref, xbc_ref, dt_ref, sp_ref, sc_ref, st_ref, cw_ref, cb_ref, dtb_ref, alog_ref,
                   dsk_ref, nw_ref, plw_ref, plb_ref, psc_ref, mix_ref, po_ref, co_ref, so_ref,
                   xdt_t, dec_t, b_t, c_t, xs_keep, y_t):
    h = pl.program_id(0)

    @pl.when(h == 0)
    def _():
        xbc = xbc_ref[...]
        conv = cb_ref[...] + sc_ref[0] * cw_ref[0:1, :]
        conv = conv + sc_ref[1] * cw_ref[1:2, :]
        conv = conv + sc_ref[2] * cw_ref[2:3, :]
        conv = conv + xbc * cw_ref[3:4, :]
        conv = _silu(conv)
        co_ref[0] = sc_ref[1]
        co_ref[1] = sc_ref[2]
        co_ref[2] = xbc
        xs = conv[:, 0:SSD_WIDTH]
        xs_keep[...] = xs
        b_t[...] = conv[:, SSD_WIDTH:SSD_WIDTH + SSD_BC].T
        c_t[...] = conv[:, SSD_WIDTH + SSD_BC:CONV_DIM].T
        dt = _softplus(dt_ref[...] + dtb_ref[...])
        d_a = dt * (-jnp.exp(alog_ref[...]))
        dt_t = dt.T
        dec_t[...] = jnp.exp(d_a).T
        xs_t = xs.T
        for k in range(SSD_HEADS):
            blk = slice(k * SSD_HEAD_DIM, (k + 1) * SSD_HEAD_DIM)
            xdt_t[blk, :] = xs_t[blk, :] * dt_t[k:k + 1, :]
        v = vp_ref[...]
        for k in range(POOL_BUF - 1):
            po_ref[k] = sp_ref[k + 1]
        po_ref[POOL_BUF - 1] = v
        for g, w in enumerate(POOL_WINDOWS):
            blk = slice(g * POOL_GROUP_DIM, (g + 1) * POOL_GROUP_DIM)
            acc = sp_ref[POOL_BUF - (w - 1), :, blk]
            for k in range(w - 2, 0, -1):
                acc = acc + sp_ref[POOL_BUF - k, :, blk]
            acc = acc + v[:, blk]
            m = acc / float(min(PAST_LEN + 1, w)) - v[:, blk]
            yg = _dot(m.astype(BF16), plw_ref[g]) + plb_ref[:, blk]
            mix_ref[:, blk] = (yg * psc_ref[:, blk]).astype(BF16)

    g_off = pl.multiple_of((h // (SSD_HEADS // SSD_GROUPS)) * SSD_STATE, SSD_STATE)
    h_off = pl.multiple_of(h * SSD_HEAD_DIM, SSD_HEAD_DIM)
    b_g = b_t[pl.ds(g_off, SSD_STATE), :]
    c_g = c_t[pl.ds(g_off, SSD_STATE), :]
    dec = dec_t[pl.ds(h, 1), :]
    xdt = xdt_t[pl.ds(h_off, SSD_HEAD_DIM), :]
    y_rows = []
    for p in range(SSD_HEAD_DIM):
        s_new = st_ref[0, p] * dec + xdt[p:p + 1, :] * b_g
        so_ref[0, p] = s_new
        y_rows.append(jnp.sum(s_new * c_g, axis=0, keepdims=True))
    y_t[pl.ds(h_off, SSD_HEAD_DIM), :] = jnp.concatenate(y_rows, axis=0)

    @pl.when(h == pl.num_programs(0) - 1)
    def _():
        xs = xs_keep[...]
        y = y_t[...].T + dsk_ref[...] * xs
        y = y * _silu(z_ref[...])
        width = SSD_WIDTH // SSD_GROUPS
        for g in range(SSD_GROUPS):
            blk = slice(g * width, (g + 1) * width)
            yg = y[:, blk]
            rs = lax.rsqrt(jnp.mean(yg * yg, axis=-1, keepdims=True) + EPS)
            mix_ref[:, POOL_WIDTH + g * width:POOL_WIDTH + (g + 1) * width] = (yg * rs * nw_ref[:, blk]).astype(BF16)


def _mix_step(vp, z, xbc, dt, pool_t, conv_t, ssm_t, cw, cb, dtb, alog, dsk, nw, plw16, plb, psc):
    n = vp.shape[0]
    fixed2 = lambda h: (0, 0)
    fixed3 = lambda h: (0, 0, 0)
    st_spec = pl.BlockSpec((1, SSD_HEAD_DIM, SSD_STATE, n), lambda h: (h, 0, 0, 0))
    return pl.pallas_call(
        _mix_step_body,
        grid=(SSD_HEADS,),
        in_specs=[pl.BlockSpec((n, POOL_WIDTH), fixed2), pl.BlockSpec((n, SSD_WIDTH), fixed2),
                  pl.BlockSpec((n, CONV_DIM), fixed2), pl.BlockSpec((n, LANES), fixed2),
                  pl.BlockSpec((POOL_BUF, n, POOL_WIDTH), fixed3),
                  pl.BlockSpec((SSD_CONV - 1, n, CONV_DIM), fixed3),
                  st_spec,
                  pl.BlockSpec((SSD_CONV, CONV_DIM), fixed2), pl.BlockSpec((1, CONV_DIM), fixed2),
                  pl.BlockSpec((1, LANES), fixed2), pl.BlockSpec((1, LANES), fixed2),
                  pl.BlockSpec((1, SSD_WIDTH), fixed2), pl.BlockSpec((1, SSD_WIDTH), fixed2),
                  pl.BlockSpec((len(POOL_WINDOWS), POOL_GROUP_DIM, POOL_GROUP_DIM), fixed3),
                  pl.BlockSpec((1, POOL_WIDTH), fixed2), pl.BlockSpec((1, POOL_WIDTH), fixed2)],
        out_specs=[pl.BlockSpec((n, D_MODEL), fixed2), pl.BlockSpec((POOL_BUF, n, POOL_WIDTH), fixed3),
                   pl.BlockSpec((SSD_CONV - 1, n, CONV_DIM), fixed3), st_spec],
        out_shape=[jax.ShapeDtypeStruct((n, D_MODEL), BF16), jax.ShapeDtypeStruct(pool_t.shape, F32),
                   jax.ShapeDtypeStruct(conv_t.shape, F32), jax.ShapeDtypeStruct(ssm_t.shape, F32)],
        scratch_shapes=[pltpu.VMEM((SSD_WIDTH, n), F32), pltpu.VMEM((LANES, n), F32),
                        pltpu.VMEM((SSD_BC, n), F32), pltpu.VMEM((SSD_BC, n), F32),
                        pltpu.VMEM((n, SSD_WIDTH), F32), pltpu.VMEM((SSD_WIDTH, n), F32)],
        compiler_params=_cparams("arbitrary"),
        name="mix_step",
    )(vp, z, xbc, dt, pool_t, conv_t, ssm_t, cw, cb, dtb, alog, dsk, nw, plw16, plb, psc)


PROMPT_TILE = 512


def _sort_tables(counts, tile_tokens, data_rows):
    cnt = jnp.concatenate(counts, axis=0)
    pc = (cnt + RUN_PAD - 1) // RUN_PAD * RUN_PAD
    off_local = jnp.cumsum(pc, axis=1) - pc
    tile_rows = jnp.sum(pc, axis=1)
    region = (jnp.sum(pc, axis=0) + MOE_TILE - 1) // MOE_TILE * MOE_TILE
    base = jnp.cumsum(region) - region
    dst = base[None, :] + jnp.cumsum(pc, axis=0) - pc
    per_pass = []
    lo = 0
    for c, tm in zip(counts, tile_tokens):
        hi = lo + c.shape[0]
        n_chunk = _sorted_rows_per_tile(tm) // RUN_PAD
        s = jnp.arange(n_chunk, dtype=jnp.int32) * RUN_PAD
        begins = off_local[lo:hi, None, :]
        ends = begins + pc[lo:hi, None, :]
        inside = (begins <= s[None, :, None]) & (s[None, :, None] < ends)
        shift = jnp.sum(jnp.where(inside, dst[lo:hi, None, :] - begins, 0), axis=2)
        valid = s[None, :] < tile_rows[lo:hi, None]
        parity = (jnp.arange(hi - lo, dtype=jnp.int32) % 2)[:, None]
        spare = data_rows + parity * _sorted_rows_per_tile(max(tile_tokens)) + s[None, :]
        scatter_dst = jnp.where(valid, s[None, :] + shift, spare).astype(jnp.int32)
        gather_src = jnp.where(valid, s[None, :] + shift, 0).astype(jnp.int32)
        off_v = jnp.broadcast_to(off_local[lo:hi, :, None].astype(F32), (hi - lo, N_EXPERTS, LANES))
        per_pass.append((scatter_dst, gather_src, off_v))
        lo = hi
    tiles_cum = jnp.cumsum(region // MOE_TILE)
    n_active = tiles_cum[-1]
    return per_pass, tiles_cum, n_active


def _max_sorted_rows(tile_counts, tile_tokens):
    rows = sum(n * (2 * tm + N_EXPERTS * (RUN_PAD - 1)) for n, tm in zip(tile_counts, tile_tokens))
    data_rows = (-(-rows // MOE_TILE) + N_EXPERTS) * MOE_TILE
    return data_rows, data_rows + 2 * _sorted_rows_per_tile(max(tile_tokens))


def _row(v):
    return v.reshape(1, -1).astype(F32)


def _pad_lanes(v):
    return jnp.pad(v.reshape(1, -1).astype(F32), ((0, 0), (0, LANES - v.size)))


def _state_from_blocks(st):
    n = st.shape[0]
    hpg = SSD_HEADS // SSD_GROUPS
    s6 = st.reshape(n, SSD_GROUPS, SSD_STATE, SSD_GROUPS, hpg, SSD_HEAD_DIM)
    per_group = [s6[:, g, :, g] for g in range(SSD_GROUPS)]
    s = jnp.stack(per_group, axis=1)
    return jnp.transpose(s, (0, 1, 3, 4, 2)).reshape(n, SSD_HEADS, SSD_HEAD_DIM, SSD_STATE)


def kernel(x_prompt, x_sample, p_prompt, p_sample, state_pool, state_conv, state_ssm, norm_mix, w_in, pool_lin_w, pool_lin_b, pool_scale, conv_w, conv_b, dt_bias, a_log, d_skip, ssd_norm, w_out, norm_ffn, router_grp_w, router_grp_b, router_exp_w, router_exp_b, exp_w_gate, exp_w_up, exp_w_down, norm_ple, ple_gate_w, ple_gate_b, ple_proj_w, norm_final):
    nb, seq, _ = x_prompt.shape
    ns = x_sample.shape[0]
    assert ns == LANES and x_sample.shape[1] == 1 and seq % MIX_ROWS == 0 and seq >= POOL_BUF

    w_in16 = jnp.pad(w_in[0], ((0, 0), (0, IN_COLS - w_in.shape[2]))).astype(BF16)
    w_out16 = w_out[0].astype(BF16)
    g_mix, g_ffn, g_ple, g_fin = _row(norm_mix[0]), _row(norm_ffn[0]), _row(norm_ple[0]), _row(norm_final)
    cw, cb = conv_w[0].astype(F32), _row(conv_b[0])
    dtb, alog = _pad_lanes(dt_bias[0]), _pad_lanes(a_log[0])
    dsk = _row(jnp.repeat(d_skip[0], SSD_HEAD_DIM))
    nw = _row(ssd_norm[0])
    plw16 = pool_lin_w[0].astype(BF16)
    plb, psc = _row(pool_lin_b[0]), _row(pool_scale[0])
    zeros4 = jnp.zeros((D_MODEL, 8 - N_EXPERT_GROUPS), F32)
    r_w = jnp.concatenate([router_grp_w[0], zeros4, router_exp_w[0],
                           jnp.zeros((D_MODEL, LANES - 8 - N_EXPERTS), F32)], axis=1).T
    r_hi = r_w.astype(BF16)
    r_lo = (r_w - r_hi.astype(F32)).astype(BF16)
    r_b = jnp.concatenate([router_grp_b[0], jnp.zeros((8 - N_EXPERT_GROUPS,), F32), router_exp_b[0],
                           jnp.zeros((LANES - 8 - N_EXPERTS,), F32)])
    wg = exp_w_gate[0].reshape(N_EXPERTS, D_MODEL, EXPERT_FF)
    wu = exp_w_up[0].reshape(N_EXPERTS, D_MODEL, EXPERT_FF)
    wd = exp_w_down[0].reshape(N_EXPERTS, EXPERT_FF, D_MODEL)
    pg16 = ple_gate_w[0].astype(BF16)
    pgb = _row(ple_gate_b[0])
    pp16 = ple_proj_w[0].astype(BF16)

    def route(x, mix16, tm):
        r_bias = jnp.broadcast_to(r_b[:, None], (LANES, tm))
        return _out_proj(x, mix16, w_out16, g_ffn, r_hi, r_lo, r_bias, tm)

    xp = x_prompt.reshape(nb * seq, D_MODEL)
    h1_p, u_p, rt_p, gate_p, cnt_p, st, pool_tail, conv_tail = _front_prompt(
        xp, g_mix, w_in16, cw, cb, dtb, alog, dsk, nw, plw16, plb, psc, w_out16, g_ffn, r_hi, r_lo,
        jnp.broadcast_to(r_b[:, None], (LANES, MIX_ROWS)), nb, seq)
    pool_p = pool_tail[:, TAIL_ROWS - POOL_BUF:]
    conv_p = conv_tail[:, CONV_HDR - (SSD_CONV - 1):]
    ssm_p = _state_from_blocks(st)

    xs_ = x_sample.reshape(ns, D_MODEL)
    vp_s, z_s, xbc_s, dt_s = _in_proj(xs_, g_mix, w_in16, ns)
    mix_s, pool_t, conv_t, ssm_t = _mix_step(
        vp_s, z_s, xbc_s, dt_s, jnp.transpose(state_pool[0], (1, 0, 2)), jnp.transpose(state_conv[0], (1, 0, 2)),
        jnp.transpose(state_ssm[0], (1, 2, 3, 0)), cw, cb, dtb, alog, dsk, nw, plw16, plb, psc)
    h1_s, u_s, rt_s, gate_s, cnt_s = route(xs_, mix_s, ns)
    pool_s = jnp.transpose(pool_t, (1, 0, 2))
    conv_s = jnp.transpose(conv_t, (1, 0, 2))
    ssm_s = jnp.transpose(ssm_t, (3, 0, 1, 2))

    counts = [cnt_p[:, :, 0].astype(jnp.int32), cnt_s[:, :, 0].astype(jnp.int32)]
    tiles = (PROMPT_TILE, ns)
    data_rows, total_rows = _max_sorted_rows([c.shape[0] for c in counts], tiles)
    (tab_p, tab_s), tiles_cum, n_active = _sort_tables(counts, tiles, data_rows)
    xs_sorted, dest_p = _sort_tokens(tab_p[0], u_p, rt_p, tab_p[2], None, PROMPT_TILE, total_rows)
    xs_sorted, dest_s = _sort_tokens(tab_s[0], u_s, rt_s, tab_s[2], xs_sorted, ns, total_rows)
    tile_start = jnp.concatenate([jnp.zeros((1,), jnp.int32), tiles_cum.astype(jnp.int32)])
    ys_sorted = _moe_sorted(tile_start, n_active.reshape(1).astype(jnp.int32), xs_sorted, wg, wu, wd)

    y_prompt = _ple(tab_p[1], h1_p, dest_p, gate_p, p_prompt[0].reshape(nb * seq, PLE_DIM),
                    g_ple, pg16, pgb, pp16, g_fin, ys_sorted, PROMPT_TILE)
    y_sample = _ple(tab_s[1], h1_s, dest_s, gate_s, p_sample[0].reshape(ns, PLE_DIM),
                    g_ple, pg16, pgb, pp16, g_fin, ys_sorted, ns)

    return (y_prompt.reshape(nb, seq, D_MODEL), y_sample.reshape(ns, 1, D_MODEL),
            pool_p[None], conv_p[None], ssm_p[None], pool_s[None], conv_s[None], ssm_s[None])
```

```python
import functools

import jax
import jax.numpy as jnp
import numpy as np
from jax import lax
from jax.experimental import pallas as pl
from jax.experimental.pallas import tpu as pltpu

F32 = jnp.float32
BF16 = jnp.bfloat16

D_MODEL = 1024
POOL_WIDTH = 512
POOL_WINDOWS = (2, 4, 8, 16)
POOL_GROUP_DIM = 128
POOL_BUF = 15
SSD_WIDTH = 512
SSD_HEAD_DIM = 64
SSD_HEADS = 8
SSD_GROUPS = 2
SSD_STATE = 64
SSD_CONV = 4
SSD_BC = SSD_GROUPS * SSD_STATE
CONV_DIM = SSD_WIDTH + 2 * SSD_BC
N_EXPERT_GROUPS = 4
EXPERTS_PER_GROUP = 8
N_EXPERTS = N_EXPERT_GROUPS * EXPERTS_PER_GROUP
EXPERT_FF = 256
PLE_DIM = 256
PAST_LEN = 16384
EPS = 1e-6

LANES = 128
IN_COLS = 1920
DT_OFF = POOL_WIDTH + SSD_WIDTH + CONV_DIM
VMEM_LIMIT = 56 * 1024 * 1024


def _cparams(*sem):
    return pltpu.CompilerParams(dimension_semantics=sem, vmem_limit_bytes=VMEM_LIMIT)


def _rms(x, g):
    return x * lax.rsqrt(jnp.mean(x * x, axis=-1, keepdims=True) + EPS) * g


def _sigmoid(x):
    return 1.0 / (1.0 + jnp.exp(-x))


def _silu(x):
    return x * _sigmoid(x)


def _split3(v):
    hi = v.astype(BF16)
    r = v - hi.astype(F32)
    mid = r.astype(BF16)
    lo = (r - mid.astype(F32)).astype(BF16)
    return hi, mid, lo


def _dot(a, b):
    return jnp.dot(a, b, preferred_element_type=F32)


def _dot_nt(a, b):
    return lax.dot_general(a, b, (((1,), (1,)), ((), ())), preferred_element_type=F32)


def _dot_tn(a, b):
    return lax.dot_general(a, b, (((0,), (0,)), ((), ())), preferred_element_type=F32)


def _dot_exact_lhs(sel, v, terms=3):
    acc = None
    for t in _split3(v)[:terms]:
        p = _dot(sel, t)
        acc = p if acc is None else acc + p
    return acc


def _dot_exact_rhs(v, sel, terms=3):
    acc = None
    for t in _split3(v)[:terms]:
        p = _dot(t, sel)
        acc = p if acc is None else acc + p
    return acc


def _in_proj_body(x_ref, g_ref, w_ref, vp_ref, z_ref, xbc_ref, dt_ref):
    a16 = _rms(x_ref[...], g_ref[...]).astype(BF16)
    vp_ref[...] = _dot(a16, w_ref[:, 0:POOL_WIDTH])
    z_ref[...] = _dot(a16, w_ref[:, POOL_WIDTH:POOL_WIDTH + SSD_WIDTH])
    xbc_ref[...] = _dot(a16, w_ref[:, POOL_WIDTH + SSD_WIDTH:DT_OFF])
    dt_ref[...] = _dot(a16, w_ref[:, DT_OFF:IN_COLS])


def _in_proj(x, g, w16, tm):
    t = x.shape[0]
    row = lambda i: (i, 0)
    fixed = lambda i: (0, 0)
    return pl.pallas_call(
        _in_proj_body,
        grid=(t // tm,),
        in_specs=[pl.BlockSpec((tm, D_MODEL), row), pl.BlockSpec((1, D_MODEL), fixed),
                  pl.BlockSpec((D_MODEL, IN_COLS), fixed)],
        out_specs=[pl.BlockSpec((tm, POOL_WIDTH), row), pl.BlockSpec((tm, SSD_WIDTH), row),
                   pl.BlockSpec((tm, CONV_DIM), row), pl.BlockSpec((tm, LANES), row)],
        out_shape=[jax.ShapeDtypeStruct((t, POOL_WIDTH), F32), jax.ShapeDtypeStruct((t, SSD_WIDTH), F32),
                   jax.ShapeDtypeStruct((t, CONV_DIM), F32), jax.ShapeDtypeStruct((t, LANES), F32)],
        compiler_params=_cparams("parallel"),
        name="in_proj",
    )(x, g, w16)


def _route(lg):
    tm = lg.shape[1]
    gl = lg[0:N_EXPERT_GROUPS, :]
    gmax = jnp.max(gl, axis=0, keepdims=True)
    gsum = jnp.sum(jnp.exp(gl - gmax), axis=0, keepdims=True)
    g_w = 1.0 / gsum
    gi = lax.broadcasted_iota(jnp.int32, gl.shape, 0)
    g_idx = jnp.min(jnp.where(gl == gmax, gi, N_EXPERT_GROUPS), axis=0, keepdims=True)
    sel = jnp.zeros((EXPERTS_PER_GROUP, tm), F32)
    for g in range(N_EXPERT_GROUPS):
        blk = lg[8 + g * EXPERTS_PER_GROUP:8 + (g + 1) * EXPERTS_PER_GROUP, :]
        sel = jnp.where(g_idx == g, blk, sel)
    ei = lax.broadcasted_iota(jnp.int32, sel.shape, 0)
    m1 = jnp.max(sel, axis=0, keepdims=True)
    i1 = jnp.min(jnp.where(sel == m1, ei, EXPERTS_PER_GROUP), axis=0, keepdims=True)
    rest = jnp.where(ei == i1, -jnp.inf, sel)
    m2 = jnp.max(rest, axis=0, keepdims=True)
    i2 = jnp.min(jnp.where(rest == m2, ei, EXPERTS_PER_GROUP), axis=0, keepdims=True)
    p2 = jnp.exp(m2 - m1)
    w1 = g_w / (1.0 + p2)
    w2 = g_w * p2 / (1.0 + p2)
    return g_idx * EXPERTS_PER_GROUP + i1, g_idx * EXPERTS_PER_GROUP + i2, w1, w2


def _out_proj_body(x_ref, mix_ref, w_ref, g_ref, rh_ref, rl_ref, rb_ref, h_ref, u_ref, rt_ref, wt_ref, cnt_ref):
    _out_proj_and_route(x_ref[...], mix_ref[...], w_ref, g_ref, rh_ref, rl_ref, rb_ref,
                        h_ref, u_ref, rt_ref, wt_ref, cnt_ref)


def _out_proj_and_route(x, mix16, w_ref, g_ref, rh_ref, rl_ref, rb_ref, h_ref, u_ref, rt_ref, wt_ref, cnt_ref,
                        r0=0, accumulate=False):
    tm = x.shape[0]
    h = x + _dot(mix16, w_ref[...])
    h_ref[r0:r0 + tm, :] = h
    _route_rows(h, g_ref, rh_ref, rl_ref, rb_ref, u_ref, rt_ref, wt_ref, cnt_ref, r0=r0, accumulate=accumulate)


def _route_rows(h, g_ref, rh_ref, rl_ref, rb_ref, u_ref, rt_ref, wt_ref, cnt_ref, r0=0, accumulate=False):
    tm = h.shape[0]
    u = _rms(h, g_ref[...])
    u_hi = u.astype(BF16)
    u_ref[r0:r0 + tm, :] = u_hi
    u_lo = (u - u_hi.astype(F32)).astype(BF16)
    lg = (_dot_nt(rh_ref[...], u_hi) + _dot_nt(rh_ref[...], u_lo) + _dot_nt(rl_ref[...], u_hi)
          + rb_ref[:, r0:r0 + tm])
    b1, b2, w1, w2 = _route(lg)
    r8 = lax.broadcasted_iota(jnp.int32, (8, tm), 0)
    rt_ref[0, :, r0:r0 + tm] = jnp.where(r8 == 0, b1.astype(F32), jnp.where(r8 == 1, b2.astype(F32), 0.0))
    wt_ref[r0:r0 + tm, :] = jnp.concatenate(
        [jnp.broadcast_to(w1, (LANES, tm)).T, jnp.broadcast_to(w2, (LANES, tm)).T], axis=1)
    kio = lax.broadcasted_iota(jnp.int32, (N_EXPERTS, tm), 0)
    hits = ((kio == b1) | (kio == b2)).astype(F32)
    cnt = jnp.broadcast_to(jnp.sum(hits, axis=1, keepdims=True), (N_EXPERTS, LANES))
    cnt_ref[0] = cnt_ref[0] + cnt if accumulate else cnt


def _out_proj(x, mix16, w16, g, r_hi, r_lo, r_bias, tm):
    t = x.shape[0]
    nt = t // tm
    row = lambda i: (i, 0)
    fixed = lambda i: (0, 0)
    tile3 = lambda i: (i, 0, 0)
    return pl.pallas_call(
        _out_proj_body,
        grid=(nt,),
        in_specs=[pl.BlockSpec((tm, D_MODEL), row), pl.BlockSpec((tm, D_MODEL), row),
                  pl.BlockSpec((D_MODEL, D_MODEL), fixed), pl.BlockSpec((1, D_MODEL), fixed),
                  pl.BlockSpec((LANES, D_MODEL), fixed), pl.BlockSpec((LANES, D_MODEL), fixed),
                  pl.BlockSpec((LANES, tm), fixed)],
        out_specs=[pl.BlockSpec((tm, D_MODEL), row), pl.BlockSpec((tm, D_MODEL), row),
                   pl.BlockSpec((1, 8, tm), tile3), pl.BlockSpec((tm, 2 * LANES), row),
                   pl.BlockSpec((1, N_EXPERTS, LANES), tile3)],
        out_shape=[jax.ShapeDtypeStruct((t, D_MODEL), F32), jax.ShapeDtypeStruct((t, D_MODEL), BF16),
                   jax.ShapeDtypeStruct((nt, 8, tm), F32), jax.ShapeDtypeStruct((t, 2 * LANES), F32),
                   jax.ShapeDtypeStruct((nt, N_EXPERTS, LANES), F32)],
        compiler_params=_cparams("parallel"),
        name="out_proj_router",
    )(x, mix16, w16, g, r_hi, r_lo, r_bias)


RUN_PAD = 16
MOE_TILE = 256
MOE_BUFS = 6
MOE_AHEAD = 4


def _tile_lanes(v, width):
    reps = width // LANES
    return v if reps == 1 else jnp.concatenate([v] * reps, axis=1)


def _sorted_rows_per_tile(tm):
    need = 2 * tm + N_EXPERTS * (RUN_PAD - 1)
    return -(-need // LANES) * LANES


def _sort_body(cd_ref, u_ref, rt_ref, off_ref, *rest):
    xs_ref, dt_ref, loc, sem = rest[-4:]
    i = pl.program_id(0)
    tm = u_ref.shape[0]
    rows = loc.shape[1]
    rt = rt_ref[0]
    b1 = rt[0:1, :].astype(jnp.int32)
    b2 = rt[1:2, :].astype(jnp.int32)
    kio = lax.broadcasted_iota(jnp.int32, (N_EXPERTS, tm), 0)
    o1 = kio == b1
    o2 = kio == b2
    before = (lax.broadcasted_iota(jnp.int32, (tm, tm), 0) < lax.broadcasted_iota(jnp.int32, (tm, tm), 1))
    start = _dot((o1 | o2).astype(BF16), before.astype(BF16)) + _tile_lanes(off_ref[0], tm)
    d1 = jnp.sum(jnp.where(o1, start, 0.0), axis=0, keepdims=True)
    d2 = jnp.sum(jnp.where(o2, start, 0.0), axis=0, keepdims=True)
    dt_ref[...] = jnp.concatenate([jnp.broadcast_to(d1, (LANES, tm)).T, jnp.broadcast_to(d2, (LANES, tm)).T], axis=1)
    rio = lax.broadcasted_iota(jnp.int32, (rows, tm), 0)
    perm = ((rio == d1.astype(jnp.int32)) | (rio == d2.astype(jnp.int32))).astype(BF16)
    slot = i % 2
    loc[slot] = _dot(perm, u_ref[...]).astype(BF16)

    def chunk_copy(step, sl, c):
        return pltpu.make_async_copy(
            loc.at[sl, pl.ds(c * RUN_PAD, RUN_PAD), :],
            xs_ref.at[pl.ds(pl.multiple_of(cd_ref[step, c], RUN_PAD), RUN_PAD), :], sem.at[sl])

    n_chunk = rows // RUN_PAD
    for c in range(n_chunk):
        chunk_copy(i, slot, c).start()

    @pl.when(i > 0)
    def _():
        for c in range(n_chunk):
            chunk_copy(i - 1, 1 - slot, c).wait()

    @pl.when(i == pl.num_programs(0) - 1)
    def _():
        for c in range(n_chunk):
            chunk_copy(i, slot, c).wait()


def _sort_tokens(chunk_dst, u16, rt, off_v, xs_prev, tm, total_rows):
    t = u16.shape[0]
    rows = _sorted_rows_per_tile(tm)
    in_specs = [pl.BlockSpec((tm, D_MODEL), lambda i, cd: (i, 0)),
                pl.BlockSpec((1, 8, tm), lambda i, cd: (i, 0, 0)),
                pl.BlockSpec((1, N_EXPERTS, LANES), lambda i, cd: (i, 0, 0))]
    args = [chunk_dst, u16, rt, off_v]
    aliases = {}
    if xs_prev is not None:
        in_specs.append(pl.BlockSpec(memory_space=pl.ANY))
        aliases = {len(args): 0}
        args.append(xs_prev)
    return pl.pallas_call(
        _sort_body,
        grid_spec=pltpu.PrefetchScalarGridSpec(
            num_scalar_prefetch=1,
            grid=(t // tm,),
            in_specs=in_specs,
            out_specs=[pl.BlockSpec(memory_space=pl.ANY),
                       pl.BlockSpec((tm, 2 * LANES), lambda i, cd: (i, 0))],
            scratch_shapes=[pltpu.VMEM((2, rows, D_MODEL), BF16), pltpu.SemaphoreType.DMA((2,))]),
        out_shape=[jax.ShapeDtypeStruct((total_rows, D_MODEL), BF16),
                   jax.ShapeDtypeStruct((t, 2 * LANES), F32)],
        input_output_aliases=aliases,
        compiler_params=_cparams("arbitrary"),
        name="sort_tokens",
    )(*args)


def _moe_body(ts_ref, na_ref, xs_ref, wg_ref, wu_ref, wd_ref, ys_ref, xbuf, ybuf, wg16, wu16, wd16, sem_in, sem_out):
    k = pl.program_id(0)
    n_act = na_ref[0]

    def x_copy(g, sl):
        return pltpu.make_async_copy(xs_ref.at[pl.ds(pl.multiple_of(g * MOE_TILE, MOE_TILE), MOE_TILE), :],
                                     xbuf.at[sl], sem_in.at[sl])

    def y_copy(g, sl):
        return pltpu.make_async_copy(ybuf.at[sl],
                                     ys_ref.at[pl.ds(pl.multiple_of(g * MOE_TILE, MOE_TILE), MOE_TILE), :],
                                     sem_out.at[sl])

    @pl.when(k == 0)
    def _():
        for j in range(MOE_AHEAD):
            @pl.when(j < n_act)
            def _():
                x_copy(j, j).start(priority=1)

    wg16[...] = wg_ref[0].astype(BF16)
    wu16[...] = wu_ref[0].astype(BF16)
    wd16[...] = wd_ref[0].astype(BF16)

    def arrive(g):
        sl = g % MOE_BUFS

        @pl.when(g + MOE_AHEAD < n_act)
        def _():
            x_copy(g + MOE_AHEAD, (g + MOE_AHEAD) % MOE_BUFS).start(priority=1)

        x_copy(g, sl).wait()

        @pl.when(g >= MOE_BUFS)
        def _():
            y_copy(g - MOE_BUFS, sl).wait()

    def tiles(g, count):
        for j in range(count):
            arrive(g + j)
        acts = []
        for j in range(count):
            x = xbuf[(g + j) % MOE_BUFS]
            acts.append((_silu(_dot(x, wg16[...])) * _dot(x, wu16[...])).astype(BF16))
        for j in range(count):
            ybuf[(g + j) % MOE_BUFS] = _dot(acts[j], wd16[...]).astype(BF16)
        for j in range(count):
            y_copy(g + j, (g + j) % MOE_BUFS).start()

    g0 = ts_ref[k]
    n_here = ts_ref[k + 1] - g0

    def pair(p, carry):
        tiles(g0 + 2 * p, 2)
        return carry

    lax.fori_loop(0, n_here // 2, pair, 0)

    @pl.when(n_here % 2 == 1)
    def _():
        tiles(g0 + n_here - 1, 1)

    @pl.when(k == pl.num_programs(0) - 1)
    def _():
        for j in range(1, MOE_BUFS + 1):
            @pl.when(n_act >= j)
            def _():
                y_copy(n_act - j, (n_act - j) % MOE_BUFS).wait()


def _moe_sorted(tile_start, n_active, xs, wg, wu, wd):
    w_map = lambda k, ts, na: (k, 0, 0)
    return pl.pallas_call(
        _moe_body,
        grid_spec=pltpu.PrefetchScalarGridSpec(
            num_scalar_prefetch=2,
            grid=(N_EXPERTS,),
            in_specs=[pl.BlockSpec(memory_space=pl.ANY),
                      pl.BlockSpec((1, D_MODEL, EXPERT_FF), w_map),
                      pl.BlockSpec((1, D_MODEL, EXPERT_FF), w_map),
                      pl.BlockSpec((1, EXPERT_FF, D_MODEL), w_map)],
            out_specs=pl.BlockSpec(memory_space=pl.ANY),
            scratch_shapes=[pltpu.VMEM((MOE_BUFS, MOE_TILE, D_MODEL), BF16),
                            pltpu.VMEM((MOE_BUFS, MOE_TILE, D_MODEL), BF16),
                            pltpu.VMEM((D_MODEL, EXPERT_FF), BF16), pltpu.VMEM((D_MODEL, EXPERT_FF), BF16),
                            pltpu.VMEM((EXPERT_FF, D_MODEL), BF16),
                            pltpu.SemaphoreType.DMA((MOE_BUFS,)), pltpu.SemaphoreType.DMA((MOE_BUFS,))]),
        out_shape=jax.ShapeDtypeStruct(xs.shape, BF16),
        compiler_params=_cparams("arbitrary"),
        name="moe_sorted",
    )(tile_start, n_active, xs, wg, wu, wd)


def _ple_body(cd_ref, h_ref, dt_ref, wt_ref, p_ref, gn_ref, wg_ref, bg_ref, wp_ref, fn_ref, ys_ref,
              o_ref, loc, sem):
    i = pl.program_id(0)
    tm = h_ref.shape[0]
    rows = loc.shape[1]
    n_chunk = rows // RUN_PAD
    slot = i % 2

    def chunk_copy(step, sl, c):
        return pltpu.make_async_copy(
            ys_ref.at[pl.ds(pl.multiple_of(cd_ref[step, c], RUN_PAD), RUN_PAD), :],
            loc.at[sl, pl.ds(c * RUN_PAD, RUN_PAD), :], sem.at[sl])

    def fetch(step, sl):
        for c in range(n_chunk):
            chunk_copy(step, sl, c).start()

    @pl.when(i == 0)
    def _():
        fetch(0, 0)

    @pl.when(i + 1 < pl.num_programs(0))
    def _():
        fetch(i + 1, 1 - slot)

    for c in range(n_chunk):
        chunk_copy(i, slot, c).wait()

    ci = lax.broadcasted_iota(jnp.int32, (tm, LANES), 1)
    d1 = dt_ref[:, 0:LANES].astype(jnp.int32)
    d2 = dt_ref[:, LANES:2 * LANES].astype(jnp.int32)
    w1 = wt_ref[:, 0:LANES]
    w2 = wt_ref[:, LANES:2 * LANES]
    sel = jnp.concatenate(
        [jnp.where(d1 == ci + m * LANES, w1, jnp.where(d2 == ci + m * LANES, w2, 0.0)) for m in range(rows // LANES)],
        axis=1).astype(BF16)
    h = h_ref[...] + _dot(sel, loc[slot])
    a16 = _rms(h, gn_ref[...]).astype(BF16)
    gate = _sigmoid(_dot(a16, wg_ref[...]) + bg_ref[...])
    pp = _dot(p_ref[...].astype(BF16), wp_ref[...])
    h = h + gate * pp
    o_ref[...] = _rms(h, fn_ref[...])


def _ple(chunk_src, h1, dest_t, gate_t, p, g_ple, wg16, bg, wp16, g_final, ys, tm):
    t = h1.shape[0]
    row = lambda i, cd: (i, 0)
    fixed = lambda i, cd: (0, 0)
    return pl.pallas_call(
        _ple_body,
        grid_spec=pltpu.PrefetchScalarGridSpec(
            num_scalar_prefetch=1,
            grid=(t // tm,),
            in_specs=[pl.BlockSpec((tm, D_MODEL), row), pl.BlockSpec((tm, 2 * LANES), row),
                      pl.BlockSpec((tm, 2 * LANES), row), pl.BlockSpec((tm, PLE_DIM), row),
                      pl.BlockSpec((1, D_MODEL), fixed), pl.BlockSpec((D_MODEL, D_MODEL), fixed),
                      pl.BlockSpec((1, D_MODEL), fixed), pl.BlockSpec((PLE_DIM, D_MODEL), fixed),
                      pl.BlockSpec((1, D_MODEL), fixed), pl.BlockSpec(memory_space=pl.ANY)],
            out_specs=pl.BlockSpec((tm, D_MODEL), row),
            scratch_shapes=[pltpu.VMEM((2, _sorted_rows_per_tile(tm), D_MODEL), BF16),
                            pltpu.SemaphoreType.DMA((2,))]),
        out_shape=jax.ShapeDtypeStruct((t, D_MODEL), F32),
        compiler_params=_cparams("arbitrary"),
        name="ple_final",
    )(chunk_src, h1, dest_t, gate_t, p, g_ple, wg16, bg, wp16, g_final, ys)


MIX_ROWS = 512
SSD_Q = 128
POOL_HDR = 128
CONV_HDR = 8
TAIL_ROWS = 16


def _softplus(x):
    return jnp.maximum(x, 0.0) + jnp.log1p(jnp.exp(-jnp.abs(x)))


def _mixer_constants():
    q = SSD_Q
    li = np.arange(q)[:, None]
    tri = (li >= np.arange(q)[None, :])
    expand = (np.arange(SSD_WIDTH)[None, :] // SSD_HEAD_DIM) == np.arange(LANES)[:, None]
    hpg = SSD_HEADS // SSD_GROUPS
    block = (np.arange(SSD_BC)[:, None] // SSD_STATE) == (np.arange(SSD_WIDTH)[None, :] // (SSD_HEAD_DIM * hpg))
    wj = np.arange(2 * q)[None, :]
    band = np.stack([(wj <= li + POOL_HDR) & (wj > li + POOL_HDR - w) for w in POOL_WINDOWS])
    as_f32 = lambda m: jnp.asarray(m.astype(np.float32))
    return (as_f32(tri), as_f32(tri).astype(BF16), as_f32(expand).astype(BF16), as_f32(block),
            as_f32(band).astype(BF16))


def _front_body(steps_per_seq, x_ref, xp_ref, gm_ref, win_ref, trif_ref, tri_ref, exp_ref, bmask_ref, band_ref,
                cw_ref, cb_ref, dtb_ref, alog_ref, dsk_ref, nw_ref,
                plw_ref, plb_ref, psc_ref, wout_ref, gf_ref, rh_ref, rl_ref, rb_ref,
                h_ref, u_ref, rt_ref, wt_ref, cnt_ref, st_ref, ptail_ref, ctail_ref,
                pool_ext, conv_ext, s_ref, z_ref, dt_ref, mix_ref, mixp_ref):
    s = pl.program_id(0)
    n_tiles = pl.num_programs(0) - 1
    real = s < n_tiles
    c = jnp.minimum(s, n_tiles - 1) % steps_per_seq
    rows = MIX_ROWS
    q_len = SSD_Q

    @pl.when(s == 0)
    def _():
        mixp_ref[...] = jnp.zeros_like(mixp_ref)

    @pl.when(c == 0)
    def _():
        pool_ext[0:POOL_HDR, :] = jnp.zeros((POOL_HDR, POOL_WIDTH), F32)
        conv_ext[0:CONV_HDR, :] = jnp.zeros((CONV_HDR, CONV_DIM), F32)
        s_ref[...] = jnp.zeros_like(s_ref)

    @pl.when(c > 0)
    def _():
        pool_ext[0:POOL_HDR, :] = pool_ext[rows:rows + POOL_HDR, :]
        conv_ext[0:CONV_HDR, :] = conv_ext[rows:rows + CONV_HDR, :]

    causal = trif_ref[...] > 0.5
    tri = tri_ref[...]
    expand = exp_ref[...]
    blockmask = bmask_ref[...]
    lane = lax.broadcasted_iota(jnp.int32, (q_len, LANES), 1)
    left = lane < SSD_HEAD_DIM
    a_neg = -jnp.exp(alog_ref[...])
    rowi = lax.broadcasted_iota(jnp.int32, (q_len, LANES), 0)

    half = rows // 2
    n_cols = 256

    def in_proj_chunks(r0):
        a16 = _rms(x_ref[r0:r0 + half, :], gm_ref[...]).astype(BF16)
        dests = ([(pool_ext, POOL_HDR + r0, k, 0) for k in range(0, POOL_WIDTH, n_cols)]
                 + [(z_ref, r0, k, POOL_WIDTH) for k in range(0, SSD_WIDTH, n_cols)]
                 + [(conv_ext, CONV_HDR + r0, k, POOL_WIDTH + SSD_WIDTH) for k in range(0, CONV_DIM, n_cols)])

        def chunk(ref, row, k, col0):
            def run():
                ref[row:row + half, k:k + n_cols] = _dot(a16, win_ref[:, col0 + k:col0 + k + n_cols])
            return run

        def dt_chunk():
            dt_ref[r0:r0 + half, :] = _dot(a16, win_ref[:, DT_OFF:IN_COLS])

        return [chunk(*d) for d in dests] + [dt_chunk]

    def out_proj_chunks(r0):
        def chunk(k):
            def run():
                h_ref[r0:r0 + half, k:k + n_cols] = (xp_ref[r0:r0 + half, k:k + n_cols]
                                                    + _dot(mixp_ref[r0:r0 + half, :], wout_ref[:, k:k + n_cols]))
            return run

        def route():
            _route_rows(h_ref[r0:r0 + half, :], gf_ref, rh_ref, rl_ref, rb_ref, u_ref, rt_ref, wt_ref, cnt_ref,
                        r0=r0, accumulate=r0 > 0)

        return [chunk(k) for k in range(0, D_MODEL, n_cols)] + [route]

    def mix_pair(q_first, fillers):
        fillers = list(fillers)

        def fill():
            if fillers:
                fillers.pop(0)()

        subs = range(q_first, q_first + 2)
        xs_l, b16_l, call_l, c16_l = {}, {}, {}, {}
        for q in subs:
            base = CONV_HDR + q * q_len
            conv = cb_ref[...] + conv_ext[base - 3:base - 3 + q_len, :] * cw_ref[0:1, :]
            conv = conv + conv_ext[base - 2:base - 2 + q_len, :] * cw_ref[1:2, :]
            conv = conv + conv_ext[base - 1:base - 1 + q_len, :] * cw_ref[2:3, :]
            conv = conv + conv_ext[base:base + q_len, :] * cw_ref[3:4, :]
            conv = _silu(conv)
            xs_l[q] = conv[:, 0:SSD_WIDTH]
            b16_l[q] = conv[:, SSD_WIDTH:SSD_WIDTH + SSD_BC].astype(BF16)
            call_l[q] = conv[:, SSD_WIDTH + SSD_BC:CONV_DIM]
            c16_l[q] = call_l[q].astype(BF16)
            fill()
        dt_l = {q: _softplus(dt_ref[q * q_len:(q + 1) * q_len, :] + dtb_ref[...]) for q in subs}
        acs_l = {q: _dot_exact_lhs(tri, dt_l[q] * a_neg) for q in subs}
        acst_l = {q: acs_l[q].T for q in subs}
        fill()
        dtx_l = {q: _dot_exact_rhs(dt_l[q], expand, terms=2) for q in subs}
        acsx_l = {q: _dot_exact_rhs(acs_l[q], expand) for q in subs}
        lastx_l = {q: acsx_l[q][q_len - 1:q_len, :] for q in subs}
        xdt_l = {q: xs_l[q] * dtx_l[q] for q in subs}
        xdt16_l = {q: xdt_l[q].astype(BF16) for q in subs}
        fill()
        contrib_l = {q: _dot_tn(b16_l[q], (xdt_l[q] * jnp.exp(lastx_l[q] - acsx_l[q])).astype(BF16)) * blockmask
                     for q in subs}
        cb_l = {q: [_dot_nt(jnp.where(lax.shift_right_logical(lane, 6) == g, call_l[q], 0.0).astype(BF16), b16_l[q])
                    for g in range(SSD_GROUPS)] for q in subs}
        fill()
        zs_l = {}
        for q in subs:
            zs = []
            for h in range(SSD_HEADS):
                blk = slice((h // 2) * LANES, (h // 2 + 1) * LANES)
                seg = acs_l[q][:, h:h + 1] - acst_l[q][h:h + 1, :]
                decay = jnp.where(causal, jnp.exp(seg), 0.0)
                scores = (cb_l[q][h // (SSD_HEADS // SSD_GROUPS)] * decay).astype(BF16)
                zs.append(_dot(scores, xdt16_l[q][:, blk]))
                if h % 4 == 3:
                    fill()
            zs_l[q] = zs
        yoff_l = {}
        for q in subs:
            s_old = s_ref[...]
            yoff_l[q] = _dot(c16_l[q], s_old.astype(BF16)) * jnp.exp(acsx_l[q])
            s_ref[...] = s_old * jnp.exp(lastx_l[q]) + contrib_l[q]
        fill()
        for q in subs:
            r0 = q * q_len
            z = z_ref[r0:r0 + q_len, :]
            y_blocks = []
            for j in range(SSD_HEADS // 2):
                blk = slice(j * LANES, (j + 1) * LANES)
                y = jnp.where(left, zs_l[q][2 * j], zs_l[q][2 * j + 1]) + yoff_l[q][:, blk]
                y = y + dsk_ref[:, blk] * xs_l[q][:, blk]
                y_blocks.append(y * _silu(z[:, blk]))
            for g in range(SSD_GROUPS):
                y0, y1 = y_blocks[2 * g], y_blocks[2 * g + 1]
                ss = jnp.sum(y0 * y0, axis=-1, keepdims=True) + jnp.sum(y1 * y1, axis=-1, keepdims=True)
                rs = lax.rsqrt(ss * (1.0 / (2 * LANES)) + EPS)
                for k, yk in ((2 * g, y0), (2 * g + 1, y1)):
                    blk = slice(k * LANES, (k + 1) * LANES)
                    out = yk * rs * nw_ref[:, blk]
                    mix_ref[r0:r0 + q_len, POOL_WIDTH + k * LANES:POOL_WIDTH + (k + 1) * LANES] = out.astype(BF16)
            fill()
        for q in subs:
            r0 = q * q_len
            pos = c * rows + r0 + rowi
            for g, w in enumerate(POOL_WINDOWS):
                blk = slice(g * POOL_GROUP_DIM, (g + 1) * POOL_GROUP_DIM)
                pe = pool_ext[r0:r0 + 2 * q_len, blk]
                winsum = _dot_exact_lhs(band_ref[g], pe, terms=2)
                cnt = jnp.minimum(pos + 1, w).astype(F32)
                m = winsum / cnt - pe[q_len:2 * q_len, :]
                yg = _dot(m.astype(BF16), plw_ref[g]) + plb_ref[:, blk]
                mix_ref[r0:r0 + q_len, blk] = (yg * psc_ref[:, blk]).astype(BF16)
            fill()
        while fillers:
            fill()

    for run in in_proj_chunks(0):
        run()
    mix_pair(0, in_proj_chunks(half) + out_proj_chunks(0))
    mix_pair(2, out_proj_chunks(half))
    mixp_ref[...] = mix_ref[...]

    @pl.when(real & (c == steps_per_seq - 1))
    def _():
        st_ref[0] = s_ref[...]
        ptail_ref[0] = pool_ext[POOL_HDR + rows - TAIL_ROWS:POOL_HDR + rows, :]
        ctail_ref[0] = conv_ext[rows:rows + CONV_HDR, :]


def _front_prompt(x, g_mix, w_in16, cw, cb, dtb, alog, dsk, nw, plw16, plb, psc, w_out16, g_ffn, r_hi, r_lo, r_bias,
                  nb, seq):
    steps = seq // MIX_ROWS
    nt = nb * steps
    cur = lambda s: (jnp.minimum(s, nt - 1), 0)
    prev = lambda s: (jnp.maximum(s - 1, 0), 0)
    prev3 = lambda s: (jnp.maximum(s - 1, 0), 0, 0)
    per_seq = lambda s: (jnp.minimum(s, nt - 1) // steps, 0, 0)
    fixed2 = lambda s: (0, 0)
    fixed3 = lambda s: (0, 0, 0)
    return pl.pallas_call(
        functools.partial(_front_body, steps),
        grid=(nt + 1,),
        in_specs=[pl.BlockSpec((MIX_ROWS, D_MODEL), cur), pl.BlockSpec((MIX_ROWS, D_MODEL), prev),
                  pl.BlockSpec((1, D_MODEL), fixed2), pl.BlockSpec((D_MODEL, IN_COLS), fixed2),
                  pl.BlockSpec((SSD_Q, SSD_Q), fixed2), pl.BlockSpec((SSD_Q, SSD_Q), fixed2),
                  pl.BlockSpec((LANES, SSD_WIDTH), fixed2), pl.BlockSpec((SSD_BC, SSD_WIDTH), fixed2),
                  pl.BlockSpec((len(POOL_WINDOWS), SSD_Q, 2 * SSD_Q), fixed3),
                  pl.BlockSpec((SSD_CONV, CONV_DIM), fixed2), pl.BlockSpec((1, CONV_DIM), fixed2),
                  pl.BlockSpec((1, LANES), fixed2), pl.BlockSpec((1, LANES), fixed2),
                  pl.BlockSpec((1, SSD_WIDTH), fixed2), pl.BlockSpec((1, SSD_WIDTH), fixed2),
                  pl.BlockSpec((len(POOL_WINDOWS), POOL_GROUP_DIM, POOL_GROUP_DIM), fixed3),
                  pl.BlockSpec((1, POOL_WIDTH), fixed2), pl.BlockSpec((1, POOL_WIDTH), fixed2),
                  pl.BlockSpec((D_MODEL, D_MODEL), fixed2), pl.BlockSpec((1, D_MODEL), fixed2),
                  pl.BlockSpec((LANES, D_MODEL), fixed2), pl.BlockSpec((LANES, D_MODEL), fixed2),
                  pl.BlockSpec((LANES, MIX_ROWS), fixed2)],
        out_specs=[pl.BlockSpec((MIX_ROWS, D_MODEL), prev), pl.BlockSpec((MIX_ROWS, D_MODEL), prev),
                   pl.BlockSpec((1, 8, MIX_ROWS), prev3), pl.BlockSpec((MIX_ROWS, 2 * LANES), prev),
                   pl.BlockSpec((1, N_EXPERTS, LANES), prev3),
                   pl.BlockSpec((1, SSD_BC, SSD_WIDTH), per_seq),
                   pl.BlockSpec((1, TAIL_ROWS, POOL_WIDTH), per_seq),
                   pl.BlockSpec((1, CONV_HDR, CONV_DIM), per_seq)],
        out_shape=[jax.ShapeDtypeStruct((nb * seq, D_MODEL), F32), jax.ShapeDtypeStruct((nb * seq, D_MODEL), BF16),
                   jax.ShapeDtypeStruct((nt, 8, MIX_ROWS), F32), jax.ShapeDtypeStruct((nb * seq, 2 * LANES), F32),
                   jax.ShapeDtypeStruct((nt, N_EXPERTS, LANES), F32),
                   jax.ShapeDtypeStruct((nb, SSD_BC, SSD_WIDTH), F32),
                   jax.ShapeDtypeStruct((nb, TAIL_ROWS, POOL_WIDTH), F32),
                   jax.ShapeDtypeStruct((nb, CONV_HDR, CONV_DIM), F32)],
        scratch_shapes=[pltpu.VMEM((POOL_HDR + MIX_ROWS, POOL_WIDTH), F32),
                        pltpu.VMEM((CONV_HDR + MIX_ROWS, CONV_DIM), F32),
                        pltpu.VMEM((SSD_BC, SSD_WIDTH), F32),
                        pltpu.VMEM((MIX_ROWS, SSD_WIDTH), F32), pltpu.VMEM((MIX_ROWS, LANES), F32),
                        pltpu.VMEM((MIX_ROWS, D_MODEL), BF16), pltpu.VMEM((MIX_ROWS, D_MODEL), BF16)],
        compiler_params=_cparams("arbitrary"),
        name="front_prompt",
    )(x, x, g_mix, w_in16, *_mixer_constants(), cw, cb, dtb, alog, dsk, nw, plw16, plb, psc, w_out16, g_ffn,
      r_hi, r_lo, r_bias)


def _mix_step_body(vp_ref, z_ref, xbc_ref, dt_ref, sp_ref, sc_ref, st_ref, cw_ref, cb_ref, dtb_ref, alog_ref,
                   dsk_ref, nw_ref, plw_ref, plb_ref, psc_ref, mix_ref, po_ref, co_ref, so_ref,
                   xdt_t, dec_t, b_t, c_t, xs_keep, y_t):
    h = pl.program_id(0)

    @pl.when(h == 0)
    def _():
        xbc = xbc_ref[...]
        conv = cb_ref[...] + sc_ref[0] * cw_ref[0:1, :]
        conv = conv + sc_ref[1] * cw_ref[1:2, :]
        conv = conv + sc_ref[2] * cw_ref[2:3, :]
        conv = conv + xbc * cw_ref[3:4, :]
        conv = _silu(conv)
        co_ref[0] = sc_ref[1]
        co_ref[1] = sc_ref[2]
        co_ref[2] = xbc
        xs = conv[:, 0:SSD_WIDTH]
        xs_keep[...] = xs
        b_t[...] = conv[:, SSD_WIDTH:SSD_WIDTH + SSD_BC].T
        c_t[...] = conv[:, SSD_WIDTH + SSD_BC:CONV_DIM].T
        dt = _softplus(dt_ref[...] + dtb_ref[...])
        d_a = dt * (-jnp.exp(alog_ref[...]))
        dt_t = dt.T
        dec_t[...] = jnp.exp(d_a).T
        xs_t = xs.T
        for k in range(SSD_HEADS):
            blk = slice(k * SSD_HEAD_DIM, (k + 1) * SSD_HEAD_DIM)
            xdt_t[blk, :] = xs_t[blk, :] * dt_t[k:k + 1, :]
        v = vp_ref[...]
        for k in range(POOL_BUF - 1):
            po_ref[k] = sp_ref[k + 1]
        po_ref[POOL_BUF - 1] = v
        for g, w in enumerate(POOL_WINDOWS):
            blk = slice(g * POOL_GROUP_DIM, (g + 1) * POOL_GROUP_DIM)
            acc = sp_ref[POOL_BUF - (w - 1), :, blk]
            for k in range(w - 2, 0, -1):
                acc = acc + sp_ref[POOL_BUF - k, :, blk]
            acc = acc + v[:, blk]
            m = acc / float(min(PAST_LEN + 1, w)) - v[:, blk]
            yg = _dot(m.astype(BF16), plw_ref[g]) + plb_ref[:, blk]
            mix_ref[:, blk] = (yg * psc_ref[:, blk]).astype(BF16)

    g_off = pl.multiple_of((h // (SSD_HEADS // SSD_GROUPS)) * SSD_STATE, SSD_STATE)
    h_off = pl.multiple_of(h * SSD_HEAD_DIM, SSD_HEAD_DIM)
    b_g = b_t[pl.ds(g_off, SSD_STATE), :]
    c_g = c_t[pl.ds(g_off, SSD_STATE), :]
    dec = dec_t[pl.ds(h, 1), :]
    xdt = xdt_t[pl.ds(h_off, SSD_HEAD_DIM), :]
    y_rows = []
    for p in range(SSD_HEAD_DIM):
        s_new = st_ref[0, p] * dec + xdt[p:p + 1, :] * b_g
        so_ref[0, p] = s_new
        y_rows.append(jnp.sum(s_new * c_g, axis=0, keepdims=True))
    y_t[pl.ds(h_off, SSD_HEAD_DIM), :] = jnp.concatenate(y_rows, axis=0)

    @pl.when(h == pl.num_programs(0) - 1)
    def _():
        xs = xs_keep[...]
        y = y_t[...].T + dsk_ref[...] * xs
        y = y * _silu(z_ref[...])
        width = SSD_WIDTH // SSD_GROUPS
        for g in range(SSD_GROUPS):
            blk = slice(g * width, (g + 1) * width)
            yg = y[:, blk]
            rs = lax.rsqrt(jnp.mean(yg * yg, axis=-1, keepdims=True) + EPS)
            mix_ref[:, POOL_WIDTH + g * width:POOL_WIDTH + (g + 1) * width] = (yg * rs * nw_ref[:, blk]).astype(BF16)


def _mix_step(vp, z, xbc, dt, pool_t, conv_t, ssm_t, cw, cb, dtb, alog, dsk, nw, plw16, plb, psc):
    n = vp.shape[0]
    fixed2 = lambda h: (0, 0)
    fixed3 = lambda h: (0, 0, 0)
    st_spec = pl.BlockSpec((1, SSD_HEAD_DIM, SSD_STATE, n), lambda h: (h, 0, 0, 0))
    return pl.pallas_call(
        _mix_step_body,
        grid=(SSD_HEADS,),
        in_specs=[pl.BlockSpec((n, POOL_WIDTH), fixed2), pl.BlockSpec((n, SSD_WIDTH), fixed2),
                  pl.BlockSpec((n, CONV_DIM), fixed2), pl.BlockSpec((n, LANES), fixed2),
                  pl.BlockSpec((POOL_BUF, n, POOL_WIDTH), fixed3),
                  pl.BlockSpec((SSD_CONV - 1, n, CONV_DIM), fixed3),
                  st_spec,
                  pl.BlockSpec((SSD_CONV, CONV_DIM), fixed2), pl.BlockSpec((1, CONV_DIM), fixed2),
                  pl.BlockSpec((1, LANES), fixed2), pl.BlockSpec((1, LANES), fixed2),
                  pl.BlockSpec((1, SSD_WIDTH), fixed2), pl.BlockSpec((1, SSD_WIDTH), fixed2),
                  pl.BlockSpec((len(POOL_WINDOWS), POOL_GROUP_DIM, POOL_GROUP_DIM), fixed3),
                  pl.BlockSpec((1, POOL_WIDTH), fixed2), pl.BlockSpec((1, POOL_WIDTH), fixed2)],
        out_specs=[pl.BlockSpec((n, D_MODEL), fixed2), pl.BlockSpec((POOL_BUF, n, POOL_WIDTH), fixed3),
                   pl.BlockSpec((SSD_CONV - 1, n, CONV_DIM), fixed3), st_spec],
        out_shape=[jax.ShapeDtypeStruct((n, D_MODEL), BF16), jax.ShapeDtypeStruct(pool_t.shape, F32),
                   jax.ShapeDtypeStruct(conv_t.shape, F32), jax.ShapeDtypeStruct(ssm_t.shape, F32)],
        scratch_shapes=[pltpu.VMEM((SSD_WIDTH, n), F32), pltpu.VMEM((LANES, n), F32),
                        pltpu.VMEM((SSD_BC, n), F32), pltpu.VMEM((SSD_BC, n), F32),
                        pltpu.VMEM((n, SSD_WIDTH), F32), pltpu.VMEM((SSD_WIDTH, n), F32)],
        compiler_params=_cparams("arbitrary"),
        name="mix_step",
    )(vp, z, xbc, dt, pool_t, conv_t, ssm_t, cw, cb, dtb, alog, dsk, nw, plw16, plb, psc)


PROMPT_TILE = 512


def _sort_tables(counts, tile_tokens, data_rows):
    cnt = jnp.concatenate(counts, axis=0)
    pc = (cnt + RUN_PAD - 1) // RUN_PAD * RUN_PAD
    off_local = jnp.cumsum(pc, axis=1) - pc
    tile_rows = jnp.sum(pc, axis=1)
    region = (jnp.sum(pc, axis=0) + MOE_TILE - 1) // MOE_TILE * MOE_TILE
    base = jnp.cumsum(region) - region
    dst = base[None, :] + jnp.cumsum(pc, axis=0) - pc
    per_pass = []
    lo = 0
    for c, tm in zip(counts, tile_tokens):
        hi = lo + c.shape[0]
        n_chunk = _sorted_rows_per_tile(tm) // RUN_PAD
        s = jnp.arange(n_chunk, dtype=jnp.int32) * RUN_PAD
        begins = off_local[lo:hi, None, :]
        ends = begins + pc[lo:hi, None, :]
        inside = (begins <= s[None, :, None]) & (s[None, :, None] < ends)
        shift = jnp.sum(jnp.where(inside, dst[lo:hi, None, :] - begins, 0), axis=2)
        valid = s[None, :] < tile_rows[lo:hi, None]
        parity = (jnp.arange(hi - lo, dtype=jnp.int32) % 2)[:, None]
        spare = data_rows + parity * _sorted_rows_per_tile(max(tile_tokens)) + s[None, :]
        scatter_dst = jnp.where(valid, s[None, :] + shift, spare).astype(jnp.int32)
        gather_src = jnp.where(valid, s[None, :] + shift, 0).astype(jnp.int32)
        off_v = jnp.broadcast_to(off_local[lo:hi, :, None].astype(F32), (hi - lo, N_EXPERTS, LANES))
        per_pass.append((scatter_dst, gather_src, off_v))
        lo = hi
    tiles_cum = jnp.cumsum(region // MOE_TILE)
    n_active = tiles_cum[-1]
    return per_pass, tiles_cum, n_active


def _max_sorted_rows(tile_counts, tile_tokens):
    rows = sum(n * (2 * tm + N_EXPERTS * (RUN_PAD - 1)) for n, tm in zip(tile_counts, tile_tokens))
    data_rows = (-(-rows // MOE_TILE) + N_EXPERTS) * MOE_TILE
    return data_rows, data_rows + 2 * _sorted_rows_per_tile(max(tile_tokens))


def _row(v):
    return v.reshape(1, -1).astype(F32)


def _pad_lanes(v):
    return jnp.pad(v.reshape(1, -1).astype(F32), ((0, 0), (0, LANES - v.size)))


def _state_from_blocks(st):
    n = st.shape[0]
    hpg = SSD_HEADS // SSD_GROUPS
    s6 = st.reshape(n, SSD_GROUPS, SSD_STATE, SSD_GROUPS, hpg, SSD_HEAD_DIM)
    per_group = [s6[:, g, :, g] for g in range(SSD_GROUPS)]
    s = jnp.stack(per_group, axis=1)
    return jnp.transpose(s, (0, 1, 3, 4, 2)).reshape(n, SSD_HEADS, SSD_HEAD_DIM, SSD_STATE)


def kernel(x_prompt, x_sample, p_prompt, p_sample, state_pool, state_conv, state_ssm, norm_mix, w_in, pool_lin_w, pool_lin_b, pool_scale, conv_w, conv_b, dt_bias, a_log, d_skip, ssd_norm, w_out, norm_ffn, router_grp_w, router_grp_b, router_exp_w, router_exp_b, exp_w_gate, exp_w_up, exp_w_down, norm_ple, ple_gate_w, ple_gate_b, ple_proj_w, norm_final):
    nb, seq, _ = x_prompt.shape
    ns = x_sample.shape[0]
    assert ns == LANES and x_sample.shape[1] == 1 and seq % MIX_ROWS == 0 and seq >= POOL_BUF

    w_in16 = jnp.pad(w_in[0], ((0, 0), (0, IN_COLS - w_in.shape[2]))).astype(BF16)
    w_out16 = w_out[0].astype(BF16)
    g_mix, g_ffn, g_ple, g_fin = _row(norm_mix[0]), _row(norm_ffn[0]), _row(norm_ple[0]), _row(norm_final)
    cw, cb = conv_w[0].astype(F32), _row(conv_b[0])
    dtb, alog = _pad_lanes(dt_bias[0]), _pad_lanes(a_log[0])
    dsk = _row(jnp.repeat(d_skip[0], SSD_HEAD_DIM))
    nw = _row(ssd_norm[0])
    plw16 = pool_lin_w[0].astype(BF16)
    plb, psc = _row(pool_lin_b[0]), _row(pool_scale[0])
    zeros4 = jnp.zeros((D_MODEL, 8 - N_EXPERT_GROUPS), F32)
    r_w = jnp.concatenate([router_grp_w[0], zeros4, router_exp_w[0],
                           jnp.zeros((D_MODEL, LANES - 8 - N_EXPERTS), F32)], axis=1).T
    r_hi = r_w.astype(BF16)
    r_lo = (r_w - r_hi.astype(F32)).astype(BF16)
    r_b = jnp.concatenate([router_grp_b[0], jnp.zeros((8 - N_EXPERT_GROUPS,), F32), router_exp_b[0],
                           jnp.zeros((LANES - 8 - N_EXPERTS,), F32)])
    wg = exp_w_gate[0].reshape(N_EXPERTS, D_MODEL, EXPERT_FF)
    wu = exp_w_up[0].reshape(N_EXPERTS, D_MODEL, EXPERT_FF)
    wd = exp_w_down[0].reshape(N_EXPERTS, EXPERT_FF, D_MODEL)
    pg16 = ple_gate_w[0].astype(BF16)
    pgb = _row(ple_gate_b[0])
    pp16 = ple_proj_w[0].astype(BF16)

    def route(x, mix16, tm):
        r_bias = jnp.broadcast_to(r_b[:, None], (LANES, tm))
        return _out_proj(x, mix16, w_out16, g_ffn, r_hi, r_lo, r_bias, tm)

    xp = x_prompt.reshape(nb * seq, D_MODEL)
    h1_p, u_p, rt_p, gate_p, cnt_p, st, pool_tail, conv_tail = _front_prompt(
        xp, g_mix, w_in16, cw, cb, dtb, alog, dsk, nw, plw16, plb, psc, w_out16, g_ffn, r_hi, r_lo,
        jnp.broadcast_to(r_b[:, None], (LANES, MIX_ROWS)), nb, seq)
    pool_p = pool_tail[:, TAIL_ROWS - POOL_BUF:]
    conv_p = conv_tail[:, CONV_HDR - (SSD_CONV - 1):]
    ssm_p = _state_from_blocks(st)

    xs_ = x_sample.reshape(ns, D_MODEL)
    vp_s, z_s, xbc_s, dt_s = _in_proj(xs_, g_mix, w_in16, ns)
    mix_s, pool_t, conv_t, ssm_t = _mix_step(
        vp_s, z_s, xbc_s, dt_s, jnp.transpose(state_pool[0], (1, 0, 2)), jnp.transpose(state_conv[0], (1, 0, 2)),
        jnp.transpose(state_ssm[0], (1, 2, 3, 0)), cw, cb, dtb, alog, dsk, nw, plw16, plb, psc)
    h1_s, u_s, rt_s, gate_s, cnt_s = route(xs_, mix_s, ns)
    pool_s = jnp.transpose(pool_t, (1, 0, 2))
    conv_s = jnp.transpose(conv_t, (1, 0, 2))
    ssm_s = jnp.transpose(ssm_t, (3, 0, 1, 2))

    counts = [cnt_p[:, :, 0].astype(jnp.int32), cnt_s[:, :, 0].astype(jnp.int32)]
    tiles = (PROMPT_TILE, ns)
    data_rows, total_rows = _max_sorted_rows([c.shape[0] for c in counts], tiles)
    (tab_p, tab_s), tiles_cum, n_active = _sort_tables(counts, tiles, data_rows)
    xs_sorted, dest_p = _sort_tokens(tab_p[0], u_p, rt_p, tab_p[2], None, PROMPT_TILE, total_rows)
    xs_sorted, dest_s = _sort_tokens(tab_s[0], u_s, rt_s, tab_s[2], xs_sorted, ns, total_rows)
    tile_start = jnp.concatenate([jnp.zeros((1,), jnp.int32), tiles_cum.astype(jnp.int32)])
    ys_sorted = _moe_sorted(tile_start, n_active.reshape(1).astype(jnp.int32), xs_sorted, wg, wu, wd)

    y_prompt = _ple(tab_p[1], h1_p, dest_p, gate_p, p_prompt[0].reshape(nb * seq, PLE_DIM),
                    g_ple, pg16, pgb, pp16, g_fin, ys_sorted, PROMPT_TILE)
    y_sample = _ple(tab_s[1], h1_s, dest_s, gate_s, p_sample[0].reshape(ns, PLE_DIM),
                    g_ple, pg16, pgb, pp16, g_fin, ys_sorted, ns)

    return (y_prompt.reshape(nb, seq, D_MODEL), y_sample.reshape(ns, 1, D_MODEL),
            pool_p[None], conv_p[None], ssm_p[None], pool_s[None], conv_s[None], ssm_s[None])
```

```python
import functools

import jax
import jax.numpy as jnp
import numpy as np
from jax import lax
from jax.experimental import pallas as pl
from jax.experimental.pallas import tpu as pltpu

F32 = jnp.float32
BF16 = jnp.bfloat16

D_MODEL = 1024
POOL_WIDTH = 512
POOL_WINDOWS = (2, 4, 8, 16)
POOL_GROUP_DIM = 128
POOL_BUF = 15
SSD_WIDTH = 512
SSD_HEAD_DIM = 64
SSD_HEADS = 8
SSD_GROUPS = 2
SSD_STATE = 64
SSD_CONV = 4
SSD_BC = SSD_GROUPS * SSD_STATE
CONV_DIM = SSD_WIDTH + 2 * SSD_BC
N_EXPERT_GROUPS = 4
EXPERTS_PER_GROUP = 8
N_EXPERTS = N_EXPERT_GROUPS * EXPERTS_PER_GROUP
EXPERT_FF = 256
PLE_DIM = 256
PAST_LEN = 16384
EPS = 1e-6

LANES = 128
IN_COLS = 1920
DT_OFF = POOL_WIDTH + SSD_WIDTH + CONV_DIM
VMEM_LIMIT = 56 * 1024 * 1024


def _cparams(*sem):
    return pltpu.CompilerParams(dimension_semantics=sem, vmem_limit_bytes=VMEM_LIMIT)


def _rms(x, g):
    return x * lax.rsqrt(jnp.mean(x * x, axis=-1, keepdims=True) + EPS) * g


def _sigmoid(x):
    return 1.0 / (1.0 + jnp.exp(-x))


def _silu(x):
    return x * _sigmoid(x)


def _split3(v):
    hi = v.astype(BF16)
    r = v - hi.astype(F32)
    mid = r.astype(BF16)
    lo = (r - mid.astype(F32)).astype(BF16)
    return hi, mid, lo


def _dot(a, b):
    return jnp.dot(a, b, preferred_element_type=F32)


def _dot_nt(a, b):
    return lax.dot_general(a, b, (((1,), (1,)), ((), ())), preferred_element_type=F32)


def _dot_tn(a, b):
    return lax.dot_general(a, b, (((0,), (0,)), ((), ())), preferred_element_type=F32)


def _dot_exact_lhs(sel, v, terms=3):
    acc = None
    for t in _split3(v)[:terms]:
        p = _dot(sel, t)
        acc = p if acc is None else acc + p
    return acc


def _dot_exact_rhs(v, sel, terms=3):
    acc = None
    for t in _split3(v)[:terms]:
        p = _dot(t, sel)
        acc = p if acc is None else acc + p
    return acc


def _in_proj_body(x_ref, g_ref, w_ref, vp_ref, z_ref, xbc_ref, dt_ref):
    a16 = _rms(x_ref[...], g_ref[...]).astype(BF16)
    vp_ref[...] = _dot(a16, w_ref[:, 0:POOL_WIDTH])
    z_ref[...] = _dot(a16, w_ref[:, POOL_WIDTH:POOL_WIDTH + SSD_WIDTH])
    xbc_ref[...] = _dot(a16, w_ref[:, POOL_WIDTH + SSD_WIDTH:DT_OFF])
    dt_ref[...] = _dot(a16, w_ref[:, DT_OFF:IN_COLS])


def _in_proj(x, g, w16, tm):
    t = x.shape[0]
    row = lambda i: (i, 0)
    fixed = lambda i: (0, 0)
    return pl.pallas_call(
        _in_proj_body,
        grid=(t // tm,),
        in_specs=[pl.BlockSpec((tm, D_MODEL), row), pl.BlockSpec((1, D_MODEL), fixed),
                  pl.BlockSpec((D_MODEL, IN_COLS), fixed)],
        out_specs=[pl.BlockSpec((tm, POOL_WIDTH), row), pl.BlockSpec((tm, SSD_WIDTH), row),
                   pl.BlockSpec((tm, CONV_DIM), row), pl.BlockSpec((tm, LANES), row)],
        out_shape=[jax.ShapeDtypeStruct((t, POOL_WIDTH), F32), jax.ShapeDtypeStruct((t, SSD_WIDTH), F32),
                   jax.ShapeDtypeStruct((t, CONV_DIM), F32), jax.ShapeDtypeStruct((t, LANES), F32)],
        compiler_params=_cparams("parallel"),
        name="in_proj",
    )(x, g, w16)


def _route(lg):
    tm = lg.shape[1]
    gl = lg[0:N_EXPERT_GROUPS, :]
    gmax = jnp.max(gl, axis=0, keepdims=True)
    gsum = jnp.sum(jnp.exp(gl - gmax), axis=0, keepdims=True)
    g_w = 1.0 / gsum
    gi = lax.broadcasted_iota(jnp.int32, gl.shape, 0)
    g_idx = jnp.min(jnp.where(gl == gmax, gi, N_EXPERT_GROUPS), axis=0, keepdims=True)
    sel = jnp.zeros((EXPERTS_PER_GROUP, tm), F32)
    for g in range(N_EXPERT_GROUPS):
        blk = lg[8 + g * EXPERTS_PER_GROUP:8 + (g + 1) * EXPERTS_PER_GROUP, :]
        sel = jnp.where(g_idx == g, blk, sel)
    ei = lax.broadcasted_iota(jnp.int32, sel.shape, 0)
    m1 = jnp.max(sel, axis=0, keepdims=True)
    i1 = jnp.min(jnp.where(sel == m1, ei, EXPERTS_PER_GROUP), axis=0, keepdims=True)
    rest = jnp.where(ei == i1, -jnp.inf, sel)
    m2 = jnp.max(rest, axis=0, keepdims=True)
    i2 = jnp.min(jnp.where(rest == m2, ei, EXPERTS_PER_GROUP), axis=0, keepdims=True)
    p2 = jnp.exp(m2 - m1)
    w1 = g_w / (1.0 + p2)
    w2 = g_w * p2 / (1.0 + p2)
    return g_idx * EXPERTS_PER_GROUP + i1, g_idx * EXPERTS_PER_GROUP + i2, w1, w2


def _out_proj_body(x_ref, mix_ref, w_ref, g_ref, rh_ref, rl_ref, rb_ref, h_ref, u_ref, rt_ref, wt_ref, cnt_ref):
    _out_proj_and_route(x_ref[...], mix_ref[...], w_ref, g_ref, rh_ref, rl_ref, rb_ref,
                        h_ref, u_ref, rt_ref, wt_ref, cnt_ref)


def _out_proj_and_route(x, mix16, w_ref, g_ref, rh_ref, rl_ref, rb_ref, h_ref, u_ref, rt_ref, wt_ref, cnt_ref,
                        r0=0, accumulate=False):
    tm = x.shape[0]
    h = x + _dot(mix16, w_ref[...])
    h_ref[r0:r0 + tm, :] = h
    _route_rows(h, g_ref, rh_ref, rl_ref, rb_ref, u_ref, rt_ref, wt_ref, cnt_ref, r0=r0, accumulate=accumulate)


def _route_rows(h, g_ref, rh_ref, rl_ref, rb_ref, u_ref, rt_ref, wt_ref, cnt_ref, r0=0, accumulate=False):
    tm = h.shape[0]
    u = _rms(h, g_ref[...])
    u_hi = u.astype(BF16)
    u_ref[r0:r0 + tm, :] = u_hi
    u_lo = (u - u_hi.astype(F32)).astype(BF16)
    lg = (_dot_nt(rh_ref[...], u_hi) + _dot_nt(rh_ref[...], u_lo) + _dot_nt(rl_ref[...], u_hi)
          + rb_ref[:, r0:r0 + tm])
    b1, b2, w1, w2 = _route(lg)
    r8 = lax.broadcasted_iota(jnp.int32, (8, tm), 0)
    rt_ref[0, :, r0:r0 + tm] = jnp.where(r8 == 0, b1.astype(F32), jnp.where(r8 == 1, b2.astype(F32), 0.0))
    wt_ref[r0:r0 + tm, :] = jnp.concatenate(
        [jnp.broadcast_to(w1, (LANES, tm)).T, jnp.broadcast_to(w2, (LANES, tm)).T], axis=1)
    kio = lax.broadcasted_iota(jnp.int32, (N_EXPERTS, tm), 0)
    hits = ((kio == b1) | (kio == b2)).astype(F32)
    cnt = jnp.broadcast_to(jnp.sum(hits, axis=1, keepdims=True), (N_EXPERTS, LANES))
    cnt_ref[0] = cnt_ref[0] + cnt if accumulate else cnt


def _out_proj(x, mix16, w16, g, r_hi, r_lo, r_bias, tm):
    t = x.shape[0]
    nt = t // tm
    row = lambda i: (i, 0)
    fixed = lambda i: (0, 0)
    tile3 = lambda i: (i, 0, 0)
    return pl.pallas_call(
        _out_proj_body,
        grid=(nt,),
        in_specs=[pl.BlockSpec((tm, D_MODEL), row), pl.BlockSpec((tm, D_MODEL), row),
                  pl.BlockSpec((D_MODEL, D_MODEL), fixed), pl.BlockSpec((1, D_MODEL), fixed),
                  pl.BlockSpec((LANES, D_MODEL), fixed), pl.BlockSpec((LANES, D_MODEL), fixed),
                  pl.BlockSpec((LANES, tm), fixed)],
        out_specs=[pl.BlockSpec((tm, D_MODEL), row), pl.BlockSpec((tm, D_MODEL), row),
                   pl.BlockSpec((1, 8, tm), tile3), pl.BlockSpec((tm, 2 * LANES), row),
                   pl.BlockSpec((1, N_EXPERTS, LANES), tile3)],
        out_shape=[jax.ShapeDtypeStruct((t, D_MODEL), F32), jax.ShapeDtypeStruct((t, D_MODEL), BF16),
                   jax.ShapeDtypeStruct((nt, 8, tm), F32), jax.ShapeDtypeStruct((t, 2 * LANES), F32),
                   jax.ShapeDtypeStruct((nt, N_EXPERTS, LANES), F32)],
        compiler_params=_cparams("parallel"),
        name="out_proj_router",
    )(x, mix16, w16, g, r_hi, r_lo, r_bias)


RUN_PAD = 16
MOE_TILE = 256
MOE_BUFS = 6
MOE_AHEAD = 4


def _tile_lanes(v, width):
    reps = width // LANES
    return v if reps == 1 else jnp.concatenate([v] * reps, axis=1)


def _sorted_rows_per_tile(tm):
    need = 2 * tm + N_EXPERTS * (RUN_PAD - 1)
    return -(-need // LANES) * LANES


def _sort_body(cd_ref, u_ref, rt_ref, off_ref, *rest):
    xs_ref, dt_ref, loc, sem = rest[-4:]
    i = pl.program_id(0)
    tm = u_ref.shape[0]
    rows = loc.shape[1]
    rt = rt_ref[0]
    b1 = rt[0:1, :].astype(jnp.int32)
    b2 = rt[1:2, :].astype(jnp.int32)
    kio = lax.broadcasted_iota(jnp.int32, (N_EXPERTS, tm), 0)
    o1 = kio == b1
    o2 = kio == b2
    before = (lax.broadcasted_iota(jnp.int32, (tm, tm), 0) < lax.broadcasted_iota(jnp.int32, (tm, tm), 1))
    start = _dot((o1 | o2).astype(BF16), before.astype(BF16)) + _tile_lanes(off_ref[0], tm)
    d1 = jnp.sum(jnp.where(o1, start, 0.0), axis=0, keepdims=True)
    d2 = jnp.sum(jnp.where(o2, start, 0.0), axis=0, keepdims=True)
    dt_ref[...] = jnp.concatenate([jnp.broadcast_to(d1, (LANES, tm)).T, jnp.broadcast_to(d2, (LANES, tm)).T], axis=1)
    rio = lax.broadcasted_iota(jnp.int32, (rows, tm), 0)
    perm = ((rio == d1.astype(jnp.int32)) | (rio == d2.astype(jnp.int32))).astype(BF16)
    slot = i % 2
    loc[slot] = _dot(perm, u_ref[...]).astype(BF16)

    def chunk_copy(step, sl, c):
        return pltpu.make_async_copy(
            loc.at[sl, pl.ds(c * RUN_PAD, RUN_PAD), :],
            xs_ref.at[pl.ds(pl.multiple_of(cd_ref[step, c], RUN_PAD), RUN_PAD), :], sem.at[sl])

    n_chunk = rows // RUN_PAD
    for c in range(n_chunk):
        chunk_copy(i, slot, c).start()

    @pl.when(i > 0)
    def _():
        for c in range(n_chunk):
            chunk_copy(i - 1, 1 - slot, c).wait()

    @pl.when(i == pl.num_programs(0) - 1)
    def _():
        for c in range(n_chunk):
            chunk_copy(i, slot, c).wait()


def _sort_tokens(chunk_dst, u16, rt, off_v, xs_prev, tm, total_rows):
    t = u16.shape[0]
    rows = _sorted_rows_per_tile(tm)
    in_specs = [pl.BlockSpec((tm, D_MODEL), lambda i, cd: (i, 0)),
                pl.BlockSpec((1, 8, tm), lambda i, cd: (i, 0, 0)),
                pl.BlockSpec((1, N_EXPERTS, LANES), lambda i, cd: (i, 0, 0))]
    args = [chunk_dst, u16, rt, off_v]
    aliases = {}
    if xs_prev is not None:
        in_specs.append(pl.BlockSpec(memory_space=pl.ANY))
        aliases = {len(args): 0}
        args.append(xs_prev)
    return pl.pallas_call(
        _sort_body,
        grid_spec=pltpu.PrefetchScalarGridSpec(
            num_scalar_prefetch=1,
            grid=(t // tm,),
            in_specs=in_specs,
            out_specs=[pl.BlockSpec(memory_space=pl.ANY),
                       pl.BlockSpec((tm, 2 * LANES), lambda i, cd: (i, 0))],
            scratch_shapes=[pltpu.VMEM((2, rows, D_MODEL), BF16), pltpu.SemaphoreType.DMA((2,))]),
        out_shape=[jax.ShapeDtypeStruct((total_rows, D_MODEL), BF16),
                   jax.ShapeDtypeStruct((t, 2 * LANES), F32)],
        input_output_aliases=aliases,
        compiler_params=_cparams("arbitrary"),
        name="sort_tokens",
    )(*args)


def _moe_body(ts_ref, na_ref, xs_ref, wg_ref, wu_ref, wd_ref, ys_ref, xbuf, ybuf, wg16, wu16, wd16, sem_in, sem_out):
    k = pl.program_id(0)
    n_act = na_ref[0]

    def x_copy(g, sl):
        return pltpu.make_async_copy(xs_ref.at[pl.ds(pl.multiple_of(g * MOE_TILE, MOE_TILE), MOE_TILE), :],
                                     xbuf.at[sl], sem_in.at[sl])

    def y_copy(g, sl):
        return pltpu.make_async_copy(ybuf.at[sl],
                                     ys_ref.at[pl.ds(pl.multiple_of(g * MOE_TILE, MOE_TILE), MOE_TILE), :],
                                     sem_out.at[sl])

    @pl.when(k == 0)
    def _():
        for j in range(MOE_AHEAD):
            @pl.when(j < n_act)
            def _():
                x_copy(j, j).start(priority=1)

    wg16[...] = wg_ref[0].astype(BF16)
    wu16[...] = wu_ref[0].astype(BF16)
    wd16[...] = wd_ref[0].astype(BF16)

    def arrive(g):
        sl = g % MOE_BUFS

        @pl.when(g + MOE_AHEAD < n_act)
        def _():
            x_copy(g + MOE_AHEAD, (g + MOE_AHEAD) % MOE_BUFS).start(priority=1)

        x_copy(g, sl).wait()

        @pl.when(g >= MOE_BUFS)
        def _():
            y_copy(g - MOE_BUFS, sl).wait()

    def tiles(g, count):
        for j in range(count):
            arrive(g + j)
        acts = []
        for j in range(count):
            x = xbuf[(g + j) % MOE_BUFS]
            acts.append((_silu(_dot(x, wg16[...])) * _dot(x, wu16[...])).astype(BF16))
        for j in range(count):
            ybuf[(g + j) % MOE_BUFS] = _dot(acts[j], wd16[...]).astype(BF16)
        for j in range(count):
            y_copy(g + j, (g + j) % MOE_BUFS).start()

    g0 = ts_ref[k]
    n_here = ts_ref[k + 1] - g0

    def pair(p, carry):
        tiles(g0 + 2 * p, 2)
        return carry

    lax.fori_loop(0, n_here // 2, pair, 0)

    @pl.when(n_here % 2 == 1)
    def _():
        tiles(g0 + n_here - 1, 1)

    @pl.when(k == pl.num_programs(0) - 1)
    def _():
        for j in range(1, MOE_BUFS + 1):
            @pl.when(n_act >= j)
            def _():
                y_copy(n_act - j, (n_act - j) % MOE_BUFS).wait()


def _moe_sorted(tile_start, n_active, xs, wg, wu, wd):
    w_map = lambda k, ts, na: (k, 0, 0)
    return pl.pallas_call(
        _moe_body,
        grid_spec=pltpu.PrefetchScalarGridSpec(
            num_scalar_prefetch=2,
            grid=(N_EXPERTS,),
            in_specs=[pl.BlockSpec(memory_space=pl.ANY),
                      pl.BlockSpec((1, D_MODEL, EXPERT_FF), w_map),
                      pl.BlockSpec((1, D_MODEL, EXPERT_FF), w_map),
                      pl.BlockSpec((1, EXPERT_FF, D_MODEL), w_map)],
            out_specs=pl.BlockSpec(memory_space=pl.ANY),
            scratch_shapes=[pltpu.VMEM((MOE_BUFS, MOE_TILE, D_MODEL), BF16),
                            pltpu.VMEM((MOE_BUFS, MOE_TILE, D_MODEL), BF16),
                            pltpu.VMEM((D_MODEL, EXPERT_FF), BF16), pltpu.VMEM((D_MODEL, EXPERT_FF), BF16),
                            pltpu.VMEM((EXPERT_FF, D_MODEL), BF16),
                            pltpu.SemaphoreType.DMA((MOE_BUFS,)), pltpu.SemaphoreType.DMA((MOE_BUFS,))]),
        out_shape=jax.ShapeDtypeStruct(xs.shape, BF16),
        compiler_params=_cparams("arbitrary"),
        name="moe_sorted",
    )(tile_start, n_active, xs, wg, wu, wd)


def _ple_body(cd_ref, h_ref, dt_ref, wt_ref, p_ref, gn_ref, wg_ref, bg_ref, wp_ref, fn_ref, ys_ref,
              o_ref, loc, sem):
    i = pl.program_id(0)
    tm = h_ref.shape[0]
    rows = loc.shape[1]
    n_chunk = rows // RUN_PAD
    slot = i % 2

    def chunk_copy(step, sl, c):
        return pltpu.make_async_copy(
            ys_ref.at[pl.ds(pl.multiple_of(cd_ref[step, c], RUN_PAD), RUN_PAD), :],
            loc.at[sl, pl.ds(c * RUN_PAD, RUN_PAD), :], sem.at[sl])

    def fetch(step, sl):
        for c in range(n_chunk):
            chunk_copy(step, sl, c).start()

    @pl.when(i == 0)
    def _():
        fetch(0, 0)

    @pl.when(i + 1 < pl.num_programs(0))
    def _():
        fetch(i + 1, 1 - slot)

    for c in range(n_chunk):
        chunk_copy(i, slot, c).wait()

    ci = lax.broadcasted_iota(jnp.int32, (tm, LANES), 1)
    d1 = dt_ref[:, 0:LANES].astype(jnp.int32)
    d2 = dt_ref[:, LANES:2 * LANES].astype(jnp.int32)
    w1 = wt_ref[:, 0:LANES]
    w2 = wt_ref[:, LANES:2 * LANES]
    sel = jnp.concatenate(
        [jnp.where(d1 == ci + m * LANES, w1, jnp.where(d2 == ci + m * LANES, w2, 0.0)) for m in range(rows // LANES)],
        axis=1).astype(BF16)
    h = h_ref[...] + _dot(sel, loc[slot])
    a16 = _rms(h, gn_ref[...]).astype(BF16)
    gate = _sigmoid(_dot(a16, wg_ref[...]) + bg_ref[...])
    pp = _dot(p_ref[...].astype(BF16), wp_ref[...])
    h = h + gate * pp
    o_ref[...] = _rms(h, fn_ref[...])


def _ple(chunk_src, h1, dest_t, gate_t, p, g_ple, wg16, bg, wp16, g_final, ys, tm):
    t = h1.shape[0]
    row = lambda i, cd: (i, 0)
    fixed = lambda i, cd: (0, 0)
    return pl.pallas_call(
        _ple_body,
        grid_spec=pltpu.PrefetchScalarGridSpec(
            num_scalar_prefetch=1,
            grid=(t // tm,),
            in_specs=[pl.BlockSpec((tm, D_MODEL), row), pl.BlockSpec((tm, 2 * LANES), row),
                      pl.BlockSpec((tm, 2 * LANES), row), pl.BlockSpec((tm, PLE_DIM), row),
                      pl.BlockSpec((1, D_MODEL), fixed), pl.BlockSpec((D_MODEL, D_MODEL), fixed),
                      pl.BlockSpec((1, D_MODEL), fixed), pl.BlockSpec((PLE_DIM, D_MODEL), fixed),
                      pl.BlockSpec((1, D_MODEL), fixed), pl.BlockSpec(memory_space=pl.ANY)],
            out_specs=pl.BlockSpec((tm, D_MODEL), row),
            scratch_shapes=[pltpu.VMEM((2, _sorted_rows_per_tile(tm), D_MODEL), BF16),
                            pltpu.SemaphoreType.DMA((2,))]),
        out_shape=jax.ShapeDtypeStruct((t, D_MODEL), F32),
        compiler_params=_cparams("arbitrary"),
        name="ple_final",
    )(chunk_src, h1, dest_t, gate_t, p, g_ple, wg16, bg, wp16, g_final, ys)


MIX_ROWS = 512
SSD_Q = 128
POOL_HDR = 128
CONV_HDR = 8
TAIL_ROWS = 16


def _softplus(x):
    return jnp.maximum(x, 0.0) + jnp.log1p(jnp.exp(-jnp.abs(x)))


def _mixer_constants():
    q = SSD_Q
    li = np.arange(q)[:, None]
    tri = (li >= np.arange(q)[None, :])
    expand = (np.arange(SSD_WIDTH)[None, :] // SSD_HEAD_DIM) == np.arange(LANES)[:, None]
    hpg = SSD_HEADS // SSD_GROUPS
    block = (np.arange(SSD_BC)[:, None] // SSD_STATE) == (np.arange(SSD_WIDTH)[None, :] // (SSD_HEAD_DIM * hpg))
    wj = np.arange(2 * q)[None, :]
    band = np.stack([(wj <= li + POOL_HDR) & (wj > li + POOL_HDR - w) for w in POOL_WINDOWS])
    as_f32 = lambda m: jnp.asarray(m.astype(np.float32))
    return (as_f32(tri), as_f32(tri).astype(BF16), as_f32(expand).astype(BF16), as_f32(block),
            as_f32(band).astype(BF16))


def _front_body(steps_per_seq, x_ref, gm_ref, win_ref, trif_ref, tri_ref, exp_ref, bmask_ref, band_ref,
                cw_ref, cb_ref, dtb_ref, alog_ref, dsk_ref, nw_ref,
                plw_ref, plb_ref, psc_ref, wout_ref, gf_ref, rh_ref, rl_ref, rb_ref,
                h_ref, u_ref, rt_ref, wt_ref, cnt_ref, st_ref, ptail_ref, ctail_ref,
                pool_ext, conv_ext, s_ref, z_ref, dt_ref, mix_ref, mixp_ref, xp_ref):
    s = pl.program_id(0)
    n_tiles = pl.num_programs(0) - 1
    real = s < n_tiles
    c = jnp.minimum(s, n_tiles - 1) % steps_per_seq
    rows = MIX_ROWS
    q_len = SSD_Q

    @pl.when(s == 0)
    def _():
        mixp_ref[...] = jnp.zeros_like(mixp_ref)
        xp_ref[...] = jnp.zeros_like(xp_ref)

    @pl.when(c == 0)
    def _():
        pool_ext[0:POOL_HDR, :] = jnp.zeros((POOL_HDR, POOL_WIDTH), F32)
        conv_ext[0:CONV_HDR, :] = jnp.zeros((CONV_HDR, CONV_DIM), F32)
        s_ref[...] = jnp.zeros_like(s_ref)

    @pl.when(c > 0)
    def _():
        pool_ext[0:POOL_HDR, :] = pool_ext[rows:rows + POOL_HDR, :]
        conv_ext[0:CONV_HDR, :] = conv_ext[rows:rows + CONV_HDR, :]

    causal = trif_ref[...] > 0.5
    tri = tri_ref[...]
    expand = exp_ref[...]
    blockmask = bmask_ref[...]
    lane = lax.broadcasted_iota(jnp.int32, (q_len, LANES), 1)
    left = lane < SSD_HEAD_DIM
    a_neg = -jnp.exp(alog_ref[...])
    rowi = lax.broadcasted_iota(jnp.int32, (q_len, LANES), 0)

    half = rows // 2
    n_cols = 256

    def in_proj_chunks(r0):
        a16 = _rms(x_ref[r0:r0 + half, :], gm_ref[...]).astype(BF16)
        dests = ([(pool_ext, POOL_HDR + r0, k, 0) for k in range(0, POOL_WIDTH, n_cols)]
                 + [(z_ref, r0, k, POOL_WIDTH) for k in range(0, SSD_WIDTH, n_cols)]
                 + [(conv_ext, CONV_HDR + r0, k, POOL_WIDTH + SSD_WIDTH) for k in range(0, CONV_DIM, n_cols)])

        def chunk(ref, row, k, col0):
            def run():
                ref[row:row + half, k:k + n_cols] = _dot(a16, win_ref[:, col0 + k:col0 + k + n_cols])
            return run

        def dt_chunk():
            dt_ref[r0:r0 + half, :] = _dot(a16, win_ref[:, DT_OFF:IN_COLS])

        return [chunk(*d) for d in dests] + [dt_chunk]

    def out_proj_chunks(r0):
        def chunk(k):
            def run():
                h_ref[r0:r0 + half, k:k + n_cols] = (xp_ref[r0:r0 + half, k:k + n_cols]
                                                    + _dot(mixp_ref[r0:r0 + half, :], wout_ref[:, k:k + n_cols]))
            return run

        def route():
            _route_rows(h_ref[r0:r0 + half, :], gf_ref, rh_ref, rl_ref, rb_ref, u_ref, rt_ref, wt_ref, cnt_ref,
                        r0=r0, accumulate=r0 > 0)

        return [chunk(k) for k in range(0, D_MODEL, n_cols)] + [route]

    def mix_pair(q_first, fillers):
        fillers = list(fillers)

        def fill():
            if fillers:
                fillers.pop(0)()

        subs = range(q_first, q_first + 2)
        xs_l, b16_l, call_l, c16_l = {}, {}, {}, {}
        for q in subs:
            base = CONV_HDR + q * q_len
            conv = cb_ref[...] + conv_ext[base - 3:base - 3 + q_len, :] * cw_ref[0:1, :]
            conv = conv + conv_ext[base - 2:base - 2 + q_len, :] * cw_ref[1:2, :]
            conv = conv + conv_ext[base - 1:base - 1 + q_len, :] * cw_ref[2:3, :]
            conv = conv + conv_ext[base:base + q_len, :] * cw_ref[3:4, :]
            conv = _silu(conv)
            xs_l[q] = conv[:, 0:SSD_WIDTH]
            b16_l[q] = conv[:, SSD_WIDTH:SSD_WIDTH + SSD_BC].astype(BF16)
            call_l[q] = conv[:, SSD_WIDTH + SSD_BC:CONV_DIM]
            c16_l[q] = call_l[q].astype(BF16)
            fill()
        dt_l = {q: _softplus(dt_ref[q * q_len:(q + 1) * q_len, :] + dtb_ref[...]) for q in subs}
        acs_l = {q: _dot_exact_lhs(tri, dt_l[q] * a_neg) for q in subs}
        acst_l = {q: acs_l[q].T for q in subs}
        fill()
        dtx_l = {q: _dot_exact_rhs(dt_l[q], expand, terms=2) for q in subs}
        acsx_l = {q: _dot_exact_rhs(acs_l[q], expand) for q in subs}
        lastx_l = {q: acsx_l[q][q_len - 1:q_len, :] for q in subs}
        xdt_l = {q: xs_l[q] * dtx_l[q] for q in subs}
        xdt16_l = {q: xdt_l[q].astype(BF16) for q in subs}
        fill()
        contrib_l = {q: _dot_tn(b16_l[q], (xdt_l[q] * jnp.exp(lastx_l[q] - acsx_l[q])).astype(BF16)) * blockmask
                     for q in subs}
        cb_l = {q: [_dot_nt(jnp.where(lax.shift_right_logical(lane, 6) == g, call_l[q], 0.0).astype(BF16), b16_l[q])
                    for g in range(SSD_GROUPS)] for q in subs}
        fill()
        zs_l = {}
        for q in subs:
            zs = []
            for h in range(SSD_HEADS):
                blk = slice((h // 2) * LANES, (h // 2 + 1) * LANES)
                seg = acs_l[q][:, h:h + 1] - acst_l[q][h:h + 1, :]
                decay = jnp.where(causal, jnp.exp(seg), 0.0)
                scores = (cb_l[q][h // (SSD_HEADS // SSD_GROUPS)] * decay).astype(BF16)
                zs.append(_dot(scores, xdt16_l[q][:, blk]))
                if h % 4 == 3:
                    fill()
            zs_l[q] = zs
        yoff_l = {}
        for q in subs:
            s_old = s_ref[...]
            yoff_l[q] = _dot(c16_l[q], s_old.astype(BF16)) * jnp.exp(acsx_l[q])
            s_ref[...] = s_old * jnp.exp(lastx_l[q]) + contrib_l[q]
        fill()
        for q in subs:
            r0 = q * q_len
            z = z_ref[r0:r0 + q_len, :]
            y_blocks = []
            for j in range(SSD_HEADS // 2):
                blk = slice(j * LANES, (j + 1) * LANES)
                y = jnp.where(left, zs_l[q][2 * j], zs_l[q][2 * j + 1]) + yoff_l[q][:, blk]
                y = y + dsk_ref[:, blk] * xs_l[q][:, blk]
                y_blocks.append(y * _silu(z[:, blk]))
            for g in range(SSD_GROUPS):
                y0, y1 = y_blocks[2 * g], y_blocks[2 * g + 1]
                ss = jnp.sum(y0 * y0, axis=-1, keepdims=True) + jnp.sum(y1 * y1, axis=-1, keepdims=True)
                rs = lax.rsqrt(ss * (1.0 / (2 * LANES)) + EPS)
                for k, yk in ((2 * g, y0), (2 * g + 1, y1)):
                    blk = slice(k * LANES, (k + 1) * LANES)
                    out = yk * rs * nw_ref[:, blk]
                    mix_ref[r0:r0 + q_len, POOL_WIDTH + k * LANES:POOL_WIDTH + (k + 1) * LANES] = out.astype(BF16)
            fill()
        for q in subs:
            r0 = q * q_len
            pos = c * rows + r0 + rowi
            for g, w in enumerate(POOL_WINDOWS):
                blk = slice(g * POOL_GROUP_DIM, (g + 1) * POOL_GROUP_DIM)
                pe = pool_ext[r0:r0 + 2 * q_len, blk]
                winsum = _dot_exact_lhs(band_ref[g], pe, terms=2)
                cnt = jnp.minimum(pos + 1, w).astype(F32)
                m = winsum / cnt - pe[q_len:2 * q_len, :]
                yg = _dot(m.astype(BF16), plw_ref[g]) + plb_ref[:, blk]
                mix_ref[r0:r0 + q_len, blk] = (yg * psc_ref[:, blk]).astype(BF16)
            fill()
        while fillers:
            fill()

    @pl.when(real)
    def _():
        for run in in_proj_chunks(0):
            run()
        mix_pair(0, in_proj_chunks(half) + out_proj_chunks(0))
        mix_pair(2, out_proj_chunks(half))
        mixp_ref[...] = mix_ref[...]
        xp_ref[...] = x_ref[...]

    @pl.when(jnp.logical_not(real))
    def _():
        for run in out_proj_chunks(0) + out_proj_chunks(half):
            run()

    @pl.when(real & (c == steps_per_seq - 1))
    def _():
        st_ref[0] = s_ref[...]
        ptail_ref[0] = pool_ext[POOL_HDR + rows - TAIL_ROWS:POOL_HDR + rows, :]
        ctail_ref[0] = conv_ext[rows:rows + CONV_HDR, :]


def _front_prompt(x, g_mix, w_in16, cw, cb, dtb, alog, dsk, nw, plw16, plb, psc, w_out16, g_ffn, r_hi, r_lo, r_bias,
                  nb, seq):
    steps = seq // MIX_ROWS
    nt = nb * steps
    cur = lambda s: (jnp.minimum(s, nt - 1), 0)
    prev = lambda s: (jnp.maximum(s - 1, 0), 0)
    prev3 = lambda s: (jnp.maximum(s - 1, 0), 0, 0)
    per_seq = lambda s: (jnp.minimum(s, nt - 1) // steps, 0, 0)
    fixed2 = lambda s: (0, 0)
    fixed3 = lambda s: (0, 0, 0)
    return pl.pallas_call(
        functools.partial(_front_body, steps),
        grid=(nt + 1,),
        in_specs=[pl.BlockSpec((MIX_ROWS, D_MODEL), cur),
                  pl.BlockSpec((1, D_MODEL), fixed2), pl.BlockSpec((D_MODEL, IN_COLS), fixed2),
                  pl.BlockSpec((SSD_Q, SSD_Q), fixed2), pl.BlockSpec((SSD_Q, SSD_Q), fixed2),
                  pl.BlockSpec((LANES, SSD_WIDTH), fixed2), pl.BlockSpec((SSD_BC, SSD_WIDTH), fixed2),
                  pl.BlockSpec((len(POOL_WINDOWS), SSD_Q, 2 * SSD_Q), fixed3),
                  pl.BlockSpec((SSD_CONV, CONV_DIM), fixed2), pl.BlockSpec((1, CONV_DIM), fixed2),
                  pl.BlockSpec((1, LANES), fixed2), pl.BlockSpec((1, LANES), fixed2),
                  pl.BlockSpec((1, SSD_WIDTH), fixed2), pl.BlockSpec((1, SSD_WIDTH), fixed2),
                  pl.BlockSpec((len(POOL_WINDOWS), POOL_GROUP_DIM, POOL_GROUP_DIM), fixed3),
                  pl.BlockSpec((1, POOL_WIDTH), fixed2), pl.BlockSpec((1, POOL_WIDTH), fixed2),
                  pl.BlockSpec((D_MODEL, D_MODEL), fixed2), pl.BlockSpec((1, D_MODEL), fixed2),
                  pl.BlockSpec((LANES, D_MODEL), fixed2), pl.BlockSpec((LANES, D_MODEL), fixed2),
                  pl.BlockSpec((LANES, MIX_ROWS), fixed2)],
        out_specs=[pl.BlockSpec((MIX_ROWS, D_MODEL), prev), pl.BlockSpec((MIX_ROWS, D_MODEL), prev),
                   pl.BlockSpec((1, 8, MIX_ROWS), prev3), pl.BlockSpec((MIX_ROWS, 2 * LANES), prev),
                   pl.BlockSpec((1, N_EXPERTS, LANES), prev3),
                   pl.BlockSpec((1, SSD_BC, SSD_WIDTH), per_seq),
                   pl.BlockSpec((1, TAIL_ROWS, POOL_WIDTH), per_seq),
                   pl.BlockSpec((1, CONV_HDR, CONV_DIM), per_seq)],
        out_shape=[jax.ShapeDtypeStruct((nb * seq, D_MODEL), F32), jax.ShapeDtypeStruct((nb * seq, D_MODEL), BF16),
                   jax.ShapeDtypeStruct((nt, 8, MIX_ROWS), F32), jax.ShapeDtypeStruct((nb * seq, 2 * LANES), F32),
                   jax.ShapeDtypeStruct((nt, N_EXPERTS, LANES), F32),
                   jax.ShapeDtypeStruct((nb, SSD_BC, SSD_WIDTH), F32),
                   jax.ShapeDtypeStruct((nb, TAIL_ROWS, POOL_WIDTH), F32),
                   jax.ShapeDtypeStruct((nb, CONV_HDR, CONV_DIM), F32)],
        scratch_shapes=[pltpu.VMEM((POOL_HDR + MIX_ROWS, POOL_WIDTH), F32),
                        pltpu.VMEM((CONV_HDR + MIX_ROWS, CONV_DIM), F32),
                        pltpu.VMEM((SSD_BC, SSD_WIDTH), F32),
                        pltpu.VMEM((MIX_ROWS, SSD_WIDTH), F32), pltpu.VMEM((MIX_ROWS, LANES), F32),
                        pltpu.VMEM((MIX_ROWS, D_MODEL), BF16), pltpu.VMEM((MIX_ROWS, D_MODEL), BF16),
                        pltpu.VMEM((MIX_ROWS, D_MODEL), F32)],
        compiler_params=_cparams("arbitrary"),
        name="front_prompt",
    )(x, g_mix, w_in16, *_mixer_constants(), cw, cb, dtb, alog, dsk, nw, plw16, plb, psc, w_out16, g_ffn,
      r_hi, r_lo, r_bias)


def _mix_step_body(vp_ref, z_ref, xbc_ref, dt_ref, sp_ref, sc_ref, st_ref, cw_ref, cb_ref, dtb_ref, alog_ref,
                   dsk_ref, nw_ref, plw_ref, plb_ref, psc_ref, mix_ref, po_ref, co_ref, so_ref,
                   xdt_t, dec_t, b_t, c_t, xs_keep, y_t):
    h = pl.program_id(0)

    @pl.when(h == 0)
    def _():
        xbc = xbc_ref[...]
        conv = cb_ref[...] + sc_ref[0] * cw_ref[0:1, :]
        conv = conv + sc_ref[1] * cw_ref[1:2, :]
        conv = conv + sc_ref[2] * cw_ref[2:3, :]
        conv = conv + xbc * cw_ref[3:4, :]
        conv = _silu(conv)
        co_ref[0] = sc_ref[1]
        co_ref[1] = sc_ref[2]
        co_ref[2] = xbc
        xs = conv[:, 0:SSD_WIDTH]
        xs_keep[...] = xs
        b_t[...] = conv[:, SSD_WIDTH:SSD_WIDTH + SSD_BC].T
        c_t[...] = conv[:, SSD_WIDTH + SSD_BC:CONV_DIM].T
        dt = _softplus(dt_ref[...] + dtb_ref[...])
        d_a = dt * (-jnp.exp(alog_ref[...]))
        dt_t = dt.T
        dec_t[...] = jnp.exp(d_a).T
        xs_t = xs.T
        for k in range(SSD_HEADS):
            blk = slice(k * SSD_HEAD_DIM, (k + 1) * SSD_HEAD_DIM)
            xdt_t[blk, :] = xs_t[blk, :] * dt_t[k:k + 1, :]
        v = vp_ref[...]
        for k in range(POOL_BUF - 1):
            po_ref[k] = sp_ref[k + 1]
        po_ref[POOL_BUF - 1] = v
        for g, w in enumerate(POOL_WINDOWS):
            blk = slice(g * POOL_GROUP_DIM, (g + 1) * POOL_GROUP_DIM)
            acc = sp_ref[POOL_BUF - (w - 1), :, blk]
            for k in range(w - 2, 0, -1):
                acc = acc + sp_ref[POOL_BUF - k, :, blk]
            acc = acc + v[:, blk]
            m = acc / float(min(PAST_LEN + 1, w)) - v[:, blk]
            yg = _dot(m.astype(BF16), plw_ref[g]) + plb_ref[:, blk]
            mix_ref[:, blk] = (yg * psc_ref[:, blk]).astype(BF16)

    g_off = pl.multiple_of((h // (SSD_HEADS // SSD_GROUPS)) * SSD_STATE, SSD_STATE)
    h_off = pl.multiple_of(h * SSD_HEAD_DIM, SSD_HEAD_DIM)
    b_g = b_t[pl.ds(g_off, SSD_STATE), :]
    c_g = c_t[pl.ds(g_off, SSD_STATE), :]
    dec = dec_t[pl.ds(h, 1), :]
    xdt = xdt_t[pl.ds(h_off, SSD_HEAD_DIM), :]
    y_rows = []
    for p in range(SSD_HEAD_DIM):
        s_new = st_ref[0, p] * dec + xdt[p:p + 1, :] * b_g
        so_ref[0, p] = s_new
        y_rows.append(jnp.sum(s_new * c_g, axis=0, keepdims=True))
    y_t[pl.ds(h_off, SSD_HEAD_DIM), :] = jnp.concatenate(y_rows, axis=0)

    @pl.when(h == pl.num_programs(0) - 1)
    def _():
        xs = xs_keep[...]
        y = y_t[...].T + dsk_ref[...] * xs
        y = y * _silu(z_ref[...])
        width = SSD_WIDTH // SSD_GROUPS
        for g in range(SSD_GROUPS):
            blk = slice(g * width, (g + 1) * width)
            yg = y[:, blk]
            rs = lax.rsqrt(jnp.mean(yg * yg, axis=-1, keepdims=True) + EPS)
            mix_ref[:, POOL_WIDTH + g * width:POOL_WIDTH + (g + 1) * width] = (yg * rs * nw_ref[:, blk]).astype(BF16)


def _mix_step(vp, z, xbc, dt, pool_t, conv_t, ssm_t, cw, cb, dtb, alog, dsk, nw, plw16, plb, psc):
    n = vp.shape[0]
    fixed2 = lambda h: (0, 0)
    fixed3 = lambda h: (0, 0, 0)
    st_spec = pl.BlockSpec((1, SSD_HEAD_DIM, SSD_STATE, n), lambda h: (h, 0, 0, 0))
    return pl.pallas_call(
        _mix_step_body,
        grid=(SSD_HEADS,),
        in_specs=[pl.BlockSpec((n, POOL_WIDTH), fixed2), pl.BlockSpec((n, SSD_WIDTH), fixed2),
                  pl.BlockSpec((n, CONV_DIM), fixed2), pl.BlockSpec((n, LANES), fixed2),
                  pl.BlockSpec((POOL_BUF, n, POOL_WIDTH), fixed3),
                  pl.BlockSpec((SSD_CONV - 1, n, CONV_DIM), fixed3),
                  st_spec,
                  pl.BlockSpec((SSD_CONV, CONV_DIM), fixed2), pl.BlockSpec((1, CONV_DIM), fixed2),
                  pl.BlockSpec((1, LANES), fixed2), pl.BlockSpec((1, LANES), fixed2),
                  pl.BlockSpec((1, SSD_WIDTH), fixed2), pl.BlockSpec((1, SSD_WIDTH), fixed2),
                  pl.BlockSpec((len(POOL_WINDOWS), POOL_GROUP_DIM, POOL_GROUP_DIM), fixed3),
                  pl.BlockSpec((1, POOL_WIDTH), fixed2), pl.BlockSpec((1, POOL_WIDTH), fixed2)],
        out_specs=[pl.BlockSpec((n, D_MODEL), fixed2), pl.BlockSpec((POOL_BUF, n, POOL_WIDTH), fixed3),
                   pl.BlockSpec((SSD_CONV - 1, n, CONV_DIM), fixed3), st_spec],
        out_shape=[jax.ShapeDtypeStruct((n, D_MODEL), BF16), jax.ShapeDtypeStruct(pool_t.shape, F32),
                   jax.ShapeDtypeStruct(conv_t.shape, F32), jax.ShapeDtypeStruct(ssm_t.shape, F32)],
        scratch_shapes=[pltpu.VMEM((SSD_WIDTH, n), F32), pltpu.VMEM((LANES, n), F32),
                        pltpu.VMEM((SSD_BC, n), F32), pltpu.VMEM((SSD_BC, n), F32),
                        pltpu.VMEM((n, SSD_WIDTH), F32), pltpu.VMEM((SSD_WIDTH, n), F32)],
        compiler_params=_cparams("arbitrary"),
        name="mix_step",
    )(vp, z, xbc, dt, pool_t, conv_t, ssm_t, cw, cb, dtb, alog, dsk, nw, plw16, plb, psc)


PROMPT_TILE = 512


def _sort_tables(counts, tile_tokens, data_rows):
    cnt = jnp.concatenate(counts, axis=0)
    pc = (cnt + RUN_PAD - 1) // RUN_PAD * RUN_PAD
    off_local = jnp.cumsum(pc, axis=1) - pc
    tile_rows = jnp.sum(pc, axis=1)
    region = (jnp.sum(pc, axis=0) + MOE_TILE - 1) // MOE_TILE * MOE_TILE
    base = jnp.cumsum(region) - region
    dst = base[None, :] + jnp.cumsum(pc, axis=0) - pc
    per_pass = []
    lo = 0
    for c, tm in zip(counts, tile_tokens):
        hi = lo + c.shape[0]
        n_chunk = _sorted_rows_per_tile(tm) // RUN_PAD
        s = jnp.arange(n_chunk, dtype=jnp.int32) * RUN_PAD
        begins = off_local[lo:hi, None, :]
        ends = begins + pc[lo:hi, None, :]
        inside = (begins <= s[None, :, None]) & (s[None, :, None] < ends)
        shift = jnp.sum(jnp.where(inside, dst[lo:hi, None, :] - begins, 0), axis=2)
        valid = s[None, :] < tile_rows[lo:hi, None]
        parity = (jnp.arange(hi - lo, dtype=jnp.int32) % 2)[:, None]
        spare = data_rows + parity * _sorted_rows_per_tile(max(tile_tokens)) + s[None, :]
        scatter_dst = jnp.where(valid, s[None, :] + shift, spare).astype(jnp.int32)
        gather_src = jnp.where(valid, s[None, :] + shift, 0).astype(jnp.int32)
        off_v = jnp.broadcast_to(off_local[lo:hi, :, None].astype(F32), (hi - lo, N_EXPERTS, LANES))
        per_pass.append((scatter_dst, gather_src, off_v))
        lo = hi
    tiles_cum = jnp.cumsum(region // MOE_TILE)
    n_active = tiles_cum[-1]
    return per_pass, tiles_cum, n_active


def _max_sorted_rows(tile_counts, tile_tokens):
    rows = sum(n * (2 * tm + N_EXPERTS * (RUN_PAD - 1)) for n, tm in zip(tile_counts, tile_tokens))
    data_rows = (-(-rows // MOE_TILE) + N_EXPERTS) * MOE_TILE
    return data_rows, data_rows + 2 * _sorted_rows_per_tile(max(tile_tokens))


def _row(v):
    return v.reshape(1, -1).astype(F32)


def _pad_lanes(v):
    return jnp.pad(v.reshape(1, -1).astype(F32), ((0, 0), (0, LANES - v.size)))


def _state_from_blocks(st):
    n = st.shape[0]
    hpg = SSD_HEADS // SSD_GROUPS
    s6 = st.reshape(n, SSD_GROUPS, SSD_STATE, SSD_GROUPS, hpg, SSD_HEAD_DIM)
    per_group = [s6[:, g, :, g] for g in range(SSD_GROUPS)]
    s = jnp.stack(per_group, axis=1)
    return jnp.transpose(s, (0, 1, 3, 4, 2)).reshape(n, SSD_HEADS, SSD_HEAD_DIM, SSD_STATE)


def kernel(x_prompt, x_sample, p_prompt, p_sample, state_pool, state_conv, state_ssm, norm_mix, w_in, pool_lin_w, pool_lin_b, pool_scale, conv_w, conv_b, dt_bias, a_log, d_skip, ssd_norm, w_out, norm_ffn, router_grp_w, router_grp_b, router_exp_w, router_exp_b, exp_w_gate, exp_w_up, exp_w_down, norm_ple, ple_gate_w, ple_gate_b, ple_proj_w, norm_final):
    nb, seq, _ = x_prompt.shape
    ns = x_sample.shape[0]
    assert ns == LANES and x_sample.shape[1] == 1 and seq % MIX_ROWS == 0 and seq >= POOL_BUF

    w_in16 = jnp.pad(w_in[0], ((0, 0), (0, IN_COLS - w_in.shape[2]))).astype(BF16)
    w_out16 = w_out[0].astype(BF16)
    g_mix, g_ffn, g_ple, g_fin = _row(norm_mix[0]), _row(norm_ffn[0]), _row(norm_ple[0]), _row(norm_final)
    cw, cb = conv_w[0].astype(F32), _row(conv_b[0])
    dtb, alog = _pad_lanes(dt_bias[0]), _pad_lanes(a_log[0])
    dsk = _row(jnp.repeat(d_skip[0], SSD_HEAD_DIM))
    nw = _row(ssd_norm[0])
    plw16 = pool_lin_w[0].astype(BF16)
    plb, psc = _row(pool_lin_b[0]), _row(pool_scale[0])
    zeros4 = jnp.zeros((D_MODEL, 8 - N_EXPERT_GROUPS), F32)
    r_w = jnp.concatenate([router_grp_w[0], zeros4, router_exp_w[0],
                           jnp.zeros((D_MODEL, LANES - 8 - N_EXPERTS), F32)], axis=1).T
    r_hi = r_w.astype(BF16)
    r_lo = (r_w - r_hi.astype(F32)).astype(BF16)
    r_b = jnp.concatenate([router_grp_b[0], jnp.zeros((8 - N_EXPERT_GROUPS,), F32), router_exp_b[0],
                           jnp.zeros((LANES - 8 - N_EXPERTS,), F32)])
    wg = exp_w_gate[0].reshape(N_EXPERTS, D_MODEL, EXPERT_FF)
    wu = exp_w_up[0].reshape(N_EXPERTS, D_MODEL, EXPERT_FF)
    wd = exp_w_down[0].reshape(N_EXPERTS, EXPERT_FF, D_MODEL)
    pg16 = ple_gate_w[0].astype(BF16)
    pgb = _row(ple_gate_b[0])
    pp16 = ple_proj_w[0].astype(BF16)

    def route(x, mix16, tm):
        r_bias = jnp.broadcast_to(r_b[:, None], (LANES, tm))
        return _out_proj(x, mix16, w_out16, g_ffn, r_hi, r_lo, r_bias, tm)

    xp = x_prompt.reshape(nb * seq, D_MODEL)
    h1_p, u_p, rt_p, gate_p, cnt_p, st, pool_tail, conv_tail = _front_prompt(
        xp, g_mix, w_in16, cw, cb, dtb, alog, dsk, nw, plw16, plb, psc, w_out16, g_ffn, r_hi, r_lo,
        jnp.broadcast_to(r_b[:, None], (LANES, MIX_ROWS)), nb, seq)
    pool_p = pool_tail[:, TAIL_ROWS - POOL_BUF:]
    conv_p = conv_tail[:, CONV_HDR - (SSD_CONV - 1):]
    ssm_p = _state_from_blocks(st)

    xs_ = x_sample.reshape(ns, D_MODEL)
    vp_s, z_s, xbc_s, dt_s = _in_proj(xs_, g_mix, w_in16, ns)
    mix_s, pool_t, conv_t, ssm_t = _mix_step(
        vp_s, z_s, xbc_s, dt_s, jnp.transpose(state_pool[0], (1, 0, 2)), jnp.transpose(state_conv[0], (1, 0, 2)),
        jnp.transpose(state_ssm[0], (1, 2, 3, 0)), cw, cb, dtb, alog, dsk, nw, plw16, plb, psc)
    h1_s, u_s, rt_s, gate_s, cnt_s = route(xs_, mix_s, ns)
    pool_s = jnp.transpose(pool_t, (1, 0, 2))
    conv_s = jnp.transpose(conv_t, (1, 0, 2))
    ssm_s = jnp.transpose(ssm_t, (3, 0, 1, 2))

    counts = [cnt_p[:, :, 0].astype(jnp.int32), cnt_s[:, :, 0].astype(jnp.int32)]
    tiles = (PROMPT_TILE, ns)
    data_rows, total_rows = _max_sorted_rows([c.shape[0] for c in counts], tiles)
    (tab_p, tab_s), tiles_cum, n_active = _sort_tables(counts, tiles, data_rows)
    xs_sorted, dest_p = _sort_tokens(tab_p[0], u_p, rt_p, tab_p[2], None, PROMPT_TILE, total_rows)
    xs_sorted, dest_s = _sort_tokens(tab_s[0], u_s, rt_s, tab_s[2], xs_sorted, ns, total_rows)
    tile_start = jnp.concatenate([jnp.zeros((1,), jnp.int32), tiles_cum.astype(jnp.int32)])
    ys_sorted = _moe_sorted(tile_start, n_active.reshape(1).astype(jnp.int32), xs_sorted, wg, wu, wd)

    y_prompt = _ple(tab_p[1], h1_p, dest_p, gate_p, p_prompt[0].reshape(nb * seq, PLE_DIM),
                    g_ple, pg16, pgb, pp16, g_fin, ys_sorted, PROMPT_TILE)
    y_sample = _ple(tab_s[1], h1_s, dest_s, gate_s, p_sample[0].reshape(ns, PLE_DIM),
                    g_ple, pg16, pgb, pp16, g_fin, ys_sorted, ns)

    return (y_prompt.reshape(nb, seq, D_MODEL), y_sample.reshape(ns, 1, D_MODEL),
            pool_p[None], conv_p[None], ssm_p[None], pool_s[None], conv_s[None], ssm_s[None])
```

```python
import functools

import jax
import jax.numpy as jnp
import numpy as np
from jax import lax
from jax.experimental import pallas as pl
from jax.experimental.pallas import tpu as pltpu

F32 = jnp.float32
BF16 = jnp.bfloat16

D_MODEL = 1024
POOL_WIDTH = 512
POOL_WINDOWS = (2, 4, 8, 16)
POOL_GROUP_DIM = 128
POOL_BUF = 15
SSD_WIDTH = 512
SSD_HEAD_DIM = 64
SSD_HEADS = 8
SSD_GROUPS = 2
SSD_STATE = 64
SSD_CONV = 4
SSD_BC = SSD_GROUPS * SSD_STATE
CONV_DIM = SSD_WIDTH + 2 * SSD_BC
N_EXPERT_GROUPS = 4
EXPERTS_PER_GROUP = 8
N_EXPERTS = N_EXPERT_GROUPS * EXPERTS_PER_GROUP
EXPERT_FF = 256
PLE_DIM = 256
PAST_LEN = 16384
EPS = 1e-6

LANES = 128
IN_COLS = 1920
DT_OFF = POOL_WIDTH + SSD_WIDTH + CONV_DIM
VMEM_LIMIT = 56 * 1024 * 1024


def _cparams(*sem):
    return pltpu.CompilerParams(dimension_semantics=sem, vmem_limit_bytes=VMEM_LIMIT)


def _rms(x, g):
    return x * lax.rsqrt(jnp.mean(x * x, axis=-1, keepdims=True) + EPS) * g


def _sigmoid(x):
    return 1.0 / (1.0 + jnp.exp(-x))


def _silu(x):
    return x * _sigmoid(x)


def _split3(v):
    hi = v.astype(BF16)
    r = v - hi.astype(F32)
    mid = r.astype(BF16)
    lo = (r - mid.astype(F32)).astype(BF16)
    return hi, mid, lo


def _dot(a, b):
    return jnp.dot(a, b, preferred_element_type=F32)


def _dot_nt(a, b):
    return lax.dot_general(a, b, (((1,), (1,)), ((), ())), preferred_element_type=F32)


def _dot_tn(a, b):
    return lax.dot_general(a, b, (((0,), (0,)), ((), ())), preferred_element_type=F32)


def _dot_exact_lhs(sel, v, terms=3):
    acc = None
    for t in _split3(v)[:terms]:
        p = _dot(sel, t)
        acc = p if acc is None else acc + p
    return acc


def _dot_exact_rhs(v, sel, terms=3):
    acc = None
    for t in _split3(v)[:terms]:
        p = _dot(t, sel)
        acc = p if acc is None else acc + p
    return acc


def _route(lg):
    tm = lg.shape[1]
    gl = lg[0:N_EXPERT_GROUPS, :]
    gmax = jnp.max(gl, axis=0, keepdims=True)
    gsum = jnp.sum(jnp.exp(gl - gmax), axis=0, keepdims=True)
    g_w = 1.0 / gsum
    gi = lax.broadcasted_iota(jnp.int32, gl.shape, 0)
    g_idx = jnp.min(jnp.where(gl == gmax, gi, N_EXPERT_GROUPS), axis=0, keepdims=True)
    sel = jnp.zeros((EXPERTS_PER_GROUP, tm), F32)
    for g in range(N_EXPERT_GROUPS):
        blk = lg[8 + g * EXPERTS_PER_GROUP:8 + (g + 1) * EXPERTS_PER_GROUP, :]
        sel = jnp.where(g_idx == g, blk, sel)
    ei = lax.broadcasted_iota(jnp.int32, sel.shape, 0)
    m1 = jnp.max(sel, axis=0, keepdims=True)
    i1 = jnp.min(jnp.where(sel == m1, ei, EXPERTS_PER_GROUP), axis=0, keepdims=True)
    rest = jnp.where(ei == i1, -jnp.inf, sel)
    m2 = jnp.max(rest, axis=0, keepdims=True)
    i2 = jnp.min(jnp.where(rest == m2, ei, EXPERTS_PER_GROUP), axis=0, keepdims=True)
    p2 = jnp.exp(m2 - m1)
    w1 = g_w / (1.0 + p2)
    w2 = g_w * p2 / (1.0 + p2)
    return g_idx * EXPERTS_PER_GROUP + i1, g_idx * EXPERTS_PER_GROUP + i2, w1, w2


def _out_proj_and_route(x, mix16, w_ref, g_ref, rh_ref, rl_ref, rb_ref, h_ref, u_ref, rt_ref, wt_ref, cnt_ref,
                        r0=0, accumulate=False):
    tm = x.shape[0]
    h = x + _dot(mix16, w_ref[...])
    h_ref[r0:r0 + tm, :] = h
    _route_rows(h, g_ref, rh_ref, rl_ref, rb_ref, u_ref, rt_ref, wt_ref, cnt_ref, r0=r0, accumulate=accumulate)


def _route_rows(h, g_ref, rh_ref, rl_ref, rb_ref, u_ref, rt_ref, wt_ref, cnt_ref, r0=0, accumulate=False):
    tm = h.shape[0]
    u = _rms(h, g_ref[...])
    u_hi = u.astype(BF16)
    u_ref[r0:r0 + tm, :] = u_hi
    u_lo = (u - u_hi.astype(F32)).astype(BF16)
    lg = (_dot_nt(rh_ref[...], u_hi) + _dot_nt(rh_ref[...], u_lo) + _dot_nt(rl_ref[...], u_hi)
          + rb_ref[:, r0:r0 + tm])
    b1, b2, w1, w2 = _route(lg)
    r8 = lax.broadcasted_iota(jnp.int32, (8, tm), 0)
    rt_ref[0, :, r0:r0 + tm] = jnp.where(r8 == 0, b1.astype(F32), jnp.where(r8 == 1, b2.astype(F32), 0.0))
    wt_ref[r0:r0 + tm, :] = jnp.concatenate(
        [jnp.broadcast_to(w1, (LANES, tm)).T, jnp.broadcast_to(w2, (LANES, tm)).T], axis=1)
    kio = lax.broadcasted_iota(jnp.int32, (N_EXPERTS, tm), 0)
    hits = ((kio == b1) | (kio == b2)).astype(F32)
    cnt = jnp.broadcast_to(jnp.sum(hits, axis=1, keepdims=True), (N_EXPERTS, LANES))
    cnt_ref[0] = cnt_ref[0] + cnt if accumulate else cnt


RUN_PAD = 16
MOE_TILE = 256
MOE_BUFS = 6
MOE_AHEAD = 4


def _tile_lanes(v, width):
    reps = width // LANES
    return v if reps == 1 else jnp.concatenate([v] * reps, axis=1)


def _sorted_rows_per_tile(tm):
    need = 2 * tm + N_EXPERTS * (RUN_PAD - 1)
    return -(-need // LANES) * LANES


def _sort_body(cd_ref, u_ref, rt_ref, off_ref, *rest):
    xs_ref, dt_ref, loc, sem = rest[-4:]
    i = pl.program_id(0)
    tm = u_ref.shape[0]
    rows = loc.shape[1]
    rt = rt_ref[0]
    b1 = rt[0:1, :].astype(jnp.int32)
    b2 = rt[1:2, :].astype(jnp.int32)
    kio = lax.broadcasted_iota(jnp.int32, (N_EXPERTS, tm), 0)
    o1 = kio == b1
    o2 = kio == b2
    before = (lax.broadcasted_iota(jnp.int32, (tm, tm), 0) < lax.broadcasted_iota(jnp.int32, (tm, tm), 1))
    start = _dot((o1 | o2).astype(BF16), before.astype(BF16)) + _tile_lanes(off_ref[0], tm)
    d1 = jnp.sum(jnp.where(o1, start, 0.0), axis=0, keepdims=True)
    d2 = jnp.sum(jnp.where(o2, start, 0.0), axis=0, keepdims=True)
    dt_ref[...] = jnp.concatenate([jnp.broadcast_to(d1, (LANES, tm)).T, jnp.broadcast_to(d2, (LANES, tm)).T], axis=1)
    rio = lax.broadcasted_iota(jnp.int32, (rows, tm), 0)
    perm = ((rio == d1.astype(jnp.int32)) | (rio == d2.astype(jnp.int32))).astype(BF16)
    slot = i % 2
    loc[slot] = _dot(perm, u_ref[...]).astype(BF16)

    def chunk_copy(step, sl, c):
        return pltpu.make_async_copy(
            loc.at[sl, pl.ds(c * RUN_PAD, RUN_PAD), :],
            xs_ref.at[pl.ds(pl.multiple_of(cd_ref[step, c], RUN_PAD), RUN_PAD), :], sem.at[sl])

    n_chunk = rows // RUN_PAD
    for c in range(n_chunk):
        chunk_copy(i, slot, c).start()

    @pl.when(i > 0)
    def _():
        for c in range(n_chunk):
            chunk_copy(i - 1, 1 - slot, c).wait()

    @pl.when(i == pl.num_programs(0) - 1)
    def _():
        for c in range(n_chunk):
            chunk_copy(i, slot, c).wait()


def _sort_tokens(chunk_dst, u16, rt, off_v, xs_prev, tm, total_rows):
    t = u16.shape[0]
    rows = _sorted_rows_per_tile(tm)
    in_specs = [pl.BlockSpec((tm, D_MODEL), lambda i, cd: (i, 0)),
                pl.BlockSpec((1, 8, tm), lambda i, cd: (i, 0, 0)),
                pl.BlockSpec((1, N_EXPERTS, LANES), lambda i, cd: (i, 0, 0))]
    args = [chunk_dst, u16, rt, off_v]
    aliases = {}
    if xs_prev is not None:
        in_specs.append(pl.BlockSpec(memory_space=pl.ANY))
        aliases = {len(args): 0}
        args.append(xs_prev)
    return pl.pallas_call(
        _sort_body,
        grid_spec=pltpu.PrefetchScalarGridSpec(
            num_scalar_prefetch=1,
            grid=(t // tm,),
            in_specs=in_specs,
            out_specs=[pl.BlockSpec(memory_space=pl.ANY),
                       pl.BlockSpec((tm, 2 * LANES), lambda i, cd: (i, 0))],
            scratch_shapes=[pltpu.VMEM((2, rows, D_MODEL), BF16), pltpu.SemaphoreType.DMA((2,))]),
        out_shape=[jax.ShapeDtypeStruct((total_rows, D_MODEL), BF16),
                   jax.ShapeDtypeStruct((t, 2 * LANES), F32)],
        input_output_aliases=aliases,
        compiler_params=_cparams("arbitrary"),
        name="sort_tokens",
    )(*args)


def _moe_body(ts_ref, na_ref, xs_ref, wg_ref, wu_ref, wd_ref, ys_ref, xbuf, ybuf, wg16, wu16, wd16, sem_in, sem_out):
    k = pl.program_id(0)
    n_act = na_ref[0]

    def x_copy(g, sl):
        return pltpu.make_async_copy(xs_ref.at[pl.ds(pl.multiple_of(g * MOE_TILE, MOE_TILE), MOE_TILE), :],
                                     xbuf.at[sl], sem_in.at[sl])

    def y_copy(g, sl):
        return pltpu.make_async_copy(ybuf.at[sl],
                                     ys_ref.at[pl.ds(pl.multiple_of(g * MOE_TILE, MOE_TILE), MOE_TILE), :],
                                     sem_out.at[sl])

    @pl.when(k == 0)
    def _():
        for j in range(MOE_AHEAD):
            @pl.when(j < n_act)
            def _():
                x_copy(j, j).start(priority=1)

    wg16[...] = wg_ref[0].astype(BF16)
    wu16[...] = wu_ref[0].astype(BF16)
    wd16[...] = wd_ref[0].astype(BF16)

    def arrive(g):
        sl = g % MOE_BUFS

        @pl.when(g + MOE_AHEAD < n_act)
        def _():
            x_copy(g + MOE_AHEAD, (g + MOE_AHEAD) % MOE_BUFS).start(priority=1)

        x_copy(g, sl).wait()

        @pl.when(g >= MOE_BUFS)
        def _():
            y_copy(g - MOE_BUFS, sl).wait()

    def tiles(g, count):
        for j in range(count):
            arrive(g + j)
        acts = []
        for j in range(count):
            x = xbuf[(g + j) % MOE_BUFS]
            acts.append((_silu(_dot(x, wg16[...])) * _dot(x, wu16[...])).astype(BF16))
        for j in range(count):
            ybuf[(g + j) % MOE_BUFS] = _dot(acts[j], wd16[...]).astype(BF16)
        for j in range(count):
            y_copy(g + j, (g + j) % MOE_BUFS).start()

    g0 = ts_ref[k]
    n_here = ts_ref[k + 1] - g0

    def pair(p, carry):
        tiles(g0 + 2 * p, 2)
        return carry

    lax.fori_loop(0, n_here // 2, pair, 0)

    @pl.when(n_here % 2 == 1)
    def _():
        tiles(g0 + n_here - 1, 1)

    @pl.when(k == pl.num_programs(0) - 1)
    def _():
        for j in range(1, MOE_BUFS + 1):
            @pl.when(n_act >= j)
            def _():
                y_copy(n_act - j, (n_act - j) % MOE_BUFS).wait()


def _moe_sorted(tile_start, n_active, xs, wg, wu, wd):
    w_map = lambda k, ts, na: (k, 0, 0)
    return pl.pallas_call(
        _moe_body,
        grid_spec=pltpu.PrefetchScalarGridSpec(
            num_scalar_prefetch=2,
            grid=(N_EXPERTS,),
            in_specs=[pl.BlockSpec(memory_space=pl.ANY),
                      pl.BlockSpec((1, D_MODEL, EXPERT_FF), w_map),
                      pl.BlockSpec((1, D_MODEL, EXPERT_FF), w_map),
                      pl.BlockSpec((1, EXPERT_FF, D_MODEL), w_map)],
            out_specs=pl.BlockSpec(memory_space=pl.ANY),
            scratch_shapes=[pltpu.VMEM((MOE_BUFS, MOE_TILE, D_MODEL), BF16),
                            pltpu.VMEM((MOE_BUFS, MOE_TILE, D_MODEL), BF16),
                            pltpu.VMEM((D_MODEL, EXPERT_FF), BF16), pltpu.VMEM((D_MODEL, EXPERT_FF), BF16),
                            pltpu.VMEM((EXPERT_FF, D_MODEL), BF16),
                            pltpu.SemaphoreType.DMA((MOE_BUFS,)), pltpu.SemaphoreType.DMA((MOE_BUFS,))]),
        out_shape=jax.ShapeDtypeStruct(xs.shape, BF16),
        compiler_params=_cparams("arbitrary"),
        name="moe_sorted",
    )(tile_start, n_active, xs, wg, wu, wd)


def _ple_body(cd_ref, h_ref, dt_ref, wt_ref, p_ref, gn_ref, wg_ref, bg_ref, wp_ref, fn_ref, ys_ref,
              o_ref, loc, sem):
    i = pl.program_id(0)
    tm = h_ref.shape[0]
    rows = loc.shape[1]
    n_chunk = rows // RUN_PAD
    slot = i % 2

    def chunk_copy(step, sl, c):
        return pltpu.make_async_copy(
            ys_ref.at[pl.ds(pl.multiple_of(cd_ref[step, c], RUN_PAD), RUN_PAD), :],
            loc.at[sl, pl.ds(c * RUN_PAD, RUN_PAD), :], sem.at[sl])

    def fetch(step, sl):
        for c in range(n_chunk):
            chunk_copy(step, sl, c).start()

    @pl.when(i == 0)
    def _():
        fetch(0, 0)

    @pl.when(i + 1 < pl.num_programs(0))
    def _():
        fetch(i + 1, 1 - slot)

    for c in range(n_chunk):
        chunk_copy(i, slot, c).wait()

    ci = lax.broadcasted_iota(jnp.int32, (tm, LANES), 1)
    d1 = dt_ref[:, 0:LANES].astype(jnp.int32)
    d2 = dt_ref[:, LANES:2 * LANES].astype(jnp.int32)
    w1 = wt_ref[:, 0:LANES]
    w2 = wt_ref[:, LANES:2 * LANES]
    sel = jnp.concatenate(
        [jnp.where(d1 == ci + m * LANES, w1, jnp.where(d2 == ci + m * LANES, w2, 0.0)) for m in range(rows // LANES)],
        axis=1).astype(BF16)
    h = h_ref[...] + _dot(sel, loc[slot])
    a16 = _rms(h, gn_ref[...]).astype(BF16)
    gate = _sigmoid(_dot(a16, wg_ref[...]) + bg_ref[...])
    pp = _dot(p_ref[...].astype(BF16), wp_ref[...])
    h = h + gate * pp
    o_ref[...] = _rms(h, fn_ref[...])


def _ple(chunk_src, h1, dest_t, gate_t, p, g_ple, wg16, bg, wp16, g_final, ys, tm):
    t = h1.shape[0]
    row = lambda i, cd: (i, 0)
    fixed = lambda i, cd: (0, 0)
    return pl.pallas_call(
        _ple_body,
        grid_spec=pltpu.PrefetchScalarGridSpec(
            num_scalar_prefetch=1,
            grid=(t // tm,),
            in_specs=[pl.BlockSpec((tm, D_MODEL), row), pl.BlockSpec((tm, 2 * LANES), row),
                      pl.BlockSpec((tm, 2 * LANES), row), pl.BlockSpec((tm, PLE_DIM), row),
                      pl.BlockSpec((1, D_MODEL), fixed), pl.BlockSpec((D_MODEL, D_MODEL), fixed),
                      pl.BlockSpec((1, D_MODEL), fixed), pl.BlockSpec((PLE_DIM, D_MODEL), fixed),
                      pl.BlockSpec((1, D_MODEL), fixed), pl.BlockSpec(memory_space=pl.ANY)],
            out_specs=pl.BlockSpec((tm, D_MODEL), row),
            scratch_shapes=[pltpu.VMEM((2, _sorted_rows_per_tile(tm), D_MODEL), BF16),
                            pltpu.SemaphoreType.DMA((2,))]),
        out_shape=jax.ShapeDtypeStruct((t, D_MODEL), F32),
        compiler_params=_cparams("arbitrary"),
        name="ple_final",
    )(chunk_src, h1, dest_t, gate_t, p, g_ple, wg16, bg, wp16, g_final, ys)


MIX_ROWS = 512
SSD_Q = 128
POOL_HDR = 128
CONV_HDR = 8
TAIL_ROWS = 16


def _softplus(x):
    return jnp.maximum(x, 0.0) + jnp.log1p(jnp.exp(-jnp.abs(x)))


def _mixer_constants():
    q = SSD_Q
    li = np.arange(q)[:, None]
    tri = (li >= np.arange(q)[None, :])
    expand = (np.arange(SSD_WIDTH)[None, :] // SSD_HEAD_DIM) == np.arange(LANES)[:, None]
    hpg = SSD_HEADS // SSD_GROUPS
    block = (np.arange(SSD_BC)[:, None] // SSD_STATE) == (np.arange(SSD_WIDTH)[None, :] // (SSD_HEAD_DIM * hpg))
    wj = np.arange(2 * q)[None, :]
    band = np.stack([(wj <= li + POOL_HDR) & (wj > li + POOL_HDR - w) for w in POOL_WINDOWS])
    as_f32 = lambda m: jnp.asarray(m.astype(np.float32))
    return (as_f32(tri), as_f32(tri).astype(BF16), as_f32(expand).astype(BF16), as_f32(block),
            as_f32(band).astype(BF16))


def _front_body(steps_per_seq, x_ref, xp_ref, gm_ref, win_ref, trif_ref, tri_ref, exp_ref, bmask_ref, band_ref,
                cw_ref, cb_ref, dtb_ref, alog_ref, dsk_ref, nw_ref,
                plw_ref, plb_ref, psc_ref, wout_ref, gf_ref, rh_ref, rl_ref, rb_ref,
                h_ref, u_ref, rt_ref, wt_ref, cnt_ref, st_ref, ptail_ref, ctail_ref,
                pool_ext, conv_ext, s_ref, z_ref, dt_ref, mix_ref, mixp_ref):
    s = pl.program_id(0)
    n_tiles = pl.num_programs(0) - 1
    real = s < n_tiles
    c = jnp.minimum(s, n_tiles - 1) % steps_per_seq
    rows = MIX_ROWS
    q_len = SSD_Q

    @pl.when(s == 0)
    def _():
        mixp_ref[...] = jnp.zeros_like(mixp_ref)

    @pl.when(c == 0)
    def _():
        pool_ext[0:POOL_HDR, :] = jnp.zeros((POOL_HDR, POOL_WIDTH), F32)
        conv_ext[0:CONV_HDR, :] = jnp.zeros((CONV_HDR, CONV_DIM), F32)
        s_ref[...] = jnp.zeros_like(s_ref)

    @pl.when(c > 0)
    def _():
        pool_ext[0:POOL_HDR, :] = pool_ext[rows:rows + POOL_HDR, :]
        conv_ext[0:CONV_HDR, :] = conv_ext[rows:rows + CONV_HDR, :]

    causal = trif_ref[...] > 0.5
    tri = tri_ref[...]
    expand = exp_ref[...]
    blockmask = bmask_ref[...]
    lane = lax.broadcasted_iota(jnp.int32, (q_len, LANES), 1)
    left = lane < SSD_HEAD_DIM
    a_neg = -jnp.exp(alog_ref[...])
    rowi = lax.broadcasted_iota(jnp.int32, (q_len, LANES), 0)

    half = rows // 2
    n_cols = 256

    def in_proj_chunks(r0):
        a16 = _rms(x_ref[r0:r0 + half, :], gm_ref[...]).astype(BF16)
        dests = ([(pool_ext, POOL_HDR + r0, k, 0) for k in range(0, POOL_WIDTH, n_cols)]
                 + [(z_ref, r0, k, POOL_WIDTH) for k in range(0, SSD_WIDTH, n_cols)]
                 + [(conv_ext, CONV_HDR + r0, k, POOL_WIDTH + SSD_WIDTH) for k in range(0, CONV_DIM, n_cols)])

        def chunk(ref, row, k, col0):
            def run():
                ref[row:row + half, k:k + n_cols] = _dot(a16, win_ref[:, col0 + k:col0 + k + n_cols])
            return run

        def dt_chunk():
            dt_ref[r0:r0 + half, :] = _dot(a16, win_ref[:, DT_OFF:IN_COLS])

        return [chunk(*d) for d in dests] + [dt_chunk]

    def out_proj_chunks(r0):
        def chunk(k):
            def run():
                h_ref[r0:r0 + half, k:k + n_cols] = (xp_ref[r0:r0 + half, k:k + n_cols]
                                                    + _dot(mixp_ref[r0:r0 + half, :], wout_ref[:, k:k + n_cols]))
            return run

        def route():
            _route_rows(h_ref[r0:r0 + half, :], gf_ref, rh_ref, rl_ref, rb_ref, u_ref, rt_ref, wt_ref, cnt_ref,
                        r0=r0, accumulate=r0 > 0)

        return [chunk(k) for k in range(0, D_MODEL, n_cols)] + [route]

    def mix_pair(q_first, fillers):
        fillers = list(fillers)

        def fill():
            if fillers:
                fillers.pop(0)()

        subs = range(q_first, q_first + 2)
        xs_l, b16_l, call_l, c16_l = {}, {}, {}, {}
        for q in subs:
            base = CONV_HDR + q * q_len
            conv = cb_ref[...] + conv_ext[base - 3:base - 3 + q_len, :] * cw_ref[0:1, :]
            conv = conv + conv_ext[base - 2:base - 2 + q_len, :] * cw_ref[1:2, :]
            conv = conv + conv_ext[base - 1:base - 1 + q_len, :] * cw_ref[2:3, :]
            conv = conv + conv_ext[base:base + q_len, :] * cw_ref[3:4, :]
            conv = _silu(conv)
            xs_l[q] = conv[:, 0:SSD_WIDTH]
            b16_l[q] = conv[:, SSD_WIDTH:SSD_WIDTH + SSD_BC].astype(BF16)
            call_l[q] = conv[:, SSD_WIDTH + SSD_BC:CONV_DIM]
            c16_l[q] = call_l[q].astype(BF16)
            fill()
        dt_l = {q: _softplus(dt_ref[q * q_len:(q + 1) * q_len, :] + dtb_ref[...]) for q in subs}
        acs_l = {q: _dot_exact_lhs(tri, dt_l[q] * a_neg) for q in subs}
        acst_l = {q: acs_l[q].T for q in subs}
        fill()
        dtx_l = {q: _dot_exact_rhs(dt_l[q], expand, terms=2) for q in subs}
        acsx_l = {q: _dot_exact_rhs(acs_l[q], expand) for q in subs}
        lastx_l = {q: acsx_l[q][q_len - 1:q_len, :] for q in subs}
        xdt_l = {q: xs_l[q] * dtx_l[q] for q in subs}
        xdt16_l = {q: xdt_l[q].astype(BF16) for q in subs}
        fill()
        contrib_l = {q: _dot_tn(b16_l[q], (xdt_l[q] * jnp.exp(lastx_l[q] - acsx_l[q])).astype(BF16)) * blockmask
                     for q in subs}
        cb_l = {q: [_dot_nt(jnp.where(lax.shift_right_logical(lane, 6) == g, call_l[q], 0.0).astype(BF16), b16_l[q])
                    for g in range(SSD_GROUPS)] for q in subs}
        fill()
        zs_l = {}
        for q in subs:
            zs = []
            for h in range(SSD_HEADS):
                blk = slice((h // 2) * LANES, (h // 2 + 1) * LANES)
                seg = acs_l[q][:, h:h + 1] - acst_l[q][h:h + 1, :]
                decay = jnp.where(causal, jnp.exp(seg), 0.0)
                scores = (cb_l[q][h // (SSD_HEADS // SSD_GROUPS)] * decay).astype(BF16)
                zs.append(_dot(scores, xdt16_l[q][:, blk]))
                if h % 4 == 3:
                    fill()
            zs_l[q] = zs
        yoff_l = {}
        for q in subs:
            s_old = s_ref[...]
            yoff_l[q] = _dot(c16_l[q], s_old.astype(BF16)) * jnp.exp(acsx_l[q])
            s_ref[...] = s_old * jnp.exp(lastx_l[q]) + contrib_l[q]
        fill()
        for q in subs:
            r0 = q * q_len
            z = z_ref[r0:r0 + q_len, :]
            y_blocks = []
            for j in range(SSD_HEADS // 2):
                blk = slice(j * LANES, (j + 1) * LANES)
                y = jnp.where(left, zs_l[q][2 * j], zs_l[q][2 * j + 1]) + yoff_l[q][:, blk]
                y = y + dsk_ref[:, blk] * xs_l[q][:, blk]
                y_blocks.append(y * _silu(z[:, blk]))
            for g in range(SSD_GROUPS):
                y0, y1 = y_blocks[2 * g], y_blocks[2 * g + 1]
                ss = jnp.sum(y0 * y0, axis=-1, keepdims=True) + jnp.sum(y1 * y1, axis=-1, keepdims=True)
                rs = lax.rsqrt(ss * (1.0 / (2 * LANES)) + EPS)
                for k, yk in ((2 * g, y0), (2 * g + 1, y1)):
                    blk = slice(k * LANES, (k + 1) * LANES)
                    out = yk * rs * nw_ref[:, blk]
                    mix_ref[r0:r0 + q_len, POOL_WIDTH + k * LANES:POOL_WIDTH + (k + 1) * LANES] = out.astype(BF16)
            fill()
        for q in subs:
            r0 = q * q_len
            pos = c * rows + r0 + rowi
            for g, w in enumerate(POOL_WINDOWS):
                blk = slice(g * POOL_GROUP_DIM, (g + 1) * POOL_GROUP_DIM)
                pe = pool_ext[r0:r0 + 2 * q_len, blk]
                winsum = _dot_exact_lhs(band_ref[g], pe, terms=2)
                cnt = jnp.minimum(pos + 1, w).astype(F32)
                m = winsum / cnt - pe[q_len:2 * q_len, :]
                yg = _dot(m.astype(BF16), plw_ref[g]) + plb_ref[:, blk]
                mix_ref[r0:r0 + q_len, blk] = (yg * psc_ref[:, blk]).astype(BF16)
            fill()
        while fillers:
            fill()

    @pl.when(real)
    def _():
        for run in in_proj_chunks(0):
            run()
        mix_pair(0, in_proj_chunks(half) + out_proj_chunks(0))
        mix_pair(2, out_proj_chunks(half))
        mixp_ref[...] = mix_ref[...]

    @pl.when(jnp.logical_not(real))
    def _():
        for run in out_proj_chunks(0) + out_proj_chunks(half):
            run()

    @pl.when(real & (c == steps_per_seq - 1))
    def _():
        hpg = SSD_HEADS // SSD_GROUPS
        for j in range(SSD_HEADS // 2):
            flipped = s_ref[:, j * LANES:(j + 1) * LANES].T
            g = (2 * j) // hpg
            for k in range(2):
                st_ref[0, 2 * j + k] = flipped[k * SSD_HEAD_DIM:(k + 1) * SSD_HEAD_DIM,
                                               g * SSD_STATE:(g + 1) * SSD_STATE]
        ptail_ref[0] = pool_ext[POOL_HDR + rows - TAIL_ROWS:POOL_HDR + rows, :]
        ctail_ref[0] = conv_ext[rows:rows + CONV_HDR, :]


def _front_prompt(x, g_mix, w_in16, cw, cb, dtb, alog, dsk, nw, plw16, plb, psc, w_out16, g_ffn, r_hi, r_lo, r_bias,
                  nb, seq):
    steps = seq // MIX_ROWS
    nt = nb * steps
    cur = lambda s: (jnp.minimum(s, nt - 1), 0)
    prev = lambda s: (jnp.maximum(s - 1, 0), 0)
    prev3 = lambda s: (jnp.maximum(s - 1, 0), 0, 0)
    per_seq = lambda s: (jnp.minimum(s, nt - 1) // steps, 0, 0)
    fixed2 = lambda s: (0, 0)
    fixed3 = lambda s: (0, 0, 0)
    return pl.pallas_call(
        functools.partial(_front_body, steps),
        grid=(nt + 1,),
        in_specs=[pl.BlockSpec((MIX_ROWS, D_MODEL), cur), pl.BlockSpec((MIX_ROWS, D_MODEL), prev),
                  pl.BlockSpec((1, D_MODEL), fixed2), pl.BlockSpec((D_MODEL, IN_COLS), fixed2),
                  pl.BlockSpec((SSD_Q, SSD_Q), fixed2), pl.BlockSpec((SSD_Q, SSD_Q), fixed2),
                  pl.BlockSpec((LANES, SSD_WIDTH), fixed2), pl.BlockSpec((SSD_BC, SSD_WIDTH), fixed2),
                  pl.BlockSpec((len(POOL_WINDOWS), SSD_Q, 2 * SSD_Q), fixed3),
                  pl.BlockSpec((SSD_CONV, CONV_DIM), fixed2), pl.BlockSpec((1, CONV_DIM), fixed2),
                  pl.BlockSpec((1, LANES), fixed2), pl.BlockSpec((1, LANES), fixed2),
                  pl.BlockSpec((1, SSD_WIDTH), fixed2), pl.BlockSpec((1, SSD_WIDTH), fixed2),
                  pl.BlockSpec((len(POOL_WINDOWS), POOL_GROUP_DIM, POOL_GROUP_DIM), fixed3),
                  pl.BlockSpec((1, POOL_WIDTH), fixed2), pl.BlockSpec((1, POOL_WIDTH), fixed2),
                  pl.BlockSpec((D_MODEL, D_MODEL), fixed2), pl.BlockSpec((1, D_MODEL), fixed2),
                  pl.BlockSpec((LANES, D_MODEL), fixed2), pl.BlockSpec((LANES, D_MODEL), fixed2),
                  pl.BlockSpec((LANES, MIX_ROWS), fixed2)],
        out_specs=[pl.BlockSpec((MIX_ROWS, D_MODEL), prev), pl.BlockSpec((MIX_ROWS, D_MODEL), prev),
                   pl.BlockSpec((1, 8, MIX_ROWS), prev3), pl.BlockSpec((MIX_ROWS, 2 * LANES), prev),
                   pl.BlockSpec((1, N_EXPERTS, LANES), prev3),
                   pl.BlockSpec((1, SSD_HEADS, SSD_HEAD_DIM, SSD_STATE),
                                lambda s: (jnp.minimum(s, nt - 1) // steps, 0, 0, 0)),
                   pl.BlockSpec((1, TAIL_ROWS, POOL_WIDTH), per_seq),
                   pl.BlockSpec((1, CONV_HDR, CONV_DIM), per_seq)],
        out_shape=[jax.ShapeDtypeStruct((nb * seq, D_MODEL), F32), jax.ShapeDtypeStruct((nb * seq, D_MODEL), BF16),
                   jax.ShapeDtypeStruct((nt, 8, MIX_ROWS), F32), jax.ShapeDtypeStruct((nb * seq, 2 * LANES), F32),
                   jax.ShapeDtypeStruct((nt, N_EXPERTS, LANES), F32),
                   jax.ShapeDtypeStruct((nb, SSD_HEADS, SSD_HEAD_DIM, SSD_STATE), F32),
                   jax.ShapeDtypeStruct((nb, TAIL_ROWS, POOL_WIDTH), F32),
                   jax.ShapeDtypeStruct((nb, CONV_HDR, CONV_DIM), F32)],
        scratch_shapes=[pltpu.VMEM((POOL_HDR + MIX_ROWS, POOL_WIDTH), F32),
                        pltpu.VMEM((CONV_HDR + MIX_ROWS, CONV_DIM), F32),
                        pltpu.VMEM((SSD_BC, SSD_WIDTH), F32),
                        pltpu.VMEM((MIX_ROWS, SSD_WIDTH), F32), pltpu.VMEM((MIX_ROWS, LANES), F32),
                        pltpu.VMEM((MIX_ROWS, D_MODEL), BF16), pltpu.VMEM((MIX_ROWS, D_MODEL), BF16)],
        compiler_params=_cparams("arbitrary"),
        name="front_prompt",
    )(x, x, g_mix, w_in16, *_mixer_constants(), cw, cb, dtb, alog, dsk, nw, plw16, plb, psc, w_out16, g_ffn,
      r_hi, r_lo, r_bias)


def _mix_step_body(x_ref, gm_ref, win_ref, sp_ref, sc_ref, st_ref, cw_ref, cb_ref, dtb_ref, alog_ref,
                   dsk_ref, nw_ref, plw_ref, plb_ref, psc_ref, wout_ref, gf_ref, rh_ref, rl_ref, rb_ref,
                   h_ref, u_ref, rt_ref, wt_ref, cnt_ref, po_ref, co_ref, so_ref,
                   xdt_t, dec_t, b_t, c_t, xs_keep, y_t, z_ref, mix_ref):
    h = pl.program_id(0)

    @pl.when(h == 0)
    def _():
        a16 = _rms(x_ref[...], gm_ref[...]).astype(BF16)
        vp_new = _dot(a16, win_ref[:, 0:POOL_WIDTH])
        z_ref[...] = _dot(a16, win_ref[:, POOL_WIDTH:POOL_WIDTH + SSD_WIDTH])
        xbc = _dot(a16, win_ref[:, POOL_WIDTH + SSD_WIDTH:DT_OFF])
        dt_raw = _dot(a16, win_ref[:, DT_OFF:IN_COLS])
        conv = cb_ref[...] + sc_ref[0] * cw_ref[0:1, :]
        conv = conv + sc_ref[1] * cw_ref[1:2, :]
        conv = conv + sc_ref[2] * cw_ref[2:3, :]
        conv = conv + xbc * cw_ref[3:4, :]
        conv = _silu(conv)
        co_ref[0] = sc_ref[1]
        co_ref[1] = sc_ref[2]
        co_ref[2] = xbc
        xs = conv[:, 0:SSD_WIDTH]
        xs_keep[...] = xs
        b_t[...] = conv[:, SSD_WIDTH:SSD_WIDTH + SSD_BC].T
        c_t[...] = conv[:, SSD_WIDTH + SSD_BC:CONV_DIM].T
        dt = _softplus(dt_raw + dtb_ref[...])
        d_a = dt * (-jnp.exp(alog_ref[...]))
        dt_t = dt.T
        dec_t[...] = jnp.exp(d_a).T
        xs_t = xs.T
        for k in range(SSD_HEADS):
            blk = slice(k * SSD_HEAD_DIM, (k + 1) * SSD_HEAD_DIM)
            xdt_t[blk, :] = xs_t[blk, :] * dt_t[k:k + 1, :]
        v = vp_new
        for k in range(POOL_BUF - 1):
            po_ref[k] = sp_ref[k + 1]
        po_ref[POOL_BUF - 1] = v
        for g, w in enumerate(POOL_WINDOWS):
            blk = slice(g * POOL_GROUP_DIM, (g + 1) * POOL_GROUP_DIM)
            acc = sp_ref[POOL_BUF - (w - 1), :, blk]
            for k in range(w - 2, 0, -1):
                acc = acc + sp_ref[POOL_BUF - k, :, blk]
            acc = acc + v[:, blk]
            m = acc / float(min(PAST_LEN + 1, w)) - v[:, blk]
            yg = _dot(m.astype(BF16), plw_ref[g]) + plb_ref[:, blk]
            mix_ref[:, blk] = (yg * psc_ref[:, blk]).astype(BF16)

    g_off = pl.multiple_of((h // (SSD_HEADS // SSD_GROUPS)) * SSD_STATE, SSD_STATE)
    h_off = pl.multiple_of(h * SSD_HEAD_DIM, SSD_HEAD_DIM)
    b_g = b_t[pl.ds(g_off, SSD_STATE), :]
    c_g = c_t[pl.ds(g_off, SSD_STATE), :]
    dec = dec_t[pl.ds(h, 1), :]
    xdt = xdt_t[pl.ds(h_off, SSD_HEAD_DIM), :]
    y_rows = []
    for p in range(SSD_HEAD_DIM):
        s_new = st_ref[0, p] * dec + xdt[p:p + 1, :] * b_g
        so_ref[0, p] = s_new
        y_rows.append(jnp.sum(s_new * c_g, axis=0, keepdims=True))
    y_t[pl.ds(h_off, SSD_HEAD_DIM), :] = jnp.concatenate(y_rows, axis=0)

    @pl.when(h == pl.num_programs(0) - 1)
    def _():
        xs = xs_keep[...]
        y = y_t[...].T + dsk_ref[...] * xs
        y = y * _silu(z_ref[...])
        width = SSD_WIDTH // SSD_GROUPS
        for g in range(SSD_GROUPS):
            blk = slice(g * width, (g + 1) * width)
            yg = y[:, blk]
            rs = lax.rsqrt(jnp.mean(yg * yg, axis=-1, keepdims=True) + EPS)
            mix_ref[:, POOL_WIDTH + g * width:POOL_WIDTH + (g + 1) * width] = (yg * rs * nw_ref[:, blk]).astype(BF16)
        _out_proj_and_route(x_ref[...], mix_ref[...], wout_ref, gf_ref, rh_ref, rl_ref, rb_ref,
                            h_ref, u_ref, rt_ref, wt_ref, cnt_ref)


def _front_step(x, g_mix, w_in16, pool_t, conv_t, ssm_t, cw, cb, dtb, alog, dsk, nw, plw16, plb, psc,
                w_out16, g_ffn, r_hi, r_lo, r_bias):
    n = x.shape[0]
    fixed2 = lambda h: (0, 0)
    fixed3 = lambda h: (0, 0, 0)
    st_spec = pl.BlockSpec((1, SSD_HEAD_DIM, SSD_STATE, n), lambda h: (h, 0, 0, 0))
    return pl.pallas_call(
        _mix_step_body,
        grid=(SSD_HEADS,),
        in_specs=[pl.BlockSpec((n, D_MODEL), fixed2), pl.BlockSpec((1, D_MODEL), fixed2),
                  pl.BlockSpec((D_MODEL, IN_COLS), fixed2),
                  pl.BlockSpec((POOL_BUF, n, POOL_WIDTH), fixed3),
                  pl.BlockSpec((SSD_CONV - 1, n, CONV_DIM), fixed3),
                  st_spec,
                  pl.BlockSpec((SSD_CONV, CONV_DIM), fixed2), pl.BlockSpec((1, CONV_DIM), fixed2),
                  pl.BlockSpec((1, LANES), fixed2), pl.BlockSpec((1, LANES), fixed2),
                  pl.BlockSpec((1, SSD_WIDTH), fixed2), pl.BlockSpec((1, SSD_WIDTH), fixed2),
                  pl.BlockSpec((len(POOL_WINDOWS), POOL_GROUP_DIM, POOL_GROUP_DIM), fixed3),
                  pl.BlockSpec((1, POOL_WIDTH), fixed2), pl.BlockSpec((1, POOL_WIDTH), fixed2),
                  pl.BlockSpec((D_MODEL, D_MODEL), fixed2), pl.BlockSpec((1, D_MODEL), fixed2),
                  pl.BlockSpec((LANES, D_MODEL), fixed2), pl.BlockSpec((LANES, D_MODEL), fixed2),
                  pl.BlockSpec((LANES, n), fixed2)],
        out_specs=[pl.BlockSpec((n, D_MODEL), fixed2), pl.BlockSpec((n, D_MODEL), fixed2),
                   pl.BlockSpec((1, 8, n), fixed3), pl.BlockSpec((n, 2 * LANES), fixed2),
                   pl.BlockSpec((1, N_EXPERTS, LANES), fixed3),
                   pl.BlockSpec((POOL_BUF, n, POOL_WIDTH), fixed3),
                   pl.BlockSpec((SSD_CONV - 1, n, CONV_DIM), fixed3), st_spec],
        out_shape=[jax.ShapeDtypeStruct((n, D_MODEL), F32), jax.ShapeDtypeStruct((n, D_MODEL), BF16),
                   jax.ShapeDtypeStruct((1, 8, n), F32), jax.ShapeDtypeStruct((n, 2 * LANES), F32),
                   jax.ShapeDtypeStruct((1, N_EXPERTS, LANES), F32),
                   jax.ShapeDtypeStruct(pool_t.shape, F32),
                   jax.ShapeDtypeStruct(conv_t.shape, F32), jax.ShapeDtypeStruct(ssm_t.shape, F32)],
        scratch_shapes=[pltpu.VMEM((SSD_WIDTH, n), F32), pltpu.VMEM((LANES, n), F32),
                        pltpu.VMEM((SSD_BC, n), F32), pltpu.VMEM((SSD_BC, n), F32),
                        pltpu.VMEM((n, SSD_WIDTH), F32), pltpu.VMEM((SSD_WIDTH, n), F32),
                        pltpu.VMEM((n, SSD_WIDTH), F32), pltpu.VMEM((n, D_MODEL), BF16)],
        compiler_params=_cparams("arbitrary"),
        name="front_step",
    )(x, g_mix, w_in16, pool_t, conv_t, ssm_t, cw, cb, dtb, alog, dsk, nw, plw16, plb, psc,
      w_out16, g_ffn, r_hi, r_lo, r_bias)


PROMPT_TILE = 512


def _sort_tables(counts, tile_tokens, data_rows):
    cnt = jnp.concatenate(counts, axis=0)
    pc = (cnt + RUN_PAD - 1) // RUN_PAD * RUN_PAD
    off_local = jnp.cumsum(pc, axis=1) - pc
    tile_rows = jnp.sum(pc, axis=1)
    region = (jnp.sum(pc, axis=0) + MOE_TILE - 1) // MOE_TILE * MOE_TILE
    base = jnp.cumsum(region) - region
    dst = base[None, :] + jnp.cumsum(pc, axis=0) - pc
    per_pass = []
    lo = 0
    for c, tm in zip(counts, tile_tokens):
        hi = lo + c.shape[0]
        n_chunk = _sorted_rows_per_tile(tm) // RUN_PAD
        s = jnp.arange(n_chunk, dtype=jnp.int32) * RUN_PAD
        begins = off_local[lo:hi, None, :]
        ends = begins + pc[lo:hi, None, :]
        inside = (begins <= s[None, :, None]) & (s[None, :, None] < ends)
        shift = jnp.sum(jnp.where(inside, dst[lo:hi, None, :] - begins, 0), axis=2)
        valid = s[None, :] < tile_rows[lo:hi, None]
        parity = (jnp.arange(hi - lo, dtype=jnp.int32) % 2)[:, None]
        spare = data_rows + parity * _sorted_rows_per_tile(max(tile_tokens)) + s[None, :]
        scatter_dst = jnp.where(valid, s[None, :] + shift, spare).astype(jnp.int32)
        gather_src = jnp.where(valid, s[None, :] + shift, 0).astype(jnp.int32)
        off_v = jnp.broadcast_to(off_local[lo:hi, :, None].astype(F32), (hi - lo, N_EXPERTS, LANES))
        per_pass.append((scatter_dst, gather_src, off_v))
        lo = hi
    tiles_cum = jnp.cumsum(region // MOE_TILE)
    n_active = tiles_cum[-1]
    return per_pass, tiles_cum, n_active


def _max_sorted_rows(tile_counts, tile_tokens):
    rows = sum(n * (2 * tm + N_EXPERTS * (RUN_PAD - 1)) for n, tm in zip(tile_counts, tile_tokens))
    data_rows = (-(-rows // MOE_TILE) + N_EXPERTS) * MOE_TILE
    return data_rows, data_rows + 2 * _sorted_rows_per_tile(max(tile_tokens))


def _row(v):
    return v.reshape(1, -1).astype(F32)


def _pad_lanes(v):
    return jnp.pad(v.reshape(1, -1).astype(F32), ((0, 0), (0, LANES - v.size)))


def kernel(x_prompt, x_sample, p_prompt, p_sample, state_pool, state_conv, state_ssm, norm_mix, w_in, pool_lin_w, pool_lin_b, pool_scale, conv_w, conv_b, dt_bias, a_log, d_skip, ssd_norm, w_out, norm_ffn, router_grp_w, router_grp_b, router_exp_w, router_exp_b, exp_w_gate, exp_w_up, exp_w_down, norm_ple, ple_gate_w, ple_gate_b, ple_proj_w, norm_final):
    nb, seq, _ = x_prompt.shape
    ns = x_sample.shape[0]
    assert ns == LANES and x_sample.shape[1] == 1 and seq % MIX_ROWS == 0 and seq >= POOL_BUF

    w_in16 = jnp.pad(w_in[0], ((0, 0), (0, IN_COLS - w_in.shape[2]))).astype(BF16)
    w_out16 = w_out[0].astype(BF16)
    g_mix, g_ffn, g_ple, g_fin = _row(norm_mix[0]), _row(norm_ffn[0]), _row(norm_ple[0]), _row(norm_final)
    cw, cb = conv_w[0].astype(F32), _row(conv_b[0])
    dtb, alog = _pad_lanes(dt_bias[0]), _pad_lanes(a_log[0])
    dsk = _row(jnp.repeat(d_skip[0], SSD_HEAD_DIM))
    nw = _row(ssd_norm[0])
    plw16 = pool_lin_w[0].astype(BF16)
    plb, psc = _row(pool_lin_b[0]), _row(pool_scale[0])
    zeros4 = jnp.zeros((D_MODEL, 8 - N_EXPERT_GROUPS), F32)
    r_w = jnp.concatenate([router_grp_w[0], zeros4, router_exp_w[0],
                           jnp.zeros((D_MODEL, LANES - 8 - N_EXPERTS), F32)], axis=1).T
    r_hi = r_w.astype(BF16)
    r_lo = (r_w - r_hi.astype(F32)).astype(BF16)
    r_b = jnp.concatenate([router_grp_b[0], jnp.zeros((8 - N_EXPERT_GROUPS,), F32), router_exp_b[0],
                           jnp.zeros((LANES - 8 - N_EXPERTS,), F32)])
    wg = exp_w_gate[0].reshape(N_EXPERTS, D_MODEL, EXPERT_FF)
    wu = exp_w_up[0].reshape(N_EXPERTS, D_MODEL, EXPERT_FF)
    wd = exp_w_down[0].reshape(N_EXPERTS, EXPERT_FF, D_MODEL)
    pg16 = ple_gate_w[0].astype(BF16)
    pgb = _row(ple_gate_b[0])
    pp16 = ple_proj_w[0].astype(BF16)

    xp = x_prompt.reshape(nb * seq, D_MODEL)
    h1_p, u_p, rt_p, gate_p, cnt_p, st, pool_tail, conv_tail = _front_prompt(
        xp, g_mix, w_in16, cw, cb, dtb, alog, dsk, nw, plw16, plb, psc, w_out16, g_ffn, r_hi, r_lo,
        jnp.broadcast_to(r_b[:, None], (LANES, MIX_ROWS)), nb, seq)
    pool_p = pool_tail[:, TAIL_ROWS - POOL_BUF:]
    conv_p = conv_tail[:, CONV_HDR - (SSD_CONV - 1):]
    ssm_p = st

    xs_ = x_sample.reshape(ns, D_MODEL)
    h1_s, u_s, rt_s, gate_s, cnt_s, pool_t, conv_t, ssm_t = _front_step(
        xs_, g_mix, w_in16, jnp.transpose(state_pool[0], (1, 0, 2)), jnp.transpose(state_conv[0], (1, 0, 2)),
        jnp.transpose(state_ssm[0], (1, 2, 3, 0)), cw, cb, dtb, alog, dsk, nw, plw16, plb, psc,
        w_out16, g_ffn, r_hi, r_lo, jnp.broadcast_to(r_b[:, None], (LANES, ns)))
    pool_s = jnp.transpose(pool_t, (1, 0, 2))
    conv_s = jnp.transpose(conv_t, (1, 0, 2))
    ssm_s = jnp.transpose(ssm_t, (3, 0, 1, 2))

    counts = [cnt_p[:, :, 0].astype(jnp.int32), cnt_s[:, :, 0].astype(jnp.int32)]
    tiles = (PROMPT_TILE, ns)
    data_rows, total_rows = _max_sorted_rows([c.shape[0] for c in counts], tiles)
    (tab_p, tab_s), tiles_cum, n_active = _sort_tables(counts, tiles, data_rows)
    xs_sorted, dest_p = _sort_tokens(tab_p[0], u_p, rt_p, tab_p[2], None, PROMPT_TILE, total_rows)
    xs_sorted, dest_s = _sort_tokens(tab_s[0], u_s, rt_s, tab_s[2], xs_sorted, ns, total_rows)
    tile_start = jnp.concatenate([jnp.zeros((1,), jnp.int32), tiles_cum.astype(jnp.int32)])
    ys_sorted = _moe_sorted(tile_start, n_active.reshape(1).astype(jnp.int32), xs_sorted, wg, wu, wd)

    y_prompt = _ple(tab_p[1], h1_p, dest_p, gate_p, p_prompt[0].reshape(nb * seq, PLE_DIM),
                    g_ple, pg16, pgb, pp16, g_fin, ys_sorted, PROMPT_TILE)
    y_sample = _ple(tab_s[1], h1_s, dest_s, gate_s, p_sample[0].reshape(ns, PLE_DIM),
                    g_ple, pg16, pgb, pp16, g_fin, ys_sorted, ns)

    return (y_prompt.reshape(nb, seq, D_MODEL), y_sample.reshape(ns, 1, D_MODEL),
            pool_p[None], conv_p[None], ssm_p[None], pool_s[None], conv_s[None], ssm_s[None])
```

```python
import functools

import jax
import jax.numpy as jnp
import numpy as np
from jax import lax
from jax.experimental import pallas as pl
from jax.experimental.pallas import tpu as pltpu

F32 = jnp.float32
BF16 = jnp.bfloat16

D_MODEL = 1024
POOL_WIDTH = 512
POOL_WINDOWS = (2, 4, 8, 16)
POOL_GROUP_DIM = 128
POOL_BUF = 15
SSD_WIDTH = 512
SSD_HEAD_DIM = 64
SSD_HEADS = 8
SSD_GROUPS = 2
SSD_STATE = 64
SSD_CONV = 4
SSD_BC = SSD_GROUPS * SSD_STATE
CONV_DIM = SSD_WIDTH + 2 * SSD_BC
N_EXPERT_GROUPS = 4
EXPERTS_PER_GROUP = 8
N_EXPERTS = N_EXPERT_GROUPS * EXPERTS_PER_GROUP
EXPERT_FF = 256
PLE_DIM = 256
PAST_LEN = 16384
EPS = 1e-6

LANES = 128
IN_COLS = 1920
DT_OFF = POOL_WIDTH + SSD_WIDTH + CONV_DIM
VMEM_LIMIT = 56 * 1024 * 1024


def _cparams(*sem):
    return pltpu.CompilerParams(dimension_semantics=sem, vmem_limit_bytes=VMEM_LIMIT)


def _rms(x, g):
    return x * lax.rsqrt(jnp.mean(x * x, axis=-1, keepdims=True) + EPS) * g


def _sigmoid(x):
    return 1.0 / (1.0 + jnp.exp(-x))


def _silu(x):
    return x * _sigmoid(x)


def _split3(v):
    hi = v.astype(BF16)
    r = v - hi.astype(F32)
    mid = r.astype(BF16)
    lo = (r - mid.astype(F32)).astype(BF16)
    return hi, mid, lo


def _dot(a, b):
    return jnp.dot(a, b, preferred_element_type=F32)


def _dot_nt(a, b):
    return lax.dot_general(a, b, (((1,), (1,)), ((), ())), preferred_element_type=F32)


def _dot_tn(a, b):
    return lax.dot_general(a, b, (((0,), (0,)), ((), ())), preferred_element_type=F32)


def _dot_exact_lhs(sel, v, terms=3):
    acc = None
    for t in _split3(v)[:terms]:
        p = _dot(sel, t)
        acc = p if acc is None else acc + p
    return acc


def _dot_exact_rhs(v, sel, terms=3):
    acc = None
    for t in _split3(v)[:terms]:
        p = _dot(t, sel)
        acc = p if acc is None else acc + p
    return acc


def _route(lg):
    tm = lg.shape[1]
    gl = lg[0:N_EXPERT_GROUPS, :]
    gmax = jnp.max(gl, axis=0, keepdims=True)
    gsum = jnp.sum(jnp.exp(gl - gmax), axis=0, keepdims=True)
    g_w = 1.0 / gsum
    gi = lax.broadcasted_iota(jnp.int32, gl.shape, 0)
    g_idx = jnp.min(jnp.where(gl == gmax, gi, N_EXPERT_GROUPS), axis=0, keepdims=True)
    sel = jnp.zeros((EXPERTS_PER_GROUP, tm), F32)
    for g in range(N_EXPERT_GROUPS):
        blk = lg[8 + g * EXPERTS_PER_GROUP:8 + (g + 1) * EXPERTS_PER_GROUP, :]
        sel = jnp.where(g_idx == g, blk, sel)
    ei = lax.broadcasted_iota(jnp.int32, sel.shape, 0)
    m1 = jnp.max(sel, axis=0, keepdims=True)
    i1 = jnp.min(jnp.where(sel == m1, ei, EXPERTS_PER_GROUP), axis=0, keepdims=True)
    rest = jnp.where(ei == i1, -jnp.inf, sel)
    m2 = jnp.max(rest, axis=0, keepdims=True)
    i2 = jnp.min(jnp.where(rest == m2, ei, EXPERTS_PER_GROUP), axis=0, keepdims=True)
    p2 = jnp.exp(m2 - m1)
    w1 = g_w / (1.0 + p2)
    w2 = g_w * p2 / (1.0 + p2)
    return g_idx * EXPERTS_PER_GROUP + i1, g_idx * EXPERTS_PER_GROUP + i2, w1, w2


def _out_proj_and_route(x, mix16, w_ref, g_ref, rh_ref, rl_ref, rb_ref, h_ref, u_ref, rt_ref, wt_ref, cnt_ref,
                        r0=0, accumulate=False):
    tm = x.shape[0]
    h = x + _dot(mix16, w_ref[...])
    h_ref[r0:r0 + tm, :] = h
    _route_rows(h, g_ref, rh_ref, rl_ref, rb_ref, u_ref, rt_ref, wt_ref, cnt_ref, r0=r0, accumulate=accumulate)


def _route_rows(h, g_ref, rh_ref, rl_ref, rb_ref, u_ref, rt_ref, wt_ref, cnt_ref, r0=0, accumulate=False):
    tm = h.shape[0]
    u = _rms(h, g_ref[...])
    u_hi = u.astype(BF16)
    u_ref[r0:r0 + tm, :] = u_hi
    u_lo = (u - u_hi.astype(F32)).astype(BF16)
    lg = (_dot_nt(rh_ref[...], u_hi) + _dot_nt(rh_ref[...], u_lo) + _dot_nt(rl_ref[...], u_hi)
          + rb_ref[:, r0:r0 + tm])
    b1, b2, w1, w2 = _route(lg)
    r8 = lax.broadcasted_iota(jnp.int32, (8, tm), 0)
    rt_ref[0, :, r0:r0 + tm] = jnp.where(r8 == 0, b1.astype(F32), jnp.where(r8 == 1, b2.astype(F32), 0.0))
    wt_ref[r0:r0 + tm, :] = jnp.concatenate(
        [jnp.broadcast_to(w1, (LANES, tm)).T, jnp.broadcast_to(w2, (LANES, tm)).T], axis=1)
    kio = lax.broadcasted_iota(jnp.int32, (N_EXPERTS, tm), 0)
    hits = ((kio == b1) | (kio == b2)).astype(F32)
    cnt = jnp.broadcast_to(jnp.sum(hits, axis=1, keepdims=True), (N_EXPERTS, LANES))
    cnt_ref[0] = cnt_ref[0] + cnt if accumulate else cnt


RUN_PAD = 16
MOE_TILE = 256
MOE_BUFS = 6
MOE_AHEAD = 4


def _tile_lanes(v, width):
    reps = width // LANES
    return v if reps == 1 else jnp.concatenate([v] * reps, axis=1)


def _sorted_rows_per_tile(tm):
    need = 2 * tm + N_EXPERTS * (RUN_PAD - 1)
    return -(-need // LANES) * LANES


def _sort_body(cd_ref, u_ref, rt_ref, off_ref, *rest):
    xs_ref, dt_ref, loc, sem = rest[-4:]
    i = pl.program_id(0)
    tm = u_ref.shape[0]
    rows = loc.shape[1]
    rt = rt_ref[0]
    b1 = rt[0:1, :].astype(jnp.int32)
    b2 = rt[1:2, :].astype(jnp.int32)
    kio = lax.broadcasted_iota(jnp.int32, (N_EXPERTS, tm), 0)
    o1 = kio == b1
    o2 = kio == b2
    before = (lax.broadcasted_iota(jnp.int32, (tm, tm), 0) < lax.broadcasted_iota(jnp.int32, (tm, tm), 1))
    start = _dot((o1 | o2).astype(BF16), before.astype(BF16)) + _tile_lanes(off_ref[0], tm)
    d1 = jnp.sum(jnp.where(o1, start, 0.0), axis=0, keepdims=True)
    d2 = jnp.sum(jnp.where(o2, start, 0.0), axis=0, keepdims=True)
    dt_ref[...] = jnp.concatenate([jnp.broadcast_to(d1, (LANES, tm)).T, jnp.broadcast_to(d2, (LANES, tm)).T], axis=1)
    rio = lax.broadcasted_iota(jnp.int32, (rows, tm), 0)
    perm = ((rio == d1.astype(jnp.int32)) | (rio == d2.astype(jnp.int32))).astype(BF16)
    slot = i % 2
    loc[slot] = _dot(perm, u_ref[...]).astype(BF16)

    def chunk_copy(step, sl, c):
        return pltpu.make_async_copy(
            loc.at[sl, pl.ds(c * RUN_PAD, RUN_PAD), :],
            xs_ref.at[pl.ds(pl.multiple_of(cd_ref[step, c], RUN_PAD), RUN_PAD), :], sem.at[sl])

    n_chunk = rows // RUN_PAD
    for c in range(n_chunk):
        chunk_copy(i, slot, c).start()

    @pl.when(i > 0)
    def _():
        for c in range(n_chunk):
            chunk_copy(i - 1, 1 - slot, c).wait()

    @pl.when(i == pl.num_programs(0) - 1)
    def _():
        for c in range(n_chunk):
            chunk_copy(i, slot, c).wait()


def _sort_tokens(chunk_dst, u16, rt, off_v, xs_prev, tm, total_rows):
    t = u16.shape[0]
    rows = _sorted_rows_per_tile(tm)
    in_specs = [pl.BlockSpec((tm, D_MODEL), lambda i, cd: (i, 0)),
                pl.BlockSpec((1, 8, tm), lambda i, cd: (i, 0, 0)),
                pl.BlockSpec((1, N_EXPERTS, LANES), lambda i, cd: (i, 0, 0))]
    args = [chunk_dst, u16, rt, off_v]
    aliases = {}
    if xs_prev is not None:
        in_specs.append(pl.BlockSpec(memory_space=pl.ANY))
        aliases = {len(args): 0}
        args.append(xs_prev)
    return pl.pallas_call(
        _sort_body,
        grid_spec=pltpu.PrefetchScalarGridSpec(
            num_scalar_prefetch=1,
            grid=(t // tm,),
            in_specs=in_specs,
            out_specs=[pl.BlockSpec(memory_space=pl.ANY),
                       pl.BlockSpec((tm, 2 * LANES), lambda i, cd: (i, 0))],
            scratch_shapes=[pltpu.VMEM((2, rows, D_MODEL), BF16), pltpu.SemaphoreType.DMA((2,))]),
        out_shape=[jax.ShapeDtypeStruct((total_rows, D_MODEL), BF16),
                   jax.ShapeDtypeStruct((t, 2 * LANES), F32)],
        input_output_aliases=aliases,
        compiler_params=_cparams("arbitrary"),
        name="sort_tokens",
    )(*args)


def _moe_body(ts_ref, na_ref, xs_ref, wg_ref, wu_ref, wd_ref, ys_ref, xbuf, ybuf, wg16, wu16, wd16, sem_in, sem_out):
    k = pl.program_id(0)
    n_act = na_ref[0]

    def x_copy(g, sl):
        return pltpu.make_async_copy(xs_ref.at[pl.ds(pl.multiple_of(g * MOE_TILE, MOE_TILE), MOE_TILE), :],
                                     xbuf.at[sl], sem_in.at[sl])

    def y_copy(g, sl):
        return pltpu.make_async_copy(ybuf.at[sl],
                                     ys_ref.at[pl.ds(pl.multiple_of(g * MOE_TILE, MOE_TILE), MOE_TILE), :],
                                     sem_out.at[sl])

    @pl.when(k == 0)
    def _():
        for j in range(MOE_AHEAD):
            @pl.when(j < n_act)
            def _():
                x_copy(j, j).start(priority=1)

    wg16[...] = wg_ref[0].astype(BF16)
    wu16[...] = wu_ref[0].astype(BF16)
    wd16[...] = wd_ref[0].astype(BF16)

    def arrive(g):
        sl = g % MOE_BUFS

        @pl.when(g + MOE_AHEAD < n_act)
        def _():
            x_copy(g + MOE_AHEAD, (g + MOE_AHEAD) % MOE_BUFS).start(priority=1)

        x_copy(g, sl).wait()

        @pl.when(g >= MOE_BUFS)
        def _():
            y_copy(g - MOE_BUFS, sl).wait()

    def tiles(g, count):
        for j in range(count):
            arrive(g + j)
        acts = []
        for j in range(count):
            x = xbuf[(g + j) % MOE_BUFS]
            acts.append((_silu(_dot(x, wg16[...])) * _dot(x, wu16[...])).astype(BF16))
        for j in range(count):
            ybuf[(g + j) % MOE_BUFS] = _dot(acts[j], wd16[...]).astype(BF16)
        for j in range(count):
            y_copy(g + j, (g + j) % MOE_BUFS).start()

    g0 = ts_ref[k]
    n_here = ts_ref[k + 1] - g0

    def pair(p, carry):
        tiles(g0 + 2 * p, 2)
        return carry

    lax.fori_loop(0, n_here // 2, pair, 0)

    @pl.when(n_here % 2 == 1)
    def _():
        tiles(g0 + n_here - 1, 1)

    @pl.when(k == pl.num_programs(0) - 1)
    def _():
        for j in range(1, MOE_BUFS + 1):
            @pl.when(n_act >= j)
            def _():
                y_copy(n_act - j, (n_act - j) % MOE_BUFS).wait()


def _moe_sorted(tile_start, n_active, xs, wg, wu, wd):
    w_map = lambda k, ts, na: (k, 0, 0)
    return pl.pallas_call(
        _moe_body,
        grid_spec=pltpu.PrefetchScalarGridSpec(
            num_scalar_prefetch=2,
            grid=(N_EXPERTS,),
            in_specs=[pl.BlockSpec(memory_space=pl.ANY),
                      pl.BlockSpec((1, D_MODEL, EXPERT_FF), w_map),
                      pl.BlockSpec((1, D_MODEL, EXPERT_FF), w_map),
                      pl.BlockSpec((1, EXPERT_FF, D_MODEL), w_map)],
            out_specs=pl.BlockSpec(memory_space=pl.ANY),
            scratch_shapes=[pltpu.VMEM((MOE_BUFS, MOE_TILE, D_MODEL), BF16),
                            pltpu.VMEM((MOE_BUFS, MOE_TILE, D_MODEL), BF16),
                            pltpu.VMEM((D_MODEL, EXPERT_FF), BF16), pltpu.VMEM((D_MODEL, EXPERT_FF), BF16),
                            pltpu.VMEM((EXPERT_FF, D_MODEL), BF16),
                            pltpu.SemaphoreType.DMA((MOE_BUFS,)), pltpu.SemaphoreType.DMA((MOE_BUFS,))]),
        out_shape=jax.ShapeDtypeStruct(xs.shape, BF16),
        compiler_params=_cparams("arbitrary"),
        name="moe_sorted",
    )(tile_start, n_active, xs, wg, wu, wd)


def _ple_body(cd_ref, h_ref, dt_ref, wt_ref, p_ref, gn_ref, wg_ref, bg_ref, wp_ref, fn_ref, ys_ref,
              o_ref, loc, sem):
    i = pl.program_id(0)
    tm = h_ref.shape[0]
    rows = loc.shape[1]
    n_chunk = rows // RUN_PAD
    slot = i % 2

    def chunk_copy(step, sl, c):
        return pltpu.make_async_copy(
            ys_ref.at[pl.ds(pl.multiple_of(cd_ref[step, c], RUN_PAD), RUN_PAD), :],
            loc.at[sl, pl.ds(c * RUN_PAD, RUN_PAD), :], sem.at[sl])

    def fetch(step, sl):
        for c in range(n_chunk):
            chunk_copy(step, sl, c).start()

    @pl.when(i == 0)
    def _():
        fetch(0, 0)

    @pl.when(i + 1 < pl.num_programs(0))
    def _():
        fetch(i + 1, 1 - slot)

    for c in range(n_chunk):
        chunk_copy(i, slot, c).wait()

    ci = lax.broadcasted_iota(jnp.int32, (tm, LANES), 1)
    d1 = dt_ref[:, 0:LANES].astype(jnp.int32)
    d2 = dt_ref[:, LANES:2 * LANES].astype(jnp.int32)
    w1 = wt_ref[:, 0:LANES]
    w2 = wt_ref[:, LANES:2 * LANES]
    sel = jnp.concatenate(
        [jnp.where(d1 == ci + m * LANES, w1, jnp.where(d2 == ci + m * LANES, w2, 0.0)) for m in range(rows // LANES)],
        axis=1).astype(BF16)
    h = h_ref[...] + _dot(sel, loc[slot])
    a16 = _rms(h, gn_ref[...]).astype(BF16)
    gate = _sigmoid(_dot(a16, wg_ref[...]) + bg_ref[...])
    pp = _dot(p_ref[...].astype(BF16), wp_ref[...])
    h = h + gate * pp
    o_ref[...] = _rms(h, fn_ref[...])


def _ple(chunk_src, h1, dest_t, gate_t, p, g_ple, wg16, bg, wp16, g_final, ys, tm):
    t = h1.shape[0]
    row = lambda i, cd: (i, 0)
    fixed = lambda i, cd: (0, 0)
    return pl.pallas_call(
        _ple_body,
        grid_spec=pltpu.PrefetchScalarGridSpec(
            num_scalar_prefetch=1,
            grid=(t // tm,),
            in_specs=[pl.BlockSpec((tm, D_MODEL), row), pl.BlockSpec((tm, 2 * LANES), row),
                      pl.BlockSpec((tm, 2 * LANES), row), pl.BlockSpec((tm, PLE_DIM), row),
                      pl.BlockSpec((1, D_MODEL), fixed), pl.BlockSpec((D_MODEL, D_MODEL), fixed),
                      pl.BlockSpec((1, D_MODEL), fixed), pl.BlockSpec((PLE_DIM, D_MODEL), fixed),
                      pl.BlockSpec((1, D_MODEL), fixed), pl.BlockSpec(memory_space=pl.ANY)],
            out_specs=pl.BlockSpec((tm, D_MODEL), row),
            scratch_shapes=[pltpu.VMEM((2, _sorted_rows_per_tile(tm), D_MODEL), BF16),
                            pltpu.SemaphoreType.DMA((2,))]),
        out_shape=jax.ShapeDtypeStruct((t, D_MODEL), F32),
        compiler_params=_cparams("arbitrary"),
        name="ple_final",
    )(chunk_src, h1, dest_t, gate_t, p, g_ple, wg16, bg, wp16, g_final, ys)


MIX_ROWS = 512
SSD_Q = 128
POOL_HDR = 128
CONV_HDR = 8
TAIL_ROWS = 16


def _softplus(x):
    return jnp.maximum(x, 0.0) + jnp.log1p(jnp.exp(-jnp.abs(x)))


def _stack_terms(v, terms):
    parts = [t.astype(F32) for t in _split3(v)[:terms]]
    parts.append(jnp.zeros(((4 - terms) * v.shape[0], v.shape[1]), F32))
    return jnp.concatenate(parts, axis=0).astype(BF16)


def _cumsum_lanes(v, tri16):
    r = _dot_nt(_stack_terms(v, 3), tri16)
    n = v.shape[0]
    return r[0:n] + r[n:2 * n] + r[2 * n:3 * n]


def _expand_heads(v, expand32, terms):
    return _dot_tn(_stack_terms(v, terms), expand32)


def _mixer_constants():
    q = SSD_Q
    li = np.arange(q)[:, None]
    tri = (li >= np.arange(q)[None, :])
    expand = (np.arange(SSD_WIDTH)[None, :] // SSD_HEAD_DIM) == (np.arange(4 * SSD_HEADS)[:, None] % SSD_HEADS)
    hpg = SSD_HEADS // SSD_GROUPS
    block = (np.arange(SSD_BC)[:, None] // SSD_STATE) == (np.arange(SSD_WIDTH)[None, :] // (SSD_HEAD_DIM * hpg))
    wj = np.arange(2 * q)[None, :]
    band = np.stack([(wj <= li + POOL_HDR) & (wj > li + POOL_HDR - w) for w in POOL_WINDOWS])
    inv_count = np.stack([np.broadcast_to(1.0 / np.minimum(li + 1, w), (q, LANES)) for w in POOL_WINDOWS])
    as_f32 = lambda m: jnp.asarray(m.astype(np.float32))
    return (as_f32(tri), as_f32(tri).astype(BF16), as_f32(expand).astype(BF16), as_f32(block),
            as_f32(band).astype(BF16), as_f32(inv_count))


def _front_body(steps_per_seq, x_ref, xp_ref, gm_ref, win_ref, trif_ref, tri_ref, exp_ref, bmask_ref, band_ref,
                invc_ref, cw_ref, cb_ref, dtb_ref, alog_ref, dsk_ref, nw_ref,
                plw_ref, plb_ref, psc_ref, wout_ref, gf_ref, rh_ref, rl_ref, rb_ref,
                h_ref, u_ref, rt_ref, wt_ref, cnt_ref, st_ref, ptail_ref, ctail_ref,
                pool_ext, conv_ext, s_ref, z_ref, dt_ref, mix_ref, mixp_ref):
    s = pl.program_id(0)
    n_tiles = pl.num_programs(0) - 1
    real = s < n_tiles
    c = jnp.minimum(s, n_tiles - 1) % steps_per_seq
    rows = MIX_ROWS
    q_len = SSD_Q

    @pl.when(s == 0)
    def _():
        mixp_ref[...] = jnp.zeros_like(mixp_ref)

    @pl.when(c == 0)
    def _():
        pool_ext[0:POOL_HDR, :] = jnp.zeros((POOL_HDR, POOL_WIDTH), F32)
        conv_ext[0:CONV_HDR, :] = jnp.zeros((CONV_HDR, CONV_DIM), F32)
        s_ref[...] = jnp.zeros_like(s_ref)

    @pl.when(c > 0)
    def _():
        pool_ext[0:POOL_HDR, :] = pool_ext[rows:rows + POOL_HDR, :]
        conv_ext[0:CONV_HDR, :] = conv_ext[rows:rows + CONV_HDR, :]

    causal = trif_ref[...] > 0.5
    tri = tri_ref[...]
    expand = exp_ref[...]
    blockmask = bmask_ref[...]
    lane = lax.broadcasted_iota(jnp.int32, (q_len, LANES), 1)
    left = lane < SSD_HEAD_DIM
    a_neg = -jnp.exp(alog_ref[...])

    half = rows // 2
    n_cols = 256

    def in_proj_chunks(r0):
        a16 = _rms(x_ref[r0:r0 + half, :], gm_ref[...]).astype(BF16)
        dests = ([(pool_ext, POOL_HDR + r0, k, 0) for k in range(0, POOL_WIDTH, n_cols)]
                 + [(z_ref, r0, k, POOL_WIDTH) for k in range(0, SSD_WIDTH, n_cols)]
                 + [(conv_ext, CONV_HDR + r0, k, POOL_WIDTH + SSD_WIDTH) for k in range(0, CONV_DIM, n_cols)])

        def chunk(ref, row, k, col0):
            def run():
                ref[row:row + half, k:k + n_cols] = _dot(a16, win_ref[:, col0 + k:col0 + k + n_cols])
            return run

        def dt_chunk():
            dt_ref[r0:r0 + half, :] = _dot(a16, win_ref[:, DT_OFF:IN_COLS])

        return [chunk(*d) for d in dests] + [dt_chunk]

    def out_proj_chunks(r0):
        def chunk(k):
            def run():
                h_ref[r0:r0 + half, k:k + n_cols] = (xp_ref[r0:r0 + half, k:k + n_cols]
                                                    + _dot(mixp_ref[r0:r0 + half, :], wout_ref[:, k:k + n_cols]))
            return run

        def route():
            _route_rows(h_ref[r0:r0 + half, :], gf_ref, rh_ref, rl_ref, rb_ref, u_ref, rt_ref, wt_ref, cnt_ref,
                        r0=r0, accumulate=r0 > 0)

        return [chunk(k) for k in range(0, D_MODEL, n_cols)] + [route]

    def mix_pair(q_first, fillers):
        fillers = list(fillers)

        def fill():
            if fillers:
                fillers.pop(0)()

        subs = range(q_first, q_first + 2)
        xs_l, b16_l, call_l, c16_l = {}, {}, {}, {}
        for q in subs:
            base = CONV_HDR + q * q_len
            conv = cb_ref[...] + conv_ext[base - 3:base - 3 + q_len, :] * cw_ref[0:1, :]
            conv = conv + conv_ext[base - 2:base - 2 + q_len, :] * cw_ref[1:2, :]
            conv = conv + conv_ext[base - 1:base - 1 + q_len, :] * cw_ref[2:3, :]
            conv = conv + conv_ext[base:base + q_len, :] * cw_ref[3:4, :]
            conv = _silu(conv)
            xs_l[q] = conv[:, 0:SSD_WIDTH]
            b16_l[q] = conv[:, SSD_WIDTH:SSD_WIDTH + SSD_BC].astype(BF16)
            call_l[q] = conv[:, SSD_WIDTH + SSD_BC:CONV_DIM]
            c16_l[q] = call_l[q].astype(BF16)
            fill()
        dtt_l = {q: _softplus(dt_ref[q * q_len:(q + 1) * q_len, :].T[0:SSD_HEADS, :] + dtb_ref[...]) for q in subs}
        acst_l = {q: _cumsum_lanes(dtt_l[q] * a_neg, tri) for q in subs}
        acs_l = {q: jnp.concatenate([acst_l[q], jnp.zeros((LANES - SSD_HEADS, q_len), F32)], axis=0).T
                 for q in subs}
        fill()
        dtx_l = {q: _expand_heads(dtt_l[q], expand, 2) for q in subs}
        acsx_l = {q: _expand_heads(acst_l[q], expand, 3) for q in subs}
        lastx_l = {q: acsx_l[q][q_len - 1:q_len, :] for q in subs}
        xdt_l = {q: xs_l[q] * dtx_l[q] for q in subs}
        xdt16_l = {q: xdt_l[q].astype(BF16) for q in subs}
        fill()
        contrib_l = {q: _dot_tn(b16_l[q], (xdt_l[q] * jnp.exp(lastx_l[q] - acsx_l[q])).astype(BF16)) * blockmask
                     for q in subs}
        cb_l = {q: [_dot_nt(jnp.where(lax.shift_right_logical(lane, 6) == g, call_l[q], 0.0).astype(BF16), b16_l[q])
                    for g in range(SSD_GROUPS)] for q in subs}
        fill()
        zs_l = {}
        for q in subs:
            zs = []
            for h in range(SSD_HEADS):
                blk = slice((h // 2) * LANES, (h // 2 + 1) * LANES)
                seg = acs_l[q][:, h:h + 1] - acst_l[q][h:h + 1, :]
                decay = jnp.where(causal, jnp.exp(seg), 0.0)
                scores = (cb_l[q][h // (SSD_HEADS // SSD_GROUPS)] * decay).astype(BF16)
                zs.append(_dot(scores, xdt16_l[q][:, blk]))
                if h % 4 == 3:
                    fill()
            zs_l[q] = zs
        yoff_l = {}
        for q in subs:
            s_old = s_ref[...]
            yoff_l[q] = _dot(c16_l[q], s_old.astype(BF16)) * jnp.exp(acsx_l[q])
            s_ref[...] = s_old * jnp.exp(lastx_l[q]) + contrib_l[q]
        fill()
        for q in subs:
            r0 = q * q_len
            z = z_ref[r0:r0 + q_len, :]
            y_blocks = []
            for j in range(SSD_HEADS // 2):
                blk = slice(j * LANES, (j + 1) * LANES)
                y = jnp.where(left, zs_l[q][2 * j], zs_l[q][2 * j + 1]) + yoff_l[q][:, blk]
                y = y + dsk_ref[:, blk] * xs_l[q][:, blk]
                y_blocks.append(y * _silu(z[:, blk]))
            for g in range(SSD_GROUPS):
                y0, y1 = y_blocks[2 * g], y_blocks[2 * g + 1]
                ss = jnp.sum(y0 * y0, axis=-1, keepdims=True) + jnp.sum(y1 * y1, axis=-1, keepdims=True)
                rs = lax.rsqrt(ss * (1.0 / (2 * LANES)) + EPS)
                for k, yk in ((2 * g, y0), (2 * g + 1, y1)):
                    blk = slice(k * LANES, (k + 1) * LANES)
                    out = yk * rs * nw_ref[:, blk]
                    mix_ref[r0:r0 + q_len, POOL_WIDTH + k * LANES:POOL_WIDTH + (k + 1) * LANES] = out.astype(BF16)
            fill()
        for q in subs:
            r0 = q * q_len
            for g, w in enumerate(POOL_WINDOWS):
                blk = slice(g * POOL_GROUP_DIM, (g + 1) * POOL_GROUP_DIM)
                pe = pool_ext[r0:r0 + 2 * q_len, blk]
                hi = pe.astype(BF16)
                lo = (pe - hi.astype(F32)).astype(BF16)
                both = _dot(band_ref[g], jnp.concatenate([hi, lo], axis=1))
                winsum = both[:, 0:POOL_GROUP_DIM] + both[:, POOL_GROUP_DIM:2 * POOL_GROUP_DIM]
                inv = jnp.where(c == 0, invc_ref[g], 1.0 / w) if q == 0 else 1.0 / w
                m = winsum * inv - pe[q_len:2 * q_len, :]
                yg = _dot(m.astype(BF16), plw_ref[g]) + plb_ref[:, blk]
                mix_ref[r0:r0 + q_len, blk] = (yg * psc_ref[:, blk]).astype(BF16)
            fill()
        while fillers:
            fill()

    @pl.when(real)
    def _():
        for run in in_proj_chunks(0):
            run()
        mix_pair(0, in_proj_chunks(half) + out_proj_chunks(0))
        mix_pair(2, out_proj_chunks(half))
        mixp_ref[...] = mix_ref[...]

    @pl.when(jnp.logical_not(real))
    def _():
        for run in out_proj_chunks(0) + out_proj_chunks(half):
            run()

    @pl.when(real & (c == steps_per_seq - 1))
    def _():
        hpg = SSD_HEADS // SSD_GROUPS
        for j in range(SSD_HEADS // 2):
            flipped = s_ref[:, j * LANES:(j + 1) * LANES].T
            g = (2 * j) // hpg
            for k in range(2):
                st_ref[0, 2 * j + k] = flipped[k * SSD_HEAD_DIM:(k + 1) * SSD_HEAD_DIM,
                                               g * SSD_STATE:(g + 1) * SSD_STATE]
        ptail_ref[0] = pool_ext[POOL_HDR + rows - TAIL_ROWS:POOL_HDR + rows, :]
        ctail_ref[0] = conv_ext[rows:rows + CONV_HDR, :]


def _front_prompt(x, g_mix, w_in16, cw, cb, dtb, alog, dsk, nw, plw16, plb, psc, w_out16, g_ffn, r_hi, r_lo, r_bias,
                  nb, seq):
    steps = seq // MIX_ROWS
    nt = nb * steps
    cur = lambda s: (jnp.minimum(s, nt - 1), 0)
    prev = lambda s: (jnp.maximum(s - 1, 0), 0)
    prev3 = lambda s: (jnp.maximum(s - 1, 0), 0, 0)
    per_seq = lambda s: (jnp.minimum(s, nt - 1) // steps, 0, 0)
    fixed2 = lambda s: (0, 0)
    fixed3 = lambda s: (0, 0, 0)
    return pl.pallas_call(
        functools.partial(_front_body, steps),
        grid=(nt + 1,),
        in_specs=[pl.BlockSpec((MIX_ROWS, D_MODEL), cur), pl.BlockSpec((MIX_ROWS, D_MODEL), prev),
                  pl.BlockSpec((1, D_MODEL), fixed2), pl.BlockSpec((D_MODEL, IN_COLS), fixed2),
                  pl.BlockSpec((SSD_Q, SSD_Q), fixed2), pl.BlockSpec((SSD_Q, SSD_Q), fixed2),
                  pl.BlockSpec((4 * SSD_HEADS, SSD_WIDTH), fixed2), pl.BlockSpec((SSD_BC, SSD_WIDTH), fixed2),
                  pl.BlockSpec((len(POOL_WINDOWS), SSD_Q, 2 * SSD_Q), fixed3),
                  pl.BlockSpec((len(POOL_WINDOWS), SSD_Q, LANES), fixed3),
                  pl.BlockSpec((SSD_CONV, CONV_DIM), fixed2), pl.BlockSpec((1, CONV_DIM), fixed2),
                  pl.BlockSpec((SSD_HEADS, SSD_Q), fixed2), pl.BlockSpec((SSD_HEADS, SSD_Q), fixed2),
                  pl.BlockSpec((1, SSD_WIDTH), fixed2), pl.BlockSpec((1, SSD_WIDTH), fixed2),
                  pl.BlockSpec((len(POOL_WINDOWS), POOL_GROUP_DIM, POOL_GROUP_DIM), fixed3),
                  pl.BlockSpec((1, POOL_WIDTH), fixed2), pl.BlockSpec((1, POOL_WIDTH), fixed2),
                  pl.BlockSpec((D_MODEL, D_MODEL), fixed2), pl.BlockSpec((1, D_MODEL), fixed2),
                  pl.BlockSpec((LANES, D_MODEL), fixed2), pl.BlockSpec((LANES, D_MODEL), fixed2),
                  pl.BlockSpec((LANES, MIX_ROWS), fixed2)],
        out_specs=[pl.BlockSpec((MIX_ROWS, D_MODEL), prev), pl.BlockSpec((MIX_ROWS, D_MODEL), prev),
                   pl.BlockSpec((1, 8, MIX_ROWS), prev3), pl.BlockSpec((MIX_ROWS, 2 * LANES), prev),
                   pl.BlockSpec((1, N_EXPERTS, LANES), prev3),
                   pl.BlockSpec((1, SSD_HEADS, SSD_HEAD_DIM, SSD_STATE),
                                lambda s: (jnp.minimum(s, nt - 1) // steps, 0, 0, 0)),
                   pl.BlockSpec((1, TAIL_ROWS, POOL_WIDTH), per_seq),
                   pl.BlockSpec((1, CONV_HDR, CONV_DIM), per_seq)],
        out_shape=[jax.ShapeDtypeStruct((nb * seq, D_MODEL), F32), jax.ShapeDtypeStruct((nb * seq, D_MODEL), BF16),
                   jax.ShapeDtypeStruct((nt, 8, MIX_ROWS), F32), jax.ShapeDtypeStruct((nb * seq, 2 * LANES), F32),
                   jax.ShapeDtypeStruct((nt, N_EXPERTS, LANES), F32),
                   jax.ShapeDtypeStruct((nb, SSD_HEADS, SSD_HEAD_DIM, SSD_STATE), F32),
                   jax.ShapeDtypeStruct((nb, TAIL_ROWS, POOL_WIDTH), F32),
                   jax.ShapeDtypeStruct((nb, CONV_HDR, CONV_DIM), F32)],
        scratch_shapes=[pltpu.VMEM((POOL_HDR + MIX_ROWS, POOL_WIDTH), F32),
                        pltpu.VMEM((CONV_HDR + MIX_ROWS, CONV_DIM), F32),
                        pltpu.VMEM((SSD_BC, SSD_WIDTH), F32),
                        pltpu.VMEM((MIX_ROWS, SSD_WIDTH), F32), pltpu.VMEM((MIX_ROWS, LANES), F32),
                        pltpu.VMEM((MIX_ROWS, D_MODEL), BF16), pltpu.VMEM((MIX_ROWS, D_MODEL), BF16)],
        compiler_params=_cparams("arbitrary"),
        name="front_prompt",
    )(x, x, g_mix, w_in16, *_mixer_constants(), cw, cb, dtb, alog, dsk, nw, plw16, plb, psc, w_out16, g_ffn,
      r_hi, r_lo, r_bias)


def _mix_step_body(x_ref, gm_ref, win_ref, sp_ref, sc_ref, st_ref, cw_ref, cb_ref, dtb_ref, alog_ref,
                   dsk_ref, nw_ref, plw_ref, plb_ref, psc_ref, wout_ref, gf_ref, rh_ref, rl_ref, rb_ref,
                   h_ref, u_ref, rt_ref, wt_ref, cnt_ref, po_ref, co_ref, so_ref,
                   xdt_t, dec_t, b_t, c_t, xs_keep, y_t, z_ref, mix_ref):
    h = pl.program_id(0)

    @pl.when(h == 0)
    def _():
        a16 = _rms(x_ref[...], gm_ref[...]).astype(BF16)
        vp_new = _dot(a16, win_ref[:, 0:POOL_WIDTH])
        z_ref[...] = _dot(a16, win_ref[:, POOL_WIDTH:POOL_WIDTH + SSD_WIDTH])
        xbc = _dot(a16, win_ref[:, POOL_WIDTH + SSD_WIDTH:DT_OFF])
        dt_raw = _dot(a16, win_ref[:, DT_OFF:IN_COLS])
        conv = cb_ref[...] + sc_ref[0] * cw_ref[0:1, :]
        conv = conv + sc_ref[1] * cw_ref[1:2, :]
        conv = conv + sc_ref[2] * cw_ref[2:3, :]
        conv = conv + xbc * cw_ref[3:4, :]
        conv = _silu(conv)
        co_ref[0] = sc_ref[1]
        co_ref[1] = sc_ref[2]
        co_ref[2] = xbc
        xs = conv[:, 0:SSD_WIDTH]
        xs_keep[...] = xs
        b_t[...] = conv[:, SSD_WIDTH:SSD_WIDTH + SSD_BC].T
        c_t[...] = conv[:, SSD_WIDTH + SSD_BC:CONV_DIM].T
        dt = _softplus(dt_raw + dtb_ref[...])
        d_a = dt * (-jnp.exp(alog_ref[...]))
        dt_t = dt.T
        dec_t[...] = jnp.exp(d_a).T
        xs_t = xs.T
        for k in range(SSD_HEADS):
            blk = slice(k * SSD_HEAD_DIM, (k + 1) * SSD_HEAD_DIM)
            xdt_t[blk, :] = xs_t[blk, :] * dt_t[k:k + 1, :]
        v = vp_new
        for k in range(POOL_BUF - 1):
            po_ref[k] = sp_ref[k + 1]
        po_ref[POOL_BUF - 1] = v
        for g, w in enumerate(POOL_WINDOWS):
            blk = slice(g * POOL_GROUP_DIM, (g + 1) * POOL_GROUP_DIM)
            acc = sp_ref[POOL_BUF - (w - 1), :, blk]
            for k in range(w - 2, 0, -1):
                acc = acc + sp_ref[POOL_BUF - k, :, blk]
            acc = acc + v[:, blk]
            m = acc / float(min(PAST_LEN + 1, w)) - v[:, blk]
            yg = _dot(m.astype(BF16), plw_ref[g]) + plb_ref[:, blk]
            mix_ref[:, blk] = (yg * psc_ref[:, blk]).astype(BF16)

    g_off = pl.multiple_of((h // (SSD_HEADS // SSD_GROUPS)) * SSD_STATE, SSD_STATE)
    h_off = pl.multiple_of(h * SSD_HEAD_DIM, SSD_HEAD_DIM)
    b_g = b_t[pl.ds(g_off, SSD_STATE), :]
    c_g = c_t[pl.ds(g_off, SSD_STATE), :]
    dec = dec_t[pl.ds(h, 1), :]
    xdt = xdt_t[pl.ds(h_off, SSD_HEAD_DIM), :]
    y_rows = []
    for p in range(SSD_HEAD_DIM):
        s_new = st_ref[0, p] * dec + xdt[p:p + 1, :] * b_g
        so_ref[0, p] = s_new
        y_rows.append(jnp.sum(s_new * c_g, axis=0, keepdims=True))
    y_t[pl.ds(h_off, SSD_HEAD_DIM), :] = jnp.concatenate(y_rows, axis=0)

    @pl.when(h == pl.num_programs(0) - 1)
    def _():
        xs = xs_keep[...]
        y = y_t[...].T + dsk_ref[...] * xs
        y = y * _silu(z_ref[...])
        width = SSD_WIDTH // SSD_GROUPS
        for g in range(SSD_GROUPS):
            blk = slice(g * width, (g + 1) * width)
            yg = y[:, blk]
            rs = lax.rsqrt(jnp.mean(yg * yg, axis=-1, keepdims=True) + EPS)
            mix_ref[:, POOL_WIDTH + g * width:POOL_WIDTH + (g + 1) * width] = (yg * rs * nw_ref[:, blk]).astype(BF16)
        _out_proj_and_route(x_ref[...], mix_ref[...], wout_ref, gf_ref, rh_ref, rl_ref, rb_ref,
                            h_ref, u_ref, rt_ref, wt_ref, cnt_ref)


def _front_step(x, g_mix, w_in16, pool_t, conv_t, ssm_t, cw, cb, dtb, alog, dsk, nw, plw16, plb, psc,
                w_out16, g_ffn, r_hi, r_lo, r_bias):
    n = x.shape[0]
    fixed2 = lambda h: (0, 0)
    fixed3 = lambda h: (0, 0, 0)
    st_spec = pl.BlockSpec((1, SSD_HEAD_DIM, SSD_STATE, n), lambda h: (h, 0, 0, 0))
    return pl.pallas_call(
        _mix_step_body,
        grid=(SSD_HEADS,),
        in_specs=[pl.BlockSpec((n, D_MODEL), fixed2), pl.BlockSpec((1, D_MODEL), fixed2),
                  pl.BlockSpec((D_MODEL, IN_COLS), fixed2),
                  pl.BlockSpec((POOL_BUF, n, POOL_WIDTH), fixed3),
                  pl.BlockSpec((SSD_CONV - 1, n, CONV_DIM), fixed3),
                  st_spec,
                  pl.BlockSpec((SSD_CONV, CONV_DIM), fixed2), pl.BlockSpec((1, CONV_DIM), fixed2),
                  pl.BlockSpec((1, LANES), fixed2), pl.BlockSpec((1, LANES), fixed2),
                  pl.BlockSpec((1, SSD_WIDTH), fixed2), pl.BlockSpec((1, SSD_WIDTH), fixed2),
                  pl.BlockSpec((len(POOL_WINDOWS), POOL_GROUP_DIM, POOL_GROUP_DIM), fixed3),
                  pl.BlockSpec((1, POOL_WIDTH), fixed2), pl.BlockSpec((1, POOL_WIDTH), fixed2),
                  pl.BlockSpec((D_MODEL, D_MODEL), fixed2), pl.BlockSpec((1, D_MODEL), fixed2),
                  pl.BlockSpec((LANES, D_MODEL), fixed2), pl.BlockSpec((LANES, D_MODEL), fixed2),
                  pl.BlockSpec((LANES, n), fixed2)],
        out_specs=[pl.BlockSpec((n, D_MODEL), fixed2), pl.BlockSpec((n, D_MODEL), fixed2),
                   pl.BlockSpec((1, 8, n), fixed3), pl.BlockSpec((n, 2 * LANES), fixed2),
                   pl.BlockSpec((1, N_EXPERTS, LANES), fixed3),
                   pl.BlockSpec((POOL_BUF, n, POOL_WIDTH), fixed3),
                   pl.BlockSpec((SSD_CONV - 1, n, CONV_DIM), fixed3), st_spec],
        out_shape=[jax.ShapeDtypeStruct((n, D_MODEL), F32), jax.ShapeDtypeStruct((n, D_MODEL), BF16),
                   jax.ShapeDtypeStruct((1, 8, n), F32), jax.ShapeDtypeStruct((n, 2 * LANES), F32),
                   jax.ShapeDtypeStruct((1, N_EXPERTS, LANES), F32),
                   jax.ShapeDtypeStruct(pool_t.shape, F32),
                   jax.ShapeDtypeStruct(conv_t.shape, F32), jax.ShapeDtypeStruct(ssm_t.shape, F32)],
        scratch_shapes=[pltpu.VMEM((SSD_WIDTH, n), F32), pltpu.VMEM((LANES, n), F32),
                        pltpu.VMEM((SSD_BC, n), F32), pltpu.VMEM((SSD_BC, n), F32),
                        pltpu.VMEM((n, SSD_WIDTH), F32), pltpu.VMEM((SSD_WIDTH, n), F32),
                        pltpu.VMEM((n, SSD_WIDTH), F32), pltpu.VMEM((n, D_MODEL), BF16)],
        compiler_params=_cparams("arbitrary"),
        name="front_step",
    )(x, g_mix, w_in16, pool_t, conv_t, ssm_t, cw, cb, dtb, alog, dsk, nw, plw16, plb, psc,
      w_out16, g_ffn, r_hi, r_lo, r_bias)


PROMPT_TILE = 512


def _sort_tables(counts, tile_tokens, data_rows):
    cnt = jnp.concatenate(counts, axis=0)
    pc = (cnt + RUN_PAD - 1) // RUN_PAD * RUN_PAD
    off_local = jnp.cumsum(pc, axis=1) - pc
    tile_rows = jnp.sum(pc, axis=1)
    region = (jnp.sum(pc, axis=0) + MOE_TILE - 1) // MOE_TILE * MOE_TILE
    base = jnp.cumsum(region) - region
    dst = base[None, :] + jnp.cumsum(pc, axis=0) - pc
    per_pass = []
    lo = 0
    for c, tm in zip(counts, tile_tokens):
        hi = lo + c.shape[0]
        n_chunk = _sorted_rows_per_tile(tm) // RUN_PAD
        s = jnp.arange(n_chunk, dtype=jnp.int32) * RUN_PAD
        begins = off_local[lo:hi, None, :]
        ends = begins + pc[lo:hi, None, :]
        inside = (begins <= s[None, :, None]) & (s[None, :, None] < ends)
        shift = jnp.sum(jnp.where(inside, dst[lo:hi, None, :] - begins, 0), axis=2)
        valid = s[None, :] < tile_rows[lo:hi, None]
        parity = (jnp.arange(hi - lo, dtype=jnp.int32) % 2)[:, None]
        spare = data_rows + parity * _sorted_rows_per_tile(max(tile_tokens)) + s[None, :]
        scatter_dst = jnp.where(valid, s[None, :] + shift, spare).astype(jnp.int32)
        gather_src = jnp.where(valid, s[None, :] + shift, 0).astype(jnp.int32)
        off_v = jnp.broadcast_to(off_local[lo:hi, :, None].astype(F32), (hi - lo, N_EXPERTS, LANES))
        per_pass.append((scatter_dst, gather_src, off_v))
        lo = hi
    tiles_cum = jnp.cumsum(region // MOE_TILE)
    n_active = tiles_cum[-1]
    return per_pass, tiles_cum, n_active


def _max_sorted_rows(tile_counts, tile_tokens):
    rows = sum(n * (2 * tm + N_EXPERTS * (RUN_PAD - 1)) for n, tm in zip(tile_counts, tile_tokens))
    data_rows = (-(-rows // MOE_TILE) + N_EXPERTS) * MOE_TILE
    return data_rows, data_rows + 2 * _sorted_rows_per_tile(max(tile_tokens))


def _row(v):
    return v.reshape(1, -1).astype(F32)


def _per_head_rows(v):
    return jnp.broadcast_to(v.astype(F32)[:, None], (SSD_HEADS, SSD_Q))


def _pad_lanes(v):
    return jnp.pad(v.reshape(1, -1).astype(F32), ((0, 0), (0, LANES - v.size)))


def kernel(x_prompt, x_sample, p_prompt, p_sample, state_pool, state_conv, state_ssm, norm_mix, w_in, pool_lin_w, pool_lin_b, pool_scale, conv_w, conv_b, dt_bias, a_log, d_skip, ssd_norm, w_out, norm_ffn, router_grp_w, router_grp_b, router_exp_w, router_exp_b, exp_w_gate, exp_w_up, exp_w_down, norm_ple, ple_gate_w, ple_gate_b, ple_proj_w, norm_final):
    nb, seq, _ = x_prompt.shape
    ns = x_sample.shape[0]
    assert ns == LANES and x_sample.shape[1] == 1 and seq % MIX_ROWS == 0 and seq >= POOL_BUF

    w_in16 = jnp.pad(w_in[0], ((0, 0), (0, IN_COLS - w_in.shape[2]))).astype(BF16)
    w_out16 = w_out[0].astype(BF16)
    g_mix, g_ffn, g_ple, g_fin = _row(norm_mix[0]), _row(norm_ffn[0]), _row(norm_ple[0]), _row(norm_final)
    cw, cb = conv_w[0].astype(F32), _row(conv_b[0])
    dtb, alog = _pad_lanes(dt_bias[0]), _pad_lanes(a_log[0])
    dsk = _row(jnp.repeat(d_skip[0], SSD_HEAD_DIM))
    nw = _row(ssd_norm[0])
    plw16 = pool_lin_w[0].astype(BF16)
    plb, psc = _row(pool_lin_b[0]), _row(pool_scale[0])
    zeros4 = jnp.zeros((D_MODEL, 8 - N_EXPERT_GROUPS), F32)
    r_w = jnp.concatenate([router_grp_w[0], zeros4, router_exp_w[0],
                           jnp.zeros((D_MODEL, LANES - 8 - N_EXPERTS), F32)], axis=1).T
    r_hi = r_w.astype(BF16)
    r_lo = (r_w - r_hi.astype(F32)).astype(BF16)
    r_b = jnp.concatenate([router_grp_b[0], jnp.zeros((8 - N_EXPERT_GROUPS,), F32), router_exp_b[0],
                           jnp.zeros((LANES - 8 - N_EXPERTS,), F32)])
    wg = exp_w_gate[0].reshape(N_EXPERTS, D_MODEL, EXPERT_FF)
    wu = exp_w_up[0].reshape(N_EXPERTS, D_MODEL, EXPERT_FF)
    wd = exp_w_down[0].reshape(N_EXPERTS, EXPERT_FF, D_MODEL)
    pg16 = ple_gate_w[0].astype(BF16)
    pgb = _row(ple_gate_b[0])
    pp16 = ple_proj_w[0].astype(BF16)

    xp = x_prompt.reshape(nb * seq, D_MODEL)
    h1_p, u_p, rt_p, gate_p, cnt_p, st, pool_tail, conv_tail = _front_prompt(
        xp, g_mix, w_in16, cw, cb, _per_head_rows(dt_bias[0]), _per_head_rows(a_log[0]), dsk, nw, plw16, plb, psc,
        w_out16, g_ffn, r_hi, r_lo, jnp.broadcast_to(r_b[:, None], (LANES, MIX_ROWS)), nb, seq)
    pool_p = pool_tail[:, TAIL_ROWS - POOL_BUF:]
    conv_p = conv_tail[:, CONV_HDR - (SSD_CONV - 1):]
    ssm_p = st

    xs_ = x_sample.reshape(ns, D_MODEL)
    h1_s, u_s, rt_s, gate_s, cnt_s, pool_t, conv_t, ssm_t = _front_step(
        xs_, g_mix, w_in16, jnp.transpose(state_pool[0], (1, 0, 2)), jnp.transpose(state_conv[0], (1, 0, 2)),
        jnp.transpose(state_ssm[0], (1, 2, 3, 0)), cw, cb, dtb, alog, dsk, nw, plw16, plb, psc,
        w_out16, g_ffn, r_hi, r_lo, jnp.broadcast_to(r_b[:, None], (LANES, ns)))
    pool_s = jnp.transpose(pool_t, (1, 0, 2))
    conv_s = jnp.transpose(conv_t, (1, 0, 2))
    ssm_s = jnp.transpose(ssm_t, (3, 0, 1, 2))

    counts = [cnt_p[:, :, 0].astype(jnp.int32), cnt_s[:, :, 0].astype(jnp.int32)]
    tiles = (PROMPT_TILE, ns)
    data_rows, total_rows = _max_sorted_rows([c.shape[0] for c in counts], tiles)
    (tab_p, tab_s), tiles_cum, n_active = _sort_tables(counts, tiles, data_rows)
    xs_sorted, dest_p = _sort_tokens(tab_p[0], u_p, rt_p, tab_p[2], None, PROMPT_TILE, total_rows)
    xs_sorted, dest_s = _sort_tokens(tab_s[0], u_s, rt_s, tab_s[2], xs_sorted, ns, total_rows)
    tile_start = jnp.concatenate([jnp.zeros((1,), jnp.int32), tiles_cum.astype(jnp.int32)])
    ys_sorted = _moe_sorted(tile_start, n_active.reshape(1).astype(jnp.int32), xs_sorted, wg, wu, wd)

    y_prompt = _ple(tab_p[1], h1_p, dest_p, gate_p, p_prompt[0].reshape(nb * seq, PLE_DIM),
                    g_ple, pg16, pgb, pp16, g_fin, ys_sorted, PROMPT_TILE)
    y_sample = _ple(tab_s[1], h1_s, dest_s, gate_s, p_sample[0].reshape(ns, PLE_DIM),
                    g_ple, pg16, pgb, pp16, g_fin, ys_sorted, ns)

    return (y_prompt.reshape(nb, seq, D_MODEL), y_sample.reshape(ns, 1, D_MODEL),
            pool_p[None], conv_p[None], ssm_p[None], pool_s[None], conv_s[None], ssm_s[None])
```

```python
import functools

import jax
import jax.numpy as jnp
import numpy as np
from jax import lax
from jax.experimental import pallas as pl
from jax.experimental.pallas import tpu as pltpu

F32 = jnp.float32
BF16 = jnp.bfloat16

D_MODEL = 1024
POOL_WIDTH = 512
POOL_WINDOWS = (2, 4, 8, 16)
POOL_GROUP_DIM = 128
POOL_BUF = 15
SSD_WIDTH = 512
SSD_HEAD_DIM = 64
SSD_HEADS = 8
SSD_GROUPS = 2
SSD_STATE = 64
SSD_CONV = 4
SSD_BC = SSD_GROUPS * SSD_STATE
CONV_DIM = SSD_WIDTH + 2 * SSD_BC
N_EXPERT_GROUPS = 4
EXPERTS_PER_GROUP = 8
N_EXPERTS = N_EXPERT_GROUPS * EXPERTS_PER_GROUP
EXPERT_FF = 256
PLE_DIM = 256
PAST_LEN = 16384
EPS = 1e-6

LANES = 128
IN_COLS = 1920
DT_OFF = POOL_WIDTH + SSD_WIDTH + CONV_DIM
VMEM_LIMIT = 56 * 1024 * 1024


def _cparams(*sem):
    return pltpu.CompilerParams(dimension_semantics=sem, vmem_limit_bytes=VMEM_LIMIT)


def _rms(x, g):
    return x * lax.rsqrt(jnp.mean(x * x, axis=-1, keepdims=True) + EPS) * g


def _sigmoid(x):
    return 1.0 / (1.0 + jnp.exp(-x))


def _silu(x):
    return x * _sigmoid(x)


def _split3(v):
    hi = v.astype(BF16)
    r = v - hi.astype(F32)
    mid = r.astype(BF16)
    lo = (r - mid.astype(F32)).astype(BF16)
    return hi, mid, lo


def _dot(a, b):
    return jnp.dot(a, b, preferred_element_type=F32)


def _dot_nt(a, b):
    return lax.dot_general(a, b, (((1,), (1,)), ((), ())), preferred_element_type=F32)


def _dot_tn(a, b):
    return lax.dot_general(a, b, (((0,), (0,)), ((), ())), preferred_element_type=F32)


def _dot_exact_lhs(sel, v, terms=3):
    acc = None
    for t in _split3(v)[:terms]:
        p = _dot(sel, t)
        acc = p if acc is None else acc + p
    return acc


def _dot_exact_rhs(v, sel, terms=3):
    acc = None
    for t in _split3(v)[:terms]:
        p = _dot(t, sel)
        acc = p if acc is None else acc + p
    return acc


def _route(lg):
    tm = lg.shape[1]
    gl = lg[0:N_EXPERT_GROUPS, :]
    gmax = jnp.max(gl, axis=0, keepdims=True)
    gsum = jnp.sum(jnp.exp(gl - gmax), axis=0, keepdims=True)
    g_w = 1.0 / gsum
    gi = lax.broadcasted_iota(jnp.int32, gl.shape, 0)
    g_idx = jnp.min(jnp.where(gl == gmax, gi, N_EXPERT_GROUPS), axis=0, keepdims=True)
    sel = jnp.zeros((EXPERTS_PER_GROUP, tm), F32)
    for g in range(N_EXPERT_GROUPS):
        blk = lg[8 + g * EXPERTS_PER_GROUP:8 + (g + 1) * EXPERTS_PER_GROUP, :]
        sel = jnp.where(g_idx == g, blk, sel)
    ei = lax.broadcasted_iota(jnp.int32, sel.shape, 0)
    m1 = jnp.max(sel, axis=0, keepdims=True)
    i1 = jnp.min(jnp.where(sel == m1, ei, EXPERTS_PER_GROUP), axis=0, keepdims=True)
    rest = jnp.where(ei == i1, -jnp.inf, sel)
    m2 = jnp.max(rest, axis=0, keepdims=True)
    i2 = jnp.min(jnp.where(rest == m2, ei, EXPERTS_PER_GROUP), axis=0, keepdims=True)
    p2 = jnp.exp(m2 - m1)
    w1 = g_w / (1.0 + p2)
    w2 = g_w * p2 / (1.0 + p2)
    return g_idx * EXPERTS_PER_GROUP + i1, g_idx * EXPERTS_PER_GROUP + i2, w1, w2


def _out_proj_and_route(x, mix16, w_ref, g_ref, rh_ref, rl_ref, rb_ref, h_ref, u_ref, rt_ref, wt_ref, cnt_ref,
                        r0=0, accumulate=False):
    tm = x.shape[0]
    h = x + _dot(mix16, w_ref[...])
    h_ref[r0:r0 + tm, :] = h
    _route_rows(h, g_ref, rh_ref, rl_ref, rb_ref, u_ref, rt_ref, wt_ref, cnt_ref, r0=r0, accumulate=accumulate)


def _route_rows(h, g_ref, rh_ref, rl_ref, rb_ref, u_ref, rt_ref, wt_ref, cnt_ref, r0=0, accumulate=False):
    tm = h.shape[0]
    u = _rms(h, g_ref[...])
    u_hi = u.astype(BF16)
    u_ref[r0:r0 + tm, :] = u_hi
    u_lo = (u - u_hi.astype(F32)).astype(BF16)
    lg = (_dot_nt(rh_ref[...], u_hi) + _dot_nt(rh_ref[...], u_lo) + _dot_nt(rl_ref[...], u_hi)
          + rb_ref[:, r0:r0 + tm])
    b1, b2, w1, w2 = _route(lg)
    r8 = lax.broadcasted_iota(jnp.int32, (8, tm), 0)
    rt_ref[0, :, r0:r0 + tm] = jnp.where(r8 == 0, b1.astype(F32), jnp.where(r8 == 1, b2.astype(F32), 0.0))
    wt_ref[r0:r0 + tm, :] = jnp.concatenate(
        [jnp.broadcast_to(w1, (LANES, tm)).T, jnp.broadcast_to(w2, (LANES, tm)).T], axis=1)
    kio = lax.broadcasted_iota(jnp.int32, (N_EXPERTS, tm), 0)
    hits = ((kio == b1) | (kio == b2)).astype(F32)
    cnt = jnp.broadcast_to(jnp.sum(hits, axis=1, keepdims=True), (N_EXPERTS, LANES))
    cnt_ref[0] = cnt_ref[0] + cnt if accumulate else cnt


RUN_PAD = 16
MOE_TILE = 256
MOE_BUFS = 10
MOE_AHEAD = 8


def _tile_lanes(v, width):
    reps = width // LANES
    return v if reps == 1 else jnp.concatenate([v] * reps, axis=1)


def _sorted_rows_per_tile(tm):
    need = 2 * tm + N_EXPERTS * (RUN_PAD - 1)
    return -(-need // LANES) * LANES


def _sort_body(cd_ref, u_ref, rt_ref, off_ref, *rest):
    xs_ref, dt_ref, loc, sem = rest[-4:]
    i = pl.program_id(0)
    tm = u_ref.shape[0]
    rows = loc.shape[1]
    rt = rt_ref[0]
    b1 = rt[0:1, :].astype(jnp.int32)
    b2 = rt[1:2, :].astype(jnp.int32)
    kio = lax.broadcasted_iota(jnp.int32, (N_EXPERTS, tm), 0)
    o1 = kio == b1
    o2 = kio == b2
    before = (lax.broadcasted_iota(jnp.int32, (tm, tm), 0) < lax.broadcasted_iota(jnp.int32, (tm, tm), 1))
    start = _dot((o1 | o2).astype(BF16), before.astype(BF16)) + _tile_lanes(off_ref[0], tm)
    d1 = jnp.sum(jnp.where(o1, start, 0.0), axis=0, keepdims=True)
    d2 = jnp.sum(jnp.where(o2, start, 0.0), axis=0, keepdims=True)
    dt_ref[...] = jnp.concatenate([jnp.broadcast_to(d1, (LANES, tm)).T, jnp.broadcast_to(d2, (LANES, tm)).T], axis=1)
    rio = lax.broadcasted_iota(jnp.int32, (rows, tm), 0)
    perm = ((rio == d1.astype(jnp.int32)) | (rio == d2.astype(jnp.int32))).astype(BF16)
    slot = i % 2
    loc[slot] = _dot(perm, u_ref[...]).astype(BF16)

    def chunk_copy(step, sl, c):
        return pltpu.make_async_copy(
            loc.at[sl, pl.ds(c * RUN_PAD, RUN_PAD), :],
            xs_ref.at[pl.ds(pl.multiple_of(cd_ref[step, c], RUN_PAD), RUN_PAD), :], sem.at[sl])

    n_chunk = rows // RUN_PAD
    for c in range(n_chunk):
        chunk_copy(i, slot, c).start()

    @pl.when(i > 0)
    def _():
        for c in range(n_chunk):
            chunk_copy(i - 1, 1 - slot, c).wait()

    @pl.when(i == pl.num_programs(0) - 1)
    def _():
        for c in range(n_chunk):
            chunk_copy(i, slot, c).wait()


def _sort_tokens(chunk_dst, u16, rt, off_v, xs_prev, tm, total_rows):
    t = u16.shape[0]
    rows = _sorted_rows_per_tile(tm)
    in_specs = [pl.BlockSpec((tm, D_MODEL), lambda i, cd: (i, 0)),
                pl.BlockSpec((1, 8, tm), lambda i, cd: (i, 0, 0)),
                pl.BlockSpec((1, N_EXPERTS, LANES), lambda i, cd: (i, 0, 0))]
    args = [chunk_dst, u16, rt, off_v]
    aliases = {}
    if xs_prev is not None:
        in_specs.append(pl.BlockSpec(memory_space=pl.ANY))
        aliases = {len(args): 0}
        args.append(xs_prev)
    return pl.pallas_call(
        _sort_body,
        grid_spec=pltpu.PrefetchScalarGridSpec(
            num_scalar_prefetch=1,
            grid=(t // tm,),
            in_specs=in_specs,
            out_specs=[pl.BlockSpec(memory_space=pl.ANY),
                       pl.BlockSpec((tm, 2 * LANES), lambda i, cd: (i, 0))],
            scratch_shapes=[pltpu.VMEM((2, rows, D_MODEL), BF16), pltpu.SemaphoreType.DMA((2,))]),
        out_shape=[jax.ShapeDtypeStruct((total_rows, D_MODEL), BF16),
                   jax.ShapeDtypeStruct((t, 2 * LANES), F32)],
        input_output_aliases=aliases,
        compiler_params=_cparams("arbitrary"),
        name="sort_tokens",
    )(*args)


def _moe_body(ts_ref, na_ref, xs_ref, wg_ref, wu_ref, wd_ref, ys_ref, xbuf, ybuf, wg16, wu16, wd16, sem_in, sem_out):
    k = pl.program_id(0)
    n_act = na_ref[0]

    def x_copy(g, sl):
        return pltpu.make_async_copy(xs_ref.at[pl.ds(pl.multiple_of(g * MOE_TILE, MOE_TILE), MOE_TILE), :],
                                     xbuf.at[sl], sem_in.at[sl])

    def y_copy(g, sl):
        return pltpu.make_async_copy(ybuf.at[sl],
                                     ys_ref.at[pl.ds(pl.multiple_of(g * MOE_TILE, MOE_TILE), MOE_TILE), :],
                                     sem_out.at[sl])

    @pl.when(k == 0)
    def _():
        for j in range(MOE_AHEAD):
            @pl.when(j < n_act)
            def _():
                x_copy(j, j).start(priority=1)

    wg16[...] = wg_ref[0].astype(BF16)
    wu16[...] = wu_ref[0].astype(BF16)
    wd16[...] = wd_ref[0].astype(BF16)

    def arrive(g):
        sl = g % MOE_BUFS

        @pl.when(g + MOE_AHEAD < n_act)
        def _():
            x_copy(g + MOE_AHEAD, (g + MOE_AHEAD) % MOE_BUFS).start(priority=1)

        x_copy(g, sl).wait()

        @pl.when(g >= MOE_BUFS)
        def _():
            y_copy(g - MOE_BUFS, sl).wait()

    def tiles(g, count):
        for j in range(count):
            arrive(g + j)
        acts = []
        for j in range(count):
            x = xbuf[(g + j) % MOE_BUFS]
            acts.append((_silu(_dot(x, wg16[...])) * _dot(x, wu16[...])).astype(BF16))
        for j in range(count):
            ybuf[(g + j) % MOE_BUFS] = _dot(acts[j], wd16[...]).astype(BF16)
        for j in range(count):
            y_copy(g + j, (g + j) % MOE_BUFS).start()

    g0 = ts_ref[k]
    n_here = ts_ref[k + 1] - g0

    def pair(p, carry):
        tiles(g0 + 2 * p, 2)
        return carry

    lax.fori_loop(0, n_here // 2, pair, 0)

    @pl.when(n_here % 2 == 1)
    def _():
        tiles(g0 + n_here - 1, 1)

    @pl.when(k == pl.num_programs(0) - 1)
    def _():
        for j in range(1, MOE_BUFS + 1):
            @pl.when(n_act >= j)
            def _():
                y_copy(n_act - j, (n_act - j) % MOE_BUFS).wait()


def _moe_sorted(tile_start, n_active, xs, wg, wu, wd):
    w_map = lambda k, ts, na: (k, 0, 0)
    return pl.pallas_call(
        _moe_body,
        grid_spec=pltpu.PrefetchScalarGridSpec(
            num_scalar_prefetch=2,
            grid=(N_EXPERTS,),
            in_specs=[pl.BlockSpec(memory_space=pl.ANY),
                      pl.BlockSpec((1, D_MODEL, EXPERT_FF), w_map),
                      pl.BlockSpec((1, D_MODEL, EXPERT_FF), w_map),
                      pl.BlockSpec((1, EXPERT_FF, D_MODEL), w_map)],
            out_specs=pl.BlockSpec(memory_space=pl.ANY),
            scratch_shapes=[pltpu.VMEM((MOE_BUFS, MOE_TILE, D_MODEL), BF16),
                            pltpu.VMEM((MOE_BUFS, MOE_TILE, D_MODEL), BF16),
                            pltpu.VMEM((D_MODEL, EXPERT_FF), BF16), pltpu.VMEM((D_MODEL, EXPERT_FF), BF16),
                            pltpu.VMEM((EXPERT_FF, D_MODEL), BF16),
                            pltpu.SemaphoreType.DMA((MOE_BUFS,)), pltpu.SemaphoreType.DMA((MOE_BUFS,))]),
        out_shape=jax.ShapeDtypeStruct(xs.shape, BF16),
        compiler_params=_cparams("arbitrary"),
        name="moe_sorted",
    )(tile_start, n_active, xs, wg, wu, wd)


def _ple_body(cd_ref, h_ref, dt_ref, wt_ref, p_ref, gn_ref, wg_ref, bg_ref, wp_ref, fn_ref, ys_ref,
              o_ref, loc, sem):
    i = pl.program_id(0)
    tm = h_ref.shape[0]
    rows = loc.shape[1]
    n_chunk = rows // RUN_PAD
    slot = i % 2

    def chunk_copy(step, sl, c):
        return pltpu.make_async_copy(
            ys_ref.at[pl.ds(pl.multiple_of(cd_ref[step, c], RUN_PAD), RUN_PAD), :],
            loc.at[sl, pl.ds(c * RUN_PAD, RUN_PAD), :], sem.at[sl])

    def fetch(step, sl):
        for c in range(n_chunk):
            chunk_copy(step, sl, c).start()

    @pl.when(i == 0)
    def _():
        fetch(0, 0)

    @pl.when(i + 1 < pl.num_programs(0))
    def _():
        fetch(i + 1, 1 - slot)

    for c in range(n_chunk):
        chunk_copy(i, slot, c).wait()

    ci = lax.broadcasted_iota(jnp.int32, (tm, LANES), 1)
    d1 = dt_ref[:, 0:LANES].astype(jnp.int32)
    d2 = dt_ref[:, LANES:2 * LANES].astype(jnp.int32)
    w1 = wt_ref[:, 0:LANES]
    w2 = wt_ref[:, LANES:2 * LANES]
    sel = jnp.concatenate(
        [jnp.where(d1 == ci + m * LANES, w1, jnp.where(d2 == ci + m * LANES, w2, 0.0)) for m in range(rows // LANES)],
        axis=1).astype(BF16)
    h = h_ref[...] + _dot(sel, loc[slot])
    a16 = _rms(h, gn_ref[...]).astype(BF16)
    gate = _sigmoid(_dot(a16, wg_ref[...]) + bg_ref[...])
    pp = _dot(p_ref[...].astype(BF16), wp_ref[...])
    h = h + gate * pp
    o_ref[...] = _rms(h, fn_ref[...])


def _ple(chunk_src, h1, dest_t, gate_t, p, g_ple, wg16, bg, wp16, g_final, ys, tm):
    t = h1.shape[0]
    row = lambda i, cd: (i, 0)
    fixed = lambda i, cd: (0, 0)
    return pl.pallas_call(
        _ple_body,
        grid_spec=pltpu.PrefetchScalarGridSpec(
            num_scalar_prefetch=1,
            grid=(t // tm,),
            in_specs=[pl.BlockSpec((tm, D_MODEL), row), pl.BlockSpec((tm, 2 * LANES), row),
                      pl.BlockSpec((tm, 2 * LANES), row), pl.BlockSpec((tm, PLE_DIM), row),
                      pl.BlockSpec((1, D_MODEL), fixed), pl.BlockSpec((D_MODEL, D_MODEL), fixed),
                      pl.BlockSpec((1, D_MODEL), fixed), pl.BlockSpec((PLE_DIM, D_MODEL), fixed),
                      pl.BlockSpec((1, D_MODEL), fixed), pl.BlockSpec(memory_space=pl.ANY)],
            out_specs=pl.BlockSpec((tm, D_MODEL), row),
            scratch_shapes=[pltpu.VMEM((2, _sorted_rows_per_tile(tm), D_MODEL), BF16),
                            pltpu.SemaphoreType.DMA((2,))]),
        out_shape=jax.ShapeDtypeStruct((t, D_MODEL), F32),
        compiler_params=_cparams("arbitrary"),
        name="ple_final",
    )(chunk_src, h1, dest_t, gate_t, p, g_ple, wg16, bg, wp16, g_final, ys)


MIX_ROWS = 512
SSD_Q = 128
POOL_HDR = 128
CONV_HDR = 8
TAIL_ROWS = 16


def _softplus(x):
    return jnp.maximum(x, 0.0) + jnp.log1p(jnp.exp(-jnp.abs(x)))


def _stack_terms(v, terms):
    parts = [t.astype(F32) for t in _split3(v)[:terms]]
    parts.append(jnp.zeros(((4 - terms) * v.shape[0], v.shape[1]), F32))
    return jnp.concatenate(parts, axis=0).astype(BF16)


def _cumsum_lanes(v, tri16):
    r = _dot_nt(_stack_terms(v, 3), tri16)
    n = v.shape[0]
    return r[0:n] + r[n:2 * n] + r[2 * n:3 * n]


def _expand_heads(v, expand32, terms):
    return _dot_tn(_stack_terms(v, terms), expand32)


def _mixer_constants():
    q = SSD_Q
    li = np.arange(q)[:, None]
    tri = (li >= np.arange(q)[None, :])
    expand = (np.arange(SSD_WIDTH)[None, :] // SSD_HEAD_DIM) == (np.arange(4 * SSD_HEADS)[:, None] % SSD_HEADS)
    hpg = SSD_HEADS // SSD_GROUPS
    block = (np.arange(SSD_BC)[:, None] // SSD_STATE) == (np.arange(SSD_WIDTH)[None, :] // (SSD_HEAD_DIM * hpg))
    wj = np.arange(2 * q)[None, :]
    band = np.stack([(wj <= li + POOL_HDR) & (wj > li + POOL_HDR - w) for w in POOL_WINDOWS])
    inv_count = np.stack([np.broadcast_to(1.0 / np.minimum(li + 1, w), (q, LANES)) for w in POOL_WINDOWS])
    as_f32 = lambda m: jnp.asarray(m.astype(np.float32))
    return (as_f32(tri), as_f32(tri).astype(BF16), as_f32(expand).astype(BF16), as_f32(block),
            as_f32(band).astype(BF16), as_f32(inv_count))


def _front_body(steps_per_seq, x_ref, xp_ref, gm_ref, win_ref, trif_ref, tri_ref, exp_ref, bmask_ref, band_ref,
                invc_ref, cw_ref, cb_ref, dtb_ref, alog_ref, dsk_ref, nw_ref,
                plw_ref, plb_ref, psc_ref, wout_ref, gf_ref, rh_ref, rl_ref, rb_ref,
                h_ref, u_ref, rt_ref, wt_ref, cnt_ref, st_ref, ptail_ref, ctail_ref,
                pool_ext, conv_ext, s_ref, z_ref, dt_ref, mix_ref, mixp_ref):
    s = pl.program_id(0)
    n_tiles = pl.num_programs(0) - 1
    real = s < n_tiles
    c = jnp.minimum(s, n_tiles - 1) % steps_per_seq
    rows = MIX_ROWS
    q_len = SSD_Q

    @pl.when(s == 0)
    def _():
        mixp_ref[...] = jnp.zeros_like(mixp_ref)

    @pl.when(c == 0)
    def _():
        pool_ext[0:POOL_HDR, :] = jnp.zeros((POOL_HDR, POOL_WIDTH), F32)
        conv_ext[0:CONV_HDR, :] = jnp.zeros((CONV_HDR, CONV_DIM), F32)
        s_ref[...] = jnp.zeros_like(s_ref)

    @pl.when(c > 0)
    def _():
        pool_ext[0:POOL_HDR, :] = pool_ext[rows:rows + POOL_HDR, :]
        conv_ext[0:CONV_HDR, :] = conv_ext[rows:rows + CONV_HDR, :]

    causal = trif_ref[...] > 0.5
    tri = tri_ref[...]
    expand = exp_ref[...]
    blockmask = bmask_ref[...]
    lane = lax.broadcasted_iota(jnp.int32, (q_len, LANES), 1)
    left = lane < SSD_HEAD_DIM
    a_neg = -jnp.exp(alog_ref[...])

    half = rows // 2
    n_cols = 256

    def in_proj_chunks(r0):
        a16 = _rms(x_ref[r0:r0 + half, :], gm_ref[...]).astype(BF16)
        dests = ([(pool_ext, POOL_HDR + r0, k, 0) for k in range(0, POOL_WIDTH, n_cols)]
                 + [(z_ref, r0, k, POOL_WIDTH) for k in range(0, SSD_WIDTH, n_cols)]
                 + [(conv_ext, CONV_HDR + r0, k, POOL_WIDTH + SSD_WIDTH) for k in range(0, CONV_DIM, n_cols)])

        def chunk(ref, row, k, col0):
            def run():
                ref[row:row + half, k:k + n_cols] = _dot(a16, win_ref[:, col0 + k:col0 + k + n_cols])
            return run

        def dt_chunk():
            dt_ref[r0:r0 + half, :] = _dot(a16, win_ref[:, DT_OFF:IN_COLS])

        return [chunk(*d) for d in dests] + [dt_chunk]

    def out_proj_chunks(r0):
        def chunk(k):
            def run():
                h_ref[r0:r0 + half, k:k + n_cols] = (xp_ref[r0:r0 + half, k:k + n_cols]
                                                    + _dot(mixp_ref[r0:r0 + half, :], wout_ref[:, k:k + n_cols]))
            return run

        def route():
            _route_rows(h_ref[r0:r0 + half, :], gf_ref, rh_ref, rl_ref, rb_ref, u_ref, rt_ref, wt_ref, cnt_ref,
                        r0=r0, accumulate=r0 > 0)

        return [chunk(k) for k in range(0, D_MODEL, n_cols)] + [route]

    def mix_pair(q_first, fillers):
        fillers = list(fillers)

        def fill():
            if fillers:
                fillers.pop(0)()

        subs = range(q_first, q_first + 2)
        xs_l, b16_l, call_l, c16_l = {}, {}, {}, {}
        for q in subs:
            base = CONV_HDR + q * q_len
            conv = cb_ref[...] + conv_ext[base - 3:base - 3 + q_len, :] * cw_ref[0:1, :]
            conv = conv + conv_ext[base - 2:base - 2 + q_len, :] * cw_ref[1:2, :]
            conv = conv + conv_ext[base - 1:base - 1 + q_len, :] * cw_ref[2:3, :]
            conv = conv + conv_ext[base:base + q_len, :] * cw_ref[3:4, :]
            conv = _silu(conv)
            xs_l[q] = conv[:, 0:SSD_WIDTH]
            b16_l[q] = conv[:, SSD_WIDTH:SSD_WIDTH + SSD_BC].astype(BF16)
            call_l[q] = conv[:, SSD_WIDTH + SSD_BC:CONV_DIM]
            c16_l[q] = call_l[q].astype(BF16)
            fill()
        dtt_l = {q: _softplus(dt_ref[q * q_len:(q + 1) * q_len, :].T[0:SSD_HEADS, :] + dtb_ref[...]) for q in subs}
        acst_l = {q: _cumsum_lanes(dtt_l[q] * a_neg, tri) for q in subs}
        acs_l = {q: jnp.concatenate([acst_l[q], jnp.zeros((LANES - SSD_HEADS, q_len), F32)], axis=0).T
                 for q in subs}
        fill()
        dtx_l = {q: _expand_heads(dtt_l[q], expand, 2) for q in subs}
        acsx_l = {q: _expand_heads(acst_l[q], expand, 3) for q in subs}
        lastx_l = {q: acsx_l[q][q_len - 1:q_len, :] for q in subs}
        xdt_l = {q: xs_l[q] * dtx_l[q] for q in subs}
        xdt16_l = {q: xdt_l[q].astype(BF16) for q in subs}
        fill()
        contrib_l = {q: _dot_tn(b16_l[q], (xdt_l[q] * jnp.exp(lastx_l[q] - acsx_l[q])).astype(BF16)) * blockmask
                     for q in subs}
        cb_l = {q: [_dot_nt(jnp.where(lax.shift_right_logical(lane, 6) == g, call_l[q], 0.0).astype(BF16), b16_l[q])
                    for g in range(SSD_GROUPS)] for q in subs}
        fill()
        zs_l = {}
        for q in subs:
            zs = []
            for h in range(SSD_HEADS):
                blk = slice((h // 2) * LANES, (h // 2 + 1) * LANES)
                seg = acs_l[q][:, h:h + 1] - acst_l[q][h:h + 1, :]
                decay = jnp.where(causal, jnp.exp(seg), 0.0)
                scores = (cb_l[q][h // (SSD_HEADS // SSD_GROUPS)] * decay).astype(BF16)
                zs.append(_dot(scores, xdt16_l[q][:, blk]))
                if h % 4 == 3:
                    fill()
            zs_l[q] = zs
        yoff_l = {}
        for q in subs:
            s_old = s_ref[...]
            yoff_l[q] = _dot(c16_l[q], s_old.astype(BF16)) * jnp.exp(acsx_l[q])
            s_ref[...] = s_old * jnp.exp(lastx_l[q]) + contrib_l[q]
        fill()
        for q in subs:
            r0 = q * q_len
            z = z_ref[r0:r0 + q_len, :]
            y_blocks = []
            for j in range(SSD_HEADS // 2):
                blk = slice(j * LANES, (j + 1) * LANES)
                y = jnp.where(left, zs_l[q][2 * j], zs_l[q][2 * j + 1]) + yoff_l[q][:, blk]
                y = y + dsk_ref[:, blk] * xs_l[q][:, blk]
                y_blocks.append(y * _silu(z[:, blk]))
            for g in range(SSD_GROUPS):
                y0, y1 = y_blocks[2 * g], y_blocks[2 * g + 1]
                ss = jnp.sum(y0 * y0, axis=-1, keepdims=True) + jnp.sum(y1 * y1, axis=-1, keepdims=True)
                rs = lax.rsqrt(ss * (1.0 / (2 * LANES)) + EPS)
                for k, yk in ((2 * g, y0), (2 * g + 1, y1)):
                    blk = slice(k * LANES, (k + 1) * LANES)
                    out = yk * rs * nw_ref[:, blk]
                    mix_ref[r0:r0 + q_len, POOL_WIDTH + k * LANES:POOL_WIDTH + (k + 1) * LANES] = out.astype(BF16)
            fill()
        for q in subs:
            r0 = q * q_len
            for g, w in enumerate(POOL_WINDOWS):
                blk = slice(g * POOL_GROUP_DIM, (g + 1) * POOL_GROUP_DIM)
                pe = pool_ext[r0:r0 + 2 * q_len, blk]
                hi = pe.astype(BF16)
                lo = (pe - hi.astype(F32)).astype(BF16)
                both = _dot(band_ref[g], jnp.concatenate([hi, lo], axis=1))
                winsum = both[:, 0:POOL_GROUP_DIM] + both[:, POOL_GROUP_DIM:2 * POOL_GROUP_DIM]
                inv = jnp.where(c == 0, invc_ref[g], 1.0 / w) if q == 0 else 1.0 / w
                m = winsum * inv - pe[q_len:2 * q_len, :]
                yg = _dot(m.astype(BF16), plw_ref[g]) + plb_ref[:, blk]
                mix_ref[r0:r0 + q_len, blk] = (yg * psc_ref[:, blk]).astype(BF16)
            fill()
        while fillers:
            fill()

    @pl.when(real)
    def _():
        for run in in_proj_chunks(0):
            run()
        mix_pair(0, in_proj_chunks(half) + out_proj_chunks(0))
        mix_pair(2, out_proj_chunks(half))
        mixp_ref[...] = mix_ref[...]

    @pl.when(jnp.logical_not(real))
    def _():
        for run in out_proj_chunks(0) + out_proj_chunks(half):
            run()

    @pl.when(real & (c == steps_per_seq - 1))
    def _():
        hpg = SSD_HEADS // SSD_GROUPS
        for j in range(SSD_HEADS // 2):
            flipped = s_ref[:, j * LANES:(j + 1) * LANES].T
            g = (2 * j) // hpg
            for k in range(2):
                st_ref[0, 2 * j + k] = flipped[k * SSD_HEAD_DIM:(k + 1) * SSD_HEAD_DIM,
                                               g * SSD_STATE:(g + 1) * SSD_STATE]
        ptail_ref[0] = pool_ext[POOL_HDR + rows - TAIL_ROWS:POOL_HDR + rows, :]
        ctail_ref[0] = conv_ext[rows:rows + CONV_HDR, :]


def _front_prompt(x, g_mix, w_in16, cw, cb, dtb, alog, dsk, nw, plw16, plb, psc, w_out16, g_ffn, r_hi, r_lo, r_bias,
                  nb, seq):
    steps = seq // MIX_ROWS
    nt = nb * steps
    cur = lambda s: (jnp.minimum(s, nt - 1), 0)
    prev = lambda s: (jnp.maximum(s - 1, 0), 0)
    prev3 = lambda s: (jnp.maximum(s - 1, 0), 0, 0)
    per_seq = lambda s: (jnp.minimum(s, nt - 1) // steps, 0, 0)
    fixed2 = lambda s: (0, 0)
    fixed3 = lambda s: (0, 0, 0)
    return pl.pallas_call(
        functools.partial(_front_body, steps),
        grid=(nt + 1,),
        in_specs=[pl.BlockSpec((MIX_ROWS, D_MODEL), cur), pl.BlockSpec((MIX_ROWS, D_MODEL), prev),
                  pl.BlockSpec((1, D_MODEL), fixed2), pl.BlockSpec((D_MODEL, IN_COLS), fixed2),
                  pl.BlockSpec((SSD_Q, SSD_Q), fixed2), pl.BlockSpec((SSD_Q, SSD_Q), fixed2),
                  pl.BlockSpec((4 * SSD_HEADS, SSD_WIDTH), fixed2), pl.BlockSpec((SSD_BC, SSD_WIDTH), fixed2),
                  pl.BlockSpec((len(POOL_WINDOWS), SSD_Q, 2 * SSD_Q), fixed3),
                  pl.BlockSpec((len(POOL_WINDOWS), SSD_Q, LANES), fixed3),
                  pl.BlockSpec((SSD_CONV, CONV_DIM), fixed2), pl.BlockSpec((1, CONV_DIM), fixed2),
                  pl.BlockSpec((SSD_HEADS, SSD_Q), fixed2), pl.BlockSpec((SSD_HEADS, SSD_Q), fixed2),
                  pl.BlockSpec((1, SSD_WIDTH), fixed2), pl.BlockSpec((1, SSD_WIDTH), fixed2),
                  pl.BlockSpec((len(POOL_WINDOWS), POOL_GROUP_DIM, POOL_GROUP_DIM), fixed3),
                  pl.BlockSpec((1, POOL_WIDTH), fixed2), pl.BlockSpec((1, POOL_WIDTH), fixed2),
                  pl.BlockSpec((D_MODEL, D_MODEL), fixed2), pl.BlockSpec((1, D_MODEL), fixed2),
                  pl.BlockSpec((LANES, D_MODEL), fixed2), pl.BlockSpec((LANES, D_MODEL), fixed2),
                  pl.BlockSpec((LANES, MIX_ROWS), fixed2)],
        out_specs=[pl.BlockSpec((MIX_ROWS, D_MODEL), prev), pl.BlockSpec((MIX_ROWS, D_MODEL), prev),
                   pl.BlockSpec((1, 8, MIX_ROWS), prev3), pl.BlockSpec((MIX_ROWS, 2 * LANES), prev),
                   pl.BlockSpec((1, N_EXPERTS, LANES), prev3),
                   pl.BlockSpec((1, SSD_HEADS, SSD_HEAD_DIM, SSD_STATE),
                                lambda s: (jnp.minimum(s, nt - 1) // steps, 0, 0, 0)),
                   pl.BlockSpec((1, TAIL_ROWS, POOL_WIDTH), per_seq),
                   pl.BlockSpec((1, CONV_HDR, CONV_DIM), per_seq)],
        out_shape=[jax.ShapeDtypeStruct((nb * seq, D_MODEL), F32), jax.ShapeDtypeStruct((nb * seq, D_MODEL), BF16),
                   jax.ShapeDtypeStruct((nt, 8, MIX_ROWS), F32), jax.ShapeDtypeStruct((nb * seq, 2 * LANES), F32),
                   jax.ShapeDtypeStruct((nt, N_EXPERTS, LANES), F32),
                   jax.ShapeDtypeStruct((nb, SSD_HEADS, SSD_HEAD_DIM, SSD_STATE), F32),
                   jax.ShapeDtypeStruct((nb, TAIL_ROWS, POOL_WIDTH), F32),
                   jax.ShapeDtypeStruct((nb, CONV_HDR, CONV_DIM), F32)],
        scratch_shapes=[pltpu.VMEM((POOL_HDR + MIX_ROWS, POOL_WIDTH), F32),
                        pltpu.VMEM((CONV_HDR + MIX_ROWS, CONV_DIM), F32),
                        pltpu.VMEM((SSD_BC, SSD_WIDTH), F32),
                        pltpu.VMEM((MIX_ROWS, SSD_WIDTH), F32), pltpu.VMEM((MIX_ROWS, LANES), F32),
                        pltpu.VMEM((MIX_ROWS, D_MODEL), BF16), pltpu.VMEM((MIX_ROWS, D_MODEL), BF16)],
        compiler_params=_cparams("arbitrary"),
        name="front_prompt",
    )(x, x, g_mix, w_in16, *_mixer_constants(), cw, cb, dtb, alog, dsk, nw, plw16, plb, psc, w_out16, g_ffn,
      r_hi, r_lo, r_bias)


def _mix_step_body(x_ref, gm_ref, win_ref, sp_ref, sc_ref, st_ref, cw_ref, cb_ref, dtb_ref, alog_ref,
                   dsk_ref, nw_ref, plw_ref, plb_ref, psc_ref, wout_ref, gf_ref, rh_ref, rl_ref, rb_ref,
                   h_ref, u_ref, rt_ref, wt_ref, cnt_ref, po_ref, co_ref, so_ref,
                   xdt_t, dec_t, b_t, c_t, xs_keep, y_t, z_ref, mix_ref):
    h = pl.program_id(0)

    @pl.when(h == 0)
    def _():
        a16 = _rms(x_ref[...], gm_ref[...]).astype(BF16)
        vp_new = _dot(a16, win_ref[:, 0:POOL_WIDTH])
        z_ref[...] = _dot(a16, win_ref[:, POOL_WIDTH:POOL_WIDTH + SSD_WIDTH])
        xbc = _dot(a16, win_ref[:, POOL_WIDTH + SSD_WIDTH:DT_OFF])
        dt_raw = _dot(a16, win_ref[:, DT_OFF:IN_COLS])
        conv = cb_ref[...] + sc_ref[0] * cw_ref[0:1, :]
        conv = conv + sc_ref[1] * cw_ref[1:2, :]
        conv = conv + sc_ref[2] * cw_ref[2:3, :]
        conv = conv + xbc * cw_ref[3:4, :]
        conv = _silu(conv)
        co_ref[0] = sc_ref[1]
        co_ref[1] = sc_ref[2]
        co_ref[2] = xbc
        xs = conv[:, 0:SSD_WIDTH]
        xs_keep[...] = xs
        b_t[...] = conv[:, SSD_WIDTH:SSD_WIDTH + SSD_BC].T
        c_t[...] = conv[:, SSD_WIDTH + SSD_BC:CONV_DIM].T
        dt = _softplus(dt_raw + dtb_ref[...])
        d_a = dt * (-jnp.exp(alog_ref[...]))
        dt_t = dt.T
        dec_t[...] = jnp.exp(d_a).T
        xs_t = xs.T
        for k in range(SSD_HEADS):
            blk = slice(k * SSD_HEAD_DIM, (k + 1) * SSD_HEAD_DIM)
            xdt_t[blk, :] = xs_t[blk, :] * dt_t[k:k + 1, :]
        v = vp_new
        for k in range(POOL_BUF - 1):
            po_ref[k] = sp_ref[k + 1]
        po_ref[POOL_BUF - 1] = v
        for g, w in enumerate(POOL_WINDOWS):
            blk = slice(g * POOL_GROUP_DIM, (g + 1) * POOL_GROUP_DIM)
            acc = sp_ref[POOL_BUF - (w - 1), :, blk]
            for k in range(w - 2, 0, -1):
                acc = acc + sp_ref[POOL_BUF - k, :, blk]
            acc = acc + v[:, blk]
            m = acc / float(min(PAST_LEN + 1, w)) - v[:, blk]
            yg = _dot(m.astype(BF16), plw_ref[g]) + plb_ref[:, blk]
            mix_ref[:, blk] = (yg * psc_ref[:, blk]).astype(BF16)

    g_off = pl.multiple_of((h // (SSD_HEADS // SSD_GROUPS)) * SSD_STATE, SSD_STATE)
    h_off = pl.multiple_of(h * SSD_HEAD_DIM, SSD_HEAD_DIM)
    b_g = b_t[pl.ds(g_off, SSD_STATE), :]
    c_g = c_t[pl.ds(g_off, SSD_STATE), :]
    dec = dec_t[pl.ds(h, 1), :]
    xdt = xdt_t[pl.ds(h_off, SSD_HEAD_DIM), :]
    y_rows = []
    for p in range(SSD_HEAD_DIM):
        s_new = st_ref[0, p] * dec + xdt[p:p + 1, :] * b_g
        so_ref[0, p] = s_new
        y_rows.append(jnp.sum(s_new * c_g, axis=0, keepdims=True))
    y_t[pl.ds(h_off, SSD_HEAD_DIM), :] = jnp.concatenate(y_rows, axis=0)

    @pl.when(h == pl.num_programs(0) - 1)
    def _():
        xs = xs_keep[...]
        y = y_t[...].T + dsk_ref[...] * xs
        y = y * _silu(z_ref[...])
        width = SSD_WIDTH // SSD_GROUPS
        for g in range(SSD_GROUPS):
            blk = slice(g * width, (g + 1) * width)
            yg = y[:, blk]
            rs = lax.rsqrt(jnp.mean(yg * yg, axis=-1, keepdims=True) + EPS)
            mix_ref[:, POOL_WIDTH + g * width:POOL_WIDTH + (g + 1) * width] = (yg * rs * nw_ref[:, blk]).astype(BF16)
        _out_proj_and_route(x_ref[...], mix_ref[...], wout_ref, gf_ref, rh_ref, rl_ref, rb_ref,
                            h_ref, u_ref, rt_ref, wt_ref, cnt_ref)


def _front_step(x, g_mix, w_in16, pool_t, conv_t, ssm_t, cw, cb, dtb, alog, dsk, nw, plw16, plb, psc,
                w_out16, g_ffn, r_hi, r_lo, r_bias):
    n = x.shape[0]
    fixed2 = lambda h: (0, 0)
    fixed3 = lambda h: (0, 0, 0)
    st_spec = pl.BlockSpec((1, SSD_HEAD_DIM, SSD_STATE, n), lambda h: (h, 0, 0, 0))
    return pl.pallas_call(
        _mix_step_body,
        grid=(SSD_HEADS,),
        in_specs=[pl.BlockSpec((n, D_MODEL), fixed2), pl.BlockSpec((1, D_MODEL), fixed2),
                  pl.BlockSpec((D_MODEL, IN_COLS), fixed2),
                  pl.BlockSpec((POOL_BUF, n, POOL_WIDTH), fixed3),
                  pl.BlockSpec((SSD_CONV - 1, n, CONV_DIM), fixed3),
                  st_spec,
                  pl.BlockSpec((SSD_CONV, CONV_DIM), fixed2), pl.BlockSpec((1, CONV_DIM), fixed2),
                  pl.BlockSpec((1, LANES), fixed2), pl.BlockSpec((1, LANES), fixed2),
                  pl.BlockSpec((1, SSD_WIDTH), fixed2), pl.BlockSpec((1, SSD_WIDTH), fixed2),
                  pl.BlockSpec((len(POOL_WINDOWS), POOL_GROUP_DIM, POOL_GROUP_DIM), fixed3),
                  pl.BlockSpec((1, POOL_WIDTH), fixed2), pl.BlockSpec((1, POOL_WIDTH), fixed2),
                  pl.BlockSpec((D_MODEL, D_MODEL), fixed2), pl.BlockSpec((1, D_MODEL), fixed2),
                  pl.BlockSpec((LANES, D_MODEL), fixed2), pl.BlockSpec((LANES, D_MODEL), fixed2),
                  pl.BlockSpec((LANES, n), fixed2)],
        out_specs=[pl.BlockSpec((n, D_MODEL), fixed2), pl.BlockSpec((n, D_MODEL), fixed2),
                   pl.BlockSpec((1, 8, n), fixed3), pl.BlockSpec((n, 2 * LANES), fixed2),
                   pl.BlockSpec((1, N_EXPERTS, LANES), fixed3),
                   pl.BlockSpec((POOL_BUF, n, POOL_WIDTH), fixed3),
                   pl.BlockSpec((SSD_CONV - 1, n, CONV_DIM), fixed3), st_spec],
        out_shape=[jax.ShapeDtypeStruct((n, D_MODEL), F32), jax.ShapeDtypeStruct((n, D_MODEL), BF16),
                   jax.ShapeDtypeStruct((1, 8, n), F32), jax.ShapeDtypeStruct((n, 2 * LANES), F32),
                   jax.ShapeDtypeStruct((1, N_EXPERTS, LANES), F32),
                   jax.ShapeDtypeStruct(pool_t.shape, F32),
                   jax.ShapeDtypeStruct(conv_t.shape, F32), jax.ShapeDtypeStruct(ssm_t.shape, F32)],
        scratch_shapes=[pltpu.VMEM((SSD_WIDTH, n), F32), pltpu.VMEM((LANES, n), F32),
                        pltpu.VMEM((SSD_BC, n), F32), pltpu.VMEM((SSD_BC, n), F32),
                        pltpu.VMEM((n, SSD_WIDTH), F32), pltpu.VMEM((SSD_WIDTH, n), F32),
                        pltpu.VMEM((n, SSD_WIDTH), F32), pltpu.VMEM((n, D_MODEL), BF16)],
        compiler_params=_cparams("arbitrary"),
        name="front_step",
    )(x, g_mix, w_in16, pool_t, conv_t, ssm_t, cw, cb, dtb, alog, dsk, nw, plw16, plb, psc,
      w_out16, g_ffn, r_hi, r_lo, r_bias)


PROMPT_TILE = 512


def _sort_tables(counts, tile_tokens, data_rows):
    cnt = jnp.concatenate(counts, axis=0)
    pc = (cnt + RUN_PAD - 1) // RUN_PAD * RUN_PAD
    off_local = jnp.cumsum(pc, axis=1) - pc
    tile_rows = jnp.sum(pc, axis=1)
    region = (jnp.sum(pc, axis=0) + MOE_TILE - 1) // MOE_TILE * MOE_TILE
    base = jnp.cumsum(region) - region
    dst = base[None, :] + jnp.cumsum(pc, axis=0) - pc
    per_pass = []
    lo = 0
    for c, tm in zip(counts, tile_tokens):
        hi = lo + c.shape[0]
        n_chunk = _sorted_rows_per_tile(tm) // RUN_PAD
        s = jnp.arange(n_chunk, dtype=jnp.int32) * RUN_PAD
        begins = off_local[lo:hi, None, :]
        ends = begins + pc[lo:hi, None, :]
        inside = (begins <= s[None, :, None]) & (s[None, :, None] < ends)
        shift = jnp.sum(jnp.where(inside, dst[lo:hi, None, :] - begins, 0), axis=2)
        valid = s[None, :] < tile_rows[lo:hi, None]
        parity = (jnp.arange(hi - lo, dtype=jnp.int32) % 2)[:, None]
        spare = data_rows + parity * _sorted_rows_per_tile(max(tile_tokens)) + s[None, :]
        scatter_dst = jnp.where(valid, s[None, :] + shift, spare).astype(jnp.int32)
        gather_src = jnp.where(valid, s[None, :] + shift, 0).astype(jnp.int32)
        off_v = jnp.broadcast_to(off_local[lo:hi, :, None].astype(F32), (hi - lo, N_EXPERTS, LANES))
        per_pass.append((scatter_dst, gather_src, off_v))
        lo = hi
    tiles_cum = jnp.cumsum(region // MOE_TILE)
    n_active = tiles_cum[-1]
    return per_pass, tiles_cum, n_active


def _max_sorted_rows(tile_counts, tile_tokens):
    rows = sum(n * (2 * tm + N_EXPERTS * (RUN_PAD - 1)) for n, tm in zip(tile_counts, tile_tokens))
    data_rows = (-(-rows // MOE_TILE) + N_EXPERTS) * MOE_TILE
    return data_rows, data_rows + 2 * _sorted_rows_per_tile(max(tile_tokens))


def _row(v):
    return v.reshape(1, -1).astype(F32)


def _per_head_rows(v):
    return jnp.broadcast_to(v.astype(F32)[:, None], (SSD_HEADS, SSD_Q))


def _pad_lanes(v):
    return jnp.pad(v.reshape(1, -1).astype(F32), ((0, 0), (0, LANES - v.size)))


def kernel(x_prompt, x_sample, p_prompt, p_sample, state_pool, state_conv, state_ssm, norm_mix, w_in, pool_lin_w, pool_lin_b, pool_scale, conv_w, conv_b, dt_bias, a_log, d_skip, ssd_norm, w_out, norm_ffn, router_grp_w, router_grp_b, router_exp_w, router_exp_b, exp_w_gate, exp_w_up, exp_w_down, norm_ple, ple_gate_w, ple_gate_b, ple_proj_w, norm_final):
    nb, seq, _ = x_prompt.shape
    ns = x_sample.shape[0]
    assert ns == LANES and x_sample.shape[1] == 1 and seq % MIX_ROWS == 0 and seq >= POOL_BUF

    w_in16 = jnp.pad(w_in[0], ((0, 0), (0, IN_COLS - w_in.shape[2]))).astype(BF16)
    w_out16 = w_out[0].astype(BF16)
    g_mix, g_ffn, g_ple, g_fin = _row(norm_mix[0]), _row(norm_ffn[0]), _row(norm_ple[0]), _row(norm_final)
    cw, cb = conv_w[0].astype(F32), _row(conv_b[0])
    dtb, alog = _pad_lanes(dt_bias[0]), _pad_lanes(a_log[0])
    dsk = _row(jnp.repeat(d_skip[0], SSD_HEAD_DIM))
    nw = _row(ssd_norm[0])
    plw16 = pool_lin_w[0].astype(BF16)
    plb, psc = _row(pool_lin_b[0]), _row(pool_scale[0])
    zeros4 = jnp.zeros((D_MODEL, 8 - N_EXPERT_GROUPS), F32)
    r_w = jnp.concatenate([router_grp_w[0], zeros4, router_exp_w[0],
                           jnp.zeros((D_MODEL, LANES - 8 - N_EXPERTS), F32)], axis=1).T
    r_hi = r_w.astype(BF16)
    r_lo = (r_w - r_hi.astype(F32)).astype(BF16)
    r_b = jnp.concatenate([router_grp_b[0], jnp.zeros((8 - N_EXPERT_GROUPS,), F32), router_exp_b[0],
                           jnp.zeros((LANES - 8 - N_EXPERTS,), F32)])
    wg = exp_w_gate[0].reshape(N_EXPERTS, D_MODEL, EXPERT_FF)
    wu = exp_w_up[0].reshape(N_EXPERTS, D_MODEL, EXPERT_FF)
    wd = exp_w_down[0].reshape(N_EXPERTS, EXPERT_FF, D_MODEL)
    pg16 = ple_gate_w[0].astype(BF16)
    pgb = _row(ple_gate_b[0])
    pp16 = ple_proj_w[0].astype(BF16)

    xp = x_prompt.reshape(nb * seq, D_MODEL)
    h1_p, u_p, rt_p, gate_p, cnt_p, st, pool_tail, conv_tail = _front_prompt(
        xp, g_mix, w_in16, cw, cb, _per_head_rows(dt_bias[0]), _per_head_rows(a_log[0]), dsk, nw, plw16, plb, psc,
        w_out16, g_ffn, r_hi, r_lo, jnp.broadcast_to(r_b[:, None], (LANES, MIX_ROWS)), nb, seq)
    pool_p = pool_tail[:, TAIL_ROWS - POOL_BUF:]
    conv_p = conv_tail[:, CONV_HDR - (SSD_CONV - 1):]
    ssm_p = st

    xs_ = x_sample.reshape(ns, D_MODEL)
    h1_s, u_s, rt_s, gate_s, cnt_s, pool_t, conv_t, ssm_t = _front_step(
        xs_, g_mix, w_in16, jnp.transpose(state_pool[0], (1, 0, 2)), jnp.transpose(state_conv[0], (1, 0, 2)),
        jnp.transpose(state_ssm[0], (1, 2, 3, 0)), cw, cb, dtb, alog, dsk, nw, plw16, plb, psc,
        w_out16, g_ffn, r_hi, r_lo, jnp.broadcast_to(r_b[:, None], (LANES, ns)))
    pool_s = jnp.transpose(pool_t, (1, 0, 2))
    conv_s = jnp.transpose(conv_t, (1, 0, 2))
    ssm_s = jnp.transpose(ssm_t, (3, 0, 1, 2))

    counts = [cnt_p[:, :, 0].astype(jnp.int32), cnt_s[:, :, 0].astype(jnp.int32)]
    tiles = (PROMPT_TILE, ns)
    data_rows, total_rows = _max_sorted_rows([c.shape[0] for c in counts], tiles)
    (tab_p, tab_s), tiles_cum, n_active = _sort_tables(counts, tiles, data_rows)
    xs_sorted, dest_p = _sort_tokens(tab_p[0], u_p, rt_p, tab_p[2], None, PROMPT_TILE, total_rows)
    xs_sorted, dest_s = _sort_tokens(tab_s[0], u_s, rt_s, tab_s[2], xs_sorted, ns, total_rows)
    tile_start = jnp.concatenate([jnp.zeros((1,), jnp.int32), tiles_cum.astype(jnp.int32)])
    ys_sorted = _moe_sorted(tile_start, n_active.reshape(1).astype(jnp.int32), xs_sorted, wg, wu, wd)

    y_prompt = _ple(tab_p[1], h1_p, dest_p, gate_p, p_prompt[0].reshape(nb * seq, PLE_DIM),
                    g_ple, pg16, pgb, pp16, g_fin, ys_sorted, PROMPT_TILE)
    y_sample = _ple(tab_s[1], h1_s, dest_s, gate_s, p_sample[0].reshape(ns, PLE_DIM),
                    g_ple, pg16, pgb, pp16, g_fin, ys_sorted, ns)

    return (y_prompt.reshape(nb, seq, D_MODEL), y_sample.reshape(ns, 1, D_MODEL),
            pool_p[None], conv_p[None], ssm_p[None], pool_s[None], conv_s[None], ssm_s[None])
```

```python
import functools

import jax
import jax.numpy as jnp
import numpy as np
from jax import lax
from jax.experimental import pallas as pl
from jax.experimental.pallas import tpu as pltpu

F32 = jnp.float32
BF16 = jnp.bfloat16

D_MODEL = 1024
POOL_WIDTH = 512
POOL_WINDOWS = (2, 4, 8, 16)
POOL_GROUP_DIM = 128
POOL_BUF = 15
SSD_WIDTH = 512
SSD_HEAD_DIM = 64
SSD_HEADS = 8
SSD_GROUPS = 2
SSD_STATE = 64
SSD_CONV = 4
SSD_BC = SSD_GROUPS * SSD_STATE
CONV_DIM = SSD_WIDTH + 2 * SSD_BC
N_EXPERT_GROUPS = 4
EXPERTS_PER_GROUP = 8
N_EXPERTS = N_EXPERT_GROUPS * EXPERTS_PER_GROUP
EXPERT_FF = 256
PLE_DIM = 256
PAST_LEN = 16384
EPS = 1e-6

LANES = 128
IN_COLS = 1920
DT_OFF = POOL_WIDTH + SSD_WIDTH + CONV_DIM
VMEM_LIMIT = 56 * 1024 * 1024


def _cparams(*sem):
    return pltpu.CompilerParams(dimension_semantics=sem, vmem_limit_bytes=VMEM_LIMIT)


def _rms(x, g):
    return x * lax.rsqrt(jnp.mean(x * x, axis=-1, keepdims=True) + EPS) * g


def _sigmoid(x):
    return 1.0 / (1.0 + jnp.exp(-x))


def _silu(x):
    return x * _sigmoid(x)


def _split3(v):
    hi = v.astype(BF16)
    r = v - hi.astype(F32)
    mid = r.astype(BF16)
    lo = (r - mid.astype(F32)).astype(BF16)
    return hi, mid, lo


def _dot(a, b):
    return jnp.dot(a, b, preferred_element_type=F32)


def _dot_nt(a, b):
    return lax.dot_general(a, b, (((1,), (1,)), ((), ())), preferred_element_type=F32)


def _dot_tn(a, b):
    return lax.dot_general(a, b, (((0,), (0,)), ((), ())), preferred_element_type=F32)


def _dot_exact_lhs(sel, v, terms=3):
    acc = None
    for t in _split3(v)[:terms]:
        p = _dot(sel, t)
        acc = p if acc is None else acc + p
    return acc


def _dot_exact_rhs(v, sel, terms=3):
    acc = None
    for t in _split3(v)[:terms]:
        p = _dot(t, sel)
        acc = p if acc is None else acc + p
    return acc


def _route(lg):
    tm = lg.shape[1]
    gl = lg[0:N_EXPERT_GROUPS, :]
    gmax = jnp.max(gl, axis=0, keepdims=True)
    gsum = jnp.sum(jnp.exp(gl - gmax), axis=0, keepdims=True)
    g_w = 1.0 / gsum
    gi = lax.broadcasted_iota(jnp.int32, gl.shape, 0)
    g_idx = jnp.min(jnp.where(gl == gmax, gi, N_EXPERT_GROUPS), axis=0, keepdims=True)
    sel = jnp.zeros((EXPERTS_PER_GROUP, tm), F32)
    for g in range(N_EXPERT_GROUPS):
        blk = lg[8 + g * EXPERTS_PER_GROUP:8 + (g + 1) * EXPERTS_PER_GROUP, :]
        sel = jnp.where(g_idx == g, blk, sel)
    ei = lax.broadcasted_iota(jnp.int32, sel.shape, 0)
    m1 = jnp.max(sel, axis=0, keepdims=True)
    i1 = jnp.min(jnp.where(sel == m1, ei, EXPERTS_PER_GROUP), axis=0, keepdims=True)
    rest = jnp.where(ei == i1, -jnp.inf, sel)
    m2 = jnp.max(rest, axis=0, keepdims=True)
    i2 = jnp.min(jnp.where(rest == m2, ei, EXPERTS_PER_GROUP), axis=0, keepdims=True)
    p2 = jnp.exp(m2 - m1)
    w1 = g_w / (1.0 + p2)
    w2 = g_w * p2 / (1.0 + p2)
    return g_idx * EXPERTS_PER_GROUP + i1, g_idx * EXPERTS_PER_GROUP + i2, w1, w2


def _out_proj_and_route(x, mix16, w_ref, g_ref, rh_ref, rl_ref, rb_ref, h_ref, u_ref, rt_ref, wt_ref, cnt_ref,
                        r0=0, accumulate=False):
    tm = x.shape[0]
    h = x + _dot(mix16, w_ref[...])
    h_ref[r0:r0 + tm, :] = h
    _route_rows(h, g_ref, rh_ref, rl_ref, rb_ref, u_ref, rt_ref, wt_ref, cnt_ref, r0=r0, accumulate=accumulate)


def _route_rows(h, g_ref, rh_ref, rl_ref, rb_ref, u_ref, rt_ref, wt_ref, cnt_ref, r0=0, accumulate=False):
    tm = h.shape[0]
    u = _rms(h, g_ref[...])
    u_hi = u.astype(BF16)
    u_ref[r0:r0 + tm, :] = u_hi
    u_lo = (u - u_hi.astype(F32)).astype(BF16)
    lg = (_dot_nt(rh_ref[...], u_hi) + _dot_nt(rh_ref[...], u_lo) + _dot_nt(rl_ref[...], u_hi)
          + rb_ref[:, r0:r0 + tm])
    b1, b2, w1, w2 = _route(lg)
    r8 = lax.broadcasted_iota(jnp.int32, (8, tm), 0)
    rt_ref[0, :, r0:r0 + tm] = jnp.where(r8 == 0, b1.astype(F32), jnp.where(r8 == 1, b2.astype(F32), 0.0))
    wt_ref[r0:r0 + tm, :] = jnp.concatenate(
        [jnp.broadcast_to(w1, (LANES, tm)).T, jnp.broadcast_to(w2, (LANES, tm)).T], axis=1)
    kio = lax.broadcasted_iota(jnp.int32, (N_EXPERTS, tm), 0)
    hits = ((kio == b1) | (kio == b2)).astype(F32)
    cnt = jnp.broadcast_to(jnp.sum(hits, axis=1, keepdims=True), (N_EXPERTS, LANES))
    cnt_ref[0] = cnt_ref[0] + cnt if accumulate else cnt


RUN_PAD = 16
MOE_TILE = 256
MOE_GROUP = 3
MOE_AHEAD = 8
MOE_BUFS = MOE_AHEAD + MOE_GROUP + 1


def _tile_lanes(v, width):
    reps = width // LANES
    return v if reps == 1 else jnp.concatenate([v] * reps, axis=1)


def _sorted_rows_per_tile(tm):
    need = 2 * tm + N_EXPERTS * (RUN_PAD - 1)
    return -(-need // LANES) * LANES


def _sort_body(cd_ref, u_ref, rt_ref, off_ref, *rest):
    xs_ref, dt_ref, loc, sem = rest[-4:]
    i = pl.program_id(0)
    tm = u_ref.shape[0]
    rows = loc.shape[1]
    rt = rt_ref[0]
    b1 = rt[0:1, :].astype(jnp.int32)
    b2 = rt[1:2, :].astype(jnp.int32)
    kio = lax.broadcasted_iota(jnp.int32, (N_EXPERTS, tm), 0)
    o1 = kio == b1
    o2 = kio == b2
    before = (lax.broadcasted_iota(jnp.int32, (tm, tm), 0) < lax.broadcasted_iota(jnp.int32, (tm, tm), 1))
    start = _dot((o1 | o2).astype(BF16), before.astype(BF16)) + _tile_lanes(off_ref[0], tm)
    d1 = jnp.sum(jnp.where(o1, start, 0.0), axis=0, keepdims=True)
    d2 = jnp.sum(jnp.where(o2, start, 0.0), axis=0, keepdims=True)
    dt_ref[...] = jnp.concatenate([jnp.broadcast_to(d1, (LANES, tm)).T, jnp.broadcast_to(d2, (LANES, tm)).T], axis=1)
    rio = lax.broadcasted_iota(jnp.int32, (rows, tm), 0)
    perm = ((rio == d1.astype(jnp.int32)) | (rio == d2.astype(jnp.int32))).astype(BF16)
    slot = i % 2
    loc[slot] = _dot(perm, u_ref[...]).astype(BF16)

    def chunk_copy(step, sl, c):
        return pltpu.make_async_copy(
            loc.at[sl, pl.ds(c * RUN_PAD, RUN_PAD), :],
            xs_ref.at[pl.ds(pl.multiple_of(cd_ref[step, c], RUN_PAD), RUN_PAD), :], sem.at[sl])

    n_chunk = rows // RUN_PAD
    for c in range(n_chunk):
        chunk_copy(i, slot, c).start()

    @pl.when(i > 0)
    def _():
        for c in range(n_chunk):
            chunk_copy(i - 1, 1 - slot, c).wait()

    @pl.when(i == pl.num_programs(0) - 1)
    def _():
        for c in range(n_chunk):
            chunk_copy(i, slot, c).wait()


def _sort_tokens(chunk_dst, u16, rt, off_v, xs_prev, tm, total_rows):
    t = u16.shape[0]
    rows = _sorted_rows_per_tile(tm)
    in_specs = [pl.BlockSpec((tm, D_MODEL), lambda i, cd: (i, 0)),
                pl.BlockSpec((1, 8, tm), lambda i, cd: (i, 0, 0)),
                pl.BlockSpec((1, N_EXPERTS, LANES), lambda i, cd: (i, 0, 0))]
    args = [chunk_dst, u16, rt, off_v]
    aliases = {}
    if xs_prev is not None:
        in_specs.append(pl.BlockSpec(memory_space=pl.ANY))
        aliases = {len(args): 0}
        args.append(xs_prev)
    return pl.pallas_call(
        _sort_body,
        grid_spec=pltpu.PrefetchScalarGridSpec(
            num_scalar_prefetch=1,
            grid=(t // tm,),
            in_specs=in_specs,
            out_specs=[pl.BlockSpec(memory_space=pl.ANY),
                       pl.BlockSpec((tm, 2 * LANES), lambda i, cd: (i, 0))],
            scratch_shapes=[pltpu.VMEM((2, rows, D_MODEL), BF16), pltpu.SemaphoreType.DMA((2,))]),
        out_shape=[jax.ShapeDtypeStruct((total_rows, D_MODEL), BF16),
                   jax.ShapeDtypeStruct((t, 2 * LANES), F32)],
        input_output_aliases=aliases,
        compiler_params=_cparams("arbitrary"),
        name="sort_tokens",
    )(*args)


def _moe_body(ts_ref, na_ref, xs_ref, wg_ref, wu_ref, wd_ref, ys_ref, xbuf, ybuf, wg16, wu16, wd16, sem_in, sem_out):
    k = pl.program_id(0)
    n_act = na_ref[0]

    def x_copy(g, sl):
        return pltpu.make_async_copy(xs_ref.at[pl.ds(pl.multiple_of(g * MOE_TILE, MOE_TILE), MOE_TILE), :],
                                     xbuf.at[sl], sem_in.at[sl])

    def y_copy(g, sl):
        return pltpu.make_async_copy(ybuf.at[sl],
                                     ys_ref.at[pl.ds(pl.multiple_of(g * MOE_TILE, MOE_TILE), MOE_TILE), :],
                                     sem_out.at[sl])

    @pl.when(k == 0)
    def _():
        for j in range(MOE_AHEAD):
            @pl.when(j < n_act)
            def _():
                x_copy(j, j).start(priority=1)

    wg16[...] = wg_ref[0].astype(BF16)
    wu16[...] = wu_ref[0].astype(BF16)
    wd16[...] = wd_ref[0].astype(BF16)

    def arrive(g):
        sl = g % MOE_BUFS

        @pl.when(g + MOE_AHEAD < n_act)
        def _():
            x_copy(g + MOE_AHEAD, (g + MOE_AHEAD) % MOE_BUFS).start(priority=1)

        x_copy(g, sl).wait()

        @pl.when(g >= MOE_BUFS)
        def _():
            y_copy(g - MOE_BUFS, sl).wait()

    def tiles(g, count):
        for j in range(count):
            arrive(g + j)
        acts = []
        for j in range(count):
            x = xbuf[(g + j) % MOE_BUFS]
            acts.append((_silu(_dot(x, wg16[...])) * _dot(x, wu16[...])).astype(BF16))
        for j in range(count):
            ybuf[(g + j) % MOE_BUFS] = _dot(acts[j], wd16[...]).astype(BF16)
        for j in range(count):
            y_copy(g + j, (g + j) % MOE_BUFS).start()

    g0 = ts_ref[k]
    n_here = ts_ref[k + 1] - g0

    def group(p, carry):
        tiles(g0 + MOE_GROUP * p, MOE_GROUP)
        return carry

    n_groups = n_here // MOE_GROUP
    lax.fori_loop(0, n_groups, group, 0)
    for rest in range(1, MOE_GROUP):
        @pl.when(n_here - n_groups * MOE_GROUP == rest)
        def _():
            tiles(g0 + n_groups * MOE_GROUP, rest)

    @pl.when(k == pl.num_programs(0) - 1)
    def _():
        for j in range(1, MOE_BUFS + 1):
            @pl.when(n_act >= j)
            def _():
                y_copy(n_act - j, (n_act - j) % MOE_BUFS).wait()


def _moe_sorted(tile_start, n_active, xs, wg, wu, wd):
    w_map = lambda k, ts, na: (k, 0, 0)
    return pl.pallas_call(
        _moe_body,
        grid_spec=pltpu.PrefetchScalarGridSpec(
            num_scalar_prefetch=2,
            grid=(N_EXPERTS,),
            in_specs=[pl.BlockSpec(memory_space=pl.ANY),
                      pl.BlockSpec((1, D_MODEL, EXPERT_FF), w_map),
                      pl.BlockSpec((1, D_MODEL, EXPERT_FF), w_map),
                      pl.BlockSpec((1, EXPERT_FF, D_MODEL), w_map)],
            out_specs=pl.BlockSpec(memory_space=pl.ANY),
            scratch_shapes=[pltpu.VMEM((MOE_BUFS, MOE_TILE, D_MODEL), BF16),
                            pltpu.VMEM((MOE_BUFS, MOE_TILE, D_MODEL), BF16),
                            pltpu.VMEM((D_MODEL, EXPERT_FF), BF16), pltpu.VMEM((D_MODEL, EXPERT_FF), BF16),
                            pltpu.VMEM((EXPERT_FF, D_MODEL), BF16),
                            pltpu.SemaphoreType.DMA((MOE_BUFS,)), pltpu.SemaphoreType.DMA((MOE_BUFS,))]),
        out_shape=jax.ShapeDtypeStruct(xs.shape, BF16),
        compiler_params=_cparams("arbitrary"),
        name="moe_sorted",
    )(tile_start, n_active, xs, wg, wu, wd)


def _ple_body(cd_ref, h_ref, dt_ref, wt_ref, p_ref, gn_ref, wg_ref, bg_ref, wp_ref, fn_ref, ys_ref,
              o_ref, loc, sem):
    i = pl.program_id(0)
    tm = h_ref.shape[0]
    rows = loc.shape[1]
    n_chunk = rows // RUN_PAD
    slot = i % 2

    def chunk_copy(step, sl, c):
        return pltpu.make_async_copy(
            ys_ref.at[pl.ds(pl.multiple_of(cd_ref[step, c], RUN_PAD), RUN_PAD), :],
            loc.at[sl, pl.ds(c * RUN_PAD, RUN_PAD), :], sem.at[sl])

    def fetch(step, sl):
        for c in range(n_chunk):
            chunk_copy(step, sl, c).start()

    @pl.when(i == 0)
    def _():
        fetch(0, 0)

    @pl.when(i + 1 < pl.num_programs(0))
    def _():
        fetch(i + 1, 1 - slot)

    for c in range(n_chunk):
        chunk_copy(i, slot, c).wait()

    ci = lax.broadcasted_iota(jnp.int32, (tm, LANES), 1)
    d1 = dt_ref[:, 0:LANES].astype(jnp.int32)
    d2 = dt_ref[:, LANES:2 * LANES].astype(jnp.int32)
    w1 = wt_ref[:, 0:LANES]
    w2 = wt_ref[:, LANES:2 * LANES]
    sel = jnp.concatenate(
        [jnp.where(d1 == ci + m * LANES, w1, jnp.where(d2 == ci + m * LANES, w2, 0.0)) for m in range(rows // LANES)],
        axis=1).astype(BF16)
    h = h_ref[...] + _dot(sel, loc[slot])
    a16 = _rms(h, gn_ref[...]).astype(BF16)
    gate = _sigmoid(_dot(a16, wg_ref[...]) + bg_ref[...])
    pp = _dot(p_ref[...].astype(BF16), wp_ref[...])
    h = h + gate * pp
    o_ref[...] = _rms(h, fn_ref[...])


def _ple(chunk_src, h1, dest_t, gate_t, p, g_ple, wg16, bg, wp16, g_final, ys, tm):
    t = h1.shape[0]
    row = lambda i, cd: (i, 0)
    fixed = lambda i, cd: (0, 0)
    return pl.pallas_call(
        _ple_body,
        grid_spec=pltpu.PrefetchScalarGridSpec(
            num_scalar_prefetch=1,
            grid=(t // tm,),
            in_specs=[pl.BlockSpec((tm, D_MODEL), row), pl.BlockSpec((tm, 2 * LANES), row),
                      pl.BlockSpec((tm, 2 * LANES), row), pl.BlockSpec((tm, PLE_DIM), row),
                      pl.BlockSpec((1, D_MODEL), fixed), pl.BlockSpec((D_MODEL, D_MODEL), fixed),
                      pl.BlockSpec((1, D_MODEL), fixed), pl.BlockSpec((PLE_DIM, D_MODEL), fixed),
                      pl.BlockSpec((1, D_MODEL), fixed), pl.BlockSpec(memory_space=pl.ANY)],
            out_specs=pl.BlockSpec((tm, D_MODEL), row),
            scratch_shapes=[pltpu.VMEM((2, _sorted_rows_per_tile(tm), D_MODEL), BF16),
                            pltpu.SemaphoreType.DMA((2,))]),
        out_shape=jax.ShapeDtypeStruct((t, D_MODEL), F32),
        compiler_params=_cparams("arbitrary"),
        name="ple_final",
    )(chunk_src, h1, dest_t, gate_t, p, g_ple, wg16, bg, wp16, g_final, ys)


MIX_ROWS = 512
SSD_Q = 128
POOL_HDR = 128
CONV_HDR = 8
TAIL_ROWS = 16


def _softplus(x):
    return jnp.maximum(x, 0.0) + jnp.log1p(jnp.exp(-jnp.abs(x)))


def _stack_terms(v, terms):
    parts = [t.astype(F32) for t in _split3(v)[:terms]]
    parts.append(jnp.zeros(((4 - terms) * v.shape[0], v.shape[1]), F32))
    return jnp.concatenate(parts, axis=0).astype(BF16)


def _cumsum_lanes(v, tri16):
    r = _dot_nt(_stack_terms(v, 3), tri16)
    n = v.shape[0]
    return r[0:n] + r[n:2 * n] + r[2 * n:3 * n]


def _expand_heads(v, expand32, terms):
    return _dot_tn(_stack_terms(v, terms), expand32)


def _mixer_constants():
    q = SSD_Q
    li = np.arange(q)[:, None]
    tri = (li >= np.arange(q)[None, :])
    expand = (np.arange(SSD_WIDTH)[None, :] // SSD_HEAD_DIM) == (np.arange(4 * SSD_HEADS)[:, None] % SSD_HEADS)
    hpg = SSD_HEADS // SSD_GROUPS
    block = (np.arange(SSD_BC)[:, None] // SSD_STATE) == (np.arange(SSD_WIDTH)[None, :] // (SSD_HEAD_DIM * hpg))
    wj = np.arange(2 * q)[None, :]
    band = np.stack([(wj <= li + POOL_HDR) & (wj > li + POOL_HDR - w) for w in POOL_WINDOWS])
    inv_count = np.stack([np.broadcast_to(1.0 / np.minimum(li + 1, w), (q, LANES)) for w in POOL_WINDOWS])
    as_f32 = lambda m: jnp.asarray(m.astype(np.float32))
    return (as_f32(tri), as_f32(tri).astype(BF16), as_f32(expand).astype(BF16), as_f32(block),
            as_f32(band).astype(BF16), as_f32(inv_count))


def _front_body(steps_per_seq, x_ref, xp_ref, gm_ref, win_ref, trif_ref, tri_ref, exp_ref, bmask_ref, band_ref,
                invc_ref, cw_ref, cb_ref, dtb_ref, alog_ref, dsk_ref, nw_ref,
                plw_ref, plb_ref, psc_ref, wout_ref, gf_ref, rh_ref, rl_ref, rb_ref,
                h_ref, u_ref, rt_ref, wt_ref, cnt_ref, st_ref, ptail_ref, ctail_ref,
                pool_ext, conv_ext, s_ref, z_ref, dt_ref, mix_ref, mixp_ref):
    s = pl.program_id(0)
    n_tiles = pl.num_programs(0) - 1
    real = s < n_tiles
    c = jnp.minimum(s, n_tiles - 1) % steps_per_seq
    rows = MIX_ROWS
    q_len = SSD_Q

    @pl.when(s == 0)
    def _():
        mixp_ref[...] = jnp.zeros_like(mixp_ref)

    @pl.when(c == 0)
    def _():
        pool_ext[0:POOL_HDR, :] = jnp.zeros((POOL_HDR, POOL_WIDTH), F32)
        conv_ext[0:CONV_HDR, :] = jnp.zeros((CONV_HDR, CONV_DIM), F32)
        s_ref[...] = jnp.zeros_like(s_ref)

    @pl.when(c > 0)
    def _():
        pool_ext[0:POOL_HDR, :] = pool_ext[rows:rows + POOL_HDR, :]
        conv_ext[0:CONV_HDR, :] = conv_ext[rows:rows + CONV_HDR, :]

    causal = trif_ref[...] > 0.5
    tri = tri_ref[...]
    expand = exp_ref[...]
    blockmask = bmask_ref[...]
    lane = lax.broadcasted_iota(jnp.int32, (q_len, LANES), 1)
    left = lane < SSD_HEAD_DIM
    a_neg = -jnp.exp(alog_ref[...])

    half = rows // 2
    n_cols = 256

    def in_proj_chunks(r0):
        a16 = _rms(x_ref[r0:r0 + half, :], gm_ref[...]).astype(BF16)
        dests = ([(pool_ext, POOL_HDR + r0, k, 0) for k in range(0, POOL_WIDTH, n_cols)]
                 + [(z_ref, r0, k, POOL_WIDTH) for k in range(0, SSD_WIDTH, n_cols)]
                 + [(conv_ext, CONV_HDR + r0, k, POOL_WIDTH + SSD_WIDTH) for k in range(0, CONV_DIM, n_cols)])

        def chunk(ref, row, k, col0):
            def run():
                ref[row:row + half, k:k + n_cols] = _dot(a16, win_ref[:, col0 + k:col0 + k + n_cols])
            return run

        def dt_chunk():
            dt_ref[r0:r0 + half, :] = _dot(a16, win_ref[:, DT_OFF:IN_COLS])

        return [chunk(*d) for d in dests] + [dt_chunk]

    def out_proj_chunks(r0):
        def chunk(k):
            def run():
                h_ref[r0:r0 + half, k:k + n_cols] = (xp_ref[r0:r0 + half, k:k + n_cols]
                                                    + _dot(mixp_ref[r0:r0 + half, :], wout_ref[:, k:k + n_cols]))
            return run

        def route():
            _route_rows(h_ref[r0:r0 + half, :], gf_ref, rh_ref, rl_ref, rb_ref, u_ref, rt_ref, wt_ref, cnt_ref,
                        r0=r0, accumulate=r0 > 0)

        return [chunk(k) for k in range(0, D_MODEL, n_cols)] + [route]

    def mix_pair(q_first, fillers):
        fillers = list(fillers)

        def fill():
            if fillers:
                fillers.pop(0)()

        subs = range(q_first, q_first + 2)
        xs_l, b16_l, call_l, c16_l = {}, {}, {}, {}
        for q in subs:
            base = CONV_HDR + q * q_len
            conv = cb_ref[...] + conv_ext[base - 3:base - 3 + q_len, :] * cw_ref[0:1, :]
            conv = conv + conv_ext[base - 2:base - 2 + q_len, :] * cw_ref[1:2, :]
            conv = conv + conv_ext[base - 1:base - 1 + q_len, :] * cw_ref[2:3, :]
            conv = conv + conv_ext[base:base + q_len, :] * cw_ref[3:4, :]
            conv = _silu(conv)
            xs_l[q] = conv[:, 0:SSD_WIDTH]
            b16_l[q] = conv[:, SSD_WIDTH:SSD_WIDTH + SSD_BC].astype(BF16)
            call_l[q] = conv[:, SSD_WIDTH + SSD_BC:CONV_DIM]
            c16_l[q] = call_l[q].astype(BF16)
            fill()
        dtt_l = {q: _softplus(dt_ref[q * q_len:(q + 1) * q_len, :].T[0:SSD_HEADS, :] + dtb_ref[...]) for q in subs}
        acst_l = {q: _cumsum_lanes(dtt_l[q] * a_neg, tri) for q in subs}
        acs_l = {q: jnp.concatenate([acst_l[q], jnp.zeros((LANES - SSD_HEADS, q_len), F32)], axis=0).T
                 for q in subs}
        fill()
        dtx_l = {q: _expand_heads(dtt_l[q], expand, 2) for q in subs}
        acsx_l = {q: _expand_heads(acst_l[q], expand, 3) for q in subs}
        lastx_l = {q: acsx_l[q][q_len - 1:q_len, :] for q in subs}
        xdt_l = {q: xs_l[q] * dtx_l[q] for q in subs}
        xdt16_l = {q: xdt_l[q].astype(BF16) for q in subs}
        fill()
        contrib_l = {q: _dot_tn(b16_l[q], (xdt_l[q] * jnp.exp(lastx_l[q] - acsx_l[q])).astype(BF16)) * blockmask
                     for q in subs}
        cb_l = {q: [_dot_nt(jnp.where(lax.shift_right_logical(lane, 6) == g, call_l[q], 0.0).astype(BF16), b16_l[q])
                    for g in range(SSD_GROUPS)] for q in subs}
        fill()
        zs_l = {}
        for q in subs:
            zs = []
            for h in range(SSD_HEADS):
                blk = slice((h // 2) * LANES, (h // 2 + 1) * LANES)
                seg = acs_l[q][:, h:h + 1] - acst_l[q][h:h + 1, :]
                decay = jnp.where(causal, jnp.exp(seg), 0.0)
                scores = (cb_l[q][h // (SSD_HEADS // SSD_GROUPS)] * decay).astype(BF16)
                zs.append(_dot(scores, xdt16_l[q][:, blk]))
                if h % 4 == 3:
                    fill()
            zs_l[q] = zs
        yoff_l = {}
        for q in subs:
            s_old = s_ref[...]
            yoff_l[q] = _dot(c16_l[q], s_old.astype(BF16)) * jnp.exp(acsx_l[q])
            s_ref[...] = s_old * jnp.exp(lastx_l[q]) + contrib_l[q]
        fill()
        for q in subs:
            r0 = q * q_len
            z = z_ref[r0:r0 + q_len, :]
            y_blocks = []
            for j in range(SSD_HEADS // 2):
                blk = slice(j * LANES, (j + 1) * LANES)
                y = jnp.where(left, zs_l[q][2 * j], zs_l[q][2 * j + 1]) + yoff_l[q][:, blk]
                y = y + dsk_ref[:, blk] * xs_l[q][:, blk]
                y_blocks.append(y * _silu(z[:, blk]))
            for g in range(SSD_GROUPS):
                y0, y1 = y_blocks[2 * g], y_blocks[2 * g + 1]
                ss = jnp.sum(y0 * y0, axis=-1, keepdims=True) + jnp.sum(y1 * y1, axis=-1, keepdims=True)
                rs = lax.rsqrt(ss * (1.0 / (2 * LANES)) + EPS)
                for k, yk in ((2 * g, y0), (2 * g + 1, y1)):
                    blk = slice(k * LANES, (k + 1) * LANES)
                    out = yk * rs * nw_ref[:, blk]
                    mix_ref[r0:r0 + q_len, POOL_WIDTH + k * LANES:POOL_WIDTH + (k + 1) * LANES] = out.astype(BF16)
            fill()
        for q in subs:
            r0 = q * q_len
            for g, w in enumerate(POOL_WINDOWS):
                blk = slice(g * POOL_GROUP_DIM, (g + 1) * POOL_GROUP_DIM)
                pe = pool_ext[r0:r0 + 2 * q_len, blk]
                hi = pe.astype(BF16)
                lo = (pe - hi.astype(F32)).astype(BF16)
                both = _dot(band_ref[g], jnp.concatenate([hi, lo], axis=1))
                winsum = both[:, 0:POOL_GROUP_DIM] + both[:, POOL_GROUP_DIM:2 * POOL_GROUP_DIM]
                inv = jnp.where(c == 0, invc_ref[g], 1.0 / w) if q == 0 else 1.0 / w
                m = winsum * inv - pe[q_len:2 * q_len, :]
                yg = _dot(m.astype(BF16), plw_ref[g]) + plb_ref[:, blk]
                mix_ref[r0:r0 + q_len, blk] = (yg * psc_ref[:, blk]).astype(BF16)
            fill()
        while fillers:
            fill()

    @pl.when(real)
    def _():
        for run in in_proj_chunks(0):
            run()
        mix_pair(0, in_proj_chunks(half) + out_proj_chunks(0))
        mix_pair(2, out_proj_chunks(half))
        mixp_ref[...] = mix_ref[...]

    @pl.when(jnp.logical_not(real))
    def _():
        for run in out_proj_chunks(0) + out_proj_chunks(half):
            run()

    @pl.when(real & (c == steps_per_seq - 1))
    def _():
        hpg = SSD_HEADS // SSD_GROUPS
        for j in range(SSD_HEADS // 2):
            flipped = s_ref[:, j * LANES:(j + 1) * LANES].T
            g = (2 * j) // hpg
            for k in range(2):
                st_ref[0, 2 * j + k] = flipped[k * SSD_HEAD_DIM:(k + 1) * SSD_HEAD_DIM,
                                               g * SSD_STATE:(g + 1) * SSD_STATE]
        ptail_ref[0] = pool_ext[POOL_HDR + rows - TAIL_ROWS:POOL_HDR + rows, :]
        ctail_ref[0] = conv_ext[rows:rows + CONV_HDR, :]


def _front_prompt(x, g_mix, w_in16, cw, cb, dtb, alog, dsk, nw, plw16, plb, psc, w_out16, g_ffn, r_hi, r_lo, r_bias,
                  nb, seq):
    steps = seq // MIX_ROWS
    nt = nb * steps
    cur = lambda s: (jnp.minimum(s, nt - 1), 0)
    prev = lambda s: (jnp.maximum(s - 1, 0), 0)
    prev3 = lambda s: (jnp.maximum(s - 1, 0), 0, 0)
    per_seq = lambda s: (jnp.minimum(s, nt - 1) // steps, 0, 0)
    fixed2 = lambda s: (0, 0)
    fixed3 = lambda s: (0, 0, 0)
    return pl.pallas_call(
        functools.partial(_front_body, steps),
        grid=(nt + 1,),
        in_specs=[pl.BlockSpec((MIX_ROWS, D_MODEL), cur), pl.BlockSpec((MIX_ROWS, D_MODEL), prev),
                  pl.BlockSpec((1, D_MODEL), fixed2), pl.BlockSpec((D_MODEL, IN_COLS), fixed2),
                  pl.BlockSpec((SSD_Q, SSD_Q), fixed2), pl.BlockSpec((SSD_Q, SSD_Q), fixed2),
                  pl.BlockSpec((4 * SSD_HEADS, SSD_WIDTH), fixed2), pl.BlockSpec((SSD_BC, SSD_WIDTH), fixed2),
                  pl.BlockSpec((len(POOL_WINDOWS), SSD_Q, 2 * SSD_Q), fixed3),
                  pl.BlockSpec((len(POOL_WINDOWS), SSD_Q, LANES), fixed3),
                  pl.BlockSpec((SSD_CONV, CONV_DIM), fixed2), pl.BlockSpec((1, CONV_DIM), fixed2),
                  pl.BlockSpec((SSD_HEADS, SSD_Q), fixed2), pl.BlockSpec((SSD_HEADS, SSD_Q), fixed2),
                  pl.BlockSpec((1, SSD_WIDTH), fixed2), pl.BlockSpec((1, SSD_WIDTH), fixed2),
                  pl.BlockSpec((len(POOL_WINDOWS), POOL_GROUP_DIM, POOL_GROUP_DIM), fixed3),
                  pl.BlockSpec((1, POOL_WIDTH), fixed2), pl.BlockSpec((1, POOL_WIDTH), fixed2),
                  pl.BlockSpec((D_MODEL, D_MODEL), fixed2), pl.BlockSpec((1, D_MODEL), fixed2),
                  pl.BlockSpec((LANES, D_MODEL), fixed2), pl.BlockSpec((LANES, D_MODEL), fixed2),
                  pl.BlockSpec((LANES, MIX_ROWS), fixed2)],
        out_specs=[pl.BlockSpec((MIX_ROWS, D_MODEL), prev), pl.BlockSpec((MIX_ROWS, D_MODEL), prev),
                   pl.BlockSpec((1, 8, MIX_ROWS), prev3), pl.BlockSpec((MIX_ROWS, 2 * LANES), prev),
                   pl.BlockSpec((1, N_EXPERTS, LANES), prev3),
                   pl.BlockSpec((1, SSD_HEADS, SSD_HEAD_DIM, SSD_STATE),
                                lambda s: (jnp.minimum(s, nt - 1) // steps, 0, 0, 0)),
                   pl.BlockSpec((1, TAIL_ROWS, POOL_WIDTH), per_seq),
                   pl.BlockSpec((1, CONV_HDR, CONV_DIM), per_seq)],
        out_shape=[jax.ShapeDtypeStruct((nb * seq, D_MODEL), F32), jax.ShapeDtypeStruct((nb * seq, D_MODEL), BF16),
                   jax.ShapeDtypeStruct((nt, 8, MIX_ROWS), F32), jax.ShapeDtypeStruct((nb * seq, 2 * LANES), F32),
                   jax.ShapeDtypeStruct((nt, N_EXPERTS, LANES), F32),
                   jax.ShapeDtypeStruct((nb, SSD_HEADS, SSD_HEAD_DIM, SSD_STATE), F32),
                   jax.ShapeDtypeStruct((nb, TAIL_ROWS, POOL_WIDTH), F32),
                   jax.ShapeDtypeStruct((nb, CONV_HDR, CONV_DIM), F32)],
        scratch_shapes=[pltpu.VMEM((POOL_HDR + MIX_ROWS, POOL_WIDTH), F32),
                        pltpu.VMEM((CONV_HDR + MIX_ROWS, CONV_DIM), F32),
                        pltpu.VMEM((SSD_BC, SSD_WIDTH), F32),
                        pltpu.VMEM((MIX_ROWS, SSD_WIDTH), F32), pltpu.VMEM((MIX_ROWS, LANES), F32),
                        pltpu.VMEM((MIX_ROWS, D_MODEL), BF16), pltpu.VMEM((MIX_ROWS, D_MODEL), BF16)],
        compiler_params=_cparams("arbitrary"),
        name="front_prompt",
    )(x, x, g_mix, w_in16, *_mixer_constants(), cw, cb, dtb, alog, dsk, nw, plw16, plb, psc, w_out16, g_ffn,
      r_hi, r_lo, r_bias)


def _mix_step_body(x_ref, gm_ref, win_ref, sp_ref, sc_ref, st_ref, cw_ref, cb_ref, dtb_ref, alog_ref,
                   dsk_ref, nw_ref, plw_ref, plb_ref, psc_ref, wout_ref, gf_ref, rh_ref, rl_ref, rb_ref,
                   h_ref, u_ref, rt_ref, wt_ref, cnt_ref, po_ref, co_ref, so_ref,
                   xdt_t, dec_t, b_t, c_t, xs_keep, y_t, z_ref, mix_ref):
    h = pl.program_id(0)

    @pl.when(h == 0)
    def _():
        a16 = _rms(x_ref[...], gm_ref[...]).astype(BF16)
        vp_new = _dot(a16, win_ref[:, 0:POOL_WIDTH])
        z_ref[...] = _dot(a16, win_ref[:, POOL_WIDTH:POOL_WIDTH + SSD_WIDTH])
        xbc = _dot(a16, win_ref[:, POOL_WIDTH + SSD_WIDTH:DT_OFF])
        dt_raw = _dot(a16, win_ref[:, DT_OFF:IN_COLS])
        conv = cb_ref[...] + sc_ref[0] * cw_ref[0:1, :]
        conv = conv + sc_ref[1] * cw_ref[1:2, :]
        conv = conv + sc_ref[2] * cw_ref[2:3, :]
        conv = conv + xbc * cw_ref[3:4, :]
        conv = _silu(conv)
        co_ref[0] = sc_ref[1]
        co_ref[1] = sc_ref[2]
        co_ref[2] = xbc
        xs = conv[:, 0:SSD_WIDTH]
        xs_keep[...] = xs
        b_t[...] = conv[:, SSD_WIDTH:SSD_WIDTH + SSD_BC].T
        c_t[...] = conv[:, SSD_WIDTH + SSD_BC:CONV_DIM].T
        dt = _softplus(dt_raw + dtb_ref[...])
        d_a = dt * (-jnp.exp(alog_ref[...]))
        dt_t = dt.T
        dec_t[...] = jnp.exp(d_a).T
        xs_t = xs.T
        for k in range(SSD_HEADS):
            blk = slice(k * SSD_HEAD_DIM, (k + 1) * SSD_HEAD_DIM)
            xdt_t[blk, :] = xs_t[blk, :] * dt_t[k:k + 1, :]
        v = vp_new
        for k in range(POOL_BUF - 1):
            po_ref[k] = sp_ref[k + 1]
        po_ref[POOL_BUF - 1] = v
        for g, w in enumerate(POOL_WINDOWS):
            blk = slice(g * POOL_GROUP_DIM, (g + 1) * POOL_GROUP_DIM)
            acc = sp_ref[POOL_BUF - (w - 1), :, blk]
            for k in range(w - 2, 0, -1):
                acc = acc + sp_ref[POOL_BUF - k, :, blk]
            acc = acc + v[:, blk]
            m = acc / float(min(PAST_LEN + 1, w)) - v[:, blk]
            yg = _dot(m.astype(BF16), plw_ref[g]) + plb_ref[:, blk]
            mix_ref[:, blk] = (yg * psc_ref[:, blk]).astype(BF16)

    g_off = pl.multiple_of((h // (SSD_HEADS // SSD_GROUPS)) * SSD_STATE, SSD_STATE)
    h_off = pl.multiple_of(h * SSD_HEAD_DIM, SSD_HEAD_DIM)
    b_g = b_t[pl.ds(g_off, SSD_STATE), :]
    c_g = c_t[pl.ds(g_off, SSD_STATE), :]
    dec = dec_t[pl.ds(h, 1), :]
    xdt = xdt_t[pl.ds(h_off, SSD_HEAD_DIM), :]
    y_rows = []
    for p in range(SSD_HEAD_DIM):
        s_new = st_ref[0, p] * dec + xdt[p:p + 1, :] * b_g
        so_ref[0, p] = s_new
        y_rows.append(jnp.sum(s_new * c_g, axis=0, keepdims=True))
    y_t[pl.ds(h_off, SSD_HEAD_DIM), :] = jnp.concatenate(y_rows, axis=0)

    @pl.when(h == pl.num_programs(0) - 1)
    def _():
        xs = xs_keep[...]
        y = y_t[...].T + dsk_ref[...] * xs
        y = y * _silu(z_ref[...])
        width = SSD_WIDTH // SSD_GROUPS
        for g in range(SSD_GROUPS):
            blk = slice(g * width, (g + 1) * width)
            yg = y[:, blk]
            rs = lax.rsqrt(jnp.mean(yg * yg, axis=-1, keepdims=True) + EPS)
            mix_ref[:, POOL_WIDTH + g * width:POOL_WIDTH + (g + 1) * width] = (yg * rs * nw_ref[:, blk]).astype(BF16)
        _out_proj_and_route(x_ref[...], mix_ref[...], wout_ref, gf_ref, rh_ref, rl_ref, rb_ref,
                            h_ref, u_ref, rt_ref, wt_ref, cnt_ref)


def _front_step(x, g_mix, w_in16, pool_t, conv_t, ssm_t, cw, cb, dtb, alog, dsk, nw, plw16, plb, psc,
                w_out16, g_ffn, r_hi, r_lo, r_bias):
    n = x.shape[0]
    fixed2 = lambda h: (0, 0)
    fixed3 = lambda h: (0, 0, 0)
    st_spec = pl.BlockSpec((1, SSD_HEAD_DIM, SSD_STATE, n), lambda h: (h, 0, 0, 0))
    return pl.pallas_call(
        _mix_step_body,
        grid=(SSD_HEADS,),
        in_specs=[pl.BlockSpec((n, D_MODEL), fixed2), pl.BlockSpec((1, D_MODEL), fixed2),
                  pl.BlockSpec((D_MODEL, IN_COLS), fixed2),
                  pl.BlockSpec((POOL_BUF, n, POOL_WIDTH), fixed3),
                  pl.BlockSpec((SSD_CONV - 1, n, CONV_DIM), fixed3),
                  st_spec,
                  pl.BlockSpec((SSD_CONV, CONV_DIM), fixed2), pl.BlockSpec((1, CONV_DIM), fixed2),
                  pl.BlockSpec((1, LANES), fixed2), pl.BlockSpec((1, LANES), fixed2),
                  pl.BlockSpec((1, SSD_WIDTH), fixed2), pl.BlockSpec((1, SSD_WIDTH), fixed2),
                  pl.BlockSpec((len(POOL_WINDOWS), POOL_GROUP_DIM, POOL_GROUP_DIM), fixed3),
                  pl.BlockSpec((1, POOL_WIDTH), fixed2), pl.BlockSpec((1, POOL_WIDTH), fixed2),
                  pl.BlockSpec((D_MODEL, D_MODEL), fixed2), pl.BlockSpec((1, D_MODEL), fixed2),
                  pl.BlockSpec((LANES, D_MODEL), fixed2), pl.BlockSpec((LANES, D_MODEL), fixed2),
                  pl.BlockSpec((LANES, n), fixed2)],
        out_specs=[pl.BlockSpec((n, D_MODEL), fixed2), pl.BlockSpec((n, D_MODEL), fixed2),
                   pl.BlockSpec((1, 8, n), fixed3), pl.BlockSpec((n, 2 * LANES), fixed2),
                   pl.BlockSpec((1, N_EXPERTS, LANES), fixed3),
                   pl.BlockSpec((POOL_BUF, n, POOL_WIDTH), fixed3),
                   pl.BlockSpec((SSD_CONV - 1, n, CONV_DIM), fixed3), st_spec],
        out_shape=[jax.ShapeDtypeStruct((n, D_MODEL), F32), jax.ShapeDtypeStruct((n, D_MODEL), BF16),
                   jax.ShapeDtypeStruct((1, 8, n), F32), jax.ShapeDtypeStruct((n, 2 * LANES), F32),
                   jax.ShapeDtypeStruct((1, N_EXPERTS, LANES), F32),
                   jax.ShapeDtypeStruct(pool_t.shape, F32),
                   jax.ShapeDtypeStruct(conv_t.shape, F32), jax.ShapeDtypeStruct(ssm_t.shape, F32)],
        scratch_shapes=[pltpu.VMEM((SSD_WIDTH, n), F32), pltpu.VMEM((LANES, n), F32),
                        pltpu.VMEM((SSD_BC, n), F32), pltpu.VMEM((SSD_BC, n), F32),
                        pltpu.VMEM((n, SSD_WIDTH), F32), pltpu.VMEM((SSD_WIDTH, n), F32),
                        pltpu.VMEM((n, SSD_WIDTH), F32), pltpu.VMEM((n, D_MODEL), BF16)],
        compiler_params=_cparams("arbitrary"),
        name="front_step",
    )(x, g_mix, w_in16, pool_t, conv_t, ssm_t, cw, cb, dtb, alog, dsk, nw, plw16, plb, psc,
      w_out16, g_ffn, r_hi, r_lo, r_bias)


PROMPT_TILE = 512


def _sort_tables(counts, tile_tokens, data_rows):
    cnt = jnp.concatenate(counts, axis=0)
    pc = (cnt + RUN_PAD - 1) // RUN_PAD * RUN_PAD
    off_local = jnp.cumsum(pc, axis=1) - pc
    tile_rows = jnp.sum(pc, axis=1)
    region = (jnp.sum(pc, axis=0) + MOE_TILE - 1) // MOE_TILE * MOE_TILE
    base = jnp.cumsum(region) - region
    dst = base[None, :] + jnp.cumsum(pc, axis=0) - pc
    per_pass = []
    lo = 0
    for c, tm in zip(counts, tile_tokens):
        hi = lo + c.shape[0]
        n_chunk = _sorted_rows_per_tile(tm) // RUN_PAD
        s = jnp.arange(n_chunk, dtype=jnp.int32) * RUN_PAD
        begins = off_local[lo:hi, None, :]
        ends = begins + pc[lo:hi, None, :]
        inside = (begins <= s[None, :, None]) & (s[None, :, None] < ends)
        shift = jnp.sum(jnp.where(inside, dst[lo:hi, None, :] - begins, 0), axis=2)
        valid = s[None, :] < tile_rows[lo:hi, None]
        parity = (jnp.arange(hi - lo, dtype=jnp.int32) % 2)[:, None]
        spare = data_rows + parity * _sorted_rows_per_tile(max(tile_tokens)) + s[None, :]
        scatter_dst = jnp.where(valid, s[None, :] + shift, spare).astype(jnp.int32)
        gather_src = jnp.where(valid, s[None, :] + shift, 0).astype(jnp.int32)
        off_v = jnp.broadcast_to(off_local[lo:hi, :, None].astype(F32), (hi - lo, N_EXPERTS, LANES))
        per_pass.append((scatter_dst, gather_src, off_v))
        lo = hi
    tiles_cum = jnp.cumsum(region // MOE_TILE)
    n_active = tiles_cum[-1]
    return per_pass, tiles_cum, n_active


def _max_sorted_rows(tile_counts, tile_tokens):
    rows = sum(n * (2 * tm + N_EXPERTS * (RUN_PAD - 1)) for n, tm in zip(tile_counts, tile_tokens))
    data_rows = (-(-rows // MOE_TILE) + N_EXPERTS) * MOE_TILE
    return data_rows, data_rows + 2 * _sorted_rows_per_tile(max(tile_tokens))


def _row(v):
    return v.reshape(1, -1).astype(F32)


def _per_head_rows(v):
    return jnp.broadcast_to(v.astype(F32)[:, None], (SSD_HEADS, SSD_Q))


def _pad_lanes(v):
    return jnp.pad(v.reshape(1, -1).astype(F32), ((0, 0), (0, LANES - v.size)))


def kernel(x_prompt, x_sample, p_prompt, p_sample, state_pool, state_conv, state_ssm, norm_mix, w_in, pool_lin_w, pool_lin_b, pool_scale, conv_w, conv_b, dt_bias, a_log, d_skip, ssd_norm, w_out, norm_ffn, router_grp_w, router_grp_b, router_exp_w, router_exp_b, exp_w_gate, exp_w_up, exp_w_down, norm_ple, ple_gate_w, ple_gate_b, ple_proj_w, norm_final):
    nb, seq, _ = x_prompt.shape
    ns = x_sample.shape[0]
    assert ns == LANES and x_sample.shape[1] == 1 and seq % MIX_ROWS == 0 and seq >= POOL_BUF

    w_in16 = jnp.pad(w_in[0], ((0, 0), (0, IN_COLS - w_in.shape[2]))).astype(BF16)
    w_out16 = w_out[0].astype(BF16)
    g_mix, g_ffn, g_ple, g_fin = _row(norm_mix[0]), _row(norm_ffn[0]), _row(norm_ple[0]), _row(norm_final)
    cw, cb = conv_w[0].astype(F32), _row(conv_b[0])
    dtb, alog = _pad_lanes(dt_bias[0]), _pad_lanes(a_log[0])
    dsk = _row(jnp.repeat(d_skip[0], SSD_HEAD_DIM))
    nw = _row(ssd_norm[0])
    plw16 = pool_lin_w[0].astype(BF16)
    plb, psc = _row(pool_lin_b[0]), _row(pool_scale[0])
    zeros4 = jnp.zeros((D_MODEL, 8 - N_EXPERT_GROUPS), F32)
    r_w = jnp.concatenate([router_grp_w[0], zeros4, router_exp_w[0],
                           jnp.zeros((D_MODEL, LANES - 8 - N_EXPERTS), F32)], axis=1).T
    r_hi = r_w.astype(BF16)
    r_lo = (r_w - r_hi.astype(F32)).astype(BF16)
    r_b = jnp.concatenate([router_grp_b[0], jnp.zeros((8 - N_EXPERT_GROUPS,), F32), router_exp_b[0],
                           jnp.zeros((LANES - 8 - N_EXPERTS,), F32)])
    wg = exp_w_gate[0].reshape(N_EXPERTS, D_MODEL, EXPERT_FF)
    wu = exp_w_up[0].reshape(N_EXPERTS, D_MODEL, EXPERT_FF)
    wd = exp_w_down[0].reshape(N_EXPERTS, EXPERT_FF, D_MODEL)
    pg16 = ple_gate_w[0].astype(BF16)
    pgb = _row(ple_gate_b[0])
    pp16 = ple_proj_w[0].astype(BF16)

    xp = x_prompt.reshape(nb * seq, D_MODEL)
    h1_p, u_p, rt_p, gate_p, cnt_p, st, pool_tail, conv_tail = _front_prompt(
        xp, g_mix, w_in16, cw, cb, _per_head_rows(dt_bias[0]), _per_head_rows(a_log[0]), dsk, nw, plw16, plb, psc,
        w_out16, g_ffn, r_hi, r_lo, jnp.broadcast_to(r_b[:, None], (LANES, MIX_ROWS)), nb, seq)
    pool_p = pool_tail[:, TAIL_ROWS - POOL_BUF:]
    conv_p = conv_tail[:, CONV_HDR - (SSD_CONV - 1):]
    ssm_p = st

    xs_ = x_sample.reshape(ns, D_MODEL)
    h1_s, u_s, rt_s, gate_s, cnt_s, pool_t, conv_t, ssm_t = _front_step(
        xs_, g_mix, w_in16, jnp.transpose(state_pool[0], (1, 0, 2)), jnp.transpose(state_conv[0], (1, 0, 2)),
        jnp.transpose(state_ssm[0], (1, 2, 3, 0)), cw, cb, dtb, alog, dsk, nw, plw16, plb, psc,
        w_out16, g_ffn, r_hi, r_lo, jnp.broadcast_to(r_b[:, None], (LANES, ns)))
    pool_s = jnp.transpose(pool_t, (1, 0, 2))
    conv_s = jnp.transpose(conv_t, (1, 0, 2))
    ssm_s = jnp.transpose(ssm_t, (3, 0, 1, 2))

    counts = [cnt_p[:, :, 0].astype(jnp.int32), cnt_s[:, :, 0].astype(jnp.int32)]
    tiles = (PROMPT_TILE, ns)
    data_rows, total_rows = _max_sorted_rows([c.shape[0] for c in counts], tiles)
    (tab_p, tab_s), tiles_cum, n_active = _sort_tables(counts, tiles, data_rows)
    xs_sorted, dest_p = _sort_tokens(tab_p[0], u_p, rt_p, tab_p[2], None, PROMPT_TILE, total_rows)
    xs_sorted, dest_s = _sort_tokens(tab_s[0], u_s, rt_s, tab_s[2], xs_sorted, ns, total_rows)
    tile_start = jnp.concatenate([jnp.zeros((1,), jnp.int32), tiles_cum.astype(jnp.int32)])
    ys_sorted = _moe_sorted(tile_start, n_active.reshape(1).astype(jnp.int32), xs_sorted, wg, wu, wd)

    y_prompt = _ple(tab_p[1], h1_p, dest_p, gate_p, p_prompt[0].reshape(nb * seq, PLE_DIM),
                    g_ple, pg16, pgb, pp16, g_fin, ys_sorted, PROMPT_TILE)
    y_sample = _ple(tab_s[1], h1_s, dest_s, gate_s, p_sample[0].reshape(ns, PLE_DIM),
                    g_ple, pg16, pgb, pp16, g_fin, ys_sorted, ns)

    return (y_prompt.reshape(nb, seq, D_MODEL), y_sample.reshape(ns, 1, D_MODEL),
            pool_p[None], conv_p[None], ssm_p[None], pool_s[None], conv_s[None], ssm_s[None])
```

```python
import functools

import jax
import jax.numpy as jnp
import numpy as np
from jax import lax
from jax.experimental import pallas as pl
from jax.experimental.pallas import tpu as pltpu

F32 = jnp.float32
BF16 = jnp.bfloat16

D_MODEL = 1024
POOL_WIDTH = 512
POOL_WINDOWS = (2, 4, 8, 16)
POOL_GROUP_DIM = 128
POOL_BUF = 15
SSD_WIDTH = 512
SSD_HEAD_DIM = 64
SSD_HEADS = 8
SSD_GROUPS = 2
SSD_STATE = 64
SSD_CONV = 4
SSD_BC = SSD_GROUPS * SSD_STATE
CONV_DIM = SSD_WIDTH + 2 * SSD_BC
N_EXPERT_GROUPS = 4
EXPERTS_PER_GROUP = 8
N_EXPERTS = N_EXPERT_GROUPS * EXPERTS_PER_GROUP
EXPERT_FF = 256
PLE_DIM = 256
PAST_LEN = 16384
EPS = 1e-6

LANES = 128
IN_COLS = 1920
DT_OFF = POOL_WIDTH + SSD_WIDTH + CONV_DIM
VMEM_LIMIT = 56 * 1024 * 1024


def _cparams(*sem):
    return pltpu.CompilerParams(dimension_semantics=sem, vmem_limit_bytes=VMEM_LIMIT)


def _rms(x, g):
    return x * lax.rsqrt(jnp.mean(x * x, axis=-1, keepdims=True) + EPS) * g


def _sigmoid(x):
    return 1.0 / (1.0 + jnp.exp(-x))


def _silu(x):
    return x * _sigmoid(x)


def _split3(v):
    hi = v.astype(BF16)
    r = v - hi.astype(F32)
    mid = r.astype(BF16)
    lo = (r - mid.astype(F32)).astype(BF16)
    return hi, mid, lo


def _dot(a, b):
    return jnp.dot(a, b, preferred_element_type=F32)


def _dot_nt(a, b):
    return lax.dot_general(a, b, (((1,), (1,)), ((), ())), preferred_element_type=F32)


def _dot_tn(a, b):
    return lax.dot_general(a, b, (((0,), (0,)), ((), ())), preferred_element_type=F32)


def _route(lg):
    tm = lg.shape[1]
    gl = lg[0:N_EXPERT_GROUPS, :]
    gmax = jnp.max(gl, axis=0, keepdims=True)
    gsum = jnp.sum(jnp.exp(gl - gmax), axis=0, keepdims=True)
    g_w = 1.0 / gsum
    gi = lax.broadcasted_iota(jnp.int32, gl.shape, 0)
    g_idx = jnp.min(jnp.where(gl == gmax, gi, N_EXPERT_GROUPS), axis=0, keepdims=True)
    sel = jnp.zeros((EXPERTS_PER_GROUP, tm), F32)
    for g in range(N_EXPERT_GROUPS):
        blk = lg[8 + g * EXPERTS_PER_GROUP:8 + (g + 1) * EXPERTS_PER_GROUP, :]
        sel = jnp.where(g_idx == g, blk, sel)
    ei = lax.broadcasted_iota(jnp.int32, sel.shape, 0)
    m1 = jnp.max(sel, axis=0, keepdims=True)
    i1 = jnp.min(jnp.where(sel == m1, ei, EXPERTS_PER_GROUP), axis=0, keepdims=True)
    rest = jnp.where(ei == i1, -jnp.inf, sel)
    m2 = jnp.max(rest, axis=0, keepdims=True)
    i2 = jnp.min(jnp.where(rest == m2, ei, EXPERTS_PER_GROUP), axis=0, keepdims=True)
    p2 = jnp.exp(m2 - m1)
    w1 = g_w / (1.0 + p2)
    w2 = g_w * p2 / (1.0 + p2)
    return g_idx * EXPERTS_PER_GROUP + i1, g_idx * EXPERTS_PER_GROUP + i2, w1, w2


def _out_proj_and_route(x, mix16, w_ref, g_ref, rh_ref, rl_ref, rb_ref, h_ref, u_ref, rt_ref, wt_ref, cnt_ref,
                        r0=0, accumulate=False):
    tm = x.shape[0]
    h = x + _dot(mix16, w_ref[...])
    h_ref[r0:r0 + tm, :] = h
    _route_rows(h, g_ref, rh_ref, rl_ref, rb_ref, u_ref, rt_ref, wt_ref, cnt_ref, r0=r0, accumulate=accumulate)


def _route_rows(h, g_ref, rh_ref, rl_ref, rb_ref, u_ref, rt_ref, wt_ref, cnt_ref, r0=0, accumulate=False):
    tm = h.shape[0]
    u = _rms(h, g_ref[...])
    u_hi = u.astype(BF16)
    u_ref[r0:r0 + tm, :] = u_hi
    u_lo = (u - u_hi.astype(F32)).astype(BF16)
    lg = (_dot_nt(rh_ref[...], u_hi) + _dot_nt(rh_ref[...], u_lo) + _dot_nt(rl_ref[...], u_hi)
          + rb_ref[:, r0:r0 + tm])
    b1, b2, w1, w2 = _route(lg)
    r8 = lax.broadcasted_iota(jnp.int32, (8, tm), 0)
    rt_ref[0, :, r0:r0 + tm] = jnp.where(r8 == 0, b1.astype(F32), jnp.where(r8 == 1, b2.astype(F32), 0.0))
    wt_ref[r0:r0 + tm, :] = jnp.concatenate(
        [jnp.broadcast_to(w1, (LANES, tm)).T, jnp.broadcast_to(w2, (LANES, tm)).T], axis=1)
    kio = lax.broadcasted_iota(jnp.int32, (N_EXPERTS, tm), 0)
    hits = ((kio == b1) | (kio == b2)).astype(F32)
    cnt = jnp.broadcast_to(jnp.sum(hits, axis=1, keepdims=True), (N_EXPERTS, LANES))
    cnt_ref[0] = cnt_ref[0] + cnt if accumulate else cnt


RUN_PAD = 16
MOE_TILE = 256
MOE_GROUP = 3
MOE_AHEAD = 8
MOE_BUFS = MOE_AHEAD + MOE_GROUP + 1


def _tile_lanes(v, width):
    reps = width // LANES
    return v if reps == 1 else jnp.concatenate([v] * reps, axis=1)


def _sorted_rows_per_tile(tm):
    need = 2 * tm + N_EXPERTS * (RUN_PAD - 1)
    return -(-need // LANES) * LANES


def _sort_body(cd_ref, u_ref, rt_ref, off_ref, before_ref, *rest):
    xs_ref, dt_ref, loc, sem = rest[-4:]
    i = pl.program_id(0)
    tm = u_ref.shape[0]
    rows = loc.shape[1]
    rt = rt_ref[0]
    b1 = rt[0:1, :].astype(jnp.int32)
    b2 = rt[1:2, :].astype(jnp.int32)
    kio = lax.broadcasted_iota(jnp.int32, (N_EXPERTS, tm), 0)
    o1 = kio == b1
    o2 = kio == b2
    start = _dot((o1 | o2).astype(BF16), before_ref[...]) + _tile_lanes(off_ref[0], tm)
    d1 = jnp.sum(jnp.where(o1, start, 0.0), axis=0, keepdims=True)
    d2 = jnp.sum(jnp.where(o2, start, 0.0), axis=0, keepdims=True)
    dt_ref[...] = jnp.concatenate([jnp.broadcast_to(d1, (LANES, tm)).T, jnp.broadcast_to(d2, (LANES, tm)).T], axis=1)
    rio = lax.broadcasted_iota(jnp.int32, (rows, tm), 0)
    perm = ((rio == d1.astype(jnp.int32)) | (rio == d2.astype(jnp.int32))).astype(BF16)
    slot = i % 2
    loc[slot] = _dot(perm, u_ref[...]).astype(BF16)

    def chunk_copy(step, sl, c):
        return pltpu.make_async_copy(
            loc.at[sl, pl.ds(c * RUN_PAD, RUN_PAD), :],
            xs_ref.at[pl.ds(pl.multiple_of(cd_ref[step, c], RUN_PAD), RUN_PAD), :], sem.at[sl])

    n_chunk = rows // RUN_PAD
    for c in range(n_chunk):
        chunk_copy(i, slot, c).start()

    @pl.when(i > 0)
    def _():
        for c in range(n_chunk):
            chunk_copy(i - 1, 1 - slot, c).wait()

    @pl.when(i == pl.num_programs(0) - 1)
    def _():
        for c in range(n_chunk):
            chunk_copy(i, slot, c).wait()


def _sort_tokens(chunk_dst, u16, rt, off_v, xs_prev, tm, total_rows):
    t = u16.shape[0]
    rows = _sorted_rows_per_tile(tm)
    in_specs = [pl.BlockSpec((tm, D_MODEL), lambda i, cd: (i, 0)),
                pl.BlockSpec((1, 8, tm), lambda i, cd: (i, 0, 0)),
                pl.BlockSpec((1, N_EXPERTS, LANES), lambda i, cd: (i, 0, 0)),
                pl.BlockSpec((tm, tm), lambda i, cd: (0, 0))]
    before = jnp.asarray(np.triu(np.ones((tm, tm), np.float32), 1)).astype(BF16)
    args = [chunk_dst, u16, rt, off_v, before]
    aliases = {}
    if xs_prev is not None:
        in_specs.append(pl.BlockSpec(memory_space=pl.ANY))
        aliases = {len(args): 0}
        args.append(xs_prev)
    return pl.pallas_call(
        _sort_body,
        grid_spec=pltpu.PrefetchScalarGridSpec(
            num_scalar_prefetch=1,
            grid=(t // tm,),
            in_specs=in_specs,
            out_specs=[pl.BlockSpec(memory_space=pl.ANY),
                       pl.BlockSpec((tm, 2 * LANES), lambda i, cd: (i, 0))],
            scratch_shapes=[pltpu.VMEM((2, rows, D_MODEL), BF16), pltpu.SemaphoreType.DMA((2,))]),
        out_shape=[jax.ShapeDtypeStruct((total_rows, D_MODEL), BF16),
                   jax.ShapeDtypeStruct((t, 2 * LANES), F32)],
        input_output_aliases=aliases,
        compiler_params=_cparams("arbitrary"),
        name="sort_tokens",
    )(*args)


def _moe_body(ts_ref, na_ref, xs_ref, wg_ref, wu_ref, wd_ref, ys_ref, xbuf, ybuf, wg16, wu16, wd16, sem_in, sem_out):
    k = pl.program_id(0)
    n_act = na_ref[0]

    def x_copy(g, sl):
        return pltpu.make_async_copy(xs_ref.at[pl.ds(pl.multiple_of(g * MOE_TILE, MOE_TILE), MOE_TILE), :],
                                     xbuf.at[sl], sem_in.at[sl])

    def y_copy(g, sl):
        return pltpu.make_async_copy(ybuf.at[sl],
                                     ys_ref.at[pl.ds(pl.multiple_of(g * MOE_TILE, MOE_TILE), MOE_TILE), :],
                                     sem_out.at[sl])

    @pl.when(k == 0)
    def _():
        for j in range(MOE_AHEAD):
            @pl.when(j < n_act)
            def _():
                x_copy(j, j).start(priority=1)

    wg16[...] = wg_ref[0].astype(BF16)
    wu16[...] = wu_ref[0].astype(BF16)
    wd16[...] = wd_ref[0].astype(BF16)

    def arrive(g):
        sl = g % MOE_BUFS

        @pl.when(g + MOE_AHEAD < n_act)
        def _():
            x_copy(g + MOE_AHEAD, (g + MOE_AHEAD) % MOE_BUFS).start(priority=1)

        x_copy(g, sl).wait()

        @pl.when(g >= MOE_BUFS)
        def _():
            y_copy(g - MOE_BUFS, sl).wait()

    def tiles(g, count):
        for j in range(count):
            arrive(g + j)
        acts = []
        for j in range(count):
            x = xbuf[(g + j) % MOE_BUFS]
            acts.append((_silu(_dot(x, wg16[...])) * _dot(x, wu16[...])).astype(BF16))
        for j in range(count):
            ybuf[(g + j) % MOE_BUFS] = _dot(acts[j], wd16[...]).astype(BF16)
        for j in range(count):
            y_copy(g + j, (g + j) % MOE_BUFS).start()

    g0 = ts_ref[k]
    n_here = ts_ref[k + 1] - g0

    def group(p, carry):
        tiles(g0 + MOE_GROUP * p, MOE_GROUP)
        return carry

    n_groups = n_here // MOE_GROUP
    lax.fori_loop(0, n_groups, group, 0)
    for rest in range(1, MOE_GROUP):
        @pl.when(n_here - n_groups * MOE_GROUP == rest)
        def _():
            tiles(g0 + n_groups * MOE_GROUP, rest)

    @pl.when(k == pl.num_programs(0) - 1)
    def _():
        for j in range(1, MOE_BUFS + 1):
            @pl.when(n_act >= j)
            def _():
                y_copy(n_act - j, (n_act - j) % MOE_BUFS).wait()


def _moe_sorted(tile_start, n_active, xs, wg, wu, wd):
    w_map = lambda k, ts, na: (k, 0, 0)
    return pl.pallas_call(
        _moe_body,
        grid_spec=pltpu.PrefetchScalarGridSpec(
            num_scalar_prefetch=2,
            grid=(N_EXPERTS,),
            in_specs=[pl.BlockSpec(memory_space=pl.ANY),
                      pl.BlockSpec((1, D_MODEL, EXPERT_FF), w_map),
                      pl.BlockSpec((1, D_MODEL, EXPERT_FF), w_map),
                      pl.BlockSpec((1, EXPERT_FF, D_MODEL), w_map)],
            out_specs=pl.BlockSpec(memory_space=pl.ANY),
            scratch_shapes=[pltpu.VMEM((MOE_BUFS, MOE_TILE, D_MODEL), BF16),
                            pltpu.VMEM((MOE_BUFS, MOE_TILE, D_MODEL), BF16),
                            pltpu.VMEM((D_MODEL, EXPERT_FF), BF16), pltpu.VMEM((D_MODEL, EXPERT_FF), BF16),
                            pltpu.VMEM((EXPERT_FF, D_MODEL), BF16),
                            pltpu.SemaphoreType.DMA((MOE_BUFS,)), pltpu.SemaphoreType.DMA((MOE_BUFS,))]),
        out_shape=jax.ShapeDtypeStruct(xs.shape, BF16),
        compiler_params=_cparams("arbitrary"),
        name="moe_sorted",
    )(tile_start, n_active, xs, wg, wu, wd)


def _ple_body(cd_ref, h_ref, dt_ref, wt_ref, p_ref, gn_ref, wg_ref, bg_ref, wp_ref, fn_ref, ys_ref,
              o_ref, loc, sem):
    i = pl.program_id(0)
    tm = h_ref.shape[0]
    rows = loc.shape[1]
    n_chunk = rows // RUN_PAD
    slot = i % 2

    def chunk_copy(step, sl, c):
        return pltpu.make_async_copy(
            ys_ref.at[pl.ds(pl.multiple_of(cd_ref[step, c], RUN_PAD), RUN_PAD), :],
            loc.at[sl, pl.ds(c * RUN_PAD, RUN_PAD), :], sem.at[sl])

    def fetch(step, sl):
        for c in range(n_chunk):
            chunk_copy(step, sl, c).start()

    @pl.when(i == 0)
    def _():
        fetch(0, 0)

    @pl.when(i + 1 < pl.num_programs(0))
    def _():
        fetch(i + 1, 1 - slot)

    for c in range(n_chunk):
        chunk_copy(i, slot, c).wait()

    ci = lax.broadcasted_iota(jnp.int32, (tm, LANES), 1)
    d1 = dt_ref[:, 0:LANES].astype(jnp.int32)
    d2 = dt_ref[:, LANES:2 * LANES].astype(jnp.int32)
    w1 = wt_ref[:, 0:LANES]
    w2 = wt_ref[:, LANES:2 * LANES]
    sel = jnp.concatenate(
        [jnp.where(d1 == ci + m * LANES, w1, jnp.where(d2 == ci + m * LANES, w2, 0.0)) for m in range(rows // LANES)],
        axis=1).astype(BF16)
    h = h_ref[...] + _dot(sel, loc[slot])
    a16 = _rms(h, gn_ref[...]).astype(BF16)
    gate = _sigmoid(_dot(a16, wg_ref[...]) + bg_ref[...])
    pp = _dot(p_ref[...].astype(BF16), wp_ref[...])
    h = h + gate * pp
    o_ref[...] = _rms(h, fn_ref[...])


def _ple(chunk_src, h1, dest_t, gate_t, p, g_ple, wg16, bg, wp16, g_final, ys, tm):
    t = h1.shape[0]
    row = lambda i, cd: (i, 0)
    fixed = lambda i, cd: (0, 0)
    return pl.pallas_call(
        _ple_body,
        grid_spec=pltpu.PrefetchScalarGridSpec(
            num_scalar_prefetch=1,
            grid=(t // tm,),
            in_specs=[pl.BlockSpec((tm, D_MODEL), row), pl.BlockSpec((tm, 2 * LANES), row),
                      pl.BlockSpec((tm, 2 * LANES), row), pl.BlockSpec((tm, PLE_DIM), row),
                      pl.BlockSpec((1, D_MODEL), fixed), pl.BlockSpec((D_MODEL, D_MODEL), fixed),
                      pl.BlockSpec((1, D_MODEL), fixed), pl.BlockSpec((PLE_DIM, D_MODEL), fixed),
                      pl.BlockSpec((1, D_MODEL), fixed), pl.BlockSpec(memory_space=pl.ANY)],
            out_specs=pl.BlockSpec((tm, D_MODEL), row),
            scratch_shapes=[pltpu.VMEM((2, _sorted_rows_per_tile(tm), D_MODEL), BF16),
                            pltpu.SemaphoreType.DMA((2,))]),
        out_shape=jax.ShapeDtypeStruct((t, D_MODEL), F32),
        compiler_params=_cparams("arbitrary"),
        name="ple_final",
    )(chunk_src, h1, dest_t, gate_t, p, g_ple, wg16, bg, wp16, g_final, ys)


MIX_ROWS = 512
SSD_Q = 128
POOL_HDR = 128
CONV_HDR = 8
TAIL_ROWS = 16


def _softplus(x):
    return jnp.maximum(x, 0.0) + jnp.log1p(jnp.exp(-jnp.abs(x)))


def _stack_terms(v, terms):
    parts = [t.astype(F32) for t in _split3(v)[:terms]]
    parts.append(jnp.zeros(((4 - terms) * v.shape[0], v.shape[1]), F32))
    return jnp.concatenate(parts, axis=0).astype(BF16)


def _cumsum_lanes(v, tri16):
    r = _dot_nt(_stack_terms(v, 3), tri16)
    n = v.shape[0]
    return r[0:n] + r[n:2 * n] + r[2 * n:3 * n]


def _expand_heads(v, expand32, terms):
    return _dot_tn(_stack_terms(v, terms), expand32)


def _mixer_constants():
    q = SSD_Q
    li = np.arange(q)[:, None]
    tri = (li >= np.arange(q)[None, :])
    expand = (np.arange(SSD_WIDTH)[None, :] // SSD_HEAD_DIM) == (np.arange(4 * SSD_HEADS)[:, None] % SSD_HEADS)
    hpg = SSD_HEADS // SSD_GROUPS
    block = (np.arange(SSD_BC)[:, None] // SSD_STATE) == (np.arange(SSD_WIDTH)[None, :] // (SSD_HEAD_DIM * hpg))
    wj = np.arange(2 * q)[None, :]
    band = np.stack([(wj <= li + POOL_HDR) & (wj > li + POOL_HDR - w) for w in POOL_WINDOWS])
    inv_count = np.stack([np.broadcast_to(1.0 / np.minimum(li + 1, w), (q, LANES)) for w in POOL_WINDOWS])
    as_f32 = lambda m: jnp.asarray(m.astype(np.float32))
    return (as_f32(tri), as_f32(tri).astype(BF16), as_f32(expand).astype(BF16), as_f32(block),
            as_f32(band).astype(BF16), as_f32(inv_count))


def _front_body(steps_per_seq, x_ref, xp_ref, gm_ref, win_ref, trif_ref, tri_ref, exp_ref, bmask_ref, band_ref,
                invc_ref, cw_ref, cb_ref, dtb_ref, alog_ref, dsk_ref, nw_ref,
                plw_ref, plb_ref, psc_ref, wout_ref, gf_ref, rh_ref, rl_ref, rb_ref,
                h_ref, u_ref, rt_ref, wt_ref, cnt_ref, st_ref, ptail_ref, ctail_ref,
                pool_ext, conv_ext, s_ref, z_ref, dt_ref, mix_ref, mixp_ref):
    s = pl.program_id(0)
    n_tiles = pl.num_programs(0) - 1
    real = s < n_tiles
    c = jnp.minimum(s, n_tiles - 1) % steps_per_seq
    rows = MIX_ROWS
    q_len = SSD_Q

    @pl.when(s == 0)
    def _():
        mixp_ref[...] = jnp.zeros_like(mixp_ref)

    @pl.when(c == 0)
    def _():
        pool_ext[0:POOL_HDR, :] = jnp.zeros((POOL_HDR, POOL_WIDTH), F32)
        conv_ext[0:CONV_HDR, :] = jnp.zeros((CONV_HDR, CONV_DIM), F32)
        s_ref[...] = jnp.zeros_like(s_ref)

    @pl.when(c > 0)
    def _():
        pool_ext[0:POOL_HDR, :] = pool_ext[rows:rows + POOL_HDR, :]
        conv_ext[0:CONV_HDR, :] = conv_ext[rows:rows + CONV_HDR, :]

    causal = trif_ref[...] > 0.5
    tri = tri_ref[...]
    expand = exp_ref[...]
    blockmask = bmask_ref[...]
    lane = lax.broadcasted_iota(jnp.int32, (q_len, LANES), 1)
    left = lane < SSD_HEAD_DIM
    a_neg = -jnp.exp(alog_ref[...])

    half = rows // 2
    n_cols = 256

    def in_proj_chunks(r0):
        a16 = _rms(x_ref[r0:r0 + half, :], gm_ref[...]).astype(BF16)
        dests = ([(pool_ext, POOL_HDR + r0, k, 0) for k in range(0, POOL_WIDTH, n_cols)]
                 + [(z_ref, r0, k, POOL_WIDTH) for k in range(0, SSD_WIDTH, n_cols)]
                 + [(conv_ext, CONV_HDR + r0, k, POOL_WIDTH + SSD_WIDTH) for k in range(0, CONV_DIM, n_cols)])

        def chunk(ref, row, k, col0):
            def run():
                ref[row:row + half, k:k + n_cols] = _dot(a16, win_ref[:, col0 + k:col0 + k + n_cols])
            return run

        def dt_chunk():
            dt_ref[r0:r0 + half, :] = _dot(a16, win_ref[:, DT_OFF:IN_COLS])

        return [chunk(*d) for d in dests] + [dt_chunk]

    def out_proj_chunks(r0):
        def chunk(k):
            def run():
                h_ref[r0:r0 + half, k:k + n_cols] = (xp_ref[r0:r0 + half, k:k + n_cols]
                                                    + _dot(mixp_ref[r0:r0 + half, :], wout_ref[:, k:k + n_cols]))
            return run

        def route():
            _route_rows(h_ref[r0:r0 + half, :], gf_ref, rh_ref, rl_ref, rb_ref, u_ref, rt_ref, wt_ref, cnt_ref,
                        r0=r0, accumulate=r0 > 0)

        return [chunk(k) for k in range(0, D_MODEL, n_cols)] + [route]

    def mix_pair(q_first, fillers):
        fillers = list(fillers)

        def fill():
            if fillers:
                fillers.pop(0)()

        subs = range(q_first, q_first + 2)
        xs_l, b16_l, call_l, c16_l = {}, {}, {}, {}
        for q in subs:
            base = CONV_HDR + q * q_len
            conv = cb_ref[...] + conv_ext[base - 3:base - 3 + q_len, :] * cw_ref[0:1, :]
            conv = conv + conv_ext[base - 2:base - 2 + q_len, :] * cw_ref[1:2, :]
            conv = conv + conv_ext[base - 1:base - 1 + q_len, :] * cw_ref[2:3, :]
            conv = conv + conv_ext[base:base + q_len, :] * cw_ref[3:4, :]
            conv = _silu(conv)
            xs_l[q] = conv[:, 0:SSD_WIDTH]
            b16_l[q] = conv[:, SSD_WIDTH:SSD_WIDTH + SSD_BC].astype(BF16)
            call_l[q] = conv[:, SSD_WIDTH + SSD_BC:CONV_DIM]
            c16_l[q] = call_l[q].astype(BF16)
            fill()
        dtt_l = {q: _softplus(dt_ref[q * q_len:(q + 1) * q_len, :].T[0:SSD_HEADS, :] + dtb_ref[...]) for q in subs}
        acst_l = {q: _cumsum_lanes(dtt_l[q] * a_neg, tri) for q in subs}
        acs_l = {q: jnp.concatenate([acst_l[q], jnp.zeros((LANES - SSD_HEADS, q_len), F32)], axis=0).T
                 for q in subs}
        fill()
        dtx_l = {q: _expand_heads(dtt_l[q], expand, 2) for q in subs}
        acsx_l = {q: _expand_heads(acst_l[q], expand, 3) for q in subs}
        lastx_l = {q: acsx_l[q][q_len - 1:q_len, :] for q in subs}
        xdt_l = {q: xs_l[q] * dtx_l[q] for q in subs}
        xdt16_l = {q: xdt_l[q].astype(BF16) for q in subs}
        fill()
        contrib_l = {q: _dot_tn(b16_l[q], (xdt_l[q] * jnp.exp(lastx_l[q] - acsx_l[q])).astype(BF16)) * blockmask
                     for q in subs}
        cb_l = {q: [_dot_nt(jnp.where(lax.shift_right_logical(lane, 6) == g, call_l[q], 0.0).astype(BF16), b16_l[q])
                    for g in range(SSD_GROUPS)] for q in subs}
        fill()
        zs_l = {}
        for q in subs:
            zs = []
            for h in range(SSD_HEADS):
                blk = slice((h // 2) * LANES, (h // 2 + 1) * LANES)
                seg = acs_l[q][:, h:h + 1] - acst_l[q][h:h + 1, :]
                decay = jnp.where(causal, jnp.exp(seg), 0.0)
                scores = (cb_l[q][h // (SSD_HEADS // SSD_GROUPS)] * decay).astype(BF16)
                zs.append(_dot(scores, xdt16_l[q][:, blk]))
                if h % 4 == 3:
                    fill()
            zs_l[q] = zs
        yoff_l = {}
        for q in subs:
            s_old = s_ref[...]
            yoff_l[q] = _dot(c16_l[q], s_old.astype(BF16)) * jnp.exp(acsx_l[q])
            s_ref[...] = s_old * jnp.exp(lastx_l[q]) + contrib_l[q]
        fill()
        for q in subs:
            r0 = q * q_len
            z = z_ref[r0:r0 + q_len, :]
            y_blocks = []
            for j in range(SSD_HEADS // 2):
                blk = slice(j * LANES, (j + 1) * LANES)
                y = jnp.where(left, zs_l[q][2 * j], zs_l[q][2 * j + 1]) + yoff_l[q][:, blk]
                y = y + dsk_ref[:, blk] * xs_l[q][:, blk]
                y_blocks.append(y * _silu(z[:, blk]))
            for g in range(SSD_GROUPS):
                y0, y1 = y_blocks[2 * g], y_blocks[2 * g + 1]
                ss = jnp.sum(y0 * y0, axis=-1, keepdims=True) + jnp.sum(y1 * y1, axis=-1, keepdims=True)
                rs = lax.rsqrt(ss * (1.0 / (2 * LANES)) + EPS)
                for k, yk in ((2 * g, y0), (2 * g + 1, y1)):
                    blk = slice(k * LANES, (k + 1) * LANES)
                    out = yk * rs * nw_ref[:, blk]
                    mix_ref[r0:r0 + q_len, POOL_WIDTH + k * LANES:POOL_WIDTH + (k + 1) * LANES] = out.astype(BF16)
            fill()
        for q in subs:
            r0 = q * q_len
            for g, w in enumerate(POOL_WINDOWS):
                blk = slice(g * POOL_GROUP_DIM, (g + 1) * POOL_GROUP_DIM)
                pe = pool_ext[r0:r0 + 2 * q_len, blk]
                hi = pe.astype(BF16)
                lo = (pe - hi.astype(F32)).astype(BF16)
                both = _dot(band_ref[g], jnp.concatenate([hi, lo], axis=1))
                winsum = both[:, 0:POOL_GROUP_DIM] + both[:, POOL_GROUP_DIM:2 * POOL_GROUP_DIM]
                inv = jnp.where(c == 0, invc_ref[g], 1.0 / w) if q == 0 else 1.0 / w
                m = winsum * inv - pe[q_len:2 * q_len, :]
                yg = _dot(m.astype(BF16), plw_ref[g]) + plb_ref[:, blk]
                mix_ref[r0:r0 + q_len, blk] = (yg * psc_ref[:, blk]).astype(BF16)
            fill()
        while fillers:
            fill()

    @pl.when(real)
    def _():
        for run in in_proj_chunks(0):
            run()
        mix_pair(0, in_proj_chunks(half) + out_proj_chunks(0))
        mix_pair(2, out_proj_chunks(half))
        mixp_ref[...] = mix_ref[...]

    @pl.when(jnp.logical_not(real))
    def _():
        for run in out_proj_chunks(0) + out_proj_chunks(half):
            run()

    @pl.when(real & (c == steps_per_seq - 1))
    def _():
        hpg = SSD_HEADS // SSD_GROUPS
        for j in range(SSD_HEADS // 2):
            flipped = s_ref[:, j * LANES:(j + 1) * LANES].T
            g = (2 * j) // hpg
            for k in range(2):
                st_ref[0, 2 * j + k] = flipped[k * SSD_HEAD_DIM:(k + 1) * SSD_HEAD_DIM,
                                               g * SSD_STATE:(g + 1) * SSD_STATE]
        ptail_ref[0] = pool_ext[POOL_HDR + rows - TAIL_ROWS:POOL_HDR + rows, :]
        ctail_ref[0] = conv_ext[rows:rows + CONV_HDR, :]


def _front_prompt(x, g_mix, w_in16, cw, cb, dtb, alog, dsk, nw, plw16, plb, psc, w_out16, g_ffn, r_hi, r_lo, r_bias,
                  nb, seq):
    steps = seq // MIX_ROWS
    nt = nb * steps
    cur = lambda s: (jnp.minimum(s, nt - 1), 0)
    prev = lambda s: (jnp.maximum(s - 1, 0), 0)
    prev3 = lambda s: (jnp.maximum(s - 1, 0), 0, 0)
    per_seq = lambda s: (jnp.minimum(s, nt - 1) // steps, 0, 0)
    fixed2 = lambda s: (0, 0)
    fixed3 = lambda s: (0, 0, 0)
    return pl.pallas_call(
        functools.partial(_front_body, steps),
        grid=(nt + 1,),
        in_specs=[pl.BlockSpec((MIX_ROWS, D_MODEL), cur), pl.BlockSpec((MIX_ROWS, D_MODEL), prev),
                  pl.BlockSpec((1, D_MODEL), fixed2), pl.BlockSpec((D_MODEL, IN_COLS), fixed2),
                  pl.BlockSpec((SSD_Q, SSD_Q), fixed2), pl.BlockSpec((SSD_Q, SSD_Q), fixed2),
                  pl.BlockSpec((4 * SSD_HEADS, SSD_WIDTH), fixed2), pl.BlockSpec((SSD_BC, SSD_WIDTH), fixed2),
                  pl.BlockSpec((len(POOL_WINDOWS), SSD_Q, 2 * SSD_Q), fixed3),
                  pl.BlockSpec((len(POOL_WINDOWS), SSD_Q, LANES), fixed3),
                  pl.BlockSpec((SSD_CONV, CONV_DIM), fixed2), pl.BlockSpec((1, CONV_DIM), fixed2),
                  pl.BlockSpec((SSD_HEADS, SSD_Q), fixed2), pl.BlockSpec((SSD_HEADS, SSD_Q), fixed2),
                  pl.BlockSpec((1, SSD_WIDTH), fixed2), pl.BlockSpec((1, SSD_WIDTH), fixed2),
                  pl.BlockSpec((len(POOL_WINDOWS), POOL_GROUP_DIM, POOL_GROUP_DIM), fixed3),
                  pl.BlockSpec((1, POOL_WIDTH), fixed2), pl.BlockSpec((1, POOL_WIDTH), fixed2),
                  pl.BlockSpec((D_MODEL, D_MODEL), fixed2), pl.BlockSpec((1, D_MODEL), fixed2),
                  pl.BlockSpec((LANES, D_MODEL), fixed2), pl.BlockSpec((LANES, D_MODEL), fixed2),
                  pl.BlockSpec((LANES, MIX_ROWS), fixed2)],
        out_specs=[pl.BlockSpec((MIX_ROWS, D_MODEL), prev), pl.BlockSpec((MIX_ROWS, D_MODEL), prev),
                   pl.BlockSpec((1, 8, MIX_ROWS), prev3), pl.BlockSpec((MIX_ROWS, 2 * LANES), prev),
                   pl.BlockSpec((1, N_EXPERTS, LANES), prev3),
                   pl.BlockSpec((1, SSD_HEADS, SSD_HEAD_DIM, SSD_STATE),
                                lambda s: (jnp.minimum(s, nt - 1) // steps, 0, 0, 0)),
                   pl.BlockSpec((1, TAIL_ROWS, POOL_WIDTH), per_seq),
                   pl.BlockSpec((1, CONV_HDR, CONV_DIM), per_seq)],
        out_shape=[jax.ShapeDtypeStruct((nb * seq, D_MODEL), F32), jax.ShapeDtypeStruct((nb * seq, D_MODEL), BF16),
                   jax.ShapeDtypeStruct((nt, 8, MIX_ROWS), F32), jax.ShapeDtypeStruct((nb * seq, 2 * LANES), F32),
                   jax.ShapeDtypeStruct((nt, N_EXPERTS, LANES), F32),
                   jax.ShapeDtypeStruct((nb, SSD_HEADS, SSD_HEAD_DIM, SSD_STATE), F32),
                   jax.ShapeDtypeStruct((nb, TAIL_ROWS, POOL_WIDTH), F32),
                   jax.ShapeDtypeStruct((nb, CONV_HDR, CONV_DIM), F32)],
        scratch_shapes=[pltpu.VMEM((POOL_HDR + MIX_ROWS, POOL_WIDTH), F32),
                        pltpu.VMEM((CONV_HDR + MIX_ROWS, CONV_DIM), F32),
                        pltpu.VMEM((SSD_BC, SSD_WIDTH), F32),
                        pltpu.VMEM((MIX_ROWS, SSD_WIDTH), F32), pltpu.VMEM((MIX_ROWS, LANES), F32),
                        pltpu.VMEM((MIX_ROWS, D_MODEL), BF16), pltpu.VMEM((MIX_ROWS, D_MODEL), BF16)],
        compiler_params=_cparams("arbitrary"),
        name="front_prompt",
    )(x, x, g_mix, w_in16, *_mixer_constants(), cw, cb, dtb, alog, dsk, nw, plw16, plb, psc, w_out16, g_ffn,
      r_hi, r_lo, r_bias)


def _mix_step_body(x_ref, gm_ref, win_ref, sp_ref, sc_ref, st_ref, cw_ref, cb_ref, dtb_ref, alog_ref,
                   dsk_ref, nw_ref, plw_ref, plb_ref, psc_ref, wout_ref, gf_ref, rh_ref, rl_ref, rb_ref,
                   h_ref, u_ref, rt_ref, wt_ref, cnt_ref, po_ref, co_ref, so_ref,
                   xdt_t, dec_t, b_t, c_t, xs_keep, y_t, z_ref, mix_ref):
    h = pl.program_id(0)

    @pl.when(h == 0)
    def _():
        a16 = _rms(x_ref[...], gm_ref[...]).astype(BF16)
        vp_new = _dot(a16, win_ref[:, 0:POOL_WIDTH])
        z_ref[...] = _dot(a16, win_ref[:, POOL_WIDTH:POOL_WIDTH + SSD_WIDTH])
        xbc = _dot(a16, win_ref[:, POOL_WIDTH + SSD_WIDTH:DT_OFF])
        dt_raw = _dot(a16, win_ref[:, DT_OFF:IN_COLS])
        conv = cb_ref[...] + sc_ref[0] * cw_ref[0:1, :]
        conv = conv + sc_ref[1] * cw_ref[1:2, :]
        conv = conv + sc_ref[2] * cw_ref[2:3, :]
        conv = conv + xbc * cw_ref[3:4, :]
        conv = _silu(conv)
        co_ref[0] = sc_ref[1]
        co_ref[1] = sc_ref[2]
        co_ref[2] = xbc
        xs = conv[:, 0:SSD_WIDTH]
        xs_keep[...] = xs
        b_t[...] = conv[:, SSD_WIDTH:SSD_WIDTH + SSD_BC].T
        c_t[...] = conv[:, SSD_WIDTH + SSD_BC:CONV_DIM].T
        dt = _softplus(dt_raw + dtb_ref[...])
        d_a = dt * (-jnp.exp(alog_ref[...]))
        dt_t = dt.T
        dec_t[...] = jnp.exp(d_a).T
        xs_t = xs.T
        for k in range(SSD_HEADS):
            blk = slice(k * SSD_HEAD_DIM, (k + 1) * SSD_HEAD_DIM)
            xdt_t[blk, :] = xs_t[blk, :] * dt_t[k:k + 1, :]
        v = vp_new
        for k in range(POOL_BUF - 1):
            po_ref[k] = sp_ref[k + 1]
        po_ref[POOL_BUF - 1] = v
        for g, w in enumerate(POOL_WINDOWS):
            blk = slice(g * POOL_GROUP_DIM, (g + 1) * POOL_GROUP_DIM)
            acc = sp_ref[POOL_BUF - (w - 1), :, blk]
            for k in range(w - 2, 0, -1):
                acc = acc + sp_ref[POOL_BUF - k, :, blk]
            acc = acc + v[:, blk]
            m = acc / float(min(PAST_LEN + 1, w)) - v[:, blk]
            yg = _dot(m.astype(BF16), plw_ref[g]) + plb_ref[:, blk]
            mix_ref[:, blk] = (yg * psc_ref[:, blk]).astype(BF16)

    g_off = pl.multiple_of((h // (SSD_HEADS // SSD_GROUPS)) * SSD_STATE, SSD_STATE)
    h_off = pl.multiple_of(h * SSD_HEAD_DIM, SSD_HEAD_DIM)
    b_g = b_t[pl.ds(g_off, SSD_STATE), :]
    c_g = c_t[pl.ds(g_off, SSD_STATE), :]
    dec = dec_t[pl.ds(h, 1), :]
    xdt = xdt_t[pl.ds(h_off, SSD_HEAD_DIM), :]
    y_rows = []
    for p in range(SSD_HEAD_DIM):
        s_new = st_ref[0, p] * dec + xdt[p:p + 1, :] * b_g
        so_ref[0, p] = s_new
        y_rows.append(jnp.sum(s_new * c_g, axis=0, keepdims=True))
    y_t[pl.ds(h_off, SSD_HEAD_DIM), :] = jnp.concatenate(y_rows, axis=0)

    @pl.when(h == pl.num_programs(0) - 1)
    def _():
        xs = xs_keep[...]
        y = y_t[...].T + dsk_ref[...] * xs
        y = y * _silu(z_ref[...])
        width = SSD_WIDTH // SSD_GROUPS
        for g in range(SSD_GROUPS):
            blk = slice(g * width, (g + 1) * width)
            yg = y[:, blk]
            rs = lax.rsqrt(jnp.mean(yg * yg, axis=-1, keepdims=True) + EPS)
            mix_ref[:, POOL_WIDTH + g * width:POOL_WIDTH + (g + 1) * width] = (yg * rs * nw_ref[:, blk]).astype(BF16)
        _out_proj_and_route(x_ref[...], mix_ref[...], wout_ref, gf_ref, rh_ref, rl_ref, rb_ref,
                            h_ref, u_ref, rt_ref, wt_ref, cnt_ref)


def _front_step(x, g_mix, w_in16, pool_t, conv_t, ssm_t, cw, cb, dtb, alog, dsk, nw, plw16, plb, psc,
                w_out16, g_ffn, r_hi, r_lo, r_bias):
    n = x.shape[0]
    fixed2 = lambda h: (0, 0)
    fixed3 = lambda h: (0, 0, 0)
    st_spec = pl.BlockSpec((1, SSD_HEAD_DIM, SSD_STATE, n), lambda h: (h, 0, 0, 0))
    return pl.pallas_call(
        _mix_step_body,
        grid=(SSD_HEADS,),
        in_specs=[pl.BlockSpec((n, D_MODEL), fixed2), pl.BlockSpec((1, D_MODEL), fixed2),
                  pl.BlockSpec((D_MODEL, IN_COLS), fixed2),
                  pl.BlockSpec((POOL_BUF, n, POOL_WIDTH), fixed3),
                  pl.BlockSpec((SSD_CONV - 1, n, CONV_DIM), fixed3),
                  st_spec,
                  pl.BlockSpec((SSD_CONV, CONV_DIM), fixed2), pl.BlockSpec((1, CONV_DIM), fixed2),
                  pl.BlockSpec((1, LANES), fixed2), pl.BlockSpec((1, LANES), fixed2),
                  pl.BlockSpec((1, SSD_WIDTH), fixed2), pl.BlockSpec((1, SSD_WIDTH), fixed2),
                  pl.BlockSpec((len(POOL_WINDOWS), POOL_GROUP_DIM, POOL_GROUP_DIM), fixed3),
                  pl.BlockSpec((1, POOL_WIDTH), fixed2), pl.BlockSpec((1, POOL_WIDTH), fixed2),
                  pl.BlockSpec((D_MODEL, D_MODEL), fixed2), pl.BlockSpec((1, D_MODEL), fixed2),
                  pl.BlockSpec((LANES, D_MODEL), fixed2), pl.BlockSpec((LANES, D_MODEL), fixed2),
                  pl.BlockSpec((LANES, n), fixed2)],
        out_specs=[pl.BlockSpec((n, D_MODEL), fixed2), pl.BlockSpec((n, D_MODEL), fixed2),
                   pl.BlockSpec((1, 8, n), fixed3), pl.BlockSpec((n, 2 * LANES), fixed2),
                   pl.BlockSpec((1, N_EXPERTS, LANES), fixed3),
                   pl.BlockSpec((POOL_BUF, n, POOL_WIDTH), fixed3),
                   pl.BlockSpec((SSD_CONV - 1, n, CONV_DIM), fixed3), st_spec],
        out_shape=[jax.ShapeDtypeStruct((n, D_MODEL), F32), jax.ShapeDtypeStruct((n, D_MODEL), BF16),
                   jax.ShapeDtypeStruct((1, 8, n), F32), jax.ShapeDtypeStruct((n, 2 * LANES), F32),
                   jax.ShapeDtypeStruct((1, N_EXPERTS, LANES), F32),
                   jax.ShapeDtypeStruct(pool_t.shape, F32),
                   jax.ShapeDtypeStruct(conv_t.shape, F32), jax.ShapeDtypeStruct(ssm_t.shape, F32)],
        scratch_shapes=[pltpu.VMEM((SSD_WIDTH, n), F32), pltpu.VMEM((LANES, n), F32),
                        pltpu.VMEM((SSD_BC, n), F32), pltpu.VMEM((SSD_BC, n), F32),
                        pltpu.VMEM((n, SSD_WIDTH), F32), pltpu.VMEM((SSD_WIDTH, n), F32),
                        pltpu.VMEM((n, SSD_WIDTH), F32), pltpu.VMEM((n, D_MODEL), BF16)],
        compiler_params=_cparams("arbitrary"),
        name="front_step",
    )(x, g_mix, w_in16, pool_t, conv_t, ssm_t, cw, cb, dtb, alog, dsk, nw, plw16, plb, psc,
      w_out16, g_ffn, r_hi, r_lo, r_bias)


PROMPT_TILE = 512


def _sort_tables(counts, tile_tokens, data_rows):
    cnt = jnp.concatenate(counts, axis=0)
    pc = (cnt + RUN_PAD - 1) // RUN_PAD * RUN_PAD
    off_local = jnp.cumsum(pc, axis=1) - pc
    tile_rows = jnp.sum(pc, axis=1)
    region = (jnp.sum(pc, axis=0) + MOE_TILE - 1) // MOE_TILE * MOE_TILE
    base = jnp.cumsum(region) - region
    dst = base[None, :] + jnp.cumsum(pc, axis=0) - pc
    per_pass = []
    lo = 0
    for c, tm in zip(counts, tile_tokens):
        hi = lo + c.shape[0]
        n_chunk = _sorted_rows_per_tile(tm) // RUN_PAD
        s = jnp.arange(n_chunk, dtype=jnp.int32) * RUN_PAD
        begins = off_local[lo:hi, None, :]
        ends = begins + pc[lo:hi, None, :]
        inside = (begins <= s[None, :, None]) & (s[None, :, None] < ends)
        shift = jnp.sum(jnp.where(inside, dst[lo:hi, None, :] - begins, 0), axis=2)
        valid = s[None, :] < tile_rows[lo:hi, None]
        parity = (jnp.arange(hi - lo, dtype=jnp.int32) % 2)[:, None]
        spare = data_rows + parity * _sorted_rows_per_tile(max(tile_tokens)) + s[None, :]
        scatter_dst = jnp.where(valid, s[None, :] + shift, spare).astype(jnp.int32)
        gather_src = jnp.where(valid, s[None, :] + shift, 0).astype(jnp.int32)
        off_v = jnp.broadcast_to(off_local[lo:hi, :, None].astype(F32), (hi - lo, N_EXPERTS, LANES))
        per_pass.append((scatter_dst, gather_src, off_v))
        lo = hi
    tiles_cum = jnp.cumsum(region // MOE_TILE)
    n_active = tiles_cum[-1]
    return per_pass, tiles_cum, n_active


def _max_sorted_rows(tile_counts, tile_tokens):
    rows = sum(n * (2 * tm + N_EXPERTS * (RUN_PAD - 1)) for n, tm in zip(tile_counts, tile_tokens))
    data_rows = (-(-rows // MOE_TILE) + N_EXPERTS) * MOE_TILE
    return data_rows, data_rows + 2 * _sorted_rows_per_tile(max(tile_tokens))


def _row(v):
    return v.reshape(1, -1).astype(F32)


def _per_head_rows(v):
    return jnp.broadcast_to(v.astype(F32)[:, None], (SSD_HEADS, SSD_Q))


def _pad_lanes(v):
    return jnp.pad(v.reshape(1, -1).astype(F32), ((0, 0), (0, LANES - v.size)))


def kernel(x_prompt, x_sample, p_prompt, p_sample, state_pool, state_conv, state_ssm, norm_mix, w_in, pool_lin_w, pool_lin_b, pool_scale, conv_w, conv_b, dt_bias, a_log, d_skip, ssd_norm, w_out, norm_ffn, router_grp_w, router_grp_b, router_exp_w, router_exp_b, exp_w_gate, exp_w_up, exp_w_down, norm_ple, ple_gate_w, ple_gate_b, ple_proj_w, norm_final):
    nb, seq, _ = x_prompt.shape
    ns = x_sample.shape[0]
    assert ns == LANES and x_sample.shape[1] == 1 and seq % MIX_ROWS == 0 and seq >= POOL_BUF

    w_in16 = jnp.concatenate([w_in[0].astype(BF16), jnp.zeros((D_MODEL, IN_COLS - w_in.shape[2]), BF16)], axis=1)
    w_out16 = w_out[0].astype(BF16)
    g_mix, g_ffn, g_ple, g_fin = _row(norm_mix[0]), _row(norm_ffn[0]), _row(norm_ple[0]), _row(norm_final)
    cw, cb = conv_w[0].astype(F32), _row(conv_b[0])
    dtb, alog = _pad_lanes(dt_bias[0]), _pad_lanes(a_log[0])
    dsk = _row(jnp.repeat(d_skip[0], SSD_HEAD_DIM))
    nw = _row(ssd_norm[0])
    plw16 = pool_lin_w[0].astype(BF16)
    plb, psc = _row(pool_lin_b[0]), _row(pool_scale[0])
    zeros4 = jnp.zeros((D_MODEL, 8 - N_EXPERT_GROUPS), F32)
    r_w = jnp.concatenate([router_grp_w[0], zeros4, router_exp_w[0],
                           jnp.zeros((D_MODEL, LANES - 8 - N_EXPERTS), F32)], axis=1).T
    r_hi = r_w.astype(BF16)
    r_lo = (r_w - r_hi.astype(F32)).astype(BF16)
    r_b = jnp.concatenate([router_grp_b[0], jnp.zeros((8 - N_EXPERT_GROUPS,), F32), router_exp_b[0],
                           jnp.zeros((LANES - 8 - N_EXPERTS,), F32)])
    wg = exp_w_gate[0].reshape(N_EXPERTS, D_MODEL, EXPERT_FF)
    wu = exp_w_up[0].reshape(N_EXPERTS, D_MODEL, EXPERT_FF)
    wd = exp_w_down[0].reshape(N_EXPERTS, EXPERT_FF, D_MODEL)
    pg16 = ple_gate_w[0].astype(BF16)
    pgb = _row(ple_gate_b[0])
    pp16 = ple_proj_w[0].astype(BF16)

    xp = x_prompt.reshape(nb * seq, D_MODEL)
    h1_p, u_p, rt_p, gate_p, cnt_p, st, pool_tail, conv_tail = _front_prompt(
        xp, g_mix, w_in16, cw, cb, _per_head_rows(dt_bias[0]), _per_head_rows(a_log[0]), dsk, nw, plw16, plb, psc,
        w_out16, g_ffn, r_hi, r_lo, jnp.broadcast_to(r_b[:, None], (LANES, MIX_ROWS)), nb, seq)
    pool_p = pool_tail[:, TAIL_ROWS - POOL_BUF:]
    conv_p = conv_tail[:, CONV_HDR - (SSD_CONV - 1):]
    ssm_p = st

    xs_ = x_sample.reshape(ns, D_MODEL)
    h1_s, u_s, rt_s, gate_s, cnt_s, pool_t, conv_t, ssm_t = _front_step(
        xs_, g_mix, w_in16, jnp.transpose(state_pool[0], (1, 0, 2)), jnp.transpose(state_conv[0], (1, 0, 2)),
        jnp.transpose(state_ssm[0], (1, 2, 3, 0)), cw, cb, dtb, alog, dsk, nw, plw16, plb, psc,
        w_out16, g_ffn, r_hi, r_lo, jnp.broadcast_to(r_b[:, None], (LANES, ns)))
    pool_s = jnp.transpose(pool_t, (1, 0, 2))
    conv_s = jnp.transpose(conv_t, (1, 0, 2))
    ssm_s = jnp.transpose(ssm_t, (3, 0, 1, 2))

    counts = [cnt_p[:, :, 0].astype(jnp.int32), cnt_s[:, :, 0].astype(jnp.int32)]
    tiles = (PROMPT_TILE, ns)
    data_rows, total_rows = _max_sorted_rows([c.shape[0] for c in counts], tiles)
    (tab_p, tab_s), tiles_cum, n_active = _sort_tables(counts, tiles, data_rows)
    xs_sorted, dest_p = _sort_tokens(tab_p[0], u_p, rt_p, tab_p[2], None, PROMPT_TILE, total_rows)
    xs_sorted, dest_s = _sort_tokens(tab_s[0], u_s, rt_s, tab_s[2], xs_sorted, ns, total_rows)
    tile_start = jnp.concatenate([jnp.zeros((1,), jnp.int32), tiles_cum.astype(jnp.int32)])
    ys_sorted = _moe_sorted(tile_start, n_active.reshape(1).astype(jnp.int32), xs_sorted, wg, wu, wd)

    y_prompt = _ple(tab_p[1], h1_p, dest_p, gate_p, p_prompt[0].reshape(nb * seq, PLE_DIM),
                    g_ple, pg16, pgb, pp16, g_fin, ys_sorted, PROMPT_TILE)
    y_sample = _ple(tab_s[1], h1_s, dest_s, gate_s, p_sample[0].reshape(ns, PLE_DIM),
                    g_ple, pg16, pgb, pp16, g_fin, ys_sorted, ns)

    return (y_prompt.reshape(nb, seq, D_MODEL), y_sample.reshape(ns, 1, D_MODEL),
            pool_p[None], conv_p[None], ssm_p[None], pool_s[None], conv_s[None], ssm_s[None])
```

```python
import functools

import jax
import jax.numpy as jnp
import numpy as np
from jax import lax
from jax.experimental import pallas as pl
from jax.experimental.pallas import tpu as pltpu

F32 = jnp.float32
BF16 = jnp.bfloat16

D_MODEL = 1024
POOL_WIDTH = 512
POOL_WINDOWS = (2, 4, 8, 16)
POOL_GROUP_DIM = 128
POOL_BUF = 15
SSD_WIDTH = 512
SSD_HEAD_DIM = 64
SSD_HEADS = 8
SSD_GROUPS = 2
SSD_STATE = 64
SSD_CONV = 4
SSD_BC = SSD_GROUPS * SSD_STATE
CONV_DIM = SSD_WIDTH + 2 * SSD_BC
N_EXPERT_GROUPS = 4
EXPERTS_PER_GROUP = 8
N_EXPERTS = N_EXPERT_GROUPS * EXPERTS_PER_GROUP
EXPERT_FF = 256
PLE_DIM = 256
PAST_LEN = 16384
EPS = 1e-6

LANES = 128
IN_COLS = 1920
DT_OFF = POOL_WIDTH + SSD_WIDTH + CONV_DIM
VMEM_LIMIT = 56 * 1024 * 1024


def _cparams(*sem):
    return pltpu.CompilerParams(dimension_semantics=sem, vmem_limit_bytes=VMEM_LIMIT)


def _rms(x, g):
    return x * lax.rsqrt(jnp.mean(x * x, axis=-1, keepdims=True) + EPS) * g


def _sigmoid(x):
    return 1.0 / (1.0 + jnp.exp(-x))


def _silu(x):
    return x * _sigmoid(x)


def _split3(v):
    hi = v.astype(BF16)
    r = v - hi.astype(F32)
    mid = r.astype(BF16)
    lo = (r - mid.astype(F32)).astype(BF16)
    return hi, mid, lo


def _dot(a, b):
    return jnp.dot(a, b, preferred_element_type=F32)


def _dot_nt(a, b):
    return lax.dot_general(a, b, (((1,), (1,)), ((), ())), preferred_element_type=F32)


def _dot_tn(a, b):
    return lax.dot_general(a, b, (((0,), (0,)), ((), ())), preferred_element_type=F32)


def _route(lg):
    tm = lg.shape[1]
    gl = lg[0:N_EXPERT_GROUPS, :]
    gmax = jnp.max(gl, axis=0, keepdims=True)
    gsum = jnp.sum(jnp.exp(gl - gmax), axis=0, keepdims=True)
    g_w = 1.0 / gsum
    gi = lax.broadcasted_iota(jnp.int32, gl.shape, 0)
    g_idx = jnp.min(jnp.where(gl == gmax, gi, N_EXPERT_GROUPS), axis=0, keepdims=True)
    sel = jnp.zeros((EXPERTS_PER_GROUP, tm), F32)
    for g in range(N_EXPERT_GROUPS):
        blk = lg[8 + g * EXPERTS_PER_GROUP:8 + (g + 1) * EXPERTS_PER_GROUP, :]
        sel = jnp.where(g_idx == g, blk, sel)
    ei = lax.broadcasted_iota(jnp.int32, sel.shape, 0)
    m1 = jnp.max(sel, axis=0, keepdims=True)
    i1 = jnp.min(jnp.where(sel == m1, ei, EXPERTS_PER_GROUP), axis=0, keepdims=True)
    rest = jnp.where(ei == i1, -jnp.inf, sel)
    m2 = jnp.max(rest, axis=0, keepdims=True)
    i2 = jnp.min(jnp.where(rest == m2, ei, EXPERTS_PER_GROUP), axis=0, keepdims=True)
    p2 = jnp.exp(m2 - m1)
    w1 = g_w / (1.0 + p2)
    w2 = g_w * p2 / (1.0 + p2)
    return g_idx * EXPERTS_PER_GROUP + i1, g_idx * EXPERTS_PER_GROUP + i2, w1, w2


def _out_proj_and_route(x, mix16, w_ref, g_ref, rw_ref, rb_ref, h_ref, u_ref, rt_ref, wt_ref, cnt_ref,
                        r0=0, accumulate=False):
    tm = x.shape[0]
    h = x + _dot(mix16, w_ref[...])
    h_ref[r0:r0 + tm, :] = h
    _route_rows(h, g_ref, rw_ref, rb_ref, u_ref, rt_ref, wt_ref, cnt_ref, r0=r0, accumulate=accumulate)


def _route_rows(h, g_ref, rw_ref, rb_ref, u_ref, rt_ref, wt_ref, cnt_ref, r0=0, accumulate=False):
    tm = h.shape[0]
    u = _rms(h, g_ref[...])
    u_hi = u.astype(BF16)
    u_ref[r0:r0 + tm, :] = u_hi
    u_lo = (u - u_hi.astype(F32)).astype(BF16)
    both = _dot(u_hi, rw_ref[...])
    lg_rows = both[:, 0:LANES] + both[:, LANES:2 * LANES] + _dot(u_lo, rw_ref[:, 0:LANES])
    lg = lg_rows.T + rb_ref[:, r0:r0 + tm]
    b1, b2, w1, w2 = _route(lg)
    r8 = lax.broadcasted_iota(jnp.int32, (8, tm), 0)
    rt_ref[0, :, r0:r0 + tm] = jnp.where(r8 == 0, b1.astype(F32), jnp.where(r8 == 1, b2.astype(F32), 0.0))
    wt_ref[r0:r0 + tm, :] = jnp.concatenate(
        [jnp.broadcast_to(w1, (LANES, tm)).T, jnp.broadcast_to(w2, (LANES, tm)).T], axis=1)
    kio = lax.broadcasted_iota(jnp.int32, (N_EXPERTS, tm), 0)
    hits = ((kio == b1) | (kio == b2)).astype(F32)
    cnt = jnp.broadcast_to(jnp.sum(hits, axis=1, keepdims=True), (N_EXPERTS, LANES))
    cnt_ref[0] = cnt_ref[0] + cnt if accumulate else cnt


RUN_PAD = 16
MOE_TILE = 256
MOE_GROUP = 3
MOE_AHEAD = 8
MOE_BUFS = MOE_AHEAD + MOE_GROUP + 1


def _tile_lanes(v, width):
    reps = width // LANES
    return v if reps == 1 else jnp.concatenate([v] * reps, axis=1)


def _sorted_rows_per_tile(tm):
    need = 2 * tm + N_EXPERTS * (RUN_PAD - 1)
    return -(-need // LANES) * LANES


def _sort_body(cd_ref, u_ref, rt_ref, off_ref, before_ref, *rest):
    xs_ref, dt_ref, loc, sem = rest[-4:]
    i = pl.program_id(0)
    tm = u_ref.shape[0]
    rows = loc.shape[1]
    rt = rt_ref[0]
    b1 = rt[0:1, :].astype(jnp.int32)
    b2 = rt[1:2, :].astype(jnp.int32)
    kio = lax.broadcasted_iota(jnp.int32, (N_EXPERTS, tm), 0)
    o1 = kio == b1
    o2 = kio == b2
    start = _dot((o1 | o2).astype(BF16), before_ref[...]) + _tile_lanes(off_ref[0], tm)
    d1 = jnp.sum(jnp.where(o1, start, 0.0), axis=0, keepdims=True)
    d2 = jnp.sum(jnp.where(o2, start, 0.0), axis=0, keepdims=True)
    dt_ref[...] = jnp.concatenate([jnp.broadcast_to(d1, (LANES, tm)).T, jnp.broadcast_to(d2, (LANES, tm)).T], axis=1)
    rio = lax.broadcasted_iota(jnp.int32, (rows, tm), 0)
    perm = ((rio == d1.astype(jnp.int32)) | (rio == d2.astype(jnp.int32))).astype(BF16)
    slot = i % 2
    loc[slot] = _dot(perm, u_ref[...]).astype(BF16)

    def chunk_copy(step, sl, c):
        return pltpu.make_async_copy(
            loc.at[sl, pl.ds(c * RUN_PAD, RUN_PAD), :],
            xs_ref.at[pl.ds(pl.multiple_of(cd_ref[step, c], RUN_PAD), RUN_PAD), :], sem.at[sl])

    n_chunk = rows // RUN_PAD
    for c in range(n_chunk):
        chunk_copy(i, slot, c).start()

    @pl.when(i > 0)
    def _():
        for c in range(n_chunk):
            chunk_copy(i - 1, 1 - slot, c).wait()

    @pl.when(i == pl.num_programs(0) - 1)
    def _():
        for c in range(n_chunk):
            chunk_copy(i, slot, c).wait()


def _sort_tokens(chunk_dst, u16, rt, off_v, xs_prev, tm, total_rows):
    t = u16.shape[0]
    rows = _sorted_rows_per_tile(tm)
    in_specs = [pl.BlockSpec((tm, D_MODEL), lambda i, cd: (i, 0)),
                pl.BlockSpec((1, 8, tm), lambda i, cd: (i, 0, 0)),
                pl.BlockSpec((1, N_EXPERTS, LANES), lambda i, cd: (i, 0, 0)),
                pl.BlockSpec((tm, tm), lambda i, cd: (0, 0))]
    before = jnp.asarray(np.triu(np.ones((tm, tm), np.float32), 1)).astype(BF16)
    args = [chunk_dst, u16, rt, off_v, before]
    aliases = {}
    if xs_prev is not None:
        in_specs.append(pl.BlockSpec(memory_space=pl.ANY))
        aliases = {len(args): 0}
        args.append(xs_prev)
    return pl.pallas_call(
        _sort_body,
        grid_spec=pltpu.PrefetchScalarGridSpec(
            num_scalar_prefetch=1,
            grid=(t // tm,),
            in_specs=in_specs,
            out_specs=[pl.BlockSpec(memory_space=pl.ANY),
                       pl.BlockSpec((tm, 2 * LANES), lambda i, cd: (i, 0))],
            scratch_shapes=[pltpu.VMEM((2, rows, D_MODEL), BF16), pltpu.SemaphoreType.DMA((2,))]),
        out_shape=[jax.ShapeDtypeStruct((total_rows, D_MODEL), BF16),
                   jax.ShapeDtypeStruct((t, 2 * LANES), F32)],
        input_output_aliases=aliases,
        compiler_params=_cparams("arbitrary"),
        name="sort_tokens",
    )(*args)


def _moe_body(ts_ref, na_ref, xs_ref, wg_ref, wu_ref, wd_ref, ys_ref, xbuf, ybuf, wg16, wu16, wd16, sem_in, sem_out):
    k = pl.program_id(0)
    n_act = na_ref[0]

    def x_copy(g, sl):
        return pltpu.make_async_copy(xs_ref.at[pl.ds(pl.multiple_of(g * MOE_TILE, MOE_TILE), MOE_TILE), :],
                                     xbuf.at[sl], sem_in.at[sl])

    def y_copy(g, sl):
        return pltpu.make_async_copy(ybuf.at[sl],
                                     ys_ref.at[pl.ds(pl.multiple_of(g * MOE_TILE, MOE_TILE), MOE_TILE), :],
                                     sem_out.at[sl])

    @pl.when(k == 0)
    def _():
        for j in range(MOE_AHEAD):
            @pl.when(j < n_act)
            def _():
                x_copy(j, j).start(priority=1)

    wg16[...] = wg_ref[0].astype(BF16)
    wu16[...] = wu_ref[0].astype(BF16)
    wd16[...] = wd_ref[0].astype(BF16)

    def arrive(g):
        sl = g % MOE_BUFS

        @pl.when(g + MOE_AHEAD < n_act)
        def _():
            x_copy(g + MOE_AHEAD, (g + MOE_AHEAD) % MOE_BUFS).start(priority=1)

        x_copy(g, sl).wait()

        @pl.when(g >= MOE_BUFS)
        def _():
            y_copy(g - MOE_BUFS, sl).wait()

    def tiles(g, count):
        for j in range(count):
            arrive(g + j)
        acts = []
        for j in range(count):
            x = xbuf[(g + j) % MOE_BUFS]
            acts.append((_silu(_dot(x, wg16[...])) * _dot(x, wu16[...])).astype(BF16))
        for j in range(count):
            ybuf[(g + j) % MOE_BUFS] = _dot(acts[j], wd16[...]).astype(BF16)
        for j in range(count):
            y_copy(g + j, (g + j) % MOE_BUFS).start()

    g0 = ts_ref[k]
    n_here = ts_ref[k + 1] - g0

    def group(p, carry):
        tiles(g0 + MOE_GROUP * p, MOE_GROUP)
        return carry

    n_groups = n_here // MOE_GROUP
    lax.fori_loop(0, n_groups, group, 0)
    for rest in range(1, MOE_GROUP):
        @pl.when(n_here - n_groups * MOE_GROUP == rest)
        def _():
            tiles(g0 + n_groups * MOE_GROUP, rest)

    @pl.when(k == pl.num_programs(0) - 1)
    def _():
        for j in range(1, MOE_BUFS + 1):
            @pl.when(n_act >= j)
            def _():
                y_copy(n_act - j, (n_act - j) % MOE_BUFS).wait()


def _moe_sorted(tile_start, n_active, xs, wg, wu, wd):
    w_map = lambda k, ts, na: (k, 0, 0)
    return pl.pallas_call(
        _moe_body,
        grid_spec=pltpu.PrefetchScalarGridSpec(
            num_scalar_prefetch=2,
            grid=(N_EXPERTS,),
            in_specs=[pl.BlockSpec(memory_space=pl.ANY),
                      pl.BlockSpec((1, D_MODEL, EXPERT_FF), w_map),
                      pl.BlockSpec((1, D_MODEL, EXPERT_FF), w_map),
                      pl.BlockSpec((1, EXPERT_FF, D_MODEL), w_map)],
            out_specs=pl.BlockSpec(memory_space=pl.ANY),
            scratch_shapes=[pltpu.VMEM((MOE_BUFS, MOE_TILE, D_MODEL), BF16),
                            pltpu.VMEM((MOE_BUFS, MOE_TILE, D_MODEL), BF16),
                            pltpu.VMEM((D_MODEL, EXPERT_FF), BF16), pltpu.VMEM((D_MODEL, EXPERT_FF), BF16),
                            pltpu.VMEM((EXPERT_FF, D_MODEL), BF16),
                            pltpu.SemaphoreType.DMA((MOE_BUFS,)), pltpu.SemaphoreType.DMA((MOE_BUFS,))]),
        out_shape=jax.ShapeDtypeStruct(xs.shape, BF16),
        compiler_params=_cparams("arbitrary"),
        name="moe_sorted",
    )(tile_start, n_active, xs, wg, wu, wd)


def _ple_body(cd_ref, h_ref, dt_ref, wt_ref, p_ref, gn_ref, wg_ref, bg_ref, wp_ref, fn_ref, ys_ref,
              o_ref, loc, sem):
    i = pl.program_id(0)
    tm = h_ref.shape[0]
    rows = loc.shape[1]
    n_chunk = rows // RUN_PAD
    slot = i % 2

    def chunk_copy(step, sl, c):
        return pltpu.make_async_copy(
            ys_ref.at[pl.ds(pl.multiple_of(cd_ref[step, c], RUN_PAD), RUN_PAD), :],
            loc.at[sl, pl.ds(c * RUN_PAD, RUN_PAD), :], sem.at[sl])

    def fetch(step, sl):
        for c in range(n_chunk):
            chunk_copy(step, sl, c).start()

    @pl.when(i == 0)
    def _():
        fetch(0, 0)

    @pl.when(i + 1 < pl.num_programs(0))
    def _():
        fetch(i + 1, 1 - slot)

    for c in range(n_chunk):
        chunk_copy(i, slot, c).wait()

    ci = lax.broadcasted_iota(jnp.int32, (tm, LANES), 1)
    d1 = dt_ref[:, 0:LANES].astype(jnp.int32)
    d2 = dt_ref[:, LANES:2 * LANES].astype(jnp.int32)
    w1 = wt_ref[:, 0:LANES]
    w2 = wt_ref[:, LANES:2 * LANES]
    sel = jnp.concatenate(
        [jnp.where(d1 == ci + m * LANES, w1, jnp.where(d2 == ci + m * LANES, w2, 0.0)) for m in range(rows // LANES)],
        axis=1).astype(BF16)
    h = h_ref[...] + _dot(sel, loc[slot])
    a16 = _rms(h, gn_ref[...]).astype(BF16)
    gate = _sigmoid(_dot(a16, wg_ref[...]) + bg_ref[...])
    pp = _dot(p_ref[...].astype(BF16), wp_ref[...])
    h = h + gate * pp
    o_ref[...] = _rms(h, fn_ref[...])


def _ple(chunk_src, h1, dest_t, gate_t, p, g_ple, wg16, bg, wp16, g_final, ys, tm):
    t = h1.shape[0]
    row = lambda i, cd: (i, 0)
    fixed = lambda i, cd: (0, 0)
    return pl.pallas_call(
        _ple_body,
        grid_spec=pltpu.PrefetchScalarGridSpec(
            num_scalar_prefetch=1,
            grid=(t // tm,),
            in_specs=[pl.BlockSpec((tm, D_MODEL), row), pl.BlockSpec((tm, 2 * LANES), row),
                      pl.BlockSpec((tm, 2 * LANES), row), pl.BlockSpec((tm, PLE_DIM), row),
                      pl.BlockSpec((1, D_MODEL), fixed), pl.BlockSpec((D_MODEL, D_MODEL), fixed),
                      pl.BlockSpec((1, D_MODEL), fixed), pl.BlockSpec((PLE_DIM, D_MODEL), fixed),
                      pl.BlockSpec((1, D_MODEL), fixed), pl.BlockSpec(memory_space=pl.ANY)],
            out_specs=pl.BlockSpec((tm, D_MODEL), row),
            scratch_shapes=[pltpu.VMEM((2, _sorted_rows_per_tile(tm), D_MODEL), BF16),
                            pltpu.SemaphoreType.DMA((2,))]),
        out_shape=jax.ShapeDtypeStruct((t, D_MODEL), F32),
        compiler_params=_cparams("arbitrary"),
        name="ple_final",
    )(chunk_src, h1, dest_t, gate_t, p, g_ple, wg16, bg, wp16, g_final, ys)


MIX_ROWS = 512
SSD_Q = 128
POOL_HDR = 128
CONV_HDR = 8
TAIL_ROWS = 16


def _softplus(x):
    return jnp.maximum(x, 0.0) + jnp.log1p(jnp.exp(-jnp.abs(x)))


def _stack_terms(v, terms):
    parts = [t.astype(F32) for t in _split3(v)[:terms]]
    parts.append(jnp.zeros(((4 - terms) * v.shape[0], v.shape[1]), F32))
    return jnp.concatenate(parts, axis=0).astype(BF16)


def _cumsum_lanes(v, tri16):
    r = _dot_nt(_stack_terms(v, 3), tri16)
    n = v.shape[0]
    return r[0:n] + r[n:2 * n] + r[2 * n:3 * n]


def _expand_heads(v, expand32, terms):
    return _dot_tn(_stack_terms(v, terms), expand32)


def _mixer_constants():
    q = SSD_Q
    li = np.arange(q)[:, None]
    tri = (li >= np.arange(q)[None, :])
    expand = (np.arange(SSD_WIDTH)[None, :] // SSD_HEAD_DIM) == (np.arange(4 * SSD_HEADS)[:, None] % SSD_HEADS)
    hpg = SSD_HEADS // SSD_GROUPS
    block = (np.arange(SSD_BC)[:, None] // SSD_STATE) == (np.arange(SSD_WIDTH)[None, :] // (SSD_HEAD_DIM * hpg))
    wj = np.arange(2 * q)[None, :]
    band = np.stack([(wj <= li + POOL_HDR) & (wj > li + POOL_HDR - w) for w in POOL_WINDOWS])
    inv_count = np.stack([np.broadcast_to(1.0 / np.minimum(li + 1, w), (q, LANES)) for w in POOL_WINDOWS])
    as_f32 = lambda m: jnp.asarray(m.astype(np.float32))
    return (as_f32(tri), as_f32(tri).astype(BF16), as_f32(expand).astype(BF16), as_f32(block),
            as_f32(band).astype(BF16), as_f32(inv_count))


def _front_body(steps_per_seq, x_ref, xp_ref, gm_ref, win_ref, trif_ref, tri_ref, exp_ref, bmask_ref, band_ref,
                invc_ref, cw_ref, cb_ref, dtb_ref, alog_ref, dsk_ref, nw_ref,
                plw_ref, plb_ref, psc_ref, wout_ref, gf_ref, rw_ref, rb_ref,
                h_ref, u_ref, rt_ref, wt_ref, cnt_ref, st_ref, ptail_ref, ctail_ref,
                pool_ext, conv_ext, s_ref, z_ref, dt_ref, mix_ref, mixp_ref, a_ref):
    s = pl.program_id(0)
    n_tiles = pl.num_programs(0) - 1
    real = s < n_tiles
    c = jnp.minimum(s, n_tiles - 1) % steps_per_seq
    rows = MIX_ROWS
    q_len = SSD_Q

    @pl.when(s == 0)
    def _():
        mixp_ref[...] = jnp.zeros_like(mixp_ref)

    @pl.when(c == 0)
    def _():
        pool_ext[0:POOL_HDR, :] = jnp.zeros((POOL_HDR, POOL_WIDTH), F32)
        conv_ext[0:CONV_HDR, :] = jnp.zeros((CONV_HDR, CONV_DIM), F32)
        s_ref[...] = jnp.zeros_like(s_ref)

    @pl.when(c > 0)
    def _():
        pool_ext[0:POOL_HDR, :] = pool_ext[rows:rows + POOL_HDR, :]
        conv_ext[0:CONV_HDR, :] = conv_ext[rows:rows + CONV_HDR, :]

    causal = trif_ref[...] > 0.5
    tri = tri_ref[...]
    expand = exp_ref[...]
    blockmask = bmask_ref[...]
    lane = lax.broadcasted_iota(jnp.int32, (q_len, LANES), 1)
    left = lane < SSD_HEAD_DIM
    a_neg = -jnp.exp(alog_ref[...])

    half = rows // 2
    n_cols = 256

    def norm_rows():
        for r0 in (0, half):
            a_ref[r0:r0 + half, :] = _rms(x_ref[r0:r0 + half, :], gm_ref[...]).astype(BF16)

    def in_proj_chunk(ref, row, k, col0):
        def run():
            ref[row:row + rows, k:k + n_cols] = _dot(a_ref[...], win_ref[:, col0 + k:col0 + k + n_cols])
        return run

    def dt_chunk():
        dt_ref[...] = _dot(a_ref[...], win_ref[:, DT_OFF:IN_COLS])

    conv_chunks = [in_proj_chunk(conv_ext, CONV_HDR, k, POOL_WIDTH + SSD_WIDTH) for k in range(0, CONV_DIM, n_cols)]
    z_chunks = [in_proj_chunk(z_ref, 0, k, POOL_WIDTH) for k in range(0, SSD_WIDTH, n_cols)]
    pool_chunks = [in_proj_chunk(pool_ext, POOL_HDR, k, 0) for k in range(0, POOL_WIDTH, n_cols)]

    def out_proj_chunk(k):
        def run():
            h_ref[:, k:k + n_cols] = xp_ref[:, k:k + n_cols] + _dot(mixp_ref[...], wout_ref[:, k:k + n_cols])
        return run

    def route(r0):
        def run():
            _route_rows(h_ref[r0:r0 + half, :], gf_ref, rw_ref, rb_ref, u_ref, rt_ref, wt_ref, cnt_ref,
                        r0=r0, accumulate=r0 > 0)
        return run

    out_chunks = [out_proj_chunk(k) for k in range(0, D_MODEL, n_cols)]
    routes = [route(0), route(half)]

    def mix_pair(q_first, fillers):
        fillers = list(fillers)

        def fill():
            if fillers:
                fillers.pop(0)()

        subs = range(q_first, q_first + 2)
        xs_l, b16_l, call_l, c16_l = {}, {}, {}, {}
        for q in subs:
            base = CONV_HDR + q * q_len
            conv = cb_ref[...] + conv_ext[base - 3:base - 3 + q_len, :] * cw_ref[0:1, :]
            conv = conv + conv_ext[base - 2:base - 2 + q_len, :] * cw_ref[1:2, :]
            conv = conv + conv_ext[base - 1:base - 1 + q_len, :] * cw_ref[2:3, :]
            conv = conv + conv_ext[base:base + q_len, :] * cw_ref[3:4, :]
            conv = _silu(conv)
            xs_l[q] = conv[:, 0:SSD_WIDTH]
            b16_l[q] = conv[:, SSD_WIDTH:SSD_WIDTH + SSD_BC].astype(BF16)
            call_l[q] = conv[:, SSD_WIDTH + SSD_BC:CONV_DIM]
            c16_l[q] = call_l[q].astype(BF16)
            fill()
        dtt_l = {q: _softplus(dt_ref[q * q_len:(q + 1) * q_len, :].T[0:SSD_HEADS, :] + dtb_ref[...]) for q in subs}
        acst_l = {q: _cumsum_lanes(dtt_l[q] * a_neg, tri) for q in subs}
        acs_l = {q: jnp.concatenate([acst_l[q], jnp.zeros((LANES - SSD_HEADS, q_len), F32)], axis=0).T
                 for q in subs}
        fill()
        dtx_l = {q: _expand_heads(dtt_l[q], expand, 2) for q in subs}
        acsx_l = {q: _expand_heads(acst_l[q], expand, 3) for q in subs}
        lastx_l = {q: acsx_l[q][q_len - 1:q_len, :] for q in subs}
        xdt_l = {q: xs_l[q] * dtx_l[q] for q in subs}
        xdt16_l = {q: xdt_l[q].astype(BF16) for q in subs}
        fill()
        contrib_l = {q: _dot_tn(b16_l[q], (xdt_l[q] * jnp.exp(lastx_l[q] - acsx_l[q])).astype(BF16)) * blockmask
                     for q in subs}
        cb_l = {q: [_dot_nt(jnp.where(lax.shift_right_logical(lane, 6) == g, call_l[q], 0.0).astype(BF16), b16_l[q])
                    for g in range(SSD_GROUPS)] for q in subs}
        fill()
        zs_l = {}
        for q in subs:
            zs = []
            for h in range(SSD_HEADS):
                blk = slice((h // 2) * LANES, (h // 2 + 1) * LANES)
                seg = acs_l[q][:, h:h + 1] - acst_l[q][h:h + 1, :]
                decay = jnp.where(causal, jnp.exp(seg), 0.0)
                scores = (cb_l[q][h // (SSD_HEADS // SSD_GROUPS)] * decay).astype(BF16)
                zs.append(_dot(scores, xdt16_l[q][:, blk]))
                if h % 4 == 3:
                    fill()
            zs_l[q] = zs
        yoff_l = {}
        for q in subs:
            s_old = s_ref[...]
            yoff_l[q] = _dot(c16_l[q], s_old.astype(BF16)) * jnp.exp(acsx_l[q])
            s_ref[...] = s_old * jnp.exp(lastx_l[q]) + contrib_l[q]
        fill()
        for q in subs:
            r0 = q * q_len
            z = z_ref[r0:r0 + q_len, :]
            y_blocks = []
            for j in range(SSD_HEADS // 2):
                blk = slice(j * LANES, (j + 1) * LANES)
                y = jnp.where(left, zs_l[q][2 * j], zs_l[q][2 * j + 1]) + yoff_l[q][:, blk]
                y = y + dsk_ref[:, blk] * xs_l[q][:, blk]
                y_blocks.append(y * _silu(z[:, blk]))
            for g in range(SSD_GROUPS):
                y0, y1 = y_blocks[2 * g], y_blocks[2 * g + 1]
                ss = jnp.sum(y0 * y0, axis=-1, keepdims=True) + jnp.sum(y1 * y1, axis=-1, keepdims=True)
                rs = lax.rsqrt(ss * (1.0 / (2 * LANES)) + EPS)
                for k, yk in ((2 * g, y0), (2 * g + 1, y1)):
                    blk = slice(k * LANES, (k + 1) * LANES)
                    out = yk * rs * nw_ref[:, blk]
                    mix_ref[r0:r0 + q_len, POOL_WIDTH + k * LANES:POOL_WIDTH + (k + 1) * LANES] = out.astype(BF16)
            fill()
        for q in subs:
            r0 = q * q_len
            for g, w in enumerate(POOL_WINDOWS):
                blk = slice(g * POOL_GROUP_DIM, (g + 1) * POOL_GROUP_DIM)
                pe = pool_ext[r0:r0 + 2 * q_len, blk]
                hi = pe.astype(BF16)
                lo = (pe - hi.astype(F32)).astype(BF16)
                both = _dot(band_ref[g], jnp.concatenate([hi, lo], axis=1))
                winsum = both[:, 0:POOL_GROUP_DIM] + both[:, POOL_GROUP_DIM:2 * POOL_GROUP_DIM]
                inv = jnp.where(c == 0, invc_ref[g], 1.0 / w) if q == 0 else 1.0 / w
                m = winsum * inv - pe[q_len:2 * q_len, :]
                yg = _dot(m.astype(BF16), plw_ref[g]) + plb_ref[:, blk]
                mix_ref[r0:r0 + q_len, blk] = (yg * psc_ref[:, blk]).astype(BF16)
            fill()
        while fillers:
            fill()

    @pl.when(real)
    def _():
        norm_rows()
        for run in conv_chunks + [dt_chunk]:
            run()
        mix_pair(0, z_chunks + pool_chunks + out_chunks[:2])
        mix_pair(2, out_chunks[2:] + routes)
        mixp_ref[...] = mix_ref[...]

    @pl.when(jnp.logical_not(real))
    def _():
        for run in out_chunks + routes:
            run()

    @pl.when(real & (c == steps_per_seq - 1))
    def _():
        hpg = SSD_HEADS // SSD_GROUPS
        for j in range(SSD_HEADS // 2):
            flipped = s_ref[:, j * LANES:(j + 1) * LANES].T
            g = (2 * j) // hpg
            for k in range(2):
                st_ref[0, 2 * j + k] = flipped[k * SSD_HEAD_DIM:(k + 1) * SSD_HEAD_DIM,
                                               g * SSD_STATE:(g + 1) * SSD_STATE]
        ptail_ref[0] = pool_ext[POOL_HDR + rows - TAIL_ROWS:POOL_HDR + rows, :]
        ctail_ref[0] = conv_ext[rows:rows + CONV_HDR, :]


def _front_prompt(x, g_mix, w_in16, cw, cb, dtb, alog, dsk, nw, plw16, plb, psc, w_out16, g_ffn, r_w16, r_bias,
                  nb, seq):
    steps = seq // MIX_ROWS
    nt = nb * steps
    cur = lambda s: (jnp.minimum(s, nt - 1), 0)
    prev = lambda s: (jnp.maximum(s - 1, 0), 0)
    prev3 = lambda s: (jnp.maximum(s - 1, 0), 0, 0)
    per_seq = lambda s: (jnp.minimum(s, nt - 1) // steps, 0, 0)
    fixed2 = lambda s: (0, 0)
    fixed3 = lambda s: (0, 0, 0)
    return pl.pallas_call(
        functools.partial(_front_body, steps),
        grid=(nt + 1,),
        in_specs=[pl.BlockSpec((MIX_ROWS, D_MODEL), cur), pl.BlockSpec((MIX_ROWS, D_MODEL), prev),
                  pl.BlockSpec((1, D_MODEL), fixed2), pl.BlockSpec((D_MODEL, IN_COLS), fixed2),
                  pl.BlockSpec((SSD_Q, SSD_Q), fixed2), pl.BlockSpec((SSD_Q, SSD_Q), fixed2),
                  pl.BlockSpec((4 * SSD_HEADS, SSD_WIDTH), fixed2), pl.BlockSpec((SSD_BC, SSD_WIDTH), fixed2),
                  pl.BlockSpec((len(POOL_WINDOWS), SSD_Q, 2 * SSD_Q), fixed3),
                  pl.BlockSpec((len(POOL_WINDOWS), SSD_Q, LANES), fixed3),
                  pl.BlockSpec((SSD_CONV, CONV_DIM), fixed2), pl.BlockSpec((1, CONV_DIM), fixed2),
                  pl.BlockSpec((SSD_HEADS, SSD_Q), fixed2), pl.BlockSpec((SSD_HEADS, SSD_Q), fixed2),
                  pl.BlockSpec((1, SSD_WIDTH), fixed2), pl.BlockSpec((1, SSD_WIDTH), fixed2),
                  pl.BlockSpec((len(POOL_WINDOWS), POOL_GROUP_DIM, POOL_GROUP_DIM), fixed3),
                  pl.BlockSpec((1, POOL_WIDTH), fixed2), pl.BlockSpec((1, POOL_WIDTH), fixed2),
                  pl.BlockSpec((D_MODEL, D_MODEL), fixed2), pl.BlockSpec((1, D_MODEL), fixed2),
                  pl.BlockSpec((D_MODEL, 2 * LANES), fixed2),
                  pl.BlockSpec((LANES, MIX_ROWS), fixed2)],
        out_specs=[pl.BlockSpec((MIX_ROWS, D_MODEL), prev), pl.BlockSpec((MIX_ROWS, D_MODEL), prev),
                   pl.BlockSpec((1, 8, MIX_ROWS), prev3), pl.BlockSpec((MIX_ROWS, 2 * LANES), prev),
                   pl.BlockSpec((1, N_EXPERTS, LANES), prev3),
                   pl.BlockSpec((1, SSD_HEADS, SSD_HEAD_DIM, SSD_STATE),
                                lambda s: (jnp.minimum(s, nt - 1) // steps, 0, 0, 0)),
                   pl.BlockSpec((1, TAIL_ROWS, POOL_WIDTH), per_seq),
                   pl.BlockSpec((1, CONV_HDR, CONV_DIM), per_seq)],
        out_shape=[jax.ShapeDtypeStruct((nb * seq, D_MODEL), F32), jax.ShapeDtypeStruct((nb * seq, D_MODEL), BF16),
                   jax.ShapeDtypeStruct((nt, 8, MIX_ROWS), F32), jax.ShapeDtypeStruct((nb * seq, 2 * LANES), F32),
                   jax.ShapeDtypeStruct((nt, N_EXPERTS, LANES), F32),
                   jax.ShapeDtypeStruct((nb, SSD_HEADS, SSD_HEAD_DIM, SSD_STATE), F32),
                   jax.ShapeDtypeStruct((nb, TAIL_ROWS, POOL_WIDTH), F32),
                   jax.ShapeDtypeStruct((nb, CONV_HDR, CONV_DIM), F32)],
        scratch_shapes=[pltpu.VMEM((POOL_HDR + MIX_ROWS, POOL_WIDTH), F32),
                        pltpu.VMEM((CONV_HDR + MIX_ROWS, CONV_DIM), F32),
                        pltpu.VMEM((SSD_BC, SSD_WIDTH), F32),
                        pltpu.VMEM((MIX_ROWS, SSD_WIDTH), F32), pltpu.VMEM((MIX_ROWS, LANES), F32),
                        pltpu.VMEM((MIX_ROWS, D_MODEL), BF16), pltpu.VMEM((MIX_ROWS, D_MODEL), BF16),
                        pltpu.VMEM((MIX_ROWS, D_MODEL), BF16)],
        compiler_params=_cparams("arbitrary"),
        name="front_prompt",
    )(x, x, g_mix, w_in16, *_mixer_constants(), cw, cb, dtb, alog, dsk, nw, plw16, plb, psc, w_out16, g_ffn,
      r_w16, r_bias)


def _mix_step_body(x_ref, gm_ref, win_ref, sp_ref, sc_ref, st_ref, cw_ref, cb_ref, dtb_ref, alog_ref,
                   dsk_ref, nw_ref, plw_ref, plb_ref, psc_ref, wout_ref, gf_ref, rw_ref, rb_ref,
                   h_ref, u_ref, rt_ref, wt_ref, cnt_ref, po_ref, co_ref, so_ref,
                   xdt_t, dec_t, b_t, c_t, xs_keep, y_t, z_ref, mix_ref):
    h = pl.program_id(0)

    @pl.when(h == 0)
    def _():
        a16 = _rms(x_ref[...], gm_ref[...]).astype(BF16)
        vp_new = _dot(a16, win_ref[:, 0:POOL_WIDTH])
        z_ref[...] = _dot(a16, win_ref[:, POOL_WIDTH:POOL_WIDTH + SSD_WIDTH])
        xbc = _dot(a16, win_ref[:, POOL_WIDTH + SSD_WIDTH:DT_OFF])
        dt_raw = _dot(a16, win_ref[:, DT_OFF:IN_COLS])
        conv = cb_ref[...] + sc_ref[0] * cw_ref[0:1, :]
        conv = conv + sc_ref[1] * cw_ref[1:2, :]
        conv = conv + sc_ref[2] * cw_ref[2:3, :]
        conv = conv + xbc * cw_ref[3:4, :]
        conv = _silu(conv)
        co_ref[0] = sc_ref[1]
        co_ref[1] = sc_ref[2]
        co_ref[2] = xbc
        xs = conv[:, 0:SSD_WIDTH]
        xs_keep[...] = xs
        b_t[...] = conv[:, SSD_WIDTH:SSD_WIDTH + SSD_BC].T
        c_t[...] = conv[:, SSD_WIDTH + SSD_BC:CONV_DIM].T
        dt = _softplus(dt_raw + dtb_ref[...])
        d_a = dt * (-jnp.exp(alog_ref[...]))
        dt_t = dt.T
        dec_t[...] = jnp.exp(d_a).T
        xs_t = xs.T
        for k in range(SSD_HEADS):
            blk = slice(k * SSD_HEAD_DIM, (k + 1) * SSD_HEAD_DIM)
            xdt_t[blk, :] = xs_t[blk, :] * dt_t[k:k + 1, :]
        v = vp_new
        for k in range(POOL_BUF - 1):
            po_ref[k] = sp_ref[k + 1]
        po_ref[POOL_BUF - 1] = v
        for g, w in enumerate(POOL_WINDOWS):
            blk = slice(g * POOL_GROUP_DIM, (g + 1) * POOL_GROUP_DIM)
            acc = sp_ref[POOL_BUF - (w - 1), :, blk]
            for k in range(w - 2, 0, -1):
                acc = acc + sp_ref[POOL_BUF - k, :, blk]
            acc = acc + v[:, blk]
            m = acc / float(min(PAST_LEN + 1, w)) - v[:, blk]
            yg = _dot(m.astype(BF16), plw_ref[g]) + plb_ref[:, blk]
            mix_ref[:, blk] = (yg * psc_ref[:, blk]).astype(BF16)

    g_off = pl.multiple_of((h // (SSD_HEADS // SSD_GROUPS)) * SSD_STATE, SSD_STATE)
    h_off = pl.multiple_of(h * SSD_HEAD_DIM, SSD_HEAD_DIM)
    b_g = b_t[pl.ds(g_off, SSD_STATE), :]
    c_g = c_t[pl.ds(g_off, SSD_STATE), :]
    dec = dec_t[pl.ds(h, 1), :]
    xdt = xdt_t[pl.ds(h_off, SSD_HEAD_DIM), :]
    y_rows = []
    for p in range(SSD_HEAD_DIM):
        s_new = st_ref[0, p] * dec + xdt[p:p + 1, :] * b_g
        so_ref[0, p] = s_new
        y_rows.append(jnp.sum(s_new * c_g, axis=0, keepdims=True))
    y_t[pl.ds(h_off, SSD_HEAD_DIM), :] = jnp.concatenate(y_rows, axis=0)

    @pl.when(h == pl.num_programs(0) - 1)
    def _():
        xs = xs_keep[...]
        y = y_t[...].T + dsk_ref[...] * xs
        y = y * _silu(z_ref[...])
        width = SSD_WIDTH // SSD_GROUPS
        for g in range(SSD_GROUPS):
            blk = slice(g * width, (g + 1) * width)
            yg = y[:, blk]
            rs = lax.rsqrt(jnp.mean(yg * yg, axis=-1, keepdims=True) + EPS)
            mix_ref[:, POOL_WIDTH + g * width:POOL_WIDTH + (g + 1) * width] = (yg * rs * nw_ref[:, blk]).astype(BF16)
        _out_proj_and_route(x_ref[...], mix_ref[...], wout_ref, gf_ref, rw_ref, rb_ref,
                            h_ref, u_ref, rt_ref, wt_ref, cnt_ref)


def _front_step(x, g_mix, w_in16, pool_t, conv_t, ssm_t, cw, cb, dtb, alog, dsk, nw, plw16, plb, psc,
                w_out16, g_ffn, r_w16, r_bias):
    n = x.shape[0]
    fixed2 = lambda h: (0, 0)
    fixed3 = lambda h: (0, 0, 0)
    st_spec = pl.BlockSpec((1, SSD_HEAD_DIM, SSD_STATE, n), lambda h: (h, 0, 0, 0))
    return pl.pallas_call(
        _mix_step_body,
        grid=(SSD_HEADS,),
        in_specs=[pl.BlockSpec((n, D_MODEL), fixed2), pl.BlockSpec((1, D_MODEL), fixed2),
                  pl.BlockSpec((D_MODEL, IN_COLS), fixed2),
                  pl.BlockSpec((POOL_BUF, n, POOL_WIDTH), fixed3),
                  pl.BlockSpec((SSD_CONV - 1, n, CONV_DIM), fixed3),
                  st_spec,
                  pl.BlockSpec((SSD_CONV, CONV_DIM), fixed2), pl.BlockSpec((1, CONV_DIM), fixed2),
                  pl.BlockSpec((1, LANES), fixed2), pl.BlockSpec((1, LANES), fixed2),
                  pl.BlockSpec((1, SSD_WIDTH), fixed2), pl.BlockSpec((1, SSD_WIDTH), fixed2),
                  pl.BlockSpec((len(POOL_WINDOWS), POOL_GROUP_DIM, POOL_GROUP_DIM), fixed3),
                  pl.BlockSpec((1, POOL_WIDTH), fixed2), pl.BlockSpec((1, POOL_WIDTH), fixed2),
                  pl.BlockSpec((D_MODEL, D_MODEL), fixed2), pl.BlockSpec((1, D_MODEL), fixed2),
                  pl.BlockSpec((D_MODEL, 2 * LANES), fixed2),
                  pl.BlockSpec((LANES, n), fixed2)],
        out_specs=[pl.BlockSpec((n, D_MODEL), fixed2), pl.BlockSpec((n, D_MODEL), fixed2),
                   pl.BlockSpec((1, 8, n), fixed3), pl.BlockSpec((n, 2 * LANES), fixed2),
                   pl.BlockSpec((1, N_EXPERTS, LANES), fixed3),
                   pl.BlockSpec((POOL_BUF, n, POOL_WIDTH), fixed3),
                   pl.BlockSpec((SSD_CONV - 1, n, CONV_DIM), fixed3), st_spec],
        out_shape=[jax.ShapeDtypeStruct((n, D_MODEL), F32), jax.ShapeDtypeStruct((n, D_MODEL), BF16),
                   jax.ShapeDtypeStruct((1, 8, n), F32), jax.ShapeDtypeStruct((n, 2 * LANES), F32),
                   jax.ShapeDtypeStruct((1, N_EXPERTS, LANES), F32),
                   jax.ShapeDtypeStruct(pool_t.shape, F32),
                   jax.ShapeDtypeStruct(conv_t.shape, F32), jax.ShapeDtypeStruct(ssm_t.shape, F32)],
        scratch_shapes=[pltpu.VMEM((SSD_WIDTH, n), F32), pltpu.VMEM((LANES, n), F32),
                        pltpu.VMEM((SSD_BC, n), F32), pltpu.VMEM((SSD_BC, n), F32),
                        pltpu.VMEM((n, SSD_WIDTH), F32), pltpu.VMEM((SSD_WIDTH, n), F32),
                        pltpu.VMEM((n, SSD_WIDTH), F32), pltpu.VMEM((n, D_MODEL), BF16)],
        compiler_params=_cparams("arbitrary"),
        name="front_step",
    )(x, g_mix, w_in16, pool_t, conv_t, ssm_t, cw, cb, dtb, alog, dsk, nw, plw16, plb, psc,
      w_out16, g_ffn, r_w16, r_bias)


PROMPT_TILE = MIX_ROWS


def _sort_tables(counts, tile_tokens, data_rows):
    cnt = jnp.concatenate(counts, axis=0)
    pc = (cnt + RUN_PAD - 1) // RUN_PAD * RUN_PAD
    off_local = jnp.cumsum(pc, axis=1) - pc
    tile_rows = jnp.sum(pc, axis=1)
    region = (jnp.sum(pc, axis=0) + MOE_TILE - 1) // MOE_TILE * MOE_TILE
    base = jnp.cumsum(region) - region
    dst = base[None, :] + jnp.cumsum(pc, axis=0) - pc
    per_pass = []
    lo = 0
    for c, tm in zip(counts, tile_tokens):
        hi = lo + c.shape[0]
        n_chunk = _sorted_rows_per_tile(tm) // RUN_PAD
        s = jnp.arange(n_chunk, dtype=jnp.int32) * RUN_PAD
        begins = off_local[lo:hi, None, :]
        ends = begins + pc[lo:hi, None, :]
        inside = (begins <= s[None, :, None]) & (s[None, :, None] < ends)
        shift = jnp.sum(jnp.where(inside, dst[lo:hi, None, :] - begins, 0), axis=2)
        valid = s[None, :] < tile_rows[lo:hi, None]
        parity = (jnp.arange(hi - lo, dtype=jnp.int32) % 2)[:, None]
        spare = data_rows + parity * _sorted_rows_per_tile(max(tile_tokens)) + s[None, :]
        scatter_dst = jnp.where(valid, s[None, :] + shift, spare).astype(jnp.int32)
        gather_src = jnp.where(valid, s[None, :] + shift, 0).astype(jnp.int32)
        off_v = jnp.broadcast_to(off_local[lo:hi, :, None].astype(F32), (hi - lo, N_EXPERTS, LANES))
        per_pass.append((scatter_dst, gather_src, off_v))
        lo = hi
    tiles_cum = jnp.cumsum(region // MOE_TILE)
    n_active = tiles_cum[-1]
    return per_pass, tiles_cum, n_active


def _max_sorted_rows(tile_counts, tile_tokens):
    rows = sum(n * (2 * tm + N_EXPERTS * (RUN_PAD - 1)) for n, tm in zip(tile_counts, tile_tokens))
    data_rows = (-(-rows // MOE_TILE) + N_EXPERTS) * MOE_TILE
    return data_rows, data_rows + 2 * _sorted_rows_per_tile(max(tile_tokens))


def _row(v):
    return v.reshape(1, -1).astype(F32)


def _per_head_rows(v):
    return jnp.broadcast_to(v.astype(F32)[:, None], (SSD_HEADS, SSD_Q))


def _pad_lanes(v):
    return jnp.pad(v.reshape(1, -1).astype(F32), ((0, 0), (0, LANES - v.size)))


def kernel(x_prompt, x_sample, p_prompt, p_sample, state_pool, state_conv, state_ssm, norm_mix, w_in, pool_lin_w, pool_lin_b, pool_scale, conv_w, conv_b, dt_bias, a_log, d_skip, ssd_norm, w_out, norm_ffn, router_grp_w, router_grp_b, router_exp_w, router_exp_b, exp_w_gate, exp_w_up, exp_w_down, norm_ple, ple_gate_w, ple_gate_b, ple_proj_w, norm_final):
    nb, seq, _ = x_prompt.shape
    ns = x_sample.shape[0]
    assert ns == LANES and x_sample.shape[1] == 1 and seq % MIX_ROWS == 0 and seq >= POOL_BUF

    w_in16 = jnp.concatenate([w_in[0].astype(BF16), jnp.zeros((D_MODEL, IN_COLS - w_in.shape[2]), BF16)], axis=1)
    w_out16 = w_out[0].astype(BF16)
    g_mix, g_ffn, g_ple, g_fin = _row(norm_mix[0]), _row(norm_ffn[0]), _row(norm_ple[0]), _row(norm_final)
    cw, cb = conv_w[0].astype(F32), _row(conv_b[0])
    dtb, alog = _pad_lanes(dt_bias[0]), _pad_lanes(a_log[0])
    dsk = _row(jnp.repeat(d_skip[0], SSD_HEAD_DIM))
    nw = _row(ssd_norm[0])
    plw16 = pool_lin_w[0].astype(BF16)
    plb, psc = _row(pool_lin_b[0]), _row(pool_scale[0])
    zeros4 = jnp.zeros((D_MODEL, 8 - N_EXPERT_GROUPS), F32)
    r_w = jnp.concatenate([router_grp_w[0], zeros4, router_exp_w[0],
                           jnp.zeros((D_MODEL, LANES - 8 - N_EXPERTS), F32)], axis=1)
    r_hi = r_w.astype(BF16)
    r_w16 = jnp.concatenate([r_hi, (r_w - r_hi.astype(F32)).astype(BF16)], axis=1)
    r_b = jnp.concatenate([router_grp_b[0], jnp.zeros((8 - N_EXPERT_GROUPS,), F32), router_exp_b[0],
                           jnp.zeros((LANES - 8 - N_EXPERTS,), F32)])
    wg = exp_w_gate[0].reshape(N_EXPERTS, D_MODEL, EXPERT_FF)
    wu = exp_w_up[0].reshape(N_EXPERTS, D_MODEL, EXPERT_FF)
    wd = exp_w_down[0].reshape(N_EXPERTS, EXPERT_FF, D_MODEL)
    pg16 = ple_gate_w[0].astype(BF16)
    pgb = _row(ple_gate_b[0])
    pp16 = ple_proj_w[0].astype(BF16)

    xp = x_prompt.reshape(nb * seq, D_MODEL)
    h1_p, u_p, rt_p, gate_p, cnt_p, st, pool_tail, conv_tail = _front_prompt(
        xp, g_mix, w_in16, cw, cb, _per_head_rows(dt_bias[0]), _per_head_rows(a_log[0]), dsk, nw, plw16, plb, psc,
        w_out16, g_ffn, r_w16, jnp.broadcast_to(r_b[:, None], (LANES, MIX_ROWS)), nb, seq)
    pool_p = pool_tail[:, TAIL_ROWS - POOL_BUF:]
    conv_p = conv_tail[:, CONV_HDR - (SSD_CONV - 1):]
    ssm_p = st

    xs_ = x_sample.reshape(ns, D_MODEL)
    h1_s, u_s, rt_s, gate_s, cnt_s, pool_t, conv_t, ssm_t = _front_step(
        xs_, g_mix, w_in16, jnp.transpose(state_pool[0], (1, 0, 2)), jnp.transpose(state_conv[0], (1, 0, 2)),
        jnp.transpose(state_ssm[0], (1, 2, 3, 0)), cw, cb, dtb, alog, dsk, nw, plw16, plb, psc,
        w_out16, g_ffn, r_w16, jnp.broadcast_to(r_b[:, None], (LANES, ns)))
    pool_s = jnp.transpose(pool_t, (1, 0, 2))
    conv_s = jnp.transpose(conv_t, (1, 0, 2))
    ssm_s = jnp.transpose(ssm_t, (3, 0, 1, 2))

    counts = [cnt_p[:, :, 0].astype(jnp.int32), cnt_s[:, :, 0].astype(jnp.int32)]
    tiles = (PROMPT_TILE, ns)
    data_rows, total_rows = _max_sorted_rows([c.shape[0] for c in counts], tiles)
    (tab_p, tab_s), tiles_cum, n_active = _sort_tables(counts, tiles, data_rows)
    xs_sorted, dest_p = _sort_tokens(tab_p[0], u_p, rt_p, tab_p[2], None, PROMPT_TILE, total_rows)
    xs_sorted, dest_s = _sort_tokens(tab_s[0], u_s, rt_s, tab_s[2], xs_sorted, ns, total_rows)
    tile_start = jnp.concatenate([jnp.zeros((1,), jnp.int32), tiles_cum.astype(jnp.int32)])
    ys_sorted = _moe_sorted(tile_start, n_active.reshape(1).astype(jnp.int32), xs_sorted, wg, wu, wd)

    y_prompt = _ple(tab_p[1], h1_p, dest_p, gate_p, p_prompt[0].reshape(nb * seq, PLE_DIM),
                    g_ple, pg16, pgb, pp16, g_fin, ys_sorted, PROMPT_TILE)
    y_sample = _ple(tab_s[1], h1_s, dest_s, gate_s, p_sample[0].reshape(ns, PLE_DIM),
                    g_ple, pg16, pgb, pp16, g_fin, ys_sorted, ns)

    return (y_prompt.reshape(nb, seq, D_MODEL), y_sample.reshape(ns, 1, D_MODEL),
            pool_p[None], conv_p[None], ssm_p[None], pool_s[None], conv_s[None], ssm_s[None])
```

```python
import functools

import jax
import jax.numpy as jnp
import numpy as np
from jax import lax
from jax.experimental import pallas as pl
from jax.experimental.pallas import tpu as pltpu

F32 = jnp.float32
BF16 = jnp.bfloat16

D_MODEL = 1024
POOL_WIDTH = 512
POOL_WINDOWS = (2, 4, 8, 16)
POOL_GROUP_DIM = 128
POOL_BUF = 15
SSD_WIDTH = 512
SSD_HEAD_DIM = 64
SSD_HEADS = 8
SSD_GROUPS = 2
SSD_STATE = 64
SSD_CONV = 4
SSD_BC = SSD_GROUPS * SSD_STATE
CONV_DIM = SSD_WIDTH + 2 * SSD_BC
N_EXPERT_GROUPS = 4
EXPERTS_PER_GROUP = 8
N_EXPERTS = N_EXPERT_GROUPS * EXPERTS_PER_GROUP
EXPERT_FF = 256
PLE_DIM = 256
PAST_LEN = 16384
EPS = 1e-6

LANES = 128
DT_OFF = POOL_WIDTH + SSD_WIDTH + CONV_DIM
IN_DIM = DT_OFF + SSD_HEADS
VMEM_LIMIT = 56 * 1024 * 1024


def _cparams(*sem):
    return pltpu.CompilerParams(dimension_semantics=sem, vmem_limit_bytes=VMEM_LIMIT)


def _rms(x, g):
    return x * lax.rsqrt(jnp.mean(x * x, axis=-1, keepdims=True) + EPS) * g


def _sigmoid(x):
    return 1.0 / (1.0 + jnp.exp(-x))


def _silu(x):
    return x * _sigmoid(x)


def _split3(v):
    hi = v.astype(BF16)
    r = v - hi.astype(F32)
    mid = r.astype(BF16)
    lo = (r - mid.astype(F32)).astype(BF16)
    return hi, mid, lo


def _dot(a, b):
    return jnp.dot(a, b, preferred_element_type=F32)


def _dot_nt(a, b):
    return lax.dot_general(a, b, (((1,), (1,)), ((), ())), preferred_element_type=F32)


def _dot_tn(a, b):
    return lax.dot_general(a, b, (((0,), (0,)), ((), ())), preferred_element_type=F32)


def _route(lg):
    tm = lg.shape[1]
    gl = lg[0:N_EXPERT_GROUPS, :]
    gmax = jnp.max(gl, axis=0, keepdims=True)
    gsum = jnp.sum(jnp.exp(gl - gmax), axis=0, keepdims=True)
    g_w = 1.0 / gsum
    gi = lax.broadcasted_iota(jnp.int32, gl.shape, 0)
    g_idx = jnp.min(jnp.where(gl == gmax, gi, N_EXPERT_GROUPS), axis=0, keepdims=True)
    sel = jnp.zeros((EXPERTS_PER_GROUP, tm), F32)
    for g in range(N_EXPERT_GROUPS):
        blk = lg[8 + g * EXPERTS_PER_GROUP:8 + (g + 1) * EXPERTS_PER_GROUP, :]
        sel = jnp.where(g_idx == g, blk, sel)
    ei = lax.broadcasted_iota(jnp.int32, sel.shape, 0)
    m1 = jnp.max(sel, axis=0, keepdims=True)
    i1 = jnp.min(jnp.where(sel == m1, ei, EXPERTS_PER_GROUP), axis=0, keepdims=True)
    rest = jnp.where(ei == i1, -jnp.inf, sel)
    m2 = jnp.max(rest, axis=0, keepdims=True)
    i2 = jnp.min(jnp.where(rest == m2, ei, EXPERTS_PER_GROUP), axis=0, keepdims=True)
    p2 = jnp.exp(m2 - m1)
    w1 = g_w / (1.0 + p2)
    w2 = g_w * p2 / (1.0 + p2)
    return g_idx * EXPERTS_PER_GROUP + i1, g_idx * EXPERTS_PER_GROUP + i2, w1, w2


def _out_proj_and_route(x, mix16, w_ref, g_ref, rw_ref, rb_ref, h_ref, u_ref, rt_ref, wt_ref, cnt_ref,
                        r0=0, accumulate=False):
    tm = x.shape[0]
    h = x + _dot(mix16, w_ref[...])
    h_ref[r0:r0 + tm, :] = h
    _route_rows(h, g_ref, rw_ref, rb_ref, u_ref, rt_ref, wt_ref, cnt_ref, r0=r0, accumulate=accumulate)


def _route_rows(h, g_ref, rw_ref, rb_ref, u_ref, rt_ref, wt_ref, cnt_ref, r0=0, accumulate=False):
    tm = h.shape[0]
    u = _rms(h, g_ref[...])
    u_hi = u.astype(BF16)
    u_ref[r0:r0 + tm, :] = u_hi
    u_lo = (u - u_hi.astype(F32)).astype(BF16)
    both = _dot(u_hi, rw_ref[...])
    lg_rows = both[:, 0:LANES] + both[:, LANES:2 * LANES] + _dot(u_lo, rw_ref[:, 0:LANES])
    lg = lg_rows.T + rb_ref[:, r0:r0 + tm]
    b1, b2, w1, w2 = _route(lg)
    r8 = lax.broadcasted_iota(jnp.int32, (8, tm), 0)
    rt_ref[0, :, r0:r0 + tm] = jnp.where(r8 == 0, b1.astype(F32), jnp.where(r8 == 1, b2.astype(F32), 0.0))
    wt_ref[r0:r0 + tm, :] = jnp.concatenate(
        [jnp.broadcast_to(w1, (LANES, tm)).T, jnp.broadcast_to(w2, (LANES, tm)).T], axis=1)
    kio = lax.broadcasted_iota(jnp.int32, (N_EXPERTS, tm), 0)
    hits = ((kio == b1) | (kio == b2)).astype(F32)
    cnt = jnp.broadcast_to(jnp.sum(hits, axis=1, keepdims=True), (N_EXPERTS, LANES))
    cnt_ref[0] = cnt_ref[0] + cnt if accumulate else cnt


RUN_PAD = 16
MOE_TILE = 256
MOE_GROUP = 3
MOE_AHEAD = 8
MOE_BUFS = MOE_AHEAD + MOE_GROUP + 1


def _tile_lanes(v, width):
    reps = width // LANES
    return v if reps == 1 else jnp.concatenate([v] * reps, axis=1)


def _sorted_rows_per_tile(tm):
    need = 2 * tm + N_EXPERTS * (RUN_PAD - 1)
    return -(-need // LANES) * LANES


def _sort_body(cd_ref, u_ref, rt_ref, off_ref, before_ref, *rest):
    xs_ref, dt_ref, loc, sem = rest[-4:]
    i = pl.program_id(0)
    tm = u_ref.shape[0]
    n_chunk = loc.shape[1]
    rows = n_chunk * RUN_PAD
    rt = rt_ref[0]
    b1 = rt[0:1, :].astype(jnp.int32)
    b2 = rt[1:2, :].astype(jnp.int32)
    kio = lax.broadcasted_iota(jnp.int32, (N_EXPERTS, tm), 0)
    o1 = kio == b1
    o2 = kio == b2
    start = _dot((o1 | o2).astype(BF16), before_ref[...]) + _tile_lanes(off_ref[0], tm)
    d1 = jnp.sum(jnp.where(o1, start, 0.0), axis=0, keepdims=True)
    d2 = jnp.sum(jnp.where(o2, start, 0.0), axis=0, keepdims=True)
    dt_ref[...] = jnp.concatenate([jnp.broadcast_to(d1, (LANES, tm)).T, jnp.broadcast_to(d2, (LANES, tm)).T], axis=1)
    rio = lax.broadcasted_iota(jnp.int32, (rows, tm), 0)
    perm = ((rio == d1.astype(jnp.int32)) | (rio == d2.astype(jnp.int32))).astype(BF16)
    slot = i % 2
    loc[slot] = _dot(perm, u_ref[...]).astype(BF16).reshape(n_chunk, RUN_PAD, D_MODEL)

    def chunk_copy(step, sl, c):
        return pltpu.make_async_copy(loc.at[sl, c], xs_ref.at[cd_ref[step, c]], sem.at[sl])

    for c in range(n_chunk):
        chunk_copy(i, slot, c).start()

    @pl.when(i > 0)
    def _():
        for c in range(n_chunk):
            chunk_copy(i - 1, 1 - slot, c).wait()

    @pl.when(i == pl.num_programs(0) - 1)
    def _():
        for c in range(n_chunk):
            chunk_copy(i, slot, c).wait()


def _sort_tokens(chunk_dst, u16, rt, off_v, xs_prev, tm, total_rows):
    t = u16.shape[0]
    rows = _sorted_rows_per_tile(tm)
    in_specs = [pl.BlockSpec((tm, D_MODEL), lambda i, cd: (i, 0)),
                pl.BlockSpec((1, 8, tm), lambda i, cd: (i, 0, 0)),
                pl.BlockSpec((1, N_EXPERTS, LANES), lambda i, cd: (i, 0, 0)),
                pl.BlockSpec((tm, tm), lambda i, cd: (0, 0))]
    before = jnp.asarray(np.triu(np.ones((tm, tm), np.float32), 1)).astype(BF16)
    args = [chunk_dst, u16, rt, off_v, before]
    aliases = {}
    if xs_prev is not None:
        in_specs.append(pl.BlockSpec(memory_space=pl.ANY))
        aliases = {len(args): 0}
        args.append(xs_prev)
    return pl.pallas_call(
        _sort_body,
        grid_spec=pltpu.PrefetchScalarGridSpec(
            num_scalar_prefetch=1,
            grid=(t // tm,),
            in_specs=in_specs,
            out_specs=[pl.BlockSpec(memory_space=pl.ANY),
                       pl.BlockSpec((tm, 2 * LANES), lambda i, cd: (i, 0))],
            scratch_shapes=[pltpu.VMEM((2, rows // RUN_PAD, RUN_PAD, D_MODEL), BF16),
                            pltpu.SemaphoreType.DMA((2,))]),
        out_shape=[jax.ShapeDtypeStruct((total_rows // RUN_PAD, RUN_PAD, D_MODEL), BF16),
                   jax.ShapeDtypeStruct((t, 2 * LANES), F32)],
        input_output_aliases=aliases,
        compiler_params=_cparams("arbitrary"),
        name="sort_tokens",
    )(*args)


def _moe_body(ts_ref, na_ref, xs_ref, wg_ref, wu_ref, wd_ref, ys_ref, xbuf, ybuf, wg16, wu16, wd16, sem_in, sem_out):
    k = pl.program_id(0)
    n_act = na_ref[0]

    def x_copy(g, sl):
        return pltpu.make_async_copy(xs_ref.at[pl.ds(pl.multiple_of(g * MOE_TILE, MOE_TILE), MOE_TILE), :],
                                     xbuf.at[sl], sem_in.at[sl])

    def y_copy(g, sl):
        return pltpu.make_async_copy(ybuf.at[sl],
                                     ys_ref.at[pl.ds(pl.multiple_of(g * MOE_TILE, MOE_TILE), MOE_TILE), :],
                                     sem_out.at[sl])

    @pl.when(k == 0)
    def _():
        for j in range(MOE_AHEAD):
            @pl.when(j < n_act)
            def _():
                x_copy(j, j).start(priority=1)

    wg16[...] = wg_ref[0].astype(BF16)
    wu16[...] = wu_ref[0].astype(BF16)
    wd16[...] = wd_ref[0].astype(BF16)

    def arrive(g):
        sl = g % MOE_BUFS

        @pl.when(g + MOE_AHEAD < n_act)
        def _():
            x_copy(g + MOE_AHEAD, (g + MOE_AHEAD) % MOE_BUFS).start(priority=1)

        x_copy(g, sl).wait()

        @pl.when(g >= MOE_BUFS)
        def _():
            y_copy(g - MOE_BUFS, sl).wait()

    def tiles(g, count):
        for j in range(count):
            arrive(g + j)
        acts = []
        for j in range(count):
            x = xbuf[(g + j) % MOE_BUFS]
            acts.append((_silu(_dot(x, wg16[...])) * _dot(x, wu16[...])).astype(BF16))
        for j in range(count):
            ybuf[(g + j) % MOE_BUFS] = _dot(acts[j], wd16[...]).astype(BF16)
        for j in range(count):
            y_copy(g + j, (g + j) % MOE_BUFS).start()

    g0 = ts_ref[k]
    n_here = ts_ref[k + 1] - g0

    def group(p, carry):
        tiles(g0 + MOE_GROUP * p, MOE_GROUP)
        return carry

    n_groups = n_here // MOE_GROUP
    lax.fori_loop(0, n_groups, group, 0)
    for rest in range(1, MOE_GROUP):
        @pl.when(n_here - n_groups * MOE_GROUP == rest)
        def _():
            tiles(g0 + n_groups * MOE_GROUP, rest)

    @pl.when(k == pl.num_programs(0) - 1)
    def _():
        for j in range(1, MOE_BUFS + 1):
            @pl.when(n_act >= j)
            def _():
                y_copy(n_act - j, (n_act - j) % MOE_BUFS).wait()


def _moe_sorted(tile_start, n_active, xs, wg, wu, wd):
    w_map = lambda k, ts, na: (k, 0, 0)
    return pl.pallas_call(
        _moe_body,
        grid_spec=pltpu.PrefetchScalarGridSpec(
            num_scalar_prefetch=2,
            grid=(N_EXPERTS,),
            in_specs=[pl.BlockSpec(memory_space=pl.ANY),
                      pl.BlockSpec((1, D_MODEL, EXPERT_FF), w_map),
                      pl.BlockSpec((1, D_MODEL, EXPERT_FF), w_map),
                      pl.BlockSpec((1, EXPERT_FF, D_MODEL), w_map)],
            out_specs=pl.BlockSpec(memory_space=pl.ANY),
            scratch_shapes=[pltpu.VMEM((MOE_BUFS, MOE_TILE, D_MODEL), BF16),
                            pltpu.VMEM((MOE_BUFS, MOE_TILE, D_MODEL), BF16),
                            pltpu.VMEM((D_MODEL, EXPERT_FF), BF16), pltpu.VMEM((D_MODEL, EXPERT_FF), BF16),
                            pltpu.VMEM((EXPERT_FF, D_MODEL), BF16),
                            pltpu.SemaphoreType.DMA((MOE_BUFS,)), pltpu.SemaphoreType.DMA((MOE_BUFS,))]),
        out_shape=jax.ShapeDtypeStruct(xs.shape, BF16),
        compiler_params=_cparams("arbitrary"),
        name="moe_sorted",
    )(tile_start, n_active, xs, wg, wu, wd)


def _ple_body(cd_ref, h_ref, dt_ref, wt_ref, p_ref, gn_ref, wg_ref, bg_ref, wp_ref, fn_ref, ys_ref,
              o_ref, loc, sem):
    i = pl.program_id(0)
    tm = h_ref.shape[0]
    n_chunk = loc.shape[1]
    rows = n_chunk * RUN_PAD
    slot = i % 2

    def chunk_copy(step, sl, c):
        return pltpu.make_async_copy(ys_ref.at[cd_ref[step, c]], loc.at[sl, c], sem.at[sl])

    def fetch(step, sl):
        for c in range(n_chunk):
            chunk_copy(step, sl, c).start()

    @pl.when(i == 0)
    def _():
        fetch(0, 0)

    @pl.when(i + 1 < pl.num_programs(0))
    def _():
        fetch(i + 1, 1 - slot)

    for c in range(n_chunk):
        chunk_copy(i, slot, c).wait()

    ci = lax.broadcasted_iota(jnp.int32, (tm, LANES), 1)
    d1 = dt_ref[:, 0:LANES].astype(jnp.int32)
    d2 = dt_ref[:, LANES:2 * LANES].astype(jnp.int32)
    w1 = wt_ref[:, 0:LANES]
    w2 = wt_ref[:, LANES:2 * LANES]
    sel = jnp.concatenate(
        [jnp.where(d1 == ci + m * LANES, w1, jnp.where(d2 == ci + m * LANES, w2, 0.0)) for m in range(rows // LANES)],
        axis=1).astype(BF16)
    h = h_ref[...] + _dot(sel, loc[slot].reshape(rows, D_MODEL))
    a16 = _rms(h, gn_ref[...]).astype(BF16)
    gate = _sigmoid(_dot(a16, wg_ref[...]) + bg_ref[...])
    pp = _dot(p_ref[...].astype(BF16), wp_ref[...])
    h = h + gate * pp
    o_ref[...] = _rms(h, fn_ref[...])


def _ple(chunk_src, h1, dest_t, gate_t, p, g_ple, wg16, bg, wp16, g_final, ys, tm):
    t = h1.shape[0]
    row = lambda i, cd: (i, 0)
    fixed = lambda i, cd: (0, 0)
    return pl.pallas_call(
        _ple_body,
        grid_spec=pltpu.PrefetchScalarGridSpec(
            num_scalar_prefetch=1,
            grid=(t // tm,),
            in_specs=[pl.BlockSpec((tm, D_MODEL), row), pl.BlockSpec((tm, 2 * LANES), row),
                      pl.BlockSpec((tm, 2 * LANES), row), pl.BlockSpec((tm, PLE_DIM), row),
                      pl.BlockSpec((1, D_MODEL), fixed), pl.BlockSpec((D_MODEL, D_MODEL), fixed),
                      pl.BlockSpec((1, D_MODEL), fixed), pl.BlockSpec((PLE_DIM, D_MODEL), fixed),
                      pl.BlockSpec((1, D_MODEL), fixed), pl.BlockSpec(memory_space=pl.ANY)],
            out_specs=pl.BlockSpec((tm, D_MODEL), row),
            scratch_shapes=[pltpu.VMEM((2, _sorted_rows_per_tile(tm) // RUN_PAD, RUN_PAD, D_MODEL), BF16),
                            pltpu.SemaphoreType.DMA((2,))]),
        out_shape=jax.ShapeDtypeStruct((t, D_MODEL), F32),
        compiler_params=_cparams("arbitrary"),
        name="ple_final",
    )(chunk_src, h1, dest_t, gate_t, p, g_ple, wg16, bg, wp16, g_final, ys)


MIX_ROWS = 512
SSD_Q = 128
POOL_HDR = 128
CONV_HDR = 8
TAIL_ROWS = 16


def _softplus(x):
    return jnp.maximum(x, 0.0) + jnp.log1p(jnp.exp(-jnp.abs(x)))


def _stack_terms(v, terms):
    parts = [t.astype(F32) for t in _split3(v)[:terms]]
    parts.append(jnp.zeros(((4 - terms) * v.shape[0], v.shape[1]), F32))
    return jnp.concatenate(parts, axis=0).astype(BF16)


def _cumsum_lanes(v, tri16):
    r = _dot_nt(_stack_terms(v, 3), tri16)
    n = v.shape[0]
    return r[0:n] + r[n:2 * n] + r[2 * n:3 * n]


def _expand_heads(v, expand32, terms):
    return _dot_tn(_stack_terms(v, terms), expand32)


def _mixer_constants():
    q = SSD_Q
    li = np.arange(q)[:, None]
    tri = (li >= np.arange(q)[None, :])
    expand = (np.arange(SSD_WIDTH)[None, :] // SSD_HEAD_DIM) == (np.arange(4 * SSD_HEADS)[:, None] % SSD_HEADS)
    hpg = SSD_HEADS // SSD_GROUPS
    block = (np.arange(SSD_BC)[:, None] // SSD_STATE) == (np.arange(SSD_WIDTH)[None, :] // (SSD_HEAD_DIM * hpg))
    wj = np.arange(2 * q)[None, :]
    band = np.stack([(wj <= li + POOL_HDR) & (wj > li + POOL_HDR - w) for w in POOL_WINDOWS])
    inv_count = np.stack([np.broadcast_to(1.0 / np.minimum(li + 1, w), (q, LANES)) for w in POOL_WINDOWS])
    as_f32 = lambda m: jnp.asarray(m.astype(np.float32))
    return (as_f32(tri), as_f32(tri).astype(BF16), as_f32(expand).astype(BF16), as_f32(block),
            as_f32(band).astype(BF16), as_f32(inv_count))


def _front_body(steps_per_seq, x_ref, xp_ref, gm_ref, win_ref, wdt_ref, trif_ref, tri_ref, exp_ref, bmask_ref, band_ref,
                invc_ref, cw_ref, cb_ref, dtb_ref, alog_ref, dsk_ref, nw_ref,
                plw_ref, plb_ref, psc_ref, wout_ref, gf_ref, rw_ref, rb_ref,
                h_ref, u_ref, rt_ref, wt_ref, cnt_ref, st_ref, ptail_ref, ctail_ref,
                pool_ext, conv_ext, s_ref, z_ref, dt_ref, mix_ref, mixp_ref, a_ref):
    s = pl.program_id(0)
    n_tiles = pl.num_programs(0) - 1
    real = s < n_tiles
    c = jnp.minimum(s, n_tiles - 1) % steps_per_seq
    rows = MIX_ROWS
    q_len = SSD_Q

    @pl.when(s == 0)
    def _():
        mixp_ref[...] = jnp.zeros_like(mixp_ref)

    @pl.when(c == 0)
    def _():
        pool_ext[0:POOL_HDR, :] = jnp.zeros((POOL_HDR, POOL_WIDTH), F32)
        conv_ext[0:CONV_HDR, :] = jnp.zeros((CONV_HDR, CONV_DIM), F32)
        s_ref[...] = jnp.zeros_like(s_ref)

    @pl.when(c > 0)
    def _():
        pool_ext[0:POOL_HDR, :] = pool_ext[rows:rows + POOL_HDR, :]
        conv_ext[0:CONV_HDR, :] = conv_ext[rows:rows + CONV_HDR, :]

    causal = trif_ref[...] > 0.5
    tri = tri_ref[...]
    expand = exp_ref[...]
    blockmask = bmask_ref[...]
    lane = lax.broadcasted_iota(jnp.int32, (q_len, LANES), 1)
    left = lane < SSD_HEAD_DIM
    a_neg = -jnp.exp(alog_ref[...])

    half = rows // 2
    n_cols = 256

    def norm_rows():
        for r0 in (0, half):
            a_ref[r0:r0 + half, :] = _rms(x_ref[r0:r0 + half, :], gm_ref[...]).astype(BF16)

    def in_proj_chunk(ref, row, k, col0):
        def run():
            ref[row:row + rows, k:k + n_cols] = _dot(a_ref[...], win_ref[:, col0 + k:col0 + k + n_cols])
        return run

    def dt_chunk():
        dt_ref[...] = _dot(a_ref[...], wdt_ref[...])

    conv_chunks = [in_proj_chunk(conv_ext, CONV_HDR, k, POOL_WIDTH + SSD_WIDTH) for k in range(0, CONV_DIM, n_cols)]
    z_chunks = [in_proj_chunk(z_ref, 0, k, POOL_WIDTH) for k in range(0, SSD_WIDTH, n_cols)]
    pool_chunks = [in_proj_chunk(pool_ext, POOL_HDR, k, 0) for k in range(0, POOL_WIDTH, n_cols)]

    def out_proj_chunk(k):
        def run():
            h_ref[:, k:k + n_cols] = xp_ref[:, k:k + n_cols] + _dot(mixp_ref[...], wout_ref[:, k:k + n_cols])
        return run

    def route(r0):
        def run():
            _route_rows(h_ref[r0:r0 + half, :], gf_ref, rw_ref, rb_ref, u_ref, rt_ref, wt_ref, cnt_ref,
                        r0=r0, accumulate=r0 > 0)
        return run

    out_chunks = [out_proj_chunk(k) for k in range(0, D_MODEL, n_cols)]
    routes = [route(0), route(half)]

    def mix_pair(q_first, fillers):
        fillers = list(fillers)

        def fill():
            if fillers:
                fillers.pop(0)()

        subs = range(q_first, q_first + 2)
        xs_l, b16_l, call_l, c16_l = {}, {}, {}, {}
        for q in subs:
            base = CONV_HDR + q * q_len
            conv = cb_ref[...] + conv_ext[base - 3:base - 3 + q_len, :] * cw_ref[0:1, :]
            conv = conv + conv_ext[base - 2:base - 2 + q_len, :] * cw_ref[1:2, :]
            conv = conv + conv_ext[base - 1:base - 1 + q_len, :] * cw_ref[2:3, :]
            conv = conv + conv_ext[base:base + q_len, :] * cw_ref[3:4, :]
            conv = _silu(conv)
            xs_l[q] = conv[:, 0:SSD_WIDTH]
            b16_l[q] = conv[:, SSD_WIDTH:SSD_WIDTH + SSD_BC].astype(BF16)
            call_l[q] = conv[:, SSD_WIDTH + SSD_BC:CONV_DIM]
            c16_l[q] = call_l[q].astype(BF16)
            fill()
        dtt_l = {q: _softplus(dt_ref[q * q_len:(q + 1) * q_len, :].T[0:SSD_HEADS, :] + dtb_ref[...]) for q in subs}
        acst_l = {q: _cumsum_lanes(dtt_l[q] * a_neg, tri) for q in subs}
        acs_l = {q: jnp.concatenate([acst_l[q], jnp.zeros((LANES - SSD_HEADS, q_len), F32)], axis=0).T
                 for q in subs}
        fill()
        dtx_l = {q: _expand_heads(dtt_l[q], expand, 2) for q in subs}
        acsx_l = {q: _expand_heads(acst_l[q], expand, 3) for q in subs}
        lastx_l = {q: acsx_l[q][q_len - 1:q_len, :] for q in subs}
        xdt_l = {q: xs_l[q] * dtx_l[q] for q in subs}
        xdt16_l = {q: xdt_l[q].astype(BF16) for q in subs}
        fill()
        contrib_l = {q: _dot_tn(b16_l[q], (xdt_l[q] * jnp.exp(lastx_l[q] - acsx_l[q])).astype(BF16)) * blockmask
                     for q in subs}
        cb_l = {q: [_dot_nt(jnp.where(lax.shift_right_logical(lane, 6) == g, call_l[q], 0.0).astype(BF16), b16_l[q])
                    for g in range(SSD_GROUPS)] for q in subs}
        fill()
        zs_l = {}
        for q in subs:
            zs = []
            for h in range(SSD_HEADS):
                blk = slice((h // 2) * LANES, (h // 2 + 1) * LANES)
                seg = acs_l[q][:, h:h + 1] - acst_l[q][h:h + 1, :]
                decay = jnp.where(causal, jnp.exp(seg), 0.0)
                scores = (cb_l[q][h // (SSD_HEADS // SSD_GROUPS)] * decay).astype(BF16)
                zs.append(_dot(scores, xdt16_l[q][:, blk]))
                if h % 4 == 3:
                    fill()
            zs_l[q] = zs
        yoff_l = {}
        for q in subs:
            s_old = s_ref[...]
            yoff_l[q] = _dot(c16_l[q], s_old.astype(BF16)) * jnp.exp(acsx_l[q])
            s_ref[...] = s_old * jnp.exp(lastx_l[q]) + contrib_l[q]
        fill()
        for q in subs:
            r0 = q * q_len
            z = z_ref[r0:r0 + q_len, :]
            y_blocks = []
            for j in range(SSD_HEADS // 2):
                blk = slice(j * LANES, (j + 1) * LANES)
                y = jnp.where(left, zs_l[q][2 * j], zs_l[q][2 * j + 1]) + yoff_l[q][:, blk]
                y = y + dsk_ref[:, blk] * xs_l[q][:, blk]
                y_blocks.append(y * _silu(z[:, blk]))
            for g in range(SSD_GROUPS):
                y0, y1 = y_blocks[2 * g], y_blocks[2 * g + 1]
                ss = jnp.sum(y0 * y0, axis=-1, keepdims=True) + jnp.sum(y1 * y1, axis=-1, keepdims=True)
                rs = lax.rsqrt(ss * (1.0 / (2 * LANES)) + EPS)
                for k, yk in ((2 * g, y0), (2 * g + 1, y1)):
                    blk = slice(k * LANES, (k + 1) * LANES)
                    out = yk * rs * nw_ref[:, blk]
                    mix_ref[r0:r0 + q_len, POOL_WIDTH + k * LANES:POOL_WIDTH + (k + 1) * LANES] = out.astype(BF16)
            fill()
        for q in subs:
            r0 = q * q_len
            for g, w in enumerate(POOL_WINDOWS):
                blk = slice(g * POOL_GROUP_DIM, (g + 1) * POOL_GROUP_DIM)
                pe = pool_ext[r0:r0 + 2 * q_len, blk]
                hi = pe.astype(BF16)
                lo = (pe - hi.astype(F32)).astype(BF16)
                both = _dot(band_ref[g], jnp.concatenate([hi, lo], axis=1))
                winsum = both[:, 0:POOL_GROUP_DIM] + both[:, POOL_GROUP_DIM:2 * POOL_GROUP_DIM]
                inv = jnp.where(c == 0, invc_ref[g], 1.0 / w) if q == 0 else 1.0 / w
                m = winsum * inv - pe[q_len:2 * q_len, :]
                yg = _dot(m.astype(BF16), plw_ref[g]) + plb_ref[:, blk]
                mix_ref[r0:r0 + q_len, blk] = (yg * psc_ref[:, blk]).astype(BF16)
            fill()
        while fillers:
            fill()

    @pl.when(real)
    def _():
        norm_rows()
        for run in conv_chunks + [dt_chunk]:
            run()
        mix_pair(0, z_chunks + pool_chunks + out_chunks[:2])
        mix_pair(2, out_chunks[2:] + routes)
        mixp_ref[...] = mix_ref[...]

    @pl.when(jnp.logical_not(real))
    def _():
        for run in out_chunks + routes:
            run()

    @pl.when(real & (c == steps_per_seq - 1))
    def _():
        hpg = SSD_HEADS // SSD_GROUPS
        for j in range(SSD_HEADS // 2):
            flipped = s_ref[:, j * LANES:(j + 1) * LANES].T
            g = (2 * j) // hpg
            for k in range(2):
                st_ref[0, 2 * j + k] = flipped[k * SSD_HEAD_DIM:(k + 1) * SSD_HEAD_DIM,
                                               g * SSD_STATE:(g + 1) * SSD_STATE]
        ptail_ref[0] = pool_ext[POOL_HDR + rows - TAIL_ROWS:POOL_HDR + rows, :]
        ctail_ref[0] = conv_ext[rows:rows + CONV_HDR, :]


def _front_prompt(x, g_mix, w_in16, w_dt16, cw, cb, dtb, alog, dsk, nw, plw16, plb, psc, w_out16, g_ffn, r_w16, r_bias,
                  nb, seq):
    steps = seq // MIX_ROWS
    nt = nb * steps
    cur = lambda s: (jnp.minimum(s, nt - 1), 0)
    prev = lambda s: (jnp.maximum(s - 1, 0), 0)
    prev3 = lambda s: (jnp.maximum(s - 1, 0), 0, 0)
    per_seq = lambda s: (jnp.minimum(s, nt - 1) // steps, 0, 0)
    fixed2 = lambda s: (0, 0)
    fixed3 = lambda s: (0, 0, 0)
    return pl.pallas_call(
        functools.partial(_front_body, steps),
        grid=(nt + 1,),
        in_specs=[pl.BlockSpec((MIX_ROWS, D_MODEL), cur), pl.BlockSpec((MIX_ROWS, D_MODEL), prev),
                  pl.BlockSpec((1, D_MODEL), fixed2), pl.BlockSpec((D_MODEL, IN_DIM), fixed2),
                  pl.BlockSpec((D_MODEL, LANES), fixed2),
                  pl.BlockSpec((SSD_Q, SSD_Q), fixed2), pl.BlockSpec((SSD_Q, SSD_Q), fixed2),
                  pl.BlockSpec((4 * SSD_HEADS, SSD_WIDTH), fixed2), pl.BlockSpec((SSD_BC, SSD_WIDTH), fixed2),
                  pl.BlockSpec((len(POOL_WINDOWS), SSD_Q, 2 * SSD_Q), fixed3),
                  pl.BlockSpec((len(POOL_WINDOWS), SSD_Q, LANES), fixed3),
                  pl.BlockSpec((SSD_CONV, CONV_DIM), fixed2), pl.BlockSpec((1, CONV_DIM), fixed2),
                  pl.BlockSpec((SSD_HEADS, SSD_Q), fixed2), pl.BlockSpec((SSD_HEADS, SSD_Q), fixed2),
                  pl.BlockSpec((1, SSD_WIDTH), fixed2), pl.BlockSpec((1, SSD_WIDTH), fixed2),
                  pl.BlockSpec((len(POOL_WINDOWS), POOL_GROUP_DIM, POOL_GROUP_DIM), fixed3),
                  pl.BlockSpec((1, POOL_WIDTH), fixed2), pl.BlockSpec((1, POOL_WIDTH), fixed2),
                  pl.BlockSpec((D_MODEL, D_MODEL), fixed2), pl.BlockSpec((1, D_MODEL), fixed2),
                  pl.BlockSpec((D_MODEL, 2 * LANES), fixed2),
                  pl.BlockSpec((LANES, MIX_ROWS), fixed2)],
        out_specs=[pl.BlockSpec((MIX_ROWS, D_MODEL), prev), pl.BlockSpec((MIX_ROWS, D_MODEL), prev),
                   pl.BlockSpec((1, 8, MIX_ROWS), prev3), pl.BlockSpec((MIX_ROWS, 2 * LANES), prev),
                   pl.BlockSpec((1, N_EXPERTS, LANES), prev3),
                   pl.BlockSpec((1, SSD_HEADS, SSD_HEAD_DIM, SSD_STATE),
                                lambda s: (jnp.minimum(s, nt - 1) // steps, 0, 0, 0)),
                   pl.BlockSpec((1, TAIL_ROWS, POOL_WIDTH), per_seq),
                   pl.BlockSpec((1, CONV_HDR, CONV_DIM), per_seq)],
        out_shape=[jax.ShapeDtypeStruct((nb * seq, D_MODEL), F32), jax.ShapeDtypeStruct((nb * seq, D_MODEL), BF16),
                   jax.ShapeDtypeStruct((nt, 8, MIX_ROWS), F32), jax.ShapeDtypeStruct((nb * seq, 2 * LANES), F32),
                   jax.ShapeDtypeStruct((nt, N_EXPERTS, LANES), F32),
                   jax.ShapeDtypeStruct((nb, SSD_HEADS, SSD_HEAD_DIM, SSD_STATE), F32),
                   jax.ShapeDtypeStruct((nb, TAIL_ROWS, POOL_WIDTH), F32),
                   jax.ShapeDtypeStruct((nb, CONV_HDR, CONV_DIM), F32)],
        scratch_shapes=[pltpu.VMEM((POOL_HDR + MIX_ROWS, POOL_WIDTH), F32),
                        pltpu.VMEM((CONV_HDR + MIX_ROWS, CONV_DIM), F32),
                        pltpu.VMEM((SSD_BC, SSD_WIDTH), F32),
                        pltpu.VMEM((MIX_ROWS, SSD_WIDTH), F32), pltpu.VMEM((MIX_ROWS, LANES), F32),
                        pltpu.VMEM((MIX_ROWS, D_MODEL), BF16), pltpu.VMEM((MIX_ROWS, D_MODEL), BF16),
                        pltpu.VMEM((MIX_ROWS, D_MODEL), BF16)],
        compiler_params=_cparams("arbitrary"),
        name="front_prompt",
    )(x, x, g_mix, w_in16, w_dt16, *_mixer_constants(), cw, cb, dtb, alog, dsk, nw, plw16, plb, psc, w_out16, g_ffn,
      r_w16, r_bias)


def _mix_step_body(x_ref, gm_ref, win_ref, wdt_ref, sp_ref, sc_ref, st_ref, cw_ref, cb_ref, dtb_ref, alog_ref,
                   dsk_ref, nw_ref, plw_ref, plb_ref, psc_ref, wout_ref, gf_ref, rw_ref, rb_ref,
                   h_ref, u_ref, rt_ref, wt_ref, cnt_ref, po_ref, co_ref, so_ref,
                   xdt_t, dec_t, b_t, c_t, xs_keep, y_t, z_ref, mix_ref):
    h = pl.program_id(0)

    @pl.when(h == 0)
    def _():
        a16 = _rms(x_ref[...], gm_ref[...]).astype(BF16)
        vp_new = _dot(a16, win_ref[:, 0:POOL_WIDTH])
        z_ref[...] = _dot(a16, win_ref[:, POOL_WIDTH:POOL_WIDTH + SSD_WIDTH])
        xbc = _dot(a16, win_ref[:, POOL_WIDTH + SSD_WIDTH:DT_OFF])
        dt_raw = _dot(a16, wdt_ref[...])
        conv = cb_ref[...] + sc_ref[0] * cw_ref[0:1, :]
        conv = conv + sc_ref[1] * cw_ref[1:2, :]
        conv = conv + sc_ref[2] * cw_ref[2:3, :]
        conv = conv + xbc * cw_ref[3:4, :]
        conv = _silu(conv)
        co_ref[0] = sc_ref[1]
        co_ref[1] = sc_ref[2]
        co_ref[2] = xbc
        xs = conv[:, 0:SSD_WIDTH]
        xs_keep[...] = xs
        b_t[...] = conv[:, SSD_WIDTH:SSD_WIDTH + SSD_BC].T
        c_t[...] = conv[:, SSD_WIDTH + SSD_BC:CONV_DIM].T
        dt = _softplus(dt_raw + dtb_ref[...])
        d_a = dt * (-jnp.exp(alog_ref[...]))
        dt_t = dt.T
        dec_t[...] = jnp.exp(d_a).T
        xs_t = xs.T
        for k in range(SSD_HEADS):
            blk = slice(k * SSD_HEAD_DIM, (k + 1) * SSD_HEAD_DIM)
            xdt_t[blk, :] = xs_t[blk, :] * dt_t[k:k + 1, :]
        v = vp_new
        for k in range(POOL_BUF - 1):
            po_ref[k] = sp_ref[k + 1]
        po_ref[POOL_BUF - 1] = v
        for g, w in enumerate(POOL_WINDOWS):
            blk = slice(g * POOL_GROUP_DIM, (g + 1) * POOL_GROUP_DIM)
            acc = sp_ref[POOL_BUF - (w - 1), :, blk]
            for k in range(w - 2, 0, -1):
                acc = acc + sp_ref[POOL_BUF - k, :, blk]
            acc = acc + v[:, blk]
            m = acc / float(min(PAST_LEN + 1, w)) - v[:, blk]
            yg = _dot(m.astype(BF16), plw_ref[g]) + plb_ref[:, blk]
            mix_ref[:, blk] = (yg * psc_ref[:, blk]).astype(BF16)

    g_off = pl.multiple_of((h // (SSD_HEADS // SSD_GROUPS)) * SSD_STATE, SSD_STATE)
    h_off = pl.multiple_of(h * SSD_HEAD_DIM, SSD_HEAD_DIM)
    b_g = b_t[pl.ds(g_off, SSD_STATE), :]
    c_g = c_t[pl.ds(g_off, SSD_STATE), :]
    dec = dec_t[pl.ds(h, 1), :]
    xdt = xdt_t[pl.ds(h_off, SSD_HEAD_DIM), :]
    y_rows = []
    for p in range(SSD_HEAD_DIM):
        s_new = st_ref[0, p] * dec + xdt[p:p + 1, :] * b_g
        so_ref[0, p] = s_new
        y_rows.append(jnp.sum(s_new * c_g, axis=0, keepdims=True))
    y_t[pl.ds(h_off, SSD_HEAD_DIM), :] = jnp.concatenate(y_rows, axis=0)

    @pl.when(h == pl.num_programs(0) - 1)
    def _():
        xs = xs_keep[...]
        y = y_t[...].T + dsk_ref[...] * xs
        y = y * _silu(z_ref[...])
        width = SSD_WIDTH // SSD_GROUPS
        for g in range(SSD_GROUPS):
            blk = slice(g * width, (g + 1) * width)
            yg = y[:, blk]
            rs = lax.rsqrt(jnp.mean(yg * yg, axis=-1, keepdims=True) + EPS)
            mix_ref[:, POOL_WIDTH + g * width:POOL_WIDTH + (g + 1) * width] = (yg * rs * nw_ref[:, blk]).astype(BF16)
        _out_proj_and_route(x_ref[...], mix_ref[...], wout_ref, gf_ref, rw_ref, rb_ref,
                            h_ref, u_ref, rt_ref, wt_ref, cnt_ref)


def _front_step(x, g_mix, w_in16, w_dt16, pool_t, conv_t, ssm_t, cw, cb, dtb, alog, dsk, nw, plw16, plb, psc,
                w_out16, g_ffn, r_w16, r_bias):
    n = x.shape[0]
    fixed2 = lambda h: (0, 0)
    fixed3 = lambda h: (0, 0, 0)
    st_spec = pl.BlockSpec((1, SSD_HEAD_DIM, SSD_STATE, n), lambda h: (h, 0, 0, 0))
    return pl.pallas_call(
        _mix_step_body,
        grid=(SSD_HEADS,),
        in_specs=[pl.BlockSpec((n, D_MODEL), fixed2), pl.BlockSpec((1, D_MODEL), fixed2),
                  pl.BlockSpec((D_MODEL, IN_DIM), fixed2), pl.BlockSpec((D_MODEL, LANES), fixed2),
                  pl.BlockSpec((POOL_BUF, n, POOL_WIDTH), fixed3),
                  pl.BlockSpec((SSD_CONV - 1, n, CONV_DIM), fixed3),
                  st_spec,
                  pl.BlockSpec((SSD_CONV, CONV_DIM), fixed2), pl.BlockSpec((1, CONV_DIM), fixed2),
                  pl.BlockSpec((1, LANES), fixed2), pl.BlockSpec((1, LANES), fixed2),
                  pl.BlockSpec((1, SSD_WIDTH), fixed2), pl.BlockSpec((1, SSD_WIDTH), fixed2),
                  pl.BlockSpec((len(POOL_WINDOWS), POOL_GROUP_DIM, POOL_GROUP_DIM), fixed3),
                  pl.BlockSpec((1, POOL_WIDTH), fixed2), pl.BlockSpec((1, POOL_WIDTH), fixed2),
                  pl.BlockSpec((D_MODEL, D_MODEL), fixed2), pl.BlockSpec((1, D_MODEL), fixed2),
                  pl.BlockSpec((D_MODEL, 2 * LANES), fixed2),
                  pl.BlockSpec((LANES, n), fixed2)],
        out_specs=[pl.BlockSpec((n, D_MODEL), fixed2), pl.BlockSpec((n, D_MODEL), fixed2),
                   pl.BlockSpec((1, 8, n), fixed3), pl.BlockSpec((n, 2 * LANES), fixed2),
                   pl.BlockSpec((1, N_EXPERTS, LANES), fixed3),
                   pl.BlockSpec((POOL_BUF, n, POOL_WIDTH), fixed3),
                   pl.BlockSpec((SSD_CONV - 1, n, CONV_DIM), fixed3), st_spec],
        out_shape=[jax.ShapeDtypeStruct((n, D_MODEL), F32), jax.ShapeDtypeStruct((n, D_MODEL), BF16),
                   jax.ShapeDtypeStruct((1, 8, n), F32), jax.ShapeDtypeStruct((n, 2 * LANES), F32),
                   jax.ShapeDtypeStruct((1, N_EXPERTS, LANES), F32),
                   jax.ShapeDtypeStruct(pool_t.shape, F32),
                   jax.ShapeDtypeStruct(conv_t.shape, F32), jax.ShapeDtypeStruct(ssm_t.shape, F32)],
        scratch_shapes=[pltpu.VMEM((SSD_WIDTH, n), F32), pltpu.VMEM((LANES, n), F32),
                        pltpu.VMEM((SSD_BC, n), F32), pltpu.VMEM((SSD_BC, n), F32),
                        pltpu.VMEM((n, SSD_WIDTH), F32), pltpu.VMEM((SSD_WIDTH, n), F32),
                        pltpu.VMEM((n, SSD_WIDTH), F32), pltpu.VMEM((n, D_MODEL), BF16)],
        compiler_params=_cparams("arbitrary"),
        name="front_step",
    )(x, g_mix, w_in16, w_dt16, pool_t, conv_t, ssm_t, cw, cb, dtb, alog, dsk, nw, plw16, plb, psc,
      w_out16, g_ffn, r_w16, r_bias)


PROMPT_TILE = MIX_ROWS


def _sort_tables(counts, tile_tokens, data_rows):
    cnt = jnp.concatenate(counts, axis=0)
    pc = (cnt + RUN_PAD - 1) // RUN_PAD * RUN_PAD
    off_local = jnp.cumsum(pc, axis=1) - pc
    tile_rows = jnp.sum(pc, axis=1)
    region = (jnp.sum(pc, axis=0) + MOE_TILE - 1) // MOE_TILE * MOE_TILE
    base = jnp.cumsum(region) - region
    dst = base[None, :] + jnp.cumsum(pc, axis=0) - pc
    per_pass = []
    lo = 0
    for c, tm in zip(counts, tile_tokens):
        hi = lo + c.shape[0]
        n_chunk = _sorted_rows_per_tile(tm) // RUN_PAD
        s = jnp.arange(n_chunk, dtype=jnp.int32) * RUN_PAD
        begins = off_local[lo:hi, None, :]
        ends = begins + pc[lo:hi, None, :]
        inside = (begins <= s[None, :, None]) & (s[None, :, None] < ends)
        shift = jnp.sum(jnp.where(inside, dst[lo:hi, None, :] - begins, 0), axis=2)
        valid = s[None, :] < tile_rows[lo:hi, None]
        parity = (jnp.arange(hi - lo, dtype=jnp.int32) % 2)[:, None]
        spare = data_rows + parity * _sorted_rows_per_tile(max(tile_tokens)) + s[None, :]
        scatter_dst = (jnp.where(valid, s[None, :] + shift, spare) // RUN_PAD).astype(jnp.int32)
        gather_src = (jnp.where(valid, s[None, :] + shift, 0) // RUN_PAD).astype(jnp.int32)
        off_v = jnp.broadcast_to(off_local[lo:hi, :, None].astype(F32), (hi - lo, N_EXPERTS, LANES))
        per_pass.append((scatter_dst, gather_src, off_v))
        lo = hi
    tiles_cum = jnp.cumsum(region // MOE_TILE)
    n_active = tiles_cum[-1]
    return per_pass, tiles_cum, n_active


def _max_sorted_rows(tile_counts, tile_tokens):
    rows = sum(n * (2 * tm + N_EXPERTS * (RUN_PAD - 1)) for n, tm in zip(tile_counts, tile_tokens))
    data_rows = (-(-rows // MOE_TILE) + N_EXPERTS) * MOE_TILE
    return data_rows, data_rows + 2 * _sorted_rows_per_tile(max(tile_tokens))


def _row(v):
    return v.reshape(1, -1).astype(F32)


def _per_head_rows(v):
    return jnp.broadcast_to(v.astype(F32)[:, None], (SSD_HEADS, SSD_Q))


def _pad_lanes(v):
    return jnp.pad(v.reshape(1, -1).astype(F32), ((0, 0), (0, LANES - v.size)))


def kernel(x_prompt, x_sample, p_prompt, p_sample, state_pool, state_conv, state_ssm, norm_mix, w_in, pool_lin_w, pool_lin_b, pool_scale, conv_w, conv_b, dt_bias, a_log, d_skip, ssd_norm, w_out, norm_ffn, router_grp_w, router_grp_b, router_exp_w, router_exp_b, exp_w_gate, exp_w_up, exp_w_down, norm_ple, ple_gate_w, ple_gate_b, ple_proj_w, norm_final):
    nb, seq, _ = x_prompt.shape
    ns = x_sample.shape[0]
    assert ns == LANES and x_sample.shape[1] == 1 and seq % MIX_ROWS == 0 and seq >= POOL_BUF

    w_in16 = w_in[0].astype(BF16)
    w_dt16 = jnp.pad(w_in[0][:, DT_OFF:], ((0, 0), (0, LANES - SSD_HEADS))).astype(BF16)
    w_out16 = w_out[0].astype(BF16)
    g_mix, g_ffn, g_ple, g_fin = _row(norm_mix[0]), _row(norm_ffn[0]), _row(norm_ple[0]), _row(norm_final)
    cw, cb = conv_w[0].astype(F32), _row(conv_b[0])
    dtb, alog = _pad_lanes(dt_bias[0]), _pad_lanes(a_log[0])
    dsk = _row(jnp.repeat(d_skip[0], SSD_HEAD_DIM))
    nw = _row(ssd_norm[0])
    plw16 = pool_lin_w[0].astype(BF16)
    plb, psc = _row(pool_lin_b[0]), _row(pool_scale[0])
    zeros4 = jnp.zeros((D_MODEL, 8 - N_EXPERT_GROUPS), F32)
    r_w = jnp.concatenate([router_grp_w[0], zeros4, router_exp_w[0],
                           jnp.zeros((D_MODEL, LANES - 8 - N_EXPERTS), F32)], axis=1)
    r_hi = r_w.astype(BF16)
    r_w16 = jnp.concatenate([r_hi, (r_w - r_hi.astype(F32)).astype(BF16)], axis=1)
    r_b = jnp.concatenate([router_grp_b[0], jnp.zeros((8 - N_EXPERT_GROUPS,), F32), router_exp_b[0],
                           jnp.zeros((LANES - 8 - N_EXPERTS,), F32)])
    wg = exp_w_gate[0].reshape(N_EXPERTS, D_MODEL, EXPERT_FF)
    wu = exp_w_up[0].reshape(N_EXPERTS, D_MODEL, EXPERT_FF)
    wd = exp_w_down[0].reshape(N_EXPERTS, EXPERT_FF, D_MODEL)
    pg16 = ple_gate_w[0].astype(BF16)
    pgb = _row(ple_gate_b[0])
    pp16 = ple_proj_w[0].astype(BF16)

    xp = x_prompt.reshape(nb * seq, D_MODEL)
    h1_p, u_p, rt_p, gate_p, cnt_p, st, pool_tail, conv_tail = _front_prompt(
        xp, g_mix, w_in16, w_dt16, cw, cb, _per_head_rows(dt_bias[0]), _per_head_rows(a_log[0]), dsk, nw, plw16, plb, psc,
        w_out16, g_ffn, r_w16, jnp.broadcast_to(r_b[:, None], (LANES, MIX_ROWS)), nb, seq)
    pool_p = pool_tail[:, TAIL_ROWS - POOL_BUF:]
    conv_p = conv_tail[:, CONV_HDR - (SSD_CONV - 1):]
    ssm_p = st

    xs_ = x_sample.reshape(ns, D_MODEL)
    h1_s, u_s, rt_s, gate_s, cnt_s, pool_t, conv_t, ssm_t = _front_step(
        xs_, g_mix, w_in16, w_dt16, jnp.transpose(state_pool[0], (1, 0, 2)), jnp.transpose(state_conv[0], (1, 0, 2)),
        jnp.transpose(state_ssm[0], (1, 2, 3, 0)), cw, cb, dtb, alog, dsk, nw, plw16, plb, psc,
        w_out16, g_ffn, r_w16, jnp.broadcast_to(r_b[:, None], (LANES, ns)))
    pool_s = jnp.transpose(pool_t, (1, 0, 2))
    conv_s = jnp.transpose(conv_t, (1, 0, 2))
    ssm_s = jnp.transpose(ssm_t, (3, 0, 1, 2))

    counts = [cnt_p[:, :, 0].astype(jnp.int32), cnt_s[:, :, 0].astype(jnp.int32)]
    tiles = (PROMPT_TILE, ns)
    data_rows, total_rows = _max_sorted_rows([c.shape[0] for c in counts], tiles)
    (tab_p, tab_s), tiles_cum, n_active = _sort_tables(counts, tiles, data_rows)
    xs_sorted, dest_p = _sort_tokens(tab_p[0], u_p, rt_p, tab_p[2], None, PROMPT_TILE, total_rows)
    xs_sorted, dest_s = _sort_tokens(tab_s[0], u_s, rt_s, tab_s[2], xs_sorted, ns, total_rows)
    tile_start = jnp.concatenate([jnp.zeros((1,), jnp.int32), tiles_cum.astype(jnp.int32)])
    ys_sorted = _moe_sorted(tile_start, n_active.reshape(1).astype(jnp.int32),
                            xs_sorted.reshape(total_rows, D_MODEL), wg, wu, wd).reshape(xs_sorted.shape)

    y_prompt = _ple(tab_p[1], h1_p, dest_p, gate_p, p_prompt[0].reshape(nb * seq, PLE_DIM),
                    g_ple, pg16, pgb, pp16, g_fin, ys_sorted, PROMPT_TILE)
    y_sample = _ple(tab_s[1], h1_s, dest_s, gate_s, p_sample[0].reshape(ns, PLE_DIM),
                    g_ple, pg16, pgb, pp16, g_fin, ys_sorted, ns)

    return (y_prompt.reshape(nb, seq, D_MODEL), y_sample.reshape(ns, 1, D_MODEL),
            pool_p[None], conv_p[None], ssm_p[None], pool_s[None], conv_s[None], ssm_s[None])
```

```python
import functools

import jax
import jax.numpy as jnp
import numpy as np
from jax import lax
from jax.experimental import pallas as pl
from jax.experimental.pallas import tpu as pltpu

F32 = jnp.float32
BF16 = jnp.bfloat16

D_MODEL = 1024
POOL_WIDTH = 512
POOL_WINDOWS = (2, 4, 8, 16)
POOL_GROUP_DIM = 128
POOL_BUF = 15
SSD_WIDTH = 512
SSD_HEAD_DIM = 64
SSD_HEADS = 8
SSD_GROUPS = 2
SSD_STATE = 64
SSD_CONV = 4
SSD_BC = SSD_GROUPS * SSD_STATE
CONV_DIM = SSD_WIDTH + 2 * SSD_BC
N_EXPERT_GROUPS = 4
EXPERTS_PER_GROUP = 8
N_EXPERTS = N_EXPERT_GROUPS * EXPERTS_PER_GROUP
EXPERT_FF = 256
PLE_DIM = 256
PAST_LEN = 16384
EPS = 1e-6

LANES = 128
DT_OFF = POOL_WIDTH + SSD_WIDTH + CONV_DIM
IN_DIM = DT_OFF + SSD_HEADS
VMEM_LIMIT = 56 * 1024 * 1024


def _cparams(*sem):
    return pltpu.CompilerParams(dimension_semantics=sem, vmem_limit_bytes=VMEM_LIMIT)


def _rms(x, g):
    return x * lax.rsqrt(jnp.mean(x * x, axis=-1, keepdims=True) + EPS) * g


def _sigmoid(x):
    return 1.0 / (1.0 + jnp.exp(-x))


def _silu(x):
    return x * _sigmoid(x)


def _split3(v):
    hi = v.astype(BF16)
    r = v - hi.astype(F32)
    mid = r.astype(BF16)
    lo = (r - mid.astype(F32)).astype(BF16)
    return hi, mid, lo


def _dot(a, b):
    return jnp.dot(a, b, preferred_element_type=F32)


def _dot_nt(a, b):
    return lax.dot_general(a, b, (((1,), (1,)), ((), ())), preferred_element_type=F32)


def _dot_tn(a, b):
    return lax.dot_general(a, b, (((0,), (0,)), ((), ())), preferred_element_type=F32)


def _route(lg):
    tm = lg.shape[1]
    gl = lg[0:N_EXPERT_GROUPS, :]
    gmax = jnp.max(gl, axis=0, keepdims=True)
    gsum = jnp.sum(jnp.exp(gl - gmax), axis=0, keepdims=True)
    g_w = 1.0 / gsum
    gi = lax.broadcasted_iota(jnp.int32, gl.shape, 0)
    g_idx = jnp.min(jnp.where(gl == gmax, gi, N_EXPERT_GROUPS), axis=0, keepdims=True)
    sel = jnp.zeros((EXPERTS_PER_GROUP, tm), F32)
    for g in range(N_EXPERT_GROUPS):
        blk = lg[8 + g * EXPERTS_PER_GROUP:8 + (g + 1) * EXPERTS_PER_GROUP, :]
        sel = jnp.where(g_idx == g, blk, sel)
    ei = lax.broadcasted_iota(jnp.int32, sel.shape, 0)
    m1 = jnp.max(sel, axis=0, keepdims=True)
    i1 = jnp.min(jnp.where(sel == m1, ei, EXPERTS_PER_GROUP), axis=0, keepdims=True)
    rest = jnp.where(ei == i1, -jnp.inf, sel)
    m2 = jnp.max(rest, axis=0, keepdims=True)
    i2 = jnp.min(jnp.where(rest == m2, ei, EXPERTS_PER_GROUP), axis=0, keepdims=True)
    p2 = jnp.exp(m2 - m1)
    w1 = g_w / (1.0 + p2)
    w2 = g_w * p2 / (1.0 + p2)
    return g_idx * EXPERTS_PER_GROUP + i1, g_idx * EXPERTS_PER_GROUP + i2, w1, w2


def _out_proj_and_route(x, mix16, w_ref, g_ref, rw_ref, rb_ref, h_ref, u_ref, rt_ref, wt_ref, cnt_ref,
                        r0=0, accumulate=False):
    tm = x.shape[0]
    h = x + _dot(mix16, w_ref[...])
    h_ref[r0:r0 + tm, :] = h
    _route_rows(h, g_ref, rw_ref, rb_ref, u_ref, rt_ref, wt_ref, cnt_ref, r0=r0, accumulate=accumulate)


def _route_rows(h, g_ref, rw_ref, rb_ref, u_ref, rt_ref, wt_ref, cnt_ref, r0=0, accumulate=False):
    tm = h.shape[0]
    u = _rms(h, g_ref[...])
    u_hi = u.astype(BF16)
    u_ref[r0:r0 + tm, :] = u_hi
    u_lo = (u - u_hi.astype(F32)).astype(BF16)
    both = _dot(u_hi, rw_ref[...])
    lg_rows = both[:, 0:LANES] + both[:, LANES:2 * LANES] + _dot(u_lo, rw_ref[:, 0:LANES])
    lg = lg_rows.T + rb_ref[:, r0:r0 + tm]
    b1, b2, w1, w2 = _route(lg)
    r8 = lax.broadcasted_iota(jnp.int32, (8, tm), 0)
    rt_ref[0, :, r0:r0 + tm] = jnp.where(r8 == 0, b1.astype(F32), jnp.where(r8 == 1, b2.astype(F32), 0.0))
    wt_ref[r0:r0 + tm, :] = jnp.concatenate(
        [jnp.broadcast_to(w1, (LANES, tm)).T, jnp.broadcast_to(w2, (LANES, tm)).T], axis=1)
    kio = lax.broadcasted_iota(jnp.int32, (N_EXPERTS, tm), 0)
    hits = ((kio == b1) | (kio == b2)).astype(F32)
    cnt = jnp.broadcast_to(jnp.sum(hits, axis=1, keepdims=True), (N_EXPERTS, LANES))
    cnt_ref[0] = cnt_ref[0] + cnt if accumulate else cnt


RUN_PAD = 16
MOE_TILE = 256
MOE_GROUP = 3
MOE_AHEAD = 8
MOE_BUFS = MOE_AHEAD + MOE_GROUP + 1


def _tile_lanes(v, width):
    reps = width // LANES
    return v if reps == 1 else jnp.concatenate([v] * reps, axis=1)


def _sorted_rows_per_tile(tm):
    need = 2 * tm + N_EXPERTS * (RUN_PAD - 1)
    return -(-need // LANES) * LANES


def _sort_body(cd_ref, u_ref, rt_ref, off_ref, rtn_ref, offn_ref, before_ref, *rest):
    xs_ref, dt_ref, loc, dsc, sem = rest[-5:]
    i = pl.program_id(0)
    tm = u_ref.shape[0]
    n_chunk = loc.shape[1]
    rows = n_chunk * RUN_PAD
    slot = i % 2

    def destinations(rt_r, off_r, sl):
        rt = rt_r[0]
        b1 = rt[0:1, :].astype(jnp.int32)
        b2 = rt[1:2, :].astype(jnp.int32)
        kio = lax.broadcasted_iota(jnp.int32, (N_EXPERTS, tm), 0)
        o1 = kio == b1
        o2 = kio == b2
        start = _dot((o1 | o2).astype(BF16), before_ref[...]) + _tile_lanes(off_r[0], tm)
        dsc[sl, 0:1, :] = jnp.sum(jnp.where(o1, start, 0.0), axis=0, keepdims=True)
        dsc[sl, 1:2, :] = jnp.sum(jnp.where(o2, start, 0.0), axis=0, keepdims=True)

    @pl.when(i == 0)
    def _():
        destinations(rt_ref, off_ref, slot)

    d1 = dsc[slot, 0:1, :]
    d2 = dsc[slot, 1:2, :]
    dt_ref[...] = jnp.concatenate([jnp.broadcast_to(d1, (LANES, tm)).T, jnp.broadcast_to(d2, (LANES, tm)).T], axis=1)
    rio = lax.broadcasted_iota(jnp.int32, (rows, tm), 0)
    perm = ((rio == d1.astype(jnp.int32)) | (rio == d2.astype(jnp.int32))).astype(BF16)
    loc[slot] = _dot(perm, u_ref[...]).astype(BF16).reshape(n_chunk, RUN_PAD, D_MODEL)
    destinations(rtn_ref, offn_ref, 1 - slot)

    def chunk_copy(step, sl, c):
        return pltpu.make_async_copy(loc.at[sl, c], xs_ref.at[cd_ref[step, c]], sem.at[sl])

    for c in range(n_chunk):
        chunk_copy(i, slot, c).start()

    @pl.when(i > 0)
    def _():
        for c in range(n_chunk):
            chunk_copy(i - 1, 1 - slot, c).wait()

    @pl.when(i == pl.num_programs(0) - 1)
    def _():
        for c in range(n_chunk):
            chunk_copy(i, slot, c).wait()


def _sort_tokens(chunk_dst, u16, rt, off_v, xs_prev, tm, total_rows):
    t = u16.shape[0]
    rows = _sorted_rows_per_tile(tm)
    in_specs = [pl.BlockSpec((tm, D_MODEL), lambda i, cd: (i, 0)),
                pl.BlockSpec((1, 8, tm), lambda i, cd: (i, 0, 0)),
                pl.BlockSpec((1, N_EXPERTS, LANES), lambda i, cd: (i, 0, 0)),
                pl.BlockSpec((1, 8, tm), lambda i, cd: (jnp.minimum(i + 1, t // tm - 1), 0, 0)),
                pl.BlockSpec((1, N_EXPERTS, LANES), lambda i, cd: (jnp.minimum(i + 1, t // tm - 1), 0, 0)),
                pl.BlockSpec((tm, tm), lambda i, cd: (0, 0))]
    before = jnp.asarray(np.triu(np.ones((tm, tm), np.float32), 1)).astype(BF16)
    args = [chunk_dst, u16, rt, off_v, rt, off_v, before]
    aliases = {}
    if xs_prev is not None:
        in_specs.append(pl.BlockSpec(memory_space=pl.ANY))
        aliases = {len(args): 0}
        args.append(xs_prev)
    return pl.pallas_call(
        _sort_body,
        grid_spec=pltpu.PrefetchScalarGridSpec(
            num_scalar_prefetch=1,
            grid=(t // tm,),
            in_specs=in_specs,
            out_specs=[pl.BlockSpec(memory_space=pl.ANY),
                       pl.BlockSpec((tm, 2 * LANES), lambda i, cd: (i, 0))],
            scratch_shapes=[pltpu.VMEM((2, rows // RUN_PAD, RUN_PAD, D_MODEL), BF16),
                            pltpu.VMEM((2, 8, tm), F32), pltpu.SemaphoreType.DMA((2,))]),
        out_shape=[jax.ShapeDtypeStruct((total_rows // RUN_PAD, RUN_PAD, D_MODEL), BF16),
                   jax.ShapeDtypeStruct((t, 2 * LANES), F32)],
        input_output_aliases=aliases,
        compiler_params=_cparams("arbitrary"),
        name="sort_tokens",
    )(*args)


def _moe_body(ts_ref, na_ref, xs_ref, wg_ref, wu_ref, wd_ref, ys_ref, xbuf, ybuf, wg16, wu16, wd16, sem_in, sem_out):
    k = pl.program_id(0)
    n_act = na_ref[0]

    def x_copy(g, sl):
        return pltpu.make_async_copy(xs_ref.at[pl.ds(pl.multiple_of(g * MOE_TILE, MOE_TILE), MOE_TILE), :],
                                     xbuf.at[sl], sem_in.at[sl])

    def y_copy(g, sl):
        return pltpu.make_async_copy(ybuf.at[sl],
                                     ys_ref.at[pl.ds(pl.multiple_of(g * MOE_TILE, MOE_TILE), MOE_TILE), :],
                                     sem_out.at[sl])

    @pl.when(k == 0)
    def _():
        for j in range(MOE_AHEAD):
            @pl.when(j < n_act)
            def _():
                x_copy(j, j).start(priority=1)

    wg16[...] = wg_ref[0].astype(BF16)
    wu16[...] = wu_ref[0].astype(BF16)
    wd16[...] = wd_ref[0].astype(BF16)

    def arrive(g):
        sl = g % MOE_BUFS

        @pl.when(g + MOE_AHEAD < n_act)
        def _():
            x_copy(g + MOE_AHEAD, (g + MOE_AHEAD) % MOE_BUFS).start(priority=1)

        x_copy(g, sl).wait()

        @pl.when(g >= MOE_BUFS)
        def _():
            y_copy(g - MOE_BUFS, sl).wait()

    def tiles(g, count):
        for j in range(count):
            arrive(g + j)
        acts = []
        for j in range(count):
            x = xbuf[(g + j) % MOE_BUFS]
            acts.append((_silu(_dot(x, wg16[...])) * _dot(x, wu16[...])).astype(BF16))
        for j in range(count):
            ybuf[(g + j) % MOE_BUFS] = _dot(acts[j], wd16[...]).astype(BF16)
        for j in range(count):
            y_copy(g + j, (g + j) % MOE_BUFS).start()

    g0 = ts_ref[k]
    n_here = ts_ref[k + 1] - g0

    def group(p, carry):
        tiles(g0 + MOE_GROUP * p, MOE_GROUP)
        return carry

    n_groups = n_here // MOE_GROUP
    lax.fori_loop(0, n_groups, group, 0)
    for rest in range(1, MOE_GROUP):
        @pl.when(n_here - n_groups * MOE_GROUP == rest)
        def _():
            tiles(g0 + n_groups * MOE_GROUP, rest)

    @pl.when(k == pl.num_programs(0) - 1)
    def _():
        for j in range(1, MOE_BUFS + 1):
            @pl.when(n_act >= j)
            def _():
                y_copy(n_act - j, (n_act - j) % MOE_BUFS).wait()


def _moe_sorted(tile_start, n_active, xs, wg, wu, wd):
    w_map = lambda k, ts, na: (k, 0, 0)
    return pl.pallas_call(
        _moe_body,
        grid_spec=pltpu.PrefetchScalarGridSpec(
            num_scalar_prefetch=2,
            grid=(N_EXPERTS,),
            in_specs=[pl.BlockSpec(memory_space=pl.ANY),
                      pl.BlockSpec((1, D_MODEL, EXPERT_FF), w_map),
                      pl.BlockSpec((1, D_MODEL, EXPERT_FF), w_map),
                      pl.BlockSpec((1, EXPERT_FF, D_MODEL), w_map)],
            out_specs=pl.BlockSpec(memory_space=pl.ANY),
            scratch_shapes=[pltpu.VMEM((MOE_BUFS, MOE_TILE, D_MODEL), BF16),
                            pltpu.VMEM((MOE_BUFS, MOE_TILE, D_MODEL), BF16),
                            pltpu.VMEM((D_MODEL, EXPERT_FF), BF16), pltpu.VMEM((D_MODEL, EXPERT_FF), BF16),
                            pltpu.VMEM((EXPERT_FF, D_MODEL), BF16),
                            pltpu.SemaphoreType.DMA((MOE_BUFS,)), pltpu.SemaphoreType.DMA((MOE_BUFS,))]),
        out_shape=jax.ShapeDtypeStruct(xs.shape, BF16),
        compiler_params=_cparams("arbitrary"),
        name="moe_sorted",
    )(tile_start, n_active, xs, wg, wu, wd)


def _ple_body(cd_ref, h_ref, dt_ref, wt_ref, p_ref, gn_ref, wg_ref, bg_ref, wp_ref, fn_ref, ys_ref,
              o_ref, loc, sem):
    i = pl.program_id(0)
    tm = h_ref.shape[0]
    n_chunk = loc.shape[1]
    rows = n_chunk * RUN_PAD
    slot = i % 2

    def chunk_copy(step, sl, c):
        return pltpu.make_async_copy(ys_ref.at[cd_ref[step, c]], loc.at[sl, c], sem.at[sl])

    def fetch(step, sl):
        for c in range(n_chunk):
            chunk_copy(step, sl, c).start()

    @pl.when(i == 0)
    def _():
        fetch(0, 0)

    @pl.when(i + 1 < pl.num_programs(0))
    def _():
        fetch(i + 1, 1 - slot)

    for c in range(n_chunk):
        chunk_copy(i, slot, c).wait()

    ci = lax.broadcasted_iota(jnp.int32, (tm, LANES), 1)
    d1 = dt_ref[:, 0:LANES].astype(jnp.int32)
    d2 = dt_ref[:, LANES:2 * LANES].astype(jnp.int32)
    w1 = wt_ref[:, 0:LANES]
    w2 = wt_ref[:, LANES:2 * LANES]
    sel = jnp.concatenate(
        [jnp.where(d1 == ci + m * LANES, w1, jnp.where(d2 == ci + m * LANES, w2, 0.0)) for m in range(rows // LANES)],
        axis=1).astype(BF16)
    h = h_ref[...] + _dot(sel, loc[slot].reshape(rows, D_MODEL))
    a16 = _rms(h, gn_ref[...]).astype(BF16)
    gate = _sigmoid(_dot(a16, wg_ref[...]) + bg_ref[...])
    pp = _dot(p_ref[...].astype(BF16), wp_ref[...])
    h = h + gate * pp
    o_ref[...] = _rms(h, fn_ref[...])


def _ple(chunk_src, h1, dest_t, gate_t, p, g_ple, wg16, bg, wp16, g_final, ys, tm):
    t = h1.shape[0]
    row = lambda i, cd: (i, 0)
    fixed = lambda i, cd: (0, 0)
    return pl.pallas_call(
        _ple_body,
        grid_spec=pltpu.PrefetchScalarGridSpec(
            num_scalar_prefetch=1,
            grid=(t // tm,),
            in_specs=[pl.BlockSpec((tm, D_MODEL), row), pl.BlockSpec((tm, 2 * LANES), row),
                      pl.BlockSpec((tm, 2 * LANES), row), pl.BlockSpec((tm, PLE_DIM), row),
                      pl.BlockSpec((1, D_MODEL), fixed), pl.BlockSpec((D_MODEL, D_MODEL), fixed),
                      pl.BlockSpec((1, D_MODEL), fixed), pl.BlockSpec((PLE_DIM, D_MODEL), fixed),
                      pl.BlockSpec((1, D_MODEL), fixed), pl.BlockSpec(memory_space=pl.ANY)],
            out_specs=pl.BlockSpec((tm, D_MODEL), row),
            scratch_shapes=[pltpu.VMEM((2, _sorted_rows_per_tile(tm) // RUN_PAD, RUN_PAD, D_MODEL), BF16),
                            pltpu.SemaphoreType.DMA((2,))]),
        out_shape=jax.ShapeDtypeStruct((t, D_MODEL), F32),
        compiler_params=_cparams("arbitrary"),
        name="ple_final",
    )(chunk_src, h1, dest_t, gate_t, p, g_ple, wg16, bg, wp16, g_final, ys)


MIX_ROWS = 512
SSD_Q = 128
POOL_HDR = 128
CONV_HDR = 8
TAIL_ROWS = 16


def _softplus(x):
    return jnp.maximum(x, 0.0) + jnp.log1p(jnp.exp(-jnp.abs(x)))


def _stack_terms(v, terms):
    parts = [t.astype(F32) for t in _split3(v)[:terms]]
    parts.append(jnp.zeros(((4 - terms) * v.shape[0], v.shape[1]), F32))
    return jnp.concatenate(parts, axis=0).astype(BF16)


def _cumsum_lanes(v, tri16):
    r = _dot_nt(_stack_terms(v, 3), tri16)
    n = v.shape[0]
    return r[0:n] + r[n:2 * n] + r[2 * n:3 * n]


def _expand_heads(v, expand32, terms):
    return _dot_tn(_stack_terms(v, terms), expand32)


def _mixer_constants():
    q = SSD_Q
    li = np.arange(q)[:, None]
    tri = (li >= np.arange(q)[None, :])
    expand = (np.arange(SSD_WIDTH)[None, :] // SSD_HEAD_DIM) == (np.arange(4 * SSD_HEADS)[:, None] % SSD_HEADS)
    hpg = SSD_HEADS // SSD_GROUPS
    block = (np.arange(SSD_BC)[:, None] // SSD_STATE) == (np.arange(SSD_WIDTH)[None, :] // (SSD_HEAD_DIM * hpg))
    wj = np.arange(2 * q)[None, :]
    band = np.stack([(wj <= li + POOL_HDR) & (wj > li + POOL_HDR - w) for w in POOL_WINDOWS])
    inv_count = np.stack([np.broadcast_to(1.0 / np.minimum(li + 1, w), (q, LANES)) for w in POOL_WINDOWS])
    as_f32 = lambda m: jnp.asarray(m.astype(np.float32))
    return (as_f32(tri), as_f32(tri).astype(BF16), as_f32(expand).astype(BF16), as_f32(block),
            as_f32(band).astype(BF16), as_f32(inv_count))


def _front_body(steps_per_seq, x_ref, xp_ref, gm_ref, win_ref, wdt_ref, trif_ref, tri_ref, exp_ref, bmask_ref, band_ref,
                invc_ref, cw_ref, cb_ref, dtb_ref, alog_ref, dsk_ref, nw_ref,
                plw_ref, plb_ref, psc_ref, wout_ref, gf_ref, rw_ref, rb_ref,
                h_ref, u_ref, rt_ref, wt_ref, cnt_ref, st_ref, ptail_ref, ctail_ref,
                pool_ext, conv_ext, s_ref, z_ref, dt_ref, mix_ref, mixp_ref, a_ref):
    s = pl.program_id(0)
    n_tiles = pl.num_programs(0) - 1
    real = s < n_tiles
    c = jnp.minimum(s, n_tiles - 1) % steps_per_seq
    rows = MIX_ROWS
    q_len = SSD_Q

    @pl.when(s == 0)
    def _():
        mixp_ref[...] = jnp.zeros_like(mixp_ref)

    @pl.when(c == 0)
    def _():
        pool_ext[0:POOL_HDR, :] = jnp.zeros((POOL_HDR, POOL_WIDTH), F32)
        conv_ext[0:CONV_HDR, :] = jnp.zeros((CONV_HDR, CONV_DIM), F32)
        s_ref[...] = jnp.zeros_like(s_ref)

    @pl.when(c > 0)
    def _():
        pool_ext[0:POOL_HDR, :] = pool_ext[rows:rows + POOL_HDR, :]
        conv_ext[0:CONV_HDR, :] = conv_ext[rows:rows + CONV_HDR, :]

    causal = trif_ref[...] > 0.5
    tri = tri_ref[...]
    expand = exp_ref[...]
    blockmask = bmask_ref[...]
    lane = lax.broadcasted_iota(jnp.int32, (q_len, LANES), 1)
    left = lane < SSD_HEAD_DIM
    a_neg = -jnp.exp(alog_ref[...])

    half = rows // 2
    n_cols = 256

    def norm_rows():
        for r0 in (0, half):
            a_ref[r0:r0 + half, :] = _rms(x_ref[r0:r0 + half, :], gm_ref[...]).astype(BF16)

    def in_proj_chunk(ref, row, k, col0):
        def run():
            ref[row:row + rows, k:k + n_cols] = _dot(a_ref[...], win_ref[:, col0 + k:col0 + k + n_cols])
        return run

    def dt_chunk():
        dt_ref[...] = _dot(a_ref[...], wdt_ref[...])

    conv_chunks = [in_proj_chunk(conv_ext, CONV_HDR, k, POOL_WIDTH + SSD_WIDTH) for k in range(0, CONV_DIM, n_cols)]
    z_chunks = [in_proj_chunk(z_ref, 0, k, POOL_WIDTH) for k in range(0, SSD_WIDTH, n_cols)]
    pool_chunks = [in_proj_chunk(pool_ext, POOL_HDR, k, 0) for k in range(0, POOL_WIDTH, n_cols)]

    def out_proj_chunk(k):
        def run():
            h_ref[:, k:k + n_cols] = xp_ref[:, k:k + n_cols] + _dot(mixp_ref[...], wout_ref[:, k:k + n_cols])
        return run

    def route(r0):
        def run():
            _route_rows(h_ref[r0:r0 + half, :], gf_ref, rw_ref, rb_ref, u_ref, rt_ref, wt_ref, cnt_ref,
                        r0=r0, accumulate=r0 > 0)
        return run

    out_chunks = [out_proj_chunk(k) for k in range(0, D_MODEL, n_cols)]
    routes = [route(0), route(half)]

    def mix_pair(q_first, fillers):
        fillers = list(fillers)

        def fill():
            if fillers:
                fillers.pop(0)()

        subs = range(q_first, q_first + 2)
        xs_l, b16_l, call_l, c16_l = {}, {}, {}, {}
        for q in subs:
            base = CONV_HDR + q * q_len
            conv = cb_ref[...] + conv_ext[base - 3:base - 3 + q_len, :] * cw_ref[0:1, :]
            conv = conv + conv_ext[base - 2:base - 2 + q_len, :] * cw_ref[1:2, :]
            conv = conv + conv_ext[base - 1:base - 1 + q_len, :] * cw_ref[2:3, :]
            conv = conv + conv_ext[base:base + q_len, :] * cw_ref[3:4, :]
            conv = _silu(conv)
            xs_l[q] = conv[:, 0:SSD_WIDTH]
            b16_l[q] = conv[:, SSD_WIDTH:SSD_WIDTH + SSD_BC].astype(BF16)
            call_l[q] = conv[:, SSD_WIDTH + SSD_BC:CONV_DIM]
            c16_l[q] = call_l[q].astype(BF16)
            fill()
        dtt_l = {q: _softplus(dt_ref[q * q_len:(q + 1) * q_len, :].T[0:SSD_HEADS, :] + dtb_ref[...]) for q in subs}
        acst_l = {q: _cumsum_lanes(dtt_l[q] * a_neg, tri) for q in subs}
        acs_l = {q: jnp.concatenate([acst_l[q], jnp.zeros((LANES - SSD_HEADS, q_len), F32)], axis=0).T
                 for q in subs}
        fill()
        dtx_l = {q: _expand_heads(dtt_l[q], expand, 2) for q in subs}
        acsx_l = {q: _expand_heads(acst_l[q], expand, 3) for q in subs}
        lastx_l = {q: acsx_l[q][q_len - 1:q_len, :] for q in subs}
        xdt_l = {q: xs_l[q] * dtx_l[q] for q in subs}
        xdt16_l = {q: xdt_l[q].astype(BF16) for q in subs}
        fill()
        contrib_l = {q: _dot_tn(b16_l[q], (xdt_l[q] * jnp.exp(lastx_l[q] - acsx_l[q])).astype(BF16)) * blockmask
                     for q in subs}
        cb_l = {q: [_dot_nt(jnp.where(lax.shift_right_logical(lane, 6) == g, call_l[q], 0.0).astype(BF16), b16_l[q])
                    for g in range(SSD_GROUPS)] for q in subs}
        fill()
        zs_l = {}
        for q in subs:
            zs = []
            for h in range(SSD_HEADS):
                blk = slice((h // 2) * LANES, (h // 2 + 1) * LANES)
                seg = acs_l[q][:, h:h + 1] - acst_l[q][h:h + 1, :]
                decay = jnp.where(causal, jnp.exp(seg), 0.0)
                scores = (cb_l[q][h // (SSD_HEADS // SSD_GROUPS)] * decay).astype(BF16)
                zs.append(_dot(scores, xdt16_l[q][:, blk]))
                if h % 4 == 3:
                    fill()
            zs_l[q] = zs
        yoff_l = {}
        for q in subs:
            s_old = s_ref[...]
            yoff_l[q] = _dot(c16_l[q], s_old.astype(BF16)) * jnp.exp(acsx_l[q])
            s_ref[...] = s_old * jnp.exp(lastx_l[q]) + contrib_l[q]
        fill()
        for q in subs:
            r0 = q * q_len
            z = z_ref[r0:r0 + q_len, :]
            y_blocks = []
            for j in range(SSD_HEADS // 2):
                blk = slice(j * LANES, (j + 1) * LANES)
                y = jnp.where(left, zs_l[q][2 * j], zs_l[q][2 * j + 1]) + yoff_l[q][:, blk]
                y = y + dsk_ref[:, blk] * xs_l[q][:, blk]
                y_blocks.append(y * _silu(z[:, blk]))
            for g in range(SSD_GROUPS):
                y0, y1 = y_blocks[2 * g], y_blocks[2 * g + 1]
                ss = jnp.sum(y0 * y0, axis=-1, keepdims=True) + jnp.sum(y1 * y1, axis=-1, keepdims=True)
                rs = lax.rsqrt(ss * (1.0 / (2 * LANES)) + EPS)
                for k, yk in ((2 * g, y0), (2 * g + 1, y1)):
                    blk = slice(k * LANES, (k + 1) * LANES)
                    out = yk * rs * nw_ref[:, blk]
                    mix_ref[r0:r0 + q_len, POOL_WIDTH + k * LANES:POOL_WIDTH + (k + 1) * LANES] = out.astype(BF16)
            fill()
        for q in subs:
            r0 = q * q_len
            for g, w in enumerate(POOL_WINDOWS):
                blk = slice(g * POOL_GROUP_DIM, (g + 1) * POOL_GROUP_DIM)
                pe = pool_ext[r0:r0 + 2 * q_len, blk]
                hi = pe.astype(BF16)
                lo = (pe - hi.astype(F32)).astype(BF16)
                both = _dot(band_ref[g], jnp.concatenate([hi, lo], axis=1))
                winsum = both[:, 0:POOL_GROUP_DIM] + both[:, POOL_GROUP_DIM:2 * POOL_GROUP_DIM]
                inv = jnp.where(c == 0, invc_ref[g], 1.0 / w) if q == 0 else 1.0 / w
                m = winsum * inv - pe[q_len:2 * q_len, :]
                yg = _dot(m.astype(BF16), plw_ref[g]) + plb_ref[:, blk]
                mix_ref[r0:r0 + q_len, blk] = (yg * psc_ref[:, blk]).astype(BF16)
            fill()
        while fillers:
            fill()

    @pl.when(real)
    def _():
        norm_rows()
        for run in conv_chunks + [dt_chunk]:
            run()
        mix_pair(0, z_chunks + pool_chunks + out_chunks[:2])
        mix_pair(2, out_chunks[2:] + routes)
        mixp_ref[...] = mix_ref[...]

    @pl.when(jnp.logical_not(real))
    def _():
        for run in out_chunks + routes:
            run()

    @pl.when(real & (c == steps_per_seq - 1))
    def _():
        hpg = SSD_HEADS // SSD_GROUPS
        for j in range(SSD_HEADS // 2):
            flipped = s_ref[:, j * LANES:(j + 1) * LANES].T
            g = (2 * j) // hpg
            for k in range(2):
                st_ref[0, 2 * j + k] = flipped[k * SSD_HEAD_DIM:(k + 1) * SSD_HEAD_DIM,
                                               g * SSD_STATE:(g + 1) * SSD_STATE]
        ptail_ref[0] = pool_ext[POOL_HDR + rows - TAIL_ROWS:POOL_HDR + rows, :]
        ctail_ref[0] = conv_ext[rows:rows + CONV_HDR, :]


def _front_prompt(x, g_mix, w_in16, w_dt16, cw, cb, dtb, alog, dsk, nw, plw16, plb, psc, w_out16, g_ffn, r_w16, r_bias,
                  nb, seq):
    steps = seq // MIX_ROWS
    nt = nb * steps
    cur = lambda s: (jnp.minimum(s, nt - 1), 0)
    prev = lambda s: (jnp.maximum(s - 1, 0), 0)
    prev3 = lambda s: (jnp.maximum(s - 1, 0), 0, 0)
    per_seq = lambda s: (jnp.minimum(s, nt - 1) // steps, 0, 0)
    fixed2 = lambda s: (0, 0)
    fixed3 = lambda s: (0, 0, 0)
    return pl.pallas_call(
        functools.partial(_front_body, steps),
        grid=(nt + 1,),
        in_specs=[pl.BlockSpec((MIX_ROWS, D_MODEL), cur), pl.BlockSpec((MIX_ROWS, D_MODEL), prev),
                  pl.BlockSpec((1, D_MODEL), fixed2), pl.BlockSpec((D_MODEL, IN_DIM), fixed2),
                  pl.BlockSpec((D_MODEL, LANES), fixed2),
                  pl.BlockSpec((SSD_Q, SSD_Q), fixed2), pl.BlockSpec((SSD_Q, SSD_Q), fixed2),
                  pl.BlockSpec((4 * SSD_HEADS, SSD_WIDTH), fixed2), pl.BlockSpec((SSD_BC, SSD_WIDTH), fixed2),
                  pl.BlockSpec((len(POOL_WINDOWS), SSD_Q, 2 * SSD_Q), fixed3),
                  pl.BlockSpec((len(POOL_WINDOWS), SSD_Q, LANES), fixed3),
                  pl.BlockSpec((SSD_CONV, CONV_DIM), fixed2), pl.BlockSpec((1, CONV_DIM), fixed2),
                  pl.BlockSpec((SSD_HEADS, SSD_Q), fixed2), pl.BlockSpec((SSD_HEADS, SSD_Q), fixed2),
                  pl.BlockSpec((1, SSD_WIDTH), fixed2), pl.BlockSpec((1, SSD_WIDTH), fixed2),
                  pl.BlockSpec((len(POOL_WINDOWS), POOL_GROUP_DIM, POOL_GROUP_DIM), fixed3),
                  pl.BlockSpec((1, POOL_WIDTH), fixed2), pl.BlockSpec((1, POOL_WIDTH), fixed2),
                  pl.BlockSpec((D_MODEL, D_MODEL), fixed2), pl.BlockSpec((1, D_MODEL), fixed2),
                  pl.BlockSpec((D_MODEL, 2 * LANES), fixed2),
                  pl.BlockSpec((LANES, MIX_ROWS), fixed2)],
        out_specs=[pl.BlockSpec((MIX_ROWS, D_MODEL), prev), pl.BlockSpec((MIX_ROWS, D_MODEL), prev),
                   pl.BlockSpec((1, 8, MIX_ROWS), prev3), pl.BlockSpec((MIX_ROWS, 2 * LANES), prev),
                   pl.BlockSpec((1, N_EXPERTS, LANES), prev3),
                   pl.BlockSpec((1, SSD_HEADS, SSD_HEAD_DIM, SSD_STATE),
                                lambda s: (jnp.minimum(s, nt - 1) // steps, 0, 0, 0)),
                   pl.BlockSpec((1, TAIL_ROWS, POOL_WIDTH), per_seq),
                   pl.BlockSpec((1, CONV_HDR, CONV_DIM), per_seq)],
        out_shape=[jax.ShapeDtypeStruct((nb * seq, D_MODEL), F32), jax.ShapeDtypeStruct((nb * seq, D_MODEL), BF16),
                   jax.ShapeDtypeStruct((nt, 8, MIX_ROWS), F32), jax.ShapeDtypeStruct((nb * seq, 2 * LANES), F32),
                   jax.ShapeDtypeStruct((nt, N_EXPERTS, LANES), F32),
                   jax.ShapeDtypeStruct((nb, SSD_HEADS, SSD_HEAD_DIM, SSD_STATE), F32),
                   jax.ShapeDtypeStruct((nb, TAIL_ROWS, POOL_WIDTH), F32),
                   jax.ShapeDtypeStruct((nb, CONV_HDR, CONV_DIM), F32)],
        scratch_shapes=[pltpu.VMEM((POOL_HDR + MIX_ROWS, POOL_WIDTH), F32),
                        pltpu.VMEM((CONV_HDR + MIX_ROWS, CONV_DIM), F32),
                        pltpu.VMEM((SSD_BC, SSD_WIDTH), F32),
                        pltpu.VMEM((MIX_ROWS, SSD_WIDTH), F32), pltpu.VMEM((MIX_ROWS, LANES), F32),
                        pltpu.VMEM((MIX_ROWS, D_MODEL), BF16), pltpu.VMEM((MIX_ROWS, D_MODEL), BF16),
                        pltpu.VMEM((MIX_ROWS, D_MODEL), BF16)],
        compiler_params=_cparams("arbitrary"),
        name="front_prompt",
    )(x, x, g_mix, w_in16, w_dt16, *_mixer_constants(), cw, cb, dtb, alog, dsk, nw, plw16, plb, psc, w_out16, g_ffn,
      r_w16, r_bias)


def _mix_step_body(x_ref, gm_ref, win_ref, wdt_ref, sp_ref, sc_ref, st_ref, cw_ref, cb_ref, dtb_ref, alog_ref,
                   dsk_ref, nw_ref, plw_ref, plb_ref, psc_ref, wout_ref, gf_ref, rw_ref, rb_ref,
                   h_ref, u_ref, rt_ref, wt_ref, cnt_ref, po_ref, co_ref, so_ref,
                   xdt_t, dec_t, b_t, c_t, xs_keep, y_t, z_ref, mix_ref):
    h = pl.program_id(0)

    @pl.when(h == 0)
    def _():
        a16 = _rms(x_ref[...], gm_ref[...]).astype(BF16)
        vp_new = _dot(a16, win_ref[:, 0:POOL_WIDTH])
        z_ref[...] = _dot(a16, win_ref[:, POOL_WIDTH:POOL_WIDTH + SSD_WIDTH])
        xbc = _dot(a16, win_ref[:, POOL_WIDTH + SSD_WIDTH:DT_OFF])
        dt_raw = _dot(a16, wdt_ref[...])
        conv = cb_ref[...] + sc_ref[0] * cw_ref[0:1, :]
        conv = conv + sc_ref[1] * cw_ref[1:2, :]
        conv = conv + sc_ref[2] * cw_ref[2:3, :]
        conv = conv + xbc * cw_ref[3:4, :]
        conv = _silu(conv)
        co_ref[0] = sc_ref[1]
        co_ref[1] = sc_ref[2]
        co_ref[2] = xbc
        xs = conv[:, 0:SSD_WIDTH]
        xs_keep[...] = xs
        b_t[...] = conv[:, SSD_WIDTH:SSD_WIDTH + SSD_BC].T
        c_t[...] = conv[:, SSD_WIDTH + SSD_BC:CONV_DIM].T
        dt = _softplus(dt_raw + dtb_ref[...])
        d_a = dt * (-jnp.exp(alog_ref[...]))
        dt_t = dt.T
        dec_t[...] = jnp.exp(d_a).T
        xs_t = xs.T
        for k in range(SSD_HEADS):
            blk = slice(k * SSD_HEAD_DIM, (k + 1) * SSD_HEAD_DIM)
            xdt_t[blk, :] = xs_t[blk, :] * dt_t[k:k + 1, :]
        v = vp_new
        for k in range(POOL_BUF - 1):
            po_ref[k] = sp_ref[k + 1]
        po_ref[POOL_BUF - 1] = v
        for g, w in enumerate(POOL_WINDOWS):
            blk = slice(g * POOL_GROUP_DIM, (g + 1) * POOL_GROUP_DIM)
            acc = sp_ref[POOL_BUF - (w - 1), :, blk]
            for k in range(w - 2, 0, -1):
                acc = acc + sp_ref[POOL_BUF - k, :, blk]
            acc = acc + v[:, blk]
            m = acc / float(min(PAST_LEN + 1, w)) - v[:, blk]
            yg = _dot(m.astype(BF16), plw_ref[g]) + plb_ref[:, blk]
            mix_ref[:, blk] = (yg * psc_ref[:, blk]).astype(BF16)

    g_off = pl.multiple_of((h // (SSD_HEADS // SSD_GROUPS)) * SSD_STATE, SSD_STATE)
    h_off = pl.multiple_of(h * SSD_HEAD_DIM, SSD_HEAD_DIM)
    b_g = b_t[pl.ds(g_off, SSD_STATE), :]
    c_g = c_t[pl.ds(g_off, SSD_STATE), :]
    dec = dec_t[pl.ds(h, 1), :]
    xdt = xdt_t[pl.ds(h_off, SSD_HEAD_DIM), :]
    y_rows = []
    for p in range(SSD_HEAD_DIM):
        s_new = st_ref[0, p] * dec + xdt[p:p + 1, :] * b_g
        so_ref[0, p] = s_new
        y_rows.append(jnp.sum(s_new * c_g, axis=0, keepdims=True))
    y_t[pl.ds(h_off, SSD_HEAD_DIM), :] = jnp.concatenate(y_rows, axis=0)

    @pl.when(h == pl.num_programs(0) - 1)
    def _():
        xs = xs_keep[...]
        y = y_t[...].T + dsk_ref[...] * xs
        y = y * _silu(z_ref[...])
        width = SSD_WIDTH // SSD_GROUPS
        for g in range(SSD_GROUPS):
            blk = slice(g * width, (g + 1) * width)
            yg = y[:, blk]
            rs = lax.rsqrt(jnp.mean(yg * yg, axis=-1, keepdims=True) + EPS)
            mix_ref[:, POOL_WIDTH + g * width:POOL_WIDTH + (g + 1) * width] = (yg * rs * nw_ref[:, blk]).astype(BF16)
        _out_proj_and_route(x_ref[...], mix_ref[...], wout_ref, gf_ref, rw_ref, rb_ref,
                            h_ref, u_ref, rt_ref, wt_ref, cnt_ref)


def _front_step(x, g_mix, w_in16, w_dt16, pool_t, conv_t, ssm_t, cw, cb, dtb, alog, dsk, nw, plw16, plb, psc,
                w_out16, g_ffn, r_w16, r_bias):
    n = x.shape[0]
    fixed2 = lambda h: (0, 0)
    fixed3 = lambda h: (0, 0, 0)
    st_spec = pl.BlockSpec((1, SSD_HEAD_DIM, SSD_STATE, n), lambda h: (h, 0, 0, 0))
    return pl.pallas_call(
        _mix_step_body,
        grid=(SSD_HEADS,),
        in_specs=[pl.BlockSpec((n, D_MODEL), fixed2), pl.BlockSpec((1, D_MODEL), fixed2),
                  pl.BlockSpec((D_MODEL, IN_DIM), fixed2), pl.BlockSpec((D_MODEL, LANES), fixed2),
                  pl.BlockSpec((POOL_BUF, n, POOL_WIDTH), fixed3),
                  pl.BlockSpec((SSD_CONV - 1, n, CONV_DIM), fixed3),
                  st_spec,
                  pl.BlockSpec((SSD_CONV, CONV_DIM), fixed2), pl.BlockSpec((1, CONV_DIM), fixed2),
                  pl.BlockSpec((1, LANES), fixed2), pl.BlockSpec((1, LANES), fixed2),
                  pl.BlockSpec((1, SSD_WIDTH), fixed2), pl.BlockSpec((1, SSD_WIDTH), fixed2),
                  pl.BlockSpec((len(POOL_WINDOWS), POOL_GROUP_DIM, POOL_GROUP_DIM), fixed3),
                  pl.BlockSpec((1, POOL_WIDTH), fixed2), pl.BlockSpec((1, POOL_WIDTH), fixed2),
                  pl.BlockSpec((D_MODEL, D_MODEL), fixed2), pl.BlockSpec((1, D_MODEL), fixed2),
                  pl.BlockSpec((D_MODEL, 2 * LANES), fixed2),
                  pl.BlockSpec((LANES, n), fixed2)],
        out_specs=[pl.BlockSpec((n, D_MODEL), fixed2), pl.BlockSpec((n, D_MODEL), fixed2),
                   pl.BlockSpec((1, 8, n), fixed3), pl.BlockSpec((n, 2 * LANES), fixed2),
                   pl.BlockSpec((1, N_EXPERTS, LANES), fixed3),
                   pl.BlockSpec((POOL_BUF, n, POOL_WIDTH), fixed3),
                   pl.BlockSpec((SSD_CONV - 1, n, CONV_DIM), fixed3), st_spec],
        out_shape=[jax.ShapeDtypeStruct((n, D_MODEL), F32), jax.ShapeDtypeStruct((n, D_MODEL), BF16),
                   jax.ShapeDtypeStruct((1, 8, n), F32), jax.ShapeDtypeStruct((n, 2 * LANES), F32),
                   jax.ShapeDtypeStruct((1, N_EXPERTS, LANES), F32),
                   jax.ShapeDtypeStruct(pool_t.shape, F32),
                   jax.ShapeDtypeStruct(conv_t.shape, F32), jax.ShapeDtypeStruct(ssm_t.shape, F32)],
        scratch_shapes=[pltpu.VMEM((SSD_WIDTH, n), F32), pltpu.VMEM((LANES, n), F32),
                        pltpu.VMEM((SSD_BC, n), F32), pltpu.VMEM((SSD_BC, n), F32),
                        pltpu.VMEM((n, SSD_WIDTH), F32), pltpu.VMEM((SSD_WIDTH, n), F32),
                        pltpu.VMEM((n, SSD_WIDTH), F32), pltpu.VMEM((n, D_MODEL), BF16)],
        compiler_params=_cparams("arbitrary"),
        name="front_step",
    )(x, g_mix, w_in16, w_dt16, pool_t, conv_t, ssm_t, cw, cb, dtb, alog, dsk, nw, plw16, plb, psc,
      w_out16, g_ffn, r_w16, r_bias)


PROMPT_TILE = MIX_ROWS


def _sort_tables(counts, tile_tokens, data_rows):
    cnt = jnp.concatenate(counts, axis=0)
    pc = (cnt + RUN_PAD - 1) // RUN_PAD * RUN_PAD
    off_local = jnp.cumsum(pc, axis=1) - pc
    tile_rows = jnp.sum(pc, axis=1)
    region = (jnp.sum(pc, axis=0) + MOE_TILE - 1) // MOE_TILE * MOE_TILE
    base = jnp.cumsum(region) - region
    dst = base[None, :] + jnp.cumsum(pc, axis=0) - pc
    per_pass = []
    lo = 0
    for c, tm in zip(counts, tile_tokens):
        hi = lo + c.shape[0]
        n_chunk = _sorted_rows_per_tile(tm) // RUN_PAD
        s = jnp.arange(n_chunk, dtype=jnp.int32) * RUN_PAD
        begins = off_local[lo:hi, None, :]
        ends = begins + pc[lo:hi, None, :]
        inside = (begins <= s[None, :, None]) & (s[None, :, None] < ends)
        shift = jnp.sum(jnp.where(inside, dst[lo:hi, None, :] - begins, 0), axis=2)
        valid = s[None, :] < tile_rows[lo:hi, None]
        parity = (jnp.arange(hi - lo, dtype=jnp.int32) % 2)[:, None]
        spare = data_rows + parity * _sorted_rows_per_tile(max(tile_tokens)) + s[None, :]
        scatter_dst = (jnp.where(valid, s[None, :] + shift, spare) // RUN_PAD).astype(jnp.int32)
        gather_src = (jnp.where(valid, s[None, :] + shift, 0) // RUN_PAD).astype(jnp.int32)
        off_v = jnp.broadcast_to(off_local[lo:hi, :, None].astype(F32), (hi - lo, N_EXPERTS, LANES))
        per_pass.append((scatter_dst, gather_src, off_v))
        lo = hi
    tiles_cum = jnp.cumsum(region // MOE_TILE)
    n_active = tiles_cum[-1]
    return per_pass, tiles_cum, n_active


def _max_sorted_rows(tile_counts, tile_tokens):
    rows = sum(n * (2 * tm + N_EXPERTS * (RUN_PAD - 1)) for n, tm in zip(tile_counts, tile_tokens))
    data_rows = (-(-rows // MOE_TILE) + N_EXPERTS) * MOE_TILE
    return data_rows, data_rows + 2 * _sorted_rows_per_tile(max(tile_tokens))


def _row(v):
    return v.reshape(1, -1).astype(F32)


def _per_head_rows(v):
    return jnp.broadcast_to(v.astype(F32)[:, None], (SSD_HEADS, SSD_Q))


def _pad_lanes(v):
    return jnp.pad(v.reshape(1, -1).astype(F32), ((0, 0), (0, LANES - v.size)))


def kernel(x_prompt, x_sample, p_prompt, p_sample, state_pool, state_conv, state_ssm, norm_mix, w_in, pool_lin_w, pool_lin_b, pool_scale, conv_w, conv_b, dt_bias, a_log, d_skip, ssd_norm, w_out, norm_ffn, router_grp_w, router_grp_b, router_exp_w, router_exp_b, exp_w_gate, exp_w_up, exp_w_down, norm_ple, ple_gate_w, ple_gate_b, ple_proj_w, norm_final):
    nb, seq, _ = x_prompt.shape
    ns = x_sample.shape[0]
    assert ns == LANES and x_sample.shape[1] == 1 and seq % MIX_ROWS == 0 and seq >= POOL_BUF

    w_in16 = w_in[0].astype(BF16)
    w_dt16 = jnp.pad(w_in[0][:, DT_OFF:], ((0, 0), (0, LANES - SSD_HEADS))).astype(BF16)
    w_out16 = w_out[0].astype(BF16)
    g_mix, g_ffn, g_ple, g_fin = _row(norm_mix[0]), _row(norm_ffn[0]), _row(norm_ple[0]), _row(norm_final)
    cw, cb = conv_w[0].astype(F32), _row(conv_b[0])
    dtb, alog = _pad_lanes(dt_bias[0]), _pad_lanes(a_log[0])
    dsk = _row(jnp.repeat(d_skip[0], SSD_HEAD_DIM))
    nw = _row(ssd_norm[0])
    plw16 = pool_lin_w[0].astype(BF16)
    plb, psc = _row(pool_lin_b[0]), _row(pool_scale[0])
    zeros4 = jnp.zeros((D_MODEL, 8 - N_EXPERT_GROUPS), F32)
    r_w = jnp.concatenate([router_grp_w[0], zeros4, router_exp_w[0],
                           jnp.zeros((D_MODEL, LANES - 8 - N_EXPERTS), F32)], axis=1)
    r_hi = r_w.astype(BF16)
    r_w16 = jnp.concatenate([r_hi, (r_w - r_hi.astype(F32)).astype(BF16)], axis=1)
    r_b = jnp.concatenate([router_grp_b[0], jnp.zeros((8 - N_EXPERT_GROUPS,), F32), router_exp_b[0],
                           jnp.zeros((LANES - 8 - N_EXPERTS,), F32)])
    wg = exp_w_gate[0].reshape(N_EXPERTS, D_MODEL, EXPERT_FF)
    wu = exp_w_up[0].reshape(N_EXPERTS, D_MODEL, EXPERT_FF)
    wd = exp_w_down[0].reshape(N_EXPERTS, EXPERT_FF, D_MODEL)
    pg16 = ple_gate_w[0].astype(BF16)
    pgb = _row(ple_gate_b[0])
    pp16 = ple_proj_w[0].astype(BF16)

    xp = x_prompt.reshape(nb * seq, D_MODEL)
    h1_p, u_p, rt_p, gate_p, cnt_p, st, pool_tail, conv_tail = _front_prompt(
        xp, g_mix, w_in16, w_dt16, cw, cb, _per_head_rows(dt_bias[0]), _per_head_rows(a_log[0]), dsk, nw, plw16, plb, psc,
        w_out16, g_ffn, r_w16, jnp.broadcast_to(r_b[:, None], (LANES, MIX_ROWS)), nb, seq)
    pool_p = pool_tail[:, TAIL_ROWS - POOL_BUF:]
    conv_p = conv_tail[:, CONV_HDR - (SSD_CONV - 1):]
    ssm_p = st

    xs_ = x_sample.reshape(ns, D_MODEL)
    h1_s, u_s, rt_s, gate_s, cnt_s, pool_t, conv_t, ssm_t = _front_step(
        xs_, g_mix, w_in16, w_dt16, jnp.transpose(state_pool[0], (1, 0, 2)), jnp.transpose(state_conv[0], (1, 0, 2)),
        jnp.transpose(state_ssm[0], (1, 2, 3, 0)), cw, cb, dtb, alog, dsk, nw, plw16, plb, psc,
        w_out16, g_ffn, r_w16, jnp.broadcast_to(r_b[:, None], (LANES, ns)))
    pool_s = jnp.transpose(pool_t, (1, 0, 2))
    conv_s = jnp.transpose(conv_t, (1, 0, 2))
    ssm_s = jnp.transpose(ssm_t, (3, 0, 1, 2))

    counts = [cnt_p[:, :, 0].astype(jnp.int32), cnt_s[:, :, 0].astype(jnp.int32)]
    tiles = (PROMPT_TILE, ns)
    data_rows, total_rows = _max_sorted_rows([c.shape[0] for c in counts], tiles)
    (tab_p, tab_s), tiles_cum, n_active = _sort_tables(counts, tiles, data_rows)
    xs_sorted, dest_p = _sort_tokens(tab_p[0], u_p, rt_p, tab_p[2], None, PROMPT_TILE, total_rows)
    xs_sorted, dest_s = _sort_tokens(tab_s[0], u_s, rt_s, tab_s[2], xs_sorted, ns, total_rows)
    tile_start = jnp.concatenate([jnp.zeros((1,), jnp.int32), tiles_cum.astype(jnp.int32)])
    ys_sorted = _moe_sorted(tile_start, n_active.reshape(1).astype(jnp.int32),
                            xs_sorted.reshape(total_rows, D_MODEL), wg, wu, wd).reshape(xs_sorted.shape)

    y_prompt = _ple(tab_p[1], h1_p, dest_p, gate_p, p_prompt[0].reshape(nb * seq, PLE_DIM),
                    g_ple, pg16, pgb, pp16, g_fin, ys_sorted, PROMPT_TILE)
    y_sample = _ple(tab_s[1], h1_s, dest_s, gate_s, p_sample[0].reshape(ns, PLE_DIM),
                    g_ple, pg16, pgb, pp16, g_fin, ys_sorted, ns)

    return (y_prompt.reshape(nb, seq, D_MODEL), y_sample.reshape(ns, 1, D_MODEL),
            pool_p[None], conv_p[None], ssm_p[None], pool_s[None], conv_s[None], ssm_s[None])
```

```python
import functools

import jax
import jax.numpy as jnp
import numpy as np
from jax import lax
from jax.experimental import pallas as pl
from jax.experimental.pallas import tpu as pltpu

F32 = jnp.float32
BF16 = jnp.bfloat16

D_MODEL = 1024
POOL_WIDTH = 512
POOL_WINDOWS = (2, 4, 8, 16)
POOL_GROUP_DIM = 128
POOL_BUF = 15
SSD_WIDTH = 512
SSD_HEAD_DIM = 64
SSD_HEADS = 8
SSD_GROUPS = 2
SSD_STATE = 64
SSD_CONV = 4
SSD_BC = SSD_GROUPS * SSD_STATE
CONV_DIM = SSD_WIDTH + 2 * SSD_BC
N_EXPERT_GROUPS = 4
EXPERTS_PER_GROUP = 8
N_EXPERTS = N_EXPERT_GROUPS * EXPERTS_PER_GROUP
EXPERT_FF = 256
PLE_DIM = 256
PAST_LEN = 16384
EPS = 1e-6

LANES = 128
DT_OFF = POOL_WIDTH + SSD_WIDTH + CONV_DIM
IN_DIM = DT_OFF + SSD_HEADS
VMEM_LIMIT = 56 * 1024 * 1024


def _cparams(*sem):
    return pltpu.CompilerParams(dimension_semantics=sem, vmem_limit_bytes=VMEM_LIMIT)


def _rms(x, g):
    return x * lax.rsqrt(jnp.mean(x * x, axis=-1, keepdims=True) + EPS) * g


def _sigmoid(x):
    return 1.0 / (1.0 + jnp.exp(-x))


def _silu(x):
    return x * _sigmoid(x)


def _split3(v):
    hi = v.astype(BF16)
    r = v - hi.astype(F32)
    mid = r.astype(BF16)
    lo = (r - mid.astype(F32)).astype(BF16)
    return hi, mid, lo


def _dot(a, b):
    return jnp.dot(a, b, preferred_element_type=F32)


def _dot_nt(a, b):
    return lax.dot_general(a, b, (((1,), (1,)), ((), ())), preferred_element_type=F32)


def _dot_tn(a, b):
    return lax.dot_general(a, b, (((0,), (0,)), ((), ())), preferred_element_type=F32)


def _route(lg):
    tm = lg.shape[1]
    gl = lg[0:N_EXPERT_GROUPS, :]
    gmax = jnp.max(gl, axis=0, keepdims=True)
    gsum = jnp.sum(jnp.exp(gl - gmax), axis=0, keepdims=True)
    g_w = 1.0 / gsum
    gi = lax.broadcasted_iota(jnp.int32, gl.shape, 0)
    g_idx = jnp.min(jnp.where(gl == gmax, gi, N_EXPERT_GROUPS), axis=0, keepdims=True)
    sel = jnp.zeros((EXPERTS_PER_GROUP, tm), F32)
    for g in range(N_EXPERT_GROUPS):
        blk = lg[8 + g * EXPERTS_PER_GROUP:8 + (g + 1) * EXPERTS_PER_GROUP, :]
        sel = jnp.where(g_idx == g, blk, sel)
    ei = lax.broadcasted_iota(jnp.int32, sel.shape, 0)
    m1 = jnp.max(sel, axis=0, keepdims=True)
    i1 = jnp.min(jnp.where(sel == m1, ei, EXPERTS_PER_GROUP), axis=0, keepdims=True)
    rest = jnp.where(ei == i1, -jnp.inf, sel)
    m2 = jnp.max(rest, axis=0, keepdims=True)
    i2 = jnp.min(jnp.where(rest == m2, ei, EXPERTS_PER_GROUP), axis=0, keepdims=True)
    p2 = jnp.exp(m2 - m1)
    w1 = g_w / (1.0 + p2)
    w2 = g_w * p2 / (1.0 + p2)
    return g_idx * EXPERTS_PER_GROUP + i1, g_idx * EXPERTS_PER_GROUP + i2, w1, w2


def _out_proj_and_route(x, mix16, w_ref, g_ref, rw_ref, rb_ref, h_ref, u_ref, rt_ref, wt_ref, cnt_ref,
                        r0=0, accumulate=False):
    tm = x.shape[0]
    h = x + _dot(mix16, w_ref[...])
    h_ref[r0:r0 + tm, :] = h
    _route_rows(h, g_ref, rw_ref, rb_ref, u_ref, rt_ref, wt_ref, cnt_ref, r0=r0, accumulate=accumulate)


def _route_rows(h, g_ref, rw_ref, rb_ref, u_ref, rt_ref, wt_ref, cnt_ref, r0=0, accumulate=False):
    tm = h.shape[0]
    u = _rms(h, g_ref[...])
    u_hi = u.astype(BF16)
    u_ref[r0:r0 + tm, :] = u_hi
    u_lo = (u - u_hi.astype(F32)).astype(BF16)
    both = _dot(u_hi, rw_ref[...])
    lg_rows = both[:, 0:LANES] + both[:, LANES:2 * LANES] + _dot(u_lo, rw_ref[:, 0:LANES])
    lg = lg_rows.T + rb_ref[:, r0:r0 + tm]
    b1, b2, w1, w2 = _route(lg)
    r8 = lax.broadcasted_iota(jnp.int32, (8, tm), 0)
    rt_ref[0, :, r0:r0 + tm] = jnp.where(r8 == 0, b1.astype(F32), jnp.where(r8 == 1, b2.astype(F32), 0.0))
    wt_ref[r0:r0 + tm, :] = jnp.concatenate(
        [jnp.broadcast_to(w1, (LANES, tm)).T, jnp.broadcast_to(w2, (LANES, tm)).T], axis=1)
    kio = lax.broadcasted_iota(jnp.int32, (N_EXPERTS, tm), 0)
    hits = ((kio == b1) | (kio == b2)).astype(F32)
    cnt = jnp.broadcast_to(jnp.sum(hits, axis=1, keepdims=True), (N_EXPERTS, LANES))
    cnt_ref[0] = cnt_ref[0] + cnt if accumulate else cnt


RUN_PAD = 16
MOE_TILE = 256
MOE_GROUP = 3
MOE_AHEAD = 8
MOE_BUFS = MOE_AHEAD + MOE_GROUP + 1


def _tile_lanes(v, width):
    reps = width // LANES
    return v if reps == 1 else jnp.concatenate([v] * reps, axis=1)


def _sorted_rows_per_tile(tm):
    need = 2 * tm + N_EXPERTS * (RUN_PAD - 1)
    return -(-need // LANES) * LANES


def _sort_body(cd_ref, u_ref, rt_ref, off_ref, rtn_ref, offn_ref, before_ref, *rest):
    xs_ref, dt_ref, loc, dsc, sem = rest[-5:]
    i = pl.program_id(0)
    tm = u_ref.shape[0]
    n_chunk = loc.shape[1]
    rows = n_chunk * RUN_PAD
    slot = i % 2

    def destinations(rt_r, off_r, sl):
        rt = rt_r[0]
        b1 = rt[0:1, :].astype(jnp.int32)
        b2 = rt[1:2, :].astype(jnp.int32)
        kio = lax.broadcasted_iota(jnp.int32, (N_EXPERTS, tm), 0)
        o1 = kio == b1
        o2 = kio == b2
        start = _dot((o1 | o2).astype(BF16), before_ref[...]) + _tile_lanes(off_r[0], tm)
        dsc[sl, 0:1, :] = jnp.sum(jnp.where(o1, start, 0.0), axis=0, keepdims=True)
        dsc[sl, 1:2, :] = jnp.sum(jnp.where(o2, start, 0.0), axis=0, keepdims=True)

    @pl.when(i == 0)
    def _():
        destinations(rt_ref, off_ref, slot)

    d1 = dsc[slot, 0:1, :]
    d2 = dsc[slot, 1:2, :]
    dt_ref[...] = jnp.concatenate([jnp.broadcast_to(d1, (LANES, tm)).T, jnp.broadcast_to(d2, (LANES, tm)).T], axis=1)
    rio = lax.broadcasted_iota(jnp.int32, (rows, tm), 0)
    perm = ((rio == d1.astype(jnp.int32)) | (rio == d2.astype(jnp.int32))).astype(BF16)
    loc[slot] = _dot(perm, u_ref[...]).astype(BF16).reshape(n_chunk, RUN_PAD, D_MODEL)
    destinations(rtn_ref, offn_ref, 1 - slot)

    def chunk_copy(step, sl, c):
        return pltpu.make_async_copy(loc.at[sl, c], xs_ref.at[cd_ref[step, c]], sem.at[sl])

    for c in range(n_chunk):
        chunk_copy(i, slot, c).start(priority=c % 2)

    @pl.when(i > 0)
    def _():
        for c in range(n_chunk):
            chunk_copy(i - 1, 1 - slot, c).wait()

    @pl.when(i == pl.num_programs(0) - 1)
    def _():
        for c in range(n_chunk):
            chunk_copy(i, slot, c).wait()


def _sort_tokens(chunk_dst, u16, rt, off_v, xs_prev, tm, total_rows):
    t = u16.shape[0]
    rows = _sorted_rows_per_tile(tm)
    in_specs = [pl.BlockSpec((tm, D_MODEL), lambda i, cd: (i, 0)),
                pl.BlockSpec((1, 8, tm), lambda i, cd: (i, 0, 0)),
                pl.BlockSpec((1, N_EXPERTS, LANES), lambda i, cd: (i, 0, 0)),
                pl.BlockSpec((1, 8, tm), lambda i, cd: (jnp.minimum(i + 1, t // tm - 1), 0, 0)),
                pl.BlockSpec((1, N_EXPERTS, LANES), lambda i, cd: (jnp.minimum(i + 1, t // tm - 1), 0, 0)),
                pl.BlockSpec((tm, tm), lambda i, cd: (0, 0))]
    before = jnp.asarray(np.triu(np.ones((tm, tm), np.float32), 1)).astype(BF16)
    args = [chunk_dst, u16, rt, off_v, rt, off_v, before]
    aliases = {}
    if xs_prev is not None:
        in_specs.append(pl.BlockSpec(memory_space=pl.ANY))
        aliases = {len(args): 0}
        args.append(xs_prev)
    return pl.pallas_call(
        _sort_body,
        grid_spec=pltpu.PrefetchScalarGridSpec(
            num_scalar_prefetch=1,
            grid=(t // tm,),
            in_specs=in_specs,
            out_specs=[pl.BlockSpec(memory_space=pl.ANY),
                       pl.BlockSpec((tm, 2 * LANES), lambda i, cd: (i, 0))],
            scratch_shapes=[pltpu.VMEM((2, rows // RUN_PAD, RUN_PAD, D_MODEL), BF16),
                            pltpu.VMEM((2, 8, tm), F32), pltpu.SemaphoreType.DMA((2,))]),
        out_shape=[jax.ShapeDtypeStruct((total_rows // RUN_PAD, RUN_PAD, D_MODEL), BF16),
                   jax.ShapeDtypeStruct((t, 2 * LANES), F32)],
        input_output_aliases=aliases,
        compiler_params=_cparams("arbitrary"),
        name="sort_tokens",
    )(*args)


def _moe_body(ts_ref, na_ref, xs_ref, wg_ref, wu_ref, wd_ref, ys_ref, xbuf, ybuf, wg16, wu16, wd16, sem_in, sem_out):
    k = pl.program_id(0)
    n_act = na_ref[0]

    def x_copy(g, sl):
        return pltpu.make_async_copy(xs_ref.at[pl.ds(pl.multiple_of(g * MOE_TILE, MOE_TILE), MOE_TILE), :],
                                     xbuf.at[sl], sem_in.at[sl])

    def y_copy(g, sl):
        return pltpu.make_async_copy(ybuf.at[sl],
                                     ys_ref.at[pl.ds(pl.multiple_of(g * MOE_TILE, MOE_TILE), MOE_TILE), :],
                                     sem_out.at[sl])

    @pl.when(k == 0)
    def _():
        for j in range(MOE_AHEAD):
            @pl.when(j < n_act)
            def _():
                x_copy(j, j).start(priority=1)

    wg16[...] = wg_ref[0].astype(BF16)
    wu16[...] = wu_ref[0].astype(BF16)
    wd16[...] = wd_ref[0].astype(BF16)

    def arrive(g):
        sl = g % MOE_BUFS

        @pl.when(g + MOE_AHEAD < n_act)
        def _():
            x_copy(g + MOE_AHEAD, (g + MOE_AHEAD) % MOE_BUFS).start(priority=1)

        x_copy(g, sl).wait()

        @pl.when(g >= MOE_BUFS)
        def _():
            y_copy(g - MOE_BUFS, sl).wait()

    def tiles(g, count):
        for j in range(count):
            arrive(g + j)
        acts = []
        for j in range(count):
            x = xbuf[(g + j) % MOE_BUFS]
            acts.append((_silu(_dot(x, wg16[...])) * _dot(x, wu16[...])).astype(BF16))
        for j in range(count):
            ybuf[(g + j) % MOE_BUFS] = _dot(acts[j], wd16[...]).astype(BF16)
        for j in range(count):
            y_copy(g + j, (g + j) % MOE_BUFS).start()

    g0 = ts_ref[k]
    n_here = ts_ref[k + 1] - g0

    def group(p, carry):
        tiles(g0 + MOE_GROUP * p, MOE_GROUP)
        return carry

    n_groups = n_here // MOE_GROUP
    lax.fori_loop(0, n_groups, group, 0)
    for rest in range(1, MOE_GROUP):
        @pl.when(n_here - n_groups * MOE_GROUP == rest)
        def _():
            tiles(g0 + n_groups * MOE_GROUP, rest)

    @pl.when(k == pl.num_programs(0) - 1)
    def _():
        for j in range(1, MOE_BUFS + 1):
            @pl.when(n_act >= j)
            def _():
                y_copy(n_act - j, (n_act - j) % MOE_BUFS).wait()


def _moe_sorted(tile_start, n_active, xs, wg, wu, wd):
    w_map = lambda k, ts, na: (k, 0, 0)
    return pl.pallas_call(
        _moe_body,
        grid_spec=pltpu.PrefetchScalarGridSpec(
            num_scalar_prefetch=2,
            grid=(N_EXPERTS,),
            in_specs=[pl.BlockSpec(memory_space=pl.ANY),
                      pl.BlockSpec((1, D_MODEL, EXPERT_FF), w_map),
                      pl.BlockSpec((1, D_MODEL, EXPERT_FF), w_map),
                      pl.BlockSpec((1, EXPERT_FF, D_MODEL), w_map)],
            out_specs=pl.BlockSpec(memory_space=pl.ANY),
            scratch_shapes=[pltpu.VMEM((MOE_BUFS, MOE_TILE, D_MODEL), BF16),
                            pltpu.VMEM((MOE_BUFS, MOE_TILE, D_MODEL), BF16),
                            pltpu.VMEM((D_MODEL, EXPERT_FF), BF16), pltpu.VMEM((D_MODEL, EXPERT_FF), BF16),
                            pltpu.VMEM((EXPERT_FF, D_MODEL), BF16),
                            pltpu.SemaphoreType.DMA((MOE_BUFS,)), pltpu.SemaphoreType.DMA((MOE_BUFS,))]),
        out_shape=jax.ShapeDtypeStruct(xs.shape, BF16),
        compiler_params=_cparams("arbitrary"),
        name="moe_sorted",
    )(tile_start, n_active, xs, wg, wu, wd)


def _ple_body(cd_ref, h_ref, dt_ref, wt_ref, p_ref, gn_ref, wg_ref, bg_ref, wp_ref, fn_ref, ys_ref,
              o_ref, loc, sem):
    i = pl.program_id(0)
    tm = h_ref.shape[0]
    n_chunk = loc.shape[1]
    rows = n_chunk * RUN_PAD
    slot = i % 2

    def chunk_copy(step, sl, c):
        return pltpu.make_async_copy(ys_ref.at[cd_ref[step, c]], loc.at[sl, c], sem.at[sl])

    def fetch(step, sl):
        for c in range(n_chunk):
            chunk_copy(step, sl, c).start(priority=c % 2)

    @pl.when(i == 0)
    def _():
        fetch(0, 0)

    @pl.when(i + 1 < pl.num_programs(0))
    def _():
        fetch(i + 1, 1 - slot)

    for c in range(n_chunk):
        chunk_copy(i, slot, c).wait()

    ci = lax.broadcasted_iota(jnp.int32, (tm, LANES), 1)
    d1 = dt_ref[:, 0:LANES].astype(jnp.int32)
    d2 = dt_ref[:, LANES:2 * LANES].astype(jnp.int32)
    w1 = wt_ref[:, 0:LANES]
    w2 = wt_ref[:, LANES:2 * LANES]
    sel = jnp.concatenate(
        [jnp.where(d1 == ci + m * LANES, w1, jnp.where(d2 == ci + m * LANES, w2, 0.0)) for m in range(rows // LANES)],
        axis=1).astype(BF16)
    h = h_ref[...] + _dot(sel, loc[slot].reshape(rows, D_MODEL))
    a16 = _rms(h, gn_ref[...]).astype(BF16)
    gate = _sigmoid(_dot(a16, wg_ref[...]) + bg_ref[...])
    pp = _dot(p_ref[...].astype(BF16), wp_ref[...])
    h = h + gate * pp
    o_ref[...] = _rms(h, fn_ref[...])


def _ple(chunk_src, h1, dest_t, gate_t, p, g_ple, wg16, bg, wp16, g_final, ys, tm):
    t = h1.shape[0]
    row = lambda i, cd: (i, 0)
    fixed = lambda i, cd: (0, 0)
    return pl.pallas_call(
        _ple_body,
        grid_spec=pltpu.PrefetchScalarGridSpec(
            num_scalar_prefetch=1,
            grid=(t // tm,),
            in_specs=[pl.BlockSpec((tm, D_MODEL), row), pl.BlockSpec((tm, 2 * LANES), row),
                      pl.BlockSpec((tm, 2 * LANES), row), pl.BlockSpec((tm, PLE_DIM), row),
                      pl.BlockSpec((1, D_MODEL), fixed), pl.BlockSpec((D_MODEL, D_MODEL), fixed),
                      pl.BlockSpec((1, D_MODEL), fixed), pl.BlockSpec((PLE_DIM, D_MODEL), fixed),
                      pl.BlockSpec((1, D_MODEL), fixed), pl.BlockSpec(memory_space=pl.ANY)],
            out_specs=pl.BlockSpec((tm, D_MODEL), row),
            scratch_shapes=[pltpu.VMEM((2, _sorted_rows_per_tile(tm) // RUN_PAD, RUN_PAD, D_MODEL), BF16),
                            pltpu.SemaphoreType.DMA((2,))]),
        out_shape=jax.ShapeDtypeStruct((t, D_MODEL), F32),
        compiler_params=_cparams("arbitrary"),
        name="ple_final",
    )(chunk_src, h1, dest_t, gate_t, p, g_ple, wg16, bg, wp16, g_final, ys)


MIX_ROWS = 512
SSD_Q = 128
POOL_HDR = 128
CONV_HDR = 8
TAIL_ROWS = 16


def _softplus(x):
    return jnp.maximum(x, 0.0) + jnp.log1p(jnp.exp(-jnp.abs(x)))


def _stack_terms(v, terms):
    parts = [t.astype(F32) for t in _split3(v)[:terms]]
    parts.append(jnp.zeros(((4 - terms) * v.shape[0], v.shape[1]), F32))
    return jnp.concatenate(parts, axis=0).astype(BF16)


def _cumsum_lanes(v, tri16):
    r = _dot_nt(_stack_terms(v, 3), tri16)
    n = v.shape[0]
    return r[0:n] + r[n:2 * n] + r[2 * n:3 * n]


def _expand_heads(v, expand32, terms):
    return _dot_tn(_stack_terms(v, terms), expand32)


def _mixer_constants():
    q = SSD_Q
    li = np.arange(q)[:, None]
    tri = (li >= np.arange(q)[None, :])
    expand = (np.arange(SSD_WIDTH)[None, :] // SSD_HEAD_DIM) == (np.arange(4 * SSD_HEADS)[:, None] % SSD_HEADS)
    hpg = SSD_HEADS // SSD_GROUPS
    block = (np.arange(SSD_BC)[:, None] // SSD_STATE) == (np.arange(SSD_WIDTH)[None, :] // (SSD_HEAD_DIM * hpg))
    wj = np.arange(2 * q)[None, :]
    band = np.stack([(wj <= li + POOL_HDR) & (wj > li + POOL_HDR - w) for w in POOL_WINDOWS])
    inv_count = np.stack([np.broadcast_to(1.0 / np.minimum(li + 1, w), (q, LANES)) for w in POOL_WINDOWS])
    as_f32 = lambda m: jnp.asarray(m.astype(np.float32))
    return (as_f32(tri), as_f32(tri).astype(BF16), as_f32(expand).astype(BF16), as_f32(block),
            as_f32(band).astype(BF16), as_f32(inv_count))


def _front_body(steps_per_seq, x_ref, xp_ref, gm_ref, win_ref, wdt_ref, trif_ref, tri_ref, exp_ref, bmask_ref, band_ref,
                invc_ref, cw_ref, cb_ref, dtb_ref, alog_ref, dsk_ref, nw_ref,
                plw_ref, plb_ref, psc_ref, wout_ref, gf_ref, rw_ref, rb_ref,
                h_ref, u_ref, rt_ref, wt_ref, cnt_ref, st_ref, ptail_ref, ctail_ref,
                pool_ext, conv_ext, s_ref, z_ref, dt_ref, mix_ref, mixp_ref, a_ref):
    s = pl.program_id(0)
    n_tiles = pl.num_programs(0) - 1
    real = s < n_tiles
    c = jnp.minimum(s, n_tiles - 1) % steps_per_seq
    rows = MIX_ROWS
    q_len = SSD_Q

    @pl.when(s == 0)
    def _():
        mixp_ref[...] = jnp.zeros_like(mixp_ref)

    @pl.when(c == 0)
    def _():
        pool_ext[0:POOL_HDR, :] = jnp.zeros((POOL_HDR, POOL_WIDTH), F32)
        conv_ext[0:CONV_HDR, :] = jnp.zeros((CONV_HDR, CONV_DIM), F32)
        s_ref[...] = jnp.zeros_like(s_ref)

    @pl.when(c > 0)
    def _():
        pool_ext[0:POOL_HDR, :] = pool_ext[rows:rows + POOL_HDR, :]
        conv_ext[0:CONV_HDR, :] = conv_ext[rows:rows + CONV_HDR, :]

    causal = trif_ref[...] > 0.5
    tri = tri_ref[...]
    expand = exp_ref[...]
    blockmask = bmask_ref[...]
    lane = lax.broadcasted_iota(jnp.int32, (q_len, LANES), 1)
    left = lane < SSD_HEAD_DIM
    a_neg = -jnp.exp(alog_ref[...])

    half = rows // 2
    n_cols = 256

    def norm_rows():
        for r0 in (0, half):
            a_ref[r0:r0 + half, :] = _rms(x_ref[r0:r0 + half, :], gm_ref[...]).astype(BF16)

    def in_proj_chunk(ref, row, k, col0):
        def run():
            ref[row:row + rows, k:k + n_cols] = _dot(a_ref[...], win_ref[:, col0 + k:col0 + k + n_cols])
        return run

    def dt_chunk():
        dt_ref[...] = _dot(a_ref[...], wdt_ref[...])

    conv_chunks = [in_proj_chunk(conv_ext, CONV_HDR, k, POOL_WIDTH + SSD_WIDTH) for k in range(0, CONV_DIM, n_cols)]
    z_chunks = [in_proj_chunk(z_ref, 0, k, POOL_WIDTH) for k in range(0, SSD_WIDTH, n_cols)]
    pool_chunks = [in_proj_chunk(pool_ext, POOL_HDR, k, 0) for k in range(0, POOL_WIDTH, n_cols)]

    def out_proj_chunk(k):
        def run():
            h_ref[:, k:k + n_cols] = xp_ref[:, k:k + n_cols] + _dot(mixp_ref[...], wout_ref[:, k:k + n_cols])
        return run

    def route(r0):
        def run():
            _route_rows(h_ref[r0:r0 + half, :], gf_ref, rw_ref, rb_ref, u_ref, rt_ref, wt_ref, cnt_ref,
                        r0=r0, accumulate=r0 > 0)
        return run

    out_chunks = [out_proj_chunk(k) for k in range(0, D_MODEL, n_cols)]
    routes = [route(0), route(half)]

    def mix_pair(q_first, fillers):
        fillers = list(fillers)

        def fill():
            if fillers:
                fillers.pop(0)()

        subs = range(q_first, q_first + 2)
        xs_l, b16_l, call_l, c16_l = {}, {}, {}, {}
        for q in subs:
            base = CONV_HDR + q * q_len
            conv = cb_ref[...] + conv_ext[base - 3:base - 3 + q_len, :] * cw_ref[0:1, :]
            conv = conv + conv_ext[base - 2:base - 2 + q_len, :] * cw_ref[1:2, :]
            conv = conv + conv_ext[base - 1:base - 1 + q_len, :] * cw_ref[2:3, :]
            conv = conv + conv_ext[base:base + q_len, :] * cw_ref[3:4, :]
            conv = _silu(conv)
            xs_l[q] = conv[:, 0:SSD_WIDTH]
            b16_l[q] = conv[:, SSD_WIDTH:SSD_WIDTH + SSD_BC].astype(BF16)
            call_l[q] = conv[:, SSD_WIDTH + SSD_BC:CONV_DIM]
            c16_l[q] = call_l[q].astype(BF16)
            fill()
        dtt_l = {q: _softplus(dt_ref[q * q_len:(q + 1) * q_len, :].T[0:SSD_HEADS, :] + dtb_ref[...]) for q in subs}
        acst_l = {q: _cumsum_lanes(dtt_l[q] * a_neg, tri) for q in subs}
        acs_l = {q: jnp.concatenate([acst_l[q], jnp.zeros((LANES - SSD_HEADS, q_len), F32)], axis=0).T
                 for q in subs}
        fill()
        dtx_l = {q: _expand_heads(dtt_l[q], expand, 2) for q in subs}
        acsx_l = {q: _expand_heads(acst_l[q], expand, 3) for q in subs}
        lastx_l = {q: acsx_l[q][q_len - 1:q_len, :] for q in subs}
        xdt_l = {q: xs_l[q] * dtx_l[q] for q in subs}
        xdt16_l = {q: xdt_l[q].astype(BF16) for q in subs}
        fill()
        contrib_l = {q: _dot_tn(b16_l[q], (xdt_l[q] * jnp.exp(lastx_l[q] - acsx_l[q])).astype(BF16)) * blockmask
                     for q in subs}
        cb_l = {q: [_dot_nt(jnp.where(lax.shift_right_logical(lane, 6) == g, call_l[q], 0.0).astype(BF16), b16_l[q])
                    for g in range(SSD_GROUPS)] for q in subs}
        fill()
        zs_l = {}
        for q in subs:
            zs = []
            for h in range(SSD_HEADS):
                blk = slice((h // 2) * LANES, (h // 2 + 1) * LANES)
                seg = acs_l[q][:, h:h + 1] - acst_l[q][h:h + 1, :]
                decay = jnp.where(causal, jnp.exp(seg), 0.0)
                scores = (cb_l[q][h // (SSD_HEADS // SSD_GROUPS)] * decay).astype(BF16)
                zs.append(_dot(scores, xdt16_l[q][:, blk]))
                if h % 4 == 3:
                    fill()
            zs_l[q] = zs
        yoff_l = {}
        for q in subs:
            s_old = s_ref[...]
            yoff_l[q] = _dot(c16_l[q], s_old.astype(BF16)) * jnp.exp(acsx_l[q])
            s_ref[...] = s_old * jnp.exp(lastx_l[q]) + contrib_l[q]
        fill()
        for q in subs:
            r0 = q * q_len
            z = z_ref[r0:r0 + q_len, :]
            y_blocks = []
            for j in range(SSD_HEADS // 2):
                blk = slice(j * LANES, (j + 1) * LANES)
                y = jnp.where(left, zs_l[q][2 * j], zs_l[q][2 * j + 1]) + yoff_l[q][:, blk]
                y = y + dsk_ref[:, blk] * xs_l[q][:, blk]
                y_blocks.append(y * _silu(z[:, blk]))
            for g in range(SSD_GROUPS):
                y0, y1 = y_blocks[2 * g], y_blocks[2 * g + 1]
                ss = jnp.sum(y0 * y0, axis=-1, keepdims=True) + jnp.sum(y1 * y1, axis=-1, keepdims=True)
                rs = lax.rsqrt(ss * (1.0 / (2 * LANES)) + EPS)
                for k, yk in ((2 * g, y0), (2 * g + 1, y1)):
                    blk = slice(k * LANES, (k + 1) * LANES)
                    out = yk * rs * nw_ref[:, blk]
                    mix_ref[r0:r0 + q_len, POOL_WIDTH + k * LANES:POOL_WIDTH + (k + 1) * LANES] = out.astype(BF16)
            fill()
        for q in subs:
            r0 = q * q_len
            for g, w in enumerate(POOL_WINDOWS):
                blk = slice(g * POOL_GROUP_DIM, (g + 1) * POOL_GROUP_DIM)
                pe = pool_ext[r0:r0 + 2 * q_len, blk]
                hi = pe.astype(BF16)
                lo = (pe - hi.astype(F32)).astype(BF16)
                both = _dot(band_ref[g], jnp.concatenate([hi, lo], axis=1))
                winsum = both[:, 0:POOL_GROUP_DIM] + both[:, POOL_GROUP_DIM:2 * POOL_GROUP_DIM]
                inv = jnp.where(c == 0, invc_ref[g], 1.0 / w) if q == 0 else 1.0 / w
                m = winsum * inv - pe[q_len:2 * q_len, :]
                yg = _dot(m.astype(BF16), plw_ref[g]) + plb_ref[:, blk]
                mix_ref[r0:r0 + q_len, blk] = (yg * psc_ref[:, blk]).astype(BF16)
            fill()
        while fillers:
            fill()

    @pl.when(real)
    def _():
        norm_rows()
        for run in conv_chunks + [dt_chunk]:
            run()
        mix_pair(0, z_chunks + pool_chunks + out_chunks[:2])
        mix_pair(2, out_chunks[2:] + routes)
        mixp_ref[...] = mix_ref[...]

    @pl.when(jnp.logical_not(real))
    def _():
        for run in out_chunks + routes:
            run()

    @pl.when(real & (c == steps_per_seq - 1))
    def _():
        hpg = SSD_HEADS // SSD_GROUPS
        for j in range(SSD_HEADS // 2):
            flipped = s_ref[:, j * LANES:(j + 1) * LANES].T
            g = (2 * j) // hpg
            for k in range(2):
                st_ref[0, 2 * j + k] = flipped[k * SSD_HEAD_DIM:(k + 1) * SSD_HEAD_DIM,
                                               g * SSD_STATE:(g + 1) * SSD_STATE]
        ptail_ref[0] = pool_ext[POOL_HDR + rows - TAIL_ROWS:POOL_HDR + rows, :]
        ctail_ref[0] = conv_ext[rows:rows + CONV_HDR, :]


def _front_prompt(x, g_mix, w_in16, w_dt16, cw, cb, dtb, alog, dsk, nw, plw16, plb, psc, w_out16, g_ffn, r_w16, r_bias,
                  nb, seq):
    steps = seq // MIX_ROWS
    nt = nb * steps
    cur = lambda s: (jnp.minimum(s, nt - 1), 0)
    prev = lambda s: (jnp.maximum(s - 1, 0), 0)
    prev3 = lambda s: (jnp.maximum(s - 1, 0), 0, 0)
    per_seq = lambda s: (jnp.minimum(s, nt - 1) // steps, 0, 0)
    fixed2 = lambda s: (0, 0)
    fixed3 = lambda s: (0, 0, 0)
    return pl.pallas_call(
        functools.partial(_front_body, steps),
        grid=(nt + 1,),
        in_specs=[pl.BlockSpec((MIX_ROWS, D_MODEL), cur), pl.BlockSpec((MIX_ROWS, D_MODEL), prev),
                  pl.BlockSpec((1, D_MODEL), fixed2), pl.BlockSpec((D_MODEL, IN_DIM), fixed2),
                  pl.BlockSpec((D_MODEL, LANES), fixed2),
                  pl.BlockSpec((SSD_Q, SSD_Q), fixed2), pl.BlockSpec((SSD_Q, SSD_Q), fixed2),
                  pl.BlockSpec((4 * SSD_HEADS, SSD_WIDTH), fixed2), pl.BlockSpec((SSD_BC, SSD_WIDTH), fixed2),
                  pl.BlockSpec((len(POOL_WINDOWS), SSD_Q, 2 * SSD_Q), fixed3),
                  pl.BlockSpec((len(POOL_WINDOWS), SSD_Q, LANES), fixed3),
                  pl.BlockSpec((SSD_CONV, CONV_DIM), fixed2), pl.BlockSpec((1, CONV_DIM), fixed2),
                  pl.BlockSpec((SSD_HEADS, SSD_Q), fixed2), pl.BlockSpec((SSD_HEADS, SSD_Q), fixed2),
                  pl.BlockSpec((1, SSD_WIDTH), fixed2), pl.BlockSpec((1, SSD_WIDTH), fixed2),
                  pl.BlockSpec((len(POOL_WINDOWS), POOL_GROUP_DIM, POOL_GROUP_DIM), fixed3),
                  pl.BlockSpec((1, POOL_WIDTH), fixed2), pl.BlockSpec((1, POOL_WIDTH), fixed2),
                  pl.BlockSpec((D_MODEL, D_MODEL), fixed2), pl.BlockSpec((1, D_MODEL), fixed2),
                  pl.BlockSpec((D_MODEL, 2 * LANES), fixed2),
                  pl.BlockSpec((LANES, MIX_ROWS), fixed2)],
        out_specs=[pl.BlockSpec((MIX_ROWS, D_MODEL), prev), pl.BlockSpec((MIX_ROWS, D_MODEL), prev),
                   pl.BlockSpec((1, 8, MIX_ROWS), prev3), pl.BlockSpec((MIX_ROWS, 2 * LANES), prev),
                   pl.BlockSpec((1, N_EXPERTS, LANES), prev3),
                   pl.BlockSpec((1, SSD_HEADS, SSD_HEAD_DIM, SSD_STATE),
                                lambda s: (jnp.minimum(s, nt - 1) // steps, 0, 0, 0)),
                   pl.BlockSpec((1, TAIL_ROWS, POOL_WIDTH), per_seq),
                   pl.BlockSpec((1, CONV_HDR, CONV_DIM), per_seq)],
        out_shape=[jax.ShapeDtypeStruct((nb * seq, D_MODEL), F32), jax.ShapeDtypeStruct((nb * seq, D_MODEL), BF16),
                   jax.ShapeDtypeStruct((nt, 8, MIX_ROWS), F32), jax.ShapeDtypeStruct((nb * seq, 2 * LANES), F32),
                   jax.ShapeDtypeStruct((nt, N_EXPERTS, LANES), F32),
                   jax.ShapeDtypeStruct((nb, SSD_HEADS, SSD_HEAD_DIM, SSD_STATE), F32),
                   jax.ShapeDtypeStruct((nb, TAIL_ROWS, POOL_WIDTH), F32),
                   jax.ShapeDtypeStruct((nb, CONV_HDR, CONV_DIM), F32)],
        scratch_shapes=[pltpu.VMEM((POOL_HDR + MIX_ROWS, POOL_WIDTH), F32),
                        pltpu.VMEM((CONV_HDR + MIX_ROWS, CONV_DIM), F32),
                        pltpu.VMEM((SSD_BC, SSD_WIDTH), F32),
                        pltpu.VMEM((MIX_ROWS, SSD_WIDTH), F32), pltpu.VMEM((MIX_ROWS, LANES), F32),
                        pltpu.VMEM((MIX_ROWS, D_MODEL), BF16), pltpu.VMEM((MIX_ROWS, D_MODEL), BF16),
                        pltpu.VMEM((MIX_ROWS, D_MODEL), BF16)],
        compiler_params=_cparams("arbitrary"),
        name="front_prompt",
    )(x, x, g_mix, w_in16, w_dt16, *_mixer_constants(), cw, cb, dtb, alog, dsk, nw, plw16, plb, psc, w_out16, g_ffn,
      r_w16, r_bias)


def _mix_step_body(x_ref, gm_ref, win_ref, wdt_ref, sp_ref, sc_ref, st_ref, cw_ref, cb_ref, dtb_ref, alog_ref,
                   dsk_ref, nw_ref, plw_ref, plb_ref, psc_ref, wout_ref, gf_ref, rw_ref, rb_ref,
                   h_ref, u_ref, rt_ref, wt_ref, cnt_ref, po_ref, co_ref, so_ref,
                   xdt_t, dec_t, b_t, c_t, xs_keep, y_t, z_ref, mix_ref):
    h = pl.program_id(0)

    @pl.when(h == 0)
    def _():
        a16 = _rms(x_ref[...], gm_ref[...]).astype(BF16)
        vp_new = _dot(a16, win_ref[:, 0:POOL_WIDTH])
        z_ref[...] = _dot(a16, win_ref[:, POOL_WIDTH:POOL_WIDTH + SSD_WIDTH])
        xbc = _dot(a16, win_ref[:, POOL_WIDTH + SSD_WIDTH:DT_OFF])
        dt_raw = _dot(a16, wdt_ref[...])
        conv = cb_ref[...] + sc_ref[0] * cw_ref[0:1, :]
        conv = conv + sc_ref[1] * cw_ref[1:2, :]
        conv = conv + sc_ref[2] * cw_ref[2:3, :]
        conv = conv + xbc * cw_ref[3:4, :]
        conv = _silu(conv)
        co_ref[0] = sc_ref[1]
        co_ref[1] = sc_ref[2]
        co_ref[2] = xbc
        xs = conv[:, 0:SSD_WIDTH]
        xs_keep[...] = xs
        b_t[...] = conv[:, SSD_WIDTH:SSD_WIDTH + SSD_BC].T
        c_t[...] = conv[:, SSD_WIDTH + SSD_BC:CONV_DIM].T
        dt = _softplus(dt_raw + dtb_ref[...])
        d_a = dt * (-jnp.exp(alog_ref[...]))
        dt_t = dt.T
        dec_t[...] = jnp.exp(d_a).T
        xs_t = xs.T
        for k in range(SSD_HEADS):
            blk = slice(k * SSD_HEAD_DIM, (k + 1) * SSD_HEAD_DIM)
            xdt_t[blk, :] = xs_t[blk, :] * dt_t[k:k + 1, :]
        v = vp_new
        for k in range(POOL_BUF - 1):
            po_ref[k] = sp_ref[k + 1]
        po_ref[POOL_BUF - 1] = v
        for g, w in enumerate(POOL_WINDOWS):
            blk = slice(g * POOL_GROUP_DIM, (g + 1) * POOL_GROUP_DIM)
            acc = sp_ref[POOL_BUF - (w - 1), :, blk]
            for k in range(w - 2, 0, -1):
                acc = acc + sp_ref[POOL_BUF - k, :, blk]
            acc = acc + v[:, blk]
            m = acc / float(min(PAST_LEN + 1, w)) - v[:, blk]
            yg = _dot(m.astype(BF16), plw_ref[g]) + plb_ref[:, blk]
            mix_ref[:, blk] = (yg * psc_ref[:, blk]).astype(BF16)

    g_off = pl.multiple_of((h // (SSD_HEADS // SSD_GROUPS)) * SSD_STATE, SSD_STATE)
    h_off = pl.multiple_of(h * SSD_HEAD_DIM, SSD_HEAD_DIM)
    b_g = b_t[pl.ds(g_off, SSD_STATE), :]
    c_g = c_t[pl.ds(g_off, SSD_STATE), :]
    dec = dec_t[pl.ds(h, 1), :]
    xdt = xdt_t[pl.ds(h_off, SSD_HEAD_DIM), :]
    y_rows = []
    for p in range(SSD_HEAD_DIM):
        s_new = st_ref[0, p] * dec + xdt[p:p + 1, :] * b_g
        so_ref[0, p] = s_new
        y_rows.append(jnp.sum(s_new * c_g, axis=0, keepdims=True))
    y_t[pl.ds(h_off, SSD_HEAD_DIM), :] = jnp.concatenate(y_rows, axis=0)

    @pl.when(h == pl.num_programs(0) - 1)
    def _():
        xs = xs_keep[...]
        y = y_t[...].T + dsk_ref[...] * xs
        y = y * _silu(z_ref[...])
        width = SSD_WIDTH // SSD_GROUPS
        for g in range(SSD_GROUPS):
            blk = slice(g * width, (g + 1) * width)
            yg = y[:, blk]
            rs = lax.rsqrt(jnp.mean(yg * yg, axis=-1, keepdims=True) + EPS)
            mix_ref[:, POOL_WIDTH + g * width:POOL_WIDTH + (g + 1) * width] = (yg * rs * nw_ref[:, blk]).astype(BF16)
        _out_proj_and_route(x_ref[...], mix_ref[...], wout_ref, gf_ref, rw_ref, rb_ref,
                            h_ref, u_ref, rt_ref, wt_ref, cnt_ref)


def _front_step(x, g_mix, w_in16, w_dt16, pool_t, conv_t, ssm_t, cw, cb, dtb, alog, dsk, nw, plw16, plb, psc,
                w_out16, g_ffn, r_w16, r_bias):
    n = x.shape[0]
    fixed2 = lambda h: (0, 0)
    fixed3 = lambda h: (0, 0, 0)
    st_spec = pl.BlockSpec((1, SSD_HEAD_DIM, SSD_STATE, n), lambda h: (h, 0, 0, 0))
    return pl.pallas_call(
        _mix_step_body,
        grid=(SSD_HEADS,),
        in_specs=[pl.BlockSpec((n, D_MODEL), fixed2), pl.BlockSpec((1, D_MODEL), fixed2),
                  pl.BlockSpec((D_MODEL, IN_DIM), fixed2), pl.BlockSpec((D_MODEL, LANES), fixed2),
                  pl.BlockSpec((POOL_BUF, n, POOL_WIDTH), fixed3),
                  pl.BlockSpec((SSD_CONV - 1, n, CONV_DIM), fixed3),
                  st_spec,
                  pl.BlockSpec((SSD_CONV, CONV_DIM), fixed2), pl.BlockSpec((1, CONV_DIM), fixed2),
                  pl.BlockSpec((1, LANES), fixed2), pl.BlockSpec((1, LANES), fixed2),
                  pl.BlockSpec((1, SSD_WIDTH), fixed2), pl.BlockSpec((1, SSD_WIDTH), fixed2),
                  pl.BlockSpec((len(POOL_WINDOWS), POOL_GROUP_DIM, POOL_GROUP_DIM), fixed3),
                  pl.BlockSpec((1, POOL_WIDTH), fixed2), pl.BlockSpec((1, POOL_WIDTH), fixed2),
                  pl.BlockSpec((D_MODEL, D_MODEL), fixed2), pl.BlockSpec((1, D_MODEL), fixed2),
                  pl.BlockSpec((D_MODEL, 2 * LANES), fixed2),
                  pl.BlockSpec((LANES, n), fixed2)],
        out_specs=[pl.BlockSpec((n, D_MODEL), fixed2), pl.BlockSpec((n, D_MODEL), fixed2),
                   pl.BlockSpec((1, 8, n), fixed3), pl.BlockSpec((n, 2 * LANES), fixed2),
                   pl.BlockSpec((1, N_EXPERTS, LANES), fixed3),
                   pl.BlockSpec((POOL_BUF, n, POOL_WIDTH), fixed3),
                   pl.BlockSpec((SSD_CONV - 1, n, CONV_DIM), fixed3), st_spec],
        out_shape=[jax.ShapeDtypeStruct((n, D_MODEL), F32), jax.ShapeDtypeStruct((n, D_MODEL), BF16),
                   jax.ShapeDtypeStruct((1, 8, n), F32), jax.ShapeDtypeStruct((n, 2 * LANES), F32),
                   jax.ShapeDtypeStruct((1, N_EXPERTS, LANES), F32),
                   jax.ShapeDtypeStruct(pool_t.shape, F32),
                   jax.ShapeDtypeStruct(conv_t.shape, F32), jax.ShapeDtypeStruct(ssm_t.shape, F32)],
        scratch_shapes=[pltpu.VMEM((SSD_WIDTH, n), F32), pltpu.VMEM((LANES, n), F32),
                        pltpu.VMEM((SSD_BC, n), F32), pltpu.VMEM((SSD_BC, n), F32),
                        pltpu.VMEM((n, SSD_WIDTH), F32), pltpu.VMEM((SSD_WIDTH, n), F32),
                        pltpu.VMEM((n, SSD_WIDTH), F32), pltpu.VMEM((n, D_MODEL), BF16)],
        compiler_params=_cparams("arbitrary"),
        name="front_step",
    )(x, g_mix, w_in16, w_dt16, pool_t, conv_t, ssm_t, cw, cb, dtb, alog, dsk, nw, plw16, plb, psc,
      w_out16, g_ffn, r_w16, r_bias)


PROMPT_TILE = MIX_ROWS


def _sort_tables(counts, tile_tokens, data_rows):
    cnt = jnp.concatenate(counts, axis=0)
    pc = (cnt + RUN_PAD - 1) // RUN_PAD * RUN_PAD
    off_local = jnp.cumsum(pc, axis=1) - pc
    tile_rows = jnp.sum(pc, axis=1)
    region = (jnp.sum(pc, axis=0) + MOE_TILE - 1) // MOE_TILE * MOE_TILE
    base = jnp.cumsum(region) - region
    dst = base[None, :] + jnp.cumsum(pc, axis=0) - pc
    per_pass = []
    lo = 0
    for c, tm in zip(counts, tile_tokens):
        hi = lo + c.shape[0]
        n_chunk = _sorted_rows_per_tile(tm) // RUN_PAD
        s = jnp.arange(n_chunk, dtype=jnp.int32) * RUN_PAD
        begins = off_local[lo:hi, None, :]
        ends = begins + pc[lo:hi, None, :]
        inside = (begins <= s[None, :, None]) & (s[None, :, None] < ends)
        shift = jnp.sum(jnp.where(inside, dst[lo:hi, None, :] - begins, 0), axis=2)
        valid = s[None, :] < tile_rows[lo:hi, None]
        parity = (jnp.arange(hi - lo, dtype=jnp.int32) % 2)[:, None]
        spare = data_rows + parity * _sorted_rows_per_tile(max(tile_tokens)) + s[None, :]
        scatter_dst = (jnp.where(valid, s[None, :] + shift, spare) // RUN_PAD).astype(jnp.int32)
        gather_src = (jnp.where(valid, s[None, :] + shift, 0) // RUN_PAD).astype(jnp.int32)
        off_v = jnp.broadcast_to(off_local[lo:hi, :, None].astype(F32), (hi - lo, N_EXPERTS, LANES))
        per_pass.append((scatter_dst, gather_src, off_v))
        lo = hi
    tiles_cum = jnp.cumsum(region // MOE_TILE)
    n_active = tiles_cum[-1]
    return per_pass, tiles_cum, n_active


def _max_sorted_rows(tile_counts, tile_tokens):
    rows = sum(n * (2 * tm + N_EXPERTS * (RUN_PAD - 1)) for n, tm in zip(tile_counts, tile_tokens))
    data_rows = (-(-rows // MOE_TILE) + N_EXPERTS) * MOE_TILE
    return data_rows, data_rows + 2 * _sorted_rows_per_tile(max(tile_tokens))


def _row(v):
    return v.reshape(1, -1).astype(F32)


def _per_head_rows(v):
    return jnp.broadcast_to(v.astype(F32)[:, None], (SSD_HEADS, SSD_Q))


def _pad_lanes(v):
    return jnp.pad(v.reshape(1, -1).astype(F32), ((0, 0), (0, LANES - v.size)))


def kernel(x_prompt, x_sample, p_prompt, p_sample, state_pool, state_conv, state_ssm, norm_mix, w_in, pool_lin_w, pool_lin_b, pool_scale, conv_w, conv_b, dt_bias, a_log, d_skip, ssd_norm, w_out, norm_ffn, router_grp_w, router_grp_b, router_exp_w, router_exp_b, exp_w_gate, exp_w_up, exp_w_down, norm_ple, ple_gate_w, ple_gate_b, ple_proj_w, norm_final):
    nb, seq, _ = x_prompt.shape
    ns = x_sample.shape[0]
    assert ns == LANES and x_sample.shape[1] == 1 and seq % MIX_ROWS == 0 and seq >= POOL_BUF

    w_in16 = w_in[0].astype(BF16)
    w_dt16 = jnp.pad(w_in[0][:, DT_OFF:], ((0, 0), (0, LANES - SSD_HEADS))).astype(BF16)
    w_out16 = w_out[0].astype(BF16)
    g_mix, g_ffn, g_ple, g_fin = _row(norm_mix[0]), _row(norm_ffn[0]), _row(norm_ple[0]), _row(norm_final)
    cw, cb = conv_w[0].astype(F32), _row(conv_b[0])
    dtb, alog = _pad_lanes(dt_bias[0]), _pad_lanes(a_log[0])
    dsk = _row(jnp.repeat(d_skip[0], SSD_HEAD_DIM))
    nw = _row(ssd_norm[0])
    plw16 = pool_lin_w[0].astype(BF16)
    plb, psc = _row(pool_lin_b[0]), _row(pool_scale[0])
    zeros4 = jnp.zeros((D_MODEL, 8 - N_EXPERT_GROUPS), F32)
    r_w = jnp.concatenate([router_grp_w[0], zeros4, router_exp_w[0],
                           jnp.zeros((D_MODEL, LANES - 8 - N_EXPERTS), F32)], axis=1)
    r_hi = r_w.astype(BF16)
    r_w16 = jnp.concatenate([r_hi, (r_w - r_hi.astype(F32)).astype(BF16)], axis=1)
    r_b = jnp.concatenate([router_grp_b[0], jnp.zeros((8 - N_EXPERT_GROUPS,), F32), router_exp_b[0],
                           jnp.zeros((LANES - 8 - N_EXPERTS,), F32)])
    wg = exp_w_gate[0].reshape(N_EXPERTS, D_MODEL, EXPERT_FF)
    wu = exp_w_up[0].reshape(N_EXPERTS, D_MODEL, EXPERT_FF)
    wd = exp_w_down[0].reshape(N_EXPERTS, EXPERT_FF, D_MODEL)
    pg16 = ple_gate_w[0].astype(BF16)
    pgb = _row(ple_gate_b[0])
    pp16 = ple_proj_w[0].astype(BF16)

    xp = x_prompt.reshape(nb * seq, D_MODEL)
    h1_p, u_p, rt_p, gate_p, cnt_p, st, pool_tail, conv_tail = _front_prompt(
        xp, g_mix, w_in16, w_dt16, cw, cb, _per_head_rows(dt_bias[0]), _per_head_rows(a_log[0]), dsk, nw, plw16, plb, psc,
        w_out16, g_ffn, r_w16, jnp.broadcast_to(r_b[:, None], (LANES, MIX_ROWS)), nb, seq)
    pool_p = pool_tail[:, TAIL_ROWS - POOL_BUF:]
    conv_p = conv_tail[:, CONV_HDR - (SSD_CONV - 1):]
    ssm_p = st

    xs_ = x_sample.reshape(ns, D_MODEL)
    h1_s, u_s, rt_s, gate_s, cnt_s, pool_t, conv_t, ssm_t = _front_step(
        xs_, g_mix, w_in16, w_dt16, jnp.transpose(state_pool[0], (1, 0, 2)), jnp.transpose(state_conv[0], (1, 0, 2)),
        jnp.transpose(state_ssm[0], (1, 2, 3, 0)), cw, cb, dtb, alog, dsk, nw, plw16, plb, psc,
        w_out16, g_ffn, r_w16, jnp.broadcast_to(r_b[:, None], (LANES, ns)))
    pool_s = jnp.transpose(pool_t, (1, 0, 2))
    conv_s = jnp.transpose(conv_t, (1, 0, 2))
    ssm_s = jnp.transpose(ssm_t, (3, 0, 1, 2))

    counts = [cnt_p[:, :, 0].astype(jnp.int32), cnt_s[:, :, 0].astype(jnp.int32)]
    tiles = (PROMPT_TILE, ns)
    data_rows, total_rows = _max_sorted_rows([c.shape[0] for c in counts], tiles)
    (tab_p, tab_s), tiles_cum, n_active = _sort_tables(counts, tiles, data_rows)
    xs_sorted, dest_p = _sort_tokens(tab_p[0], u_p, rt_p, tab_p[2], None, PROMPT_TILE, total_rows)
    xs_sorted, dest_s = _sort_tokens(tab_s[0], u_s, rt_s, tab_s[2], xs_sorted, ns, total_rows)
    tile_start = jnp.concatenate([jnp.zeros((1,), jnp.int32), tiles_cum.astype(jnp.int32)])
    ys_sorted = _moe_sorted(tile_start, n_active.reshape(1).astype(jnp.int32),
                            xs_sorted.reshape(total_rows, D_MODEL), wg, wu, wd).reshape(xs_sorted.shape)

    y_prompt = _ple(tab_p[1], h1_p, dest_p, gate_p, p_prompt[0].reshape(nb * seq, PLE_DIM),
                    g_ple, pg16, pgb, pp16, g_fin, ys_sorted, PROMPT_TILE)
    y_sample = _ple(tab_s[1], h1_s, dest_s, gate_s, p_sample[0].reshape(ns, PLE_DIM),
                    g_ple, pg16, pgb, pp16, g_fin, ys_sorted, ns)

    return (y_prompt.reshape(nb, seq, D_MODEL), y_sample.reshape(ns, 1, D_MODEL),
            pool_p[None], conv_p[None], ssm_p[None], pool_s[None], conv_s[None], ssm_s[None])
```

```python
import functools

import jax
import jax.numpy as jnp
import numpy as np
from jax import lax
from jax.experimental import pallas as pl
from jax.experimental.pallas import tpu as pltpu

F32 = jnp.float32
BF16 = jnp.bfloat16

D_MODEL = 1024
POOL_WIDTH = 512
POOL_WINDOWS = (2, 4, 8, 16)
POOL_GROUP_DIM = 128
POOL_BUF = 15
SSD_WIDTH = 512
SSD_HEAD_DIM = 64
SSD_HEADS = 8
SSD_GROUPS = 2
SSD_STATE = 64
SSD_CONV = 4
SSD_BC = SSD_GROUPS * SSD_STATE
CONV_DIM = SSD_WIDTH + 2 * SSD_BC
N_EXPERT_GROUPS = 4
EXPERTS_PER_GROUP = 8
N_EXPERTS = N_EXPERT_GROUPS * EXPERTS_PER_GROUP
EXPERT_FF = 256
PLE_DIM = 256
PAST_LEN = 16384
EPS = 1e-6

LANES = 128
DT_OFF = POOL_WIDTH + SSD_WIDTH + CONV_DIM
IN_DIM = DT_OFF + SSD_HEADS
VMEM_LIMIT = 56 * 1024 * 1024


def _cparams(*sem):
    return pltpu.CompilerParams(dimension_semantics=sem, vmem_limit_bytes=VMEM_LIMIT)


def _rms(x, g):
    return x * lax.rsqrt(jnp.mean(x * x, axis=-1, keepdims=True) + EPS) * g


def _sigmoid(x):
    return 1.0 / (1.0 + jnp.exp(-x))


def _silu(x):
    return x * _sigmoid(x)


def _split3(v):
    hi = v.astype(BF16)
    r = v - hi.astype(F32)
    mid = r.astype(BF16)
    lo = (r - mid.astype(F32)).astype(BF16)
    return hi, mid, lo


def _dot(a, b):
    return jnp.dot(a, b, preferred_element_type=F32)


def _dot_nt(a, b):
    return lax.dot_general(a, b, (((1,), (1,)), ((), ())), preferred_element_type=F32)


def _dot_tn(a, b):
    return lax.dot_general(a, b, (((0,), (0,)), ((), ())), preferred_element_type=F32)


def _route(lg):
    tm = lg.shape[1]
    gl = lg[0:N_EXPERT_GROUPS, :]
    gmax = jnp.max(gl, axis=0, keepdims=True)
    gsum = jnp.sum(jnp.exp(gl - gmax), axis=0, keepdims=True)
    g_w = 1.0 / gsum
    gi = lax.broadcasted_iota(jnp.int32, gl.shape, 0)
    g_idx = jnp.min(jnp.where(gl == gmax, gi, N_EXPERT_GROUPS), axis=0, keepdims=True)
    sel = jnp.zeros((EXPERTS_PER_GROUP, tm), F32)
    for g in range(N_EXPERT_GROUPS):
        blk = lg[8 + g * EXPERTS_PER_GROUP:8 + (g + 1) * EXPERTS_PER_GROUP, :]
        sel = jnp.where(g_idx == g, blk, sel)
    ei = lax.broadcasted_iota(jnp.int32, sel.shape, 0)
    m1 = jnp.max(sel, axis=0, keepdims=True)
    i1 = jnp.min(jnp.where(sel == m1, ei, EXPERTS_PER_GROUP), axis=0, keepdims=True)
    rest = jnp.where(ei == i1, -jnp.inf, sel)
    m2 = jnp.max(rest, axis=0, keepdims=True)
    i2 = jnp.min(jnp.where(rest == m2, ei, EXPERTS_PER_GROUP), axis=0, keepdims=True)
    p2 = jnp.exp(m2 - m1)
    w1 = g_w / (1.0 + p2)
    w2 = g_w * p2 / (1.0 + p2)
    return g_idx * EXPERTS_PER_GROUP + i1, g_idx * EXPERTS_PER_GROUP + i2, w1, w2


def _out_proj_and_route(x, mix16, w_ref, g_ref, rw_ref, rb_ref, h_ref, u_ref, rt_ref, wt_ref, cnt_ref,
                        r0=0, accumulate=False):
    tm = x.shape[0]
    h = x + _dot(mix16, w_ref[...])
    h_ref[r0:r0 + tm, :] = h
    _route_rows(h, g_ref, rw_ref, rb_ref, u_ref, rt_ref, wt_ref, cnt_ref, r0=r0, accumulate=accumulate)


def _route_rows(h, g_ref, rw_ref, rb_ref, u_ref, rt_ref, wt_ref, cnt_ref, r0=0, accumulate=False):
    tm = h.shape[0]
    u = _rms(h, g_ref[...])
    u_hi = u.astype(BF16)
    u_ref[r0:r0 + tm, :] = u_hi
    u_lo = (u - u_hi.astype(F32)).astype(BF16)
    both = _dot(u_hi, rw_ref[...])
    lg_rows = both[:, 0:LANES] + both[:, LANES:2 * LANES] + _dot(u_lo, rw_ref[:, 0:LANES])
    lg = lg_rows.T + rb_ref[:, r0:r0 + tm]
    b1, b2, w1, w2 = _route(lg)
    r8 = lax.broadcasted_iota(jnp.int32, (8, tm), 0)
    rt_ref[0, :, r0:r0 + tm] = jnp.where(r8 == 0, b1.astype(F32), jnp.where(r8 == 1, b2.astype(F32), 0.0))
    wt_ref[r0:r0 + tm, :] = jnp.concatenate(
        [jnp.broadcast_to(w1, (LANES, tm)).T, jnp.broadcast_to(w2, (LANES, tm)).T], axis=1)
    kio = lax.broadcasted_iota(jnp.int32, (N_EXPERTS, tm), 0)
    hits = ((kio == b1) | (kio == b2)).astype(F32)
    cnt = jnp.broadcast_to(jnp.sum(hits, axis=1, keepdims=True), (N_EXPERTS, LANES))
    cnt_ref[0] = cnt_ref[0] + cnt if accumulate else cnt


RUN_PAD = 16
MOE_TILE = 256
MOE_GROUP = 3
MOE_AHEAD = 8
MOE_BUFS = MOE_AHEAD + MOE_GROUP + 1


def _tile_lanes(v, width):
    reps = width // LANES
    return v if reps == 1 else jnp.concatenate([v] * reps, axis=1)


def _sorted_rows_per_tile(tm):
    need = 2 * tm + N_EXPERTS * (RUN_PAD - 1)
    return -(-need // LANES) * LANES


def _sort_body(has_next, cd_ref, u_ref, rt_ref, off_ref, rtn_ref, offn_ref, before_ref, *rest):
    xs_ref, dt_ref, loc, dsc, sem = rest[-5:]
    i = pl.program_id(0)
    tm = u_ref.shape[0]
    n_chunk = loc.shape[1]
    rows = n_chunk * RUN_PAD
    slot = i % 2

    def destinations(rt_r, off_r, sl):
        rt = rt_r[0]
        b1 = rt[0:1, :].astype(jnp.int32)
        b2 = rt[1:2, :].astype(jnp.int32)
        kio = lax.broadcasted_iota(jnp.int32, (N_EXPERTS, tm), 0)
        o1 = kio == b1
        o2 = kio == b2
        start = _dot((o1 | o2).astype(BF16), before_ref[...]) + _tile_lanes(off_r[0], tm)
        dsc[sl, 0:1, :] = jnp.sum(jnp.where(o1, start, 0.0), axis=0, keepdims=True)
        dsc[sl, 1:2, :] = jnp.sum(jnp.where(o2, start, 0.0), axis=0, keepdims=True)

    @pl.when(i == 0)
    def _():
        destinations(rt_ref, off_ref, slot)

    d1 = dsc[slot, 0:1, :]
    d2 = dsc[slot, 1:2, :]
    dt_ref[...] = jnp.concatenate([jnp.broadcast_to(d1, (LANES, tm)).T, jnp.broadcast_to(d2, (LANES, tm)).T], axis=1)
    rio = lax.broadcasted_iota(jnp.int32, (rows, tm), 0)
    perm = ((rio == d1.astype(jnp.int32)) | (rio == d2.astype(jnp.int32))).astype(BF16)
    loc[slot] = _dot(perm, u_ref[...]).astype(BF16).reshape(n_chunk, RUN_PAD, D_MODEL)
    if has_next:
        destinations(rtn_ref, offn_ref, 1 - slot)

    def chunk_copy(step, sl, c):
        return pltpu.make_async_copy(loc.at[sl, c], xs_ref.at[cd_ref[step, c]], sem.at[sl])

    for c in range(n_chunk):
        chunk_copy(i, slot, c).start(priority=c % 2)

    @pl.when(i > 0)
    def _():
        for c in range(n_chunk):
            chunk_copy(i - 1, 1 - slot, c).wait()

    @pl.when(i == pl.num_programs(0) - 1)
    def _():
        for c in range(n_chunk):
            chunk_copy(i, slot, c).wait()


def _sort_tokens(chunk_dst, u16, rt, off_v, xs_prev, tm, total_rows):
    t = u16.shape[0]
    rows = _sorted_rows_per_tile(tm)
    in_specs = [pl.BlockSpec((tm, D_MODEL), lambda i, cd: (i, 0)),
                pl.BlockSpec((1, 8, tm), lambda i, cd: (i, 0, 0)),
                pl.BlockSpec((1, N_EXPERTS, LANES), lambda i, cd: (i, 0, 0)),
                pl.BlockSpec((1, 8, tm), lambda i, cd: (jnp.minimum(i + 1, t // tm - 1), 0, 0)),
                pl.BlockSpec((1, N_EXPERTS, LANES), lambda i, cd: (jnp.minimum(i + 1, t // tm - 1), 0, 0)),
                pl.BlockSpec((tm, tm), lambda i, cd: (0, 0))]
    before = jnp.asarray(np.triu(np.ones((tm, tm), np.float32), 1)).astype(BF16)
    args = [chunk_dst, u16, rt, off_v, rt, off_v, before]
    aliases = {}
    if xs_prev is not None:
        in_specs.append(pl.BlockSpec(memory_space=pl.ANY))
        aliases = {len(args): 0}
        args.append(xs_prev)
    return pl.pallas_call(
        functools.partial(_sort_body, t // tm > 1),
        grid_spec=pltpu.PrefetchScalarGridSpec(
            num_scalar_prefetch=1,
            grid=(t // tm,),
            in_specs=in_specs,
            out_specs=[pl.BlockSpec(memory_space=pl.ANY),
                       pl.BlockSpec((tm, 2 * LANES), lambda i, cd: (i, 0))],
            scratch_shapes=[pltpu.VMEM((2, rows // RUN_PAD, RUN_PAD, D_MODEL), BF16),
                            pltpu.VMEM((2, 8, tm), F32), pltpu.SemaphoreType.DMA((2,))]),
        out_shape=[jax.ShapeDtypeStruct((total_rows // RUN_PAD, RUN_PAD, D_MODEL), BF16),
                   jax.ShapeDtypeStruct((t, 2 * LANES), F32)],
        input_output_aliases=aliases,
        compiler_params=_cparams("arbitrary"),
        name="sort_tokens",
    )(*args)


def _moe_body(ts_ref, na_ref, xs_ref, wg_ref, wu_ref, wd_ref, ys_ref, xbuf, ybuf, wg16, wu16, wd16, sem_in, sem_out):
    k = pl.program_id(0)
    n_act = na_ref[0]

    def x_copy(g, sl):
        return pltpu.make_async_copy(xs_ref.at[pl.ds(pl.multiple_of(g * MOE_TILE, MOE_TILE), MOE_TILE), :],
                                     xbuf.at[sl], sem_in.at[sl])

    def y_copy(g, sl):
        return pltpu.make_async_copy(ybuf.at[sl],
                                     ys_ref.at[pl.ds(pl.multiple_of(g * MOE_TILE, MOE_TILE), MOE_TILE), :],
                                     sem_out.at[sl])

    @pl.when(k == 0)
    def _():
        for j in range(MOE_AHEAD):
            @pl.when(j < n_act)
            def _():
                x_copy(j, j).start(priority=1)

    wg16[...] = wg_ref[0].astype(BF16)
    wu16[...] = wu_ref[0].astype(BF16)
    wd16[...] = wd_ref[0].astype(BF16)

    def arrive(g):
        sl = g % MOE_BUFS

        @pl.when(g + MOE_AHEAD < n_act)
        def _():
            x_copy(g + MOE_AHEAD, (g + MOE_AHEAD) % MOE_BUFS).start(priority=1)

        x_copy(g, sl).wait()

        @pl.when(g >= MOE_BUFS)
        def _():
            y_copy(g - MOE_BUFS, sl).wait()

    def tiles(g, count):
        for j in range(count):
            arrive(g + j)
        acts = []
        for j in range(count):
            x = xbuf[(g + j) % MOE_BUFS]
            acts.append((_silu(_dot(x, wg16[...])) * _dot(x, wu16[...])).astype(BF16))
        for j in range(count):
            ybuf[(g + j) % MOE_BUFS] = _dot(acts[j], wd16[...]).astype(BF16)
        for j in range(count):
            y_copy(g + j, (g + j) % MOE_BUFS).start()

    g0 = ts_ref[k]
    n_here = ts_ref[k + 1] - g0

    def group(p, carry):
        tiles(g0 + MOE_GROUP * p, MOE_GROUP)
        return carry

    n_groups = n_here // MOE_GROUP
    lax.fori_loop(0, n_groups, group, 0)
    for rest in range(1, MOE_GROUP):
        @pl.when(n_here - n_groups * MOE_GROUP == rest)
        def _():
            tiles(g0 + n_groups * MOE_GROUP, rest)

    @pl.when(k == pl.num_programs(0) - 1)
    def _():
        for j in range(1, MOE_BUFS + 1):
            @pl.when(n_act >= j)
            def _():
                y_copy(n_act - j, (n_act - j) % MOE_BUFS).wait()


def _moe_sorted(tile_start, n_active, xs, wg, wu, wd):
    w_map = lambda k, ts, na: (k, 0, 0)
    return pl.pallas_call(
        _moe_body,
        grid_spec=pltpu.PrefetchScalarGridSpec(
            num_scalar_prefetch=2,
            grid=(N_EXPERTS,),
            in_specs=[pl.BlockSpec(memory_space=pl.ANY),
                      pl.BlockSpec((1, D_MODEL, EXPERT_FF), w_map),
                      pl.BlockSpec((1, D_MODEL, EXPERT_FF), w_map),
                      pl.BlockSpec((1, EXPERT_FF, D_MODEL), w_map)],
            out_specs=pl.BlockSpec(memory_space=pl.ANY),
            scratch_shapes=[pltpu.VMEM((MOE_BUFS, MOE_TILE, D_MODEL), BF16),
                            pltpu.VMEM((MOE_BUFS, MOE_TILE, D_MODEL), BF16),
                            pltpu.VMEM((D_MODEL, EXPERT_FF), BF16), pltpu.VMEM((D_MODEL, EXPERT_FF), BF16),
                            pltpu.VMEM((EXPERT_FF, D_MODEL), BF16),
                            pltpu.SemaphoreType.DMA((MOE_BUFS,)), pltpu.SemaphoreType.DMA((MOE_BUFS,))]),
        out_shape=jax.ShapeDtypeStruct(xs.shape, BF16),
        compiler_params=_cparams("arbitrary"),
        name="moe_sorted",
    )(tile_start, n_active, xs, wg, wu, wd)


def _ple_body(cd_ref, h_ref, dt_ref, wt_ref, p_ref, gn_ref, wg_ref, bg_ref, wp_ref, fn_ref, ys_ref,
              o_ref, loc, sem):
    i = pl.program_id(0)
    tm = h_ref.shape[0]
    n_chunk = loc.shape[1]
    rows = n_chunk * RUN_PAD
    slot = i % 2

    def chunk_copy(step, sl, c):
        return pltpu.make_async_copy(ys_ref.at[cd_ref[step, c]], loc.at[sl, c], sem.at[sl])

    def fetch(step, sl):
        for c in range(n_chunk):
            chunk_copy(step, sl, c).start(priority=c % 2)

    @pl.when(i == 0)
    def _():
        fetch(0, 0)

    @pl.when(i + 1 < pl.num_programs(0))
    def _():
        fetch(i + 1, 1 - slot)

    for c in range(n_chunk):
        chunk_copy(i, slot, c).wait()

    ci = lax.broadcasted_iota(jnp.int32, (tm, LANES), 1)
    d1 = dt_ref[:, 0:LANES].astype(jnp.int32)
    d2 = dt_ref[:, LANES:2 * LANES].astype(jnp.int32)
    w1 = wt_ref[:, 0:LANES]
    w2 = wt_ref[:, LANES:2 * LANES]
    sel = jnp.concatenate(
        [jnp.where(d1 == ci + m * LANES, w1, jnp.where(d2 == ci + m * LANES, w2, 0.0)) for m in range(rows // LANES)],
        axis=1).astype(BF16)
    h = h_ref[...] + _dot(sel, loc[slot].reshape(rows, D_MODEL))
    a16 = _rms(h, gn_ref[...]).astype(BF16)
    gate = _sigmoid(_dot(a16, wg_ref[...]) + bg_ref[...])
    pp = _dot(p_ref[...].astype(BF16), wp_ref[...])
    h = h + gate * pp
    o_ref[...] = _rms(h, fn_ref[...])


def _ple(chunk_src, h1, dest_t, gate_t, p, g_ple, wg16, bg, wp16, g_final, ys, tm):
    t = h1.shape[0]
    row = lambda i, cd: (i, 0)
    fixed = lambda i, cd: (0, 0)
    return pl.pallas_call(
        _ple_body,
        grid_spec=pltpu.PrefetchScalarGridSpec(
            num_scalar_prefetch=1,
            grid=(t // tm,),
            in_specs=[pl.BlockSpec((tm, D_MODEL), row), pl.BlockSpec((tm, 2 * LANES), row),
                      pl.BlockSpec((tm, 2 * LANES), row), pl.BlockSpec((tm, PLE_DIM), row),
                      pl.BlockSpec((1, D_MODEL), fixed), pl.BlockSpec((D_MODEL, D_MODEL), fixed),
                      pl.BlockSpec((1, D_MODEL), fixed), pl.BlockSpec((PLE_DIM, D_MODEL), fixed),
                      pl.BlockSpec((1, D_MODEL), fixed), pl.BlockSpec(memory_space=pl.ANY)],
            out_specs=pl.BlockSpec((tm, D_MODEL), row),
            scratch_shapes=[pltpu.VMEM((2, _sorted_rows_per_tile(tm) // RUN_PAD, RUN_PAD, D_MODEL), BF16),
                            pltpu.SemaphoreType.DMA((2,))]),
        out_shape=jax.ShapeDtypeStruct((t, D_MODEL), F32),
        compiler_params=_cparams("arbitrary"),
        name="ple_final",
    )(chunk_src, h1, dest_t, gate_t, p, g_ple, wg16, bg, wp16, g_final, ys)


MIX_ROWS = 512
SSD_Q = 128
POOL_HDR = 128
CONV_HDR = 8
TAIL_ROWS = 16


def _softplus(x):
    return jnp.maximum(x, 0.0) + jnp.log1p(jnp.exp(-jnp.abs(x)))


def _stack_terms(v, terms):
    parts = [t.astype(F32) for t in _split3(v)[:terms]]
    parts.append(jnp.zeros(((4 - terms) * v.shape[0], v.shape[1]), F32))
    return jnp.concatenate(parts, axis=0).astype(BF16)


def _cumsum_lanes(v, tri16):
    r = _dot_nt(_stack_terms(v, 3), tri16)
    n = v.shape[0]
    return r[0:n] + r[n:2 * n] + r[2 * n:3 * n]


def _expand_heads(v, expand32, terms):
    return _dot_tn(_stack_terms(v, terms), expand32)


def _mixer_constants():
    q = SSD_Q
    li = np.arange(q)[:, None]
    tri = (li >= np.arange(q)[None, :])
    expand = (np.arange(SSD_WIDTH)[None, :] // SSD_HEAD_DIM) == (np.arange(4 * SSD_HEADS)[:, None] % SSD_HEADS)
    hpg = SSD_HEADS // SSD_GROUPS
    block = (np.arange(SSD_BC)[:, None] // SSD_STATE) == (np.arange(SSD_WIDTH)[None, :] // (SSD_HEAD_DIM * hpg))
    wj = np.arange(2 * q)[None, :]
    band = np.stack([(wj <= li + POOL_HDR) & (wj > li + POOL_HDR - w) for w in POOL_WINDOWS])
    inv_count = np.stack([np.broadcast_to(1.0 / np.minimum(li + 1, w), (q, LANES)) for w in POOL_WINDOWS])
    as_f32 = lambda m: jnp.asarray(m.astype(np.float32))
    return (as_f32(tri), as_f32(tri).astype(BF16), as_f32(expand).astype(BF16), as_f32(block),
            as_f32(band).astype(BF16), as_f32(inv_count))


def _front_body(steps_per_seq, x_ref, xp_ref, gm_ref, win_ref, wdt_ref, trif_ref, tri_ref, exp_ref, bmask_ref, band_ref,
                invc_ref, cw_ref, cb_ref, dtb_ref, alog_ref, dsk_ref, nw_ref,
                plw_ref, plb_ref, psc_ref, wout_ref, gf_ref, rw_ref, rb_ref,
                h_ref, u_ref, rt_ref, wt_ref, cnt_ref, st_ref, ptail_ref, ctail_ref,
                pool_ext, conv_ext, s_ref, z_ref, dt_ref, mix_ref, mixp_ref, a_ref):
    s = pl.program_id(0)
    n_tiles = pl.num_programs(0) - 1
    real = s < n_tiles
    c = jnp.minimum(s, n_tiles - 1) % steps_per_seq
    rows = MIX_ROWS
    q_len = SSD_Q

    @pl.when(s == 0)
    def _():
        mixp_ref[...] = jnp.zeros_like(mixp_ref)

    @pl.when(c == 0)
    def _():
        pool_ext[0:POOL_HDR, :] = jnp.zeros((POOL_HDR, POOL_WIDTH), F32)
        conv_ext[0:CONV_HDR, :] = jnp.zeros((CONV_HDR, CONV_DIM), F32)
        s_ref[...] = jnp.zeros_like(s_ref)

    @pl.when(c > 0)
    def _():
        pool_ext[0:POOL_HDR, :] = pool_ext[rows:rows + POOL_HDR, :]
        conv_ext[0:CONV_HDR, :] = conv_ext[rows:rows + CONV_HDR, :]

    causal = trif_ref[...] > 0.5
    tri = tri_ref[...]
    expand = exp_ref[...]
    blockmask = bmask_ref[...]
    lane = lax.broadcasted_iota(jnp.int32, (q_len, LANES), 1)
    left = lane < SSD_HEAD_DIM
    a_neg = -jnp.exp(alog_ref[...])

    half = rows // 2
    n_cols = 256

    def norm_rows():
        for r0 in (0, half):
            a_ref[r0:r0 + half, :] = _rms(x_ref[r0:r0 + half, :], gm_ref[...]).astype(BF16)

    def in_proj_chunk(ref, row, k, col0):
        def run():
            ref[row:row + rows, k:k + n_cols] = _dot(a_ref[...], win_ref[:, col0 + k:col0 + k + n_cols])
        return run

    def dt_chunk():
        dt_ref[...] = _dot(a_ref[...], wdt_ref[...])

    conv_chunks = [in_proj_chunk(conv_ext, CONV_HDR, k, POOL_WIDTH + SSD_WIDTH) for k in range(0, CONV_DIM, n_cols)]
    z_chunks = [in_proj_chunk(z_ref, 0, k, POOL_WIDTH) for k in range(0, SSD_WIDTH, n_cols)]
    pool_chunks = [in_proj_chunk(pool_ext, POOL_HDR, k, 0) for k in range(0, POOL_WIDTH, n_cols)]

    def out_proj_chunk(k):
        def run():
            h_ref[:, k:k + n_cols] = xp_ref[:, k:k + n_cols] + _dot(mixp_ref[...], wout_ref[:, k:k + n_cols])
        return run

    def route(r0):
        def run():
            _route_rows(h_ref[r0:r0 + half, :], gf_ref, rw_ref, rb_ref, u_ref, rt_ref, wt_ref, cnt_ref,
                        r0=r0, accumulate=r0 > 0)
        return run

    out_chunks = [out_proj_chunk(k) for k in range(0, D_MODEL, n_cols)]
    routes = [route(0), route(half)]

    def mix_pair(q_first, fillers):
        fillers = list(fillers)

        def fill():
            if fillers:
                fillers.pop(0)()

        subs = range(q_first, q_first + 2)
        xs_l, b16_l, call_l, c16_l = {}, {}, {}, {}
        for q in subs:
            base = CONV_HDR + q * q_len
            conv = cb_ref[...] + conv_ext[base - 3:base - 3 + q_len, :] * cw_ref[0:1, :]
            conv = conv + conv_ext[base - 2:base - 2 + q_len, :] * cw_ref[1:2, :]
            conv = conv + conv_ext[base - 1:base - 1 + q_len, :] * cw_ref[2:3, :]
            conv = conv + conv_ext[base:base + q_len, :] * cw_ref[3:4, :]
            conv = _silu(conv)
            xs_l[q] = conv[:, 0:SSD_WIDTH]
            b16_l[q] = conv[:, SSD_WIDTH:SSD_WIDTH + SSD_BC].astype(BF16)
            call_l[q] = conv[:, SSD_WIDTH + SSD_BC:CONV_DIM]
            c16_l[q] = call_l[q].astype(BF16)
            fill()
        dtt_l = {q: _softplus(dt_ref[q * q_len:(q + 1) * q_len, :].T[0:SSD_HEADS, :] + dtb_ref[...]) for q in subs}
        acst_l = {q: _cumsum_lanes(dtt_l[q] * a_neg, tri) for q in subs}
        acs_l = {q: jnp.concatenate([acst_l[q], jnp.zeros((LANES - SSD_HEADS, q_len), F32)], axis=0).T
                 for q in subs}
        fill()
        dtx_l = {q: _expand_heads(dtt_l[q], expand, 2) for q in subs}
        acsx_l = {q: _expand_heads(acst_l[q], expand, 3) for q in subs}
        lastx_l = {q: acsx_l[q][q_len - 1:q_len, :] for q in subs}
        xdt_l = {q: xs_l[q] * dtx_l[q] for q in subs}
        xdt16_l = {q: xdt_l[q].astype(BF16) for q in subs}
        fill()
        contrib_l = {q: _dot_tn(b16_l[q], (xdt_l[q] * jnp.exp(lastx_l[q] - acsx_l[q])).astype(BF16)) * blockmask
                     for q in subs}
        cb_l = {q: [_dot_nt(jnp.where(lax.shift_right_logical(lane, 6) == g, call_l[q], 0.0).astype(BF16), b16_l[q])
                    for g in range(SSD_GROUPS)] for q in subs}
        fill()
        zs_l = {}
        for q in subs:
            zs = []
            for h in range(SSD_HEADS):
                blk = slice((h // 2) * LANES, (h // 2 + 1) * LANES)
                seg = acs_l[q][:, h:h + 1] - acst_l[q][h:h + 1, :]
                decay = jnp.where(causal, jnp.exp(seg), 0.0)
                scores = (cb_l[q][h // (SSD_HEADS // SSD_GROUPS)] * decay).astype(BF16)
                zs.append(_dot(scores, xdt16_l[q][:, blk]))
                if h % 4 == 3:
                    fill()
            zs_l[q] = zs
        yoff_l = {}
        for q in subs:
            s_old = s_ref[...]
            yoff_l[q] = _dot(c16_l[q], s_old.astype(BF16)) * jnp.exp(acsx_l[q])
            s_ref[...] = s_old * jnp.exp(lastx_l[q]) + contrib_l[q]
        fill()
        for q in subs:
            r0 = q * q_len
            z = z_ref[r0:r0 + q_len, :]
            y_blocks = []
            for j in range(SSD_HEADS // 2):
                blk = slice(j * LANES, (j + 1) * LANES)
                y = jnp.where(left, zs_l[q][2 * j], zs_l[q][2 * j + 1]) + yoff_l[q][:, blk]
                y = y + dsk_ref[:, blk] * xs_l[q][:, blk]
                y_blocks.append(y * _silu(z[:, blk]))
            for g in range(SSD_GROUPS):
                y0, y1 = y_blocks[2 * g], y_blocks[2 * g + 1]
                ss = jnp.sum(y0 * y0, axis=-1, keepdims=True) + jnp.sum(y1 * y1, axis=-1, keepdims=True)
                rs = lax.rsqrt(ss * (1.0 / (2 * LANES)) + EPS)
                for k, yk in ((2 * g, y0), (2 * g + 1, y1)):
                    blk = slice(k * LANES, (k + 1) * LANES)
                    out = yk * rs * nw_ref[:, blk]
                    mix_ref[r0:r0 + q_len, POOL_WIDTH + k * LANES:POOL_WIDTH + (k + 1) * LANES] = out.astype(BF16)
            fill()
        for q in subs:
            r0 = q * q_len
            for g, w in enumerate(POOL_WINDOWS):
                blk = slice(g * POOL_GROUP_DIM, (g + 1) * POOL_GROUP_DIM)
                pe = pool_ext[r0:r0 + 2 * q_len, blk]
                hi = pe.astype(BF16)
                lo = (pe - hi.astype(F32)).astype(BF16)
                both = _dot(band_ref[g], jnp.concatenate([hi, lo], axis=1))
                winsum = both[:, 0:POOL_GROUP_DIM] + both[:, POOL_GROUP_DIM:2 * POOL_GROUP_DIM]
                inv = jnp.where(c == 0, invc_ref[g], 1.0 / w) if q == 0 else 1.0 / w
                m = winsum * inv - pe[q_len:2 * q_len, :]
                yg = _dot(m.astype(BF16), plw_ref[g]) + plb_ref[:, blk]
                mix_ref[r0:r0 + q_len, blk] = (yg * psc_ref[:, blk]).astype(BF16)
            fill()
        while fillers:
            fill()

    @pl.when(real)
    def _():
        norm_rows()
        for run in conv_chunks + [dt_chunk]:
            run()
        mix_pair(0, z_chunks + pool_chunks + out_chunks[:2])
        mix_pair(2, out_chunks[2:] + routes)
        mixp_ref[...] = mix_ref[...]

    @pl.when(jnp.logical_not(real))
    def _():
        for run in out_chunks + routes:
            run()

    @pl.when(real & (c == steps_per_seq - 1))
    def _():
        hpg = SSD_HEADS // SSD_GROUPS
        for j in range(SSD_HEADS // 2):
            flipped = s_ref[:, j * LANES:(j + 1) * LANES].T
            g = (2 * j) // hpg
            for k in range(2):
                st_ref[0, 2 * j + k] = flipped[k * SSD_HEAD_DIM:(k + 1) * SSD_HEAD_DIM,
                                               g * SSD_STATE:(g + 1) * SSD_STATE]
        ptail_ref[0] = pool_ext[POOL_HDR + rows - TAIL_ROWS:POOL_HDR + rows, :]
        ctail_ref[0] = conv_ext[rows:rows + CONV_HDR, :]


def _front_prompt(x, g_mix, w_in16, w_dt16, cw, cb, dtb, alog, dsk, nw, plw16, plb, psc, w_out16, g_ffn, r_w16, r_bias,
                  nb, seq):
    steps = seq // MIX_ROWS
    nt = nb * steps
    cur = lambda s: (jnp.minimum(s, nt - 1), 0)
    prev = lambda s: (jnp.maximum(s - 1, 0), 0)
    prev3 = lambda s: (jnp.maximum(s - 1, 0), 0, 0)
    per_seq = lambda s: (jnp.minimum(s, nt - 1) // steps, 0, 0)
    fixed2 = lambda s: (0, 0)
    fixed3 = lambda s: (0, 0, 0)
    return pl.pallas_call(
        functools.partial(_front_body, steps),
        grid=(nt + 1,),
        in_specs=[pl.BlockSpec((MIX_ROWS, D_MODEL), cur), pl.BlockSpec((MIX_ROWS, D_MODEL), prev),
                  pl.BlockSpec((1, D_MODEL), fixed2), pl.BlockSpec((D_MODEL, IN_DIM), fixed2),
                  pl.BlockSpec((D_MODEL, LANES), fixed2),
                  pl.BlockSpec((SSD_Q, SSD_Q), fixed2), pl.BlockSpec((SSD_Q, SSD_Q), fixed2),
                  pl.BlockSpec((4 * SSD_HEADS, SSD_WIDTH), fixed2), pl.BlockSpec((SSD_BC, SSD_WIDTH), fixed2),
                  pl.BlockSpec((len(POOL_WINDOWS), SSD_Q, 2 * SSD_Q), fixed3),
                  pl.BlockSpec((len(POOL_WINDOWS), SSD_Q, LANES), fixed3),
                  pl.BlockSpec((SSD_CONV, CONV_DIM), fixed2), pl.BlockSpec((1, CONV_DIM), fixed2),
                  pl.BlockSpec((SSD_HEADS, SSD_Q), fixed2), pl.BlockSpec((SSD_HEADS, SSD_Q), fixed2),
                  pl.BlockSpec((1, SSD_WIDTH), fixed2), pl.BlockSpec((1, SSD_WIDTH), fixed2),
                  pl.BlockSpec((len(POOL_WINDOWS), POOL_GROUP_DIM, POOL_GROUP_DIM), fixed3),
                  pl.BlockSpec((1, POOL_WIDTH), fixed2), pl.BlockSpec((1, POOL_WIDTH), fixed2),
                  pl.BlockSpec((D_MODEL, D_MODEL), fixed2), pl.BlockSpec((1, D_MODEL), fixed2),
                  pl.BlockSpec((D_MODEL, 2 * LANES), fixed2),
                  pl.BlockSpec((LANES, MIX_ROWS), fixed2)],
        out_specs=[pl.BlockSpec((MIX_ROWS, D_MODEL), prev), pl.BlockSpec((MIX_ROWS, D_MODEL), prev),
                   pl.BlockSpec((1, 8, MIX_ROWS), prev3), pl.BlockSpec((MIX_ROWS, 2 * LANES), prev),
                   pl.BlockSpec((1, N_EXPERTS, LANES), prev3),
                   pl.BlockSpec((1, SSD_HEADS, SSD_HEAD_DIM, SSD_STATE),
                                lambda s: (jnp.minimum(s, nt - 1) // steps, 0, 0, 0)),
                   pl.BlockSpec((1, TAIL_ROWS, POOL_WIDTH), per_seq),
                   pl.BlockSpec((1, CONV_HDR, CONV_DIM), per_seq)],
        out_shape=[jax.ShapeDtypeStruct((nb * seq, D_MODEL), F32), jax.ShapeDtypeStruct((nb * seq, D_MODEL), BF16),
                   jax.ShapeDtypeStruct((nt, 8, MIX_ROWS), F32), jax.ShapeDtypeStruct((nb * seq, 2 * LANES), F32),
                   jax.ShapeDtypeStruct((nt, N_EXPERTS, LANES), F32),
                   jax.ShapeDtypeStruct((nb, SSD_HEADS, SSD_HEAD_DIM, SSD_STATE), F32),
                   jax.ShapeDtypeStruct((nb, TAIL_ROWS, POOL_WIDTH), F32),
                   jax.ShapeDtypeStruct((nb, CONV_HDR, CONV_DIM), F32)],
        scratch_shapes=[pltpu.VMEM((POOL_HDR + MIX_ROWS, POOL_WIDTH), F32),
                        pltpu.VMEM((CONV_HDR + MIX_ROWS, CONV_DIM), F32),
                        pltpu.VMEM((SSD_BC, SSD_WIDTH), F32),
                        pltpu.VMEM((MIX_ROWS, SSD_WIDTH), F32), pltpu.VMEM((MIX_ROWS, LANES), F32),
                        pltpu.VMEM((MIX_ROWS, D_MODEL), BF16), pltpu.VMEM((MIX_ROWS, D_MODEL), BF16),
                        pltpu.VMEM((MIX_ROWS, D_MODEL), BF16)],
        compiler_params=_cparams("arbitrary"),
        name="front_prompt",
    )(x, x, g_mix, w_in16, w_dt16, *_mixer_constants(), cw, cb, dtb, alog, dsk, nw, plw16, plb, psc, w_out16, g_ffn,
      r_w16, r_bias)


def _mix_step_body(x_ref, gm_ref, win_ref, wdt_ref, sp_ref, sc_ref, st_ref, cw_ref, cb_ref, dtb_ref, alog_ref,
                   dsk_ref, nw_ref, plw_ref, plb_ref, psc_ref, wout_ref, gf_ref, rw_ref, rb_ref,
                   h_ref, u_ref, rt_ref, wt_ref, cnt_ref, po_ref, co_ref, so_ref,
                   xdt_t, dec_t, b_t, c_t, xs_keep, y_t, z_ref, mix_ref):
    h = pl.program_id(0)

    @pl.when(h == 0)
    def _():
        a16 = _rms(x_ref[...], gm_ref[...]).astype(BF16)
        vp_new = _dot(a16, win_ref[:, 0:POOL_WIDTH])
        z_ref[...] = _dot(a16, win_ref[:, POOL_WIDTH:POOL_WIDTH + SSD_WIDTH])
        xbc = _dot(a16, win_ref[:, POOL_WIDTH + SSD_WIDTH:DT_OFF])
        dt_raw = _dot(a16, wdt_ref[...])
        conv = cb_ref[...] + sc_ref[0] * cw_ref[0:1, :]
        conv = conv + sc_ref[1] * cw_ref[1:2, :]
        conv = conv + sc_ref[2] * cw_ref[2:3, :]
        conv = conv + xbc * cw_ref[3:4, :]
        conv = _silu(conv)
        co_ref[0] = sc_ref[1]
        co_ref[1] = sc_ref[2]
        co_ref[2] = xbc
        xs = conv[:, 0:SSD_WIDTH]
        xs_keep[...] = xs
        b_t[...] = conv[:, SSD_WIDTH:SSD_WIDTH + SSD_BC].T
        c_t[...] = conv[:, SSD_WIDTH + SSD_BC:CONV_DIM].T
        dt = _softplus(dt_raw + dtb_ref[...])
        d_a = dt * (-jnp.exp(alog_ref[...]))
        dt_t = dt.T
        dec_t[...] = jnp.exp(d_a).T
        xs_t = xs.T
        for k in range(SSD_HEADS):
            blk = slice(k * SSD_HEAD_DIM, (k + 1) * SSD_HEAD_DIM)
            xdt_t[blk, :] = xs_t[blk, :] * dt_t[k:k + 1, :]
        v = vp_new
        for k in range(POOL_BUF - 1):
            po_ref[k] = sp_ref[k + 1]
        po_ref[POOL_BUF - 1] = v
        for g, w in enumerate(POOL_WINDOWS):
            blk = slice(g * POOL_GROUP_DIM, (g + 1) * POOL_GROUP_DIM)
            acc = sp_ref[POOL_BUF - (w - 1), :, blk]
            for k in range(w - 2, 0, -1):
                acc = acc + sp_ref[POOL_BUF - k, :, blk]
            acc = acc + v[:, blk]
            m = acc / float(min(PAST_LEN + 1, w)) - v[:, blk]
            yg = _dot(m.astype(BF16), plw_ref[g]) + plb_ref[:, blk]
            mix_ref[:, blk] = (yg * psc_ref[:, blk]).astype(BF16)

    g_off = pl.multiple_of((h // (SSD_HEADS // SSD_GROUPS)) * SSD_STATE, SSD_STATE)
    h_off = pl.multiple_of(h * SSD_HEAD_DIM, SSD_HEAD_DIM)
    b_g = b_t[pl.ds(g_off, SSD_STATE), :]
    c_g = c_t[pl.ds(g_off, SSD_STATE), :]
    dec = dec_t[pl.ds(h, 1), :]
    xdt = xdt_t[pl.ds(h_off, SSD_HEAD_DIM), :]
    y_rows = []
    for p in range(SSD_HEAD_DIM):
        s_new = st_ref[0, p] * dec + xdt[p:p + 1, :] * b_g
        so_ref[0, p] = s_new
        y_rows.append(jnp.sum(s_new * c_g, axis=0, keepdims=True))
    y_t[pl.ds(h_off, SSD_HEAD_DIM), :] = jnp.concatenate(y_rows, axis=0)

    @pl.when(h == pl.num_programs(0) - 1)
    def _():
        xs = xs_keep[...]
        y = y_t[...].T + dsk_ref[...] * xs
        y = y * _silu(z_ref[...])
        width = SSD_WIDTH // SSD_GROUPS
        for g in range(SSD_GROUPS):
            blk = slice(g * width, (g + 1) * width)
            yg = y[:, blk]
            rs = lax.rsqrt(jnp.mean(yg * yg, axis=-1, keepdims=True) + EPS)
            mix_ref[:, POOL_WIDTH + g * width:POOL_WIDTH + (g + 1) * width] = (yg * rs * nw_ref[:, blk]).astype(BF16)
        _out_proj_and_route(x_ref[...], mix_ref[...], wout_ref, gf_ref, rw_ref, rb_ref,
                            h_ref, u_ref, rt_ref, wt_ref, cnt_ref)


def _front_step(x, g_mix, w_in16, w_dt16, pool_t, conv_t, ssm_t, cw, cb, dtb, alog, dsk, nw, plw16, plb, psc,
                w_out16, g_ffn, r_w16, r_bias):
    n = x.shape[0]
    fixed2 = lambda h: (0, 0)
    fixed3 = lambda h: (0, 0, 0)
    st_spec = pl.BlockSpec((1, SSD_HEAD_DIM, SSD_STATE, n), lambda h: (h, 0, 0, 0))
    return pl.pallas_call(
        _mix_step_body,
        grid=(SSD_HEADS,),
        in_specs=[pl.BlockSpec((n, D_MODEL), fixed2), pl.BlockSpec((1, D_MODEL), fixed2),
                  pl.BlockSpec((D_MODEL, IN_DIM), fixed2), pl.BlockSpec((D_MODEL, LANES), fixed2),
                  pl.BlockSpec((POOL_BUF, n, POOL_WIDTH), fixed3),
                  pl.BlockSpec((SSD_CONV - 1, n, CONV_DIM), fixed3),
                  st_spec,
                  pl.BlockSpec((SSD_CONV, CONV_DIM), fixed2), pl.BlockSpec((1, CONV_DIM), fixed2),
                  pl.BlockSpec((1, LANES), fixed2), pl.BlockSpec((1, LANES), fixed2),
                  pl.BlockSpec((1, SSD_WIDTH), fixed2), pl.BlockSpec((1, SSD_WIDTH), fixed2),
                  pl.BlockSpec((len(POOL_WINDOWS), POOL_GROUP_DIM, POOL_GROUP_DIM), fixed3),
                  pl.BlockSpec((1, POOL_WIDTH), fixed2), pl.BlockSpec((1, POOL_WIDTH), fixed2),
                  pl.BlockSpec((D_MODEL, D_MODEL), fixed2), pl.BlockSpec((1, D_MODEL), fixed2),
                  pl.BlockSpec((D_MODEL, 2 * LANES), fixed2),
                  pl.BlockSpec((LANES, n), fixed2)],
        out_specs=[pl.BlockSpec((n, D_MODEL), fixed2), pl.BlockSpec((n, D_MODEL), fixed2),
                   pl.BlockSpec((1, 8, n), fixed3), pl.BlockSpec((n, 2 * LANES), fixed2),
                   pl.BlockSpec((1, N_EXPERTS, LANES), fixed3),
                   pl.BlockSpec((POOL_BUF, n, POOL_WIDTH), fixed3),
                   pl.BlockSpec((SSD_CONV - 1, n, CONV_DIM), fixed3), st_spec],
        out_shape=[jax.ShapeDtypeStruct((n, D_MODEL), F32), jax.ShapeDtypeStruct((n, D_MODEL), BF16),
                   jax.ShapeDtypeStruct((1, 8, n), F32), jax.ShapeDtypeStruct((n, 2 * LANES), F32),
                   jax.ShapeDtypeStruct((1, N_EXPERTS, LANES), F32),
                   jax.ShapeDtypeStruct(pool_t.shape, F32),
                   jax.ShapeDtypeStruct(conv_t.shape, F32), jax.ShapeDtypeStruct(ssm_t.shape, F32)],
        scratch_shapes=[pltpu.VMEM((SSD_WIDTH, n), F32), pltpu.VMEM((LANES, n), F32),
                        pltpu.VMEM((SSD_BC, n), F32), pltpu.VMEM((SSD_BC, n), F32),
                        pltpu.VMEM((n, SSD_WIDTH), F32), pltpu.VMEM((SSD_WIDTH, n), F32),
                        pltpu.VMEM((n, SSD_WIDTH), F32), pltpu.VMEM((n, D_MODEL), BF16)],
        compiler_params=_cparams("arbitrary"),
        name="front_step",
    )(x, g_mix, w_in16, w_dt16, pool_t, conv_t, ssm_t, cw, cb, dtb, alog, dsk, nw, plw16, plb, psc,
      w_out16, g_ffn, r_w16, r_bias)


PROMPT_TILE = MIX_ROWS


def _sort_tables(counts, tile_tokens, data_rows):
    cnt = jnp.concatenate(counts, axis=0)
    pc = (cnt + RUN_PAD - 1) // RUN_PAD * RUN_PAD
    off_local = jnp.cumsum(pc, axis=1) - pc
    tile_rows = jnp.sum(pc, axis=1)
    region = (jnp.sum(pc, axis=0) + MOE_TILE - 1) // MOE_TILE * MOE_TILE
    base = jnp.cumsum(region) - region
    dst = base[None, :] + jnp.cumsum(pc, axis=0) - pc
    per_pass = []
    lo = 0
    for c, tm in zip(counts, tile_tokens):
        hi = lo + c.shape[0]
        n_chunk = _sorted_rows_per_tile(tm) // RUN_PAD
        s = jnp.arange(n_chunk, dtype=jnp.int32) * RUN_PAD
        begins = off_local[lo:hi, None, :]
        ends = begins + pc[lo:hi, None, :]
        inside = (begins <= s[None, :, None]) & (s[None, :, None] < ends)
        shift = jnp.sum(jnp.where(inside, dst[lo:hi, None, :] - begins, 0), axis=2)
        valid = s[None, :] < tile_rows[lo:hi, None]
        parity = (jnp.arange(hi - lo, dtype=jnp.int32) % 2)[:, None]
        spare = data_rows + parity * _sorted_rows_per_tile(max(tile_tokens)) + s[None, :]
        scatter_dst = (jnp.where(valid, s[None, :] + shift, spare) // RUN_PAD).astype(jnp.int32)
        gather_src = (jnp.where(valid, s[None, :] + shift, 0) // RUN_PAD).astype(jnp.int32)
        off_v = jnp.broadcast_to(off_local[lo:hi, :, None].astype(F32), (hi - lo, N_EXPERTS, LANES))
        per_pass.append((scatter_dst, gather_src, off_v))
        lo = hi
    tiles_cum = jnp.cumsum(region // MOE_TILE)
    n_active = tiles_cum[-1]
    return per_pass, tiles_cum, n_active


def _max_sorted_rows(tile_counts, tile_tokens):
    rows = sum(n * (2 * tm + N_EXPERTS * (RUN_PAD - 1)) for n, tm in zip(tile_counts, tile_tokens))
    data_rows = (-(-rows // MOE_TILE) + N_EXPERTS) * MOE_TILE
    return data_rows, data_rows + 2 * _sorted_rows_per_tile(max(tile_tokens))


def _row(v):
    return v.reshape(1, -1).astype(F32)


def _per_head_rows(v):
    return jnp.broadcast_to(v.astype(F32)[:, None], (SSD_HEADS, SSD_Q))


def _pad_lanes(v):
    return jnp.pad(v.reshape(1, -1).astype(F32), ((0, 0), (0, LANES - v.size)))


def kernel(x_prompt, x_sample, p_prompt, p_sample, state_pool, state_conv, state_ssm, norm_mix, w_in, pool_lin_w, pool_lin_b, pool_scale, conv_w, conv_b, dt_bias, a_log, d_skip, ssd_norm, w_out, norm_ffn, router_grp_w, router_grp_b, router_exp_w, router_exp_b, exp_w_gate, exp_w_up, exp_w_down, norm_ple, ple_gate_w, ple_gate_b, ple_proj_w, norm_final):
    nb, seq, _ = x_prompt.shape
    ns = x_sample.shape[0]
    assert ns == LANES and x_sample.shape[1] == 1 and seq % MIX_ROWS == 0 and seq >= POOL_BUF

    w_in16 = w_in[0].astype(BF16)
    w_dt16 = jnp.pad(w_in[0][:, DT_OFF:], ((0, 0), (0, LANES - SSD_HEADS))).astype(BF16)
    w_out16 = w_out[0].astype(BF16)
    g_mix, g_ffn, g_ple, g_fin = _row(norm_mix[0]), _row(norm_ffn[0]), _row(norm_ple[0]), _row(norm_final)
    cw, cb = conv_w[0].astype(F32), _row(conv_b[0])
    dtb, alog = _pad_lanes(dt_bias[0]), _pad_lanes(a_log[0])
    dsk = _row(jnp.repeat(d_skip[0], SSD_HEAD_DIM))
    nw = _row(ssd_norm[0])
    plw16 = pool_lin_w[0].astype(BF16)
    plb, psc = _row(pool_lin_b[0]), _row(pool_scale[0])
    zeros4 = jnp.zeros((D_MODEL, 8 - N_EXPERT_GROUPS), F32)
    r_w = jnp.concatenate([router_grp_w[0], zeros4, router_exp_w[0],
                           jnp.zeros((D_MODEL, LANES - 8 - N_EXPERTS), F32)], axis=1)
    r_hi = r_w.astype(BF16)
    r_w16 = jnp.concatenate([r_hi, (r_w - r_hi.astype(F32)).astype(BF16)], axis=1)
    r_b = jnp.concatenate([router_grp_b[0], jnp.zeros((8 - N_EXPERT_GROUPS,), F32), router_exp_b[0],
                           jnp.zeros((LANES - 8 - N_EXPERTS,), F32)])
    wg = exp_w_gate[0].reshape(N_EXPERTS, D_MODEL, EXPERT_FF)
    wu = exp_w_up[0].reshape(N_EXPERTS, D_MODEL, EXPERT_FF)
    wd = exp_w_down[0].reshape(N_EXPERTS, EXPERT_FF, D_MODEL)
    pg16 = ple_gate_w[0].astype(BF16)
    pgb = _row(ple_gate_b[0])
    pp16 = ple_proj_w[0].astype(BF16)

    xp = x_prompt.reshape(nb * seq, D_MODEL)
    h1_p, u_p, rt_p, gate_p, cnt_p, st, pool_tail, conv_tail = _front_prompt(
        xp, g_mix, w_in16, w_dt16, cw, cb, _per_head_rows(dt_bias[0]), _per_head_rows(a_log[0]), dsk, nw, plw16, plb, psc,
        w_out16, g_ffn, r_w16, jnp.broadcast_to(r_b[:, None], (LANES, MIX_ROWS)), nb, seq)
    pool_p = pool_tail[:, TAIL_ROWS - POOL_BUF:]
    conv_p = conv_tail[:, CONV_HDR - (SSD_CONV - 1):]
    ssm_p = st

    xs_ = x_sample.reshape(ns, D_MODEL)
    h1_s, u_s, rt_s, gate_s, cnt_s, pool_t, conv_t, ssm_t = _front_step(
        xs_, g_mix, w_in16, w_dt16, jnp.transpose(state_pool[0], (1, 0, 2)), jnp.transpose(state_conv[0], (1, 0, 2)),
        jnp.transpose(state_ssm[0], (1, 2, 3, 0)), cw, cb, dtb, alog, dsk, nw, plw16, plb, psc,
        w_out16, g_ffn, r_w16, jnp.broadcast_to(r_b[:, None], (LANES, ns)))
    pool_s = jnp.transpose(pool_t, (1, 0, 2))
    conv_s = jnp.transpose(conv_t, (1, 0, 2))
    ssm_s = jnp.transpose(ssm_t, (3, 0, 1, 2))

    counts = [cnt_p[:, :, 0].astype(jnp.int32), cnt_s[:, :, 0].astype(jnp.int32)]
    tiles = (PROMPT_TILE, ns)
    data_rows, total_rows = _max_sorted_rows([c.shape[0] for c in counts], tiles)
    (tab_p, tab_s), tiles_cum, n_active = _sort_tables(counts, tiles, data_rows)
    xs_sorted, dest_p = _sort_tokens(tab_p[0], u_p, rt_p, tab_p[2], None, PROMPT_TILE, total_rows)
    xs_sorted, dest_s = _sort_tokens(tab_s[0], u_s, rt_s, tab_s[2], xs_sorted, ns, total_rows)
    tile_start = jnp.concatenate([jnp.zeros((1,), jnp.int32), tiles_cum.astype(jnp.int32)])
    ys_sorted = _moe_sorted(tile_start, n_active.reshape(1).astype(jnp.int32),
                            xs_sorted.reshape(total_rows, D_MODEL), wg, wu, wd).reshape(xs_sorted.shape)

    y_prompt = _ple(tab_p[1], h1_p, dest_p, gate_p, p_prompt[0].reshape(nb * seq, PLE_DIM),
                    g_ple, pg16, pgb, pp16, g_fin, ys_sorted, PROMPT_TILE)
    y_sample = _ple(tab_s[1], h1_s, dest_s, gate_s, p_sample[0].reshape(ns, PLE_DIM),
                    g_ple, pg16, pgb, pp16, g_fin, ys_sorted, ns)

    return (y_prompt.reshape(nb, seq, D_MODEL), y_sample.reshape(ns, 1, D_MODEL),
            pool_p[None], conv_p[None], ssm_p[None], pool_s[None], conv_s[None], ssm_s[None])
```
